```python
import math
import jax, jax.numpy as jnp
from jax import lax
import numpy as np

D_MODEL = 1024
BATCH = 8
SEQ = 2048
DEPTH = 2

ATTN_HEADS = 8
ATTN_HEAD_DIM = 64
ATTN_WIDTH = ATTN_HEADS * ATTN_HEAD_DIM
POOL_WINDOWS = (2, 4, 8, 16)
POOL_GROUPS = len(POOL_WINDOWS)
POOL_GROUP_DIM = 64
POOL_WIDTH = POOL_GROUPS * POOL_GROUP_DIM
SSM_GROUPS = 16
SSM_GROUP_DIM = 16
SSM_STATE = 64
SSM_WIDTH = SSM_GROUPS * SSM_GROUP_DIM
MIX_WIDTH = ATTN_WIDTH + POOL_WIDTH + SSM_WIDTH
Q_BLOCK = 128

Q_OFF = 0
K_OFF = Q_OFF + ATTN_WIDTH
V_OFF = K_OFF + ATTN_WIDTH
F_OFF = V_OFF + ATTN_WIDTH
P_OFF = F_OFF + ATTN_HEADS
S_OFF = P_OFF + POOL_WIDTH
N_IN = S_OFF + SSM_WIDTH

FFN_DIM = 2816
N_EXPERTS = 8
TOP_K = 2
EXPERT_DIM = 3584
N_DENSE = (DEPTH + 1) // 2
N_MOE = DEPTH // 2

RMS_EPS = 1e-6
DT_MIN = 1e-3
DT_MAX = 1e-1
NEG_INF = -1e30

kernel_name = "hybrid_fox_pool_s5_moe_trunk"


def rmsnorm(x, g):
    xf = x.astype(jnp.float32)
    y = xf * lax.rsqrt(jnp.mean(xf * xf, axis=-1, keepdims=True) + RMS_EPS)
    return (y * g.astype(jnp.float32)).astype(x.dtype)


def forgetting_attention(q, k, v, log_f):
    b, L, H, dh = q.shape
    nb = L // Q_BLOCK
    qf = q.astype(jnp.float32) * (dh ** -0.5)
    kh = k.astype(jnp.float32).transpose(0, 2, 1, 3)
    vh = v.astype(jnp.float32).transpose(0, 2, 1, 3)
    c = jnp.cumsum(log_f.astype(jnp.float32), axis=1)
    ck = c.transpose(0, 2, 1)
    qb = qf.reshape(b, nb, Q_BLOCK, H, dh).transpose(1, 0, 3, 2, 4)
    cb = c.reshape(b, nb, Q_BLOCK, H).transpose(1, 0, 3, 2)
    kpos = jnp.arange(L)

    def block(args):
        q_blk, c_blk, start = args
        s = jnp.einsum('bhqd,bhkd->bhqk', q_blk, kh) + c_blk[..., None] - ck[:, :, None, :]
        qpos = start + jnp.arange(Q_BLOCK)
        s = jnp.where(kpos[None, :] <= qpos[:, None], s, NEG_INF)
        p = jax.nn.softmax(s, axis=-1)
        return jnp.einsum('bhqk,bhkd->bhqd', p, vh)

    o = lax.map(block, (qb, cb, jnp.arange(nb, dtype=jnp.int32) * Q_BLOCK))
    return o.transpose(1, 0, 3, 2, 4).reshape(b, L, H * dh)


def pooling_mixer(u, w, scale):
    b, L, _ = u.shape
    uf = u.astype(jnp.float32).reshape(b, L, POOL_GROUPS, POOL_GROUP_DIM)
    cs = jnp.cumsum(uf, axis=1)
    t = jnp.arange(L, dtype=jnp.float32)
    outs = []
    for g, win in enumerate(POOL_WINDOWS):
        csg = cs[:, :, g]
        prev = jnp.pad(csg, ((0, 0), (win, 0), (0, 0)))[:, :L]
        count = jnp.minimum(t + 1.0, float(win))[None, :, None]
        outs.append((csg - prev) / count - uf[:, :, g])
    d = jnp.stack(outs, axis=2)
    y = jnp.einsum('blgc,gcd->blgd', d, w.astype(jnp.float32)).reshape(b, L, POOL_WIDTH)
    return y * scale.astype(jnp.float32)


def ssm_mixer(u, a_re, a_im, b_re, b_im, c_re, c_im, d, log_dt, w_glu, b_glu):
    bsz, L, _ = u.shape
    uf = u.astype(jnp.float32).reshape(bsz, L, SSM_GROUPS, SSM_GROUP_DIM)
    lam = lax.complex(a_re.astype(jnp.float32), a_im.astype(jnp.float32))
    dt = jnp.exp(log_dt.astype(jnp.float32))[:, None]
    lam_bar = jnp.exp(lam * dt)
    bmat = lax.complex(b_re.astype(jnp.float32), b_im.astype(jnp.float32))
    cmat = lax.complex(c_re.astype(jnp.float32), c_im.astype(jnp.float32))
    b_bar = ((lam_bar - 1.0) / lam)[:, :, None] * bmat
    bu = jnp.einsum('blgh,gph->blgp', uf.astype(jnp.complex64), b_bar)
    a = jnp.broadcast_to(lam_bar, bu.shape)

    def combine(left, right):
        a1, b1 = left
        a2, b2 = right
        return a1 * a2, a2 * b1 + b2

    _, xs = lax.associative_scan(combine, (a, bu), axis=1)
    y = jnp.einsum('blgp,ghp->blgh', xs, cmat).real + d.astype(jnp.float32) * uf
    y = jax.nn.gelu(y.reshape(bsz, L, SSM_WIDTH))
    return y * jax.nn.sigmoid(y @ w_glu.astype(jnp.float32) + b_glu.astype(jnp.float32))


def swiglu(h, wg, wu, wd):
    return (jax.nn.silu(h @ wg) * (h @ wu)) @ wd


def moe_swiglu(h, router, wg, wu, wd):
    b, L, D = h.shape
    t = h.reshape(-1, D)
    logits = t.astype(jnp.float32) @ router.astype(jnp.float32)
    top_v, top_i = lax.top_k(logits, TOP_K)
    gates = jax.nn.softmax(top_v, axis=-1)
    comb = jnp.einsum('nk,nke->ne', gates, jax.nn.one_hot(top_i, N_EXPERTS, dtype=jnp.float32))
    out = jnp.zeros(t.shape, jnp.float32)
    for e in range(N_EXPERTS):
        out = out + comb[:, e:e + 1] * swiglu(t, wg[e], wu[e], wd[e]).astype(jnp.float32)
    return out.astype(h.dtype).reshape(b, L, D)


def setup_inputs(seed: int = 0) -> dict:
    key = jax.random.key(seed)
    ks = list(jax.random.split(key, 40))
    nxt = iter(ks)

    def nrm(shape, scale):
        return jax.random.normal(next(nxt), shape, jnp.float32) * scale

    def gain(shape):
        return 1.0 + nrm(shape, 0.1)

    n_idx = jnp.arange(SSM_STATE, dtype=jnp.float32)
    inp = {}
    inp['x'] = nrm((BATCH, SEQ, D_MODEL), 1.0)
    inp['norm_mix_pre'] = gain((DEPTH, D_MODEL))
    inp['norm_mix_post'] = gain((DEPTH, D_MODEL))
    inp['norm_ffn_pre'] = gain((DEPTH, D_MODEL))
    inp['norm_ffn_post'] = gain((DEPTH, D_MODEL))
    inp['w_in'] = nrm((DEPTH, D_MODEL, N_IN), D_MODEL ** -0.5)
    inp['b_forget'] = nrm((DEPTH, ATTN_HEADS), 0.1)
    inp['pool_w'] = nrm((DEPTH, POOL_GROUPS, POOL_GROUP_DIM, POOL_GROUP_DIM), POOL_GROUP_DIM ** -0.5)
    inp['pool_scale'] = gain((DEPTH, POOL_WIDTH))
    inp['ssm_a_re'] = -0.5 + nrm((DEPTH, SSM_GROUPS, SSM_STATE), 0.01)
    inp['ssm_a_im'] = math.pi * n_idx[None, None, :] + nrm((DEPTH, SSM_GROUPS, SSM_STATE), 0.01)
    inp['ssm_b_re'] = nrm((DEPTH, SSM_GROUPS, SSM_STATE, SSM_GROUP_DIM), (2 * SSM_GROUP_DIM) ** -0.5)
    inp['ssm_b_im'] = nrm((DEPTH, SSM_GROUPS, SSM_STATE, SSM_GROUP_DIM), (2 * SSM_GROUP_DIM) ** -0.5)
    inp['ssm_c_re'] = nrm((DEPTH, SSM_GROUPS, SSM_GROUP_DIM, SSM_STATE), (2 * SSM_STATE) ** -0.5)
    inp['ssm_c_im'] = nrm((DEPTH, SSM_GROUPS, SSM_GROUP_DIM, SSM_STATE), (2 * SSM_STATE) ** -0.5)
    inp['ssm_d'] = nrm((DEPTH, SSM_GROUPS, SSM_GROUP_DIM), 1.0)
    inp['ssm_log_dt'] = jax.random.uniform(next(nxt), (DEPTH, SSM_GROUPS), jnp.float32,
                                           math.log(DT_MIN), math.log(DT_MAX))
    inp['ssm_w_glu'] = nrm((DEPTH, SSM_WIDTH, SSM_WIDTH), SSM_WIDTH ** -0.5)
    inp['ssm_b_glu'] = nrm((DEPTH, SSM_WIDTH), 0.01)
    inp['branch_norm_attn'] = gain((DEPTH, ATTN_WIDTH))
    inp['branch_norm_pool'] = gain((DEPTH, POOL_WIDTH))
    inp['branch_norm_ssm'] = gain((DEPTH, SSM_WIDTH))
    inp['w_out'] = nrm((DEPTH, MIX_WIDTH, D_MODEL), MIX_WIDTH ** -0.5)
    inp['ffn_w_gate'] = nrm((N_DENSE, D_MODEL, FFN_DIM), D_MODEL ** -0.5)
    inp['ffn_w_up'] = nrm((N_DENSE, D_MODEL, FFN_DIM), D_MODEL ** -0.5)
    inp['ffn_w_down'] = nrm((N_DENSE, FFN_DIM, D_MODEL), FFN_DIM ** -0.5)
    inp['moe_router'] = nrm((N_MOE, D_MODEL, N_EXPERTS), D_MODEL ** -0.5)
    inp['moe_w_gate'] = nrm((N_MOE, N_EXPERTS, D_MODEL, EXPERT_DIM), D_MODEL ** -0.5)
    inp['moe_w_up'] = nrm((N_MOE, N_EXPERTS, D_MODEL, EXPERT_DIM), D_MODEL ** -0.5)
    inp['moe_w_down'] = nrm((N_MOE, N_EXPERTS, EXPERT_DIM, D_MODEL), EXPERT_DIM ** -0.5)
    return inp


def reference(x, norm_mix_pre, norm_mix_post, norm_ffn_pre, norm_ffn_post, w_in, b_forget,
              pool_w, pool_scale, ssm_a_re, ssm_a_im, ssm_b_re, ssm_b_im, ssm_c_re, ssm_c_im,
              ssm_d, ssm_log_dt, ssm_w_glu, ssm_b_glu, branch_norm_attn, branch_norm_pool,
              branch_norm_ssm, w_out, ffn_w_gate, ffn_w_up, ffn_w_down, moe_router,
              moe_w_gate, moe_w_up, moe_w_down):
    b, L, _ = x.shape
    for i in range(DEPTH):
        h = rmsnorm(x, norm_mix_pre[i])
        proj = h @ w_in[i]
        q = proj[..., Q_OFF:Q_OFF + ATTN_WIDTH].reshape(b, L, ATTN_HEADS, ATTN_HEAD_DIM)
        k = proj[..., K_OFF:K_OFF + ATTN_WIDTH].reshape(b, L, ATTN_HEADS, ATTN_HEAD_DIM)
        v = proj[..., V_OFF:V_OFF + ATTN_WIDTH].reshape(b, L, ATTN_HEADS, ATTN_HEAD_DIM)
        log_f = jax.nn.log_sigmoid((proj[..., F_OFF:F_OFF + ATTN_HEADS] + b_forget[i]).astype(jnp.float32))
        attn = forgetting_attention(q, k, v, log_f).astype(x.dtype)
        pool = pooling_mixer(proj[..., P_OFF:P_OFF + POOL_WIDTH], pool_w[i], pool_scale[i]).astype(x.dtype)
        ssm = ssm_mixer(proj[..., S_OFF:S_OFF + SSM_WIDTH], ssm_a_re[i], ssm_a_im[i], ssm_b_re[i],
                        ssm_b_im[i], ssm_c_re[i], ssm_c_im[i], ssm_d[i], ssm_log_dt[i],
                        ssm_w_glu[i], ssm_b_glu[i]).astype(x.dtype)
        mixed = jnp.concatenate([rmsnorm(attn, branch_norm_attn[i]),
                                 rmsnorm(pool, branch_norm_pool[i]),
                                 rmsnorm(ssm, branch_norm_ssm[i])], axis=-1)
        x = x + rmsnorm(mixed @ w_out[i], norm_mix_post[i])
        h = rmsnorm(x, norm_ffn_pre[i])
        if i % 2 == 0:
            j = i // 2
            f = swiglu(h, ffn_w_gate[j], ffn_w_up[j], ffn_w_down[j])
        else:
            j = i // 2
            f = moe_swiglu(h, moe_router[j], moe_w_gate[j], moe_w_up[j], moe_w_down[j])
        x = x + rmsnorm(f, norm_ffn_post[i])
    return x
```

```python
import functools

import jax
import jax.numpy as jnp
from jax import lax
from jax.experimental import pallas as pl
from jax.experimental.pallas import tpu as pltpu

F32 = jnp.float32
BF16 = jnp.bfloat16

RMS_EPS = 1e-6
NEG_INF = -1e30

ATTN_HEADS = 8
ATTN_HEAD_DIM = 64
ATTN_WIDTH = ATTN_HEADS * ATTN_HEAD_DIM
POOL_WINDOWS = (2, 4, 8, 16)
POOL_GROUP_DIM = 64
POOL_WIDTH = len(POOL_WINDOWS) * POOL_GROUP_DIM
SSM_GROUPS = 16
SSM_GROUP_DIM = 16
SSM_STATE = 64
SSM_WIDTH = SSM_GROUPS * SSM_GROUP_DIM
SSM_CHUNK = 16
N_EXPERTS = 8

LANES = 128
VMEM_LIMIT = 48 * 1024 * 1024


def _params(sem):
    return pltpu.CompilerParams(dimension_semantics=sem, vmem_limit_bytes=VMEM_LIMIT)


def _rms(x, g):
    return x * lax.rsqrt(jnp.mean(x * x, axis=-1, keepdims=True) + RMS_EPS) * g


def _split3(x):
    hi = x.astype(BF16)
    r = x - hi.astype(F32)
    mid = r.astype(BF16)
    lo = (r - mid.astype(F32)).astype(BF16)
    return hi, mid, lo


def _inproj_kernel(x_ref, g_ref, wm_ref, wf_ref, bf_ref,
                   q_ref, k_ref, v_ref, up_ref, us_ref, c_ref, carry_ref, *, tiles_per_seq):
    @pl.when(pl.program_id(0) % tiles_per_seq == 0)
    def _():
        carry_ref[...] = jnp.zeros_like(carry_ref)

    tm = x_ref.shape[0]
    h = _rms(x_ref[...], g_ref[...]).astype(BF16)
    proj = jnp.dot(h, wm_ref[...], preferred_element_type=F32)
    a = ATTN_WIDTH
    q_ref[...] = (proj[:, 0:a] * (ATTN_HEAD_DIM ** -0.5)).astype(BF16)
    k_ref[...] = proj[:, a:2 * a].astype(BF16)
    v_ref[...] = proj[:, 2 * a:3 * a].astype(BF16)
    up_ref[...] = proj[:, 3 * a:3 * a + POOL_WIDTH].astype(BF16)
    us_ref[...] = proj[:, 3 * a + POOL_WIDTH:].astype(BF16)

    z = jnp.dot(h, wf_ref[...], preferred_element_type=F32) + bf_ref[...]
    logf = jnp.minimum(z, 0.0) - jnp.log(1.0 + jnp.exp(-jnp.abs(z)))
    row = lax.broadcasted_iota(jnp.int32, (tm, tm), 0)
    col = lax.broadcasted_iota(jnp.int32, (tm, tm), 1)
    tri = (row >= col).astype(BF16)
    hi, mid, lo = _split3(logf)
    c = (jnp.dot(tri, hi, preferred_element_type=F32)
         + jnp.dot(tri, mid, preferred_element_type=F32)
         + jnp.dot(tri, lo, preferred_element_type=F32)) + carry_ref[...]
    c_ref[...] = c[:, :ATTN_HEADS]
    carry_ref[...] = c[tm - 1:tm, :]


def _inproj(x2, g, w_main, w_f, b_f, *, seq, tm=512):
    n, d = x2.shape
    nm = w_main.shape[1]
    a = ATTN_WIDTH
    out_shape = (
        jax.ShapeDtypeStruct((n, a), BF16), jax.ShapeDtypeStruct((n, a), BF16),
        jax.ShapeDtypeStruct((n, a), BF16), jax.ShapeDtypeStruct((n, POOL_WIDTH), BF16),
        jax.ShapeDtypeStruct((n, SSM_WIDTH), BF16), jax.ShapeDtypeStruct((n, ATTN_HEADS), F32))
    row = lambda w: pl.BlockSpec((tm, w), lambda i: (i, 0))
    full = lambda r, c: pl.BlockSpec((r, c), lambda i: (0, 0))
    return pl.pallas_call(
        functools.partial(_inproj_kernel, tiles_per_seq=seq // tm),
        grid=(n // tm,),
        in_specs=[row(d), full(1, d), full(d, nm), full(d, LANES), full(1, LANES)],
        out_specs=(row(a), row(a), row(a), row(POOL_WIDTH), row(SSM_WIDTH), row(ATTN_HEADS)),
        out_shape=out_shape,
        scratch_shapes=[pltpu.VMEM((1, LANES), F32)],
        compiler_params=_params(("arbitrary",)),
        name="inproj",
    )(x2, g, w_main, w_f, b_f)


def _attn_kernel(q_ref, k_ref, v_ref, c_ref, ct_ref, o_ref, *, tq, tk):
    hp = pl.program_id(1)
    qi = pl.program_id(2)
    nkb = k_ref.shape[1] // tk
    q2 = q_ref[0]
    c_tile = c_ref[0]
    lane = lax.broadcasted_iota(jnp.int32, (1, LANES), 1)
    head_lane = lax.broadcasted_iota(jnp.int32, (1, ATTN_HEADS), 1)
    out = jnp.zeros((tq, LANES), F32)
    for hh in range(LANES // ATTN_HEAD_DIM):
        h = hp * (LANES // ATTN_HEAD_DIM) + hh
        in_head = (lane // ATTN_HEAD_DIM) == hh
        qm = jnp.where(in_head, q2, jnp.zeros_like(q2))
        cq = jnp.sum(jnp.where(head_lane == h, c_tile, 0.0), axis=1, keepdims=True)

        def step(kb, carry, masked):
            m, l, acc = carry
            kblk = k_ref[0, pl.ds(kb * tk, tk), :]
            vblk = v_ref[0, pl.ds(kb * tk, tk), :]
            s = lax.dot_general(qm, kblk, (((1,), (1,)), ((), ())), preferred_element_type=F32)
            ck = ct_ref[0, pl.ds(h * nkb + kb, 1), :]
            s = s + (cq - ck)
            if masked:
                r = lax.broadcasted_iota(jnp.int32, (tq, tk), 0)
                cidx = lax.broadcasted_iota(jnp.int32, (tq, tk), 1)
                s = jnp.where(cidx <= r, s, NEG_INF)
            m_new = jnp.maximum(m, jnp.max(s, axis=1, keepdims=True))
            p = jnp.exp(s - m_new)
            alpha = jnp.exp(m - m_new)
            l = alpha * l + jnp.sum(p, axis=1, keepdims=True)
            acc = alpha * acc + jnp.dot(p.astype(BF16), vblk, preferred_element_type=F32)
            return m_new, l, acc

        init = (jnp.full((tq, 1), NEG_INF, F32), jnp.zeros((tq, 1), F32), jnp.zeros((tq, LANES), F32))
        carry = lax.fori_loop(0, qi, lambda kb, cr: step(kb, cr, False), init)
        m, l, acc = step(qi, carry, True)
        out = jnp.where(in_head, acc / l, out)
    o_ref[0] = out.astype(o_ref.dtype)


def _attention(q, k, v, c, ct, *, tq=512):
    b, L, a = q.shape
    tk = tq
    hp = a // LANES
    blk = lambda: pl.BlockSpec((1, tq, LANES), lambda bi, hi, qi: (bi, qi, hi))
    seq = lambda: pl.BlockSpec((1, L, LANES), lambda bi, hi, qi: (bi, 0, hi))
    return pl.pallas_call(
        functools.partial(_attn_kernel, tq=tq, tk=tk),
        grid=(b, hp, L // tq),
        in_specs=[blk(), seq(), seq(),
                  pl.BlockSpec((1, tq, ATTN_HEADS), lambda bi, hi, qi: (bi, qi, 0)),
                  pl.BlockSpec((1, ct.shape[1], tk), lambda bi, hi, qi: (bi, 0, 0))],
        out_specs=blk(),
        out_shape=jax.ShapeDtypeStruct((b, L, a), BF16),
        compiler_params=_params(("arbitrary", "arbitrary", "arbitrary")),
        name="fox_attention",
    )(q, k, v, c, ct)


def _pool_kernel(u_ref, w_ref, s_ref, o_ref):
    x = u_ref[0].astype(F32)
    L, w = x.shape
    row = lax.broadcasted_iota(jnp.int32, (L, w), 0)
    group = lax.broadcasted_iota(jnp.int32, (L, w), 1) // POOL_GROUP_DIM

    def shifted(y, s):
        return jnp.where(row >= s, pltpu.roll(y, s, 0), 0.0)

    acc = x
    sel = jnp.zeros_like(x)
    win_lane = jnp.zeros((L, w), F32)
    span = 1
    for gi, win in enumerate(POOL_WINDOWS):
        while span < win:
            acc = acc + shifted(acc, span)
            span *= 2
        sel = jnp.where(group == gi, acc, sel)
        win_lane = jnp.where(group == gi, float(win), win_lane)
    count = jnp.minimum(row.astype(F32) + 1.0, win_lane)
    d = sel / count - x
    y = jnp.dot(d.astype(BF16), w_ref[...], preferred_element_type=F32) * s_ref[...]
    o_ref[0] = y.astype(o_ref.dtype)


def _pool(u, w_bd, scale):
    b, L, w = u.shape
    return pl.pallas_call(
        _pool_kernel,
        grid=(b,),
        in_specs=[pl.BlockSpec((1, L, w), lambda i: (i, 0, 0)),
                  pl.BlockSpec((w, w), lambda i: (0, 0)),
                  pl.BlockSpec((1, w), lambda i: (0, 0))],
        out_specs=pl.BlockSpec((1, L, w), lambda i: (i, 0, 0)),
        out_shape=jax.ShapeDtypeStruct((b, L, w), BF16),
        compiler_params=_params(("arbitrary",)),
        name="pool_mixer",
    )(u, w_bd, scale)


def _ssm_kernel(u_ref, m_ref, wz_ref, wy_ref, coef_ref, y_ref, z_ref, s_ref, *, batch):
    u = u_ref[0]
    rows = u.shape[0]
    half = z_ref.shape[1] // 2
    z_ref[...] = jnp.dot(u, wz_ref[0], preferred_element_type=F32)
    a1 = jnp.broadcast_to(coef_ref[0, 0:1, :], (batch, 2 * half))
    a2 = jnp.broadcast_to(coef_ref[0, 1:2, :], (batch, 2 * half))

    def step(c, st):
        r0 = pl.multiple_of(c * batch, batch)
        s_ref[pl.ds(r0, batch), :] = st[:, :half]
        sw = jnp.concatenate([st[:, half:], st[:, :half]], axis=1)
        return a1 * st + a2 * sw + z_ref[pl.ds(r0, batch), :]

    lax.fori_loop(0, rows // batch, step, jnp.zeros((batch, 2 * half), F32), unroll=8)
    y = jnp.dot(u, m_ref[0], preferred_element_type=F32)
    y = y + jnp.dot(s_ref[...].astype(BF16), wy_ref[0], preferred_element_type=F32)
    y_ref[0] = y


def _ssm_core(u_chunked, m, wz, wy, coef, *, batch):
    g, rows, w = u_chunked.shape
    p2 = wy.shape[1]
    return pl.pallas_call(
        functools.partial(_ssm_kernel, batch=batch),
        grid=(g,),
        in_specs=[pl.BlockSpec((1, rows, w), lambda i: (i, 0, 0)),
                  pl.BlockSpec((1, w, w), lambda i: (i, 0, 0)),
                  pl.BlockSpec((1, w, 2 * p2), lambda i: (i, 0, 0)),
                  pl.BlockSpec((1, p2, w), lambda i: (i, 0, 0)),
                  pl.BlockSpec((1, 2, 2 * p2), lambda i: (i, 0, 0))],
        out_specs=pl.BlockSpec((1, rows, w), lambda i: (i, 0, 0)),
        out_shape=jax.ShapeDtypeStruct((g, rows, w), F32),
        scratch_shapes=[pltpu.VMEM((rows, 2 * p2), F32), pltpu.VMEM((rows, p2), F32)],
        compiler_params=_params(("arbitrary",)),
        name="s5_core",
    )(u_chunked, m, wz, wy, coef)


def _ssm_matrices(a_re, a_im, b_re, b_im, c_re, c_im, d, log_dt):
    t = SSM_CHUNK
    g, p = a_re.shape
    hdim = d.shape[1]
    lam = lax.complex(a_re, a_im)
    lam_dt = lam * jnp.exp(log_dt)[:, None]
    lam_bar = jnp.exp(lam_dt)
    b_bar = ((lam_bar - 1.0) / lam)[:, :, None] * lax.complex(b_re, b_im)
    cmat = lax.complex(c_re, c_im)
    steps = jnp.arange(t + 1, dtype=F32)
    pw = jnp.exp(lam_dt[:, None, :] * steps[None, :, None])
    kern = jnp.einsum('gop,gdp,gpi->gdoi', cmat, pw[:, :t], b_bar).real
    kern = kern.at[:, 0].add(jax.vmap(jnp.diag)(d))
    j = jnp.arange(t)[:, None]
    i = jnp.arange(t)[None, :]
    m = jnp.where((i >= j)[None, :, :, None, None], kern[:, jnp.clip(i - j, 0, t - 1)], 0.0)
    m = m.transpose(0, 1, 4, 2, 3).reshape(g, t * hdim, t * hdim)
    wz = jnp.einsum('gjp,gpi->gjip', pw[:, t - 1 - jnp.arange(t)], b_bar).reshape(g, t * hdim, p)
    wz = jnp.concatenate([wz.real, wz.imag, wz.imag, wz.real], axis=-1)
    cw = jnp.einsum('gop,gip->giop', cmat, pw[:, 1:t + 1])
    wy = jnp.concatenate([cw.real, -cw.imag], axis=-1).reshape(g, t * hdim, 2 * p).transpose(0, 2, 1)
    ar, ai = pw[:, t].real, pw[:, t].imag
    coef = jnp.stack([jnp.concatenate([ar, ar, ar, ar], -1),
                      jnp.concatenate([-ai, ai, ai, -ai], -1)], axis=1)
    return m.astype(BF16), wz.astype(BF16), wy.astype(BF16), coef.astype(F32)


def _mixout_kernel(*refs, moe):
    (x_ref, attn_ref, pool_ref, ys_ref, wglu_ref, bglu_ref, ga_ref, gp_ref, gs_ref,
     wo_ref, gpost_ref, gpre_ref) = refs[:12]
    if moe:
        router_ref, xo_ref, h_ref, comb_ref = refs[12:]
    else:
        xo_ref, h_ref = refs[12:]
    a = attn_ref.shape[1]
    pw = pool_ref.shape[1]
    attn_n = _rms(attn_ref[...].astype(F32), ga_ref[...]).astype(BF16)
    pool_n = _rms(pool_ref[...].astype(F32), gp_ref[...]).astype(BF16)
    y = jax.nn.gelu(ys_ref[...])
    gate = jnp.dot(y.astype(BF16), wglu_ref[...], preferred_element_type=F32) + bglu_ref[...]
    ssm = y * jax.nn.sigmoid(gate)
    ssm_n = _rms(ssm, gs_ref[...]).astype(BF16)
    mix = (jnp.dot(attn_n, wo_ref[0:a, :], preferred_element_type=F32)
           + jnp.dot(pool_n, wo_ref[a:a + pw, :], preferred_element_type=F32)
           + jnp.dot(ssm_n, wo_ref[a + pw:, :], preferred_element_type=F32))
    x = x_ref[...] + _rms(mix, gpost_ref[...])
    xo_ref[...] = x
    h = _rms(x, gpre_ref[...])
    h_ref[...] = h.astype(BF16)
    if moe:
        hi, mid, lo = _split3(h)
        rhi, rmid, rlo = _split3(router_ref[...])
        dot = lambda p, q: jnp.dot(p, q, preferred_element_type=F32)
        logits = (dot(hi, rhi) + dot(hi, rmid) + dot(mid, rhi)
                  + dot(hi, rlo) + dot(mid, rmid) + dot(lo, rhi))
        lane = lax.broadcasted_iota(jnp.int32, logits.shape, 1)
        logits = jnp.where(lane < N_EXPERTS, logits, -jnp.inf)
        m1 = jnp.max(logits, axis=1, keepdims=True)
        i1 = jnp.min(jnp.where(logits == m1, lane, LANES), axis=1, keepdims=True)
        rest = jnp.where(lane == i1, -jnp.inf, logits)
        m2 = jnp.max(rest, axis=1, keepdims=True)
        i2 = jnp.min(jnp.where(rest == m2, lane, LANES), axis=1, keepdims=True)
        e2 = jnp.exp(m2 - m1)
        g1 = 1.0 / (1.0 + e2)
        comb_ref[...] = jnp.where(lane == i1, g1, 0.0) + jnp.where(lane == i2, e2 * g1, 0.0)


def _mixout(x2, attn, pool, ys, w_glu, b_glu, g_attn, g_pool, g_ssm, w_out, g_post, g_pre,
            router=None, *, tm=512):
    n, d = x2.shape
    moe = router is not None
    row = lambda w: pl.BlockSpec((tm, w), lambda i: (i, 0))
    full = lambda arr: pl.BlockSpec(arr.shape, lambda i: (0, 0))
    ins = [x2, attn, pool, ys, w_glu, b_glu, g_attn, g_pool, g_ssm, w_out, g_post, g_pre]
    in_specs = [row(d), row(attn.shape[1]), row(pool.shape[1]), row(ys.shape[1])] + [full(t) for t in ins[4:]]
    out_shape = [jax.ShapeDtypeStruct((n, d), F32), jax.ShapeDtypeStruct((n, d), BF16)]
    out_specs = [row(d), row(d)]
    if moe:
        ins.append(router)
        in_specs.append(full(router))
        out_shape.append(jax.ShapeDtypeStruct((n, LANES), F32))
        out_specs.append(row(LANES))
    return pl.pallas_call(
        functools.partial(_mixout_kernel, moe=moe),
        grid=(n // tm,),
        in_specs=in_specs,
        out_specs=tuple(out_specs),
        out_shape=tuple(out_shape),
        compiler_params=_params(("arbitrary",)),
        name="mix_out",
    )(*ins)


def _ffn_kernel(*refs, moe):
    if moe:
        h_ref, x_ref, comb_ref, wg_ref, wu_ref, wd_ref, g_ref, o_ref, acc_ref = refs
    else:
        h_ref, x_ref, wg_ref, wu_ref, wd_ref, g_ref, o_ref, acc_ref = refs
    e = pl.program_id(1)
    f = pl.program_id(2)

    @pl.when((e == 0) & (f == 0))
    def _():
        acc_ref[...] = jnp.zeros_like(acc_ref)

    h = h_ref[...]
    hg = jnp.dot(h, wg_ref[0], preferred_element_type=F32)
    hu = jnp.dot(h, wu_ref[0], preferred_element_type=F32)
    act = hg * jax.nn.sigmoid(hg) * hu
    if moe:
        lane = lax.broadcasted_iota(jnp.int32, comb_ref.shape, 1)
        act = act * jnp.sum(jnp.where(lane == e, comb_ref[...], 0.0), axis=1, keepdims=True)
    acc_ref[...] += jnp.dot(act.astype(BF16), wd_ref[0], preferred_element_type=F32)

    @pl.when((e == pl.num_programs(1) - 1) & (f == pl.num_programs(2) - 1))
    def _():
        o_ref[...] = x_ref[...] + _rms(acc_ref[...], g_ref[...])


def _ffn(h, x2, wg, wu, wd, g_post, comb=None, *, tm=512, tf):
    n, d = x2.shape
    ne, _, fdim = wg.shape
    moe = comb is not None
    row = lambda w: pl.BlockSpec((tm, w), lambda i, e, f: (i, 0))
    ins = [h, x2] + ([comb] if moe else []) + [wg, wu, wd, g_post]
    in_specs = ([row(d), row(d)] + ([row(LANES)] if moe else [])
                + [pl.BlockSpec((1, d, tf), lambda i, e, f: (e, 0, f)),
                   pl.BlockSpec((1, d, tf), lambda i, e, f: (e, 0, f)),
                   pl.BlockSpec((1, tf, d), lambda i, e, f: (e, f, 0)),
                   pl.BlockSpec((1, d), lambda i, e, f: (0, 0))])
    return pl.pallas_call(
        functools.partial(_ffn_kernel, moe=moe),
        grid=(n // tm, ne, fdim // tf),
        in_specs=in_specs,
        out_specs=row(d),
        out_shape=jax.ShapeDtypeStruct((n, d), F32),
        scratch_shapes=[pltpu.VMEM((tm, d), F32)],
        compiler_params=_params(("arbitrary", "arbitrary", "arbitrary")),
        name="moe_ffn" if moe else "dense_ffn",
    )(*ins)


def _row(v):
    return v.reshape(1, -1).astype(F32)


def kernel(x, norm_mix_pre, norm_mix_post, norm_ffn_pre, norm_ffn_post, w_in, b_forget, pool_w, pool_scale, ssm_a_re, ssm_a_im, ssm_b_re, ssm_b_im, ssm_c_re, ssm_c_im, ssm_d, ssm_log_dt, ssm_w_glu, ssm_b_glu, branch_norm_attn, branch_norm_pool, branch_norm_ssm, w_out, ffn_w_gate, ffn_w_up, ffn_w_down, moe_router, moe_w_gate, moe_w_up, moe_w_down):
    b, L, d = x.shape
    depth = w_in.shape[0]
    n = b * L
    a = ATTN_WIDTH
    f_off = 3 * a
    p_off = f_off + ATTN_HEADS
    t, g, hd = SSM_CHUNK, SSM_GROUPS, SSM_GROUP_DIM
    nchunk = L // t
    attn_tile = 512
    x2 = x.reshape(n, d)
    for i in range(depth):
        w = w_in[i]
        w_main = jnp.concatenate([w[:, :f_off], w[:, p_off:]], axis=1).astype(BF16)
        w_f = jnp.pad(w[:, f_off:p_off], ((0, 0), (0, LANES - ATTN_HEADS))).astype(BF16)
        b_f = jnp.pad(b_forget[i], (0, LANES - ATTN_HEADS)).reshape(1, LANES)
        q, k, v, up, us, c = _inproj(x2, _row(norm_mix_pre[i]), w_main, w_f, b_f, seq=L)

        ct = c.reshape(b, L, ATTN_HEADS).transpose(0, 2, 1).reshape(b, ATTN_HEADS * (L // attn_tile), attn_tile)
        attn = _attention(q.reshape(b, L, a), k.reshape(b, L, a), v.reshape(b, L, a),
                          c.reshape(b, L, ATTN_HEADS), ct, tq=attn_tile).reshape(n, a)

        pool = _pool(up.reshape(b, L, POOL_WIDTH),
                     jax.scipy.linalg.block_diag(*pool_w[i]).astype(BF16),
                     _row(pool_scale[i])).reshape(n, POOL_WIDTH)

        mats = _ssm_matrices(ssm_a_re[i], ssm_a_im[i], ssm_b_re[i], ssm_b_im[i],
                             ssm_c_re[i], ssm_c_im[i], ssm_d[i], ssm_log_dt[i])
        uc = us.reshape(b, nchunk, t, g, hd).transpose(3, 1, 0, 2, 4).reshape(g, nchunk * b, t * hd)
        yc = _ssm_core(uc, *mats, batch=b)
        ys = yc.reshape(g, nchunk, b, t, hd).transpose(2, 1, 3, 0, 4).reshape(n, SSM_WIDTH)

        moe = i % 2 == 1
        j = i // 2
        router = jnp.pad(moe_router[j], ((0, 0), (0, LANES - N_EXPERTS))) if moe else None
        res = _mixout(x2, attn, pool, ys, ssm_w_glu[i].astype(BF16), _row(ssm_b_glu[i]),
                      _row(branch_norm_attn[i]), _row(branch_norm_pool[i]), _row(branch_norm_ssm[i]),
                      w_out[i].astype(BF16), _row(norm_mix_post[i]), _row(norm_ffn_pre[i]), router)
        if moe:
            x2, h, comb = res
            x2 = _ffn(h, x2, moe_w_gate[j].astype(BF16), moe_w_up[j].astype(BF16),
                      moe_w_down[j].astype(BF16), _row(norm_ffn_post[i]), comb, tf=896)
        else:
            x2, h = res
            x2 = _ffn(h, x2, ffn_w_gate[j][None].astype(BF16), ffn_w_up[j][None].astype(BF16),
                      ffn_w_down[j][None].astype(BF16), _row(norm_ffn_post[i]), tf=1408)
    return x2.reshape(b, L, d)
```

```python
import functools

import jax
import jax.numpy as jnp
from jax import lax
from jax.experimental import pallas as pl
from jax.experimental.pallas import tpu as pltpu

F32 = jnp.float32
BF16 = jnp.bfloat16

RMS_EPS = 1e-6
NEG_INF = -1e30

ATTN_HEADS = 8
ATTN_HEAD_DIM = 64
ATTN_WIDTH = ATTN_HEADS * ATTN_HEAD_DIM
POOL_WINDOWS = (2, 4, 8, 16)
POOL_GROUP_DIM = 64
POOL_WIDTH = len(POOL_WINDOWS) * POOL_GROUP_DIM
SSM_GROUPS = 16
SSM_GROUP_DIM = 16
SSM_STATE = 64
SSM_WIDTH = SSM_GROUPS * SSM_GROUP_DIM
SSM_CHUNK = 16
N_EXPERTS = 8

LANES = 128
VMEM_LIMIT = 48 * 1024 * 1024


def _params(sem):
    return pltpu.CompilerParams(dimension_semantics=sem, vmem_limit_bytes=VMEM_LIMIT)


def _rms(x, g):
    return x * lax.rsqrt(jnp.mean(x * x, axis=-1, keepdims=True) + RMS_EPS) * g


def _split3(x):
    hi = x.astype(BF16)
    r = x - hi.astype(F32)
    mid = r.astype(BF16)
    lo = (r - mid.astype(F32)).astype(BF16)
    return hi, mid, lo


def _inproj_kernel(x_ref, g_ref, wm_ref, wf_ref, bf_ref,
                   q_ref, k_ref, v_ref, up_ref, us_ref, c_ref, carry_ref, *, tiles_per_seq):
    @pl.when(pl.program_id(0) % tiles_per_seq == 0)
    def _():
        carry_ref[...] = jnp.zeros_like(carry_ref)

    tm = x_ref.shape[0]
    h = _rms(x_ref[...], g_ref[...]).astype(BF16)
    proj = jnp.dot(h, wm_ref[...], preferred_element_type=F32)
    a = ATTN_WIDTH
    q_ref[...] = (proj[:, 0:a] * (ATTN_HEAD_DIM ** -0.5)).astype(BF16)
    k_ref[...] = proj[:, a:2 * a].astype(BF16)
    v_ref[...] = proj[:, 2 * a:3 * a].astype(BF16)
    up_ref[...] = proj[:, 3 * a:3 * a + POOL_WIDTH].astype(BF16)
    us_ref[...] = proj[:, 3 * a + POOL_WIDTH:].astype(BF16)

    z = jnp.dot(h, wf_ref[...], preferred_element_type=F32) + bf_ref[...]
    logf = jnp.minimum(z, 0.0) - jnp.log(1.0 + jnp.exp(-jnp.abs(z)))
    row = lax.broadcasted_iota(jnp.int32, (tm, tm), 0)
    col = lax.broadcasted_iota(jnp.int32, (tm, tm), 1)
    tri = (row >= col).astype(BF16)
    hi, mid, lo = _split3(logf)
    c = (jnp.dot(tri, hi, preferred_element_type=F32)
         + jnp.dot(tri, mid, preferred_element_type=F32)
         + jnp.dot(tri, lo, preferred_element_type=F32)) + carry_ref[...]
    c_ref[...] = c[:, :ATTN_HEADS]
    carry_ref[...] = c[tm - 1:tm, :]


def _inproj(x2, g, w_main, w_f, b_f, *, seq, tm=512):
    n, d = x2.shape
    nm = w_main.shape[1]
    a = ATTN_WIDTH
    out_shape = (
        jax.ShapeDtypeStruct((n, a), BF16), jax.ShapeDtypeStruct((n, a), BF16),
        jax.ShapeDtypeStruct((n, a), BF16), jax.ShapeDtypeStruct((n, POOL_WIDTH), BF16),
        jax.ShapeDtypeStruct((n, SSM_WIDTH), BF16), jax.ShapeDtypeStruct((n, ATTN_HEADS), F32))
    row = lambda w: pl.BlockSpec((tm, w), lambda i: (i, 0))
    full = lambda r, c: pl.BlockSpec((r, c), lambda i: (0, 0))
    return pl.pallas_call(
        functools.partial(_inproj_kernel, tiles_per_seq=seq // tm),
        grid=(n // tm,),
        in_specs=[row(d), full(1, d), full(d, nm), full(d, LANES), full(1, LANES)],
        out_specs=(row(a), row(a), row(a), row(POOL_WIDTH), row(SSM_WIDTH), row(ATTN_HEADS)),
        out_shape=out_shape,
        scratch_shapes=[pltpu.VMEM((1, LANES), F32)],
        compiler_params=_params(("arbitrary",)),
        name="inproj",
    )(x2, g, w_main, w_f, b_f)


def _attn_kernel(q_ref, k_ref, v_ref, c_ref, ct_ref, o_ref, *, tq, tk):
    hp = pl.program_id(1)
    qi = pl.program_id(2)
    nkb = k_ref.shape[1] // tk
    q2 = q_ref[0]
    c_tile = c_ref[0]
    lane = lax.broadcasted_iota(jnp.int32, (1, LANES), 1)
    head_lane = lax.broadcasted_iota(jnp.int32, (1, ATTN_HEADS), 1)
    out = jnp.zeros((tq, LANES), F32)
    for hh in range(LANES // ATTN_HEAD_DIM):
        h = hp * (LANES // ATTN_HEAD_DIM) + hh
        in_head = (lane // ATTN_HEAD_DIM) == hh
        qm = jnp.where(in_head, q2, jnp.zeros_like(q2))
        cq = jnp.sum(jnp.where(head_lane == h, c_tile, 0.0), axis=1, keepdims=True)

        def step(kb, carry, masked):
            m, l, acc = carry
            kblk = k_ref[0, pl.ds(kb * tk, tk), :]
            vblk = v_ref[0, pl.ds(kb * tk, tk), :]
            s = lax.dot_general(qm, kblk, (((1,), (1,)), ((), ())), preferred_element_type=F32)
            ck = ct_ref[0, pl.ds(h * nkb + kb, 1), :]
            s = s + (cq - ck)
            if masked:
                r = lax.broadcasted_iota(jnp.int32, (tq, tk), 0)
                cidx = lax.broadcasted_iota(jnp.int32, (tq, tk), 1)
                s = jnp.where(cidx <= r, s, NEG_INF)
            m_new = jnp.maximum(m, jnp.max(s, axis=1, keepdims=True))
            p = jnp.exp(s - m_new)
            alpha = jnp.exp(m - m_new)
            l = alpha * l + jnp.sum(p, axis=1, keepdims=True)
            acc = alpha * acc + jnp.dot(p.astype(BF16), vblk, preferred_element_type=F32)
            return m_new, l, acc

        init = (jnp.full((tq, 1), NEG_INF, F32), jnp.zeros((tq, 1), F32), jnp.zeros((tq, LANES), F32))
        carry = lax.fori_loop(0, qi, lambda kb, cr: step(kb, cr, False), init)
        m, l, acc = step(qi, carry, True)
        out = jnp.where(in_head, acc / l, out)
    o_ref[0] = out.astype(o_ref.dtype)


def _attention(q, k, v, c, ct, *, tq=512):
    b, L, a = q.shape
    tk = tq
    hp = a // LANES
    blk = lambda: pl.BlockSpec((1, tq, LANES), lambda bi, hi, qi: (bi, qi, hi))
    seq = lambda: pl.BlockSpec((1, L, LANES), lambda bi, hi, qi: (bi, 0, hi))
    return pl.pallas_call(
        functools.partial(_attn_kernel, tq=tq, tk=tk),
        grid=(b, hp, L // tq),
        in_specs=[blk(), seq(), seq(),
                  pl.BlockSpec((1, tq, ATTN_HEADS), lambda bi, hi, qi: (bi, qi, 0)),
                  pl.BlockSpec((1, ct.shape[1], tk), lambda bi, hi, qi: (bi, 0, 0))],
        out_specs=blk(),
        out_shape=jax.ShapeDtypeStruct((b, L, a), BF16),
        compiler_params=_params(("arbitrary", "arbitrary", "arbitrary")),
        name="fox_attention",
    )(q, k, v, c, ct)


def _pool_kernel(u_ref, w_ref, s_ref, o_ref):
    x = u_ref[0].astype(F32)
    L, w = x.shape
    row = lax.broadcasted_iota(jnp.int32, (L, w), 0)
    group = lax.broadcasted_iota(jnp.int32, (L, w), 1) // POOL_GROUP_DIM

    def shifted(y, s):
        return jnp.where(row >= s, pltpu.roll(y, s, 0), 0.0)

    acc = x
    sel = jnp.zeros_like(x)
    win_lane = jnp.zeros((L, w), F32)
    span = 1
    for gi, win in enumerate(POOL_WINDOWS):
        while span < win:
            acc = acc + shifted(acc, span)
            span *= 2
        sel = jnp.where(group == gi, acc, sel)
        win_lane = jnp.where(group == gi, float(win), win_lane)
    count = jnp.minimum(row.astype(F32) + 1.0, win_lane)
    d = sel / count - x
    y = jnp.dot(d.astype(BF16), w_ref[...], preferred_element_type=F32) * s_ref[...]
    o_ref[0] = y.astype(o_ref.dtype)


def _pool(u, w_bd, scale):
    b, L, w = u.shape
    return pl.pallas_call(
        _pool_kernel,
        grid=(b,),
        in_specs=[pl.BlockSpec((1, L, w), lambda i: (i, 0, 0)),
                  pl.BlockSpec((w, w), lambda i: (0, 0)),
                  pl.BlockSpec((1, w), lambda i: (0, 0))],
        out_specs=pl.BlockSpec((1, L, w), lambda i: (i, 0, 0)),
        out_shape=jax.ShapeDtypeStruct((b, L, w), BF16),
        compiler_params=_params(("arbitrary",)),
        name="pool_mixer",
    )(u, w_bd, scale)


def _ssm_kernel(u_ref, m_ref, wz_ref, wy_ref, coef_ref, y_ref, z_ref, s_ref, *, batch):
    u = u_ref[0]
    rows = u.shape[0]
    half = z_ref.shape[1] // 2
    z_ref[...] = jnp.dot(u, wz_ref[0], preferred_element_type=F32)
    a1 = jnp.broadcast_to(coef_ref[0, 0:1, :], (batch, 2 * half))
    a2 = jnp.broadcast_to(coef_ref[0, 1:2, :], (batch, 2 * half))

    def step(c, st):
        r0 = pl.multiple_of(c * batch, batch)
        s_ref[pl.ds(r0, batch), :] = st[:, :half]
        sw = jnp.concatenate([st[:, half:], st[:, :half]], axis=1)
        return a1 * st + a2 * sw + z_ref[pl.ds(r0, batch), :]

    lax.fori_loop(0, rows // batch, step, jnp.zeros((batch, 2 * half), F32), unroll=8)
    y = jnp.dot(u, m_ref[0], preferred_element_type=F32)
    y = y + jnp.dot(s_ref[...].astype(BF16), wy_ref[0], preferred_element_type=F32)
    y_ref[0] = y


def _ssm_core(u_chunked, m, wz, wy, coef, *, batch):
    g, rows, w = u_chunked.shape
    p2 = wy.shape[1]
    return pl.pallas_call(
        functools.partial(_ssm_kernel, batch=batch),
        grid=(g,),
        in_specs=[pl.BlockSpec((1, rows, w), lambda i: (i, 0, 0)),
                  pl.BlockSpec((1, w, w), lambda i: (i, 0, 0)),
                  pl.BlockSpec((1, w, 2 * p2), lambda i: (i, 0, 0)),
                  pl.BlockSpec((1, p2, w), lambda i: (i, 0, 0)),
                  pl.BlockSpec((1, 2, 2 * p2), lambda i: (i, 0, 0))],
        out_specs=pl.BlockSpec((1, rows, w), lambda i: (i, 0, 0)),
        out_shape=jax.ShapeDtypeStruct((g, rows, w), F32),
        scratch_shapes=[pltpu.VMEM((rows, 2 * p2), F32), pltpu.VMEM((rows, p2), F32)],
        compiler_params=_params(("arbitrary",)),
        name="s5_core",
    )(u_chunked, m, wz, wy, coef)


def _ssm_matrices(a_re, a_im, b_re, b_im, c_re, c_im, d, log_dt):
    t = SSM_CHUNK
    g, p = a_re.shape
    hdim = d.shape[1]
    lam = lax.complex(a_re, a_im)
    lam_dt = lam * jnp.exp(log_dt)[:, None]
    lam_bar = jnp.exp(lam_dt)
    b_bar = ((lam_bar - 1.0) / lam)[:, :, None] * lax.complex(b_re, b_im)
    cmat = lax.complex(c_re, c_im)
    steps = jnp.arange(t + 1, dtype=F32)
    pw = jnp.exp(lam_dt[:, None, :] * steps[None, :, None])
    kern = jnp.einsum('gop,gdp,gpi->gdoi', cmat, pw[:, :t], b_bar).real
    kern = kern.at[:, 0].add(jax.vmap(jnp.diag)(d))
    j = jnp.arange(t)[:, None]
    i = jnp.arange(t)[None, :]
    m = jnp.where((i >= j)[None, :, :, None, None], kern[:, jnp.clip(i - j, 0, t - 1)], 0.0)
    m = m.transpose(0, 1, 4, 2, 3).reshape(g, t * hdim, t * hdim)
    wz = jnp.einsum('gjp,gpi->gjip', pw[:, t - 1 - jnp.arange(t)], b_bar).reshape(g, t * hdim, p)
    wz = jnp.concatenate([wz.real, wz.imag, wz.imag, wz.real], axis=-1)
    cw = jnp.einsum('gop,gip->giop', cmat, pw[:, 1:t + 1])
    wy = jnp.concatenate([cw.real, -cw.imag], axis=-1).reshape(g, t * hdim, 2 * p).transpose(0, 2, 1)
    ar, ai = pw[:, t].real, pw[:, t].imag
    coef = jnp.stack([jnp.concatenate([ar, ar, ar, ar], -1),
                      jnp.concatenate([-ai, ai, ai, -ai], -1)], axis=1)
    return m.astype(BF16), wz.astype(BF16), wy.astype(BF16), coef.astype(F32)


def _mixout_kernel(*refs, moe):
    (x_ref, attn_ref, pool_ref, ys_ref, wglu_ref, bglu_ref, ga_ref, gp_ref, gs_ref,
     wo_ref, gpost_ref, gpre_ref) = refs[:12]
    if moe:
        router_ref, xo_ref, h_ref, ri_ref, rg_ref, cnt_ref, cnt_acc = refs[12:]

        @pl.when(pl.program_id(0) == 0)
        def _():
            cnt_acc[...] = jnp.zeros_like(cnt_acc)
    else:
        xo_ref, h_ref = refs[12:]
    a = attn_ref.shape[1]
    pw = pool_ref.shape[1]
    attn_n = _rms(attn_ref[...].astype(F32), ga_ref[...]).astype(BF16)
    pool_n = _rms(pool_ref[...].astype(F32), gp_ref[...]).astype(BF16)
    y = jax.nn.gelu(ys_ref[...])
    gate = jnp.dot(y.astype(BF16), wglu_ref[...], preferred_element_type=F32) + bglu_ref[...]
    ssm = y * jax.nn.sigmoid(gate)
    ssm_n = _rms(ssm, gs_ref[...]).astype(BF16)
    mix = (jnp.dot(attn_n, wo_ref[0:a, :], preferred_element_type=F32)
           + jnp.dot(pool_n, wo_ref[a:a + pw, :], preferred_element_type=F32)
           + jnp.dot(ssm_n, wo_ref[a + pw:, :], preferred_element_type=F32))
    x = x_ref[...] + _rms(mix, gpost_ref[...])
    xo_ref[...] = x
    h = _rms(x, gpre_ref[...])
    h_ref[...] = h.astype(h_ref.dtype)
    if moe:
        hi, mid, lo = _split3(h)
        rhi, rmid, rlo = _split3(router_ref[...])
        dot = lambda p, q: jnp.dot(p, q, preferred_element_type=F32)
        logits = (dot(hi, rhi) + dot(hi, rmid) + dot(mid, rhi)
                  + dot(hi, rlo) + dot(mid, rmid) + dot(lo, rhi))
        lane = lax.broadcasted_iota(jnp.int32, logits.shape, 1)
        logits = jnp.where(lane < N_EXPERTS, logits, -jnp.inf)
        m1 = jnp.max(logits, axis=1, keepdims=True)
        i1 = jnp.min(jnp.where(logits == m1, lane, LANES), axis=1, keepdims=True)
        rest = jnp.where(lane == i1, -jnp.inf, logits)
        m2 = jnp.max(rest, axis=1, keepdims=True)
        i2 = jnp.min(jnp.where(rest == m2, lane, LANES), axis=1, keepdims=True)
        e2 = jnp.exp(m2 - m1)
        g1 = 1.0 / (1.0 + e2)
        rg_ref[...] = jnp.where(lane == 0, g1, jnp.where(lane == 1, e2 * g1, 0.0))
        tm = logits.shape[0]
        onehot = ((lane == i1) | (lane == i2)).astype(BF16)
        tri = (lax.broadcasted_iota(jnp.int32, (tm, tm), 0)
               > lax.broadcasted_iota(jnp.int32, (tm, tm), 1)).astype(BF16)
        before = jnp.dot(tri, onehot, preferred_element_type=F32) + cnt_acc[...]
        r1 = jnp.sum(jnp.where(lane == i1, before, 0.0), axis=1, keepdims=True).astype(jnp.int32)
        r2 = jnp.sum(jnp.where(lane == i2, before, 0.0), axis=1, keepdims=True).astype(jnp.int32)
        ri_ref[...] = jnp.where(lane == 0, i1, jnp.where(lane == 1, i2,
                                jnp.where(lane == 2, r1, jnp.where(lane == 3, r2, 0))))
        cnt_acc[...] += jnp.sum(onehot.astype(F32), axis=0, keepdims=True)
        cnt_ref[...] = cnt_acc[...]


def _mixout(x2, attn, pool, ys, w_glu, b_glu, g_attn, g_pool, g_ssm, w_out, g_post, g_pre,
            router=None, *, tm=512):
    n, d = x2.shape
    moe = router is not None
    row = lambda w: pl.BlockSpec((tm, w), lambda i: (i, 0))
    full = lambda arr: pl.BlockSpec(arr.shape, lambda i: (0, 0))
    ins = [x2, attn, pool, ys, w_glu, b_glu, g_attn, g_pool, g_ssm, w_out, g_post, g_pre]
    in_specs = [row(d), row(attn.shape[1]), row(pool.shape[1]), row(ys.shape[1])] + [full(t) for t in ins[4:]]
    out_shape = [jax.ShapeDtypeStruct((n, d), F32), jax.ShapeDtypeStruct((n, d), F32 if moe else BF16)]
    out_specs = [row(d), row(d)]
    scratch = []
    if moe:
        ins.append(router)
        in_specs.append(full(router))
        out_shape += [jax.ShapeDtypeStruct((n, LANES), jnp.int32), jax.ShapeDtypeStruct((n, LANES), F32),
                      jax.ShapeDtypeStruct((1, LANES), F32)]
        out_specs += [row(LANES), row(LANES), pl.BlockSpec((1, LANES), lambda i: (0, 0))]
        scratch = [pltpu.VMEM((1, LANES), F32)]
    return pl.pallas_call(
        functools.partial(_mixout_kernel, moe=moe),
        grid=(n // tm,),
        in_specs=in_specs,
        out_specs=tuple(out_specs),
        out_shape=tuple(out_shape),
        scratch_shapes=scratch,
        compiler_params=_params(("arbitrary",)),
        name="mix_out",
    )(*ins)


def _ffn_kernel(h_ref, x_ref, wg_ref, wu_ref, wd_ref, g_ref, o_ref, acc_ref):
    f = pl.program_id(1)

    @pl.when(f == 0)
    def _():
        acc_ref[...] = jnp.zeros_like(acc_ref)

    h = h_ref[...]
    hg = jnp.dot(h, wg_ref[...], preferred_element_type=F32)
    hu = jnp.dot(h, wu_ref[...], preferred_element_type=F32)
    act = hg * jax.nn.sigmoid(hg) * hu
    acc_ref[...] += jnp.dot(act.astype(BF16), wd_ref[...], preferred_element_type=F32)

    @pl.when(f == pl.num_programs(1) - 1)
    def _():
        o_ref[...] = x_ref[...] + _rms(acc_ref[...], g_ref[...])


def _ffn(h, x2, wg, wu, wd, g_post, *, tm=512, tf):
    n, d = x2.shape
    fdim = wg.shape[1]
    row = lambda w: pl.BlockSpec((tm, w), lambda i, f: (i, 0))
    return pl.pallas_call(
        _ffn_kernel,
        grid=(n // tm, fdim // tf),
        in_specs=[row(d), row(d),
                  pl.BlockSpec((d, tf), lambda i, f: (0, f)),
                  pl.BlockSpec((d, tf), lambda i, f: (0, f)),
                  pl.BlockSpec((tf, d), lambda i, f: (f, 0)),
                  pl.BlockSpec((1, d), lambda i, f: (0, 0))],
        out_specs=row(d),
        out_shape=jax.ShapeDtypeStruct((n, d), F32),
        scratch_shapes=[pltpu.VMEM((tm, d), F32)],
        compiler_params=_params(("arbitrary", "arbitrary")),
        name="dense_ffn",
    )(h, x2, wg, wu, wd, g_post)


MOE_TILE = 512


def _moe_tables(counts, n_items):
    tm = MOE_TILE
    counts = counts.astype(jnp.int32)
    ntiles = (counts + tm - 1) // tm
    ends = jnp.cumsum(ntiles)
    starts = ends - ntiles
    w = jnp.arange(n_items, dtype=jnp.int32)
    wc = jnp.minimum(w, ends[-1] - 1)
    e = jnp.searchsorted(ends, wc, side='right').astype(jnp.int32)
    valid = w < ends[-1]
    rowblock = jnp.where(valid, w, n_items)
    nvalid = jnp.where(valid, jnp.clip(counts[e] - (wc - starts[e]) * tm, 0, tm), 0)
    first = ((w == 0) | (e != jnp.roll(e, 1))).astype(jnp.int32)
    return (e, rowblock, nvalid.astype(jnp.int32), first), starts * tm


def _dispatch_kernel(dest_ref, h_ref, xs_ref, sem, *, n):
    tm = h_ref.shape[0]
    base = pl.program_id(0) * tm

    def body(r, carry):
        for k in range(2):
            d = dest_ref[k * n + base + r]
            pltpu.make_async_copy(h_ref.at[pl.ds(r, 1)], xs_ref.at[pl.ds(d, 1)], sem).start()
        return carry

    lax.fori_loop(0, tm, body, 0, unroll=8)
    for k in range(2):
        pltpu.make_async_copy(h_ref, h_ref, sem).wait()


def _dispatch(dest, h, cap_rows, *, tm=512):
    n, d = h.shape
    return pl.pallas_call(
        functools.partial(_dispatch_kernel, n=n),
        grid_spec=pltpu.PrefetchScalarGridSpec(
            num_scalar_prefetch=1,
            grid=(n // tm,),
            in_specs=[pl.BlockSpec((tm, d), lambda i, dest: (i, 0))],
            out_specs=pl.BlockSpec(memory_space=pl.ANY),
            scratch_shapes=[pltpu.SemaphoreType.DMA(())]),
        out_shape=jax.ShapeDtypeStruct((cap_rows, d), F32),
        compiler_params=_params(("arbitrary",)),
        name="moe_dispatch",
    )(dest, h)


def _moe_up_kernel(ie_ref, rb_ref, nv_ref, first_ref, xs_ref, wg_ref, wu_ref, h_ref, wg_s, wu_s):
    w = pl.program_id(1)

    @pl.when(first_ref[w] == 1)
    def _():
        wg_s[...] = wg_ref[0].astype(BF16)
        wu_s[...] = wu_ref[0].astype(BF16)

    nv = nv_ref[w]

    @pl.when(nv > 0)
    def _():
        rows = lax.broadcasted_iota(jnp.int32, (xs_ref.shape[0], 1), 0)
        x = jnp.where(rows < nv, xs_ref[...], 0.0).astype(BF16)
        hg = jnp.dot(x, wg_s[...], preferred_element_type=F32)
        hu = jnp.dot(x, wu_s[...], preferred_element_type=F32)
        h_ref[...] = (hg * jax.nn.sigmoid(hg) * hu).astype(BF16)

    @pl.when(nv == 0)
    def _():
        h_ref[...] = jnp.zeros_like(h_ref)


def _moe_up(tables, xs, wg, wu, *, tf=896):
    tm = MOE_TILE
    cap_rows, d = xs.shape
    ne, _, fdim = wg.shape
    n_items = tables[0].shape[0]
    return pl.pallas_call(
        _moe_up_kernel,
        grid_spec=pltpu.PrefetchScalarGridSpec(
            num_scalar_prefetch=4,
            grid=(fdim // tf, n_items),
            in_specs=[pl.BlockSpec((tm, d), lambda f, w, ie, rb, nv, fi: (rb[w], 0)),
                      pl.BlockSpec((1, d, tf), lambda f, w, ie, rb, nv, fi: (ie[w], 0, f)),
                      pl.BlockSpec((1, d, tf), lambda f, w, ie, rb, nv, fi: (ie[w], 0, f))],
            out_specs=pl.BlockSpec((tm, tf), lambda f, w, ie, rb, nv, fi: (rb[w], f)),
            scratch_shapes=[pltpu.VMEM((d, tf), BF16), pltpu.VMEM((d, tf), BF16)]),
        out_shape=jax.ShapeDtypeStruct((cap_rows, fdim), BF16),
        compiler_params=_params(("arbitrary", "arbitrary")),
        name="moe_up",
    )(*tables, xs, wg, wu)


def _moe_down_kernel(ie_ref, rb_ref, nv_ref, first_ref, h_ref, wd_ref, y_ref, wd_s):
    w = pl.program_id(1)

    @pl.when(first_ref[w] == 1)
    def _():
        wd_s[...] = wd_ref[0].astype(BF16)

    @pl.when(nv_ref[w] > 0)
    def _():
        y_ref[...] = jnp.dot(h_ref[...], wd_s[...], preferred_element_type=F32)

    @pl.when(nv_ref[w] == 0)
    def _():
        y_ref[...] = jnp.zeros_like(y_ref)


def _moe_down(tables, hid, wd, *, tn=512):
    tm = MOE_TILE
    cap_rows, fdim = hid.shape
    d = wd.shape[2]
    n_items = tables[0].shape[0]
    return pl.pallas_call(
        _moe_down_kernel,
        grid_spec=pltpu.PrefetchScalarGridSpec(
            num_scalar_prefetch=4,
            grid=(d // tn, n_items),
            in_specs=[pl.BlockSpec((tm, fdim), lambda c, w, ie, rb, nv, fi: (rb[w], 0)),
                      pl.BlockSpec((1, fdim, tn), lambda c, w, ie, rb, nv, fi: (ie[w], 0, c))],
            out_specs=pl.BlockSpec((tm, tn), lambda c, w, ie, rb, nv, fi: (rb[w], c)),
            scratch_shapes=[pltpu.VMEM((fdim, tn), BF16)]),
        out_shape=jax.ShapeDtypeStruct((cap_rows, d), F32),
        compiler_params=_params(("arbitrary", "arbitrary")),
        name="moe_down",
    )(*tables, hid, wd)


def _combine_kernel(dest_ref, x_ref, gate_ref, g_ref, ys_ref, o_ref, buf, sem, *, n):
    tm = x_ref.shape[0]
    i = pl.program_id(0)

    def issue(tile, slot):
        def body(r, carry):
            for k in range(2):
                d = dest_ref[k * n + tile * tm + r]
                pltpu.make_async_copy(ys_ref.at[pl.ds(d, 1)], buf.at[slot, k, pl.ds(r, 1)],
                                      sem.at[slot]).start()
            return carry
        lax.fori_loop(0, tm, body, 0, unroll=8)

    @pl.when(i == 0)
    def _():
        issue(0, 0)

    @pl.when(i + 1 < pl.num_programs(0))
    def _():
        issue(i + 1, (i + 1) % 2)

    slot = i % 2
    for k in range(2):
        pltpu.make_async_copy(buf.at[slot, k], buf.at[slot, k], sem.at[slot]).wait()
    gates = gate_ref[...]
    f = gates[:, 0:1] * buf[slot, 0] + gates[:, 1:2] * buf[slot, 1]
    o_ref[...] = x_ref[...] + _rms(f, g_ref[...])


def _combine(dest, x2, gates, g_post, ys, *, tm=256):
    n, d = x2.shape
    return pl.pallas_call(
        functools.partial(_combine_kernel, n=n),
        grid_spec=pltpu.PrefetchScalarGridSpec(
            num_scalar_prefetch=1,
            grid=(n // tm,),
            in_specs=[pl.BlockSpec((tm, d), lambda i, dest: (i, 0)),
                      pl.BlockSpec((tm, LANES), lambda i, dest: (i, 0)),
                      pl.BlockSpec((1, d), lambda i, dest: (0, 0)),
                      pl.BlockSpec(memory_space=pl.ANY)],
            out_specs=pl.BlockSpec((tm, d), lambda i, dest: (i, 0)),
            scratch_shapes=[pltpu.VMEM((2, 2, tm, d), F32), pltpu.SemaphoreType.DMA((2,))]),
        out_shape=jax.ShapeDtypeStruct((n, d), F32),
        compiler_params=_params(("arbitrary",)),
        name="moe_combine",
    )(dest, x2, gates, g_post, ys)


def _moe(h, x2, route_i, route_g, counts, wg, wu, wd, g_post):
    n = h.shape[0]
    n_items = 2 * n // MOE_TILE + N_EXPERTS
    tables, base = _moe_tables(counts[0, :N_EXPERTS], n_items)
    dest = jnp.concatenate([base[route_i[:, 0]] + route_i[:, 2], base[route_i[:, 1]] + route_i[:, 3]])
    xs = _dispatch(dest, h, (n_items + 1) * MOE_TILE)
    hid = _moe_up(tables, xs, wg, wu)
    ys = _moe_down(tables, hid, wd)
    return _combine(dest, x2, route_g, g_post, ys)


def _row(v):
    return v.reshape(1, -1).astype(F32)


def kernel(x, norm_mix_pre, norm_mix_post, norm_ffn_pre, norm_ffn_post, w_in, b_forget, pool_w, pool_scale, ssm_a_re, ssm_a_im, ssm_b_re, ssm_b_im, ssm_c_re, ssm_c_im, ssm_d, ssm_log_dt, ssm_w_glu, ssm_b_glu, branch_norm_attn, branch_norm_pool, branch_norm_ssm, w_out, ffn_w_gate, ffn_w_up, ffn_w_down, moe_router, moe_w_gate, moe_w_up, moe_w_down):
    b, L, d = x.shape
    depth = w_in.shape[0]
    n = b * L
    a = ATTN_WIDTH
    f_off = 3 * a
    p_off = f_off + ATTN_HEADS
    t, g, hd = SSM_CHUNK, SSM_GROUPS, SSM_GROUP_DIM
    nchunk = L // t
    attn_tile = 512
    x2 = x.reshape(n, d)
    for i in range(depth):
        w = w_in[i]
        w_main = jnp.concatenate([w[:, :f_off], w[:, p_off:]], axis=1).astype(BF16)
        w_f = jnp.pad(w[:, f_off:p_off], ((0, 0), (0, LANES - ATTN_HEADS))).astype(BF16)
        b_f = jnp.pad(b_forget[i], (0, LANES - ATTN_HEADS)).reshape(1, LANES)
        q, k, v, up, us, c = _inproj(x2, _row(norm_mix_pre[i]), w_main, w_f, b_f, seq=L)

        ct = c.reshape(b, L, ATTN_HEADS).transpose(0, 2, 1).reshape(b, ATTN_HEADS * (L // attn_tile), attn_tile)
        attn = _attention(q.reshape(b, L, a), k.reshape(b, L, a), v.reshape(b, L, a),
                          c.reshape(b, L, ATTN_HEADS), ct, tq=attn_tile).reshape(n, a)

        pool = _pool(up.reshape(b, L, POOL_WIDTH),
                     jax.scipy.linalg.block_diag(*pool_w[i]).astype(BF16),
                     _row(pool_scale[i])).reshape(n, POOL_WIDTH)

        mats = _ssm_matrices(ssm_a_re[i], ssm_a_im[i], ssm_b_re[i], ssm_b_im[i],
                             ssm_c_re[i], ssm_c_im[i], ssm_d[i], ssm_log_dt[i])
        uc = us.reshape(b, nchunk, t, g, hd).transpose(3, 1, 0, 2, 4).reshape(g, nchunk * b, t * hd)
        yc = _ssm_core(uc, *mats, batch=b)
        ys = yc.reshape(g, nchunk, b, t, hd).transpose(2, 1, 3, 0, 4).reshape(n, SSM_WIDTH)

        moe = i % 2 == 1
        j = i // 2
        router = jnp.pad(moe_router[j], ((0, 0), (0, LANES - N_EXPERTS))) if moe else None
        res = _mixout(x2, attn, pool, ys, ssm_w_glu[i].astype(BF16), _row(ssm_b_glu[i]),
                      _row(branch_norm_attn[i]), _row(branch_norm_pool[i]), _row(branch_norm_ssm[i]),
                      w_out[i].astype(BF16), _row(norm_mix_post[i]), _row(norm_ffn_pre[i]), router)
        if moe:
            x2, h, route_i, route_g, counts = res
            x2 = _moe(h, x2, route_i, route_g, counts, moe_w_gate[j], moe_w_up[j], moe_w_down[j],
                      _row(norm_ffn_post[i]))
        else:
            x2, h = res
            x2 = _ffn(h, x2, ffn_w_gate[j].astype(BF16), ffn_w_up[j].astype(BF16),
                      ffn_w_down[j].astype(BF16), _row(norm_ffn_post[i]), tf=1408)
    return x2.reshape(b, L, d)
```

```python
import functools

import jax
import jax.numpy as jnp
from jax import lax
from jax.experimental import pallas as pl
from jax.experimental.pallas import tpu as pltpu

F32 = jnp.float32
BF16 = jnp.bfloat16

RMS_EPS = 1e-6
NEG_INF = -1e30
LOG2E = 1.4426950408889634

ATTN_HEADS = 8
ATTN_HEAD_DIM = 64
ATTN_WIDTH = ATTN_HEADS * ATTN_HEAD_DIM
POOL_WINDOWS = (2, 4, 8, 16)
POOL_GROUP_DIM = 64
POOL_WIDTH = len(POOL_WINDOWS) * POOL_GROUP_DIM
SSM_GROUPS = 16
SSM_GROUP_DIM = 16
SSM_STATE = 64
SSM_WIDTH = SSM_GROUPS * SSM_GROUP_DIM
SSM_CHUNK = 16
N_EXPERTS = 8

LANES = 128
VMEM_LIMIT = 48 * 1024 * 1024


def _params(sem):
    return pltpu.CompilerParams(dimension_semantics=sem, vmem_limit_bytes=VMEM_LIMIT)


def _rms(x, g):
    return x * lax.rsqrt(jnp.mean(x * x, axis=-1, keepdims=True) + RMS_EPS) * g


def _split3(x):
    hi = x.astype(BF16)
    r = x - hi.astype(F32)
    mid = r.astype(BF16)
    lo = (r - mid.astype(F32)).astype(BF16)
    return hi, mid, lo


def _inproj_kernel(x_ref, g_ref, wm_ref, wf_ref, bf_ref,
                   q_ref, k_ref, v_ref, up_ref, us_ref, c_ref, carry_ref, *, tiles_per_seq):
    @pl.when(pl.program_id(0) % tiles_per_seq == 0)
    def _():
        carry_ref[...] = jnp.zeros_like(carry_ref)

    tm = x_ref.shape[0]
    h = _rms(x_ref[...], g_ref[...]).astype(BF16)
    proj = jnp.dot(h, wm_ref[...], preferred_element_type=F32)
    a = ATTN_WIDTH
    q_ref[...] = (proj[:, 0:a] * (LOG2E * ATTN_HEAD_DIM ** -0.5)).astype(BF16)
    k_ref[...] = proj[:, a:2 * a].astype(BF16)
    v_ref[...] = proj[:, 2 * a:3 * a].astype(BF16)
    up_ref[...] = proj[:, 3 * a:3 * a + POOL_WIDTH].astype(BF16)
    us_ref[...] = proj[:, 3 * a + POOL_WIDTH:].astype(BF16)

    z = jnp.dot(h, wf_ref[...], preferred_element_type=F32) + bf_ref[...]
    logf = jnp.minimum(z, 0.0) - jnp.log(1.0 + jnp.exp(-jnp.abs(z)))
    row = lax.broadcasted_iota(jnp.int32, (tm, tm), 0)
    col = lax.broadcasted_iota(jnp.int32, (tm, tm), 1)
    tri = (row >= col).astype(BF16)
    hi, mid, lo = _split3(logf)
    c = (jnp.dot(tri, hi, preferred_element_type=F32)
         + jnp.dot(tri, mid, preferred_element_type=F32)
         + jnp.dot(tri, lo, preferred_element_type=F32)) + carry_ref[...]
    c_ref[...] = c[:, :ATTN_HEADS]
    carry_ref[...] = c[tm - 1:tm, :]


def _inproj(x2, g, w_main, w_f, b_f, *, seq, tm=512):
    n, d = x2.shape
    nm = w_main.shape[1]
    a = ATTN_WIDTH
    out_shape = (
        jax.ShapeDtypeStruct((n, a), BF16), jax.ShapeDtypeStruct((n, a), BF16),
        jax.ShapeDtypeStruct((n, a), BF16), jax.ShapeDtypeStruct((n, POOL_WIDTH), BF16),
        jax.ShapeDtypeStruct((n, SSM_WIDTH), BF16), jax.ShapeDtypeStruct((n, ATTN_HEADS), F32))
    row = lambda w: pl.BlockSpec((tm, w), lambda i: (i, 0))
    full = lambda r, c: pl.BlockSpec((r, c), lambda i: (0, 0))
    return pl.pallas_call(
        functools.partial(_inproj_kernel, tiles_per_seq=seq // tm),
        grid=(n // tm,),
        in_specs=[row(d), full(1, d), full(d, nm), full(d, LANES), full(1, LANES)],
        out_specs=(row(a), row(a), row(a), row(POOL_WIDTH), row(SSM_WIDTH), row(ATTN_HEADS)),
        out_shape=out_shape,
        scratch_shapes=[pltpu.VMEM((1, LANES), F32)],
        compiler_params=_params(("arbitrary",)),
        name="inproj",
    )(x2, g, w_main, w_f, b_f)


HEADS_PER_BLOCK = LANES // ATTN_HEAD_DIM


def _attn_kernel(q_ref, k_ref, v_ref, ct_ref, o_ref, *, tq, tk):
    hp = pl.program_id(1)
    qi = pl.program_id(2)
    nkb = k_ref.shape[1] // tk
    q2 = q_ref[0]
    lane = lax.broadcasted_iota(jnp.int32, (1, LANES), 1)
    in_head = [(lane // ATTN_HEAD_DIM) == hh for hh in range(HEADS_PER_BLOCK)]
    qm = [jnp.where(msk, q2, jnp.zeros_like(q2)) for msk in in_head]
    crow = [(hp * HEADS_PER_BLOCK + hh) * nkb for hh in range(HEADS_PER_BLOCK)]
    cref = [ct_ref[0, pl.ds(crow[hh] + qi, 1), :][:, 0:1] for hh in range(HEADS_PER_BLOCK)]

    def step(kb, carry, masked):
        kblk = k_ref[0, pl.ds(kb * tk, tk), :]
        vblk = v_ref[0, pl.ds(kb * tk, tk), :]
        new = []
        for hh in range(HEADS_PER_BLOCK):
            m, acc = carry[hh]
            s = lax.dot_general(qm[hh], kblk, (((1,), (1,)), ((), ())), preferred_element_type=F32)
            s = s - (ct_ref[0, pl.ds(crow[hh] + kb, 1), :] - cref[hh]) * LOG2E
            if masked:
                r = lax.broadcasted_iota(jnp.int32, (tq, tk), 0)
                cidx = lax.broadcasted_iota(jnp.int32, (tq, tk), 1)
                s = jnp.where(cidx <= r, s, NEG_INF)
            m_new = jnp.maximum(m, jnp.max(s, axis=1, keepdims=True))
            p = jnp.exp2(s - m_new)
            vsel = jnp.where(in_head[hh], vblk, jnp.ones_like(vblk))
            acc = jnp.exp2(m - m_new) * acc + jnp.dot(p.astype(BF16), vsel, preferred_element_type=F32)
            new.append((m_new, acc))
        return tuple(new)

    init = tuple((jnp.full((tq, 1), NEG_INF, F32), jnp.zeros((tq, LANES), F32))
                 for _ in range(HEADS_PER_BLOCK))
    carry = lax.fori_loop(0, qi, lambda kb, cr: step(kb, cr, False), init)
    carry = step(qi, carry, True)
    out = jnp.zeros((tq, LANES), F32)
    for hh in range(HEADS_PER_BLOCK):
        acc = carry[hh][1]
        out = jnp.where(in_head[hh], acc / pltpu.roll(acc, ATTN_HEAD_DIM, 1), out)
    o_ref[0] = out.astype(o_ref.dtype)


def _attention(q, k, v, ct, *, tq):
    b, L, a = q.shape
    tk = ct.shape[2]
    assert tq == tk, "the diagonal key block of query tile i must be key block i"
    blk = lambda: pl.BlockSpec((1, tq, LANES), lambda bi, hi, qi: (bi, qi, hi))
    seq = lambda: pl.BlockSpec((1, L, LANES), lambda bi, hi, qi: (bi, 0, hi))
    return pl.pallas_call(
        functools.partial(_attn_kernel, tq=tq, tk=tk),
        grid=(b, a // LANES, L // tq),
        in_specs=[blk(), seq(), seq(),
                  pl.BlockSpec((1, ct.shape[1], tk), lambda bi, hi, qi: (bi, 0, 0))],
        out_specs=blk(),
        out_shape=jax.ShapeDtypeStruct((b, L, a), BF16),
        compiler_params=_params(("arbitrary", "arbitrary", "arbitrary")),
        name="fox_attention",
    )(q, k, v, ct)


def _pool_kernel(u_ref, w_ref, s_ref, o_ref):
    x = u_ref[0].astype(F32)
    L, w = x.shape
    row = lax.broadcasted_iota(jnp.int32, (L, w), 0)
    group = lax.broadcasted_iota(jnp.int32, (L, w), 1) // POOL_GROUP_DIM

    def shifted(y, s):
        return jnp.where(row >= s, pltpu.roll(y, s, 0), 0.0)

    acc = x
    sel = jnp.zeros_like(x)
    win_lane = jnp.zeros((L, w), F32)
    span = 1
    for gi, win in enumerate(POOL_WINDOWS):
        while span < win:
            acc = acc + shifted(acc, span)
            span *= 2
        sel = jnp.where(group == gi, acc, sel)
        win_lane = jnp.where(group == gi, float(win), win_lane)
    count = jnp.minimum(row.astype(F32) + 1.0, win_lane)
    d = sel / count - x
    y = jnp.dot(d.astype(BF16), w_ref[...], preferred_element_type=F32) * s_ref[...]
    o_ref[0] = y.astype(o_ref.dtype)


def _pool(u, w_bd, scale):
    b, L, w = u.shape
    return pl.pallas_call(
        _pool_kernel,
        grid=(b,),
        in_specs=[pl.BlockSpec((1, L, w), lambda i: (i, 0, 0)),
                  pl.BlockSpec((w, w), lambda i: (0, 0)),
                  pl.BlockSpec((1, w), lambda i: (0, 0))],
        out_specs=pl.BlockSpec((1, L, w), lambda i: (i, 0, 0)),
        out_shape=jax.ShapeDtypeStruct((b, L, w), BF16),
        compiler_params=_params(("arbitrary",)),
        name="pool_mixer",
    )(u, w_bd, scale)


def _ssm_kernel(u_ref, m_ref, wz_ref, wy_ref, coef_ref, y_ref, z_ref, s_ref, *, batch):
    u = u_ref[0]
    rows = u.shape[0]
    half = z_ref.shape[1] // 2
    z_ref[...] = jnp.dot(u, wz_ref[0], preferred_element_type=F32)
    a1 = jnp.broadcast_to(coef_ref[0, 0:1, :], (batch, 2 * half))
    a2 = jnp.broadcast_to(coef_ref[0, 1:2, :], (batch, 2 * half))

    def step(c, st):
        r0 = pl.multiple_of(c * batch, batch)
        s_ref[pl.ds(r0, batch), :] = st[:, :half]
        sw = jnp.concatenate([st[:, half:], st[:, :half]], axis=1)
        return a1 * st + a2 * sw + z_ref[pl.ds(r0, batch), :]

    lax.fori_loop(0, rows // batch, step, jnp.zeros((batch, 2 * half), F32), unroll=8)
    y = jnp.dot(u, m_ref[0], preferred_element_type=F32)
    y = y + jnp.dot(s_ref[...].astype(BF16), wy_ref[0], preferred_element_type=F32)
    y_ref[0] = y


def _ssm_core(u_chunked, m, wz, wy, coef, *, batch):
    g, rows, w = u_chunked.shape
    p2 = wy.shape[1]
    return pl.pallas_call(
        functools.partial(_ssm_kernel, batch=batch),
        grid=(g,),
        in_specs=[pl.BlockSpec((1, rows, w), lambda i: (i, 0, 0)),
                  pl.BlockSpec((1, w, w), lambda i: (i, 0, 0)),
                  pl.BlockSpec((1, w, 2 * p2), lambda i: (i, 0, 0)),
                  pl.BlockSpec((1, p2, w), lambda i: (i, 0, 0)),
                  pl.BlockSpec((1, 2, 2 * p2), lambda i: (i, 0, 0))],
        out_specs=pl.BlockSpec((1, rows, w), lambda i: (i, 0, 0)),
        out_shape=jax.ShapeDtypeStruct((g, rows, w), F32),
        scratch_shapes=[pltpu.VMEM((rows, 2 * p2), F32), pltpu.VMEM((rows, p2), F32)],
        compiler_params=_params(("arbitrary",)),
        name="s5_core",
    )(u_chunked, m, wz, wy, coef)


def _ssm_matrices(a_re, a_im, b_re, b_im, c_re, c_im, d, log_dt):
    t = SSM_CHUNK
    g, p = a_re.shape
    hdim = d.shape[1]
    lam = lax.complex(a_re, a_im)
    lam_dt = lam * jnp.exp(log_dt)[:, None]
    lam_bar = jnp.exp(lam_dt)
    b_bar = ((lam_bar - 1.0) / lam)[:, :, None] * lax.complex(b_re, b_im)
    cmat = lax.complex(c_re, c_im)
    steps = jnp.arange(t + 1, dtype=F32)
    pw = jnp.exp(lam_dt[:, None, :] * steps[None, :, None])
    kern = jnp.einsum('gop,gdp,gpi->gdoi', cmat, pw[:, :t], b_bar).real
    kern = kern.at[:, 0].add(jax.vmap(jnp.diag)(d))
    j = jnp.arange(t)[:, None]
    i = jnp.arange(t)[None, :]
    m = jnp.where((i >= j)[None, :, :, None, None], kern[:, jnp.clip(i - j, 0, t - 1)], 0.0)
    m = m.transpose(0, 1, 4, 2, 3).reshape(g, t * hdim, t * hdim)
    wz = jnp.einsum('gjp,gpi->gjip', pw[:, t - 1 - jnp.arange(t)], b_bar).reshape(g, t * hdim, p)
    wz = jnp.concatenate([wz.real, wz.imag, wz.imag, wz.real], axis=-1)
    cw = jnp.einsum('gop,gip->giop', cmat, pw[:, 1:t + 1])
    wy = jnp.concatenate([cw.real, -cw.imag], axis=-1).reshape(g, t * hdim, 2 * p).transpose(0, 2, 1)
    ar, ai = pw[:, t].real, pw[:, t].imag
    coef = jnp.stack([jnp.concatenate([ar, ar, ar, ar], -1),
                      jnp.concatenate([-ai, ai, ai, -ai], -1)], axis=1)
    return m.astype(BF16), wz.astype(BF16), wy.astype(BF16), coef.astype(F32)


def _mixout_kernel(*refs, moe):
    (x_ref, attn_ref, pool_ref, ys_ref, wglu_ref, bglu_ref, ga_ref, gp_ref, gs_ref,
     wo_ref, gpost_ref, gpre_ref) = refs[:12]
    if moe:
        router_ref, xo_ref, h_ref, ri_ref, rg_ref, cnt_ref, cnt_acc = refs[12:]

        @pl.when(pl.program_id(0) == 0)
        def _():
            cnt_acc[...] = jnp.zeros_like(cnt_acc)
    else:
        xo_ref, h_ref = refs[12:]
    a = attn_ref.shape[1]
    pw = pool_ref.shape[1]
    attn_n = _rms(attn_ref[...].astype(F32), ga_ref[...]).astype(BF16)
    pool_n = _rms(pool_ref[...].astype(F32), gp_ref[...]).astype(BF16)
    y = jax.nn.gelu(ys_ref[...])
    gate = jnp.dot(y.astype(BF16), wglu_ref[...], preferred_element_type=F32) + bglu_ref[...]
    ssm = y * jax.nn.sigmoid(gate)
    ssm_n = _rms(ssm, gs_ref[...]).astype(BF16)
    mix = (jnp.dot(attn_n, wo_ref[0:a, :], preferred_element_type=F32)
           + jnp.dot(pool_n, wo_ref[a:a + pw, :], preferred_element_type=F32)
           + jnp.dot(ssm_n, wo_ref[a + pw:, :], preferred_element_type=F32))
    x = x_ref[...] + _rms(mix, gpost_ref[...])
    xo_ref[...] = x
    h = _rms(x, gpre_ref[...])
    h_ref[...] = h.astype(h_ref.dtype)
    if moe:
        hi, mid, lo = _split3(h)
        rhi, rmid, rlo = _split3(router_ref[...])
        dot = lambda p, q: jnp.dot(p, q, preferred_element_type=F32)
        logits = (dot(hi, rhi) + dot(hi, rmid) + dot(mid, rhi)
                  + dot(hi, rlo) + dot(mid, rmid) + dot(lo, rhi))
        lane = lax.broadcasted_iota(jnp.int32, logits.shape, 1)
        logits = jnp.where(lane < N_EXPERTS, logits, -jnp.inf)
        m1 = jnp.max(logits, axis=1, keepdims=True)
        i1 = jnp.min(jnp.where(logits == m1, lane, LANES), axis=1, keepdims=True)
        rest = jnp.where(lane == i1, -jnp.inf, logits)
        m2 = jnp.max(rest, axis=1, keepdims=True)
        i2 = jnp.min(jnp.where(rest == m2, lane, LANES), axis=1, keepdims=True)
        e2 = jnp.exp(m2 - m1)
        g1 = 1.0 / (1.0 + e2)
        rg_ref[...] = jnp.where(lane == 0, g1, jnp.where(lane == 1, e2 * g1, 0.0))
        tm = logits.shape[0]
        onehot = ((lane == i1) | (lane == i2)).astype(BF16)
        tri = (lax.broadcasted_iota(jnp.int32, (tm, tm), 0)
               > lax.broadcasted_iota(jnp.int32, (tm, tm), 1)).astype(BF16)
        before = jnp.dot(tri, onehot, preferred_element_type=F32) + cnt_acc[...]
        r1 = jnp.sum(jnp.where(lane == i1, before, 0.0), axis=1, keepdims=True).astype(jnp.int32)
        r2 = jnp.sum(jnp.where(lane == i2, before, 0.0), axis=1, keepdims=True).astype(jnp.int32)
        ri_ref[...] = jnp.where(lane == 0, i1, jnp.where(lane == 1, i2,
                                jnp.where(lane == 2, r1, jnp.where(lane == 3, r2, 0))))
        cnt_acc[...] += jnp.sum(onehot.astype(F32), axis=0, keepdims=True)
        cnt_ref[...] = cnt_acc[...]


def _mixout(x2, attn, pool, ys, w_glu, b_glu, g_attn, g_pool, g_ssm, w_out, g_post, g_pre,
            router=None, *, tm=512):
    n, d = x2.shape
    moe = router is not None
    row = lambda w: pl.BlockSpec((tm, w), lambda i: (i, 0))
    full = lambda arr: pl.BlockSpec(arr.shape, lambda i: (0, 0))
    ins = [x2, attn, pool, ys, w_glu, b_glu, g_attn, g_pool, g_ssm, w_out, g_post, g_pre]
    in_specs = [row(d), row(attn.shape[1]), row(pool.shape[1]), row(ys.shape[1])] + [full(t) for t in ins[4:]]
    out_shape = [jax.ShapeDtypeStruct((n, d), F32), jax.ShapeDtypeStruct((n, d), F32 if moe else BF16)]
    out_specs = [row(d), row(d)]
    scratch = []
    if moe:
        ins.append(router)
        in_specs.append(full(router))
        out_shape += [jax.ShapeDtypeStruct((n, LANES), jnp.int32), jax.ShapeDtypeStruct((n, LANES), F32),
                      jax.ShapeDtypeStruct((1, LANES), F32)]
        out_specs += [row(LANES), row(LANES), pl.BlockSpec((1, LANES), lambda i: (0, 0))]
        scratch = [pltpu.VMEM((1, LANES), F32)]
    return pl.pallas_call(
        functools.partial(_mixout_kernel, moe=moe),
        grid=(n // tm,),
        in_specs=in_specs,
        out_specs=tuple(out_specs),
        out_shape=tuple(out_shape),
        scratch_shapes=scratch,
        compiler_params=_params(("arbitrary",)),
        name="mix_out",
    )(*ins)


def _ffn_kernel(h_ref, x_ref, wg_ref, wu_ref, wd_ref, g_ref, o_ref, acc_ref):
    f = pl.program_id(1)

    @pl.when(f == 0)
    def _():
        acc_ref[...] = jnp.zeros_like(acc_ref)

    h = h_ref[...]
    hg = jnp.dot(h, wg_ref[...], preferred_element_type=F32)
    hu = jnp.dot(h, wu_ref[...], preferred_element_type=F32)
    act = hg * jax.nn.sigmoid(hg) * hu
    acc_ref[...] += jnp.dot(act.astype(BF16), wd_ref[...], preferred_element_type=F32)

    @pl.when(f == pl.num_programs(1) - 1)
    def _():
        o_ref[...] = x_ref[...] + _rms(acc_ref[...], g_ref[...])


def _ffn(h, x2, wg, wu, wd, g_post, *, tm=512, tf):
    n, d = x2.shape
    fdim = wg.shape[1]
    row = lambda w: pl.BlockSpec((tm, w), lambda i, f: (i, 0))
    return pl.pallas_call(
        _ffn_kernel,
        grid=(n // tm, fdim // tf),
        in_specs=[row(d), row(d),
                  pl.BlockSpec((d, tf), lambda i, f: (0, f)),
                  pl.BlockSpec((d, tf), lambda i, f: (0, f)),
                  pl.BlockSpec((tf, d), lambda i, f: (f, 0)),
                  pl.BlockSpec((1, d), lambda i, f: (0, 0))],
        out_specs=row(d),
        out_shape=jax.ShapeDtypeStruct((n, d), F32),
        scratch_shapes=[pltpu.VMEM((tm, d), F32)],
        compiler_params=_params(("arbitrary", "arbitrary")),
        name="dense_ffn",
    )(h, x2, wg, wu, wd, g_post)


MOE_TILE = 512


def _moe_tables(counts, n_items):
    tm = MOE_TILE
    counts = counts.astype(jnp.int32)
    ntiles = (counts + tm - 1) // tm
    ends = jnp.cumsum(ntiles)
    starts = ends - ntiles
    w = jnp.arange(n_items, dtype=jnp.int32)
    wc = jnp.minimum(w, ends[-1] - 1)
    e = jnp.searchsorted(ends, wc, side='right').astype(jnp.int32)
    valid = w < ends[-1]
    rowblock = jnp.where(valid, w, n_items)
    nvalid = jnp.where(valid, jnp.clip(counts[e] - (wc - starts[e]) * tm, 0, tm), 0)
    first = ((w == 0) | (e != jnp.roll(e, 1))).astype(jnp.int32)
    return (e, rowblock, nvalid.astype(jnp.int32), first), starts * tm


def _dispatch_kernel(dest_ref, h_ref, xs_ref, sem, *, n):
    tm = h_ref.shape[0]
    base = pl.program_id(0) * tm

    def body(r, carry):
        for k in range(2):
            d = dest_ref[k * n + base + r]
            pltpu.make_async_copy(h_ref.at[pl.ds(r, 1)], xs_ref.at[pl.ds(d, 1)], sem).start()
        return carry

    lax.fori_loop(0, tm, body, 0, unroll=8)
    for k in range(2):
        pltpu.make_async_copy(h_ref, h_ref, sem).wait()


def _dispatch(dest, h, cap_rows, *, tm=512):
    n, d = h.shape
    return pl.pallas_call(
        functools.partial(_dispatch_kernel, n=n),
        grid_spec=pltpu.PrefetchScalarGridSpec(
            num_scalar_prefetch=1,
            grid=(n // tm,),
            in_specs=[pl.BlockSpec((tm, d), lambda i, dest: (i, 0))],
            out_specs=pl.BlockSpec(memory_space=pl.ANY),
            scratch_shapes=[pltpu.SemaphoreType.DMA(())]),
        out_shape=jax.ShapeDtypeStruct((cap_rows, d), F32),
        compiler_params=_params(("arbitrary",)),
        name="moe_dispatch",
    )(dest, h)


def _moe_up_kernel(ie_ref, rb_ref, nv_ref, first_ref, xs_ref, wg_ref, wu_ref, h_ref, wg_s, wu_s):
    w = pl.program_id(1)

    @pl.when(first_ref[w] == 1)
    def _():
        wg_s[...] = wg_ref[0].astype(BF16)
        wu_s[...] = wu_ref[0].astype(BF16)

    nv = nv_ref[w]

    @pl.when(nv > 0)
    def _():
        rows = lax.broadcasted_iota(jnp.int32, (xs_ref.shape[0], 1), 0)
        x = jnp.where(rows < nv, xs_ref[...], 0.0).astype(BF16)
        hg = jnp.dot(x, wg_s[...], preferred_element_type=F32)
        hu = jnp.dot(x, wu_s[...], preferred_element_type=F32)
        h_ref[...] = (hg * jax.nn.sigmoid(hg) * hu).astype(BF16)

    @pl.when(nv == 0)
    def _():
        h_ref[...] = jnp.zeros_like(h_ref)


def _moe_up(tables, xs, wg, wu, *, tf=896):
    tm = MOE_TILE
    cap_rows, d = xs.shape
    ne, _, fdim = wg.shape
    n_items = tables[0].shape[0]
    return pl.pallas_call(
        _moe_up_kernel,
        grid_spec=pltpu.PrefetchScalarGridSpec(
            num_scalar_prefetch=4,
            grid=(fdim // tf, n_items),
            in_specs=[pl.BlockSpec((tm, d), lambda f, w, ie, rb, nv, fi: (rb[w], 0)),
                      pl.BlockSpec((1, d, tf), lambda f, w, ie, rb, nv, fi: (ie[w], 0, f)),
                      pl.BlockSpec((1, d, tf), lambda f, w, ie, rb, nv, fi: (ie[w], 0, f))],
            out_specs=pl.BlockSpec((tm, tf), lambda f, w, ie, rb, nv, fi: (rb[w], f)),
            scratch_shapes=[pltpu.VMEM((d, tf), BF16), pltpu.VMEM((d, tf), BF16)]),
        out_shape=jax.ShapeDtypeStruct((cap_rows, fdim), BF16),
        compiler_params=_params(("arbitrary", "arbitrary")),
        name="moe_up",
    )(*tables, xs, wg, wu)


def _moe_down_kernel(ie_ref, rb_ref, nv_ref, first_ref, h_ref, wd_ref, y_ref, wd_s):
    w = pl.program_id(1)

    @pl.when(first_ref[w] == 1)
    def _():
        wd_s[...] = wd_ref[0].astype(BF16)

    @pl.when(nv_ref[w] > 0)
    def _():
        y_ref[...] = jnp.dot(h_ref[...], wd_s[...], preferred_element_type=F32)

    @pl.when(nv_ref[w] == 0)
    def _():
        y_ref[...] = jnp.zeros_like(y_ref)


def _moe_down(tables, hid, wd, *, tn=512):
    tm = MOE_TILE
    cap_rows, fdim = hid.shape
    d = wd.shape[2]
    n_items = tables[0].shape[0]
    return pl.pallas_call(
        _moe_down_kernel,
        grid_spec=pltpu.PrefetchScalarGridSpec(
            num_scalar_prefetch=4,
            grid=(d // tn, n_items),
            in_specs=[pl.BlockSpec((tm, fdim), lambda c, w, ie, rb, nv, fi: (rb[w], 0)),
                      pl.BlockSpec((1, fdim, tn), lambda c, w, ie, rb, nv, fi: (ie[w], 0, c))],
            out_specs=pl.BlockSpec((tm, tn), lambda c, w, ie, rb, nv, fi: (rb[w], c)),
            scratch_shapes=[pltpu.VMEM((fdim, tn), BF16)]),
        out_shape=jax.ShapeDtypeStruct((cap_rows, d), F32),
        compiler_params=_params(("arbitrary", "arbitrary")),
        name="moe_down",
    )(*tables, hid, wd)


def _combine_kernel(dest_ref, x_ref, gate_ref, g_ref, ys_ref, o_ref, buf, sem, *, n):
    tm = x_ref.shape[0]
    i = pl.program_id(0)

    def issue(tile, slot):
        def body(r, carry):
            for k in range(2):
                d = dest_ref[k * n + tile * tm + r]
                pltpu.make_async_copy(ys_ref.at[pl.ds(d, 1)], buf.at[slot, k, pl.ds(r, 1)],
                                      sem.at[slot]).start()
            return carry
        lax.fori_loop(0, tm, body, 0, unroll=8)

    @pl.when(i == 0)
    def _():
        issue(0, 0)

    @pl.when(i + 1 < pl.num_programs(0))
    def _():
        issue(i + 1, (i + 1) % 2)

    slot = i % 2
    for k in range(2):
        pltpu.make_async_copy(buf.at[slot, k], buf.at[slot, k], sem.at[slot]).wait()
    gates = gate_ref[...]
    f = gates[:, 0:1] * buf[slot, 0] + gates[:, 1:2] * buf[slot, 1]
    o_ref[...] = x_ref[...] + _rms(f, g_ref[...])


def _combine(dest, x2, gates, g_post, ys, *, tm=256):
    n, d = x2.shape
    return pl.pallas_call(
        functools.partial(_combine_kernel, n=n),
        grid_spec=pltpu.PrefetchScalarGridSpec(
            num_scalar_prefetch=1,
            grid=(n // tm,),
            in_specs=[pl.BlockSpec((tm, d), lambda i, dest: (i, 0)),
                      pl.BlockSpec((tm, LANES), lambda i, dest: (i, 0)),
                      pl.BlockSpec((1, d), lambda i, dest: (0, 0)),
                      pl.BlockSpec(memory_space=pl.ANY)],
            out_specs=pl.BlockSpec((tm, d), lambda i, dest: (i, 0)),
            scratch_shapes=[pltpu.VMEM((2, 2, tm, d), F32), pltpu.SemaphoreType.DMA((2,))]),
        out_shape=jax.ShapeDtypeStruct((n, d), F32),
        compiler_params=_params(("arbitrary",)),
        name="moe_combine",
    )(dest, x2, gates, g_post, ys)


def _moe(h, x2, route_i, route_g, counts, wg, wu, wd, g_post):
    n = h.shape[0]
    n_items = 2 * n // MOE_TILE + N_EXPERTS
    tables, base = _moe_tables(counts[0, :N_EXPERTS], n_items)
    dest = jnp.concatenate([base[route_i[:, 0]] + route_i[:, 2], base[route_i[:, 1]] + route_i[:, 3]])
    xs = _dispatch(dest, h, (n_items + 1) * MOE_TILE)
    hid = _moe_up(tables, xs, wg, wu)
    ys = _moe_down(tables, hid, wd)
    return _combine(dest, x2, route_g, g_post, ys)


def _row(v):
    return v.reshape(1, -1).astype(F32)


def kernel(x, norm_mix_pre, norm_mix_post, norm_ffn_pre, norm_ffn_post, w_in, b_forget, pool_w, pool_scale, ssm_a_re, ssm_a_im, ssm_b_re, ssm_b_im, ssm_c_re, ssm_c_im, ssm_d, ssm_log_dt, ssm_w_glu, ssm_b_glu, branch_norm_attn, branch_norm_pool, branch_norm_ssm, w_out, ffn_w_gate, ffn_w_up, ffn_w_down, moe_router, moe_w_gate, moe_w_up, moe_w_down):
    b, L, d = x.shape
    depth = w_in.shape[0]
    n = b * L
    a = ATTN_WIDTH
    f_off = 3 * a
    p_off = f_off + ATTN_HEADS
    t, g, hd = SSM_CHUNK, SSM_GROUPS, SSM_GROUP_DIM
    nchunk = L // t
    attn_tile = 512
    x2 = x.reshape(n, d)
    for i in range(depth):
        w = w_in[i]
        w_main = jnp.concatenate([w[:, :f_off], w[:, p_off:]], axis=1).astype(BF16)
        w_f = jnp.pad(w[:, f_off:p_off], ((0, 0), (0, LANES - ATTN_HEADS))).astype(BF16)
        b_f = jnp.pad(b_forget[i], (0, LANES - ATTN_HEADS)).reshape(1, LANES)
        q, k, v, up, us, c = _inproj(x2, _row(norm_mix_pre[i]), w_main, w_f, b_f, seq=L)

        ct = c.reshape(b, L, ATTN_HEADS).transpose(0, 2, 1).reshape(b, ATTN_HEADS * (L // attn_tile), attn_tile)
        attn = _attention(q.reshape(b, L, a), k.reshape(b, L, a), v.reshape(b, L, a),
                          ct, tq=attn_tile).reshape(n, a)

        pool = _pool(up.reshape(b, L, POOL_WIDTH),
                     jax.scipy.linalg.block_diag(*pool_w[i]).astype(BF16),
                     _row(pool_scale[i])).reshape(n, POOL_WIDTH)

        mats = _ssm_matrices(ssm_a_re[i], ssm_a_im[i], ssm_b_re[i], ssm_b_im[i],
                             ssm_c_re[i], ssm_c_im[i], ssm_d[i], ssm_log_dt[i])
        uc = us.reshape(b, nchunk, t, g, hd).transpose(3, 1, 0, 2, 4).reshape(g, nchunk * b, t * hd)
        yc = _ssm_core(uc, *mats, batch=b)
        ys = yc.reshape(g, nchunk, b, t, hd).transpose(2, 1, 3, 0, 4).reshape(n, SSM_WIDTH)

        moe = i % 2 == 1
        j = i // 2
        router = jnp.pad(moe_router[j], ((0, 0), (0, LANES - N_EXPERTS))) if moe else None
        res = _mixout(x2, attn, pool, ys, ssm_w_glu[i].astype(BF16), _row(ssm_b_glu[i]),
                      _row(branch_norm_attn[i]), _row(branch_norm_pool[i]), _row(branch_norm_ssm[i]),
                      w_out[i].astype(BF16), _row(norm_mix_post[i]), _row(norm_ffn_pre[i]), router)
        if moe:
            x2, h, route_i, route_g, counts = res
            x2 = _moe(h, x2, route_i, route_g, counts, moe_w_gate[j], moe_w_up[j], moe_w_down[j],
                      _row(norm_ffn_post[i]))
        else:
            x2, h = res
            x2 = _ffn(h, x2, ffn_w_gate[j].astype(BF16), ffn_w_up[j].astype(BF16),
                      ffn_w_down[j].astype(BF16), _row(norm_ffn_post[i]), tf=1408)
    return x2.reshape(b, L, d)
```

```python
import functools

import jax
import jax.numpy as jnp
from jax import lax
from jax.experimental import pallas as pl
from jax.experimental.pallas import tpu as pltpu

F32 = jnp.float32
BF16 = jnp.bfloat16

RMS_EPS = 1e-6
NEG_INF = -1e30
LOG2E = 1.4426950408889634

ATTN_HEADS = 8
ATTN_HEAD_DIM = 64
ATTN_WIDTH = ATTN_HEADS * ATTN_HEAD_DIM
POOL_WINDOWS = (2, 4, 8, 16)
POOL_GROUP_DIM = 64
POOL_WIDTH = len(POOL_WINDOWS) * POOL_GROUP_DIM
SSM_GROUPS = 16
SSM_GROUP_DIM = 16
SSM_STATE = 64
SSM_WIDTH = SSM_GROUPS * SSM_GROUP_DIM
SSM_CHUNK = 8
N_EXPERTS = 8

LANES = 128
VMEM_LIMIT = 48 * 1024 * 1024


def _params(sem):
    return pltpu.CompilerParams(dimension_semantics=sem, vmem_limit_bytes=VMEM_LIMIT)


def _rms(x, g):
    return x * lax.rsqrt(jnp.mean(x * x, axis=-1, keepdims=True) + RMS_EPS) * g


def _split3(x):
    hi = x.astype(BF16)
    r = x - hi.astype(F32)
    mid = r.astype(BF16)
    lo = (r - mid.astype(F32)).astype(BF16)
    return hi, mid, lo


def _inproj_kernel(x_ref, g_ref, wm_ref, wf_ref, bf_ref,
                   q_ref, k_ref, v_ref, up_ref, us_ref, c_ref, carry_ref, *, tiles_per_seq):
    @pl.when(pl.program_id(0) % tiles_per_seq == 0)
    def _():
        carry_ref[...] = jnp.zeros_like(carry_ref)

    tm = x_ref.shape[0]
    h = _rms(x_ref[...], g_ref[...]).astype(BF16)
    proj = jnp.dot(h, wm_ref[...], preferred_element_type=F32)
    a = ATTN_WIDTH
    q_ref[...] = (proj[:, 0:a] * (LOG2E * ATTN_HEAD_DIM ** -0.5)).astype(BF16)
    k_ref[...] = proj[:, a:2 * a].astype(BF16)
    v_ref[...] = proj[:, 2 * a:3 * a].astype(BF16)
    up_ref[...] = proj[:, 3 * a:3 * a + POOL_WIDTH].astype(BF16)
    us_ref[...] = proj[:, 3 * a + POOL_WIDTH:]

    z = jnp.dot(h, wf_ref[...], preferred_element_type=F32) + bf_ref[...]
    logf = jnp.minimum(z, 0.0) - jnp.log(1.0 + jnp.exp(-jnp.abs(z)))
    row = lax.broadcasted_iota(jnp.int32, (tm, tm), 0)
    col = lax.broadcasted_iota(jnp.int32, (tm, tm), 1)
    tri = (row >= col).astype(BF16)
    hi, mid, lo = _split3(logf)
    c = (jnp.dot(tri, hi, preferred_element_type=F32)
         + jnp.dot(tri, mid, preferred_element_type=F32)
         + jnp.dot(tri, lo, preferred_element_type=F32)) + carry_ref[...]
    c_ref[...] = c[:, :ATTN_HEADS]
    carry_ref[...] = c[tm - 1:tm, :]


def _inproj(x2, g, w_main, w_f, b_f, *, seq, tm=512):
    n, d = x2.shape
    nm = w_main.shape[1]
    a = ATTN_WIDTH
    out_shape = (
        jax.ShapeDtypeStruct((n, a), BF16), jax.ShapeDtypeStruct((n, a), BF16),
        jax.ShapeDtypeStruct((n, a), BF16), jax.ShapeDtypeStruct((n, POOL_WIDTH), BF16),
        jax.ShapeDtypeStruct((n, SSM_WIDTH), F32), jax.ShapeDtypeStruct((n, ATTN_HEADS), F32))
    row = lambda w: pl.BlockSpec((tm, w), lambda i: (i, 0))
    full = lambda r, c: pl.BlockSpec((r, c), lambda i: (0, 0))
    return pl.pallas_call(
        functools.partial(_inproj_kernel, tiles_per_seq=seq // tm),
        grid=(n // tm,),
        in_specs=[row(d), full(1, d), full(d, nm), full(d, LANES), full(1, LANES)],
        out_specs=(row(a), row(a), row(a), row(POOL_WIDTH), row(SSM_WIDTH), row(ATTN_HEADS)),
        out_shape=out_shape,
        scratch_shapes=[pltpu.VMEM((1, LANES), F32)],
        compiler_params=_params(("arbitrary",)),
        name="inproj",
    )(x2, g, w_main, w_f, b_f)


HEADS_PER_BLOCK = LANES // ATTN_HEAD_DIM


def _attn_kernel(q_ref, k_ref, v_ref, ct_ref, o_ref, *, tq, tk):
    hp = pl.program_id(1)
    qi = pl.program_id(2)
    nkb = k_ref.shape[1] // tk
    q2 = q_ref[0]
    lane = lax.broadcasted_iota(jnp.int32, (1, LANES), 1)
    in_head = [(lane // ATTN_HEAD_DIM) == hh for hh in range(HEADS_PER_BLOCK)]
    qm = [jnp.where(msk, q2, jnp.zeros_like(q2)) for msk in in_head]
    crow = [(hp * HEADS_PER_BLOCK + hh) * nkb for hh in range(HEADS_PER_BLOCK)]
    cref = [ct_ref[0, pl.ds(crow[hh] + qi, 1), :][:, 0:1] for hh in range(HEADS_PER_BLOCK)]

    def step(kb, carry, masked):
        kblk = k_ref[0, pl.ds(kb * tk, tk), :]
        vblk = v_ref[0, pl.ds(kb * tk, tk), :]
        new = []
        for hh in range(HEADS_PER_BLOCK):
            m, acc = carry[hh]
            s = lax.dot_general(qm[hh], kblk, (((1,), (1,)), ((), ())), preferred_element_type=F32)
            s = s - (ct_ref[0, pl.ds(crow[hh] + kb, 1), :] - cref[hh]) * LOG2E
            if masked:
                r = lax.broadcasted_iota(jnp.int32, (tq, tk), 0)
                cidx = lax.broadcasted_iota(jnp.int32, (tq, tk), 1)
                s = jnp.where(cidx <= r, s, NEG_INF)
            m_new = jnp.maximum(m, jnp.max(s, axis=1, keepdims=True))
            p = jnp.exp2(s - m_new)
            vsel = jnp.where(in_head[hh], vblk, jnp.ones_like(vblk))
            acc = jnp.exp2(m - m_new) * acc + jnp.dot(p.astype(BF16), vsel, preferred_element_type=F32)
            new.append((m_new, acc))
        return tuple(new)

    init = tuple((jnp.full((tq, 1), NEG_INF, F32), jnp.zeros((tq, LANES), F32))
                 for _ in range(HEADS_PER_BLOCK))
    carry = lax.fori_loop(0, qi, lambda kb, cr: step(kb, cr, False), init)
    carry = step(qi, carry, True)
    out = jnp.zeros((tq, LANES), F32)
    for hh in range(HEADS_PER_BLOCK):
        acc = carry[hh][1]
        out = jnp.where(in_head[hh], acc / pltpu.roll(acc, ATTN_HEAD_DIM, 1), out)
    o_ref[0] = out.astype(o_ref.dtype)


def _attention(q, k, v, ct, *, tq):
    b, L, a = q.shape
    tk = ct.shape[2]
    assert tq == tk, "the diagonal key block of query tile i must be key block i"
    blk = lambda: pl.BlockSpec((1, tq, LANES), lambda bi, hi, qi: (bi, qi, hi))
    seq = lambda: pl.BlockSpec((1, L, LANES), lambda bi, hi, qi: (bi, 0, hi))
    return pl.pallas_call(
        functools.partial(_attn_kernel, tq=tq, tk=tk),
        grid=(b, a // LANES, L // tq),
        in_specs=[blk(), seq(), seq(),
                  pl.BlockSpec((1, ct.shape[1], tk), lambda bi, hi, qi: (bi, 0, 0))],
        out_specs=blk(),
        out_shape=jax.ShapeDtypeStruct((b, L, a), BF16),
        compiler_params=_params(("arbitrary", "arbitrary", "arbitrary")),
        name="fox_attention",
    )(q, k, v, ct)


def _pool_kernel(u_ref, w_ref, s_ref, o_ref):
    x = u_ref[0].astype(F32)
    L, w = x.shape
    row = lax.broadcasted_iota(jnp.int32, (L, w), 0)
    group = lax.broadcasted_iota(jnp.int32, (L, w), 1) // POOL_GROUP_DIM

    def shifted(y, s):
        return jnp.where(row >= s, pltpu.roll(y, s, 0), 0.0)

    acc = x
    sel = jnp.zeros_like(x)
    win_lane = jnp.zeros((L, w), F32)
    span = 1
    for gi, win in enumerate(POOL_WINDOWS):
        while span < win:
            acc = acc + shifted(acc, span)
            span *= 2
        sel = jnp.where(group == gi, acc, sel)
        win_lane = jnp.where(group == gi, float(win), win_lane)
    count = jnp.minimum(row.astype(F32) + 1.0, win_lane)
    d = sel / count - x
    y = jnp.dot(d.astype(BF16), w_ref[...], preferred_element_type=F32) * s_ref[...]
    o_ref[0] = y.astype(o_ref.dtype)


def _pool(u, w_bd, scale):
    b, L, w = u.shape
    return pl.pallas_call(
        _pool_kernel,
        grid=(b,),
        in_specs=[pl.BlockSpec((1, L, w), lambda i: (i, 0, 0)),
                  pl.BlockSpec((w, w), lambda i: (0, 0)),
                  pl.BlockSpec((1, w), lambda i: (0, 0))],
        out_specs=pl.BlockSpec((1, L, w), lambda i: (i, 0, 0)),
        out_shape=jax.ShapeDtypeStruct((b, L, w), BF16),
        compiler_params=_params(("arbitrary",)),
        name="pool_mixer",
    )(u, w_bd, scale)


def _ssm_kernel(u_ref, m_ref, wz_ref, wy_ref, coef_ref, y_ref, ucat, z_ref, s_ref, *, seqs):
    t = SSM_CHUNK
    nc = u_ref.shape[0] // t
    cps = nc // seqs
    for k in range(t):
        ucat[:, k * LANES:(k + 1) * LANES] = u_ref[pl.ds(k, nc, stride=t), :].astype(BF16)
    u = ucat[...]
    z_ref[...] = jnp.dot(u, wz_ref[0], preferred_element_type=F32)
    half = z_ref.shape[1] // 2
    a1 = coef_ref[0, 0:1, :]
    a2 = coef_ref[0, 1:2, :]

    def step(c, states):
        new = []
        for b in range(seqs):
            st = states[b]
            r = b * cps + c
            s_ref[pl.ds(r, 1), :] = st
            sw = jnp.concatenate([st[:, half:], st[:, :half]], axis=1)
            new.append(a1 * st + a2 * sw + z_ref[pl.ds(r, 1), :])
        return tuple(new)

    init = tuple(jnp.zeros((1, 2 * half), F32) for _ in range(seqs))
    lax.fori_loop(0, cps, step, init, unroll=8)
    y = jnp.dot(u, m_ref[0], preferred_element_type=F32)
    y = y + jnp.dot(s_ref[...].astype(BF16), wy_ref[0], preferred_element_type=F32)
    for k in range(t):
        y_ref[pl.ds(k, nc, stride=t), :] = y[:, k * LANES:(k + 1) * LANES]


def _ssm_core(u, m, wz, wy, coef, *, seq, seqs_per_block=2):
    n, w = u.shape
    rows = seq * seqs_per_block
    nc = rows // SSM_CHUNK
    kw = m.shape[1]
    p2 = wy.shape[1]
    blk = pl.BlockSpec((rows, LANES), lambda hf, r: (r, hf))
    wspec = lambda arr: pl.BlockSpec((1,) + arr.shape[1:], lambda hf, r: (hf, 0, 0))
    return pl.pallas_call(
        functools.partial(_ssm_kernel, seqs=seqs_per_block),
        grid=(w // LANES, n // rows),
        in_specs=[blk, wspec(m), wspec(wz), wspec(wy), wspec(coef)],
        out_specs=blk,
        out_shape=jax.ShapeDtypeStruct((n, w), F32),
        scratch_shapes=[pltpu.VMEM((nc, kw), BF16), pltpu.VMEM((nc, p2), F32), pltpu.VMEM((nc, p2), F32)],
        compiler_params=_params(("arbitrary", "arbitrary")),
        name="s5_core",
    )(u, m, wz, wy, coef)


def _ssm_matrices(a_re, a_im, b_re, b_im, c_re, c_im, d, log_dt):
    t = SSM_CHUNK
    g, p = a_re.shape
    hdim = d.shape[1]
    gl = LANES // hdim
    nb = g // gl
    lam = lax.complex(a_re, a_im)
    lam_dt = lam * jnp.exp(log_dt)[:, None]
    lam_bar = jnp.exp(lam_dt)
    b_bar = ((lam_bar - 1.0) / lam)[:, :, None] * lax.complex(b_re, b_im)
    cmat = lax.complex(c_re, c_im)
    steps = jnp.arange(t + 1, dtype=F32)
    pw = jnp.exp(lam_dt[:, None, :] * steps[None, :, None])
    kern = jnp.einsum('gop,gdp,gpi->gdoi', cmat, pw[:, :t], b_bar).real
    kern = kern.at[:, 0].add(jax.vmap(jnp.diag)(d))
    j = jnp.arange(t)[:, None]
    i = jnp.arange(t)[None, :]
    m = jnp.where((i >= j)[None, :, :, None, None], kern[:, jnp.clip(i - j, 0, t - 1)], 0.0)
    wz = jnp.einsum('gjp,gpi->gjip', pw[:, t - 1 - jnp.arange(t)], b_bar)
    cw = jnp.einsum('gop,gip->giop', cmat, pw[:, 1:t + 1])
    eye = jnp.eye(gl, dtype=F32)
    blockdiag = lambda arr, spec: jnp.einsum(spec, arr.reshape((nb, gl) + arr.shape[1:]), eye)
    m_big = blockdiag(m, 'najioh,ab->njahibo').reshape(nb, t * LANES, t * LANES)
    wz_big = jnp.stack([blockdiag(wz.real, 'najhp,ab->njahbp'), blockdiag(wz.imag, 'najhp,ab->njahbp')], axis=4)
    wz_big = wz_big.reshape(nb, t * LANES, 2 * gl * p)
    wy_big = jnp.stack([blockdiag(cw.real, 'naiop,ab->napibo'), blockdiag(-cw.imag, 'naiop,ab->napibo')], axis=1)
    wy_big = wy_big.reshape(nb, 2 * gl * p, t * LANES)
    ar = pw[:, t].real.reshape(nb, gl * p)
    ai = pw[:, t].imag.reshape(nb, gl * p)
    coef = jnp.stack([jnp.concatenate([ar, ar], -1), jnp.concatenate([-ai, ai], -1)], axis=1)
    return m_big.astype(BF16), wz_big.astype(BF16), wy_big.astype(BF16), coef.astype(F32)


def _mixout_kernel(*refs, moe):
    (x_ref, attn_ref, pool_ref, ys_ref, wglu_ref, bglu_ref, ga_ref, gp_ref, gs_ref,
     wo_ref, gpost_ref, gpre_ref) = refs[:12]
    if moe:
        router_ref, xo_ref, h_ref, ri_ref, rg_ref, cnt_ref, cnt_acc = refs[12:]

        @pl.when(pl.program_id(0) == 0)
        def _():
            cnt_acc[...] = jnp.zeros_like(cnt_acc)
    else:
        xo_ref, h_ref = refs[12:]
    a = attn_ref.shape[1]
    pw = pool_ref.shape[1]
    attn_n = _rms(attn_ref[...].astype(F32), ga_ref[...]).astype(BF16)
    pool_n = _rms(pool_ref[...].astype(F32), gp_ref[...]).astype(BF16)
    y = jax.nn.gelu(ys_ref[...])
    gate = jnp.dot(y.astype(BF16), wglu_ref[...], preferred_element_type=F32) + bglu_ref[...]
    ssm = y * jax.nn.sigmoid(gate)
    ssm_n = _rms(ssm, gs_ref[...]).astype(BF16)
    mix = (jnp.dot(attn_n, wo_ref[0:a, :], preferred_element_type=F32)
           + jnp.dot(pool_n, wo_ref[a:a + pw, :], preferred_element_type=F32)
           + jnp.dot(ssm_n, wo_ref[a + pw:, :], preferred_element_type=F32))
    x = x_ref[...] + _rms(mix, gpost_ref[...])
    xo_ref[...] = x
    h = _rms(x, gpre_ref[...])
    h_ref[...] = h.astype(h_ref.dtype)
    if moe:
        hi, mid, lo = _split3(h)
        rhi, rmid, rlo = _split3(router_ref[...])
        dot = lambda p, q: jnp.dot(p, q, preferred_element_type=F32)
        logits = (dot(hi, rhi) + dot(hi, rmid) + dot(mid, rhi)
                  + dot(hi, rlo) + dot(mid, rmid) + dot(lo, rhi))
        lane = lax.broadcasted_iota(jnp.int32, logits.shape, 1)
        logits = jnp.where(lane < N_EXPERTS, logits, -jnp.inf)
        m1 = jnp.max(logits, axis=1, keepdims=True)
        i1 = jnp.min(jnp.where(logits == m1, lane, LANES), axis=1, keepdims=True)
        rest = jnp.where(lane == i1, -jnp.inf, logits)
        m2 = jnp.max(rest, axis=1, keepdims=True)
        i2 = jnp.min(jnp.where(rest == m2, lane, LANES), axis=1, keepdims=True)
        e2 = jnp.exp(m2 - m1)
        g1 = 1.0 / (1.0 + e2)
        rg_ref[...] = jnp.where(lane == 0, g1, jnp.where(lane == 1, e2 * g1, 0.0))
        tm = logits.shape[0]
        onehot = ((lane == i1) | (lane == i2)).astype(BF16)
        tri = (lax.broadcasted_iota(jnp.int32, (tm, tm), 0)
               > lax.broadcasted_iota(jnp.int32, (tm, tm), 1)).astype(BF16)
        before = jnp.dot(tri, onehot, preferred_element_type=F32) + cnt_acc[...]
        r1 = jnp.sum(jnp.where(lane == i1, before, 0.0), axis=1, keepdims=True).astype(jnp.int32)
        r2 = jnp.sum(jnp.where(lane == i2, before, 0.0), axis=1, keepdims=True).astype(jnp.int32)
        ri_ref[...] = jnp.where(lane == 0, i1, jnp.where(lane == 1, i2,
                                jnp.where(lane == 2, r1, jnp.where(lane == 3, r2, 0))))
        cnt_acc[...] += jnp.sum(onehot.astype(F32), axis=0, keepdims=True)
        cnt_ref[...] = cnt_acc[...]


def _mixout(x2, attn, pool, ys, w_glu, b_glu, g_attn, g_pool, g_ssm, w_out, g_post, g_pre,
            router=None, *, tm=512):
    n, d = x2.shape
    moe = router is not None
    row = lambda w: pl.BlockSpec((tm, w), lambda i: (i, 0))
    full = lambda arr: pl.BlockSpec(arr.shape, lambda i: (0, 0))
    ins = [x2, attn, pool, ys, w_glu, b_glu, g_attn, g_pool, g_ssm, w_out, g_post, g_pre]
    in_specs = [row(d), row(attn.shape[1]), row(pool.shape[1]), row(ys.shape[1])] + [full(t) for t in ins[4:]]
    out_shape = [jax.ShapeDtypeStruct((n, d), F32), jax.ShapeDtypeStruct((n, d), F32 if moe else BF16)]
    out_specs = [row(d), row(d)]
    scratch = []
    if moe:
        ins.append(router)
        in_specs.append(full(router))
        out_shape += [jax.ShapeDtypeStruct((n, LANES), jnp.int32), jax.ShapeDtypeStruct((n, LANES), F32),
                      jax.ShapeDtypeStruct((1, LANES), F32)]
        out_specs += [row(LANES), row(LANES), pl.BlockSpec((1, LANES), lambda i: (0, 0))]
        scratch = [pltpu.VMEM((1, LANES), F32)]
    return pl.pallas_call(
        functools.partial(_mixout_kernel, moe=moe),
        grid=(n // tm,),
        in_specs=in_specs,
        out_specs=tuple(out_specs),
        out_shape=tuple(out_shape),
        scratch_shapes=scratch,
        compiler_params=_params(("arbitrary",)),
        name="mix_out",
    )(*ins)


def _ffn_kernel(h_ref, x_ref, wg_ref, wu_ref, wd_ref, g_ref, o_ref, acc_ref):
    f = pl.program_id(1)

    @pl.when(f == 0)
    def _():
        acc_ref[...] = jnp.zeros_like(acc_ref)

    h = h_ref[...]
    hg = jnp.dot(h, wg_ref[...], preferred_element_type=F32)
    hu = jnp.dot(h, wu_ref[...], preferred_element_type=F32)
    act = hg * jax.nn.sigmoid(hg) * hu
    acc_ref[...] += jnp.dot(act.astype(BF16), wd_ref[...], preferred_element_type=F32)

    @pl.when(f == pl.num_programs(1) - 1)
    def _():
        o_ref[...] = x_ref[...] + _rms(acc_ref[...], g_ref[...])


def _ffn(h, x2, wg, wu, wd, g_post, *, tm=512, tf):
    n, d = x2.shape
    fdim = wg.shape[1]
    row = lambda w: pl.BlockSpec((tm, w), lambda i, f: (i, 0))
    return pl.pallas_call(
        _ffn_kernel,
        grid=(n // tm, fdim // tf),
        in_specs=[row(d), row(d),
                  pl.BlockSpec((d, tf), lambda i, f: (0, f)),
                  pl.BlockSpec((d, tf), lambda i, f: (0, f)),
                  pl.BlockSpec((tf, d), lambda i, f: (f, 0)),
                  pl.BlockSpec((1, d), lambda i, f: (0, 0))],
        out_specs=row(d),
        out_shape=jax.ShapeDtypeStruct((n, d), F32),
        scratch_shapes=[pltpu.VMEM((tm, d), F32)],
        compiler_params=_params(("arbitrary", "arbitrary")),
        name="dense_ffn",
    )(h, x2, wg, wu, wd, g_post)


MOE_TILE = 512


def _moe_tables(counts, n_items):
    tm = MOE_TILE
    counts = counts.astype(jnp.int32)
    ntiles = (counts + tm - 1) // tm
    ends = jnp.cumsum(ntiles)
    starts = ends - ntiles
    w = jnp.arange(n_items, dtype=jnp.int32)
    wc = jnp.minimum(w, ends[-1] - 1)
    e = jnp.searchsorted(ends, wc, side='right').astype(jnp.int32)
    valid = w < ends[-1]
    rowblock = jnp.where(valid, w, n_items)
    nvalid = jnp.where(valid, jnp.clip(counts[e] - (wc - starts[e]) * tm, 0, tm), 0)
    first = ((w == 0) | (e != jnp.roll(e, 1))).astype(jnp.int32)
    return (e, rowblock, nvalid.astype(jnp.int32), first), starts * tm


def _dispatch_kernel(dest_ref, h_ref, xs_ref, sem, *, n):
    tm = h_ref.shape[0]
    base = pl.program_id(0) * tm

    def body(r, carry):
        for k in range(2):
            d = dest_ref[k * n + base + r]
            pltpu.make_async_copy(h_ref.at[pl.ds(r, 1)], xs_ref.at[pl.ds(d, 1)], sem).start()
        return carry

    lax.fori_loop(0, tm, body, 0, unroll=8)
    for k in range(2):
        pltpu.make_async_copy(h_ref, h_ref, sem).wait()


def _dispatch(dest, h, cap_rows, *, tm=512):
    n, d = h.shape
    return pl.pallas_call(
        functools.partial(_dispatch_kernel, n=n),
        grid_spec=pltpu.PrefetchScalarGridSpec(
            num_scalar_prefetch=1,
            grid=(n // tm,),
            in_specs=[pl.BlockSpec((tm, d), lambda i, dest: (i, 0))],
            out_specs=pl.BlockSpec(memory_space=pl.ANY),
            scratch_shapes=[pltpu.SemaphoreType.DMA(())]),
        out_shape=jax.ShapeDtypeStruct((cap_rows, d), F32),
        compiler_params=_params(("arbitrary",)),
        name="moe_dispatch",
    )(dest, h)


def _moe_up_kernel(ie_ref, rb_ref, nv_ref, first_ref, xs_ref, wg_ref, wu_ref, h_ref, wg_s, wu_s):
    w = pl.program_id(1)

    @pl.when(first_ref[w] == 1)
    def _():
        wg_s[...] = wg_ref[0].astype(BF16)
        wu_s[...] = wu_ref[0].astype(BF16)

    nv = nv_ref[w]

    @pl.when(nv > 0)
    def _():
        rows = lax.broadcasted_iota(jnp.int32, (xs_ref.shape[0], 1), 0)
        x = jnp.where(rows < nv, xs_ref[...], 0.0).astype(BF16)
        hg = jnp.dot(x, wg_s[...], preferred_element_type=F32)
        hu = jnp.dot(x, wu_s[...], preferred_element_type=F32)
        h_ref[...] = (hg * jax.nn.sigmoid(hg) * hu).astype(BF16)

    @pl.when(nv == 0)
    def _():
        h_ref[...] = jnp.zeros_like(h_ref)


def _moe_up(tables, xs, wg, wu, *, tf=896):
    tm = MOE_TILE
    cap_rows, d = xs.shape
    ne, _, fdim = wg.shape
    n_items = tables[0].shape[0]
    return pl.pallas_call(
        _moe_up_kernel,
        grid_spec=pltpu.PrefetchScalarGridSpec(
            num_scalar_prefetch=4,
            grid=(fdim // tf, n_items),
            in_specs=[pl.BlockSpec((tm, d), lambda f, w, ie, rb, nv, fi: (rb[w], 0)),
                      pl.BlockSpec((1, d, tf), lambda f, w, ie, rb, nv, fi: (ie[w], 0, f)),
                      pl.BlockSpec((1, d, tf), lambda f, w, ie, rb, nv, fi: (ie[w], 0, f))],
            out_specs=pl.BlockSpec((tm, tf), lambda f, w, ie, rb, nv, fi: (rb[w], f)),
            scratch_shapes=[pltpu.VMEM((d, tf), BF16), pltpu.VMEM((d, tf), BF16)]),
        out_shape=jax.ShapeDtypeStruct((cap_rows, fdim), BF16),
        compiler_params=_params(("arbitrary", "arbitrary")),
        name="moe_up",
    )(*tables, xs, wg, wu)


def _moe_down_kernel(ie_ref, rb_ref, nv_ref, first_ref, h_ref, wd_ref, y_ref, wd_s):
    w = pl.program_id(1)

    @pl.when(first_ref[w] == 1)
    def _():
        wd_s[...] = wd_ref[0].astype(BF16)

    @pl.when(nv_ref[w] > 0)
    def _():
        y_ref[...] = jnp.dot(h_ref[...], wd_s[...], preferred_element_type=F32)

    @pl.when(nv_ref[w] == 0)
    def _():
        y_ref[...] = jnp.zeros_like(y_ref)


def _moe_down(tables, hid, wd, *, tn=512):
    tm = MOE_TILE
    cap_rows, fdim = hid.shape
    d = wd.shape[2]
    n_items = tables[0].shape[0]
    return pl.pallas_call(
        _moe_down_kernel,
        grid_spec=pltpu.PrefetchScalarGridSpec(
            num_scalar_prefetch=4,
            grid=(d // tn, n_items),
            in_specs=[pl.BlockSpec((tm, fdim), lambda c, w, ie, rb, nv, fi: (rb[w], 0)),
                      pl.BlockSpec((1, fdim, tn), lambda c, w, ie, rb, nv, fi: (ie[w], 0, c))],
            out_specs=pl.BlockSpec((tm, tn), lambda c, w, ie, rb, nv, fi: (rb[w], c)),
            scratch_shapes=[pltpu.VMEM((fdim, tn), BF16)]),
        out_shape=jax.ShapeDtypeStruct((cap_rows, d), F32),
        compiler_params=_params(("arbitrary", "arbitrary")),
        name="moe_down",
    )(*tables, hid, wd)


def _combine_kernel(dest_ref, x_ref, gate_ref, g_ref, ys_ref, o_ref, buf, sem, *, n):
    tm = x_ref.shape[0]
    i = pl.program_id(0)

    def issue(tile, slot):
        def body(r, carry):
            for k in range(2):
                d = dest_ref[k * n + tile * tm + r]
                pltpu.make_async_copy(ys_ref.at[pl.ds(d, 1)], buf.at[slot, k, pl.ds(r, 1)],
                                      sem.at[slot]).start()
            return carry
        lax.fori_loop(0, tm, body, 0, unroll=8)

    @pl.when(i == 0)
    def _():
        issue(0, 0)

    @pl.when(i + 1 < pl.num_programs(0))
    def _():
        issue(i + 1, (i + 1) % 2)

    slot = i % 2
    for k in range(2):
        pltpu.make_async_copy(buf.at[slot, k], buf.at[slot, k], sem.at[slot]).wait()
    gates = gate_ref[...]
    f = gates[:, 0:1] * buf[slot, 0] + gates[:, 1:2] * buf[slot, 1]
    o_ref[...] = x_ref[...] + _rms(f, g_ref[...])


def _combine(dest, x2, gates, g_post, ys, *, tm=256):
    n, d = x2.shape
    return pl.pallas_call(
        functools.partial(_combine_kernel, n=n),
        grid_spec=pltpu.PrefetchScalarGridSpec(
            num_scalar_prefetch=1,
            grid=(n // tm,),
            in_specs=[pl.BlockSpec((tm, d), lambda i, dest: (i, 0)),
                      pl.BlockSpec((tm, LANES), lambda i, dest: (i, 0)),
                      pl.BlockSpec((1, d), lambda i, dest: (0, 0)),
                      pl.BlockSpec(memory_space=pl.ANY)],
            out_specs=pl.BlockSpec((tm, d), lambda i, dest: (i, 0)),
            scratch_shapes=[pltpu.VMEM((2, 2, tm, d), F32), pltpu.SemaphoreType.DMA((2,))]),
        out_shape=jax.ShapeDtypeStruct((n, d), F32),
        compiler_params=_params(("arbitrary",)),
        name="moe_combine",
    )(dest, x2, gates, g_post, ys)


def _moe(h, x2, route_i, route_g, counts, wg, wu, wd, g_post):
    n = h.shape[0]
    n_items = 2 * n // MOE_TILE + N_EXPERTS
    tables, base = _moe_tables(counts[0, :N_EXPERTS], n_items)
    dest = jnp.concatenate([base[route_i[:, 0]] + route_i[:, 2], base[route_i[:, 1]] + route_i[:, 3]])
    xs = _dispatch(dest, h, (n_items + 1) * MOE_TILE)
    hid = _moe_up(tables, xs, wg, wu)
    ys = _moe_down(tables, hid, wd)
    return _combine(dest, x2, route_g, g_post, ys)


def _row(v):
    return v.reshape(1, -1).astype(F32)


def kernel(x, norm_mix_pre, norm_mix_post, norm_ffn_pre, norm_ffn_post, w_in, b_forget, pool_w, pool_scale, ssm_a_re, ssm_a_im, ssm_b_re, ssm_b_im, ssm_c_re, ssm_c_im, ssm_d, ssm_log_dt, ssm_w_glu, ssm_b_glu, branch_norm_attn, branch_norm_pool, branch_norm_ssm, w_out, ffn_w_gate, ffn_w_up, ffn_w_down, moe_router, moe_w_gate, moe_w_up, moe_w_down):
    b, L, d = x.shape
    depth = w_in.shape[0]
    n = b * L
    a = ATTN_WIDTH
    f_off = 3 * a
    p_off = f_off + ATTN_HEADS
    attn_tile = 512
    x2 = x.reshape(n, d)
    for i in range(depth):
        w = w_in[i]
        w_main = jnp.concatenate([w[:, :f_off], w[:, p_off:]], axis=1).astype(BF16)
        w_f = jnp.pad(w[:, f_off:p_off], ((0, 0), (0, LANES - ATTN_HEADS))).astype(BF16)
        b_f = jnp.pad(b_forget[i], (0, LANES - ATTN_HEADS)).reshape(1, LANES)
        q, k, v, up, us, c = _inproj(x2, _row(norm_mix_pre[i]), w_main, w_f, b_f, seq=L)

        ct = c.reshape(b, L, ATTN_HEADS).transpose(0, 2, 1).reshape(b, ATTN_HEADS * (L // attn_tile), attn_tile)
        attn = _attention(q.reshape(b, L, a), k.reshape(b, L, a), v.reshape(b, L, a),
                          ct, tq=attn_tile).reshape(n, a)

        pool = _pool(up.reshape(b, L, POOL_WIDTH),
                     jax.scipy.linalg.block_diag(*pool_w[i]).astype(BF16),
                     _row(pool_scale[i])).reshape(n, POOL_WIDTH)

        mats = _ssm_matrices(ssm_a_re[i], ssm_a_im[i], ssm_b_re[i], ssm_b_im[i],
                             ssm_c_re[i], ssm_c_im[i], ssm_d[i], ssm_log_dt[i])
        ys = _ssm_core(us, *mats, seq=L)

        moe = i % 2 == 1
        j = i // 2
        router = jnp.pad(moe_router[j], ((0, 0), (0, LANES - N_EXPERTS))) if moe else None
        res = _mixout(x2, attn, pool, ys, ssm_w_glu[i].astype(BF16), _row(ssm_b_glu[i]),
                      _row(branch_norm_attn[i]), _row(branch_norm_pool[i]), _row(branch_norm_ssm[i]),
                      w_out[i].astype(BF16), _row(norm_mix_post[i]), _row(norm_ffn_pre[i]), router)
        if moe:
            x2, h, route_i, route_g, counts = res
            x2 = _moe(h, x2, route_i, route_g, counts, moe_w_gate[j], moe_w_up[j], moe_w_down[j],
                      _row(norm_ffn_post[i]))
        else:
            x2, h = res
            x2 = _ffn(h, x2, ffn_w_gate[j].astype(BF16), ffn_w_up[j].astype(BF16),
                      ffn_w_down[j].astype(BF16), _row(norm_ffn_post[i]), tf=1408)
    return x2.reshape(b, L, d)
```

```python
import functools

import jax
import jax.numpy as jnp
from jax import lax
from jax.experimental import pallas as pl
from jax.experimental.pallas import tpu as pltpu

F32 = jnp.float32
BF16 = jnp.bfloat16

RMS_EPS = 1e-6
NEG_INF = -1e30
LOG2E = 1.4426950408889634

ATTN_HEADS = 8
ATTN_HEAD_DIM = 64
ATTN_WIDTH = ATTN_HEADS * ATTN_HEAD_DIM
POOL_WINDOWS = (2, 4, 8, 16)
POOL_GROUP_DIM = 64
POOL_WIDTH = len(POOL_WINDOWS) * POOL_GROUP_DIM
SSM_GROUPS = 16
SSM_GROUP_DIM = 16
SSM_STATE = 64
SSM_WIDTH = SSM_GROUPS * SSM_GROUP_DIM
SSM_CHUNK = 8
N_EXPERTS = 8

LANES = 128
VMEM_LIMIT = 48 * 1024 * 1024


def _params(sem):
    return pltpu.CompilerParams(dimension_semantics=sem, vmem_limit_bytes=VMEM_LIMIT)


def _rms(x, g):
    return x * lax.rsqrt(jnp.mean(x * x, axis=-1, keepdims=True) + RMS_EPS) * g


def _split3(x):
    hi = x.astype(BF16)
    r = x - hi.astype(F32)
    mid = r.astype(BF16)
    lo = (r - mid.astype(F32)).astype(BF16)
    return hi, mid, lo


def _inproj_kernel(x_ref, g_ref, wm_ref, wf_ref, bf_ref,
                   q_ref, k_ref, v_ref, up_ref, us_ref, c_ref, carry_ref, *, tiles_per_seq):
    @pl.when(pl.program_id(0) % tiles_per_seq == 0)
    def _():
        carry_ref[...] = jnp.zeros_like(carry_ref)

    tm = x_ref.shape[0]
    h = _rms(x_ref[...], g_ref[...]).astype(BF16)
    proj = jnp.dot(h, wm_ref[...], preferred_element_type=F32)
    a = ATTN_WIDTH
    q_ref[...] = (proj[:, 0:a] * (LOG2E * ATTN_HEAD_DIM ** -0.5)).astype(BF16)
    k_ref[...] = proj[:, a:2 * a].astype(BF16)
    v_ref[...] = proj[:, 2 * a:3 * a].astype(BF16)
    up_ref[...] = proj[:, 3 * a:3 * a + POOL_WIDTH].astype(BF16)
    us_ref[...] = proj[:, 3 * a + POOL_WIDTH:]

    z = jnp.dot(h, wf_ref[...], preferred_element_type=F32) + bf_ref[...]
    logf = jnp.minimum(z, 0.0) - jnp.log(1.0 + jnp.exp(-jnp.abs(z)))
    row = lax.broadcasted_iota(jnp.int32, (tm, tm), 0)
    col = lax.broadcasted_iota(jnp.int32, (tm, tm), 1)
    tri = (row >= col).astype(BF16)
    hi, mid, lo = _split3(logf)
    c = (jnp.dot(tri, hi, preferred_element_type=F32)
         + jnp.dot(tri, mid, preferred_element_type=F32)
         + jnp.dot(tri, lo, preferred_element_type=F32)) + carry_ref[...]
    c_ref[...] = c[:, :ATTN_HEADS]
    carry_ref[...] = c[tm - 1:tm, :]


def _inproj(x2, g, w_main, w_f, b_f, *, seq, tm=512):
    n, d = x2.shape
    nm = w_main.shape[1]
    a = ATTN_WIDTH
    out_shape = (
        jax.ShapeDtypeStruct((n, a), BF16), jax.ShapeDtypeStruct((n, a), BF16),
        jax.ShapeDtypeStruct((n, a), BF16), jax.ShapeDtypeStruct((n, POOL_WIDTH), BF16),
        jax.ShapeDtypeStruct((n, SSM_WIDTH), F32), jax.ShapeDtypeStruct((n, ATTN_HEADS), F32))
    row = lambda w: pl.BlockSpec((tm, w), lambda i: (i, 0))
    full = lambda r, c: pl.BlockSpec((r, c), lambda i: (0, 0))
    return pl.pallas_call(
        functools.partial(_inproj_kernel, tiles_per_seq=seq // tm),
        grid=(n // tm,),
        in_specs=[row(d), full(1, d), full(d, nm), full(d, LANES), full(1, LANES)],
        out_specs=(row(a), row(a), row(a), row(POOL_WIDTH), row(SSM_WIDTH), row(ATTN_HEADS)),
        out_shape=out_shape,
        scratch_shapes=[pltpu.VMEM((1, LANES), F32)],
        compiler_params=_params(("arbitrary",)),
        name="inproj",
    )(x2, g, w_main, w_f, b_f)


HEADS_PER_BLOCK = LANES // ATTN_HEAD_DIM


def _attn_kernel(q_ref, k_ref, v_ref, ct_ref, o_ref, *, tq, tk):
    hp = pl.program_id(1)
    qi = pl.program_id(2)
    nkb = k_ref.shape[1] // tk
    q2 = q_ref[0]
    lane = lax.broadcasted_iota(jnp.int32, (1, LANES), 1)
    in_head = [(lane // ATTN_HEAD_DIM) == hh for hh in range(HEADS_PER_BLOCK)]
    qm = [jnp.where(msk, q2, jnp.zeros_like(q2)) for msk in in_head]
    crow = [(hp * HEADS_PER_BLOCK + hh) * nkb for hh in range(HEADS_PER_BLOCK)]
    cref = [ct_ref[0, pl.ds(crow[hh] + qi, 1), :][:, 0:1] for hh in range(HEADS_PER_BLOCK)]

    def step(kb, carry, masked):
        kblk = k_ref[0, pl.ds(kb * tk, tk), :]
        vblk = v_ref[0, pl.ds(kb * tk, tk), :]
        new = []
        for hh in range(HEADS_PER_BLOCK):
            m, acc = carry[hh]
            s = lax.dot_general(qm[hh], kblk, (((1,), (1,)), ((), ())), preferred_element_type=F32)
            s = s - (ct_ref[0, pl.ds(crow[hh] + kb, 1), :] - cref[hh]) * LOG2E
            if masked:
                r = lax.broadcasted_iota(jnp.int32, (tq, tk), 0)
                cidx = lax.broadcasted_iota(jnp.int32, (tq, tk), 1)
                s = jnp.where(cidx <= r, s, NEG_INF)
            m_new = jnp.maximum(m, jnp.max(s, axis=1, keepdims=True))
            p = jnp.exp2(s - m_new)
            vsel = jnp.where(in_head[hh], vblk, jnp.ones_like(vblk))
            acc = jnp.exp2(m - m_new) * acc + jnp.dot(p.astype(BF16), vsel, preferred_element_type=F32)
            new.append((m_new, acc))
        return tuple(new)

    init = tuple((jnp.full((tq, 1), NEG_INF, F32), jnp.zeros((tq, LANES), F32))
                 for _ in range(HEADS_PER_BLOCK))
    carry = lax.fori_loop(0, qi, lambda kb, cr: step(kb, cr, False), init)
    carry = step(qi, carry, True)
    out = jnp.zeros((tq, LANES), F32)
    for hh in range(HEADS_PER_BLOCK):
        acc = carry[hh][1]
        out = jnp.where(in_head[hh], acc / pltpu.roll(acc, ATTN_HEAD_DIM, 1), out)
    o_ref[0] = out.astype(o_ref.dtype)


def _attention(q, k, v, ct, *, tq):
    b, L, a = q.shape
    tk = ct.shape[2]
    assert tq == tk, "the diagonal key block of query tile i must be key block i"
    blk = lambda: pl.BlockSpec((1, tq, LANES), lambda bi, hi, qi: (bi, qi, hi))
    seq = lambda: pl.BlockSpec((1, L, LANES), lambda bi, hi, qi: (bi, 0, hi))
    return pl.pallas_call(
        functools.partial(_attn_kernel, tq=tq, tk=tk),
        grid=(b, a // LANES, L // tq),
        in_specs=[blk(), seq(), seq(),
                  pl.BlockSpec((1, ct.shape[1], tk), lambda bi, hi, qi: (bi, 0, 0))],
        out_specs=blk(),
        out_shape=jax.ShapeDtypeStruct((b, L, a), BF16),
        compiler_params=_params(("arbitrary", "arbitrary", "arbitrary")),
        name="fox_attention",
    )(q, k, v, ct)


def _pool_kernel(u_ref, w_ref, s_ref, o_ref):
    x = u_ref[0].astype(F32)
    L, w = x.shape
    row = lax.broadcasted_iota(jnp.int32, (L, w), 0)
    group = lax.broadcasted_iota(jnp.int32, (L, w), 1) // POOL_GROUP_DIM

    def shifted(y, s):
        return jnp.where(row >= s, pltpu.roll(y, s, 0), 0.0)

    acc = x
    sel = jnp.zeros_like(x)
    win_lane = jnp.zeros((L, w), F32)
    span = 1
    for gi, win in enumerate(POOL_WINDOWS):
        while span < win:
            acc = acc + shifted(acc, span)
            span *= 2
        sel = jnp.where(group == gi, acc, sel)
        win_lane = jnp.where(group == gi, float(win), win_lane)
    count = jnp.minimum(row.astype(F32) + 1.0, win_lane)
    d = sel / count - x
    y = jnp.dot(d.astype(BF16), w_ref[...], preferred_element_type=F32) * s_ref[...]
    o_ref[0] = y.astype(o_ref.dtype)


def _pool(u, w_bd, scale):
    b, L, w = u.shape
    return pl.pallas_call(
        _pool_kernel,
        grid=(b,),
        in_specs=[pl.BlockSpec((1, L, w), lambda i: (i, 0, 0)),
                  pl.BlockSpec((w, w), lambda i: (0, 0)),
                  pl.BlockSpec((1, w), lambda i: (0, 0))],
        out_specs=pl.BlockSpec((1, L, w), lambda i: (i, 0, 0)),
        out_shape=jax.ShapeDtypeStruct((b, L, w), BF16),
        compiler_params=_params(("arbitrary",)),
        name="pool_mixer",
    )(u, w_bd, scale)


def _group_of(idx, width):
    groups = LANES // SSM_GROUP_DIM
    return (idx >> (width.bit_length() - 1)) & (groups - 1)


def _expand_blockdiag(r_ref, row_inner, col_inner):
    rows, w = r_ref.shape[1:]
    groups = LANES // SSM_GROUP_DIM
    r = lax.broadcasted_iota(jnp.int32, (w, groups * w), 0)
    c = lax.broadcasted_iota(jnp.int32, (w, groups * w), 1)
    shift = col_inner.bit_length() - 1
    outer_c = c >> (shift + groups.bit_length() - 1)
    spread = (outer_c == (r >> shift)) & ((c & (col_inner - 1)) == (r & (col_inner - 1)))
    big = jnp.dot(r_ref[0], spread.astype(BF16), preferred_element_type=F32)
    ri = lax.broadcasted_iota(jnp.int32, (rows, groups * w), 0)
    ci = lax.broadcasted_iota(jnp.int32, (rows, groups * w), 1)
    return jnp.where(_group_of(ri, row_inner) == _group_of(ci, col_inner), big, 0.0).astype(BF16)


def _ssm_kernel(u_ref, rm_ref, rz_ref, ry_ref, coef_ref, y_ref, m_s, wz_s, wy_s, ucat, z_ref, s_ref, *, seqs):
    t = SSM_CHUNK

    @pl.when(pl.program_id(1) == 0)
    def _():
        m_s[...] = _expand_blockdiag(rm_ref, SSM_GROUP_DIM, SSM_GROUP_DIM)
        wz_s[...] = _expand_blockdiag(rz_ref, SSM_GROUP_DIM, SSM_STATE)
        wy_s[...] = _expand_blockdiag(ry_ref, SSM_STATE, SSM_GROUP_DIM)

    nc = u_ref.shape[0] // t
    cps = nc // seqs
    for k in range(t):
        ucat[:, k * LANES:(k + 1) * LANES] = u_ref[pl.ds(k, nc, stride=t), :].astype(BF16)
    u = ucat[...]
    z_ref[...] = jnp.dot(u, wz_s[...], preferred_element_type=F32)
    half = z_ref.shape[1] // 2
    a1 = coef_ref[0, 0:1, :]
    a2 = coef_ref[0, 1:2, :]

    def step(c, states):
        new = []
        for b in range(seqs):
            st = states[b]
            r = b * cps + c
            s_ref[pl.ds(r, 1), :] = st
            sw = jnp.concatenate([st[:, half:], st[:, :half]], axis=1)
            new.append(a1 * st + a2 * sw + z_ref[pl.ds(r, 1), :])
        return tuple(new)

    init = tuple(jnp.zeros((1, 2 * half), F32) for _ in range(seqs))
    lax.fori_loop(0, cps, step, init, unroll=8)
    y = jnp.dot(u, m_s[...], preferred_element_type=F32)
    y = y + jnp.dot(s_ref[...].astype(BF16), wy_s[...], preferred_element_type=F32)
    for k in range(t):
        y_ref[pl.ds(k, nc, stride=t), :] = y[:, k * LANES:(k + 1) * LANES]


def _ssm_core(u, rm, rz, ry, coef, *, seq, seqs_per_block=2):
    n, w = u.shape
    rows = seq * seqs_per_block
    nc = rows // SSM_CHUNK
    kw = SSM_CHUNK * LANES
    p2 = coef.shape[2]
    blk = pl.BlockSpec((rows, LANES), lambda hf, r: (r, hf))
    wspec = lambda arr: pl.BlockSpec((1,) + arr.shape[1:], lambda hf, r: (hf, 0, 0))
    return pl.pallas_call(
        functools.partial(_ssm_kernel, seqs=seqs_per_block),
        grid=(w // LANES, n // rows),
        in_specs=[blk, wspec(rm), wspec(rz), wspec(ry), wspec(coef)],
        out_specs=blk,
        out_shape=jax.ShapeDtypeStruct((n, w), F32),
        scratch_shapes=[pltpu.VMEM((kw, kw), BF16), pltpu.VMEM((kw, p2), BF16), pltpu.VMEM((p2, kw), BF16),
                        pltpu.VMEM((nc, kw), BF16), pltpu.VMEM((nc, p2), F32), pltpu.VMEM((nc, p2), F32)],
        compiler_params=_params(("arbitrary", "arbitrary")),
        name="s5_core",
    )(u, rm, rz, ry, coef)


def _ssm_matrices(a_re, a_im, b_re, b_im, c_re, c_im, d, log_dt):
    t = SSM_CHUNK
    g, p = a_re.shape
    hdim = d.shape[1]
    gl = LANES // hdim
    nb = g // gl
    lam = lax.complex(a_re, a_im)
    lam_dt = lam * jnp.exp(log_dt)[:, None]
    lam_bar = jnp.exp(lam_dt)
    b_bar = ((lam_bar - 1.0) / lam)[:, :, None] * lax.complex(b_re, b_im)
    cmat = lax.complex(c_re, c_im)
    steps = jnp.arange(t + 1, dtype=F32)
    pw = jnp.exp(lam_dt[:, None, :] * steps[None, :, None])
    kern = jnp.einsum('gop,gdp,gpi->gdoi', cmat, pw[:, :t], b_bar).real
    kern = kern.at[:, 0].add(jax.vmap(jnp.diag)(d))
    j = jnp.arange(t)[:, None]
    i = jnp.arange(t)[None, :]
    m = jnp.where((i >= j)[None, :, :, None, None], kern[:, jnp.clip(i - j, 0, t - 1)], 0.0)
    wz = jnp.einsum('gjp,gpi->gjip', pw[:, t - 1 - jnp.arange(t)], b_bar)
    cw = jnp.einsum('gop,gip->giop', cmat, pw[:, 1:t + 1])
    rm = m.reshape(nb, gl, t, t, hdim, hdim).transpose(0, 2, 1, 5, 3, 4).reshape(nb, t * LANES, t * hdim)
    rz = jnp.stack([wz.real, wz.imag], axis=3).reshape(nb, gl, t, hdim, 2, p)
    rz = rz.transpose(0, 2, 1, 3, 4, 5).reshape(nb, t * LANES, 2 * p)
    ry = jnp.stack([cw.real, -cw.imag], axis=1).reshape(nb, gl, 2, t, hdim, p)
    ry = ry.transpose(0, 2, 1, 5, 3, 4).reshape(nb, 2 * gl * p, t * hdim)
    ar = pw[:, t].real.reshape(nb, gl * p)
    ai = pw[:, t].imag.reshape(nb, gl * p)
    coef = jnp.stack([jnp.concatenate([ar, ar], -1), jnp.concatenate([-ai, ai], -1)], axis=1)
    return rm.astype(BF16), rz.astype(BF16), ry.astype(BF16), coef.astype(F32)


def _mixout_kernel(*refs, moe):
    (x_ref, attn_ref, pool_ref, ys_ref, wglu_ref, bglu_ref, ga_ref, gp_ref, gs_ref,
     wo_ref, gpost_ref, gpre_ref) = refs[:12]
    if moe:
        router_ref, xo_ref, h_ref, ri_ref, rg_ref, cnt_ref, cnt_acc = refs[12:]

        @pl.when(pl.program_id(0) == 0)
        def _():
            cnt_acc[...] = jnp.zeros_like(cnt_acc)
    else:
        xo_ref, h_ref = refs[12:]
    a = attn_ref.shape[1]
    pw = pool_ref.shape[1]
    attn_n = _rms(attn_ref[...].astype(F32), ga_ref[...]).astype(BF16)
    pool_n = _rms(pool_ref[...].astype(F32), gp_ref[...]).astype(BF16)
    y = jax.nn.gelu(ys_ref[...])
    gate = jnp.dot(y.astype(BF16), wglu_ref[...], preferred_element_type=F32) + bglu_ref[...]
    ssm = y * jax.nn.sigmoid(gate)
    ssm_n = _rms(ssm, gs_ref[...]).astype(BF16)
    mix = (jnp.dot(attn_n, wo_ref[0:a, :], preferred_element_type=F32)
           + jnp.dot(pool_n, wo_ref[a:a + pw, :], preferred_element_type=F32)
           + jnp.dot(ssm_n, wo_ref[a + pw:, :], preferred_element_type=F32))
    x = x_ref[...] + _rms(mix, gpost_ref[...])
    xo_ref[...] = x
    h = _rms(x, gpre_ref[...])
    h_ref[...] = h.astype(h_ref.dtype)
    if moe:
        hi, mid, _ = _split3(h)
        rhi, rmid, _ = _split3(router_ref[...])
        dot = lambda p, q: jnp.dot(p, q, preferred_element_type=F32)
        logits = dot(hi, rhi) + dot(hi, rmid) + dot(mid, rhi)
        lane = lax.broadcasted_iota(jnp.int32, logits.shape, 1)
        logits = jnp.where(lane < N_EXPERTS, logits, -jnp.inf)
        m1 = jnp.max(logits, axis=1, keepdims=True)
        i1 = jnp.min(jnp.where(logits == m1, lane, LANES), axis=1, keepdims=True)
        rest = jnp.where(lane == i1, -jnp.inf, logits)
        m2 = jnp.max(rest, axis=1, keepdims=True)
        i2 = jnp.min(jnp.where(rest == m2, lane, LANES), axis=1, keepdims=True)
        e2 = jnp.exp(m2 - m1)
        g1 = 1.0 / (1.0 + e2)
        rg_ref[...] = jnp.where(lane == 0, g1, jnp.where(lane == 1, e2 * g1, 0.0))
        tm = logits.shape[0]
        onehot = ((lane == i1) | (lane == i2)).astype(BF16)
        tri = (lax.broadcasted_iota(jnp.int32, (tm, tm), 0)
               > lax.broadcasted_iota(jnp.int32, (tm, tm), 1)).astype(BF16)
        before = jnp.dot(tri, onehot, preferred_element_type=F32) + cnt_acc[...]
        r1 = jnp.sum(jnp.where(lane == i1, before, 0.0), axis=1, keepdims=True).astype(jnp.int32)
        r2 = jnp.sum(jnp.where(lane == i2, before, 0.0), axis=1, keepdims=True).astype(jnp.int32)
        ri = jnp.where(lane == 0, i1, jnp.where(lane == 1, i2,
                       jnp.where(lane == 2, r1, jnp.where(lane == 3, r2, 0))))
        ri_ref[...] = ri.T[:ri_ref.shape[0], :]
        cnt_acc[...] += jnp.sum(onehot.astype(F32), axis=0, keepdims=True)
        cnt_ref[...] = cnt_acc[...]


def _mixout(x2, attn, pool, ys, w_glu, b_glu, g_attn, g_pool, g_ssm, w_out, g_post, g_pre,
            router=None, *, tm=512):
    n, d = x2.shape
    moe = router is not None
    row = lambda w: pl.BlockSpec((tm, w), lambda i: (i, 0))
    full = lambda arr: pl.BlockSpec(arr.shape, lambda i: (0, 0))
    ins = [x2, attn, pool, ys, w_glu, b_glu, g_attn, g_pool, g_ssm, w_out, g_post, g_pre]
    in_specs = [row(d), row(attn.shape[1]), row(pool.shape[1]), row(ys.shape[1])] + [full(t) for t in ins[4:]]
    out_shape = [jax.ShapeDtypeStruct((n, d), F32), jax.ShapeDtypeStruct((n, d), F32 if moe else BF16)]
    out_specs = [row(d), row(d)]
    scratch = []
    if moe:
        ins.append(router)
        in_specs.append(full(router))
        out_shape += [jax.ShapeDtypeStruct((8, n), jnp.int32), jax.ShapeDtypeStruct((n, LANES), F32),
                      jax.ShapeDtypeStruct((1, LANES), F32)]
        out_specs += [pl.BlockSpec((8, tm), lambda i: (0, i)), row(LANES),
                      pl.BlockSpec((1, LANES), lambda i: (0, 0))]
        scratch = [pltpu.VMEM((1, LANES), F32)]
    return pl.pallas_call(
        functools.partial(_mixout_kernel, moe=moe),
        grid=(n // tm,),
        in_specs=in_specs,
        out_specs=tuple(out_specs),
        out_shape=tuple(out_shape),
        scratch_shapes=scratch,
        compiler_params=_params(("arbitrary",)),
        name="mix_out",
    )(*ins)


def _ffn_kernel(h_ref, x_ref, wg_ref, wu_ref, wd_ref, g_ref, o_ref, acc_ref):
    f = pl.program_id(1)

    @pl.when(f == 0)
    def _():
        acc_ref[...] = jnp.zeros_like(acc_ref)

    h = h_ref[...]
    hg = jnp.dot(h, wg_ref[...], preferred_element_type=F32)
    hu = jnp.dot(h, wu_ref[...], preferred_element_type=F32)
    act = hg * jax.nn.sigmoid(hg) * hu
    acc_ref[...] += jnp.dot(act.astype(BF16), wd_ref[...], preferred_element_type=F32)

    @pl.when(f == pl.num_programs(1) - 1)
    def _():
        o_ref[...] = x_ref[...] + _rms(acc_ref[...], g_ref[...])


def _ffn(h, x2, wg, wu, wd, g_post, *, tm=512, tf):
    n, d = x2.shape
    fdim = wg.shape[1]
    row = lambda w: pl.BlockSpec((tm, w), lambda i, f: (i, 0))
    return pl.pallas_call(
        _ffn_kernel,
        grid=(n // tm, fdim // tf),
        in_specs=[row(d), row(d),
                  pl.BlockSpec((d, tf), lambda i, f: (0, f)),
                  pl.BlockSpec((d, tf), lambda i, f: (0, f)),
                  pl.BlockSpec((tf, d), lambda i, f: (f, 0)),
                  pl.BlockSpec((1, d), lambda i, f: (0, 0))],
        out_specs=row(d),
        out_shape=jax.ShapeDtypeStruct((n, d), F32),
        scratch_shapes=[pltpu.VMEM((tm, d), F32)],
        compiler_params=_params(("arbitrary", "arbitrary")),
        name="dense_ffn",
    )(h, x2, wg, wu, wd, g_post)


MOE_TILE = 512


def _moe_tables(counts, n_items):
    tm = MOE_TILE
    counts = counts.astype(jnp.int32)
    ntiles = (counts + tm - 1) // tm
    ends = jnp.cumsum(ntiles)
    starts = ends - ntiles
    w = jnp.arange(n_items, dtype=jnp.int32)
    wc = jnp.minimum(w, ends[-1] - 1)
    e = jnp.sum((wc[:, None] >= ends[None, :]).astype(jnp.int32), axis=1)
    valid = w < ends[-1]
    rowblock = jnp.where(valid, w, n_items)
    nvalid = jnp.where(valid, jnp.clip(counts[e] - (wc - starts[e]) * tm, 0, tm), 0)
    first = ((w == 0) | (e != jnp.roll(e, 1))).astype(jnp.int32)
    return (e, rowblock, nvalid.astype(jnp.int32), first), starts * tm


def _dispatch_kernel(dest_ref, h_ref, xs_ref, sem, *, n):
    tm = h_ref.shape[0]
    base = pl.program_id(0) * tm

    def body(r, carry):
        for k in range(2):
            d = dest_ref[k * n + base + r]
            pltpu.make_async_copy(h_ref.at[pl.ds(r, 1)], xs_ref.at[pl.ds(d, 1)], sem).start()
        return carry

    lax.fori_loop(0, tm, body, 0, unroll=8)
    for k in range(2):
        pltpu.make_async_copy(h_ref, h_ref, sem).wait()


def _dispatch(dest, h, cap_rows, *, tm=512):
    n, d = h.shape
    return pl.pallas_call(
        functools.partial(_dispatch_kernel, n=n),
        grid_spec=pltpu.PrefetchScalarGridSpec(
            num_scalar_prefetch=1,
            grid=(n // tm,),
            in_specs=[pl.BlockSpec((tm, d), lambda i, dest: (i, 0))],
            out_specs=pl.BlockSpec(memory_space=pl.ANY),
            scratch_shapes=[pltpu.SemaphoreType.DMA(())]),
        out_shape=jax.ShapeDtypeStruct((cap_rows, d), F32),
        compiler_params=_params(("arbitrary",)),
        name="moe_dispatch",
    )(dest, h)


def _moe_up_kernel(ie_ref, rb_ref, nv_ref, first_ref, xs_ref, wg_ref, wu_ref, h_ref, wg_s, wu_s):
    w = pl.program_id(1)

    @pl.when(first_ref[w] == 1)
    def _():
        wg_s[...] = wg_ref[0].astype(BF16)
        wu_s[...] = wu_ref[0].astype(BF16)

    nv = nv_ref[w]

    @pl.when(nv > 0)
    def _():
        rows = lax.broadcasted_iota(jnp.int32, (xs_ref.shape[0], 1), 0)
        x = jnp.where(rows < nv, xs_ref[...], 0.0).astype(BF16)
        hg = jnp.dot(x, wg_s[...], preferred_element_type=F32)
        hu = jnp.dot(x, wu_s[...], preferred_element_type=F32)
        h_ref[...] = (hg * jax.nn.sigmoid(hg) * hu).astype(BF16)

    @pl.when(nv == 0)
    def _():
        h_ref[...] = jnp.zeros_like(h_ref)


def _moe_up(tables, xs, wg, wu, *, tf=896):
    tm = MOE_TILE
    cap_rows, d = xs.shape
    ne, _, fdim = wg.shape
    n_items = tables[0].shape[0]
    return pl.pallas_call(
        _moe_up_kernel,
        grid_spec=pltpu.PrefetchScalarGridSpec(
            num_scalar_prefetch=4,
            grid=(fdim // tf, n_items),
            in_specs=[pl.BlockSpec((tm, d), lambda f, w, ie, rb, nv, fi: (rb[w], 0)),
                      pl.BlockSpec((1, d, tf), lambda f, w, ie, rb, nv, fi: (ie[w], 0, f)),
                      pl.BlockSpec((1, d, tf), lambda f, w, ie, rb, nv, fi: (ie[w], 0, f))],
            out_specs=pl.BlockSpec((tm, tf), lambda f, w, ie, rb, nv, fi: (rb[w], f)),
            scratch_shapes=[pltpu.VMEM((d, tf), BF16), pltpu.VMEM((d, tf), BF16)]),
        out_shape=jax.ShapeDtypeStruct((cap_rows, fdim), BF16),
        compiler_params=_params(("arbitrary", "arbitrary")),
        name="moe_up",
    )(*tables, xs, wg, wu)


def _moe_down_kernel(ie_ref, rb_ref, nv_ref, first_ref, h_ref, wd_ref, y_ref, wd_s):
    w = pl.program_id(1)

    @pl.when(first_ref[w] == 1)
    def _():
        wd_s[...] = wd_ref[0].astype(BF16)

    @pl.when(nv_ref[w] > 0)
    def _():
        y_ref[...] = jnp.dot(h_ref[...], wd_s[...], preferred_element_type=F32)

    @pl.when(nv_ref[w] == 0)
    def _():
        y_ref[...] = jnp.zeros_like(y_ref)


def _moe_down(tables, hid, wd, *, tn=512):
    tm = MOE_TILE
    cap_rows, fdim = hid.shape
    d = wd.shape[2]
    n_items = tables[0].shape[0]
    return pl.pallas_call(
        _moe_down_kernel,
        grid_spec=pltpu.PrefetchScalarGridSpec(
            num_scalar_prefetch=4,
            grid=(d // tn, n_items),
            in_specs=[pl.BlockSpec((tm, fdim), lambda c, w, ie, rb, nv, fi: (rb[w], 0)),
                      pl.BlockSpec((1, fdim, tn), lambda c, w, ie, rb, nv, fi: (ie[w], 0, c))],
            out_specs=pl.BlockSpec((tm, tn), lambda c, w, ie, rb, nv, fi: (rb[w], c)),
            scratch_shapes=[pltpu.VMEM((fdim, tn), BF16)]),
        out_shape=jax.ShapeDtypeStruct((cap_rows, d), F32),
        compiler_params=_params(("arbitrary", "arbitrary")),
        name="moe_down",
    )(*tables, hid, wd)


def _combine_kernel(dest_ref, x_ref, gate_ref, g_ref, ys_ref, o_ref, buf, sem, *, n):
    tm = x_ref.shape[0]
    i = pl.program_id(0)

    def issue(tile, slot):
        def body(r, carry):
            for k in range(2):
                d = dest_ref[k * n + tile * tm + r]
                pltpu.make_async_copy(ys_ref.at[pl.ds(d, 1)], buf.at[slot, k, pl.ds(r, 1)],
                                      sem.at[slot]).start()
            return carry
        lax.fori_loop(0, tm, body, 0, unroll=8)

    @pl.when(i == 0)
    def _():
        issue(0, 0)

    @pl.when(i + 1 < pl.num_programs(0))
    def _():
        issue(i + 1, (i + 1) % 2)

    slot = i % 2
    for k in range(2):
        pltpu.make_async_copy(buf.at[slot, k], buf.at[slot, k], sem.at[slot]).wait()
    gates = gate_ref[...]
    f = gates[:, 0:1] * buf[slot, 0] + gates[:, 1:2] * buf[slot, 1]
    o_ref[...] = x_ref[...] + _rms(f, g_ref[...])


def _combine(dest, x2, gates, g_post, ys, *, tm=256):
    n, d = x2.shape
    return pl.pallas_call(
        functools.partial(_combine_kernel, n=n),
        grid_spec=pltpu.PrefetchScalarGridSpec(
            num_scalar_prefetch=1,
            grid=(n // tm,),
            in_specs=[pl.BlockSpec((tm, d), lambda i, dest: (i, 0)),
                      pl.BlockSpec((tm, LANES), lambda i, dest: (i, 0)),
                      pl.BlockSpec((1, d), lambda i, dest: (0, 0)),
                      pl.BlockSpec(memory_space=pl.ANY)],
            out_specs=pl.BlockSpec((tm, d), lambda i, dest: (i, 0)),
            scratch_shapes=[pltpu.VMEM((2, 2, tm, d), F32), pltpu.SemaphoreType.DMA((2,))]),
        out_shape=jax.ShapeDtypeStruct((n, d), F32),
        compiler_params=_params(("arbitrary",)),
        name="moe_combine",
    )(dest, x2, gates, g_post, ys)


def _moe(h, x2, route_i, route_g, counts, wg, wu, wd, g_post):
    n = h.shape[0]
    n_items = 2 * n // MOE_TILE + N_EXPERTS
    tables, base = _moe_tables(counts[0, :N_EXPERTS], n_items)
    base_of = lambda e: jnp.sum(jnp.where(e[None, :] == jnp.arange(N_EXPERTS)[:, None], base[:, None], 0), axis=0)
    dest = jnp.concatenate([base_of(route_i[0]) + route_i[2], base_of(route_i[1]) + route_i[3]])
    xs = _dispatch(dest, h, (n_items + 1) * MOE_TILE)
    hid = _moe_up(tables, xs, wg, wu)
    ys = _moe_down(tables, hid, wd)
    return _combine(dest, x2, route_g, g_post, ys)


def _row(v):
    return v.reshape(1, -1).astype(F32)


def kernel(x, norm_mix_pre, norm_mix_post, norm_ffn_pre, norm_ffn_post, w_in, b_forget, pool_w, pool_scale, ssm_a_re, ssm_a_im, ssm_b_re, ssm_b_im, ssm_c_re, ssm_c_im, ssm_d, ssm_log_dt, ssm_w_glu, ssm_b_glu, branch_norm_attn, branch_norm_pool, branch_norm_ssm, w_out, ffn_w_gate, ffn_w_up, ffn_w_down, moe_router, moe_w_gate, moe_w_up, moe_w_down):
    b, L, d = x.shape
    depth = w_in.shape[0]
    n = b * L
    a = ATTN_WIDTH
    f_off = 3 * a
    p_off = f_off + ATTN_HEADS
    attn_tile = 512
    x2 = x.reshape(n, d)
    for i in range(depth):
        w = w_in[i]
        w_main = jnp.concatenate([w[:, :f_off], w[:, p_off:]], axis=1).astype(BF16)
        w_f = jnp.pad(w[:, f_off:p_off], ((0, 0), (0, LANES - ATTN_HEADS))).astype(BF16)
        b_f = jnp.pad(b_forget[i], (0, LANES - ATTN_HEADS)).reshape(1, LANES)
        q, k, v, up, us, c = _inproj(x2, _row(norm_mix_pre[i]), w_main, w_f, b_f, seq=L)

        ct = c.reshape(b, L, ATTN_HEADS).transpose(0, 2, 1).reshape(b, ATTN_HEADS * (L // attn_tile), attn_tile)
        attn = _attention(q.reshape(b, L, a), k.reshape(b, L, a), v.reshape(b, L, a),
                          ct, tq=attn_tile).reshape(n, a)

        pool = _pool(up.reshape(b, L, POOL_WIDTH),
                     jax.scipy.linalg.block_diag(*pool_w[i]).astype(BF16),
                     _row(pool_scale[i])).reshape(n, POOL_WIDTH)

        mats = _ssm_matrices(ssm_a_re[i], ssm_a_im[i], ssm_b_re[i], ssm_b_im[i],
                             ssm_c_re[i], ssm_c_im[i], ssm_d[i], ssm_log_dt[i])
        ys = _ssm_core(us, *mats, seq=L)

        moe = i % 2 == 1
        j = i // 2
        router = jnp.pad(moe_router[j], ((0, 0), (0, LANES - N_EXPERTS))) if moe else None
        res = _mixout(x2, attn, pool, ys, ssm_w_glu[i].astype(BF16), _row(ssm_b_glu[i]),
                      _row(branch_norm_attn[i]), _row(branch_norm_pool[i]), _row(branch_norm_ssm[i]),
                      w_out[i].astype(BF16), _row(norm_mix_post[i]), _row(norm_ffn_pre[i]), router)
        if moe:
            x2, h, route_i, route_g, counts = res
            x2 = _moe(h, x2, route_i, route_g, counts, moe_w_gate[j], moe_w_up[j], moe_w_down[j],
                      _row(norm_ffn_post[i]))
        else:
            x2, h = res
            x2 = _ffn(h, x2, ffn_w_gate[j].astype(BF16), ffn_w_up[j].astype(BF16),
                      ffn_w_down[j].astype(BF16), _row(norm_ffn_post[i]), tf=1408)
    return x2.reshape(b, L, d)
```

```python
import functools

import jax
import jax.numpy as jnp
from jax import lax
from jax.experimental import pallas as pl
from jax.experimental.pallas import tpu as pltpu

F32 = jnp.float32
BF16 = jnp.bfloat16

RMS_EPS = 1e-6
NEG_INF = -1e30
LOG2E = 1.4426950408889634

ATTN_HEADS = 8
ATTN_HEAD_DIM = 64
ATTN_WIDTH = ATTN_HEADS * ATTN_HEAD_DIM
POOL_WINDOWS = (2, 4, 8, 16)
POOL_GROUP_DIM = 64
POOL_WIDTH = len(POOL_WINDOWS) * POOL_GROUP_DIM
SSM_GROUPS = 16
SSM_GROUP_DIM = 16
SSM_STATE = 64
SSM_WIDTH = SSM_GROUPS * SSM_GROUP_DIM
SSM_CHUNK = 8
N_EXPERTS = 8

LANES = 128
VMEM_LIMIT = 48 * 1024 * 1024


def _params(sem):
    return pltpu.CompilerParams(dimension_semantics=sem, vmem_limit_bytes=VMEM_LIMIT)


def _rms(x, g):
    return x * lax.rsqrt(jnp.mean(x * x, axis=-1, keepdims=True) + RMS_EPS) * g


def _split3(x):
    hi = x.astype(BF16)
    r = x - hi.astype(F32)
    mid = r.astype(BF16)
    lo = (r - mid.astype(F32)).astype(BF16)
    return hi, mid, lo


def _inproj_kernel(x_ref, g_ref, wm_ref, wf_ref, bf_ref,
                   q_ref, k_ref, v_ref, up_ref, us_ref, c_ref, carry_ref, *, tiles_per_seq):
    @pl.when(pl.program_id(0) % tiles_per_seq == 0)
    def _():
        carry_ref[...] = jnp.zeros_like(carry_ref)

    tm = x_ref.shape[0]
    h = _rms(x_ref[...], g_ref[...]).astype(BF16)
    proj = jnp.dot(h, wm_ref[...], preferred_element_type=F32)
    a = ATTN_WIDTH
    q_ref[...] = (proj[:, 0:a] * (LOG2E * ATTN_HEAD_DIM ** -0.5)).astype(BF16)
    k_ref[...] = proj[:, a:2 * a].astype(BF16)
    v_ref[...] = proj[:, 2 * a:3 * a].astype(BF16)
    up_ref[...] = proj[:, 3 * a:3 * a + POOL_WIDTH].astype(BF16)
    us_ref[...] = proj[:, 3 * a + POOL_WIDTH:]

    z = jnp.dot(h, wf_ref[...], preferred_element_type=F32) + bf_ref[...]
    logf = jnp.minimum(z, 0.0) - jnp.log(1.0 + jnp.exp(-jnp.abs(z)))
    row = lax.broadcasted_iota(jnp.int32, (tm, tm), 0)
    col = lax.broadcasted_iota(jnp.int32, (tm, tm), 1)
    tri = (row >= col).astype(BF16)
    hi, mid, lo = _split3(logf)
    c = (jnp.dot(tri, hi, preferred_element_type=F32)
         + jnp.dot(tri, mid, preferred_element_type=F32)
         + jnp.dot(tri, lo, preferred_element_type=F32)) + carry_ref[...]
    c_ref[...] = c[:, :ATTN_HEADS]
    carry_ref[...] = c[tm - 1:tm, :]


def _inproj(x2, g, w_main, w_f, b_f, *, seq, tm=512):
    n, d = x2.shape
    nm = w_main.shape[1]
    a = ATTN_WIDTH
    out_shape = (
        jax.ShapeDtypeStruct((n, a), BF16), jax.ShapeDtypeStruct((n, a), BF16),
        jax.ShapeDtypeStruct((n, a), BF16), jax.ShapeDtypeStruct((n, POOL_WIDTH), BF16),
        jax.ShapeDtypeStruct((n, SSM_WIDTH), F32), jax.ShapeDtypeStruct((n, ATTN_HEADS), F32))
    row = lambda w: pl.BlockSpec((tm, w), lambda i: (i, 0))
    full = lambda r, c: pl.BlockSpec((r, c), lambda i: (0, 0))
    return pl.pallas_call(
        functools.partial(_inproj_kernel, tiles_per_seq=seq // tm),
        grid=(n // tm,),
        in_specs=[row(d), full(1, d), full(d, nm), full(d, LANES), full(1, LANES)],
        out_specs=(row(a), row(a), row(a), row(POOL_WIDTH), row(SSM_WIDTH), row(ATTN_HEADS)),
        out_shape=out_shape,
        scratch_shapes=[pltpu.VMEM((1, LANES), F32)],
        compiler_params=_params(("arbitrary",)),
        name="inproj",
    )(x2, g, w_main, w_f, b_f)


HEADS_PER_BLOCK = LANES // ATTN_HEAD_DIM


def _attn_kernel(q_ref, k_ref, v_ref, ct_ref, o_ref, *, tq, tk):
    hp = pl.program_id(1)
    qi = pl.program_id(2)
    nkb = k_ref.shape[1] // tk
    q2 = q_ref[0]
    lane = lax.broadcasted_iota(jnp.int32, (1, LANES), 1)
    in_head = [(lane // ATTN_HEAD_DIM) == hh for hh in range(HEADS_PER_BLOCK)]
    qm = [jnp.where(msk, q2, jnp.zeros_like(q2)) for msk in in_head]
    crow = [(hp * HEADS_PER_BLOCK + hh) * nkb for hh in range(HEADS_PER_BLOCK)]
    cref = [ct_ref[0, pl.ds(crow[hh] + qi, 1), :][:, 0:1] for hh in range(HEADS_PER_BLOCK)]

    def step(kb, carry, masked):
        kblk = k_ref[0, pl.ds(kb * tk, tk), :]
        vblk = v_ref[0, pl.ds(kb * tk, tk), :]
        new = []
        for hh in range(HEADS_PER_BLOCK):
            m, acc = carry[hh]
            s = lax.dot_general(qm[hh], kblk, (((1,), (1,)), ((), ())), preferred_element_type=F32)
            s = s - (ct_ref[0, pl.ds(crow[hh] + kb, 1), :] - cref[hh]) * LOG2E
            if masked:
                r = lax.broadcasted_iota(jnp.int32, (tq, tk), 0)
                cidx = lax.broadcasted_iota(jnp.int32, (tq, tk), 1)
                s = jnp.where(cidx <= r, s, NEG_INF)
            m_new = jnp.maximum(m, jnp.max(s, axis=1, keepdims=True))
            p = jnp.exp2(s - m_new)
            vsel = jnp.where(in_head[hh], vblk, jnp.ones_like(vblk))
            acc = jnp.exp2(m - m_new) * acc + jnp.dot(p.astype(BF16), vsel, preferred_element_type=F32)
            new.append((m_new, acc))
        return tuple(new)

    init = tuple((jnp.full((tq, 1), NEG_INF, F32), jnp.zeros((tq, LANES), F32))
                 for _ in range(HEADS_PER_BLOCK))
    carry = lax.fori_loop(0, qi, lambda kb, cr: step(kb, cr, False), init)
    carry = step(qi, carry, True)
    out = jnp.zeros((tq, LANES), F32)
    for hh in range(HEADS_PER_BLOCK):
        acc = carry[hh][1]
        out = jnp.where(in_head[hh], acc / pltpu.roll(acc, ATTN_HEAD_DIM, 1), out)
    o_ref[0] = out.astype(o_ref.dtype)


def _attention(q, k, v, ct, *, tq):
    b, L, a = q.shape
    tk = ct.shape[2]
    assert tq == tk, "the diagonal key block of query tile i must be key block i"
    blk = lambda: pl.BlockSpec((1, tq, LANES), lambda bi, hi, qi: (bi, qi, hi))
    seq = lambda: pl.BlockSpec((1, L, LANES), lambda bi, hi, qi: (bi, 0, hi))
    return pl.pallas_call(
        functools.partial(_attn_kernel, tq=tq, tk=tk),
        grid=(b, a // LANES, L // tq),
        in_specs=[blk(), seq(), seq(),
                  pl.BlockSpec((1, ct.shape[1], tk), lambda bi, hi, qi: (bi, 0, 0))],
        out_specs=blk(),
        out_shape=jax.ShapeDtypeStruct((b, L, a), BF16),
        compiler_params=_params(("arbitrary", "arbitrary", "arbitrary")),
        name="fox_attention",
    )(q, k, v, ct)


def _pool_kernel(u_ref, w_ref, s_ref, o_ref):
    x = u_ref[0].astype(F32)
    L, w = x.shape
    row = lax.broadcasted_iota(jnp.int32, (L, w), 0)
    group = lax.broadcasted_iota(jnp.int32, (L, w), 1) // POOL_GROUP_DIM

    def shifted(y, s):
        return jnp.where(row >= s, pltpu.roll(y, s, 0), 0.0)

    acc = x
    sel = jnp.zeros_like(x)
    win_lane = jnp.zeros((L, w), F32)
    span = 1
    for gi, win in enumerate(POOL_WINDOWS):
        while span < win:
            acc = acc + shifted(acc, span)
            span *= 2
        sel = jnp.where(group == gi, acc, sel)
        win_lane = jnp.where(group == gi, float(win), win_lane)
    count = jnp.minimum(row.astype(F32) + 1.0, win_lane)
    d = sel / count - x
    y = jnp.dot(d.astype(BF16), w_ref[...], preferred_element_type=F32) * s_ref[...]
    o_ref[0] = y.astype(o_ref.dtype)


def _pool(u, w_bd, scale):
    b, L, w = u.shape
    return pl.pallas_call(
        _pool_kernel,
        grid=(b,),
        in_specs=[pl.BlockSpec((1, L, w), lambda i: (i, 0, 0)),
                  pl.BlockSpec((w, w), lambda i: (0, 0)),
                  pl.BlockSpec((1, w), lambda i: (0, 0))],
        out_specs=pl.BlockSpec((1, L, w), lambda i: (i, 0, 0)),
        out_shape=jax.ShapeDtypeStruct((b, L, w), BF16),
        compiler_params=_params(("arbitrary",)),
        name="pool_mixer",
    )(u, w_bd, scale)


def _group_of(idx, width):
    groups = LANES // SSM_GROUP_DIM
    return (idx >> (width.bit_length() - 1)) & (groups - 1)


def _expand_blockdiag(r_ref, row_inner, col_inner):
    rows, w = r_ref.shape[1:]
    groups = LANES // SSM_GROUP_DIM
    r = lax.broadcasted_iota(jnp.int32, (w, groups * w), 0)
    c = lax.broadcasted_iota(jnp.int32, (w, groups * w), 1)
    shift = col_inner.bit_length() - 1
    outer_c = c >> (shift + groups.bit_length() - 1)
    spread = (outer_c == (r >> shift)) & ((c & (col_inner - 1)) == (r & (col_inner - 1)))
    big = jnp.dot(r_ref[0], spread.astype(BF16), preferred_element_type=F32)
    ri = lax.broadcasted_iota(jnp.int32, (rows, groups * w), 0)
    ci = lax.broadcasted_iota(jnp.int32, (rows, groups * w), 1)
    return jnp.where(_group_of(ri, row_inner) == _group_of(ci, col_inner), big, 0.0).astype(BF16)


def _ssm_kernel(u_ref, rm_ref, rz_ref, ry_ref, coef_ref, y_ref, m_s, wz_s, wy_s, ucat, z_ref, s_ref, *, seqs):
    t = SSM_CHUNK

    @pl.when(pl.program_id(1) == 0)
    def _():
        m_s[...] = _expand_blockdiag(rm_ref, SSM_GROUP_DIM, SSM_GROUP_DIM)
        wz_s[...] = _expand_blockdiag(rz_ref, SSM_GROUP_DIM, SSM_STATE)
        wy_s[...] = _expand_blockdiag(ry_ref, SSM_STATE, SSM_GROUP_DIM)

    nc = u_ref.shape[0] // t
    cps = nc // seqs
    for k in range(t):
        ucat[:, k * LANES:(k + 1) * LANES] = u_ref[pl.ds(k, nc, stride=t), :].astype(BF16)
    u = ucat[...]
    z_ref[...] = jnp.dot(u, wz_s[...], preferred_element_type=F32)
    half = z_ref.shape[1] // 2
    a1 = coef_ref[0, 0:1, :]
    a2 = coef_ref[0, 1:2, :]

    def step(c, states):
        new = []
        for b in range(seqs):
            st = states[b]
            r = b * cps + c
            s_ref[pl.ds(r, 1), :] = st
            sw = jnp.concatenate([st[:, half:], st[:, :half]], axis=1)
            new.append(a1 * st + a2 * sw + z_ref[pl.ds(r, 1), :])
        return tuple(new)

    init = tuple(jnp.zeros((1, 2 * half), F32) for _ in range(seqs))
    lax.fori_loop(0, cps, step, init, unroll=8)
    y = jnp.dot(u, m_s[...], preferred_element_type=F32)
    y = y + jnp.dot(s_ref[...].astype(BF16), wy_s[...], preferred_element_type=F32)
    for k in range(t):
        y_ref[pl.ds(k, nc, stride=t), :] = y[:, k * LANES:(k + 1) * LANES]


def _ssm_core(u, rm, rz, ry, coef, *, seq, seqs_per_block=2):
    n, w = u.shape
    rows = seq * seqs_per_block
    nc = rows // SSM_CHUNK
    kw = SSM_CHUNK * LANES
    p2 = coef.shape[2]
    blk = pl.BlockSpec((rows, LANES), lambda hf, r: (r, hf))
    wspec = lambda arr: pl.BlockSpec((1,) + arr.shape[1:], lambda hf, r: (hf, 0, 0))
    return pl.pallas_call(
        functools.partial(_ssm_kernel, seqs=seqs_per_block),
        grid=(w // LANES, n // rows),
        in_specs=[blk, wspec(rm), wspec(rz), wspec(ry), wspec(coef)],
        out_specs=blk,
        out_shape=jax.ShapeDtypeStruct((n, w), F32),
        scratch_shapes=[pltpu.VMEM((kw, kw), BF16), pltpu.VMEM((kw, p2), BF16), pltpu.VMEM((p2, kw), BF16),
                        pltpu.VMEM((nc, kw), BF16), pltpu.VMEM((nc, p2), F32), pltpu.VMEM((nc, p2), F32)],
        compiler_params=_params(("arbitrary", "arbitrary")),
        name="s5_core",
    )(u, rm, rz, ry, coef)


def _ssm_matrices(a_re, a_im, b_re, b_im, c_re, c_im, d, log_dt):
    t = SSM_CHUNK
    g, p = a_re.shape
    hdim = d.shape[1]
    gl = LANES // hdim
    nb = g // gl
    lam = lax.complex(a_re, a_im)
    lam_dt = lam * jnp.exp(log_dt)[:, None]
    lam_bar = jnp.exp(lam_dt)
    b_bar = ((lam_bar - 1.0) / lam)[:, :, None] * lax.complex(b_re, b_im)
    cmat = lax.complex(c_re, c_im)
    steps = jnp.arange(t + 1, dtype=F32)
    pw = jnp.exp(lam_dt[:, None, :] * steps[None, :, None])
    kern = jnp.einsum('gop,gdp,gpi->gdoi', cmat, pw[:, :t], b_bar).real
    kern = kern.at[:, 0].add(jax.vmap(jnp.diag)(d))
    j = jnp.arange(t)[:, None]
    i = jnp.arange(t)[None, :]
    m = jnp.where((i >= j)[None, :, :, None, None], kern[:, jnp.clip(i - j, 0, t - 1)], 0.0)
    wz = jnp.einsum('gjp,gpi->gjip', pw[:, t - 1 - jnp.arange(t)], b_bar)
    cw = jnp.einsum('gop,gip->giop', cmat, pw[:, 1:t + 1])
    rm = m.reshape(nb, gl, t, t, hdim, hdim).transpose(0, 2, 1, 5, 3, 4).reshape(nb, t * LANES, t * hdim)
    rz = jnp.stack([wz.real, wz.imag], axis=3).reshape(nb, gl, t, hdim, 2, p)
    rz = rz.transpose(0, 2, 1, 3, 4, 5).reshape(nb, t * LANES, 2 * p)
    ry = jnp.stack([cw.real, -cw.imag], axis=1).reshape(nb, gl, 2, t, hdim, p)
    ry = ry.transpose(0, 2, 1, 5, 3, 4).reshape(nb, 2 * gl * p, t * hdim)
    ar = pw[:, t].real.reshape(nb, gl * p)
    ai = pw[:, t].imag.reshape(nb, gl * p)
    coef = jnp.stack([jnp.concatenate([ar, ar], -1), jnp.concatenate([-ai, ai], -1)], axis=1)
    return rm.astype(BF16), rz.astype(BF16), ry.astype(BF16), coef.astype(F32)


def _mixout_kernel(*refs, moe):
    (x_ref, attn_ref, pool_ref, ys_ref, wglu_ref, bglu_ref, ga_ref, gp_ref, gs_ref,
     wo_ref, gpost_ref, gpre_ref) = refs[:12]
    if moe:
        router_ref, xo_ref, h_ref, ri_ref, rg_ref, cnt_ref, cnt_acc = refs[12:]

        @pl.when(pl.program_id(0) == 0)
        def _():
            cnt_acc[...] = jnp.zeros_like(cnt_acc)
    else:
        xo_ref, h_ref = refs[12:]
    a = attn_ref.shape[1]
    pw = pool_ref.shape[1]
    attn_n = _rms(attn_ref[...].astype(F32), ga_ref[...]).astype(BF16)
    pool_n = _rms(pool_ref[...].astype(F32), gp_ref[...]).astype(BF16)
    y = jax.nn.gelu(ys_ref[...])
    gate = jnp.dot(y.astype(BF16), wglu_ref[...], preferred_element_type=F32) + bglu_ref[...]
    ssm = y * jax.nn.sigmoid(gate)
    ssm_n = _rms(ssm, gs_ref[...]).astype(BF16)
    mix = (jnp.dot(attn_n, wo_ref[0:a, :], preferred_element_type=F32)
           + jnp.dot(pool_n, wo_ref[a:a + pw, :], preferred_element_type=F32)
           + jnp.dot(ssm_n, wo_ref[a + pw:, :], preferred_element_type=F32))
    x = x_ref[...] + _rms(mix, gpost_ref[...])
    xo_ref[...] = x
    h = _rms(x, gpre_ref[...])
    h_ref[...] = h.astype(h_ref.dtype)
    if moe:
        hi, mid, _ = _split3(h)
        rhi, rmid, _ = _split3(router_ref[...])
        dot = lambda p, q: jnp.dot(p, q, preferred_element_type=F32)
        logits = dot(hi, rhi) + dot(hi, rmid) + dot(mid, rhi)
        lane = lax.broadcasted_iota(jnp.int32, logits.shape, 1)
        logits = jnp.where(lane < N_EXPERTS, logits, -jnp.inf)
        m1 = jnp.max(logits, axis=1, keepdims=True)
        i1 = jnp.min(jnp.where(logits == m1, lane, LANES), axis=1, keepdims=True)
        rest = jnp.where(lane == i1, -jnp.inf, logits)
        m2 = jnp.max(rest, axis=1, keepdims=True)
        i2 = jnp.min(jnp.where(rest == m2, lane, LANES), axis=1, keepdims=True)
        e2 = jnp.exp(m2 - m1)
        g1 = 1.0 / (1.0 + e2)
        rg_ref[...] = jnp.where(lane == 0, g1, jnp.where(lane == 1, e2 * g1, 0.0))
        tm = logits.shape[0]
        onehot = ((lane == i1) | (lane == i2)).astype(BF16)
        tri = (lax.broadcasted_iota(jnp.int32, (tm, tm), 0)
               > lax.broadcasted_iota(jnp.int32, (tm, tm), 1)).astype(BF16)
        before = jnp.dot(tri, onehot, preferred_element_type=F32) + cnt_acc[...]
        r1 = jnp.sum(jnp.where(lane == i1, before, 0.0), axis=1, keepdims=True).astype(jnp.int32)
        r2 = jnp.sum(jnp.where(lane == i2, before, 0.0), axis=1, keepdims=True).astype(jnp.int32)
        ri = jnp.where(lane == 0, i1, jnp.where(lane == 1, i2,
                       jnp.where(lane == 2, r1, jnp.where(lane == 3, r2, 0))))
        ri_ref[...] = ri.T[:ri_ref.shape[0], :]
        cnt_acc[...] += jnp.sum(onehot.astype(F32), axis=0, keepdims=True)
        cnt_ref[...] = cnt_acc[...]


def _mixout(x2, attn, pool, ys, w_glu, b_glu, g_attn, g_pool, g_ssm, w_out, g_post, g_pre,
            router=None, *, tm=512):
    n, d = x2.shape
    moe = router is not None
    row = lambda w: pl.BlockSpec((tm, w), lambda i: (i, 0))
    full = lambda arr: pl.BlockSpec(arr.shape, lambda i: (0, 0))
    ins = [x2, attn, pool, ys, w_glu, b_glu, g_attn, g_pool, g_ssm, w_out, g_post, g_pre]
    in_specs = [row(d), row(attn.shape[1]), row(pool.shape[1]), row(ys.shape[1])] + [full(t) for t in ins[4:]]
    out_shape = [jax.ShapeDtypeStruct((n, d), F32), jax.ShapeDtypeStruct((n, d), F32 if moe else BF16)]
    out_specs = [row(d), row(d)]
    scratch = []
    if moe:
        ins.append(router)
        in_specs.append(full(router))
        out_shape += [jax.ShapeDtypeStruct((8, n), jnp.int32), jax.ShapeDtypeStruct((n, LANES), F32),
                      jax.ShapeDtypeStruct((1, LANES), F32)]
        out_specs += [pl.BlockSpec((8, tm), lambda i: (0, i)), row(LANES),
                      pl.BlockSpec((1, LANES), lambda i: (0, 0))]
        scratch = [pltpu.VMEM((1, LANES), F32)]
    return pl.pallas_call(
        functools.partial(_mixout_kernel, moe=moe),
        grid=(n // tm,),
        in_specs=in_specs,
        out_specs=tuple(out_specs),
        out_shape=tuple(out_shape),
        scratch_shapes=scratch,
        compiler_params=_params(("arbitrary",)),
        name="mix_out",
    )(*ins)


def _ffn_kernel(h_ref, x_ref, wgu_ref, wd_ref, g_ref, o_ref, acc_ref, *, sub):
    f = pl.program_id(1)

    @pl.when(f == 0)
    def _():
        acc_ref[...] = jnp.zeros_like(acc_ref)

    tf = wd_ref.shape[0]
    for r in range(h_ref.shape[0] // sub):
        rows = pl.ds(r * sub, sub)
        hgu = jnp.dot(h_ref[rows, :], wgu_ref[0], preferred_element_type=F32)
        hg, hu = hgu[:, :tf], hgu[:, tf:]
        act = (hg * jax.nn.sigmoid(hg) * hu).astype(BF16)
        acc_ref[rows, :] += jnp.dot(act, wd_ref[...], preferred_element_type=F32)

    @pl.when(f == pl.num_programs(1) - 1)
    def _():
        o_ref[...] = x_ref[...] + _rms(acc_ref[...], g_ref[...])


def _ffn(h, x2, wg, wu, wd, g_post, *, tm=1024, tf, sub=256):
    n, d = x2.shape
    fdim = wg.shape[1]
    nf = fdim // tf
    wgu = jnp.concatenate([wg.reshape(d, nf, tf), wu.reshape(d, nf, tf)], axis=2).transpose(1, 0, 2)
    row = lambda w: pl.BlockSpec((tm, w), lambda i, f: (i, 0))
    return pl.pallas_call(
        functools.partial(_ffn_kernel, sub=sub),
        grid=(n // tm, nf),
        in_specs=[row(d), row(d),
                  pl.BlockSpec((1, d, 2 * tf), lambda i, f: (f, 0, 0)),
                  pl.BlockSpec((tf, d), lambda i, f: (f, 0)),
                  pl.BlockSpec((1, d), lambda i, f: (0, 0))],
        out_specs=row(d),
        out_shape=jax.ShapeDtypeStruct((n, d), F32),
        scratch_shapes=[pltpu.VMEM((tm, d), F32)],
        compiler_params=_params(("arbitrary", "arbitrary")),
        name="dense_ffn",
    )(h, x2, wgu, wd, g_post)


MOE_TILE = 512


def _moe_tables(counts, n_items):
    tm = MOE_TILE
    counts = counts.astype(jnp.int32)
    ntiles = (counts + tm - 1) // tm
    ends = jnp.cumsum(ntiles)
    starts = ends - ntiles
    w = jnp.arange(n_items, dtype=jnp.int32)
    wc = jnp.minimum(w, ends[-1] - 1)
    e = jnp.sum((wc[:, None] >= ends[None, :]).astype(jnp.int32), axis=1)
    valid = w < ends[-1]
    rowblock = jnp.where(valid, w, n_items)
    nvalid = jnp.where(valid, jnp.clip(counts[e] - (wc - starts[e]) * tm, 0, tm), 0)
    first = ((w == 0) | (e != jnp.roll(e, 1))).astype(jnp.int32)
    return (e, rowblock, nvalid.astype(jnp.int32), first), starts * tm


def _dispatch_kernel(dest_ref, h_ref, xs_ref, sem, *, n):
    tm = h_ref.shape[0]
    base = pl.program_id(0) * tm

    def body(r, carry):
        for k in range(2):
            d = dest_ref[k * n + base + r]
            pltpu.make_async_copy(h_ref.at[pl.ds(r, 1)], xs_ref.at[pl.ds(d, 1)], sem).start()
        return carry

    lax.fori_loop(0, tm, body, 0, unroll=8)
    for k in range(2):
        pltpu.make_async_copy(h_ref, h_ref, sem).wait()


def _dispatch(dest, h, cap_rows, *, tm=512):
    n, d = h.shape
    return pl.pallas_call(
        functools.partial(_dispatch_kernel, n=n),
        grid_spec=pltpu.PrefetchScalarGridSpec(
            num_scalar_prefetch=1,
            grid=(n // tm,),
            in_specs=[pl.BlockSpec((tm, d), lambda i, dest: (i, 0))],
            out_specs=pl.BlockSpec(memory_space=pl.ANY),
            scratch_shapes=[pltpu.SemaphoreType.DMA(())]),
        out_shape=jax.ShapeDtypeStruct((cap_rows, d), F32),
        compiler_params=_params(("arbitrary",)),
        name="moe_dispatch",
    )(dest, h)


MOE_SUB = 256


def _moe_up_kernel(ie_ref, rb_ref, nv_ref, first_ref, xs_ref, wg_ref, wu_ref, h_ref, wgu_s):
    w = pl.program_id(1)
    tf = h_ref.shape[1]

    @pl.when(first_ref[w] == 1)
    def _():
        wgu_s[:, :tf] = wg_ref[0].astype(BF16)
        wgu_s[:, tf:] = wu_ref[0].astype(BF16)

    nv = nv_ref[w]

    @pl.when(nv > 0)
    def _():
        for r in range(xs_ref.shape[0] // MOE_SUB):
            rows = pl.ds(r * MOE_SUB, MOE_SUB)
            idx = r * MOE_SUB + lax.broadcasted_iota(jnp.int32, (MOE_SUB, 1), 0)
            x = jnp.where(idx < nv, xs_ref[rows, :], 0.0).astype(BF16)
            hgu = jnp.dot(x, wgu_s[...], preferred_element_type=F32)
            hg, hu = hgu[:, :tf], hgu[:, tf:]
            h_ref[rows, :] = (hg * jax.nn.sigmoid(hg) * hu).astype(BF16)

    @pl.when(nv == 0)
    def _():
        h_ref[...] = jnp.zeros_like(h_ref)


def _moe_up(tables, xs, wg, wu, *, tf=896):
    tm = MOE_TILE
    cap_rows, d = xs.shape
    ne, _, fdim = wg.shape
    n_items = tables[0].shape[0]
    return pl.pallas_call(
        _moe_up_kernel,
        grid_spec=pltpu.PrefetchScalarGridSpec(
            num_scalar_prefetch=4,
            grid=(fdim // tf, n_items),
            in_specs=[pl.BlockSpec((tm, d), lambda f, w, ie, rb, nv, fi: (rb[w], 0)),
                      pl.BlockSpec((1, d, tf), lambda f, w, ie, rb, nv, fi: (ie[w], 0, f)),
                      pl.BlockSpec((1, d, tf), lambda f, w, ie, rb, nv, fi: (ie[w], 0, f))],
            out_specs=pl.BlockSpec((tm, tf), lambda f, w, ie, rb, nv, fi: (rb[w], f)),
            scratch_shapes=[pltpu.VMEM((d, 2 * tf), BF16)]),
        out_shape=jax.ShapeDtypeStruct((cap_rows, fdim), BF16),
        compiler_params=_params(("arbitrary", "arbitrary")),
        name="moe_up",
    )(*tables, xs, wg, wu)


def _moe_down_kernel(ie_ref, rb_ref, nv_ref, first_ref, h_ref, wd_ref, y_ref, wd_s):
    w = pl.program_id(1)

    @pl.when(first_ref[w] == 1)
    def _():
        wd_s[...] = wd_ref[0].astype(BF16)

    @pl.when(nv_ref[w] > 0)
    def _():
        for r in range(h_ref.shape[0] // MOE_SUB):
            rows = pl.ds(r * MOE_SUB, MOE_SUB)
            y_ref[rows, :] = jnp.dot(h_ref[rows, :], wd_s[...], preferred_element_type=F32)

    @pl.when(nv_ref[w] == 0)
    def _():
        y_ref[...] = jnp.zeros_like(y_ref)


def _moe_down(tables, hid, wd, *, tn=512):
    tm = MOE_TILE
    cap_rows, fdim = hid.shape
    d = wd.shape[2]
    n_items = tables[0].shape[0]
    return pl.pallas_call(
        _moe_down_kernel,
        grid_spec=pltpu.PrefetchScalarGridSpec(
            num_scalar_prefetch=4,
            grid=(d // tn, n_items),
            in_specs=[pl.BlockSpec((tm, fdim), lambda c, w, ie, rb, nv, fi: (rb[w], 0)),
                      pl.BlockSpec((1, fdim, tn), lambda c, w, ie, rb, nv, fi: (ie[w], 0, c))],
            out_specs=pl.BlockSpec((tm, tn), lambda c, w, ie, rb, nv, fi: (rb[w], c)),
            scratch_shapes=[pltpu.VMEM((fdim, tn), BF16)]),
        out_shape=jax.ShapeDtypeStruct((cap_rows, d), F32),
        compiler_params=_params(("arbitrary", "arbitrary")),
        name="moe_down",
    )(*tables, hid, wd)


def _combine_kernel(dest_ref, x_ref, gate_ref, g_ref, ys_ref, o_ref, buf, sem, *, n):
    tm = x_ref.shape[0]
    i = pl.program_id(0)

    def issue(tile, slot):
        def body(r, carry):
            for k in range(2):
                d = dest_ref[k * n + tile * tm + r]
                pltpu.make_async_copy(ys_ref.at[pl.ds(d, 1)], buf.at[slot, k, pl.ds(r, 1)],
                                      sem.at[slot]).start()
            return carry
        lax.fori_loop(0, tm, body, 0, unroll=8)

    @pl.when(i == 0)
    def _():
        issue(0, 0)

    @pl.when(i + 1 < pl.num_programs(0))
    def _():
        issue(i + 1, (i + 1) % 2)

    slot = i % 2
    for k in range(2):
        pltpu.make_async_copy(buf.at[slot, k], buf.at[slot, k], sem.at[slot]).wait()
    gates = gate_ref[...]
    f = gates[:, 0:1] * buf[slot, 0] + gates[:, 1:2] * buf[slot, 1]
    o_ref[...] = x_ref[...] + _rms(f, g_ref[...])


def _combine(dest, x2, gates, g_post, ys, *, tm=256):
    n, d = x2.shape
    return pl.pallas_call(
        functools.partial(_combine_kernel, n=n),
        grid_spec=pltpu.PrefetchScalarGridSpec(
            num_scalar_prefetch=1,
            grid=(n // tm,),
            in_specs=[pl.BlockSpec((tm, d), lambda i, dest: (i, 0)),
                      pl.BlockSpec((tm, LANES), lambda i, dest: (i, 0)),
                      pl.BlockSpec((1, d), lambda i, dest: (0, 0)),
                      pl.BlockSpec(memory_space=pl.ANY)],
            out_specs=pl.BlockSpec((tm, d), lambda i, dest: (i, 0)),
            scratch_shapes=[pltpu.VMEM((2, 2, tm, d), F32), pltpu.SemaphoreType.DMA((2,))]),
        out_shape=jax.ShapeDtypeStruct((n, d), F32),
        compiler_params=_params(("arbitrary",)),
        name="moe_combine",
    )(dest, x2, gates, g_post, ys)


def _moe(h, x2, route_i, route_g, counts, wg, wu, wd, g_post):
    n = h.shape[0]
    n_items = 2 * n // MOE_TILE + N_EXPERTS
    tables, base = _moe_tables(counts[0, :N_EXPERTS], n_items)
    base_of = lambda e: jnp.sum(jnp.where(e[None, :] == jnp.arange(N_EXPERTS)[:, None], base[:, None], 0), axis=0)
    dest = jnp.concatenate([base_of(route_i[0]) + route_i[2], base_of(route_i[1]) + route_i[3]])
    xs = _dispatch(dest, h, (n_items + 1) * MOE_TILE)
    hid = _moe_up(tables, xs, wg, wu)
    ys = _moe_down(tables, hid, wd)
    return _combine(dest, x2, route_g, g_post, ys)


def _row(v):
    return v.reshape(1, -1).astype(F32)


def kernel(x, norm_mix_pre, norm_mix_post, norm_ffn_pre, norm_ffn_post, w_in, b_forget, pool_w, pool_scale, ssm_a_re, ssm_a_im, ssm_b_re, ssm_b_im, ssm_c_re, ssm_c_im, ssm_d, ssm_log_dt, ssm_w_glu, ssm_b_glu, branch_norm_attn, branch_norm_pool, branch_norm_ssm, w_out, ffn_w_gate, ffn_w_up, ffn_w_down, moe_router, moe_w_gate, moe_w_up, moe_w_down):
    b, L, d = x.shape
    depth = w_in.shape[0]
    n = b * L
    a = ATTN_WIDTH
    f_off = 3 * a
    p_off = f_off + ATTN_HEADS
    attn_tile = 512
    x2 = x.reshape(n, d)
    for i in range(depth):
        w = w_in[i]
        w_main = jnp.concatenate([w[:, :f_off], w[:, p_off:]], axis=1).astype(BF16)
        w_f = jnp.pad(w[:, f_off:p_off], ((0, 0), (0, LANES - ATTN_HEADS))).astype(BF16)
        b_f = jnp.pad(b_forget[i], (0, LANES - ATTN_HEADS)).reshape(1, LANES)
        q, k, v, up, us, c = _inproj(x2, _row(norm_mix_pre[i]), w_main, w_f, b_f, seq=L)

        ct = c.reshape(b, L, ATTN_HEADS).transpose(0, 2, 1).reshape(b, ATTN_HEADS * (L // attn_tile), attn_tile)
        attn = _attention(q.reshape(b, L, a), k.reshape(b, L, a), v.reshape(b, L, a),
                          ct, tq=attn_tile).reshape(n, a)

        pool = _pool(up.reshape(b, L, POOL_WIDTH),
                     jax.scipy.linalg.block_diag(*pool_w[i]).astype(BF16),
                     _row(pool_scale[i])).reshape(n, POOL_WIDTH)

        mats = _ssm_matrices(ssm_a_re[i], ssm_a_im[i], ssm_b_re[i], ssm_b_im[i],
                             ssm_c_re[i], ssm_c_im[i], ssm_d[i], ssm_log_dt[i])
        ys = _ssm_core(us, *mats, seq=L)

        moe = i % 2 == 1
        j = i // 2
        router = jnp.pad(moe_router[j], ((0, 0), (0, LANES - N_EXPERTS))) if moe else None
        res = _mixout(x2, attn, pool, ys, ssm_w_glu[i].astype(BF16), _row(ssm_b_glu[i]),
                      _row(branch_norm_attn[i]), _row(branch_norm_pool[i]), _row(branch_norm_ssm[i]),
                      w_out[i].astype(BF16), _row(norm_mix_post[i]), _row(norm_ffn_pre[i]), router)
        if moe:
            x2, h, route_i, route_g, counts = res
            x2 = _moe(h, x2, route_i, route_g, counts, moe_w_gate[j], moe_w_up[j], moe_w_down[j],
                      _row(norm_ffn_post[i]))
        else:
            x2, h = res
            x2 = _ffn(h, x2, ffn_w_gate[j].astype(BF16), ffn_w_up[j].astype(BF16),
                      ffn_w_down[j].astype(BF16), _row(norm_ffn_post[i]), tf=1408)
    return x2.reshape(b, L, d)
```

```python
import functools

import jax
import jax.numpy as jnp
from jax import lax
from jax.experimental import pallas as pl
from jax.experimental.pallas import tpu as pltpu

F32 = jnp.float32
BF16 = jnp.bfloat16

RMS_EPS = 1e-6
NEG_INF = -1e30
LOG2E = 1.4426950408889634

ATTN_HEADS = 8
ATTN_HEAD_DIM = 64
ATTN_WIDTH = ATTN_HEADS * ATTN_HEAD_DIM
POOL_WINDOWS = (2, 4, 8, 16)
POOL_GROUP_DIM = 64
POOL_WIDTH = len(POOL_WINDOWS) * POOL_GROUP_DIM
SSM_GROUPS = 16
SSM_GROUP_DIM = 16
SSM_STATE = 64
SSM_WIDTH = SSM_GROUPS * SSM_GROUP_DIM
SSM_CHUNK = 8
N_EXPERTS = 8

LANES = 128
VMEM_LIMIT = 48 * 1024 * 1024


def _params(sem):
    return pltpu.CompilerParams(dimension_semantics=sem, vmem_limit_bytes=VMEM_LIMIT)


def _rms(x, g):
    return x * lax.rsqrt(jnp.mean(x * x, axis=-1, keepdims=True) + RMS_EPS) * g


def _split3(x):
    hi = x.astype(BF16)
    r = x - hi.astype(F32)
    mid = r.astype(BF16)
    lo = (r - mid.astype(F32)).astype(BF16)
    return hi, mid, lo


def _inproj_kernel(x_ref, g_ref, wm_ref, wf_ref, bf_ref,
                   q_ref, k_ref, v_ref, up_ref, us_ref, c_ref, carry_ref, *, tiles_per_seq):
    @pl.when(pl.program_id(0) % tiles_per_seq == 0)
    def _():
        carry_ref[...] = jnp.zeros_like(carry_ref)

    tm = x_ref.shape[0]
    h = _rms(x_ref[...], g_ref[...]).astype(BF16)
    proj = jnp.dot(h, wm_ref[...], preferred_element_type=F32)
    a = ATTN_WIDTH
    q_ref[...] = (proj[:, 0:a] * (LOG2E * ATTN_HEAD_DIM ** -0.5)).astype(BF16)
    k_ref[...] = proj[:, a:2 * a].astype(BF16)
    v_ref[...] = proj[:, 2 * a:3 * a].astype(BF16)
    up_ref[...] = proj[:, 3 * a:3 * a + POOL_WIDTH].astype(BF16)
    us_ref[...] = proj[:, 3 * a + POOL_WIDTH:]

    z = jnp.dot(h, wf_ref[...], preferred_element_type=F32) + bf_ref[...]
    logf = jnp.minimum(z, 0.0) - jnp.log(1.0 + jnp.exp(-jnp.abs(z)))
    row = lax.broadcasted_iota(jnp.int32, (tm, tm), 0)
    col = lax.broadcasted_iota(jnp.int32, (tm, tm), 1)
    tri = (row >= col).astype(BF16)
    hi, mid, lo = _split3(logf)
    c = (jnp.dot(tri, hi, preferred_element_type=F32)
         + jnp.dot(tri, mid, preferred_element_type=F32)
         + jnp.dot(tri, lo, preferred_element_type=F32)) + carry_ref[...]
    c_ref[...] = c[:, :ATTN_HEADS]
    carry_ref[...] = c[tm - 1:tm, :]


def _inproj(x2, g, w_main, w_f, b_f, *, seq, tm=512):
    n, d = x2.shape
    nm = w_main.shape[1]
    a = ATTN_WIDTH
    out_shape = (
        jax.ShapeDtypeStruct((n, a), BF16), jax.ShapeDtypeStruct((n, a), BF16),
        jax.ShapeDtypeStruct((n, a), BF16), jax.ShapeDtypeStruct((n, POOL_WIDTH), BF16),
        jax.ShapeDtypeStruct((n, SSM_WIDTH), F32), jax.ShapeDtypeStruct((n, ATTN_HEADS), F32))
    row = lambda w: pl.BlockSpec((tm, w), lambda i: (i, 0))
    full = lambda r, c: pl.BlockSpec((r, c), lambda i: (0, 0))
    return pl.pallas_call(
        functools.partial(_inproj_kernel, tiles_per_seq=seq // tm),
        grid=(n // tm,),
        in_specs=[row(d), full(1, d), full(d, nm), full(d, LANES), full(1, LANES)],
        out_specs=(row(a), row(a), row(a), row(POOL_WIDTH), row(SSM_WIDTH), row(ATTN_HEADS)),
        out_shape=out_shape,
        scratch_shapes=[pltpu.VMEM((1, LANES), F32)],
        compiler_params=_params(("arbitrary",)),
        name="inproj",
    )(x2, g, w_main, w_f, b_f)


HEADS_PER_BLOCK = LANES // ATTN_HEAD_DIM


def _attn_kernel(q_ref, k_ref, v_ref, ct_ref, o_ref, *, tq, tk):
    nlb = q_ref.shape[2] // LANES
    hp = pl.program_id(1)
    qi = pl.program_id(2)
    nkb = k_ref.shape[1] // tk
    lane = lax.broadcasted_iota(jnp.int32, (1, LANES), 1)
    in_head = [(lane // ATTN_HEAD_DIM) == hh for hh in range(HEADS_PER_BLOCK)]
    chains = [(lb, hh) for lb in range(nlb) for hh in range(HEADS_PER_BLOCK)]
    qm, crow, cref = [], [], []
    for lb, hh in chains:
        q2 = q_ref[0, :, lb * LANES:(lb + 1) * LANES]
        qm.append(jnp.where(in_head[hh], q2, jnp.zeros_like(q2)))
        crow.append(((hp * nlb + lb) * HEADS_PER_BLOCK + hh) * nkb)
        cref.append(ct_ref[0, pl.ds(crow[-1] + qi, 1), :][:, 0:1])

    def step(kb, carry, masked):
        new = []
        for ci, (lb, hh) in enumerate(chains):
            kblk = k_ref[0, pl.ds(kb * tk, tk), lb * LANES:(lb + 1) * LANES]
            vblk = v_ref[0, pl.ds(kb * tk, tk), lb * LANES:(lb + 1) * LANES]
            m, acc = carry[ci]
            s = lax.dot_general(qm[ci], kblk, (((1,), (1,)), ((), ())), preferred_element_type=F32)
            s = s - (ct_ref[0, pl.ds(crow[ci] + kb, 1), :] - cref[ci]) * LOG2E
            if masked:
                r = lax.broadcasted_iota(jnp.int32, (tq, tk), 0)
                cidx = lax.broadcasted_iota(jnp.int32, (tq, tk), 1)
                s = jnp.where(cidx <= r, s, NEG_INF)
            m_new = jnp.maximum(m, jnp.max(s, axis=1, keepdims=True))
            p = jnp.exp2(s - m_new)
            vsel = jnp.where(in_head[hh], vblk, jnp.ones_like(vblk))
            acc = jnp.exp2(m - m_new) * acc + jnp.dot(p.astype(BF16), vsel, preferred_element_type=F32)
            new.append((m_new, acc))
        return tuple(new)

    init = tuple((jnp.full((tq, 1), NEG_INF, F32), jnp.zeros((tq, LANES), F32)) for _ in chains)
    carry = lax.fori_loop(0, qi, lambda kb, cr: step(kb, cr, False), init)
    carry = step(qi, carry, True)
    for lb in range(nlb):
        out = jnp.zeros((tq, LANES), F32)
        for hh in range(HEADS_PER_BLOCK):
            acc = carry[lb * HEADS_PER_BLOCK + hh][1]
            out = jnp.where(in_head[hh], acc / pltpu.roll(acc, ATTN_HEAD_DIM, 1), out)
        o_ref[0, :, lb * LANES:(lb + 1) * LANES] = out.astype(o_ref.dtype)


def _attention(q, k, v, ct, *, tq, lane_blocks=2):
    b, L, a = q.shape
    tk = ct.shape[2]
    assert tq == tk, "the diagonal key block of query tile i must be key block i"
    w = lane_blocks * LANES
    blk = lambda: pl.BlockSpec((1, tq, w), lambda bi, hi, qi: (bi, qi, hi))
    seq = lambda: pl.BlockSpec((1, L, w), lambda bi, hi, qi: (bi, 0, hi))
    return pl.pallas_call(
        functools.partial(_attn_kernel, tq=tq, tk=tk),
        grid=(b, a // w, L // tq),
        in_specs=[blk(), seq(), seq(),
                  pl.BlockSpec((1, ct.shape[1], tk), lambda bi, hi, qi: (bi, 0, 0))],
        out_specs=blk(),
        out_shape=jax.ShapeDtypeStruct((b, L, a), BF16),
        compiler_params=_params(("arbitrary", "arbitrary", "arbitrary")),
        name="fox_attention",
    )(q, k, v, ct)


def _pool_kernel(u_ref, w_ref, s_ref, o_ref):
    x = u_ref[0].astype(F32)
    L, w = x.shape
    row = lax.broadcasted_iota(jnp.int32, (L, w), 0)
    group = lax.broadcasted_iota(jnp.int32, (L, w), 1) // POOL_GROUP_DIM

    def shifted(y, s):
        return jnp.where(row >= s, pltpu.roll(y, s, 0), 0.0)

    acc = x
    sel = jnp.zeros_like(x)
    win_lane = jnp.zeros((L, w), F32)
    span = 1
    for gi, win in enumerate(POOL_WINDOWS):
        while span < win:
            acc = acc + shifted(acc, span)
            span *= 2
        sel = jnp.where(group == gi, acc, sel)
        win_lane = jnp.where(group == gi, float(win), win_lane)
    count = jnp.minimum(row.astype(F32) + 1.0, win_lane)
    d = sel / count - x
    y = jnp.dot(d.astype(BF16), w_ref[...], preferred_element_type=F32) * s_ref[...]
    o_ref[0] = y.astype(o_ref.dtype)


def _pool(u, w_bd, scale):
    b, L, w = u.shape
    return pl.pallas_call(
        _pool_kernel,
        grid=(b,),
        in_specs=[pl.BlockSpec((1, L, w), lambda i: (i, 0, 0)),
                  pl.BlockSpec((w, w), lambda i: (0, 0)),
                  pl.BlockSpec((1, w), lambda i: (0, 0))],
        out_specs=pl.BlockSpec((1, L, w), lambda i: (i, 0, 0)),
        out_shape=jax.ShapeDtypeStruct((b, L, w), BF16),
        compiler_params=_params(("arbitrary",)),
        name="pool_mixer",
    )(u, w_bd, scale)


def _group_of(idx, width):
    groups = LANES // SSM_GROUP_DIM
    return (idx >> (width.bit_length() - 1)) & (groups - 1)


def _expand_blockdiag(r_ref, row_inner, col_inner):
    rows, w = r_ref.shape[1:]
    groups = LANES // SSM_GROUP_DIM
    r = lax.broadcasted_iota(jnp.int32, (w, groups * w), 0)
    c = lax.broadcasted_iota(jnp.int32, (w, groups * w), 1)
    shift = col_inner.bit_length() - 1
    outer_c = c >> (shift + groups.bit_length() - 1)
    spread = (outer_c == (r >> shift)) & ((c & (col_inner - 1)) == (r & (col_inner - 1)))
    big = jnp.dot(r_ref[0], spread.astype(BF16), preferred_element_type=F32)
    ri = lax.broadcasted_iota(jnp.int32, (rows, groups * w), 0)
    ci = lax.broadcasted_iota(jnp.int32, (rows, groups * w), 1)
    return jnp.where(_group_of(ri, row_inner) == _group_of(ci, col_inner), big, 0.0).astype(BF16)


def _ssm_kernel(u_ref, rm_ref, rz_ref, ry_ref, coef_ref, y_ref, m_s, wz_s, wy_s, ucat, z_ref, s_ref, *, seqs):
    t = SSM_CHUNK

    @pl.when(pl.program_id(1) == 0)
    def _():
        m_s[...] = _expand_blockdiag(rm_ref, SSM_GROUP_DIM, SSM_GROUP_DIM)
        wz_s[...] = _expand_blockdiag(rz_ref, SSM_GROUP_DIM, SSM_STATE)
        wy_s[...] = _expand_blockdiag(ry_ref, SSM_STATE, SSM_GROUP_DIM)

    nc = u_ref.shape[0] // t
    cps = nc // seqs
    for k in range(t):
        ucat[:, k * LANES:(k + 1) * LANES] = u_ref[pl.ds(k, nc, stride=t), :].astype(BF16)
    u = ucat[...]
    z_ref[...] = jnp.dot(u, wz_s[...], preferred_element_type=F32)
    half = z_ref.shape[1] // 2
    a1 = coef_ref[0, 0:1, :]
    a2 = coef_ref[0, 1:2, :]

    def step(c, states):
        new = []
        for b in range(seqs):
            st = states[b]
            r = b * cps + c
            s_ref[pl.ds(r, 1), :] = st
            sw = jnp.concatenate([st[:, half:], st[:, :half]], axis=1)
            new.append(a1 * st + a2 * sw + z_ref[pl.ds(r, 1), :])
        return tuple(new)

    init = tuple(jnp.zeros((1, 2 * half), F32) for _ in range(seqs))
    lax.fori_loop(0, cps, step, init, unroll=8)
    y = jnp.dot(u, m_s[...], preferred_element_type=F32)
    y = y + jnp.dot(s_ref[...].astype(BF16), wy_s[...], preferred_element_type=F32)
    for k in range(t):
        y_ref[pl.ds(k, nc, stride=t), :] = y[:, k * LANES:(k + 1) * LANES]


def _ssm_core(u, rm, rz, ry, coef, *, seq, seqs_per_block=2):
    n, w = u.shape
    rows = seq * seqs_per_block
    nc = rows // SSM_CHUNK
    kw = SSM_CHUNK * LANES
    p2 = coef.shape[2]
    blk = pl.BlockSpec((rows, LANES), lambda hf, r: (r, hf))
    wspec = lambda arr: pl.BlockSpec((1,) + arr.shape[1:], lambda hf, r: (hf, 0, 0))
    return pl.pallas_call(
        functools.partial(_ssm_kernel, seqs=seqs_per_block),
        grid=(w // LANES, n // rows),
        in_specs=[blk, wspec(rm), wspec(rz), wspec(ry), wspec(coef)],
        out_specs=blk,
        out_shape=jax.ShapeDtypeStruct((n, w), F32),
        scratch_shapes=[pltpu.VMEM((kw, kw), BF16), pltpu.VMEM((kw, p2), BF16), pltpu.VMEM((p2, kw), BF16),
                        pltpu.VMEM((nc, kw), BF16), pltpu.VMEM((nc, p2), F32), pltpu.VMEM((nc, p2), F32)],
        compiler_params=_params(("arbitrary", "arbitrary")),
        name="s5_core",
    )(u, rm, rz, ry, coef)


def _ssm_matrices(a_re, a_im, b_re, b_im, c_re, c_im, d, log_dt):
    t = SSM_CHUNK
    g, p = a_re.shape
    hdim = d.shape[1]
    gl = LANES // hdim
    nb = g // gl
    lam = lax.complex(a_re, a_im)
    lam_dt = lam * jnp.exp(log_dt)[:, None]
    lam_bar = jnp.exp(lam_dt)
    b_bar = ((lam_bar - 1.0) / lam)[:, :, None] * lax.complex(b_re, b_im)
    cmat = lax.complex(c_re, c_im)
    steps = jnp.arange(t + 1, dtype=F32)
    pw = jnp.exp(lam_dt[:, None, :] * steps[None, :, None])
    kern = jnp.einsum('gop,gdp,gpi->gdoi', cmat, pw[:, :t], b_bar).real
    kern = kern.at[:, 0].add(jax.vmap(jnp.diag)(d))
    j = jnp.arange(t)[:, None]
    i = jnp.arange(t)[None, :]
    m = jnp.where((i >= j)[None, :, :, None, None], kern[:, jnp.clip(i - j, 0, t - 1)], 0.0)
    wz = jnp.einsum('gjp,gpi->gjip', pw[:, t - 1 - jnp.arange(t)], b_bar)
    cw = jnp.einsum('gop,gip->giop', cmat, pw[:, 1:t + 1])
    rm = m.reshape(nb, gl, t, t, hdim, hdim).transpose(0, 2, 1, 5, 3, 4).reshape(nb, t * LANES, t * hdim)
    rz = jnp.stack([wz.real, wz.imag], axis=3).reshape(nb, gl, t, hdim, 2, p)
    rz = rz.transpose(0, 2, 1, 3, 4, 5).reshape(nb, t * LANES, 2 * p)
    ry = jnp.stack([cw.real, -cw.imag], axis=1).reshape(nb, gl, 2, t, hdim, p)
    ry = ry.transpose(0, 2, 1, 5, 3, 4).reshape(nb, 2 * gl * p, t * hdim)
    ar = pw[:, t].real.reshape(nb, gl * p)
    ai = pw[:, t].imag.reshape(nb, gl * p)
    coef = jnp.stack([jnp.concatenate([ar, ar], -1), jnp.concatenate([-ai, ai], -1)], axis=1)
    return rm.astype(BF16), rz.astype(BF16), ry.astype(BF16), coef.astype(F32)


def _mixout_kernel(*refs, moe):
    (x_ref, attn_ref, pool_ref, ys_ref, wglu_ref, bglu_ref, ga_ref, gp_ref, gs_ref,
     wo_ref, gpost_ref, gpre_ref) = refs[:12]
    if moe:
        router_ref, xo_ref, h_ref, ri_ref, rg_ref, cnt_ref, cnt_acc = refs[12:]

        @pl.when(pl.program_id(0) == 0)
        def _():
            cnt_acc[...] = jnp.zeros_like(cnt_acc)
    else:
        xo_ref, h_ref = refs[12:]
    a = attn_ref.shape[1]
    pw = pool_ref.shape[1]
    attn_n = _rms(attn_ref[...].astype(F32), ga_ref[...]).astype(BF16)
    pool_n = _rms(pool_ref[...].astype(F32), gp_ref[...]).astype(BF16)
    y = jax.nn.gelu(ys_ref[...])
    gate = jnp.dot(y.astype(BF16), wglu_ref[...], preferred_element_type=F32) + bglu_ref[...]
    ssm = y * jax.nn.sigmoid(gate)
    ssm_n = _rms(ssm, gs_ref[...]).astype(BF16)
    mix = (jnp.dot(attn_n, wo_ref[0:a, :], preferred_element_type=F32)
           + jnp.dot(pool_n, wo_ref[a:a + pw, :], preferred_element_type=F32)
           + jnp.dot(ssm_n, wo_ref[a + pw:, :], preferred_element_type=F32))
    x = x_ref[...] + _rms(mix, gpost_ref[...])
    xo_ref[...] = x
    h = _rms(x, gpre_ref[...])
    h_ref[...] = h.astype(h_ref.dtype)
    if moe:
        hi, mid, _ = _split3(h)
        rhi, rmid, _ = _split3(router_ref[...])
        dot = lambda p, q: jnp.dot(p, q, preferred_element_type=F32)
        logits = dot(hi, rhi) + dot(hi, rmid) + dot(mid, rhi)
        lane = lax.broadcasted_iota(jnp.int32, logits.shape, 1)
        logits = jnp.where(lane < N_EXPERTS, logits, -jnp.inf)
        m1 = jnp.max(logits, axis=1, keepdims=True)
        i1 = jnp.min(jnp.where(logits == m1, lane, LANES), axis=1, keepdims=True)
        rest = jnp.where(lane == i1, -jnp.inf, logits)
        m2 = jnp.max(rest, axis=1, keepdims=True)
        i2 = jnp.min(jnp.where(rest == m2, lane, LANES), axis=1, keepdims=True)
        e2 = jnp.exp(m2 - m1)
        g1 = 1.0 / (1.0 + e2)
        rg_ref[...] = jnp.where(lane == 0, g1, jnp.where(lane == 1, e2 * g1, 0.0))
        tm = logits.shape[0]
        onehot = ((lane == i1) | (lane == i2)).astype(BF16)
        tri = (lax.broadcasted_iota(jnp.int32, (tm, tm), 0)
               > lax.broadcasted_iota(jnp.int32, (tm, tm), 1)).astype(BF16)
        before = jnp.dot(tri, onehot, preferred_element_type=F32) + cnt_acc[...]
        r1 = jnp.sum(jnp.where(lane == i1, before, 0.0), axis=1, keepdims=True).astype(jnp.int32)
        r2 = jnp.sum(jnp.where(lane == i2, before, 0.0), axis=1, keepdims=True).astype(jnp.int32)
        ri = jnp.where(lane == 0, i1, jnp.where(lane == 1, i2,
                       jnp.where(lane == 2, r1, jnp.where(lane == 3, r2, 0))))
        ri_ref[...] = ri.T[:ri_ref.shape[0], :]
        cnt_acc[...] += jnp.sum(onehot.astype(F32), axis=0, keepdims=True)
        cnt_ref[...] = cnt_acc[...]


def _mixout(x2, attn, pool, ys, w_glu, b_glu, g_attn, g_pool, g_ssm, w_out, g_post, g_pre,
            router=None, *, tm=512):
    n, d = x2.shape
    moe = router is not None
    row = lambda w: pl.BlockSpec((tm, w), lambda i: (i, 0))
    full = lambda arr: pl.BlockSpec(arr.shape, lambda i: (0, 0))
    ins = [x2, attn, pool, ys, w_glu, b_glu, g_attn, g_pool, g_ssm, w_out, g_post, g_pre]
    in_specs = [row(d), row(attn.shape[1]), row(pool.shape[1]), row(ys.shape[1])] + [full(t) for t in ins[4:]]
    out_shape = [jax.ShapeDtypeStruct((n, d), F32), jax.ShapeDtypeStruct((n, d), F32 if moe else BF16)]
    out_specs = [row(d), row(d)]
    scratch = []
    if moe:
        ins.append(router)
        in_specs.append(full(router))
        out_shape += [jax.ShapeDtypeStruct((8, n), jnp.int32), jax.ShapeDtypeStruct((n, LANES), F32),
                      jax.ShapeDtypeStruct((1, LANES), F32)]
        out_specs += [pl.BlockSpec((8, tm), lambda i: (0, i)), row(LANES),
                      pl.BlockSpec((1, LANES), lambda i: (0, 0))]
        scratch = [pltpu.VMEM((1, LANES), F32)]
    return pl.pallas_call(
        functools.partial(_mixout_kernel, moe=moe),
        grid=(n // tm,),
        in_specs=in_specs,
        out_specs=tuple(out_specs),
        out_shape=tuple(out_shape),
        scratch_shapes=scratch,
        compiler_params=_params(("arbitrary",)),
        name="mix_out",
    )(*ins)


def _ffn_kernel(h_ref, x_ref, wgu_ref, wd_ref, g_ref, o_ref, acc_ref, *, sub):
    f = pl.program_id(1)

    @pl.when(f == 0)
    def _():
        acc_ref[...] = jnp.zeros_like(acc_ref)

    tf = wd_ref.shape[0]
    for r in range(h_ref.shape[0] // sub):
        rows = pl.ds(r * sub, sub)
        hgu = jnp.dot(h_ref[rows, :], wgu_ref[0], preferred_element_type=F32)
        hg, hu = hgu[:, :tf], hgu[:, tf:]
        act = (hg * jax.nn.sigmoid(hg) * hu).astype(BF16)
        acc_ref[rows, :] += jnp.dot(act, wd_ref[...], preferred_element_type=F32)

    @pl.when(f == pl.num_programs(1) - 1)
    def _():
        o_ref[...] = x_ref[...] + _rms(acc_ref[...], g_ref[...])


def _ffn(h, x2, wg, wu, wd, g_post, *, tm=1024, tf, sub=256):
    n, d = x2.shape
    fdim = wg.shape[1]
    nf = fdim // tf
    wgu = jnp.concatenate([wg.reshape(d, nf, tf), wu.reshape(d, nf, tf)], axis=2).transpose(1, 0, 2)
    row = lambda w: pl.BlockSpec((tm, w), lambda i, f: (i, 0))
    return pl.pallas_call(
        functools.partial(_ffn_kernel, sub=sub),
        grid=(n // tm, nf),
        in_specs=[row(d), row(d),
                  pl.BlockSpec((1, d, 2 * tf), lambda i, f: (f, 0, 0)),
                  pl.BlockSpec((tf, d), lambda i, f: (f, 0)),
                  pl.BlockSpec((1, d), lambda i, f: (0, 0))],
        out_specs=row(d),
        out_shape=jax.ShapeDtypeStruct((n, d), F32),
        scratch_shapes=[pltpu.VMEM((tm, d), F32)],
        compiler_params=_params(("arbitrary", "arbitrary")),
        name="dense_ffn",
    )(h, x2, wgu, wd, g_post)


MOE_TILE = 512


def _moe_tables(counts, n_items):
    tm = MOE_TILE
    counts = counts.astype(jnp.int32)
    ntiles = (counts + tm - 1) // tm
    ends = jnp.cumsum(ntiles)
    starts = ends - ntiles
    w = jnp.arange(n_items, dtype=jnp.int32)
    wc = jnp.minimum(w, ends[-1] - 1)
    e = jnp.sum((wc[:, None] >= ends[None, :]).astype(jnp.int32), axis=1)
    valid = w < ends[-1]
    rowblock = jnp.where(valid, w, n_items)
    nvalid = jnp.where(valid, jnp.clip(counts[e] - (wc - starts[e]) * tm, 0, tm), 0)
    first = ((w == 0) | (e != jnp.roll(e, 1))).astype(jnp.int32)
    return (e, rowblock, nvalid.astype(jnp.int32), first), starts * tm


def _dispatch_kernel(dest_ref, h_ref, xs_ref, sem, *, n):
    tm = h_ref.shape[0]
    base = pl.program_id(0) * tm

    def body(r, carry):
        for k in range(2):
            d = dest_ref[k * n + base + r]
            pltpu.make_async_copy(h_ref.at[pl.ds(r, 1)], xs_ref.at[pl.ds(d, 1)], sem).start(priority=k)
        return carry

    lax.fori_loop(0, tm, body, 0, unroll=8)
    for k in range(2):
        pltpu.make_async_copy(h_ref, h_ref, sem).wait()


def _dispatch(dest, h, cap_rows, *, tm=512):
    n, d = h.shape
    return pl.pallas_call(
        functools.partial(_dispatch_kernel, n=n),
        grid_spec=pltpu.PrefetchScalarGridSpec(
            num_scalar_prefetch=1,
            grid=(n // tm,),
            in_specs=[pl.BlockSpec((tm, d), lambda i, dest: (i, 0))],
            out_specs=pl.BlockSpec(memory_space=pl.ANY),
            scratch_shapes=[pltpu.SemaphoreType.DMA(())]),
        out_shape=jax.ShapeDtypeStruct((cap_rows, d), F32),
        compiler_params=_params(("arbitrary",)),
        name="moe_dispatch",
    )(dest, h)


MOE_SUB = 256


def _moe_up_kernel(ie_ref, rb_ref, nv_ref, first_ref, xs_ref, wg_ref, wu_ref, h_ref, wgu_s):
    w = pl.program_id(1)
    tf = h_ref.shape[1]

    @pl.when(first_ref[w] == 1)
    def _():
        wgu_s[:, :tf] = wg_ref[0].astype(BF16)
        wgu_s[:, tf:] = wu_ref[0].astype(BF16)

    nv = nv_ref[w]

    @pl.when(nv > 0)
    def _():
        for r in range(xs_ref.shape[0] // MOE_SUB):
            rows = pl.ds(r * MOE_SUB, MOE_SUB)
            idx = r * MOE_SUB + lax.broadcasted_iota(jnp.int32, (MOE_SUB, 1), 0)
            x = jnp.where(idx < nv, xs_ref[rows, :], 0.0).astype(BF16)
            hgu = jnp.dot(x, wgu_s[...], preferred_element_type=F32)
            hg, hu = hgu[:, :tf], hgu[:, tf:]
            h_ref[rows, :] = (hg * jax.nn.sigmoid(hg) * hu).astype(BF16)

    @pl.when(nv == 0)
    def _():
        h_ref[...] = jnp.zeros_like(h_ref)


def _moe_up(tables, xs, wg, wu, *, tf=896):
    tm = MOE_TILE
    cap_rows, d = xs.shape
    ne, _, fdim = wg.shape
    n_items = tables[0].shape[0]
    return pl.pallas_call(
        _moe_up_kernel,
        grid_spec=pltpu.PrefetchScalarGridSpec(
            num_scalar_prefetch=4,
            grid=(fdim // tf, n_items),
            in_specs=[pl.BlockSpec((tm, d), lambda f, w, ie, rb, nv, fi: (rb[w], 0)),
                      pl.BlockSpec((1, d, tf), lambda f, w, ie, rb, nv, fi: (ie[w], 0, f)),
                      pl.BlockSpec((1, d, tf), lambda f, w, ie, rb, nv, fi: (ie[w], 0, f))],
            out_specs=pl.BlockSpec((tm, tf), lambda f, w, ie, rb, nv, fi: (rb[w], f)),
            scratch_shapes=[pltpu.VMEM((d, 2 * tf), BF16)]),
        out_shape=jax.ShapeDtypeStruct((cap_rows, fdim), BF16),
        compiler_params=_params(("arbitrary", "arbitrary")),
        name="moe_up",
    )(*tables, xs, wg, wu)


def _moe_down_kernel(ie_ref, rb_ref, nv_ref, first_ref, h_ref, wd_ref, y_ref, wd_s):
    w = pl.program_id(1)

    @pl.when(first_ref[w] == 1)
    def _():
        wd_s[...] = wd_ref[0].astype(BF16)

    @pl.when(nv_ref[w] > 0)
    def _():
        for r in range(h_ref.shape[0] // MOE_SUB):
            rows = pl.ds(r * MOE_SUB, MOE_SUB)
            y_ref[rows, :] = jnp.dot(h_ref[rows, :], wd_s[...], preferred_element_type=F32)

    @pl.when(nv_ref[w] == 0)
    def _():
        y_ref[...] = jnp.zeros_like(y_ref)


def _moe_down(tables, hid, wd, *, tn=512):
    tm = MOE_TILE
    cap_rows, fdim = hid.shape
    d = wd.shape[2]
    n_items = tables[0].shape[0]
    return pl.pallas_call(
        _moe_down_kernel,
        grid_spec=pltpu.PrefetchScalarGridSpec(
            num_scalar_prefetch=4,
            grid=(d // tn, n_items),
            in_specs=[pl.BlockSpec((tm, fdim), lambda c, w, ie, rb, nv, fi: (rb[w], 0)),
                      pl.BlockSpec((1, fdim, tn), lambda c, w, ie, rb, nv, fi: (ie[w], 0, c))],
            out_specs=pl.BlockSpec((tm, tn), lambda c, w, ie, rb, nv, fi: (rb[w], c)),
            scratch_shapes=[pltpu.VMEM((fdim, tn), BF16)]),
        out_shape=jax.ShapeDtypeStruct((cap_rows, d), F32),
        compiler_params=_params(("arbitrary", "arbitrary")),
        name="moe_down",
    )(*tables, hid, wd)


def _combine_kernel(dest_ref, x_ref, gate_ref, g_ref, ys_ref, o_ref, buf, sem, *, n):
    tm = x_ref.shape[0]
    i = pl.program_id(0)

    def issue(tile, slot):
        def body(r, carry):
            for k in range(2):
                d = dest_ref[k * n + tile * tm + r]
                pltpu.make_async_copy(ys_ref.at[pl.ds(d, 1)], buf.at[slot, k, pl.ds(r, 1)],
                                      sem.at[slot]).start(priority=k)
            return carry
        lax.fori_loop(0, tm, body, 0, unroll=8)

    @pl.when(i == 0)
    def _():
        issue(0, 0)

    @pl.when(i + 1 < pl.num_programs(0))
    def _():
        issue(i + 1, (i + 1) % 2)

    slot = i % 2
    for k in range(2):
        pltpu.make_async_copy(buf.at[slot, k], buf.at[slot, k], sem.at[slot]).wait()
    gates = gate_ref[...]
    f = gates[:, 0:1] * buf[slot, 0] + gates[:, 1:2] * buf[slot, 1]
    o_ref[...] = x_ref[...] + _rms(f, g_ref[...])


def _combine(dest, x2, gates, g_post, ys, *, tm=256):
    n, d = x2.shape
    return pl.pallas_call(
        functools.partial(_combine_kernel, n=n),
        grid_spec=pltpu.PrefetchScalarGridSpec(
            num_scalar_prefetch=1,
            grid=(n // tm,),
            in_specs=[pl.BlockSpec((tm, d), lambda i, dest: (i, 0)),
                      pl.BlockSpec((tm, LANES), lambda i, dest: (i, 0)),
                      pl.BlockSpec((1, d), lambda i, dest: (0, 0)),
                      pl.BlockSpec(memory_space=pl.ANY)],
            out_specs=pl.BlockSpec((tm, d), lambda i, dest: (i, 0)),
            scratch_shapes=[pltpu.VMEM((2, 2, tm, d), F32), pltpu.SemaphoreType.DMA((2,))]),
        out_shape=jax.ShapeDtypeStruct((n, d), F32),
        compiler_params=_params(("arbitrary",)),
        name="moe_combine",
    )(dest, x2, gates, g_post, ys)


def _moe(h, x2, route_i, route_g, counts, wg, wu, wd, g_post):
    n = h.shape[0]
    n_items = 2 * n // MOE_TILE + N_EXPERTS
    tables, base = _moe_tables(counts[0, :N_EXPERTS], n_items)
    base_of = lambda e: jnp.sum(jnp.where(e[None, :] == jnp.arange(N_EXPERTS)[:, None], base[:, None], 0), axis=0)
    dest = jnp.concatenate([base_of(route_i[0]) + route_i[2], base_of(route_i[1]) + route_i[3]])
    xs = _dispatch(dest, h, (n_items + 1) * MOE_TILE)
    hid = _moe_up(tables, xs, wg, wu)
    ys = _moe_down(tables, hid, wd)
    return _combine(dest, x2, route_g, g_post, ys)


def _row(v):
    return v.reshape(1, -1).astype(F32)


def kernel(x, norm_mix_pre, norm_mix_post, norm_ffn_pre, norm_ffn_post, w_in, b_forget, pool_w, pool_scale, ssm_a_re, ssm_a_im, ssm_b_re, ssm_b_im, ssm_c_re, ssm_c_im, ssm_d, ssm_log_dt, ssm_w_glu, ssm_b_glu, branch_norm_attn, branch_norm_pool, branch_norm_ssm, w_out, ffn_w_gate, ffn_w_up, ffn_w_down, moe_router, moe_w_gate, moe_w_up, moe_w_down):
    b, L, d = x.shape
    depth = w_in.shape[0]
    n = b * L
    a = ATTN_WIDTH
    f_off = 3 * a
    p_off = f_off + ATTN_HEADS
    attn_tile = 512
    x2 = x.reshape(n, d)
    for i in range(depth):
        w = w_in[i]
        w_main = jnp.concatenate([w[:, :f_off], w[:, p_off:]], axis=1).astype(BF16)
        w_f = jnp.pad(w[:, f_off:p_off], ((0, 0), (0, LANES - ATTN_HEADS))).astype(BF16)
        b_f = jnp.pad(b_forget[i], (0, LANES - ATTN_HEADS)).reshape(1, LANES)
        q, k, v, up, us, c = _inproj(x2, _row(norm_mix_pre[i]), w_main, w_f, b_f, seq=L)

        ct = c.reshape(b, L, ATTN_HEADS).transpose(0, 2, 1).reshape(b, ATTN_HEADS * (L // attn_tile), attn_tile)
        attn = _attention(q.reshape(b, L, a), k.reshape(b, L, a), v.reshape(b, L, a),
                          ct, tq=attn_tile).reshape(n, a)

        pool = _pool(up.reshape(b, L, POOL_WIDTH),
                     jax.scipy.linalg.block_diag(*pool_w[i]).astype(BF16),
                     _row(pool_scale[i])).reshape(n, POOL_WIDTH)

        mats = _ssm_matrices(ssm_a_re[i], ssm_a_im[i], ssm_b_re[i], ssm_b_im[i],
                             ssm_c_re[i], ssm_c_im[i], ssm_d[i], ssm_log_dt[i])
        ys = _ssm_core(us, *mats, seq=L)

        moe = i % 2 == 1
        j = i // 2
        router = jnp.pad(moe_router[j], ((0, 0), (0, LANES - N_EXPERTS))) if moe else None
        res = _mixout(x2, attn, pool, ys, ssm_w_glu[i].astype(BF16), _row(ssm_b_glu[i]),
                      _row(branch_norm_attn[i]), _row(branch_norm_pool[i]), _row(branch_norm_ssm[i]),
                      w_out[i].astype(BF16), _row(norm_mix_post[i]), _row(norm_ffn_pre[i]), router)
        if moe:
            x2, h, route_i, route_g, counts = res
            x2 = _moe(h, x2, route_i, route_g, counts, moe_w_gate[j], moe_w_up[j], moe_w_down[j],
                      _row(norm_ffn_post[i]))
        else:
            x2, h = res
            x2 = _ffn(h, x2, ffn_w_gate[j].astype(BF16), ffn_w_up[j].astype(BF16),
                      ffn_w_down[j].astype(BF16), _row(norm_ffn_post[i]), tf=1408)
    return x2.reshape(b, L, d)
```

```python
import functools

import jax
import jax.numpy as jnp
from jax import lax
from jax.experimental import pallas as pl
from jax.experimental.pallas import tpu as pltpu

F32 = jnp.float32
BF16 = jnp.bfloat16

RMS_EPS = 1e-6
NEG_INF = -1e30
LOG2E = 1.4426950408889634

ATTN_HEADS = 8
ATTN_HEAD_DIM = 64
ATTN_WIDTH = ATTN_HEADS * ATTN_HEAD_DIM
POOL_WINDOWS = (2, 4, 8, 16)
POOL_GROUP_DIM = 64
POOL_WIDTH = len(POOL_WINDOWS) * POOL_GROUP_DIM
SSM_GROUPS = 16
SSM_GROUP_DIM = 16
SSM_STATE = 64
SSM_WIDTH = SSM_GROUPS * SSM_GROUP_DIM
SSM_CHUNK = 8
N_EXPERTS = 8

LANES = 128
SUBLANES = 8
VMEM_LIMIT = 48 * 1024 * 1024


def _params(sem):
    return pltpu.CompilerParams(dimension_semantics=sem, vmem_limit_bytes=VMEM_LIMIT)


def _rms(x, g):
    return x * lax.rsqrt(jnp.mean(x * x, axis=-1, keepdims=True) + RMS_EPS) * g


def _load_slabs(ref, row0, rows):
    return jnp.concatenate([ref[pl.ds(row0 * SUBLANES + s, rows, stride=SUBLANES), :]
                            for s in range(SUBLANES)], axis=1)


def _store_slabs(ref, row0, val):
    for s in range(SUBLANES):
        ref[pl.ds(row0 * SUBLANES + s, val.shape[0], stride=SUBLANES), :] = val[:, s * LANES:(s + 1) * LANES]


def _split3(x):
    hi = x.astype(BF16)
    r = x - hi.astype(F32)
    mid = r.astype(BF16)
    lo = (r - mid.astype(F32)).astype(BF16)
    return hi, mid, lo


def _inproj_kernel(x_ref, g_ref, wm_ref, wf_ref, bf_ref,
                   q_ref, k_ref, v_ref, up_ref, us_ref, c_ref, carry_ref, *, tiles_per_seq):
    @pl.when(pl.program_id(0) % tiles_per_seq == 0)
    def _():
        carry_ref[...] = jnp.zeros_like(carry_ref)

    tm = x_ref.shape[0]
    h = _rms(x_ref[...], g_ref[...]).astype(BF16)
    proj = jnp.dot(h, wm_ref[...], preferred_element_type=F32)
    a = ATTN_WIDTH
    q_ref[...] = (proj[:, 0:a] * (LOG2E * ATTN_HEAD_DIM ** -0.5)).astype(BF16)
    k_ref[...] = proj[:, a:2 * a].astype(BF16)
    v_ref[...] = proj[:, 2 * a:3 * a].astype(BF16)
    up_ref[...] = proj[:, 3 * a:3 * a + POOL_WIDTH].astype(BF16)
    us_ref[...] = proj[:, 3 * a + POOL_WIDTH:]

    z = jnp.dot(h, wf_ref[...], preferred_element_type=F32) + bf_ref[...]
    logf = jnp.minimum(z, 0.0) - jnp.log(1.0 + jnp.exp(-jnp.abs(z)))
    row = lax.broadcasted_iota(jnp.int32, (tm, tm), 0)
    col = lax.broadcasted_iota(jnp.int32, (tm, tm), 1)
    tri = (row >= col).astype(BF16)
    hi, mid, lo = _split3(logf)
    c = (jnp.dot(tri, hi, preferred_element_type=F32)
         + jnp.dot(tri, mid, preferred_element_type=F32)
         + jnp.dot(tri, lo, preferred_element_type=F32)) + carry_ref[...]
    c_ref[...] = c[:, :ATTN_HEADS]
    carry_ref[...] = c[tm - 1:tm, :]


def _inproj(x2, g, w_main, w_f, b_f, *, seq, tm=512):
    n, d = x2.shape
    nm = w_main.shape[1]
    a = ATTN_WIDTH
    out_shape = (
        jax.ShapeDtypeStruct((n, a), BF16), jax.ShapeDtypeStruct((n, a), BF16),
        jax.ShapeDtypeStruct((n, a), BF16), jax.ShapeDtypeStruct((n, POOL_WIDTH), BF16),
        jax.ShapeDtypeStruct((n, SSM_WIDTH), F32), jax.ShapeDtypeStruct((n, ATTN_HEADS), F32))
    row = lambda w: pl.BlockSpec((tm, w), lambda i: (i, 0))
    full = lambda r, c: pl.BlockSpec((r, c), lambda i: (0, 0))
    return pl.pallas_call(
        functools.partial(_inproj_kernel, tiles_per_seq=seq // tm),
        grid=(n // tm,),
        in_specs=[row(d), full(1, d), full(d, nm), full(d, LANES), full(1, LANES)],
        out_specs=(row(a), row(a), row(a), row(POOL_WIDTH), row(SSM_WIDTH), row(ATTN_HEADS)),
        out_shape=out_shape,
        scratch_shapes=[pltpu.VMEM((1, LANES), F32)],
        compiler_params=_params(("arbitrary",)),
        name="inproj",
    )(x2, g, w_main, w_f, b_f)


HEADS_PER_BLOCK = LANES // ATTN_HEAD_DIM


def _attn_kernel(q_ref, k_ref, v_ref, ct_ref, o_ref, *, tq, tk):
    nlb = q_ref.shape[2] // LANES
    hp = pl.program_id(1)
    qi = pl.program_id(2)
    nkb = k_ref.shape[1] // tk
    lane = lax.broadcasted_iota(jnp.int32, (1, LANES), 1)
    in_head = [(lane // ATTN_HEAD_DIM) == hh for hh in range(HEADS_PER_BLOCK)]
    chains = [(lb, hh) for lb in range(nlb) for hh in range(HEADS_PER_BLOCK)]
    qm, crow, cref = [], [], []
    for lb, hh in chains:
        q2 = q_ref[0, :, lb * LANES:(lb + 1) * LANES]
        qm.append(jnp.where(in_head[hh], q2, jnp.zeros_like(q2)))
        crow.append(((hp * nlb + lb) * HEADS_PER_BLOCK + hh) * nkb)
        cref.append(ct_ref[0, pl.ds(crow[-1] + qi, 1), :][:, 0:1])

    def step(kb, carry, masked):
        new = []
        for ci, (lb, hh) in enumerate(chains):
            kblk = k_ref[0, pl.ds(kb * tk, tk), lb * LANES:(lb + 1) * LANES]
            vblk = v_ref[0, pl.ds(kb * tk, tk), lb * LANES:(lb + 1) * LANES]
            m, acc = carry[ci]
            s = lax.dot_general(qm[ci], kblk, (((1,), (1,)), ((), ())), preferred_element_type=F32)
            s = s - (ct_ref[0, pl.ds(crow[ci] + kb, 1), :] - cref[ci]) * LOG2E
            if masked:
                r = lax.broadcasted_iota(jnp.int32, (tq, tk), 0)
                cidx = lax.broadcasted_iota(jnp.int32, (tq, tk), 1)
                s = jnp.where(cidx <= r, s, NEG_INF)
            m_new = jnp.maximum(m, jnp.max(s, axis=1, keepdims=True))
            p = jnp.exp2(s - m_new)
            vsel = jnp.where(in_head[hh], vblk, jnp.ones_like(vblk))
            acc = jnp.exp2(m - m_new) * acc + jnp.dot(p.astype(BF16), vsel, preferred_element_type=F32)
            new.append((m_new, acc))
        return tuple(new)

    init = tuple((jnp.full((tq, 1), NEG_INF, F32), jnp.zeros((tq, LANES), F32)) for _ in chains)
    carry = lax.fori_loop(0, qi, lambda kb, cr: step(kb, cr, False), init)
    carry = step(qi, carry, True)
    for lb in range(nlb):
        out = jnp.zeros((tq, LANES), F32)
        for hh in range(HEADS_PER_BLOCK):
            acc = carry[lb * HEADS_PER_BLOCK + hh][1]
            out = jnp.where(in_head[hh], acc / pltpu.roll(acc, ATTN_HEAD_DIM, 1), out)
        o_ref[0, :, lb * LANES:(lb + 1) * LANES] = out.astype(o_ref.dtype)


def _attention(q, k, v, ct, *, tq, lane_blocks=2):
    b, L, a = q.shape
    tk = ct.shape[2]
    assert tq == tk, "the diagonal key block of query tile i must be key block i"
    w = lane_blocks * LANES
    blk = lambda: pl.BlockSpec((1, tq, w), lambda bi, hi, qi: (bi, qi, hi))
    seq = lambda: pl.BlockSpec((1, L, w), lambda bi, hi, qi: (bi, 0, hi))
    return pl.pallas_call(
        functools.partial(_attn_kernel, tq=tq, tk=tk),
        grid=(b, a // w, L // tq),
        in_specs=[blk(), seq(), seq(),
                  pl.BlockSpec((1, ct.shape[1], tk), lambda bi, hi, qi: (bi, 0, 0))],
        out_specs=blk(),
        out_shape=jax.ShapeDtypeStruct((b, L, a), BF16),
        compiler_params=_params(("arbitrary", "arbitrary", "arbitrary")),
        name="fox_attention",
    )(q, k, v, ct)


def _pool_kernel(u_ref, w_ref, s_ref, o_ref):
    x = u_ref[0].astype(F32)
    L, w = x.shape
    row = lax.broadcasted_iota(jnp.int32, (L, w), 0)
    group = lax.broadcasted_iota(jnp.int32, (L, w), 1) // POOL_GROUP_DIM

    def shifted(y, s):
        return jnp.where(row >= s, pltpu.roll(y, s, 0), 0.0)

    acc = x
    sel = jnp.zeros_like(x)
    win_lane = jnp.zeros((L, w), F32)
    span = 1
    for gi, win in enumerate(POOL_WINDOWS):
        while span < win:
            acc = acc + shifted(acc, span)
            span *= 2
        sel = jnp.where(group == gi, acc, sel)
        win_lane = jnp.where(group == gi, float(win), win_lane)
    count = jnp.minimum(row.astype(F32) + 1.0, win_lane)
    d = sel / count - x
    y = jnp.dot(d.astype(BF16), w_ref[...], preferred_element_type=F32) * s_ref[...]
    o_ref[0] = y.astype(o_ref.dtype)


def _pool(u, w_bd, scale):
    b, L, w = u.shape
    return pl.pallas_call(
        _pool_kernel,
        grid=(b,),
        in_specs=[pl.BlockSpec((1, L, w), lambda i: (i, 0, 0)),
                  pl.BlockSpec((w, w), lambda i: (0, 0)),
                  pl.BlockSpec((1, w), lambda i: (0, 0))],
        out_specs=pl.BlockSpec((1, L, w), lambda i: (i, 0, 0)),
        out_shape=jax.ShapeDtypeStruct((b, L, w), BF16),
        compiler_params=_params(("arbitrary",)),
        name="pool_mixer",
    )(u, w_bd, scale)


def _group_of(idx, width):
    groups = LANES // SSM_GROUP_DIM
    return (idx >> (width.bit_length() - 1)) & (groups - 1)


def _expand_blockdiag(r_ref, row_inner, col_inner):
    rows, w = r_ref.shape[1:]
    groups = LANES // SSM_GROUP_DIM
    r = lax.broadcasted_iota(jnp.int32, (w, groups * w), 0)
    c = lax.broadcasted_iota(jnp.int32, (w, groups * w), 1)
    shift = col_inner.bit_length() - 1
    outer_c = c >> (shift + groups.bit_length() - 1)
    spread = (outer_c == (r >> shift)) & ((c & (col_inner - 1)) == (r & (col_inner - 1)))
    big = jnp.dot(r_ref[0], spread.astype(BF16), preferred_element_type=F32)
    ri = lax.broadcasted_iota(jnp.int32, (rows, groups * w), 0)
    ci = lax.broadcasted_iota(jnp.int32, (rows, groups * w), 1)
    return jnp.where(_group_of(ri, row_inner) == _group_of(ci, col_inner), big, 0.0).astype(BF16)


def _ssm_kernel(u_ref, rm_ref, rz_ref, ry_ref, coef_ref, y_ref, m_s, wz_s, wy_s, ucat, z_ref, s_ref, *, seqs):
    t = SSM_CHUNK

    @pl.when(pl.program_id(1) == 0)
    def _():
        m_s[...] = _expand_blockdiag(rm_ref, SSM_GROUP_DIM, SSM_GROUP_DIM)
        wz_s[...] = _expand_blockdiag(rz_ref, SSM_GROUP_DIM, SSM_STATE)
        wy_s[...] = _expand_blockdiag(ry_ref, SSM_STATE, SSM_GROUP_DIM)

    nc = u_ref.shape[0] // t
    cps = nc // seqs
    for k in range(t):
        ucat[:, k * LANES:(k + 1) * LANES] = u_ref[pl.ds(k, nc, stride=t), :].astype(BF16)
    u = ucat[...]
    z_ref[...] = jnp.dot(u, wz_s[...], preferred_element_type=F32)
    half = z_ref.shape[1] // 2
    a1 = coef_ref[0, 0:1, :]
    a2 = coef_ref[0, 1:2, :]

    def step(c, states):
        new = []
        for b in range(seqs):
            st = states[b]
            r = b * cps + c
            s_ref[pl.ds(r, 1), :] = st
            sw = jnp.concatenate([st[:, half:], st[:, :half]], axis=1)
            new.append(a1 * st + a2 * sw + z_ref[pl.ds(r, 1), :])
        return tuple(new)

    init = tuple(jnp.zeros((1, 2 * half), F32) for _ in range(seqs))
    lax.fori_loop(0, cps, step, init, unroll=8)
    y = jnp.dot(u, m_s[...], preferred_element_type=F32)
    y = y + jnp.dot(s_ref[...].astype(BF16), wy_s[...], preferred_element_type=F32)
    for k in range(t):
        y_ref[pl.ds(k, nc, stride=t), :] = y[:, k * LANES:(k + 1) * LANES]


def _ssm_core(u, rm, rz, ry, coef, *, seq, seqs_per_block=2):
    n, w = u.shape
    rows = seq * seqs_per_block
    nc = rows // SSM_CHUNK
    kw = SSM_CHUNK * LANES
    p2 = coef.shape[2]
    blk = pl.BlockSpec((rows, LANES), lambda hf, r: (r, hf))
    wspec = lambda arr: pl.BlockSpec((1,) + arr.shape[1:], lambda hf, r: (hf, 0, 0))
    return pl.pallas_call(
        functools.partial(_ssm_kernel, seqs=seqs_per_block),
        grid=(w // LANES, n // rows),
        in_specs=[blk, wspec(rm), wspec(rz), wspec(ry), wspec(coef)],
        out_specs=blk,
        out_shape=jax.ShapeDtypeStruct((n, w), F32),
        scratch_shapes=[pltpu.VMEM((kw, kw), BF16), pltpu.VMEM((kw, p2), BF16), pltpu.VMEM((p2, kw), BF16),
                        pltpu.VMEM((nc, kw), BF16), pltpu.VMEM((nc, p2), F32), pltpu.VMEM((nc, p2), F32)],
        compiler_params=_params(("arbitrary", "arbitrary")),
        name="s5_core",
    )(u, rm, rz, ry, coef)


def _ssm_matrices(a_re, a_im, b_re, b_im, c_re, c_im, d, log_dt):
    t = SSM_CHUNK
    g, p = a_re.shape
    hdim = d.shape[1]
    gl = LANES // hdim
    nb = g // gl
    lam = lax.complex(a_re, a_im)
    lam_dt = lam * jnp.exp(log_dt)[:, None]
    lam_bar = jnp.exp(lam_dt)
    b_bar = ((lam_bar - 1.0) / lam)[:, :, None] * lax.complex(b_re, b_im)
    cmat = lax.complex(c_re, c_im)
    steps = jnp.arange(t + 1, dtype=F32)
    pw = jnp.exp(lam_dt[:, None, :] * steps[None, :, None])
    kern = jnp.einsum('gop,gdp,gpi->gdoi', cmat, pw[:, :t], b_bar).real
    kern = kern.at[:, 0].add(jax.vmap(jnp.diag)(d))
    j = jnp.arange(t)[:, None]
    i = jnp.arange(t)[None, :]
    m = jnp.where((i >= j)[None, :, :, None, None], kern[:, jnp.clip(i - j, 0, t - 1)], 0.0)
    wz = jnp.einsum('gjp,gpi->gjip', pw[:, t - 1 - jnp.arange(t)], b_bar)
    cw = jnp.einsum('gop,gip->giop', cmat, pw[:, 1:t + 1])
    rm = m.reshape(nb, gl, t, t, hdim, hdim).transpose(0, 2, 1, 5, 3, 4).reshape(nb, t * LANES, t * hdim)
    rz = jnp.stack([wz.real, wz.imag], axis=3).reshape(nb, gl, t, hdim, 2, p)
    rz = rz.transpose(0, 2, 1, 3, 4, 5).reshape(nb, t * LANES, 2 * p)
    ry = jnp.stack([cw.real, -cw.imag], axis=1).reshape(nb, gl, 2, t, hdim, p)
    ry = ry.transpose(0, 2, 1, 5, 3, 4).reshape(nb, 2 * gl * p, t * hdim)
    ar = pw[:, t].real.reshape(nb, gl * p)
    ai = pw[:, t].imag.reshape(nb, gl * p)
    coef = jnp.stack([jnp.concatenate([ar, ar], -1), jnp.concatenate([-ai, ai], -1)], axis=1)
    return rm.astype(BF16), rz.astype(BF16), ry.astype(BF16), coef.astype(F32)


def _mixout_kernel(*refs, moe):
    (x_ref, attn_ref, pool_ref, ys_ref, wglu_ref, bglu_ref, ga_ref, gp_ref, gs_ref,
     wo_ref, gpost_ref, gpre_ref) = refs[:12]
    if moe:
        router_ref, xo_ref, h_ref, ri_ref, rg_ref, cnt_ref, cnt_acc = refs[12:]

        @pl.when(pl.program_id(0) == 0)
        def _():
            cnt_acc[...] = jnp.zeros_like(cnt_acc)
    else:
        xo_ref, h_ref = refs[12:]
    a = attn_ref.shape[1]
    pw = pool_ref.shape[1]
    attn_n = _rms(attn_ref[...].astype(F32), ga_ref[...]).astype(BF16)
    pool_n = _rms(pool_ref[...].astype(F32), gp_ref[...]).astype(BF16)
    y = jax.nn.gelu(ys_ref[...])
    gate = jnp.dot(y.astype(BF16), wglu_ref[...], preferred_element_type=F32) + bglu_ref[...]
    ssm = y * jax.nn.sigmoid(gate)
    ssm_n = _rms(ssm, gs_ref[...]).astype(BF16)
    mix = (jnp.dot(attn_n, wo_ref[0:a, :], preferred_element_type=F32)
           + jnp.dot(pool_n, wo_ref[a:a + pw, :], preferred_element_type=F32)
           + jnp.dot(ssm_n, wo_ref[a + pw:, :], preferred_element_type=F32))
    x = x_ref[...] + _rms(mix, gpost_ref[...])
    xo_ref[...] = x
    h = _rms(x, gpre_ref[...])
    if moe:
        _store_slabs(h_ref, 0, h)
    else:
        h_ref[...] = h.astype(h_ref.dtype)
    if moe:
        hi, mid, _ = _split3(h)
        rhi, rmid, _ = _split3(router_ref[...])
        dot = lambda p, q: jnp.dot(p, q, preferred_element_type=F32)
        logits = dot(hi, rhi) + dot(hi, rmid) + dot(mid, rhi)
        lane = lax.broadcasted_iota(jnp.int32, logits.shape, 1)
        logits = jnp.where(lane < N_EXPERTS, logits, -jnp.inf)
        m1 = jnp.max(logits, axis=1, keepdims=True)
        i1 = jnp.min(jnp.where(logits == m1, lane, LANES), axis=1, keepdims=True)
        rest = jnp.where(lane == i1, -jnp.inf, logits)
        m2 = jnp.max(rest, axis=1, keepdims=True)
        i2 = jnp.min(jnp.where(rest == m2, lane, LANES), axis=1, keepdims=True)
        e2 = jnp.exp(m2 - m1)
        g1 = 1.0 / (1.0 + e2)
        rg_ref[...] = jnp.where(lane == 0, g1, jnp.where(lane == 1, e2 * g1, 0.0))
        tm = logits.shape[0]
        onehot = ((lane == i1) | (lane == i2)).astype(BF16)
        tri = (lax.broadcasted_iota(jnp.int32, (tm, tm), 0)
               > lax.broadcasted_iota(jnp.int32, (tm, tm), 1)).astype(BF16)
        before = jnp.dot(tri, onehot, preferred_element_type=F32) + cnt_acc[...]
        r1 = jnp.sum(jnp.where(lane == i1, before, 0.0), axis=1, keepdims=True).astype(jnp.int32)
        r2 = jnp.sum(jnp.where(lane == i2, before, 0.0), axis=1, keepdims=True).astype(jnp.int32)
        ri = jnp.where(lane == 0, i1, jnp.where(lane == 1, i2,
                       jnp.where(lane == 2, r1, jnp.where(lane == 3, r2, 0))))
        ri_ref[...] = ri.T[:ri_ref.shape[0], :]
        cnt_acc[...] += jnp.sum(onehot.astype(F32), axis=0, keepdims=True)
        cnt_ref[...] = cnt_acc[...]


def _mixout(x2, attn, pool, ys, w_glu, b_glu, g_attn, g_pool, g_ssm, w_out, g_post, g_pre,
            router=None, *, tm=512):
    n, d = x2.shape
    moe = router is not None
    row = lambda w: pl.BlockSpec((tm, w), lambda i: (i, 0))
    full = lambda arr: pl.BlockSpec(arr.shape, lambda i: (0, 0))
    ins = [x2, attn, pool, ys, w_glu, b_glu, g_attn, g_pool, g_ssm, w_out, g_post, g_pre]
    in_specs = [row(d), row(attn.shape[1]), row(pool.shape[1]), row(ys.shape[1])] + [full(t) for t in ins[4:]]
    if moe:
        assert d == SUBLANES * LANES
        h_shape = jax.ShapeDtypeStruct((n * SUBLANES, LANES), F32)
        h_spec = pl.BlockSpec((tm * SUBLANES, LANES), lambda i: (i, 0))
    else:
        h_shape, h_spec = jax.ShapeDtypeStruct((n, d), BF16), row(d)
    out_shape = [jax.ShapeDtypeStruct((n, d), F32), h_shape]
    out_specs = [row(d), h_spec]
    scratch = []
    if moe:
        ins.append(router)
        in_specs.append(full(router))
        out_shape += [jax.ShapeDtypeStruct((8, n), jnp.int32), jax.ShapeDtypeStruct((n, LANES), F32),
                      jax.ShapeDtypeStruct((1, LANES), F32)]
        out_specs += [pl.BlockSpec((8, tm), lambda i: (0, i)), row(LANES),
                      pl.BlockSpec((1, LANES), lambda i: (0, 0))]
        scratch = [pltpu.VMEM((1, LANES), F32)]
    return pl.pallas_call(
        functools.partial(_mixout_kernel, moe=moe),
        grid=(n // tm,),
        in_specs=in_specs,
        out_specs=tuple(out_specs),
        out_shape=tuple(out_shape),
        scratch_shapes=scratch,
        compiler_params=_params(("arbitrary",)),
        name="mix_out",
    )(*ins)


def _ffn_kernel(h_ref, x_ref, wgu_ref, wd_ref, g_ref, o_ref, acc_ref, *, sub):
    f = pl.program_id(1)

    @pl.when(f == 0)
    def _():
        acc_ref[...] = jnp.zeros_like(acc_ref)

    tf = wd_ref.shape[0]
    for r in range(h_ref.shape[0] // sub):
        rows = pl.ds(r * sub, sub)
        hgu = jnp.dot(h_ref[rows, :], wgu_ref[0], preferred_element_type=F32)
        hg, hu = hgu[:, :tf], hgu[:, tf:]
        act = (hg * jax.nn.sigmoid(hg) * hu).astype(BF16)
        acc_ref[rows, :] += jnp.dot(act, wd_ref[...], preferred_element_type=F32)

    @pl.when(f == pl.num_programs(1) - 1)
    def _():
        o_ref[...] = x_ref[...] + _rms(acc_ref[...], g_ref[...])


def _ffn(h, x2, wg, wu, wd, g_post, *, tm=1024, tf, sub=256):
    n, d = x2.shape
    fdim = wg.shape[1]
    nf = fdim // tf
    wgu = jnp.concatenate([wg.reshape(d, nf, tf), wu.reshape(d, nf, tf)], axis=2).transpose(1, 0, 2)
    row = lambda w: pl.BlockSpec((tm, w), lambda i, f: (i, 0))
    return pl.pallas_call(
        functools.partial(_ffn_kernel, sub=sub),
        grid=(n // tm, nf),
        in_specs=[row(d), row(d),
                  pl.BlockSpec((1, d, 2 * tf), lambda i, f: (f, 0, 0)),
                  pl.BlockSpec((tf, d), lambda i, f: (f, 0)),
                  pl.BlockSpec((1, d), lambda i, f: (0, 0))],
        out_specs=row(d),
        out_shape=jax.ShapeDtypeStruct((n, d), F32),
        scratch_shapes=[pltpu.VMEM((tm, d), F32)],
        compiler_params=_params(("arbitrary", "arbitrary")),
        name="dense_ffn",
    )(h, x2, wgu, wd, g_post)


MOE_TILE = 512


def _moe_tables(counts, n_items):
    tm = MOE_TILE
    counts = counts.astype(jnp.int32)
    ntiles = (counts + tm - 1) // tm
    ends = jnp.cumsum(ntiles)
    starts = ends - ntiles
    w = jnp.arange(n_items, dtype=jnp.int32)
    wc = jnp.minimum(w, ends[-1] - 1)
    e = jnp.sum((wc[:, None] >= ends[None, :]).astype(jnp.int32), axis=1)
    valid = w < ends[-1]
    rowblock = jnp.where(valid, w, n_items)
    nvalid = jnp.where(valid, jnp.clip(counts[e] - (wc - starts[e]) * tm, 0, tm), 0)
    first = ((w == 0) | (e != jnp.roll(e, 1))).astype(jnp.int32)
    return (e, rowblock, nvalid.astype(jnp.int32), first), starts * tm


def _slab(ref, token):
    return ref.at[pl.ds(pl.multiple_of(token * SUBLANES, SUBLANES), SUBLANES)]


def _dispatch_kernel(dest_ref, h_ref, xs_ref, sem, *, n):
    tm = h_ref.shape[0] // SUBLANES
    base = pl.program_id(0) * tm

    def body(r, carry):
        for k in range(2):
            d = dest_ref[k * n + base + r]
            pltpu.make_async_copy(_slab(h_ref, r), _slab(xs_ref, d), sem).start(priority=k)
        return carry

    lax.fori_loop(0, tm, body, 0, unroll=8)
    for k in range(2):
        pltpu.make_async_copy(h_ref, h_ref, sem).wait()


def _dispatch(dest, h, cap_rows, *, tm=512):
    n = h.shape[0] // SUBLANES
    return pl.pallas_call(
        functools.partial(_dispatch_kernel, n=n),
        grid_spec=pltpu.PrefetchScalarGridSpec(
            num_scalar_prefetch=1,
            grid=(n // tm,),
            in_specs=[pl.BlockSpec((tm * SUBLANES, LANES), lambda i, dest: (i, 0))],
            out_specs=pl.BlockSpec(memory_space=pl.ANY),
            scratch_shapes=[pltpu.SemaphoreType.DMA(())]),
        out_shape=jax.ShapeDtypeStruct((cap_rows * SUBLANES, LANES), F32),
        compiler_params=_params(("arbitrary",)),
        name="moe_dispatch",
    )(dest, h)


MOE_SUB = 256


def _moe_up_kernel(ie_ref, rb_ref, nv_ref, first_ref, xs_ref, wg_ref, wu_ref, h_ref, wgu_s):
    w = pl.program_id(1)
    tf = h_ref.shape[1]

    @pl.when(first_ref[w] == 1)
    def _():
        wgu_s[:, :tf] = wg_ref[0].astype(BF16)
        wgu_s[:, tf:] = wu_ref[0].astype(BF16)

    nv = nv_ref[w]

    @pl.when(nv > 0)
    def _():
        for r in range(h_ref.shape[0] // MOE_SUB):
            rows = pl.ds(r * MOE_SUB, MOE_SUB)
            idx = r * MOE_SUB + lax.broadcasted_iota(jnp.int32, (MOE_SUB, 1), 0)
            x = _load_slabs(xs_ref, r * MOE_SUB, MOE_SUB)
            x = jnp.where(idx < nv, x, 0.0).astype(BF16)
            hgu = jnp.dot(x, wgu_s[...], preferred_element_type=F32)
            hg, hu = hgu[:, :tf], hgu[:, tf:]
            h_ref[rows, :] = (hg * jax.nn.sigmoid(hg) * hu).astype(BF16)

    @pl.when(nv == 0)
    def _():
        h_ref[...] = jnp.zeros_like(h_ref)


def _moe_up(tables, xs, wg, wu, *, tf=896):
    tm = MOE_TILE
    cap_rows = xs.shape[0] // SUBLANES
    ne, d, fdim = wg.shape
    n_items = tables[0].shape[0]
    return pl.pallas_call(
        _moe_up_kernel,
        grid_spec=pltpu.PrefetchScalarGridSpec(
            num_scalar_prefetch=4,
            grid=(fdim // tf, n_items),
            in_specs=[pl.BlockSpec((tm * SUBLANES, LANES), lambda f, w, ie, rb, nv, fi: (rb[w], 0)),
                      pl.BlockSpec((1, d, tf), lambda f, w, ie, rb, nv, fi: (ie[w], 0, f)),
                      pl.BlockSpec((1, d, tf), lambda f, w, ie, rb, nv, fi: (ie[w], 0, f))],
            out_specs=pl.BlockSpec((tm, tf), lambda f, w, ie, rb, nv, fi: (rb[w], f)),
            scratch_shapes=[pltpu.VMEM((d, 2 * tf), BF16)]),
        out_shape=jax.ShapeDtypeStruct((cap_rows, fdim), BF16),
        compiler_params=_params(("arbitrary", "arbitrary")),
        name="moe_up",
    )(*tables, xs, wg, wu)


def _moe_down_kernel(ie_ref, rb_ref, nv_ref, first_ref, h_ref, wd_hbm, y_ref, wd_s, stage, sem):
    w = pl.program_id(0)
    chunk = stage.shape[1]
    nchunks = wd_s.shape[0] // chunk

    @pl.when(first_ref[w] == 1)
    def _():
        def copy(c):
            return pltpu.make_async_copy(wd_hbm.at[ie_ref[w], pl.ds(c * chunk, chunk)], stage.at[c % 2],
                                         sem.at[c % 2])
        copy(0).start()
        for c in range(nchunks):
            if c + 1 < nchunks:
                copy(c + 1).start()
            copy(c).wait()
            wd_s[pl.ds(c * chunk, chunk), :] = stage[c % 2].astype(BF16)

    @pl.when(nv_ref[w] > 0)
    def _():
        for r in range(h_ref.shape[0] // MOE_SUB):
            y = jnp.dot(h_ref[pl.ds(r * MOE_SUB, MOE_SUB), :], wd_s[...], preferred_element_type=F32)
            _store_slabs(y_ref, r * MOE_SUB, y)

    @pl.when(nv_ref[w] == 0)
    def _():
        y_ref[...] = jnp.zeros_like(y_ref)


def _moe_down(tables, hid, wd, *, chunk=512):
    tm = MOE_TILE
    cap_rows, fdim = hid.shape
    d = wd.shape[2]
    n_items = tables[0].shape[0]
    assert d == SUBLANES * LANES and fdim % chunk == 0
    return pl.pallas_call(
        _moe_down_kernel,
        grid_spec=pltpu.PrefetchScalarGridSpec(
            num_scalar_prefetch=4,
            grid=(n_items,),
            in_specs=[pl.BlockSpec((tm, fdim), lambda w, ie, rb, nv, fi: (rb[w], 0)),
                      pl.BlockSpec(memory_space=pl.ANY)],
            out_specs=pl.BlockSpec((tm * SUBLANES, LANES), lambda w, ie, rb, nv, fi: (rb[w], 0)),
            scratch_shapes=[pltpu.VMEM((fdim, d), BF16), pltpu.VMEM((2, chunk, d), F32),
                            pltpu.SemaphoreType.DMA((2,))]),
        out_shape=jax.ShapeDtypeStruct((cap_rows * SUBLANES, LANES), F32),
        compiler_params=_params(("arbitrary",)),
        name="moe_down",
    )(*tables, hid, wd)


def _combine_kernel(dest_ref, x_ref, gate_ref, g_ref, ys_ref, o_ref, buf, sem, *, n):
    tm = x_ref.shape[0]
    i = pl.program_id(0)

    def issue(tile, slot):
        def body(r, carry):
            for k in range(2):
                d = dest_ref[k * n + tile * tm + r]
                pltpu.make_async_copy(_slab(ys_ref, d), _slab(buf.at[slot, k], r),
                                      sem.at[slot]).start(priority=k)
            return carry
        lax.fori_loop(0, tm, body, 0, unroll=8)

    @pl.when(i == 0)
    def _():
        issue(0, 0)

    @pl.when(i + 1 < pl.num_programs(0))
    def _():
        issue(i + 1, (i + 1) % 2)

    slot = i % 2
    for k in range(2):
        pltpu.make_async_copy(buf.at[slot, k], buf.at[slot, k], sem.at[slot]).wait()
    gates = gate_ref[...]
    f = (gates[:, 0:1] * _load_slabs(buf.at[slot, 0], 0, tm)
         + gates[:, 1:2] * _load_slabs(buf.at[slot, 1], 0, tm))
    o_ref[...] = x_ref[...] + _rms(f, g_ref[...])


def _combine(dest, x2, gates, g_post, ys, *, tm=256):
    n, d = x2.shape
    return pl.pallas_call(
        functools.partial(_combine_kernel, n=n),
        grid_spec=pltpu.PrefetchScalarGridSpec(
            num_scalar_prefetch=1,
            grid=(n // tm,),
            in_specs=[pl.BlockSpec((tm, d), lambda i, dest: (i, 0)),
                      pl.BlockSpec((tm, LANES), lambda i, dest: (i, 0)),
                      pl.BlockSpec((1, d), lambda i, dest: (0, 0)),
                      pl.BlockSpec(memory_space=pl.ANY)],
            out_specs=pl.BlockSpec((tm, d), lambda i, dest: (i, 0)),
            scratch_shapes=[pltpu.VMEM((2, 2, tm * SUBLANES, LANES), F32), pltpu.SemaphoreType.DMA((2,))]),
        out_shape=jax.ShapeDtypeStruct((n, d), F32),
        compiler_params=_params(("arbitrary",)),
        name="moe_combine",
    )(dest, x2, gates, g_post, ys)


def _moe(h, x2, route_i, route_g, counts, wg, wu, wd, g_post):
    n = x2.shape[0]
    n_items = 2 * n // MOE_TILE + N_EXPERTS
    tables, base = _moe_tables(counts[0, :N_EXPERTS], n_items)
    base_of = lambda e: jnp.sum(jnp.where(e[None, :] == jnp.arange(N_EXPERTS)[:, None], base[:, None], 0), axis=0)
    dest = jnp.concatenate([base_of(route_i[0]) + route_i[2], base_of(route_i[1]) + route_i[3]])
    xs = _dispatch(dest, h, (n_items + 1) * MOE_TILE)
    hid = _moe_up(tables, xs, wg, wu)
    ys = _moe_down(tables, hid, wd)
    return _combine(dest, x2, route_g, g_post, ys)


def _row(v):
    return v.reshape(1, -1).astype(F32)


def kernel(x, norm_mix_pre, norm_mix_post, norm_ffn_pre, norm_ffn_post, w_in, b_forget, pool_w, pool_scale, ssm_a_re, ssm_a_im, ssm_b_re, ssm_b_im, ssm_c_re, ssm_c_im, ssm_d, ssm_log_dt, ssm_w_glu, ssm_b_glu, branch_norm_attn, branch_norm_pool, branch_norm_ssm, w_out, ffn_w_gate, ffn_w_up, ffn_w_down, moe_router, moe_w_gate, moe_w_up, moe_w_down):
    b, L, d = x.shape
    depth = w_in.shape[0]
    n = b * L
    a = ATTN_WIDTH
    f_off = 3 * a
    p_off = f_off + ATTN_HEADS
    attn_tile = 512
    x2 = x.reshape(n, d)
    for i in range(depth):
        w = w_in[i]
        w_main = jnp.concatenate([w[:, :f_off], w[:, p_off:]], axis=1).astype(BF16)
        w_f = jnp.pad(w[:, f_off:p_off], ((0, 0), (0, LANES - ATTN_HEADS))).astype(BF16)
        b_f = jnp.pad(b_forget[i], (0, LANES - ATTN_HEADS)).reshape(1, LANES)
        q, k, v, up, us, c = _inproj(x2, _row(norm_mix_pre[i]), w_main, w_f, b_f, seq=L)

        ct = c.reshape(b, L, ATTN_HEADS).transpose(0, 2, 1).reshape(b, ATTN_HEADS * (L // attn_tile), attn_tile)
        attn = _attention(q.reshape(b, L, a), k.reshape(b, L, a), v.reshape(b, L, a),
                          ct, tq=attn_tile).reshape(n, a)

        pool = _pool(up.reshape(b, L, POOL_WIDTH),
                     jax.scipy.linalg.block_diag(*pool_w[i]).astype(BF16),
                     _row(pool_scale[i])).reshape(n, POOL_WIDTH)

        mats = _ssm_matrices(ssm_a_re[i], ssm_a_im[i], ssm_b_re[i], ssm_b_im[i],
                             ssm_c_re[i], ssm_c_im[i], ssm_d[i], ssm_log_dt[i])
        ys = _ssm_core(us, *mats, seq=L)

        moe = i % 2 == 1
        j = i // 2
        router = jnp.pad(moe_router[j], ((0, 0), (0, LANES - N_EXPERTS))) if moe else None
        res = _mixout(x2, attn, pool, ys, ssm_w_glu[i].astype(BF16), _row(ssm_b_glu[i]),
                      _row(branch_norm_attn[i]), _row(branch_norm_pool[i]), _row(branch_norm_ssm[i]),
                      w_out[i].astype(BF16), _row(norm_mix_post[i]), _row(norm_ffn_pre[i]), router)
        if moe:
            x2, h, route_i, route_g, counts = res
            x2 = _moe(h, x2, route_i, route_g, counts, moe_w_gate[j], moe_w_up[j], moe_w_down[j],
                      _row(norm_ffn_post[i]))
        else:
            x2, h = res
            x2 = _ffn(h, x2, ffn_w_gate[j].astype(BF16), ffn_w_up[j].astype(BF16),
                      ffn_w_down[j].astype(BF16), _row(norm_ffn_post[i]), tf=1408)
    return x2.reshape(b, L, d)
```

```python
import functools

import jax
import jax.numpy as jnp
from jax import lax
from jax.experimental import pallas as pl
from jax.experimental.pallas import tpu as pltpu

F32 = jnp.float32
BF16 = jnp.bfloat16

RMS_EPS = 1e-6
NEG_INF = -1e30
LOG2E = 1.4426950408889634

ATTN_HEADS = 8
ATTN_HEAD_DIM = 64
ATTN_WIDTH = ATTN_HEADS * ATTN_HEAD_DIM
POOL_WINDOWS = (2, 4, 8, 16)
POOL_GROUP_DIM = 64
POOL_WIDTH = len(POOL_WINDOWS) * POOL_GROUP_DIM
SSM_GROUPS = 16
SSM_GROUP_DIM = 16
SSM_STATE = 64
SSM_WIDTH = SSM_GROUPS * SSM_GROUP_DIM
SSM_CHUNK = 8
N_EXPERTS = 8

LANES = 128
SUBLANES = 8
VMEM_LIMIT = 48 * 1024 * 1024


def _params(sem):
    return pltpu.CompilerParams(dimension_semantics=sem, vmem_limit_bytes=VMEM_LIMIT)


def _rms(x, g):
    return x * lax.rsqrt(jnp.mean(x * x, axis=-1, keepdims=True) + RMS_EPS) * g


def _load_slabs(ref, row0, rows):
    return jnp.concatenate([ref[pl.ds(row0 * SUBLANES + s, rows, stride=SUBLANES), :]
                            for s in range(SUBLANES)], axis=1)


def _store_slabs(ref, row0, val):
    for s in range(SUBLANES):
        ref[pl.ds(row0 * SUBLANES + s, val.shape[0], stride=SUBLANES), :] = val[:, s * LANES:(s + 1) * LANES]


def _split3(x):
    hi = x.astype(BF16)
    r = x - hi.astype(F32)
    mid = r.astype(BF16)
    lo = (r - mid.astype(F32)).astype(BF16)
    return hi, mid, lo


def _inproj_kernel(x_ref, g_ref, wm_ref, wf_ref, bf_ref,
                   q_ref, k_ref, v_ref, up_ref, us_ref, c_ref, carry_ref, *, tiles_per_seq):
    @pl.when(pl.program_id(0) % tiles_per_seq == 0)
    def _():
        carry_ref[...] = jnp.zeros_like(carry_ref)

    tm = x_ref.shape[0]
    h = _rms(x_ref[...], g_ref[...]).astype(BF16)
    proj = jnp.dot(h, wm_ref[...], preferred_element_type=F32)
    a = ATTN_WIDTH
    q_ref[...] = (proj[:, 0:a] * (LOG2E * ATTN_HEAD_DIM ** -0.5)).astype(BF16)
    k_ref[...] = proj[:, a:2 * a].astype(BF16)
    v_ref[...] = proj[:, 2 * a:3 * a].astype(BF16)
    up_ref[...] = proj[:, 3 * a:3 * a + POOL_WIDTH].astype(BF16)
    us_ref[...] = proj[:, 3 * a + POOL_WIDTH:]

    z = jnp.dot(h, wf_ref[...], preferred_element_type=F32) + bf_ref[...]
    logf = jnp.minimum(z, 0.0) - jnp.log(1.0 + jnp.exp(-jnp.abs(z)))
    row = lax.broadcasted_iota(jnp.int32, (tm, tm), 0)
    col = lax.broadcasted_iota(jnp.int32, (tm, tm), 1)
    tri = (row >= col).astype(BF16)
    hi, mid, lo = _split3(logf)
    c = (jnp.dot(tri, hi, preferred_element_type=F32)
         + jnp.dot(tri, mid, preferred_element_type=F32)
         + jnp.dot(tri, lo, preferred_element_type=F32)) + carry_ref[...]
    c_ref[...] = c[:, :ATTN_HEADS]
    carry_ref[...] = c[tm - 1:tm, :]


def _inproj(x2, g, w_main, w_f, b_f, *, seq, tm=512):
    n, d = x2.shape
    nm = w_main.shape[1]
    a = ATTN_WIDTH
    out_shape = (
        jax.ShapeDtypeStruct((n, a), BF16), jax.ShapeDtypeStruct((n, a), BF16),
        jax.ShapeDtypeStruct((n, a), BF16), jax.ShapeDtypeStruct((n, POOL_WIDTH), BF16),
        jax.ShapeDtypeStruct((n, SSM_WIDTH), F32), jax.ShapeDtypeStruct((n, ATTN_HEADS), F32))
    row = lambda w: pl.BlockSpec((tm, w), lambda i: (i, 0))
    full = lambda r, c: pl.BlockSpec((r, c), lambda i: (0, 0))
    return pl.pallas_call(
        functools.partial(_inproj_kernel, tiles_per_seq=seq // tm),
        grid=(n // tm,),
        in_specs=[row(d), full(1, d), full(d, nm), full(d, LANES), full(1, LANES)],
        out_specs=(row(a), row(a), row(a), row(POOL_WIDTH), row(SSM_WIDTH), row(ATTN_HEADS)),
        out_shape=out_shape,
        scratch_shapes=[pltpu.VMEM((1, LANES), F32)],
        compiler_params=_params(("arbitrary",)),
        name="inproj",
    )(x2, g, w_main, w_f, b_f)


HEADS_PER_BLOCK = LANES // ATTN_HEAD_DIM


def _attn_kernel(q_ref, k_ref, v_ref, ct_ref, o_ref, *, tq, tk):
    nlb = q_ref.shape[2] // LANES
    hp = pl.program_id(1)
    qi = pl.program_id(2)
    nkb = k_ref.shape[1] // tk
    lane = lax.broadcasted_iota(jnp.int32, (1, LANES), 1)
    in_head = [(lane // ATTN_HEAD_DIM) == hh for hh in range(HEADS_PER_BLOCK)]
    chains = [(lb, hh) for lb in range(nlb) for hh in range(HEADS_PER_BLOCK)]
    qm, crow, cref = [], [], []
    for lb, hh in chains:
        q2 = q_ref[0, :, lb * LANES:(lb + 1) * LANES]
        qm.append(jnp.where(in_head[hh], q2, jnp.zeros_like(q2)))
        crow.append(((hp * nlb + lb) * HEADS_PER_BLOCK + hh) * nkb)
        cref.append(ct_ref[0, pl.ds(crow[-1] + qi, 1), :][:, 0:1])

    def step(kb, carry, masked):
        new = []
        for ci, (lb, hh) in enumerate(chains):
            kblk = k_ref[0, pl.ds(kb * tk, tk), lb * LANES:(lb + 1) * LANES]
            vblk = v_ref[0, pl.ds(kb * tk, tk), lb * LANES:(lb + 1) * LANES]
            m, acc = carry[ci]
            s = lax.dot_general(qm[ci], kblk, (((1,), (1,)), ((), ())), preferred_element_type=F32)
            s = s - (ct_ref[0, pl.ds(crow[ci] + kb, 1), :] - cref[ci]) * LOG2E
            if masked:
                r = lax.broadcasted_iota(jnp.int32, (tq, tk), 0)
                cidx = lax.broadcasted_iota(jnp.int32, (tq, tk), 1)
                s = jnp.where(cidx <= r, s, NEG_INF)
            m_new = jnp.maximum(m, jnp.max(s, axis=1, keepdims=True))
            p = jnp.exp2(s - m_new)
            vsel = jnp.where(in_head[hh], vblk, jnp.ones_like(vblk))
            acc = jnp.exp2(m - m_new) * acc + jnp.dot(p.astype(BF16), vsel, preferred_element_type=F32)
            new.append((m_new, acc))
        return tuple(new)

    init = tuple((jnp.full((tq, 1), NEG_INF, F32), jnp.zeros((tq, LANES), F32)) for _ in chains)
    carry = lax.fori_loop(0, qi, lambda kb, cr: step(kb, cr, False), init)
    carry = step(qi, carry, True)
    for lb in range(nlb):
        out = jnp.zeros((tq, LANES), F32)
        for hh in range(HEADS_PER_BLOCK):
            acc = carry[lb * HEADS_PER_BLOCK + hh][1]
            out = jnp.where(in_head[hh], acc / pltpu.roll(acc, ATTN_HEAD_DIM, 1), out)
        o_ref[0, :, lb * LANES:(lb + 1) * LANES] = out.astype(o_ref.dtype)


def _attention(q, k, v, ct, *, tq, lane_blocks=2):
    b, L, a = q.shape
    tk = ct.shape[2]
    assert tq == tk, "the diagonal key block of query tile i must be key block i"
    w = lane_blocks * LANES
    blk = lambda: pl.BlockSpec((1, tq, w), lambda bi, hi, qi: (bi, qi, hi))
    seq = lambda: pl.BlockSpec((1, L, w), lambda bi, hi, qi: (bi, 0, hi))
    return pl.pallas_call(
        functools.partial(_attn_kernel, tq=tq, tk=tk),
        grid=(b, a // w, L // tq),
        in_specs=[blk(), seq(), seq(),
                  pl.BlockSpec((1, ct.shape[1], tk), lambda bi, hi, qi: (bi, 0, 0))],
        out_specs=blk(),
        out_shape=jax.ShapeDtypeStruct((b, L, a), BF16),
        compiler_params=_params(("arbitrary", "arbitrary", "arbitrary")),
        name="fox_attention",
    )(q, k, v, ct)


def _pool_kernel(u_ref, w_ref, s_ref, o_ref):
    x = u_ref[0].astype(F32)
    L, w = x.shape
    row = lax.broadcasted_iota(jnp.int32, (L, w), 0)
    group = lax.broadcasted_iota(jnp.int32, (L, w), 1) // POOL_GROUP_DIM

    def shifted(y, s):
        return jnp.where(row >= s, pltpu.roll(y, s, 0), 0.0)

    acc = x
    sel = jnp.zeros_like(x)
    win_lane = jnp.zeros((L, w), F32)
    span = 1
    for gi, win in enumerate(POOL_WINDOWS):
        while span < win:
            acc = acc + shifted(acc, span)
            span *= 2
        sel = jnp.where(group == gi, acc, sel)
        win_lane = jnp.where(group == gi, float(win), win_lane)
    count = jnp.minimum(row.astype(F32) + 1.0, win_lane)
    d = sel / count - x
    y = jnp.dot(d.astype(BF16), w_ref[...], preferred_element_type=F32) * s_ref[...]
    o_ref[0] = y.astype(o_ref.dtype)


def _pool(u, w_bd, scale):
    b, L, w = u.shape
    return pl.pallas_call(
        _pool_kernel,
        grid=(b,),
        in_specs=[pl.BlockSpec((1, L, w), lambda i: (i, 0, 0)),
                  pl.BlockSpec((w, w), lambda i: (0, 0)),
                  pl.BlockSpec((1, w), lambda i: (0, 0))],
        out_specs=pl.BlockSpec((1, L, w), lambda i: (i, 0, 0)),
        out_shape=jax.ShapeDtypeStruct((b, L, w), BF16),
        compiler_params=_params(("arbitrary",)),
        name="pool_mixer",
    )(u, w_bd, scale)


def _group_of(idx, width):
    groups = LANES // SSM_GROUP_DIM
    return (idx >> (width.bit_length() - 1)) & (groups - 1)


def _expand_blockdiag(r_ref, row_inner, col_inner):
    rows, w = r_ref.shape[1:]
    groups = LANES // SSM_GROUP_DIM
    r = lax.broadcasted_iota(jnp.int32, (w, groups * w), 0)
    c = lax.broadcasted_iota(jnp.int32, (w, groups * w), 1)
    shift = col_inner.bit_length() - 1
    outer_c = c >> (shift + groups.bit_length() - 1)
    spread = (outer_c == (r >> shift)) & ((c & (col_inner - 1)) == (r & (col_inner - 1)))
    big = jnp.dot(r_ref[0], spread.astype(BF16), preferred_element_type=F32)
    ri = lax.broadcasted_iota(jnp.int32, (rows, groups * w), 0)
    ci = lax.broadcasted_iota(jnp.int32, (rows, groups * w), 1)
    return jnp.where(_group_of(ri, row_inner) == _group_of(ci, col_inner), big, 0.0).astype(BF16)


def _ssm_kernel(u_ref, rm_ref, rz_ref, ry_ref, coef_ref, y_ref, m_s, wz_s, wy_s, ucat, z_ref, s_ref, *, seqs):
    t = SSM_CHUNK

    @pl.when(pl.program_id(1) == 0)
    def _():
        m_s[...] = _expand_blockdiag(rm_ref, SSM_GROUP_DIM, SSM_GROUP_DIM)
        wz_s[...] = _expand_blockdiag(rz_ref, SSM_GROUP_DIM, SSM_STATE)
        wy_s[...] = _expand_blockdiag(ry_ref, SSM_STATE, SSM_GROUP_DIM)

    nc = u_ref.shape[0] // t
    cps = nc // seqs
    for k in range(t):
        ucat[:, k * LANES:(k + 1) * LANES] = u_ref[pl.ds(k, nc, stride=t), :].astype(BF16)
    u = ucat[...]
    z_ref[...] = jnp.dot(u, wz_s[...], preferred_element_type=F32)
    half = z_ref.shape[1] // 2
    a1 = coef_ref[0, 0:1, :]
    a2 = coef_ref[0, 1:2, :]

    def step(c, states):
        new = []
        for b in range(seqs):
            st = states[b]
            r = b * cps + c
            s_ref[pl.ds(r, 1), :] = st
            sw = jnp.concatenate([st[:, half:], st[:, :half]], axis=1)
            new.append(a1 * st + a2 * sw + z_ref[pl.ds(r, 1), :])
        return tuple(new)

    init = tuple(jnp.zeros((1, 2 * half), F32) for _ in range(seqs))
    lax.fori_loop(0, cps, step, init, unroll=8)
    y = jnp.dot(u, m_s[...], preferred_element_type=F32)
    y = y + jnp.dot(s_ref[...].astype(BF16), wy_s[...], preferred_element_type=F32)
    for k in range(t):
        y_ref[pl.ds(k, nc, stride=t), :] = y[:, k * LANES:(k + 1) * LANES]


def _ssm_core(u, rm, rz, ry, coef, *, seq, seqs_per_block=2):
    n, w = u.shape
    rows = seq * seqs_per_block
    nc = rows // SSM_CHUNK
    kw = SSM_CHUNK * LANES
    p2 = coef.shape[2]
    blk = pl.BlockSpec((rows, LANES), lambda hf, r: (r, hf))
    wspec = lambda arr: pl.BlockSpec((1,) + arr.shape[1:], lambda hf, r: (hf, 0, 0))
    return pl.pallas_call(
        functools.partial(_ssm_kernel, seqs=seqs_per_block),
        grid=(w // LANES, n // rows),
        in_specs=[blk, wspec(rm), wspec(rz), wspec(ry), wspec(coef)],
        out_specs=blk,
        out_shape=jax.ShapeDtypeStruct((n, w), F32),
        scratch_shapes=[pltpu.VMEM((kw, kw), BF16), pltpu.VMEM((kw, p2), BF16), pltpu.VMEM((p2, kw), BF16),
                        pltpu.VMEM((nc, kw), BF16), pltpu.VMEM((nc, p2), F32), pltpu.VMEM((nc, p2), F32)],
        compiler_params=_params(("arbitrary", "arbitrary")),
        name="s5_core",
    )(u, rm, rz, ry, coef)


def _ssm_matrices(a_re, a_im, b_re, b_im, c_re, c_im, d, log_dt):
    t = SSM_CHUNK
    g, p = a_re.shape
    hdim = d.shape[1]
    gl = LANES // hdim
    nb = g // gl
    lam = lax.complex(a_re, a_im)
    lam_dt = lam * jnp.exp(log_dt)[:, None]
    lam_bar = jnp.exp(lam_dt)
    b_bar = ((lam_bar - 1.0) / lam)[:, :, None] * lax.complex(b_re, b_im)
    cmat = lax.complex(c_re, c_im)
    steps = jnp.arange(t + 1, dtype=F32)
    pw = jnp.exp(lam_dt[:, None, :] * steps[None, :, None])
    kern = jnp.einsum('gop,gdp,gpi->gdoi', cmat, pw[:, :t], b_bar).real
    kern = kern.at[:, 0].add(jax.vmap(jnp.diag)(d))
    j = jnp.arange(t)[:, None]
    i = jnp.arange(t)[None, :]
    m = jnp.where((i >= j)[None, :, :, None, None], kern[:, jnp.clip(i - j, 0, t - 1)], 0.0)
    wz = jnp.einsum('gjp,gpi->gjip', pw[:, t - 1 - jnp.arange(t)], b_bar)
    cw = jnp.einsum('gop,gip->giop', cmat, pw[:, 1:t + 1])
    rm = m.reshape(nb, gl, t, t, hdim, hdim).transpose(0, 2, 1, 5, 3, 4).reshape(nb, t * LANES, t * hdim)
    rz = jnp.stack([wz.real, wz.imag], axis=3).reshape(nb, gl, t, hdim, 2, p)
    rz = rz.transpose(0, 2, 1, 3, 4, 5).reshape(nb, t * LANES, 2 * p)
    ry = jnp.stack([cw.real, -cw.imag], axis=1).reshape(nb, gl, 2, t, hdim, p)
    ry = ry.transpose(0, 2, 1, 5, 3, 4).reshape(nb, 2 * gl * p, t * hdim)
    ar = pw[:, t].real.reshape(nb, gl * p)
    ai = pw[:, t].imag.reshape(nb, gl * p)
    coef = jnp.stack([jnp.concatenate([ar, ar], -1), jnp.concatenate([-ai, ai], -1)], axis=1)
    return rm.astype(BF16), rz.astype(BF16), ry.astype(BF16), coef.astype(F32)


def _mixout_kernel(*refs, moe):
    (x_ref, attn_ref, pool_ref, ys_ref, wglu_ref, bglu_ref, ga_ref, gp_ref, gs_ref,
     wo_ref, gpost_ref, gpre_ref) = refs[:12]
    if moe:
        router_ref, xo_ref, h_ref, ri_ref, rg_ref, cnt_ref, cnt_acc = refs[12:]

        @pl.when(pl.program_id(0) == 0)
        def _():
            cnt_acc[...] = jnp.zeros_like(cnt_acc)
    else:
        xo_ref, h_ref = refs[12:]
    a = attn_ref.shape[1]
    pw = pool_ref.shape[1]
    attn_n = _rms(attn_ref[...].astype(F32), ga_ref[...]).astype(BF16)
    pool_n = _rms(pool_ref[...].astype(F32), gp_ref[...]).astype(BF16)
    y = jax.nn.gelu(ys_ref[...])
    gate = jnp.dot(y.astype(BF16), wglu_ref[...], preferred_element_type=F32) + bglu_ref[...]
    ssm = y * jax.nn.sigmoid(gate)
    ssm_n = _rms(ssm, gs_ref[...]).astype(BF16)
    mix = (jnp.dot(attn_n, wo_ref[0:a, :], preferred_element_type=F32)
           + jnp.dot(pool_n, wo_ref[a:a + pw, :], preferred_element_type=F32)
           + jnp.dot(ssm_n, wo_ref[a + pw:, :], preferred_element_type=F32))
    x = x_ref[...] + _rms(mix, gpost_ref[...])
    xo_ref[...] = x
    h = _rms(x, gpre_ref[...])
    if moe:
        _store_slabs(h_ref, 0, h)
    else:
        h_ref[...] = h.astype(h_ref.dtype)
    if moe:
        hi, mid, _ = _split3(h)
        rhi, rmid, _ = _split3(router_ref[...])
        dot_t = lambda p, q: lax.dot_general(p, q, (((1,), (1,)), ((), ())), preferred_element_type=F32)
        logits = dot_t(rhi, hi) + dot_t(rmid, hi) + dot_t(rhi, mid)
        ne, tm = logits.shape
        eidx = lax.broadcasted_iota(jnp.int32, (ne, tm), 0)
        m1 = jnp.max(logits, axis=0, keepdims=True)
        i1 = jnp.min(jnp.where(logits == m1, eidx, ne), axis=0, keepdims=True)
        rest = jnp.where(eidx == i1, -jnp.inf, logits)
        m2 = jnp.max(rest, axis=0, keepdims=True)
        i2 = jnp.min(jnp.where(rest == m2, eidx, ne), axis=0, keepdims=True)
        e2 = jnp.exp(m2 - m1)
        g1 = 1.0 / (1.0 + e2)
        rg_ref[...] = jnp.where(eidx == 0, g1, jnp.where(eidx == 1, e2 * g1, 0.0))
        onehot = ((eidx == i1) | (eidx == i2)).astype(BF16)
        earlier = (lax.broadcasted_iota(jnp.int32, (tm, tm), 0)
                   < lax.broadcasted_iota(jnp.int32, (tm, tm), 1)).astype(BF16)
        before = jnp.dot(onehot, earlier, preferred_element_type=F32) + cnt_acc[:, 0:1]
        r1 = jnp.sum(jnp.where(eidx == i1, before, 0.0), axis=0, keepdims=True).astype(jnp.int32)
        r2 = jnp.sum(jnp.where(eidx == i2, before, 0.0), axis=0, keepdims=True).astype(jnp.int32)
        ri_ref[...] = jnp.where(eidx == 0, i1, jnp.where(eidx == 1, i2,
                                jnp.where(eidx == 2, r1, jnp.where(eidx == 3, r2, 0))))
        cnt_acc[...] += jnp.sum(onehot.astype(F32), axis=1, keepdims=True)
        cnt_ref[...] = cnt_acc[...]


def _mixout(x2, attn, pool, ys, w_glu, b_glu, g_attn, g_pool, g_ssm, w_out, g_post, g_pre,
            router=None, *, tm=512):
    n, d = x2.shape
    moe = router is not None
    row = lambda w: pl.BlockSpec((tm, w), lambda i: (i, 0))
    full = lambda arr: pl.BlockSpec(arr.shape, lambda i: (0, 0))
    ins = [x2, attn, pool, ys, w_glu, b_glu, g_attn, g_pool, g_ssm, w_out, g_post, g_pre]
    in_specs = [row(d), row(attn.shape[1]), row(pool.shape[1]), row(ys.shape[1])] + [full(t) for t in ins[4:]]
    if moe:
        assert d == SUBLANES * LANES
        h_shape = jax.ShapeDtypeStruct((n * SUBLANES, LANES), F32)
        h_spec = pl.BlockSpec((tm * SUBLANES, LANES), lambda i: (i, 0))
    else:
        h_shape, h_spec = jax.ShapeDtypeStruct((n, d), BF16), row(d)
    out_shape = [jax.ShapeDtypeStruct((n, d), F32), h_shape]
    out_specs = [row(d), h_spec]
    scratch = []
    if moe:
        ins.append(router)
        in_specs.append(full(router))
        ne = router.shape[0]
        col = pl.BlockSpec((ne, tm), lambda i: (0, i))
        out_shape += [jax.ShapeDtypeStruct((ne, n), jnp.int32), jax.ShapeDtypeStruct((ne, n), F32),
                      jax.ShapeDtypeStruct((ne, LANES), F32)]
        out_specs += [col, col, pl.BlockSpec((ne, LANES), lambda i: (0, 0))]
        scratch = [pltpu.VMEM((ne, LANES), F32)]
    return pl.pallas_call(
        functools.partial(_mixout_kernel, moe=moe),
        grid=(n // tm,),
        in_specs=in_specs,
        out_specs=tuple(out_specs),
        out_shape=tuple(out_shape),
        scratch_shapes=scratch,
        compiler_params=_params(("arbitrary",)),
        name="mix_out",
    )(*ins)


def _ffn_kernel(h_ref, x_ref, wgu_ref, wd_ref, g_ref, o_ref, acc_ref, *, sub):
    f = pl.program_id(1)

    @pl.when(f == 0)
    def _():
        acc_ref[...] = jnp.zeros_like(acc_ref)

    tf = wd_ref.shape[0]
    for r in range(h_ref.shape[0] // sub):
        rows = pl.ds(r * sub, sub)
        hgu = jnp.dot(h_ref[rows, :], wgu_ref[0], preferred_element_type=F32)
        hg, hu = hgu[:, :tf], hgu[:, tf:]
        act = (hg * jax.nn.sigmoid(hg) * hu).astype(BF16)
        acc_ref[rows, :] += jnp.dot(act, wd_ref[...], preferred_element_type=F32)

    @pl.when(f == pl.num_programs(1) - 1)
    def _():
        o_ref[...] = x_ref[...] + _rms(acc_ref[...], g_ref[...])


def _ffn(h, x2, wg, wu, wd, g_post, *, tm=1024, tf, sub=256):
    n, d = x2.shape
    fdim = wg.shape[1]
    nf = fdim // tf
    wgu = jnp.concatenate([wg.reshape(d, nf, tf), wu.reshape(d, nf, tf)], axis=2).transpose(1, 0, 2)
    row = lambda w: pl.BlockSpec((tm, w), lambda i, f: (i, 0))
    return pl.pallas_call(
        functools.partial(_ffn_kernel, sub=sub),
        grid=(n // tm, nf),
        in_specs=[row(d), row(d),
                  pl.BlockSpec((1, d, 2 * tf), lambda i, f: (f, 0, 0)),
                  pl.BlockSpec((tf, d), lambda i, f: (f, 0)),
                  pl.BlockSpec((1, d), lambda i, f: (0, 0))],
        out_specs=row(d),
        out_shape=jax.ShapeDtypeStruct((n, d), F32),
        scratch_shapes=[pltpu.VMEM((tm, d), F32)],
        compiler_params=_params(("arbitrary", "arbitrary")),
        name="dense_ffn",
    )(h, x2, wgu, wd, g_post)


MOE_TILE = 512


def _moe_tables(counts, n_items):
    tm = MOE_TILE
    counts = counts.astype(jnp.int32)
    ntiles = (counts + tm - 1) // tm
    k = jnp.arange(ntiles.shape[0])
    ends = jnp.sum(jnp.where(k[None, :] <= k[:, None], ntiles[None, :], 0), axis=1)
    starts = ends - ntiles
    w = jnp.arange(n_items, dtype=jnp.int32)
    wc = jnp.minimum(w, ends[-1] - 1)
    e = jnp.sum((wc[:, None] >= ends[None, :]).astype(jnp.int32), axis=1)
    valid = w < ends[-1]
    rowblock = jnp.where(valid, w, n_items)
    nvalid = jnp.where(valid, jnp.clip(counts[e] - (wc - starts[e]) * tm, 0, tm), 0)
    first = ((w == 0) | (e != jnp.roll(e, 1))).astype(jnp.int32)
    return (e, rowblock, nvalid.astype(jnp.int32), first), starts * tm


def _slab(ref, token):
    return ref.at[pl.ds(pl.multiple_of(token * SUBLANES, SUBLANES), SUBLANES)]


def _dispatch_kernel(dest_ref, h_ref, xs_ref, sem, *, n):
    tm = h_ref.shape[0] // SUBLANES
    base = pl.program_id(0) * tm

    def body(r, carry):
        for k in range(2):
            d = dest_ref[k * n + base + r]
            pltpu.make_async_copy(_slab(h_ref, r), _slab(xs_ref, d), sem).start(priority=k)
        return carry

    lax.fori_loop(0, tm, body, 0, unroll=8)
    for k in range(2):
        pltpu.make_async_copy(h_ref, h_ref, sem).wait()


def _dispatch(dest, h, cap_rows, *, tm=512):
    n = h.shape[0] // SUBLANES
    return pl.pallas_call(
        functools.partial(_dispatch_kernel, n=n),
        grid_spec=pltpu.PrefetchScalarGridSpec(
            num_scalar_prefetch=1,
            grid=(n // tm,),
            in_specs=[pl.BlockSpec((tm * SUBLANES, LANES), lambda i, dest: (i, 0))],
            out_specs=pl.BlockSpec(memory_space=pl.ANY),
            scratch_shapes=[pltpu.SemaphoreType.DMA(())]),
        out_shape=jax.ShapeDtypeStruct((cap_rows * SUBLANES, LANES), F32),
        compiler_params=_params(("arbitrary",)),
        name="moe_dispatch",
    )(dest, h)


MOE_SUB = 256


def _moe_up_kernel(ie_ref, rb_ref, nv_ref, first_ref, xs_ref, wg_ref, wu_ref, h_ref, wgu_s):
    w = pl.program_id(1)
    tf = h_ref.shape[1]

    @pl.when(first_ref[w] == 1)
    def _():
        wgu_s[:, :tf] = wg_ref[0].astype(BF16)
        wgu_s[:, tf:] = wu_ref[0].astype(BF16)

    nv = nv_ref[w]

    @pl.when(nv > 0)
    def _():
        for r in range(h_ref.shape[0] // MOE_SUB):
            rows = pl.ds(r * MOE_SUB, MOE_SUB)
            idx = r * MOE_SUB + lax.broadcasted_iota(jnp.int32, (MOE_SUB, 1), 0)
            x = _load_slabs(xs_ref, r * MOE_SUB, MOE_SUB)
            x = jnp.where(idx < nv, x, 0.0).astype(BF16)
            hgu = jnp.dot(x, wgu_s[...], preferred_element_type=F32)
            hg, hu = hgu[:, :tf], hgu[:, tf:]
            h_ref[rows, :] = (hg * jax.nn.sigmoid(hg) * hu).astype(BF16)

    @pl.when(nv == 0)
    def _():
        h_ref[...] = jnp.zeros_like(h_ref)


def _moe_up(tables, xs, wg, wu, *, tf=896):
    tm = MOE_TILE
    cap_rows = xs.shape[0] // SUBLANES
    ne, d, fdim = wg.shape
    n_items = tables[0].shape[0]
    return pl.pallas_call(
        _moe_up_kernel,
        grid_spec=pltpu.PrefetchScalarGridSpec(
            num_scalar_prefetch=4,
            grid=(fdim // tf, n_items),
            in_specs=[pl.BlockSpec((tm * SUBLANES, LANES), lambda f, w, ie, rb, nv, fi: (rb[w], 0)),
                      pl.BlockSpec((1, d, tf), lambda f, w, ie, rb, nv, fi: (ie[w], 0, f)),
                      pl.BlockSpec((1, d, tf), lambda f, w, ie, rb, nv, fi: (ie[w], 0, f))],
            out_specs=pl.BlockSpec((tm, tf), lambda f, w, ie, rb, nv, fi: (rb[w], f)),
            scratch_shapes=[pltpu.VMEM((d, 2 * tf), BF16)]),
        out_shape=jax.ShapeDtypeStruct((cap_rows, fdim), BF16),
        compiler_params=_params(("arbitrary", "arbitrary")),
        name="moe_up",
    )(*tables, xs, wg, wu)


def _moe_down_kernel(ie_ref, rb_ref, nv_ref, first_ref, h_ref, wd_hbm, y_ref, wd_s, stage, sem):
    w = pl.program_id(0)
    chunk = stage.shape[1]
    nchunks = wd_s.shape[0] // chunk

    @pl.when(first_ref[w] == 1)
    def _():
        def copy(c):
            return pltpu.make_async_copy(wd_hbm.at[ie_ref[w], pl.ds(c * chunk, chunk)], stage.at[c % 2],
                                         sem.at[c % 2])
        copy(0).start()
        for c in range(nchunks):
            if c + 1 < nchunks:
                copy(c + 1).start()
            copy(c).wait()
            wd_s[pl.ds(c * chunk, chunk), :] = stage[c % 2].astype(BF16)

    @pl.when(nv_ref[w] > 0)
    def _():
        for r in range(h_ref.shape[0] // MOE_SUB):
            y = jnp.dot(h_ref[pl.ds(r * MOE_SUB, MOE_SUB), :], wd_s[...], preferred_element_type=F32)
            _store_slabs(y_ref, r * MOE_SUB, y)

    @pl.when(nv_ref[w] == 0)
    def _():
        y_ref[...] = jnp.zeros_like(y_ref)


def _moe_down(tables, hid, wd, *, chunk=512):
    tm = MOE_TILE
    cap_rows, fdim = hid.shape
    d = wd.shape[2]
    n_items = tables[0].shape[0]
    assert d == SUBLANES * LANES and fdim % chunk == 0
    return pl.pallas_call(
        _moe_down_kernel,
        grid_spec=pltpu.PrefetchScalarGridSpec(
            num_scalar_prefetch=4,
            grid=(n_items,),
            in_specs=[pl.BlockSpec((tm, fdim), lambda w, ie, rb, nv, fi: (rb[w], 0)),
                      pl.BlockSpec(memory_space=pl.ANY)],
            out_specs=pl.BlockSpec((tm * SUBLANES, LANES), lambda w, ie, rb, nv, fi: (rb[w], 0)),
            scratch_shapes=[pltpu.VMEM((fdim, d), BF16), pltpu.VMEM((2, chunk, d), F32),
                            pltpu.SemaphoreType.DMA((2,))]),
        out_shape=jax.ShapeDtypeStruct((cap_rows * SUBLANES, LANES), F32),
        compiler_params=_params(("arbitrary",)),
        name="moe_down",
    )(*tables, hid, wd)


def _combine_kernel(dest_ref, x_ref, gate_ref, g_ref, ys_ref, o_ref, buf, sem, *, n):
    tm = x_ref.shape[0]
    i = pl.program_id(0)

    def issue(tile, slot):
        def body(r, carry):
            for k in range(2):
                d = dest_ref[k * n + tile * tm + r]
                pltpu.make_async_copy(_slab(ys_ref, d), _slab(buf.at[slot, k], r),
                                      sem.at[slot]).start(priority=k)
            return carry
        lax.fori_loop(0, tm, body, 0, unroll=8)

    @pl.when(i == 0)
    def _():
        issue(0, 0)

    @pl.when(i + 1 < pl.num_programs(0))
    def _():
        issue(i + 1, (i + 1) % 2)

    slot = i % 2
    for k in range(2):
        pltpu.make_async_copy(buf.at[slot, k], buf.at[slot, k], sem.at[slot]).wait()
    gates = gate_ref[...].T
    f = (gates[:, 0:1] * _load_slabs(buf.at[slot, 0], 0, tm)
         + gates[:, 1:2] * _load_slabs(buf.at[slot, 1], 0, tm))
    o_ref[...] = x_ref[...] + _rms(f, g_ref[...])


def _combine(dest, x2, gates, g_post, ys, *, tm=256):
    n, d = x2.shape
    return pl.pallas_call(
        functools.partial(_combine_kernel, n=n),
        grid_spec=pltpu.PrefetchScalarGridSpec(
            num_scalar_prefetch=1,
            grid=(n // tm,),
            in_specs=[pl.BlockSpec((tm, d), lambda i, dest: (i, 0)),
                      pl.BlockSpec((gates.shape[0], tm), lambda i, dest: (0, i)),
                      pl.BlockSpec((1, d), lambda i, dest: (0, 0)),
                      pl.BlockSpec(memory_space=pl.ANY)],
            out_specs=pl.BlockSpec((tm, d), lambda i, dest: (i, 0)),
            scratch_shapes=[pltpu.VMEM((2, 2, tm * SUBLANES, LANES), F32), pltpu.SemaphoreType.DMA((2,))]),
        out_shape=jax.ShapeDtypeStruct((n, d), F32),
        compiler_params=_params(("arbitrary",)),
        name="moe_combine",
    )(dest, x2, gates, g_post, ys)


def _moe(h, x2, route_i, route_g, counts, wg, wu, wd, g_post):
    n = x2.shape[0]
    n_items = 2 * n // MOE_TILE + N_EXPERTS
    tables, base = _moe_tables(counts[:, 0], n_items)
    base_of = lambda e: jnp.sum(jnp.where(e[None, :] == jnp.arange(N_EXPERTS)[:, None], base[:, None], 0), axis=0)
    dest = jnp.concatenate([base_of(route_i[0]) + route_i[2], base_of(route_i[1]) + route_i[3]])
    xs = _dispatch(dest, h, (n_items + 1) * MOE_TILE)
    hid = _moe_up(tables, xs, wg, wu)
    ys = _moe_down(tables, hid, wd)
    return _combine(dest, x2, route_g, g_post, ys)


def _row(v):
    return v.reshape(1, -1).astype(F32)


def kernel(x, norm_mix_pre, norm_mix_post, norm_ffn_pre, norm_ffn_post, w_in, b_forget, pool_w, pool_scale, ssm_a_re, ssm_a_im, ssm_b_re, ssm_b_im, ssm_c_re, ssm_c_im, ssm_d, ssm_log_dt, ssm_w_glu, ssm_b_glu, branch_norm_attn, branch_norm_pool, branch_norm_ssm, w_out, ffn_w_gate, ffn_w_up, ffn_w_down, moe_router, moe_w_gate, moe_w_up, moe_w_down):
    b, L, d = x.shape
    depth = w_in.shape[0]
    n = b * L
    a = ATTN_WIDTH
    f_off = 3 * a
    p_off = f_off + ATTN_HEADS
    attn_tile = 512
    x2 = x.reshape(n, d)
    for i in range(depth):
        w = w_in[i]
        w_main = jnp.concatenate([w[:, :f_off], w[:, p_off:]], axis=1).astype(BF16)
        w_f = jnp.pad(w[:, f_off:p_off], ((0, 0), (0, LANES - ATTN_HEADS))).astype(BF16)
        b_f = jnp.pad(b_forget[i], (0, LANES - ATTN_HEADS)).reshape(1, LANES)
        q, k, v, up, us, c = _inproj(x2, _row(norm_mix_pre[i]), w_main, w_f, b_f, seq=L)

        ct = c.reshape(b, L, ATTN_HEADS).transpose(0, 2, 1).reshape(b, ATTN_HEADS * (L // attn_tile), attn_tile)
        attn = _attention(q.reshape(b, L, a), k.reshape(b, L, a), v.reshape(b, L, a),
                          ct, tq=attn_tile).reshape(n, a)

        pool = _pool(up.reshape(b, L, POOL_WIDTH),
                     jax.scipy.linalg.block_diag(*pool_w[i]).astype(BF16),
                     _row(pool_scale[i])).reshape(n, POOL_WIDTH)

        mats = _ssm_matrices(ssm_a_re[i], ssm_a_im[i], ssm_b_re[i], ssm_b_im[i],
                             ssm_c_re[i], ssm_c_im[i], ssm_d[i], ssm_log_dt[i])
        ys = _ssm_core(us, *mats, seq=L)

        moe = i % 2 == 1
        j = i // 2
        router = moe_router[j].T if moe else None
        res = _mixout(x2, attn, pool, ys, ssm_w_glu[i].astype(BF16), _row(ssm_b_glu[i]),
                      _row(branch_norm_attn[i]), _row(branch_norm_pool[i]), _row(branch_norm_ssm[i]),
                      w_out[i].astype(BF16), _row(norm_mix_post[i]), _row(norm_ffn_pre[i]), router)
        if moe:
            x2, h, route_i, route_g, counts = res
            x2 = _moe(h, x2, route_i, route_g, counts, moe_w_gate[j], moe_w_up[j], moe_w_down[j],
                      _row(norm_ffn_post[i]))
        else:
            x2, h = res
            x2 = _ffn(h, x2, ffn_w_gate[j].astype(BF16), ffn_w_up[j].astype(BF16),
                      ffn_w_down[j].astype(BF16), _row(norm_ffn_post[i]), tf=1408)
    return x2.reshape(b, L, d)
```

```python
import functools

import jax
import jax.numpy as jnp
from jax import lax
from jax.experimental import pallas as pl
from jax.experimental.pallas import tpu as pltpu

F32 = jnp.float32
BF16 = jnp.bfloat16

RMS_EPS = 1e-6
NEG_INF = -1e30
LOG2E = 1.4426950408889634

ATTN_HEADS = 8
ATTN_HEAD_DIM = 64
ATTN_WIDTH = ATTN_HEADS * ATTN_HEAD_DIM
POOL_WINDOWS = (2, 4, 8, 16)
POOL_GROUP_DIM = 64
POOL_WIDTH = len(POOL_WINDOWS) * POOL_GROUP_DIM
SSM_GROUPS = 16
SSM_GROUP_DIM = 16
SSM_STATE = 64
SSM_WIDTH = SSM_GROUPS * SSM_GROUP_DIM
SSM_CHUNK = 8
N_EXPERTS = 8

LANES = 128
SUBLANES = 8
VMEM_LIMIT = 56 * 1024 * 1024


def _params(sem):
    return pltpu.CompilerParams(dimension_semantics=sem, vmem_limit_bytes=VMEM_LIMIT)


def _rms(x, g):
    return x * lax.rsqrt(jnp.mean(x * x, axis=-1, keepdims=True) + RMS_EPS) * g


def _load_slabs(ref, row0, rows):
    return jnp.concatenate([ref[pl.ds(row0 * SUBLANES + s, rows, stride=SUBLANES), :]
                            for s in range(SUBLANES)], axis=1)


def _store_slabs(ref, row0, val):
    for s in range(SUBLANES):
        ref[pl.ds(row0 * SUBLANES + s, val.shape[0], stride=SUBLANES), :] = val[:, s * LANES:(s + 1) * LANES]


def _split3(x):
    hi = x.astype(BF16)
    r = x - hi.astype(F32)
    mid = r.astype(BF16)
    lo = (r - mid.astype(F32)).astype(BF16)
    return hi, mid, lo


def _inproj_kernel(x_ref, g_ref, wm_ref, wf_ref, bf_ref,
                   q_ref, k_ref, v_ref, up_ref, us_ref, c_ref, carry_ref, *, tiles_per_seq):
    @pl.when(pl.program_id(0) % tiles_per_seq == 0)
    def _():
        carry_ref[...] = jnp.zeros_like(carry_ref)

    tm = x_ref.shape[0]
    h = _rms(x_ref[...], g_ref[...]).astype(BF16)
    proj = jnp.dot(h, wm_ref[...], preferred_element_type=F32)
    a = ATTN_WIDTH
    q_ref[...] = (proj[:, 0:a] * (LOG2E * ATTN_HEAD_DIM ** -0.5)).astype(BF16)
    k_ref[...] = proj[:, a:2 * a].astype(BF16)
    v_ref[...] = proj[:, 2 * a:3 * a].astype(BF16)
    up_ref[...] = proj[:, 3 * a:3 * a + POOL_WIDTH].astype(BF16)
    us_ref[...] = proj[:, 3 * a + POOL_WIDTH:]

    z = jnp.dot(h, wf_ref[...], preferred_element_type=F32) + bf_ref[...]
    logf = jnp.minimum(z, 0.0) - jnp.log(1.0 + jnp.exp(-jnp.abs(z)))
    row = lax.broadcasted_iota(jnp.int32, (tm, tm), 0)
    col = lax.broadcasted_iota(jnp.int32, (tm, tm), 1)
    tri = (row >= col).astype(BF16)
    hi, mid, _ = _split3(logf)
    c = (jnp.dot(tri, hi, preferred_element_type=F32)
         + jnp.dot(tri, mid, preferred_element_type=F32)) + carry_ref[...]
    c_ref[...] = c[:, :ATTN_HEADS]
    carry_ref[...] = c[tm - 1:tm, :]


def _inproj(x2, g, w_main, w_f, b_f, *, seq, tm=512):
    n, d = x2.shape
    nm = w_main.shape[1]
    a = ATTN_WIDTH
    out_shape = (
        jax.ShapeDtypeStruct((n, a), BF16), jax.ShapeDtypeStruct((n, a), BF16),
        jax.ShapeDtypeStruct((n, a), BF16), jax.ShapeDtypeStruct((n, POOL_WIDTH), BF16),
        jax.ShapeDtypeStruct((n, SSM_WIDTH), F32), jax.ShapeDtypeStruct((n, ATTN_HEADS), F32))
    row = lambda w: pl.BlockSpec((tm, w), lambda i: (i, 0))
    full = lambda r, c: pl.BlockSpec((r, c), lambda i: (0, 0))
    return pl.pallas_call(
        functools.partial(_inproj_kernel, tiles_per_seq=seq // tm),
        grid=(n // tm,),
        in_specs=[row(d), full(1, d), full(d, nm), full(d, LANES), full(1, LANES)],
        out_specs=(row(a), row(a), row(a), row(POOL_WIDTH), row(SSM_WIDTH), row(ATTN_HEADS)),
        out_shape=out_shape,
        scratch_shapes=[pltpu.VMEM((1, LANES), F32)],
        compiler_params=_params(("arbitrary",)),
        name="inproj",
    )(x2, g, w_main, w_f, b_f)


HEADS_PER_BLOCK = LANES // ATTN_HEAD_DIM


def _attn_kernel(q_ref, k_ref, v_ref, ct_ref, o_ref, *, tq, tk):
    nlb = q_ref.shape[2] // LANES
    hp = pl.program_id(1)
    qi = pl.program_id(2)
    nkb = k_ref.shape[1] // tk
    lane = lax.broadcasted_iota(jnp.int32, (1, LANES), 1)
    in_head = [(lane // ATTN_HEAD_DIM) == hh for hh in range(HEADS_PER_BLOCK)]
    chains = [(lb, hh) for lb in range(nlb) for hh in range(HEADS_PER_BLOCK)]
    qm, crow, cref = [], [], []
    for lb, hh in chains:
        q2 = q_ref[0, :, lb * LANES:(lb + 1) * LANES]
        qm.append(jnp.where(in_head[hh], q2, jnp.zeros_like(q2)))
        crow.append(((hp * nlb + lb) * HEADS_PER_BLOCK + hh) * nkb)
        cref.append(ct_ref[0, pl.ds(crow[-1] + qi, 1), :][:, 0:1])

    def step(kb, carry, masked):
        new = []
        for ci, (lb, hh) in enumerate(chains):
            kblk = k_ref[0, pl.ds(kb * tk, tk), lb * LANES:(lb + 1) * LANES]
            vblk = v_ref[0, pl.ds(kb * tk, tk), lb * LANES:(lb + 1) * LANES]
            m, acc = carry[ci]
            s = lax.dot_general(qm[ci], kblk, (((1,), (1,)), ((), ())), preferred_element_type=F32)
            s = s - (ct_ref[0, pl.ds(crow[ci] + kb, 1), :] - cref[ci]) * LOG2E
            if masked:
                r = lax.broadcasted_iota(jnp.int32, (tq, tk), 0)
                cidx = lax.broadcasted_iota(jnp.int32, (tq, tk), 1)
                s = jnp.where(cidx <= r, s, NEG_INF)
            m_new = jnp.maximum(m, jnp.max(s, axis=1, keepdims=True))
            p = jnp.exp2(s - m_new)
            vsel = jnp.where(in_head[hh], vblk, jnp.ones_like(vblk))
            acc = jnp.exp2(m - m_new) * acc + jnp.dot(p.astype(BF16), vsel, preferred_element_type=F32)
            new.append((m_new, acc))
        return tuple(new)

    init = tuple((jnp.full((tq, 1), NEG_INF, F32), jnp.zeros((tq, LANES), F32)) for _ in chains)
    carry = lax.fori_loop(0, qi, lambda kb, cr: step(kb, cr, False), init)
    carry = step(qi, carry, True)
    for lb in range(nlb):
        out = jnp.zeros((tq, LANES), F32)
        for hh in range(HEADS_PER_BLOCK):
            acc = carry[lb * HEADS_PER_BLOCK + hh][1]
            out = jnp.where(in_head[hh], acc / pltpu.roll(acc, ATTN_HEAD_DIM, 1), out)
        o_ref[0, :, lb * LANES:(lb + 1) * LANES] = out.astype(o_ref.dtype)


def _attention(q, k, v, ct, *, tq, lane_blocks=2):
    b, L, a = q.shape
    tk = ct.shape[2]
    assert tq == tk, "the diagonal key block of query tile i must be key block i"
    w = lane_blocks * LANES
    blk = lambda: pl.BlockSpec((1, tq, w), lambda bi, hi, qi: (bi, qi, hi))
    seq = lambda: pl.BlockSpec((1, L, w), lambda bi, hi, qi: (bi, 0, hi))
    return pl.pallas_call(
        functools.partial(_attn_kernel, tq=tq, tk=tk),
        grid=(b, a // w, L // tq),
        in_specs=[blk(), seq(), seq(),
                  pl.BlockSpec((1, ct.shape[1], tk), lambda bi, hi, qi: (bi, 0, 0))],
        out_specs=blk(),
        out_shape=jax.ShapeDtypeStruct((b, L, a), BF16),
        compiler_params=_params(("arbitrary", "arbitrary", "arbitrary")),
        name="fox_attention",
    )(q, k, v, ct)


def _pool_kernel(u_ref, w_ref, s_ref, o_ref):
    x = u_ref[0].astype(F32)
    L, w = x.shape
    row = lax.broadcasted_iota(jnp.int32, (L, w), 0)
    group = lax.broadcasted_iota(jnp.int32, (L, w), 1) // POOL_GROUP_DIM

    def shifted(y, s):
        return jnp.where(row >= s, pltpu.roll(y, s, 0), 0.0)

    acc = x
    sel = jnp.zeros_like(x)
    win_lane = jnp.zeros((L, w), F32)
    span = 1
    for gi, win in enumerate(POOL_WINDOWS):
        while span < win:
            acc = acc + shifted(acc, span)
            span *= 2
        sel = jnp.where(group == gi, acc, sel)
        win_lane = jnp.where(group == gi, float(win), win_lane)
    count = jnp.minimum(row.astype(F32) + 1.0, win_lane)
    d = sel / count - x
    y = jnp.dot(d.astype(BF16), w_ref[...], preferred_element_type=F32) * s_ref[...]
    o_ref[0] = y.astype(o_ref.dtype)


def _pool(u, w_bd, scale):
    b, L, w = u.shape
    return pl.pallas_call(
        _pool_kernel,
        grid=(b,),
        in_specs=[pl.BlockSpec((1, L, w), lambda i: (i, 0, 0)),
                  pl.BlockSpec((w, w), lambda i: (0, 0)),
                  pl.BlockSpec((1, w), lambda i: (0, 0))],
        out_specs=pl.BlockSpec((1, L, w), lambda i: (i, 0, 0)),
        out_shape=jax.ShapeDtypeStruct((b, L, w), BF16),
        compiler_params=_params(("arbitrary",)),
        name="pool_mixer",
    )(u, w_bd, scale)


def _group_of(idx, width):
    groups = LANES // SSM_GROUP_DIM
    return (idx >> (width.bit_length() - 1)) & (groups - 1)


def _expand_blockdiag(r_ref, row_inner, col_inner):
    rows, w = r_ref.shape[1:]
    groups = LANES // SSM_GROUP_DIM
    r = lax.broadcasted_iota(jnp.int32, (w, groups * w), 0)
    c = lax.broadcasted_iota(jnp.int32, (w, groups * w), 1)
    shift = col_inner.bit_length() - 1
    outer_c = c >> (shift + groups.bit_length() - 1)
    spread = (outer_c == (r >> shift)) & ((c & (col_inner - 1)) == (r & (col_inner - 1)))
    big = jnp.dot(r_ref[0], spread.astype(BF16), preferred_element_type=F32)
    ri = lax.broadcasted_iota(jnp.int32, (rows, groups * w), 0)
    ci = lax.broadcasted_iota(jnp.int32, (rows, groups * w), 1)
    return jnp.where(_group_of(ri, row_inner) == _group_of(ci, col_inner), big, 0.0).astype(BF16)


def _ssm_kernel(u_ref, rm_ref, rz_ref, ry_ref, coef_ref, y_ref, m_s, wz_s, wy_s, ucat, z_ref, s_ref, *, seqs):
    t = SSM_CHUNK

    @pl.when(pl.program_id(1) == 0)
    def _():
        m_s[...] = _expand_blockdiag(rm_ref, SSM_GROUP_DIM, SSM_GROUP_DIM)
        wz_s[...] = _expand_blockdiag(rz_ref, SSM_GROUP_DIM, SSM_STATE)
        wy_s[...] = _expand_blockdiag(ry_ref, SSM_STATE, SSM_GROUP_DIM)

    nc = u_ref.shape[0] // t
    cps = nc // seqs
    for k in range(t):
        ucat[:, k * LANES:(k + 1) * LANES] = u_ref[pl.ds(k, nc, stride=t), :].astype(BF16)
    u = ucat[...]
    z_ref[...] = jnp.dot(u, wz_s[...], preferred_element_type=F32)
    half = z_ref.shape[1] // 2
    a1 = coef_ref[0, 0:1, :]
    a2 = coef_ref[0, 1:2, :]

    def step(c, states):
        new = []
        for b in range(seqs):
            st = states[b]
            r = b * cps + c
            s_ref[pl.ds(r, 1), :] = st
            sw = jnp.concatenate([st[:, half:], st[:, :half]], axis=1)
            new.append(a1 * st + a2 * sw + z_ref[pl.ds(r, 1), :])
        return tuple(new)

    init = tuple(jnp.zeros((1, 2 * half), F32) for _ in range(seqs))
    lax.fori_loop(0, cps, step, init, unroll=8)
    y = jnp.dot(u, m_s[...], preferred_element_type=F32)
    y = y + jnp.dot(s_ref[...].astype(BF16), wy_s[...], preferred_element_type=F32)
    for k in range(t):
        y_ref[pl.ds(k, nc, stride=t), :] = y[:, k * LANES:(k + 1) * LANES]


def _ssm_core(u, rm, rz, ry, coef, *, seq, seqs_per_block=2):
    n, w = u.shape
    rows = seq * seqs_per_block
    nc = rows // SSM_CHUNK
    kw = SSM_CHUNK * LANES
    p2 = coef.shape[2]
    blk = pl.BlockSpec((rows, LANES), lambda hf, r: (r, hf))
    wspec = lambda arr: pl.BlockSpec((1,) + arr.shape[1:], lambda hf, r: (hf, 0, 0))
    return pl.pallas_call(
        functools.partial(_ssm_kernel, seqs=seqs_per_block),
        grid=(w // LANES, n // rows),
        in_specs=[blk, wspec(rm), wspec(rz), wspec(ry), wspec(coef)],
        out_specs=blk,
        out_shape=jax.ShapeDtypeStruct((n, w), F32),
        scratch_shapes=[pltpu.VMEM((kw, kw), BF16), pltpu.VMEM((kw, p2), BF16), pltpu.VMEM((p2, kw), BF16),
                        pltpu.VMEM((nc, kw), BF16), pltpu.VMEM((nc, p2), F32), pltpu.VMEM((nc, p2), F32)],
        compiler_params=_params(("arbitrary", "arbitrary")),
        name="s5_core",
    )(u, rm, rz, ry, coef)


def _ssm_matrices(a_re, a_im, b_re, b_im, c_re, c_im, d, log_dt):
    t = SSM_CHUNK
    g, p = a_re.shape
    hdim = d.shape[1]
    gl = LANES // hdim
    nb = g // gl
    lam = lax.complex(a_re, a_im)
    lam_dt = lam * jnp.exp(log_dt)[:, None]
    lam_bar = jnp.exp(lam_dt)
    b_bar = ((lam_bar - 1.0) / lam)[:, :, None] * lax.complex(b_re, b_im)
    cmat = lax.complex(c_re, c_im)
    steps = jnp.arange(t + 1, dtype=F32)
    pw = jnp.exp(lam_dt[:, None, :] * steps[None, :, None])
    kern = jnp.einsum('gop,gdp,gpi->gdoi', cmat, pw[:, :t], b_bar).real
    kern = kern.at[:, 0].add(jax.vmap(jnp.diag)(d))
    j = jnp.arange(t)[:, None]
    i = jnp.arange(t)[None, :]
    m = jnp.where((i >= j)[None, :, :, None, None], kern[:, jnp.clip(i - j, 0, t - 1)], 0.0)
    wz = jnp.einsum('gjp,gpi->gjip', pw[:, t - 1 - jnp.arange(t)], b_bar)
    cw = jnp.einsum('gop,gip->giop', cmat, pw[:, 1:t + 1])
    rm = m.reshape(nb, gl, t, t, hdim, hdim).transpose(0, 2, 1, 5, 3, 4).reshape(nb, t * LANES, t * hdim)
    rz = jnp.stack([wz.real, wz.imag], axis=3).reshape(nb, gl, t, hdim, 2, p)
    rz = rz.transpose(0, 2, 1, 3, 4, 5).reshape(nb, t * LANES, 2 * p)
    ry = jnp.stack([cw.real, -cw.imag], axis=1).reshape(nb, gl, 2, t, hdim, p)
    ry = ry.transpose(0, 2, 1, 5, 3, 4).reshape(nb, 2 * gl * p, t * hdim)
    ar = pw[:, t].real.reshape(nb, gl * p)
    ai = pw[:, t].imag.reshape(nb, gl * p)
    coef = jnp.stack([jnp.concatenate([ar, ar], -1), jnp.concatenate([-ai, ai], -1)], axis=1)
    return rm.astype(BF16), rz.astype(BF16), ry.astype(BF16), coef.astype(F32)


def _mixout_kernel(*refs, moe):
    (x_ref, attn_ref, pool_ref, ys_ref, wglu_ref, bglu_ref, ga_ref, gp_ref, gs_ref,
     wo_ref, gpost_ref, gpre_ref) = refs[:12]
    if moe:
        router_ref, xo_ref, h_ref, ri_ref, rg_ref, cnt_ref, cnt_acc = refs[12:]

        @pl.when(pl.program_id(0) == 0)
        def _():
            cnt_acc[...] = jnp.zeros_like(cnt_acc)
    else:
        xo_ref, h_ref = refs[12:]
    a = attn_ref.shape[1]
    pw = pool_ref.shape[1]
    attn_n = _rms(attn_ref[...].astype(F32), ga_ref[...]).astype(BF16)
    pool_n = _rms(pool_ref[...].astype(F32), gp_ref[...]).astype(BF16)
    y = jax.nn.gelu(ys_ref[...])
    gate = jnp.dot(y.astype(BF16), wglu_ref[...], preferred_element_type=F32) + bglu_ref[...]
    ssm = y * jax.nn.sigmoid(gate)
    ssm_n = _rms(ssm, gs_ref[...]).astype(BF16)
    mix = (jnp.dot(attn_n, wo_ref[0:a, :], preferred_element_type=F32)
           + jnp.dot(pool_n, wo_ref[a:a + pw, :], preferred_element_type=F32)
           + jnp.dot(ssm_n, wo_ref[a + pw:, :], preferred_element_type=F32))
    x = x_ref[...] + _rms(mix, gpost_ref[...])
    xo_ref[...] = x
    h = _rms(x, gpre_ref[...])
    if moe:
        _store_slabs(h_ref, 0, h)
    else:
        h_ref[...] = h.astype(h_ref.dtype)
    if moe:
        hi, mid, _ = _split3(h)
        rhi, rmid, _ = _split3(router_ref[...])
        dot_t = lambda p, q: lax.dot_general(p, q, (((1,), (1,)), ((), ())), preferred_element_type=F32)
        logits = dot_t(rhi, hi) + dot_t(rmid, hi) + dot_t(rhi, mid)
        ne, tm = logits.shape
        eidx = lax.broadcasted_iota(jnp.int32, (ne, tm), 0)
        m1 = jnp.max(logits, axis=0, keepdims=True)
        i1 = jnp.min(jnp.where(logits == m1, eidx, ne), axis=0, keepdims=True)
        rest = jnp.where(eidx == i1, -jnp.inf, logits)
        m2 = jnp.max(rest, axis=0, keepdims=True)
        i2 = jnp.min(jnp.where(rest == m2, eidx, ne), axis=0, keepdims=True)
        e2 = jnp.exp(m2 - m1)
        g1 = 1.0 / (1.0 + e2)
        rg_ref[...] = jnp.where(eidx == 0, g1, jnp.where(eidx == 1, e2 * g1, 0.0))
        onehot = ((eidx == i1) | (eidx == i2)).astype(BF16)
        earlier = (lax.broadcasted_iota(jnp.int32, (tm, tm), 0)
                   < lax.broadcasted_iota(jnp.int32, (tm, tm), 1)).astype(BF16)
        before = jnp.dot(onehot, earlier, preferred_element_type=F32) + cnt_acc[:, 0:1]
        r1 = jnp.sum(jnp.where(eidx == i1, before, 0.0), axis=0, keepdims=True).astype(jnp.int32)
        r2 = jnp.sum(jnp.where(eidx == i2, before, 0.0), axis=0, keepdims=True).astype(jnp.int32)
        ri_ref[...] = jnp.where(eidx == 0, i1, jnp.where(eidx == 1, i2,
                                jnp.where(eidx == 2, r1, jnp.where(eidx == 3, r2, 0))))
        cnt_acc[...] += jnp.sum(onehot.astype(F32), axis=1, keepdims=True)
        cnt_ref[...] = cnt_acc[...]


def _mixout(x2, attn, pool, ys, w_glu, b_glu, g_attn, g_pool, g_ssm, w_out, g_post, g_pre,
            router=None, *, tm=512):
    n, d = x2.shape
    moe = router is not None
    row = lambda w: pl.BlockSpec((tm, w), lambda i: (i, 0))
    full = lambda arr: pl.BlockSpec(arr.shape, lambda i: (0, 0))
    ins = [x2, attn, pool, ys, w_glu, b_glu, g_attn, g_pool, g_ssm, w_out, g_post, g_pre]
    in_specs = [row(d), row(attn.shape[1]), row(pool.shape[1]), row(ys.shape[1])] + [full(t) for t in ins[4:]]
    if moe:
        assert d == SUBLANES * LANES
        h_shape = jax.ShapeDtypeStruct((n * SUBLANES, LANES), F32)
        h_spec = pl.BlockSpec((tm * SUBLANES, LANES), lambda i: (i, 0))
    else:
        h_shape, h_spec = jax.ShapeDtypeStruct((n, d), BF16), row(d)
    out_shape = [jax.ShapeDtypeStruct((n, d), F32), h_shape]
    out_specs = [row(d), h_spec]
    scratch = []
    if moe:
        ins.append(router)
        in_specs.append(full(router))
        ne = router.shape[0]
        col = pl.BlockSpec((ne, tm), lambda i: (0, i))
        out_shape += [jax.ShapeDtypeStruct((ne, n), jnp.int32), jax.ShapeDtypeStruct((ne, n), F32),
                      jax.ShapeDtypeStruct((ne, LANES), F32)]
        out_specs += [col, col, pl.BlockSpec((ne, LANES), lambda i: (0, 0))]
        scratch = [pltpu.VMEM((ne, LANES), F32)]
    return pl.pallas_call(
        functools.partial(_mixout_kernel, moe=moe),
        grid=(n // tm,),
        in_specs=in_specs,
        out_specs=tuple(out_specs),
        out_shape=tuple(out_shape),
        scratch_shapes=scratch,
        compiler_params=_params(("arbitrary",)),
        name="mix_out",
    )(*ins)


def _ffn_kernel(h_ref, x_ref, wgu_ref, wd_ref, g_ref, o_ref, acc_ref, *, sub):
    f = pl.program_id(1)

    @pl.when(f == 0)
    def _():
        acc_ref[...] = jnp.zeros_like(acc_ref)

    tf = wd_ref.shape[0]
    for r in range(h_ref.shape[0] // sub):
        rows = pl.ds(r * sub, sub)
        hgu = jnp.dot(h_ref[rows, :], wgu_ref[0], preferred_element_type=F32)
        hg, hu = hgu[:, :tf], hgu[:, tf:]
        act = (hg * jax.nn.sigmoid(hg) * hu).astype(BF16)
        acc_ref[rows, :] += jnp.dot(act, wd_ref[...], preferred_element_type=F32)

    @pl.when(f == pl.num_programs(1) - 1)
    def _():
        o_ref[...] = x_ref[...] + _rms(acc_ref[...], g_ref[...])


def _ffn(h, x2, wg, wu, wd, g_post, *, tm=1024, tf, sub=256):
    n, d = x2.shape
    fdim = wg.shape[1]
    nf = fdim // tf
    wgu = jnp.concatenate([wg.reshape(d, nf, tf), wu.reshape(d, nf, tf)], axis=2).transpose(1, 0, 2)
    row = lambda w: pl.BlockSpec((tm, w), lambda i, f: (i, 0))
    return pl.pallas_call(
        functools.partial(_ffn_kernel, sub=sub),
        grid=(n // tm, nf),
        in_specs=[row(d), row(d),
                  pl.BlockSpec((1, d, 2 * tf), lambda i, f: (f, 0, 0)),
                  pl.BlockSpec((tf, d), lambda i, f: (f, 0)),
                  pl.BlockSpec((1, d), lambda i, f: (0, 0))],
        out_specs=row(d),
        out_shape=jax.ShapeDtypeStruct((n, d), F32),
        scratch_shapes=[pltpu.VMEM((tm, d), F32)],
        compiler_params=_params(("arbitrary", "arbitrary")),
        name="dense_ffn",
    )(h, x2, wgu, wd, g_post)


MOE_TILE = 512


def _moe_tables(counts, n_items):
    tm = MOE_TILE
    counts = counts.astype(jnp.int32)
    ntiles = (counts + tm - 1) // tm
    k = jnp.arange(ntiles.shape[0])
    ends = jnp.sum(jnp.where(k[None, :] <= k[:, None], ntiles[None, :], 0), axis=1)
    starts = ends - ntiles
    w = jnp.arange(n_items, dtype=jnp.int32)
    wc = jnp.minimum(w, ends[-1] - 1)
    e = jnp.sum((wc[:, None] >= ends[None, :]).astype(jnp.int32), axis=1)
    valid = w < ends[-1]
    rowblock = jnp.where(valid, w, n_items)
    nvalid = jnp.where(valid, jnp.clip(counts[e] - (wc - starts[e]) * tm, 0, tm), 0)
    first = ((w == 0) | (e != jnp.roll(e, 1))).astype(jnp.int32)
    return (e, rowblock, nvalid.astype(jnp.int32), first), starts * tm


def _slab(ref, token):
    return ref.at[pl.ds(pl.multiple_of(token * SUBLANES, SUBLANES), SUBLANES)]


def _dispatch_kernel(dest_ref, h_ref, xs_ref, sem, *, n):
    tm = h_ref.shape[0] // SUBLANES
    base = pl.program_id(0) * tm

    def body(r, carry):
        for k in range(2):
            d = dest_ref[k * n + base + r]
            pltpu.make_async_copy(_slab(h_ref, r), _slab(xs_ref, d), sem).start(priority=k)
        return carry

    lax.fori_loop(0, tm, body, 0, unroll=8)
    for k in range(2):
        pltpu.make_async_copy(h_ref, h_ref, sem).wait()


def _dispatch(dest, h, cap_rows, *, tm=512):
    n = h.shape[0] // SUBLANES
    return pl.pallas_call(
        functools.partial(_dispatch_kernel, n=n),
        grid_spec=pltpu.PrefetchScalarGridSpec(
            num_scalar_prefetch=1,
            grid=(n // tm,),
            in_specs=[pl.BlockSpec((tm * SUBLANES, LANES), lambda i, dest: (i, 0))],
            out_specs=pl.BlockSpec(memory_space=pl.ANY),
            scratch_shapes=[pltpu.SemaphoreType.DMA(())]),
        out_shape=jax.ShapeDtypeStruct((cap_rows * SUBLANES, LANES), F32),
        compiler_params=_params(("arbitrary",)),
        name="moe_dispatch",
    )(dest, h)


MOE_SUB = 256


def _moe_up_kernel(ie_ref, rb_ref, nv_ref, first_ref, xs_ref, wg_ref, wu_ref, h_ref, wgu_s):
    w = pl.program_id(1)
    tf = h_ref.shape[1]

    @pl.when(first_ref[w] == 1)
    def _():
        wgu_s[:, :tf] = wg_ref[0].astype(BF16)
        wgu_s[:, tf:] = wu_ref[0].astype(BF16)

    nv = nv_ref[w]

    @pl.when(nv > 0)
    def _():
        for r in range(h_ref.shape[0] // MOE_SUB):
            rows = pl.ds(r * MOE_SUB, MOE_SUB)
            idx = r * MOE_SUB + lax.broadcasted_iota(jnp.int32, (MOE_SUB, 1), 0)
            x = _load_slabs(xs_ref, r * MOE_SUB, MOE_SUB)
            x = jnp.where(idx < nv, x, 0.0).astype(BF16)
            hgu = jnp.dot(x, wgu_s[...], preferred_element_type=F32)
            hg, hu = hgu[:, :tf], hgu[:, tf:]
            h_ref[rows, :] = (hg * jax.nn.sigmoid(hg) * hu).astype(BF16)

    @pl.when(nv == 0)
    def _():
        h_ref[...] = jnp.zeros_like(h_ref)


def _moe_up(tables, xs, wg, wu, *, tf=1792):
    tm = MOE_TILE
    cap_rows = xs.shape[0] // SUBLANES
    ne, d, fdim = wg.shape
    n_items = tables[0].shape[0]
    return pl.pallas_call(
        _moe_up_kernel,
        grid_spec=pltpu.PrefetchScalarGridSpec(
            num_scalar_prefetch=4,
            grid=(fdim // tf, n_items),
            in_specs=[pl.BlockSpec((tm * SUBLANES, LANES), lambda f, w, ie, rb, nv, fi: (rb[w], 0)),
                      pl.BlockSpec((1, d, tf), lambda f, w, ie, rb, nv, fi: (ie[w], 0, f)),
                      pl.BlockSpec((1, d, tf), lambda f, w, ie, rb, nv, fi: (ie[w], 0, f))],
            out_specs=pl.BlockSpec((tm, tf), lambda f, w, ie, rb, nv, fi: (rb[w], f)),
            scratch_shapes=[pltpu.VMEM((d, 2 * tf), BF16)]),
        out_shape=jax.ShapeDtypeStruct((cap_rows, fdim), BF16),
        compiler_params=_params(("arbitrary", "arbitrary")),
        name="moe_up",
    )(*tables, xs, wg, wu)


def _moe_down_kernel(ie_ref, rb_ref, nv_ref, first_ref, h_ref, wd_hbm, y_ref, wd_s, stage, sem):
    w = pl.program_id(0)
    chunk = stage.shape[1]
    nchunks = wd_s.shape[0] // chunk

    @pl.when(first_ref[w] == 1)
    def _():
        def copy(c):
            return pltpu.make_async_copy(wd_hbm.at[ie_ref[w], pl.ds(c * chunk, chunk)], stage.at[c % 2],
                                         sem.at[c % 2])
        copy(0).start()
        for c in range(nchunks):
            if c + 1 < nchunks:
                copy(c + 1).start()
            copy(c).wait()
            wd_s[pl.ds(c * chunk, chunk), :] = stage[c % 2].astype(BF16)

    @pl.when(nv_ref[w] > 0)
    def _():
        for r in range(h_ref.shape[0] // MOE_SUB):
            y = jnp.dot(h_ref[pl.ds(r * MOE_SUB, MOE_SUB), :], wd_s[...], preferred_element_type=F32)
            _store_slabs(y_ref, r * MOE_SUB, y)

    @pl.when(nv_ref[w] == 0)
    def _():
        y_ref[...] = jnp.zeros_like(y_ref)


def _moe_down(tables, hid, wd, *, chunk=512):
    tm = MOE_TILE
    cap_rows, fdim = hid.shape
    d = wd.shape[2]
    n_items = tables[0].shape[0]
    assert d == SUBLANES * LANES and fdim % chunk == 0
    return pl.pallas_call(
        _moe_down_kernel,
        grid_spec=pltpu.PrefetchScalarGridSpec(
            num_scalar_prefetch=4,
            grid=(n_items,),
            in_specs=[pl.BlockSpec((tm, fdim), lambda w, ie, rb, nv, fi: (rb[w], 0)),
                      pl.BlockSpec(memory_space=pl.ANY)],
            out_specs=pl.BlockSpec((tm * SUBLANES, LANES), lambda w, ie, rb, nv, fi: (rb[w], 0)),
            scratch_shapes=[pltpu.VMEM((fdim, d), BF16), pltpu.VMEM((2, chunk, d), F32),
                            pltpu.SemaphoreType.DMA((2,))]),
        out_shape=jax.ShapeDtypeStruct((cap_rows * SUBLANES, LANES), F32),
        compiler_params=_params(("arbitrary",)),
        name="moe_down",
    )(*tables, hid, wd)


def _combine_kernel(dest_ref, x_ref, gate_ref, g_ref, ys_ref, o_ref, buf, sem, *, n):
    tm = x_ref.shape[0]
    i = pl.program_id(0)

    def issue(tile, slot):
        def body(r, carry):
            for k in range(2):
                d = dest_ref[k * n + tile * tm + r]
                pltpu.make_async_copy(_slab(ys_ref, d), _slab(buf.at[slot, k], r),
                                      sem.at[slot]).start(priority=k)
            return carry
        lax.fori_loop(0, tm, body, 0, unroll=8)

    @pl.when(i == 0)
    def _():
        issue(0, 0)

    @pl.when(i + 1 < pl.num_programs(0))
    def _():
        issue(i + 1, (i + 1) % 2)

    slot = i % 2
    for k in range(2):
        pltpu.make_async_copy(buf.at[slot, k], buf.at[slot, k], sem.at[slot]).wait()
    gates = gate_ref[...].T
    f = (gates[:, 0:1] * _load_slabs(buf.at[slot, 0], 0, tm)
         + gates[:, 1:2] * _load_slabs(buf.at[slot, 1], 0, tm))
    o_ref[...] = x_ref[...] + _rms(f, g_ref[...])


def _combine(dest, x2, gates, g_post, ys, *, tm=256):
    n, d = x2.shape
    return pl.pallas_call(
        functools.partial(_combine_kernel, n=n),
        grid_spec=pltpu.PrefetchScalarGridSpec(
            num_scalar_prefetch=1,
            grid=(n // tm,),
            in_specs=[pl.BlockSpec((tm, d), lambda i, dest: (i, 0)),
                      pl.BlockSpec((gates.shape[0], tm), lambda i, dest: (0, i)),
                      pl.BlockSpec((1, d), lambda i, dest: (0, 0)),
                      pl.BlockSpec(memory_space=pl.ANY)],
            out_specs=pl.BlockSpec((tm, d), lambda i, dest: (i, 0)),
            scratch_shapes=[pltpu.VMEM((2, 2, tm * SUBLANES, LANES), F32), pltpu.SemaphoreType.DMA((2,))]),
        out_shape=jax.ShapeDtypeStruct((n, d), F32),
        compiler_params=_params(("arbitrary",)),
        name="moe_combine",
    )(dest, x2, gates, g_post, ys)


def _moe(h, x2, route_i, route_g, counts, wg, wu, wd, g_post):
    n = x2.shape[0]
    n_items = 2 * n // MOE_TILE + N_EXPERTS
    tables, base = _moe_tables(counts[:, 0], n_items)
    base_of = lambda e: jnp.sum(jnp.where(e[None, :] == jnp.arange(N_EXPERTS)[:, None], base[:, None], 0), axis=0)
    dest = jnp.concatenate([base_of(route_i[0]) + route_i[2], base_of(route_i[1]) + route_i[3]])
    xs = _dispatch(dest, h, (n_items + 1) * MOE_TILE)
    hid = _moe_up(tables, xs, wg, wu)
    ys = _moe_down(tables, hid, wd)
    return _combine(dest, x2, route_g, g_post, ys)


def _row(v):
    return v.reshape(1, -1).astype(F32)


def kernel(x, norm_mix_pre, norm_mix_post, norm_ffn_pre, norm_ffn_post, w_in, b_forget, pool_w, pool_scale, ssm_a_re, ssm_a_im, ssm_b_re, ssm_b_im, ssm_c_re, ssm_c_im, ssm_d, ssm_log_dt, ssm_w_glu, ssm_b_glu, branch_norm_attn, branch_norm_pool, branch_norm_ssm, w_out, ffn_w_gate, ffn_w_up, ffn_w_down, moe_router, moe_w_gate, moe_w_up, moe_w_down):
    b, L, d = x.shape
    depth = w_in.shape[0]
    n = b * L
    a = ATTN_WIDTH
    f_off = 3 * a
    p_off = f_off + ATTN_HEADS
    attn_tile = 512
    x2 = x.reshape(n, d)
    for i in range(depth):
        w = w_in[i]
        w_main = jnp.concatenate([w[:, :f_off], w[:, p_off:]], axis=1).astype(BF16)
        w_f = jnp.pad(w[:, f_off:p_off], ((0, 0), (0, LANES - ATTN_HEADS))).astype(BF16)
        b_f = jnp.pad(b_forget[i], (0, LANES - ATTN_HEADS)).reshape(1, LANES)
        q, k, v, up, us, c = _inproj(x2, _row(norm_mix_pre[i]), w_main, w_f, b_f, seq=L)

        ct = c.reshape(b, L, ATTN_HEADS).transpose(0, 2, 1).reshape(b, ATTN_HEADS * (L // attn_tile), attn_tile)
        attn = _attention(q.reshape(b, L, a), k.reshape(b, L, a), v.reshape(b, L, a),
                          ct, tq=attn_tile).reshape(n, a)

        pool = _pool(up.reshape(b, L, POOL_WIDTH),
                     jax.scipy.linalg.block_diag(*pool_w[i]).astype(BF16),
                     _row(pool_scale[i])).reshape(n, POOL_WIDTH)

        mats = _ssm_matrices(ssm_a_re[i], ssm_a_im[i], ssm_b_re[i], ssm_b_im[i],
                             ssm_c_re[i], ssm_c_im[i], ssm_d[i], ssm_log_dt[i])
        ys = _ssm_core(us, *mats, seq=L)

        moe = i % 2 == 1
        j = i // 2
        router = moe_router[j].T if moe else None
        res = _mixout(x2, attn, pool, ys, ssm_w_glu[i].astype(BF16), _row(ssm_b_glu[i]),
                      _row(branch_norm_attn[i]), _row(branch_norm_pool[i]), _row(branch_norm_ssm[i]),
                      w_out[i].astype(BF16), _row(norm_mix_post[i]), _row(norm_ffn_pre[i]), router)
        if moe:
            x2, h, route_i, route_g, counts = res
            x2 = _moe(h, x2, route_i, route_g, counts, moe_w_gate[j], moe_w_up[j], moe_w_down[j],
                      _row(norm_ffn_post[i]))
        else:
            x2, h = res
            x2 = _ffn(h, x2, ffn_w_gate[j].astype(BF16), ffn_w_up[j].astype(BF16),
                      ffn_w_down[j].astype(BF16), _row(norm_ffn_post[i]), tf=1408)
    return x2.reshape(b, L, d)
```

```python
import functools

import jax
import jax.numpy as jnp
from jax import lax
from jax.experimental import pallas as pl
from jax.experimental.pallas import tpu as pltpu

F32 = jnp.float32
BF16 = jnp.bfloat16

RMS_EPS = 1e-6
NEG_INF = -1e30
LOG2E = 1.4426950408889634

ATTN_HEADS = 8
ATTN_HEAD_DIM = 64
ATTN_WIDTH = ATTN_HEADS * ATTN_HEAD_DIM
POOL_WINDOWS = (2, 4, 8, 16)
POOL_GROUP_DIM = 64
POOL_WIDTH = len(POOL_WINDOWS) * POOL_GROUP_DIM
SSM_GROUPS = 16
SSM_GROUP_DIM = 16
SSM_STATE = 64
SSM_WIDTH = SSM_GROUPS * SSM_GROUP_DIM
SSM_CHUNK = 8
N_EXPERTS = 8

LANES = 128
SUBLANES = 8
VMEM_LIMIT = 56 * 1024 * 1024


def _params(sem):
    return pltpu.CompilerParams(dimension_semantics=sem, vmem_limit_bytes=VMEM_LIMIT)


def _rms(x, g):
    return x * lax.rsqrt(jnp.mean(x * x, axis=-1, keepdims=True) + RMS_EPS) * g


def _load_slabs(ref, row0, rows):
    return jnp.concatenate([ref[pl.ds(row0 * SUBLANES + s, rows, stride=SUBLANES), :]
                            for s in range(SUBLANES)], axis=1)


def _store_slabs(ref, row0, val):
    for s in range(SUBLANES):
        ref[pl.ds(row0 * SUBLANES + s, val.shape[0], stride=SUBLANES), :] = val[:, s * LANES:(s + 1) * LANES]


def _split3(x):
    hi = x.astype(BF16)
    r = x - hi.astype(F32)
    mid = r.astype(BF16)
    lo = (r - mid.astype(F32)).astype(BF16)
    return hi, mid, lo


def _inproj_kernel(x_ref, g_ref, wm_ref, wf_ref, bf_ref,
                   q_ref, k_ref, v_ref, up_ref, us_ref, c_ref, carry_ref, *, tiles_per_seq):
    @pl.when(pl.program_id(0) % tiles_per_seq == 0)
    def _():
        carry_ref[...] = jnp.zeros_like(carry_ref)

    tm = x_ref.shape[0]
    h = _rms(x_ref[...], g_ref[...]).astype(BF16)
    proj = jnp.dot(h, wm_ref[...], preferred_element_type=F32)
    a = ATTN_WIDTH
    q_ref[...] = (proj[:, 0:a] * (LOG2E * ATTN_HEAD_DIM ** -0.5)).astype(BF16)
    k_ref[...] = proj[:, a:2 * a].astype(BF16)
    v_ref[...] = proj[:, 2 * a:3 * a].astype(BF16)
    up_ref[...] = proj[:, 3 * a:3 * a + POOL_WIDTH].astype(BF16)
    us_ref[...] = proj[:, 3 * a + POOL_WIDTH:]

    z = jnp.dot(h, wf_ref[...], preferred_element_type=F32) + bf_ref[...]
    logf = jnp.minimum(z, 0.0) - jnp.log(1.0 + jnp.exp(-jnp.abs(z)))
    row = lax.broadcasted_iota(jnp.int32, (tm, tm), 0)
    col = lax.broadcasted_iota(jnp.int32, (tm, tm), 1)
    tri = (row >= col).astype(BF16)
    hi, mid, _ = _split3(logf)
    c = (jnp.dot(tri, hi, preferred_element_type=F32)
         + jnp.dot(tri, mid, preferred_element_type=F32)) + carry_ref[...]
    c_ref[...] = c[:, :ATTN_HEADS]
    carry_ref[...] = c[tm - 1:tm, :]


def _inproj(x2, g, w_main, w_f, b_f, *, seq, tm=512):
    n, d = x2.shape
    nm = w_main.shape[1]
    a = ATTN_WIDTH
    out_shape = (
        jax.ShapeDtypeStruct((n, a), BF16), jax.ShapeDtypeStruct((n, a), BF16),
        jax.ShapeDtypeStruct((n, a), BF16), jax.ShapeDtypeStruct((n, POOL_WIDTH), BF16),
        jax.ShapeDtypeStruct((n, SSM_WIDTH), F32), jax.ShapeDtypeStruct((n, ATTN_HEADS), F32))
    row = lambda w: pl.BlockSpec((tm, w), lambda i: (i, 0))
    full = lambda r, c: pl.BlockSpec((r, c), lambda i: (0, 0))
    return pl.pallas_call(
        functools.partial(_inproj_kernel, tiles_per_seq=seq // tm),
        grid=(n // tm,),
        in_specs=[row(d), full(1, d), full(d, nm), full(d, LANES), full(1, LANES)],
        out_specs=(row(a), row(a), row(a), row(POOL_WIDTH), row(SSM_WIDTH), row(ATTN_HEADS)),
        out_shape=out_shape,
        scratch_shapes=[pltpu.VMEM((1, LANES), F32)],
        compiler_params=_params(("arbitrary",)),
        name="inproj",
    )(x2, g, w_main, w_f, b_f)


HEADS_PER_BLOCK = LANES // ATTN_HEAD_DIM


def _attn_kernel(q_ref, k_ref, v_ref, ct_ref, o_ref, *, tq, tk):
    nlb = q_ref.shape[2] // LANES
    hp = pl.program_id(1)
    qi = pl.program_id(2)
    nkb = k_ref.shape[1] // tk
    lane = lax.broadcasted_iota(jnp.int32, (1, LANES), 1)
    in_head = [(lane // ATTN_HEAD_DIM) == hh for hh in range(HEADS_PER_BLOCK)]
    chains = [(lb, hh) for lb in range(nlb) for hh in range(HEADS_PER_BLOCK)]
    qm, crow, cref = [], [], []
    for lb, hh in chains:
        q2 = q_ref[0, :, lb * LANES:(lb + 1) * LANES]
        qm.append(jnp.where(in_head[hh], q2, jnp.zeros_like(q2)))
        crow.append(((hp * nlb + lb) * HEADS_PER_BLOCK + hh) * nkb)
        cref.append(ct_ref[0, pl.ds(crow[-1] + qi, 1), :][:, 0:1])

    def step(kb, carry, masked):
        new = []
        for ci, (lb, hh) in enumerate(chains):
            kblk = k_ref[0, pl.ds(kb * tk, tk), lb * LANES:(lb + 1) * LANES]
            vblk = v_ref[0, pl.ds(kb * tk, tk), lb * LANES:(lb + 1) * LANES]
            m, acc = carry[ci]
            s = lax.dot_general(qm[ci], kblk, (((1,), (1,)), ((), ())), preferred_element_type=F32)
            s = s - (ct_ref[0, pl.ds(crow[ci] + kb, 1), :] - cref[ci]) * LOG2E
            if masked:
                r = lax.broadcasted_iota(jnp.int32, (tq, tk), 0)
                cidx = lax.broadcasted_iota(jnp.int32, (tq, tk), 1)
                s = jnp.where(cidx <= r, s, NEG_INF)
            m_new = jnp.maximum(m, jnp.max(s, axis=1, keepdims=True))
            p = jnp.exp2(s - m_new)
            vsel = jnp.where(in_head[hh], vblk, jnp.ones_like(vblk))
            acc = jnp.exp2(m - m_new) * acc + jnp.dot(p.astype(BF16), vsel, preferred_element_type=F32)
            new.append((m_new, acc))
        return tuple(new)

    init = tuple((jnp.full((tq, 1), NEG_INF, F32), jnp.zeros((tq, LANES), F32)) for _ in chains)
    carry = lax.fori_loop(0, qi, lambda kb, cr: step(kb, cr, False), init)
    carry = step(qi, carry, True)
    for lb in range(nlb):
        out = jnp.zeros((tq, LANES), F32)
        for hh in range(HEADS_PER_BLOCK):
            acc = carry[lb * HEADS_PER_BLOCK + hh][1]
            out = jnp.where(in_head[hh], acc / pltpu.roll(acc, ATTN_HEAD_DIM, 1), out)
        o_ref[0, :, lb * LANES:(lb + 1) * LANES] = out.astype(o_ref.dtype)


def _attention(q, k, v, ct, *, tq, lane_blocks=2):
    b, L, a = q.shape
    tk = ct.shape[2]
    assert tq == tk, "the diagonal key block of query tile i must be key block i"
    w = lane_blocks * LANES
    blk = lambda: pl.BlockSpec((1, tq, w), lambda bi, hi, qi: (bi, qi, hi))
    seq = lambda: pl.BlockSpec((1, L, w), lambda bi, hi, qi: (bi, 0, hi))
    return pl.pallas_call(
        functools.partial(_attn_kernel, tq=tq, tk=tk),
        grid=(b, a // w, L // tq),
        in_specs=[blk(), seq(), seq(),
                  pl.BlockSpec((1, ct.shape[1], tk), lambda bi, hi, qi: (bi, 0, 0))],
        out_specs=blk(),
        out_shape=jax.ShapeDtypeStruct((b, L, a), BF16),
        compiler_params=_params(("arbitrary", "arbitrary", "arbitrary")),
        name="fox_attention",
    )(q, k, v, ct)


def _pool_kernel(u_ref, w_ref, s_ref, o_ref):
    x = u_ref[0].astype(F32)
    L, w = x.shape
    row = lax.broadcasted_iota(jnp.int32, (L, w), 0)
    group = lax.broadcasted_iota(jnp.int32, (L, w), 1) // POOL_GROUP_DIM

    def shifted(y, s):
        return jnp.where(row >= s, pltpu.roll(y, s, 0), 0.0)

    acc = x
    sel = jnp.zeros_like(x)
    win_lane = jnp.zeros((L, w), F32)
    span = 1
    for gi, win in enumerate(POOL_WINDOWS):
        while span < win:
            acc = acc + shifted(acc, span)
            span *= 2
        sel = jnp.where(group == gi, acc, sel)
        win_lane = jnp.where(group == gi, float(win), win_lane)
    count = jnp.minimum(row.astype(F32) + 1.0, win_lane)
    d = sel / count - x
    y = jnp.dot(d.astype(BF16), w_ref[...], preferred_element_type=F32) * s_ref[...]
    o_ref[0] = y.astype(o_ref.dtype)


def _pool(u, w_bd, scale):
    b, L, w = u.shape
    return pl.pallas_call(
        _pool_kernel,
        grid=(b,),
        in_specs=[pl.BlockSpec((1, L, w), lambda i: (i, 0, 0)),
                  pl.BlockSpec((w, w), lambda i: (0, 0)),
                  pl.BlockSpec((1, w), lambda i: (0, 0))],
        out_specs=pl.BlockSpec((1, L, w), lambda i: (i, 0, 0)),
        out_shape=jax.ShapeDtypeStruct((b, L, w), BF16),
        compiler_params=_params(("arbitrary",)),
        name="pool_mixer",
    )(u, w_bd, scale)


def _group_of(idx, width):
    groups = LANES // SSM_GROUP_DIM
    return (idx >> (width.bit_length() - 1)) & (groups - 1)


def _expand_blockdiag(r_ref, row_inner, col_inner):
    rows, w = r_ref.shape[1:]
    groups = LANES // SSM_GROUP_DIM
    r = lax.broadcasted_iota(jnp.int32, (w, groups * w), 0)
    c = lax.broadcasted_iota(jnp.int32, (w, groups * w), 1)
    shift = col_inner.bit_length() - 1
    outer_c = c >> (shift + groups.bit_length() - 1)
    spread = (outer_c == (r >> shift)) & ((c & (col_inner - 1)) == (r & (col_inner - 1)))
    big = jnp.dot(r_ref[0], spread.astype(BF16), preferred_element_type=F32)
    ri = lax.broadcasted_iota(jnp.int32, (rows, groups * w), 0)
    ci = lax.broadcasted_iota(jnp.int32, (rows, groups * w), 1)
    return jnp.where(_group_of(ri, row_inner) == _group_of(ci, col_inner), big, 0.0).astype(BF16)


def _ssm_kernel(u_ref, rm_ref, rz_ref, ry_ref, coef_ref, y_ref, m_s, wz_s, wy_s, ucat, z_ref, s_ref, *, seqs):
    t = SSM_CHUNK

    @pl.when(pl.program_id(1) == 0)
    def _():
        m_s[...] = _expand_blockdiag(rm_ref, SSM_GROUP_DIM, SSM_GROUP_DIM)
        wz_s[...] = _expand_blockdiag(rz_ref, SSM_GROUP_DIM, SSM_STATE)
        wy_s[...] = _expand_blockdiag(ry_ref, SSM_STATE, SSM_GROUP_DIM)

    nc = u_ref.shape[0] // t
    cps = nc // seqs
    for k in range(t):
        ucat[:, k * LANES:(k + 1) * LANES] = u_ref[pl.ds(k, nc, stride=t), :].astype(BF16)
    u = ucat[...]
    z_ref[...] = jnp.dot(u, wz_s[...], preferred_element_type=F32)
    half = z_ref.shape[1] // 2
    a1 = coef_ref[0, 0:1, :]
    a2 = coef_ref[0, 1:2, :]

    def step(c, states):
        new = []
        for b in range(seqs):
            st = states[b]
            r = b * cps + c
            s_ref[pl.ds(r, 1), :] = st
            sw = jnp.concatenate([st[:, half:], st[:, :half]], axis=1)
            new.append(a1 * st + a2 * sw + z_ref[pl.ds(r, 1), :])
        return tuple(new)

    init = tuple(jnp.zeros((1, 2 * half), F32) for _ in range(seqs))
    lax.fori_loop(0, cps, step, init, unroll=8)
    y = jnp.dot(u, m_s[...], preferred_element_type=F32)
    y = y + jnp.dot(s_ref[...].astype(BF16), wy_s[...], preferred_element_type=F32)
    for k in range(t):
        y_ref[pl.ds(k, nc, stride=t), :] = y[:, k * LANES:(k + 1) * LANES]


def _ssm_core(u, rm, rz, ry, coef, *, seq, seqs_per_block=2):
    n, w = u.shape
    rows = seq * seqs_per_block
    nc = rows // SSM_CHUNK
    kw = SSM_CHUNK * LANES
    p2 = coef.shape[2]
    blk = pl.BlockSpec((rows, LANES), lambda hf, r: (r, hf))
    wspec = lambda arr: pl.BlockSpec((1,) + arr.shape[1:], lambda hf, r: (hf, 0, 0))
    return pl.pallas_call(
        functools.partial(_ssm_kernel, seqs=seqs_per_block),
        grid=(w // LANES, n // rows),
        in_specs=[blk, wspec(rm), wspec(rz), wspec(ry), wspec(coef)],
        out_specs=blk,
        out_shape=jax.ShapeDtypeStruct((n, w), F32),
        scratch_shapes=[pltpu.VMEM((kw, kw), BF16), pltpu.VMEM((kw, p2), BF16), pltpu.VMEM((p2, kw), BF16),
                        pltpu.VMEM((nc, kw), BF16), pltpu.VMEM((nc, p2), F32), pltpu.VMEM((nc, p2), F32)],
        compiler_params=_params(("arbitrary", "arbitrary")),
        name="s5_core",
    )(u, rm, rz, ry, coef)


def _ssm_matrices(a_re, a_im, b_re, b_im, c_re, c_im, d, log_dt):
    t = SSM_CHUNK
    g, p = a_re.shape
    hdim = d.shape[1]
    gl = LANES // hdim
    nb = g // gl
    lam = lax.complex(a_re, a_im)
    lam_dt = lam * jnp.exp(log_dt)[:, None]
    lam_bar = jnp.exp(lam_dt)
    b_bar = ((lam_bar - 1.0) / lam)[:, :, None] * lax.complex(b_re, b_im)
    cmat = lax.complex(c_re, c_im)
    steps = jnp.arange(t + 1, dtype=F32)
    pw = jnp.exp(lam_dt[:, None, :] * steps[None, :, None])
    kern = jnp.einsum('gop,gdp,gpi->gdoi', cmat, pw[:, :t], b_bar).real
    kern = kern.at[:, 0].add(jax.vmap(jnp.diag)(d))
    j = jnp.arange(t)[:, None]
    i = jnp.arange(t)[None, :]
    m = jnp.where((i >= j)[None, :, :, None, None], kern[:, jnp.clip(i - j, 0, t - 1)], 0.0)
    wz = jnp.einsum('gjp,gpi->gjip', pw[:, t - 1 - jnp.arange(t)], b_bar)
    cw = jnp.einsum('gop,gip->giop', cmat, pw[:, 1:t + 1])
    rm = m.reshape(nb, gl, t, t, hdim, hdim).transpose(0, 2, 1, 5, 3, 4).reshape(nb, t * LANES, t * hdim)
    rz = jnp.stack([wz.real, wz.imag], axis=3).reshape(nb, gl, t, hdim, 2, p)
    rz = rz.transpose(0, 2, 1, 3, 4, 5).reshape(nb, t * LANES, 2 * p)
    ry = jnp.stack([cw.real, -cw.imag], axis=1).reshape(nb, gl, 2, t, hdim, p)
    ry = ry.transpose(0, 2, 1, 5, 3, 4).reshape(nb, 2 * gl * p, t * hdim)
    ar = pw[:, t].real.reshape(nb, gl * p)
    ai = pw[:, t].imag.reshape(nb, gl * p)
    coef = jnp.stack([jnp.concatenate([ar, ar], -1), jnp.concatenate([-ai, ai], -1)], axis=1)
    return rm.astype(BF16), rz.astype(BF16), ry.astype(BF16), coef.astype(F32)


def _mixout_kernel(*refs, moe):
    (x_ref, attn_ref, pool_ref, ys_ref, wglu_ref, bglu_ref, ga_ref, gp_ref, gs_ref,
     wo_ref, gpost_ref, gpre_ref) = refs[:12]
    if moe:
        router_ref, xo_ref, h_ref, ri_ref, rg_ref, cnt_ref, cnt_acc = refs[12:]

        @pl.when(pl.program_id(0) == 0)
        def _():
            cnt_acc[...] = jnp.zeros_like(cnt_acc)
    else:
        xo_ref, h_ref = refs[12:]
    a = attn_ref.shape[1]
    pw = pool_ref.shape[1]
    attn_n = _rms(attn_ref[...].astype(F32), ga_ref[...]).astype(BF16)
    pool_n = _rms(pool_ref[...].astype(F32), gp_ref[...]).astype(BF16)
    y = jax.nn.gelu(ys_ref[...])
    gate = jnp.dot(y.astype(BF16), wglu_ref[...], preferred_element_type=F32) + bglu_ref[...]
    ssm = y * jax.nn.sigmoid(gate)
    ssm_n = _rms(ssm, gs_ref[...]).astype(BF16)
    mix = (jnp.dot(attn_n, wo_ref[0:a, :], preferred_element_type=F32)
           + jnp.dot(pool_n, wo_ref[a:a + pw, :], preferred_element_type=F32)
           + jnp.dot(ssm_n, wo_ref[a + pw:, :], preferred_element_type=F32))
    x = x_ref[...] + _rms(mix, gpost_ref[...])
    xo_ref[...] = x
    h = _rms(x, gpre_ref[...])
    if moe:
        _store_slabs(h_ref, 0, h)
    else:
        h_ref[...] = h.astype(h_ref.dtype)
    if moe:
        hi, mid, _ = _split3(h)
        rhi, rmid, _ = _split3(router_ref[...])
        dot_t = lambda p, q: lax.dot_general(p, q, (((1,), (1,)), ((), ())), preferred_element_type=F32)
        logits = dot_t(rhi, hi) + dot_t(rmid, hi) + dot_t(rhi, mid)
        ne, tm = logits.shape
        eidx = lax.broadcasted_iota(jnp.int32, (ne, tm), 0)
        m1 = jnp.max(logits, axis=0, keepdims=True)
        i1 = jnp.min(jnp.where(logits == m1, eidx, ne), axis=0, keepdims=True)
        rest = jnp.where(eidx == i1, -jnp.inf, logits)
        m2 = jnp.max(rest, axis=0, keepdims=True)
        i2 = jnp.min(jnp.where(rest == m2, eidx, ne), axis=0, keepdims=True)
        e2 = jnp.exp(m2 - m1)
        g1 = 1.0 / (1.0 + e2)
        rg_ref[...] = jnp.where(eidx == 0, g1, jnp.where(eidx == 1, e2 * g1, 0.0))
        onehot = ((eidx == i1) | (eidx == i2)).astype(BF16)
        earlier = (lax.broadcasted_iota(jnp.int32, (tm, tm), 0)
                   < lax.broadcasted_iota(jnp.int32, (tm, tm), 1)).astype(BF16)
        before = jnp.dot(onehot, earlier, preferred_element_type=F32) + cnt_acc[:, 0:1]
        r1 = jnp.sum(jnp.where(eidx == i1, before, 0.0), axis=0, keepdims=True).astype(jnp.int32)
        r2 = jnp.sum(jnp.where(eidx == i2, before, 0.0), axis=0, keepdims=True).astype(jnp.int32)
        ri_ref[...] = jnp.where(eidx == 0, i1, jnp.where(eidx == 1, i2,
                                jnp.where(eidx == 2, r1, jnp.where(eidx == 3, r2, 0))))
        cnt_acc[...] += jnp.sum(onehot.astype(F32), axis=1, keepdims=True)
        cnt_ref[...] = cnt_acc[...]


def _mixout(x2, attn, pool, ys, w_glu, b_glu, g_attn, g_pool, g_ssm, w_out, g_post, g_pre,
            router=None, *, tm=512):
    n, d = x2.shape
    moe = router is not None
    row = lambda w: pl.BlockSpec((tm, w), lambda i: (i, 0))
    full = lambda arr: pl.BlockSpec(arr.shape, lambda i: (0, 0))
    ins = [x2, attn, pool, ys, w_glu, b_glu, g_attn, g_pool, g_ssm, w_out, g_post, g_pre]
    in_specs = [row(d), row(attn.shape[1]), row(pool.shape[1]), row(ys.shape[1])] + [full(t) for t in ins[4:]]
    if moe:
        assert d == SUBLANES * LANES
        h_shape = jax.ShapeDtypeStruct((n * SUBLANES, LANES), F32)
        h_spec = pl.BlockSpec((tm * SUBLANES, LANES), lambda i: (i, 0))
    else:
        h_shape, h_spec = jax.ShapeDtypeStruct((n, d), BF16), row(d)
    out_shape = [jax.ShapeDtypeStruct((n, d), F32), h_shape]
    out_specs = [row(d), h_spec]
    scratch = []
    if moe:
        ins.append(router)
        in_specs.append(full(router))
        ne = router.shape[0]
        col = pl.BlockSpec((ne, tm), lambda i: (0, i))
        out_shape += [jax.ShapeDtypeStruct((ne, n), jnp.int32), jax.ShapeDtypeStruct((ne, n), F32),
                      jax.ShapeDtypeStruct((ne, LANES), F32)]
        out_specs += [col, col, pl.BlockSpec((ne, LANES), lambda i: (0, 0))]
        scratch = [pltpu.VMEM((ne, LANES), F32)]
    return pl.pallas_call(
        functools.partial(_mixout_kernel, moe=moe),
        grid=(n // tm,),
        in_specs=in_specs,
        out_specs=tuple(out_specs),
        out_shape=tuple(out_shape),
        scratch_shapes=scratch,
        compiler_params=_params(("arbitrary",)),
        name="mix_out",
    )(*ins)


def _ffn_kernel(h_ref, x_ref, wgu_ref, wd_ref, g_ref, o_ref, acc_ref, *, sub):
    f = pl.program_id(1)

    @pl.when(f == 0)
    def _():
        acc_ref[...] = jnp.zeros_like(acc_ref)

    tf = wd_ref.shape[0]
    for r in range(h_ref.shape[0] // sub):
        rows = pl.ds(r * sub, sub)
        hgu = jnp.dot(h_ref[rows, :], wgu_ref[0], preferred_element_type=F32)
        hg, hu = hgu[:, :tf], hgu[:, tf:]
        act = (hg * jax.nn.sigmoid(hg) * hu).astype(BF16)
        acc_ref[rows, :] += jnp.dot(act, wd_ref[...], preferred_element_type=F32)

    @pl.when(f == pl.num_programs(1) - 1)
    def _():
        o_ref[...] = x_ref[...] + _rms(acc_ref[...], g_ref[...])


def _ffn(h, x2, wg, wu, wd, g_post, *, tm=1024, tf, sub=256):
    n, d = x2.shape
    fdim = wg.shape[1]
    nf = fdim // tf
    wgu = jnp.concatenate([wg.reshape(d, nf, tf), wu.reshape(d, nf, tf)], axis=2).transpose(1, 0, 2)
    row = lambda w: pl.BlockSpec((tm, w), lambda i, f: (i, 0))
    return pl.pallas_call(
        functools.partial(_ffn_kernel, sub=sub),
        grid=(n // tm, nf),
        in_specs=[row(d), row(d),
                  pl.BlockSpec((1, d, 2 * tf), lambda i, f: (f, 0, 0)),
                  pl.BlockSpec((tf, d), lambda i, f: (f, 0)),
                  pl.BlockSpec((1, d), lambda i, f: (0, 0))],
        out_specs=row(d),
        out_shape=jax.ShapeDtypeStruct((n, d), F32),
        scratch_shapes=[pltpu.VMEM((tm, d), F32)],
        compiler_params=_params(("arbitrary", "arbitrary")),
        name="dense_ffn",
    )(h, x2, wgu, wd, g_post)


MOE_TILE = 512


def _moe_tables(counts, n_items):
    tm = MOE_TILE
    counts = counts.astype(jnp.int32)
    ntiles = (counts + tm - 1) // tm
    k = jnp.arange(ntiles.shape[0])
    ends = jnp.sum(jnp.where(k[None, :] <= k[:, None], ntiles[None, :], 0), axis=1)
    starts = ends - ntiles
    w = jnp.arange(n_items, dtype=jnp.int32)
    wc = jnp.minimum(w, ends[-1] - 1)
    e = jnp.sum((wc[:, None] >= ends[None, :]).astype(jnp.int32), axis=1)
    valid = w < ends[-1]
    rowblock = jnp.where(valid, w, n_items)
    nvalid = jnp.where(valid, jnp.clip(counts[e] - (wc - starts[e]) * tm, 0, tm), 0)
    first = ((w == 0) | (e != jnp.roll(e, 1))).astype(jnp.int32)
    present = ntiles > 0
    later = present[None, :] & (k[None, :] > k[:, None])
    nxt = jnp.where(jnp.any(later, axis=1), jnp.argmax(later, axis=1), -1).astype(jnp.int32)
    slot = (jnp.sum((present[None, :] & (k[None, :] < k[:, None])).astype(jnp.int32), axis=1) % 2)
    return (e, rowblock, nvalid.astype(jnp.int32), first), (nxt[e], slot[e].astype(jnp.int32)), starts * tm


def _slab(ref, token):
    return ref.at[pl.ds(pl.multiple_of(token * SUBLANES, SUBLANES), SUBLANES)]


def _dispatch_kernel(dest_ref, h_ref, xs_ref, sem, *, n):
    tm = h_ref.shape[0] // SUBLANES
    base = pl.program_id(0) * tm

    def body(r, carry):
        for k in range(2):
            d = dest_ref[k * n + base + r]
            pltpu.make_async_copy(_slab(h_ref, r), _slab(xs_ref, d), sem).start(priority=k)
        return carry

    lax.fori_loop(0, tm, body, 0, unroll=8)
    for k in range(2):
        pltpu.make_async_copy(h_ref, h_ref, sem).wait()


def _dispatch(dest, h, cap_rows, *, tm=512):
    n = h.shape[0] // SUBLANES
    return pl.pallas_call(
        functools.partial(_dispatch_kernel, n=n),
        grid_spec=pltpu.PrefetchScalarGridSpec(
            num_scalar_prefetch=1,
            grid=(n // tm,),
            in_specs=[pl.BlockSpec((tm * SUBLANES, LANES), lambda i, dest: (i, 0))],
            out_specs=pl.BlockSpec(memory_space=pl.ANY),
            scratch_shapes=[pltpu.SemaphoreType.DMA(())]),
        out_shape=jax.ShapeDtypeStruct((cap_rows * SUBLANES, LANES), F32),
        compiler_params=_params(("arbitrary",)),
        name="moe_dispatch",
    )(dest, h)


MOE_SUB = 256


def _moe_up_kernel(ie_ref, rb_ref, nv_ref, first_ref, xs_ref, wg_ref, wu_ref, h_ref, wgu_s):
    w = pl.program_id(1)
    tf = h_ref.shape[1]

    @pl.when(first_ref[w] == 1)
    def _():
        wgu_s[:, :tf] = wg_ref[0].astype(BF16)
        wgu_s[:, tf:] = wu_ref[0].astype(BF16)

    nv = nv_ref[w]

    @pl.when(nv > 0)
    def _():
        for r in range(h_ref.shape[0] // MOE_SUB):
            rows = pl.ds(r * MOE_SUB, MOE_SUB)
            idx = r * MOE_SUB + lax.broadcasted_iota(jnp.int32, (MOE_SUB, 1), 0)
            x = _load_slabs(xs_ref, r * MOE_SUB, MOE_SUB)
            x = jnp.where(idx < nv, x, 0.0).astype(BF16)
            hgu = jnp.dot(x, wgu_s[...], preferred_element_type=F32)
            hg, hu = hgu[:, :tf], hgu[:, tf:]
            h_ref[rows, :] = (hg * jax.nn.sigmoid(hg) * hu).astype(BF16)

    @pl.when(nv == 0)
    def _():
        h_ref[...] = jnp.zeros_like(h_ref)


def _moe_up(tables, xs, wg, wu, *, tf=1792):
    tm = MOE_TILE
    cap_rows = xs.shape[0] // SUBLANES
    ne, d, fdim = wg.shape
    n_items = tables[0].shape[0]
    return pl.pallas_call(
        _moe_up_kernel,
        grid_spec=pltpu.PrefetchScalarGridSpec(
            num_scalar_prefetch=4,
            grid=(fdim // tf, n_items),
            in_specs=[pl.BlockSpec((tm * SUBLANES, LANES), lambda f, w, ie, rb, nv, fi: (rb[w], 0)),
                      pl.BlockSpec((1, d, tf), lambda f, w, ie, rb, nv, fi: (ie[w], 0, f)),
                      pl.BlockSpec((1, d, tf), lambda f, w, ie, rb, nv, fi: (ie[w], 0, f))],
            out_specs=pl.BlockSpec((tm, tf), lambda f, w, ie, rb, nv, fi: (rb[w], f)),
            scratch_shapes=[pltpu.VMEM((d, 2 * tf), BF16)]),
        out_shape=jax.ShapeDtypeStruct((cap_rows, fdim), BF16),
        compiler_params=_params(("arbitrary", "arbitrary")),
        name="moe_up",
    )(*tables, xs, wg, wu)


def _moe_down_kernel(ie_ref, rb_ref, nv_ref, first_ref, nxt_ref, slot_ref, h_ref, wd_hbm, y_ref,
                     wd_s, stage, sem):
    w = pl.program_id(0)
    nchunks = sem.shape[0]
    chunk = stage.shape[0] // nchunks

    def copies(e):
        return [pltpu.make_async_copy(wd_hbm.at[e, pl.ds(c * chunk, chunk)], stage.at[pl.ds(c * chunk, chunk)],
                                      sem.at[c]) for c in range(nchunks)]

    @pl.when(w == 0)
    def _():
        for cp in copies(ie_ref[0]):
            cp.start()

    @pl.when(first_ref[w] == 1)
    def _():
        for c, cp in enumerate(copies(ie_ref[w])):
            cp.wait()
            wd_s[slot_ref[w], pl.ds(c * chunk, chunk), :] = stage[pl.ds(c * chunk, chunk), :].astype(BF16)

        @pl.when(nxt_ref[w] >= 0)
        def _():
            for cp in copies(nxt_ref[w]):
                cp.start()

    @pl.when(nv_ref[w] > 0)
    def _():
        wd = wd_s[slot_ref[w]]
        for r in range(h_ref.shape[0] // MOE_SUB):
            y = jnp.dot(h_ref[pl.ds(r * MOE_SUB, MOE_SUB), :], wd, preferred_element_type=F32)
            _store_slabs(y_ref, r * MOE_SUB, y)

    @pl.when(nv_ref[w] == 0)
    def _():
        y_ref[...] = jnp.zeros_like(y_ref)


def _moe_down(tables, order, hid, wd, *, chunk=512):
    tm = MOE_TILE
    cap_rows, fdim = hid.shape
    d = wd.shape[2]
    n_items = tables[0].shape[0]
    assert d == SUBLANES * LANES and fdim % chunk == 0
    return pl.pallas_call(
        _moe_down_kernel,
        grid_spec=pltpu.PrefetchScalarGridSpec(
            num_scalar_prefetch=6,
            grid=(n_items,),
            in_specs=[pl.BlockSpec((tm, fdim), lambda w, ie, rb, *_: (rb[w], 0)),
                      pl.BlockSpec(memory_space=pl.ANY)],
            out_specs=pl.BlockSpec((tm * SUBLANES, LANES), lambda w, ie, rb, *_: (rb[w], 0)),
            scratch_shapes=[pltpu.VMEM((2, fdim, d), BF16), pltpu.VMEM((fdim, d), F32),
                            pltpu.SemaphoreType.DMA((fdim // chunk,))]),
        out_shape=jax.ShapeDtypeStruct((cap_rows * SUBLANES, LANES), F32),
        compiler_params=_params(("arbitrary",)),
        name="moe_down",
    )(*tables, *order, hid, wd)


def _combine_kernel(dest_ref, x_ref, gate_ref, g_ref, ys_ref, o_ref, buf, sem, *, n):
    tm = x_ref.shape[0]
    i = pl.program_id(0)

    def issue(tile, slot):
        def body(r, carry):
            for k in range(2):
                d = dest_ref[k * n + tile * tm + r]
                pltpu.make_async_copy(_slab(ys_ref, d), _slab(buf.at[slot, k], r),
                                      sem.at[slot]).start(priority=k)
            return carry
        lax.fori_loop(0, tm, body, 0, unroll=8)

    @pl.when(i == 0)
    def _():
        issue(0, 0)

    @pl.when(i + 1 < pl.num_programs(0))
    def _():
        issue(i + 1, (i + 1) % 2)

    slot = i % 2
    for k in range(2):
        pltpu.make_async_copy(buf.at[slot, k], buf.at[slot, k], sem.at[slot]).wait()
    gates = gate_ref[...].T
    f = (gates[:, 0:1] * _load_slabs(buf.at[slot, 0], 0, tm)
         + gates[:, 1:2] * _load_slabs(buf.at[slot, 1], 0, tm))
    o_ref[...] = x_ref[...] + _rms(f, g_ref[...])


def _combine(dest, x2, gates, g_post, ys, *, tm=256):
    n, d = x2.shape
    return pl.pallas_call(
        functools.partial(_combine_kernel, n=n),
        grid_spec=pltpu.PrefetchScalarGridSpec(
            num_scalar_prefetch=1,
            grid=(n // tm,),
            in_specs=[pl.BlockSpec((tm, d), lambda i, dest: (i, 0)),
                      pl.BlockSpec((gates.shape[0], tm), lambda i, dest: (0, i)),
                      pl.BlockSpec((1, d), lambda i, dest: (0, 0)),
                      pl.BlockSpec(memory_space=pl.ANY)],
            out_specs=pl.BlockSpec((tm, d), lambda i, dest: (i, 0)),
            scratch_shapes=[pltpu.VMEM((2, 2, tm * SUBLANES, LANES), F32), pltpu.SemaphoreType.DMA((2,))]),
        out_shape=jax.ShapeDtypeStruct((n, d), F32),
        compiler_params=_params(("arbitrary",)),
        name="moe_combine",
    )(dest, x2, gates, g_post, ys)


def _moe(h, x2, route_i, route_g, counts, wg, wu, wd, g_post):
    n = x2.shape[0]
    n_items = 2 * n // MOE_TILE + N_EXPERTS
    tables, order, base = _moe_tables(counts[:, 0], n_items)
    base_of = lambda e: jnp.sum(jnp.where(e[None, :] == jnp.arange(N_EXPERTS)[:, None], base[:, None], 0), axis=0)
    dest = jnp.concatenate([base_of(route_i[0]) + route_i[2], base_of(route_i[1]) + route_i[3]])
    xs = _dispatch(dest, h, (n_items + 1) * MOE_TILE)
    hid = _moe_up(tables, xs, wg, wu)
    ys = _moe_down(tables, order, hid, wd)
    return _combine(dest, x2, route_g, g_post, ys)


def _row(v):
    return v.reshape(1, -1).astype(F32)


def kernel(x, norm_mix_pre, norm_mix_post, norm_ffn_pre, norm_ffn_post, w_in, b_forget, pool_w, pool_scale, ssm_a_re, ssm_a_im, ssm_b_re, ssm_b_im, ssm_c_re, ssm_c_im, ssm_d, ssm_log_dt, ssm_w_glu, ssm_b_glu, branch_norm_attn, branch_norm_pool, branch_norm_ssm, w_out, ffn_w_gate, ffn_w_up, ffn_w_down, moe_router, moe_w_gate, moe_w_up, moe_w_down):
    b, L, d = x.shape
    depth = w_in.shape[0]
    n = b * L
    a = ATTN_WIDTH
    f_off = 3 * a
    p_off = f_off + ATTN_HEADS
    attn_tile = 512
    x2 = x.reshape(n, d)
    for i in range(depth):
        w = w_in[i]
        w_main = jnp.concatenate([w[:, :f_off], w[:, p_off:]], axis=1).astype(BF16)
        w_f = jnp.pad(w[:, f_off:p_off], ((0, 0), (0, LANES - ATTN_HEADS))).astype(BF16)
        b_f = jnp.pad(b_forget[i], (0, LANES - ATTN_HEADS)).reshape(1, LANES)
        q, k, v, up, us, c = _inproj(x2, _row(norm_mix_pre[i]), w_main, w_f, b_f, seq=L)

        ct = c.reshape(b, L, ATTN_HEADS).transpose(0, 2, 1).reshape(b, ATTN_HEADS * (L // attn_tile), attn_tile)
        attn = _attention(q.reshape(b, L, a), k.reshape(b, L, a), v.reshape(b, L, a),
                          ct, tq=attn_tile).reshape(n, a)

        pool = _pool(up.reshape(b, L, POOL_WIDTH),
                     jax.scipy.linalg.block_diag(*pool_w[i]).astype(BF16),
                     _row(pool_scale[i])).reshape(n, POOL_WIDTH)

        mats = _ssm_matrices(ssm_a_re[i], ssm_a_im[i], ssm_b_re[i], ssm_b_im[i],
                             ssm_c_re[i], ssm_c_im[i], ssm_d[i], ssm_log_dt[i])
        ys = _ssm_core(us, *mats, seq=L)

        moe = i % 2 == 1
        j = i // 2
        router = moe_router[j].T if moe else None
        res = _mixout(x2, attn, pool, ys, ssm_w_glu[i].astype(BF16), _row(ssm_b_glu[i]),
                      _row(branch_norm_attn[i]), _row(branch_norm_pool[i]), _row(branch_norm_ssm[i]),
                      w_out[i].astype(BF16), _row(norm_mix_post[i]), _row(norm_ffn_pre[i]), router)
        if moe:
            x2, h, route_i, route_g, counts = res
            x2 = _moe(h, x2, route_i, route_g, counts, moe_w_gate[j], moe_w_up[j], moe_w_down[j],
                      _row(norm_ffn_post[i]))
        else:
            x2, h = res
            x2 = _ffn(h, x2, ffn_w_gate[j].astype(BF16), ffn_w_up[j].astype(BF16),
                      ffn_w_down[j].astype(BF16), _row(norm_ffn_post[i]), tf=1408)
    return x2.reshape(b, L, d)
```

```python
import functools

import jax
import jax.numpy as jnp
from jax import lax
from jax.experimental import pallas as pl
from jax.experimental.pallas import tpu as pltpu

F32 = jnp.float32
BF16 = jnp.bfloat16

RMS_EPS = 1e-6
NEG_INF = -1e30
LOG2E = 1.4426950408889634

ATTN_HEADS = 8
ATTN_HEAD_DIM = 64
ATTN_WIDTH = ATTN_HEADS * ATTN_HEAD_DIM
POOL_WINDOWS = (2, 4, 8, 16)
POOL_GROUP_DIM = 64
POOL_WIDTH = len(POOL_WINDOWS) * POOL_GROUP_DIM
SSM_GROUPS = 16
SSM_GROUP_DIM = 16
SSM_STATE = 64
SSM_WIDTH = SSM_GROUPS * SSM_GROUP_DIM
SSM_CHUNK = 8
N_EXPERTS = 8

LANES = 128
SUBLANES = 8
VMEM_LIMIT = 56 * 1024 * 1024


def _params(sem):
    return pltpu.CompilerParams(dimension_semantics=sem, vmem_limit_bytes=VMEM_LIMIT)


def _rms(x, g):
    return x * lax.rsqrt(jnp.mean(x * x, axis=-1, keepdims=True) + RMS_EPS) * g


def _load_slabs(ref, row0, rows):
    return jnp.concatenate([ref[pl.ds(row0 * SUBLANES + s, rows, stride=SUBLANES), :]
                            for s in range(SUBLANES)], axis=1)


def _store_slabs(ref, row0, val):
    for s in range(SUBLANES):
        ref[pl.ds(row0 * SUBLANES + s, val.shape[0], stride=SUBLANES), :] = val[:, s * LANES:(s + 1) * LANES]


def _split3(x):
    hi = x.astype(BF16)
    r = x - hi.astype(F32)
    mid = r.astype(BF16)
    lo = (r - mid.astype(F32)).astype(BF16)
    return hi, mid, lo


def _inproj_kernel(x_ref, g_ref, wm_ref, wf_ref, bf_ref,
                   q_ref, k_ref, v_ref, up_ref, us_ref, c_ref, carry_ref, *, tiles_per_seq):
    @pl.when(pl.program_id(0) % tiles_per_seq == 0)
    def _():
        carry_ref[...] = jnp.zeros_like(carry_ref)

    tm = x_ref.shape[0]
    h = _rms(x_ref[...], g_ref[...]).astype(BF16)
    proj = jnp.dot(h, wm_ref[...], preferred_element_type=F32)
    a = ATTN_WIDTH
    q_ref[...] = (proj[:, 0:a] * (LOG2E * ATTN_HEAD_DIM ** -0.5)).astype(BF16)
    k_ref[...] = proj[:, a:2 * a].astype(BF16)
    v_ref[...] = proj[:, 2 * a:3 * a].astype(BF16)
    up_ref[...] = proj[:, 3 * a:3 * a + POOL_WIDTH].astype(BF16)
    us_ref[...] = proj[:, 3 * a + POOL_WIDTH:]

    z = jnp.dot(h, wf_ref[...], preferred_element_type=F32) + bf_ref[...]
    logf = jnp.minimum(z, 0.0) - jnp.log(1.0 + jnp.exp(-jnp.abs(z)))
    row = lax.broadcasted_iota(jnp.int32, (tm, tm), 0)
    col = lax.broadcasted_iota(jnp.int32, (tm, tm), 1)
    tri = (row >= col).astype(BF16)
    hi, mid, _ = _split3(logf)
    c = (jnp.dot(tri, hi, preferred_element_type=F32)
         + jnp.dot(tri, mid, preferred_element_type=F32)) + carry_ref[...]
    c_ref[...] = c[:, :ATTN_HEADS]
    carry_ref[...] = c[tm - 1:tm, :]


def _inproj(x2, g, w_main, w_f, b_f, *, seq, tm=512):
    n, d = x2.shape
    nm = w_main.shape[1]
    a = ATTN_WIDTH
    out_shape = (
        jax.ShapeDtypeStruct((n, a), BF16), jax.ShapeDtypeStruct((n, a), BF16),
        jax.ShapeDtypeStruct((n, a), BF16), jax.ShapeDtypeStruct((n, POOL_WIDTH), BF16),
        jax.ShapeDtypeStruct((n, SSM_WIDTH), F32), jax.ShapeDtypeStruct((n, ATTN_HEADS), F32))
    row = lambda w: pl.BlockSpec((tm, w), lambda i: (i, 0))
    full = lambda r, c: pl.BlockSpec((r, c), lambda i: (0, 0))
    return pl.pallas_call(
        functools.partial(_inproj_kernel, tiles_per_seq=seq // tm),
        grid=(n // tm,),
        in_specs=[row(d), full(1, d), full(d, nm), full(d, LANES), full(1, LANES)],
        out_specs=(row(a), row(a), row(a), row(POOL_WIDTH), row(SSM_WIDTH), row(ATTN_HEADS)),
        out_shape=out_shape,
        scratch_shapes=[pltpu.VMEM((1, LANES), F32)],
        compiler_params=_params(("arbitrary",)),
        name="inproj",
    )(x2, g, w_main, w_f, b_f)


HEADS_PER_BLOCK = LANES // ATTN_HEAD_DIM


def _attn_kernel(q_ref, k_ref, v_ref, ct_ref, o_ref, *, tq, tk):
    nlb = q_ref.shape[2] // LANES
    hp = pl.program_id(1)
    qi = pl.program_id(2)
    nkb = k_ref.shape[1] // tk
    lane = lax.broadcasted_iota(jnp.int32, (1, LANES), 1)
    in_head = [(lane // ATTN_HEAD_DIM) == hh for hh in range(HEADS_PER_BLOCK)]
    chains = [(lb, hh) for lb in range(nlb) for hh in range(HEADS_PER_BLOCK)]
    qm, crow, cref = [], [], []
    for lb, hh in chains:
        q2 = q_ref[0, :, lb * LANES:(lb + 1) * LANES]
        qm.append(jnp.where(in_head[hh], q2, jnp.zeros_like(q2)))
        crow.append(((hp * nlb + lb) * HEADS_PER_BLOCK + hh) * nkb)
        cref.append(ct_ref[0, pl.ds(crow[-1] + qi, 1), :][:, 0:1])

    def step(kb, carry, masked):
        new = []
        for ci, (lb, hh) in enumerate(chains):
            kblk = k_ref[0, pl.ds(kb * tk, tk), lb * LANES:(lb + 1) * LANES]
            vblk = v_ref[0, pl.ds(kb * tk, tk), lb * LANES:(lb + 1) * LANES]
            m, acc = carry[ci]
            s = lax.dot_general(qm[ci], kblk, (((1,), (1,)), ((), ())), preferred_element_type=F32)
            s = s - (ct_ref[0, pl.ds(crow[ci] + kb, 1), :] - cref[ci]) * LOG2E
            if masked:
                r = lax.broadcasted_iota(jnp.int32, (tq, tk), 0)
                cidx = lax.broadcasted_iota(jnp.int32, (tq, tk), 1)
                s = jnp.where(cidx <= r, s, NEG_INF)
            m_new = jnp.maximum(m, jnp.max(s, axis=1, keepdims=True))
            p = jnp.exp2(s - m_new)
            vsel = jnp.where(in_head[hh], vblk, jnp.ones_like(vblk))
            acc = jnp.exp2(m - m_new) * acc + jnp.dot(p.astype(BF16), vsel, preferred_element_type=F32)
            new.append((m_new, acc))
        return tuple(new)

    init = tuple((jnp.full((tq, 1), NEG_INF, F32), jnp.zeros((tq, LANES), F32)) for _ in chains)
    carry = lax.fori_loop(0, qi, lambda kb, cr: step(kb, cr, False), init)
    carry = step(qi, carry, True)
    for lb in range(nlb):
        out = jnp.zeros((tq, LANES), F32)
        for hh in range(HEADS_PER_BLOCK):
            acc = carry[lb * HEADS_PER_BLOCK + hh][1]
            out = jnp.where(in_head[hh], acc / pltpu.roll(acc, ATTN_HEAD_DIM, 1), out)
        o_ref[0, :, lb * LANES:(lb + 1) * LANES] = out.astype(o_ref.dtype)


def _attention(q, k, v, ct, *, tq, lane_blocks=2):
    b, L, a = q.shape
    tk = ct.shape[2]
    assert tq == tk, "the diagonal key block of query tile i must be key block i"
    w = lane_blocks * LANES
    blk = lambda: pl.BlockSpec((1, tq, w), lambda bi, hi, qi: (bi, qi, hi))
    seq = lambda: pl.BlockSpec((1, L, w), lambda bi, hi, qi: (bi, 0, hi))
    return pl.pallas_call(
        functools.partial(_attn_kernel, tq=tq, tk=tk),
        grid=(b, a // w, L // tq),
        in_specs=[blk(), seq(), seq(),
                  pl.BlockSpec((1, ct.shape[1], tk), lambda bi, hi, qi: (bi, 0, 0))],
        out_specs=blk(),
        out_shape=jax.ShapeDtypeStruct((b, L, a), BF16),
        compiler_params=_params(("arbitrary", "arbitrary", "arbitrary")),
        name="fox_attention",
    )(q, k, v, ct)


def _pool_kernel(u_ref, w_ref, s_ref, o_ref):
    x = u_ref[0].astype(F32)
    L, w = x.shape
    row = lax.broadcasted_iota(jnp.int32, (L, w), 0)
    group = lax.broadcasted_iota(jnp.int32, (L, w), 1) // POOL_GROUP_DIM

    def shifted(y, s):
        return jnp.where(row >= s, pltpu.roll(y, s, 0), 0.0)

    acc = x
    sel = jnp.zeros_like(x)
    win_lane = jnp.zeros((L, w), F32)
    span = 1
    for gi, win in enumerate(POOL_WINDOWS):
        while span < win:
            acc = acc + shifted(acc, span)
            span *= 2
        sel = jnp.where(group == gi, acc, sel)
        win_lane = jnp.where(group == gi, float(win), win_lane)
    count = jnp.minimum(row.astype(F32) + 1.0, win_lane)
    d = sel / count - x
    y = jnp.dot(d.astype(BF16), w_ref[...], preferred_element_type=F32) * s_ref[...]
    o_ref[0] = y.astype(o_ref.dtype)


def _pool(u, w_bd, scale):
    b, L, w = u.shape
    return pl.pallas_call(
        _pool_kernel,
        grid=(b,),
        in_specs=[pl.BlockSpec((1, L, w), lambda i: (i, 0, 0)),
                  pl.BlockSpec((w, w), lambda i: (0, 0)),
                  pl.BlockSpec((1, w), lambda i: (0, 0))],
        out_specs=pl.BlockSpec((1, L, w), lambda i: (i, 0, 0)),
        out_shape=jax.ShapeDtypeStruct((b, L, w), BF16),
        compiler_params=_params(("arbitrary",)),
        name="pool_mixer",
    )(u, w_bd, scale)


def _group_of(idx, width):
    groups = LANES // SSM_GROUP_DIM
    return (idx >> (width.bit_length() - 1)) & (groups - 1)


def _expand_blockdiag(r_ref, row_inner, col_inner):
    rows, w = r_ref.shape[1:]
    groups = LANES // SSM_GROUP_DIM
    r = lax.broadcasted_iota(jnp.int32, (w, groups * w), 0)
    c = lax.broadcasted_iota(jnp.int32, (w, groups * w), 1)
    shift = col_inner.bit_length() - 1
    outer_c = c >> (shift + groups.bit_length() - 1)
    spread = (outer_c == (r >> shift)) & ((c & (col_inner - 1)) == (r & (col_inner - 1)))
    big = jnp.dot(r_ref[0], spread.astype(BF16), preferred_element_type=F32)
    ri = lax.broadcasted_iota(jnp.int32, (rows, groups * w), 0)
    ci = lax.broadcasted_iota(jnp.int32, (rows, groups * w), 1)
    return jnp.where(_group_of(ri, row_inner) == _group_of(ci, col_inner), big, 0.0).astype(BF16)


def _ssm_kernel(u_ref, rm_ref, rz_ref, ry_ref, coef_ref, y_ref, m_s, wz_s, wy_s, ucat, z_ref, s_ref, *, seqs):
    t = SSM_CHUNK

    @pl.when(pl.program_id(1) == 0)
    def _():
        m_s[...] = _expand_blockdiag(rm_ref, SSM_GROUP_DIM, SSM_GROUP_DIM)
        wz_s[...] = _expand_blockdiag(rz_ref, SSM_GROUP_DIM, SSM_STATE)
        wy_s[...] = _expand_blockdiag(ry_ref, SSM_STATE, SSM_GROUP_DIM)

    nc = u_ref.shape[0] // t
    cps = nc // seqs
    for k in range(t):
        ucat[:, k * LANES:(k + 1) * LANES] = u_ref[pl.ds(k, nc, stride=t), :].astype(BF16)
    u = ucat[...]
    z_ref[...] = jnp.dot(u, wz_s[...], preferred_element_type=F32)
    half = z_ref.shape[1] // 2
    a1 = coef_ref[0, 0:1, :]
    a2 = coef_ref[0, 1:2, :]

    def step(c, states):
        new = []
        for b in range(seqs):
            st = states[b]
            r = b * cps + c
            s_ref[pl.ds(r, 1), :] = st
            sw = jnp.concatenate([st[:, half:], st[:, :half]], axis=1)
            new.append(a1 * st + a2 * sw + z_ref[pl.ds(r, 1), :])
        return tuple(new)

    init = tuple(jnp.zeros((1, 2 * half), F32) for _ in range(seqs))
    lax.fori_loop(0, cps, step, init, unroll=8)
    y = jnp.dot(u, m_s[...], preferred_element_type=F32)
    y = y + jnp.dot(s_ref[...].astype(BF16), wy_s[...], preferred_element_type=F32)
    for k in range(t):
        y_ref[pl.ds(k, nc, stride=t), :] = y[:, k * LANES:(k + 1) * LANES]


def _ssm_core(u, rm, rz, ry, coef, *, seq, seqs_per_block=2):
    n, w = u.shape
    rows = seq * seqs_per_block
    nc = rows // SSM_CHUNK
    kw = SSM_CHUNK * LANES
    p2 = coef.shape[2]
    blk = pl.BlockSpec((rows, LANES), lambda hf, r: (r, hf))
    wspec = lambda arr: pl.BlockSpec((1,) + arr.shape[1:], lambda hf, r: (hf, 0, 0))
    return pl.pallas_call(
        functools.partial(_ssm_kernel, seqs=seqs_per_block),
        grid=(w // LANES, n // rows),
        in_specs=[blk, wspec(rm), wspec(rz), wspec(ry), wspec(coef)],
        out_specs=blk,
        out_shape=jax.ShapeDtypeStruct((n, w), F32),
        scratch_shapes=[pltpu.VMEM((kw, kw), BF16), pltpu.VMEM((kw, p2), BF16), pltpu.VMEM((p2, kw), BF16),
                        pltpu.VMEM((nc, kw), BF16), pltpu.VMEM((nc, p2), F32), pltpu.VMEM((nc, p2), F32)],
        compiler_params=_params(("arbitrary", "arbitrary")),
        name="s5_core",
    )(u, rm, rz, ry, coef)


def _ssm_matrices(a_re, a_im, b_re, b_im, c_re, c_im, d, log_dt):
    t = SSM_CHUNK
    g, p = a_re.shape
    hdim = d.shape[1]
    gl = LANES // hdim
    nb = g // gl
    lam = lax.complex(a_re, a_im)
    lam_dt = lam * jnp.exp(log_dt)[:, None]
    lam_bar = jnp.exp(lam_dt)
    b_bar = ((lam_bar - 1.0) / lam)[:, :, None] * lax.complex(b_re, b_im)
    cmat = lax.complex(c_re, c_im)
    steps = jnp.arange(t + 1, dtype=F32)
    pw = jnp.exp(lam_dt[:, None, :] * steps[None, :, None])
    kern = jnp.einsum('gop,gdp,gpi->gdoi', cmat, pw[:, :t], b_bar).real
    kern = kern.at[:, 0].add(jax.vmap(jnp.diag)(d))
    j = jnp.arange(t)[:, None]
    i = jnp.arange(t)[None, :]
    m = jnp.where((i >= j)[None, :, :, None, None], kern[:, jnp.clip(i - j, 0, t - 1)], 0.0)
    wz = jnp.einsum('gjp,gpi->gjip', pw[:, t - 1 - jnp.arange(t)], b_bar)
    cw = jnp.einsum('gop,gip->giop', cmat, pw[:, 1:t + 1])
    rm = m.reshape(nb, gl, t, t, hdim, hdim).transpose(0, 2, 1, 5, 3, 4).reshape(nb, t * LANES, t * hdim)
    rz = jnp.stack([wz.real, wz.imag], axis=3).reshape(nb, gl, t, hdim, 2, p)
    rz = rz.transpose(0, 2, 1, 3, 4, 5).reshape(nb, t * LANES, 2 * p)
    ry = jnp.stack([cw.real, -cw.imag], axis=1).reshape(nb, gl, 2, t, hdim, p)
    ry = ry.transpose(0, 2, 1, 5, 3, 4).reshape(nb, 2 * gl * p, t * hdim)
    ar = pw[:, t].real.reshape(nb, gl * p)
    ai = pw[:, t].imag.reshape(nb, gl * p)
    coef = jnp.stack([jnp.concatenate([ar, ar], -1), jnp.concatenate([-ai, ai], -1)], axis=1)
    return rm.astype(BF16), rz.astype(BF16), ry.astype(BF16), coef.astype(F32)


def _mixout_kernel(*refs, moe):
    (x_ref, attn_ref, pool_ref, ys_ref, wglu_ref, bglu_ref, ga_ref, gp_ref, gs_ref,
     wo_ref, gpost_ref, gpre_ref) = refs[:12]
    if moe:
        router_ref, xo_ref, h_ref, ri_ref, rg_ref, cnt_ref, cnt_acc = refs[12:]

        @pl.when(pl.program_id(0) == 0)
        def _():
            cnt_acc[...] = jnp.zeros_like(cnt_acc)
    else:
        xo_ref, h_ref = refs[12:]
    a = attn_ref.shape[1]
    pw = pool_ref.shape[1]
    attn_n = _rms(attn_ref[...].astype(F32), ga_ref[...]).astype(BF16)
    pool_n = _rms(pool_ref[...].astype(F32), gp_ref[...]).astype(BF16)
    y = jax.nn.gelu(ys_ref[...])
    gate = jnp.dot(y.astype(BF16), wglu_ref[...], preferred_element_type=F32) + bglu_ref[...]
    ssm = y * jax.nn.sigmoid(gate)
    ssm_n = _rms(ssm, gs_ref[...]).astype(BF16)
    mix = (jnp.dot(attn_n, wo_ref[0:a, :], preferred_element_type=F32)
           + jnp.dot(pool_n, wo_ref[a:a + pw, :], preferred_element_type=F32)
           + jnp.dot(ssm_n, wo_ref[a + pw:, :], preferred_element_type=F32))
    x = x_ref[...] + _rms(mix, gpost_ref[...])
    xo_ref[...] = x
    h = _rms(x, gpre_ref[...])
    if moe:
        _store_slabs(h_ref, 0, h)
    else:
        h_ref[...] = h.astype(h_ref.dtype)
    if moe:
        hi, mid, _ = _split3(h)
        rhi, rmid, _ = _split3(router_ref[...])
        dot_t = lambda p, q: lax.dot_general(p, q, (((1,), (1,)), ((), ())), preferred_element_type=F32)
        logits = dot_t(rhi, hi) + dot_t(rmid, hi) + dot_t(rhi, mid)
        ne, tm = logits.shape
        eidx = lax.broadcasted_iota(jnp.int32, (ne, tm), 0)
        m1 = jnp.max(logits, axis=0, keepdims=True)
        i1 = jnp.min(jnp.where(logits == m1, eidx, ne), axis=0, keepdims=True)
        rest = jnp.where(eidx == i1, -jnp.inf, logits)
        m2 = jnp.max(rest, axis=0, keepdims=True)
        i2 = jnp.min(jnp.where(rest == m2, eidx, ne), axis=0, keepdims=True)
        e2 = jnp.exp(m2 - m1)
        g1 = 1.0 / (1.0 + e2)
        rg_ref[...] = jnp.where(eidx == 0, g1, jnp.where(eidx == 1, e2 * g1, 0.0))
        onehot = ((eidx == i1) | (eidx == i2)).astype(BF16)
        earlier = (lax.broadcasted_iota(jnp.int32, (tm, tm), 0)
                   < lax.broadcasted_iota(jnp.int32, (tm, tm), 1)).astype(BF16)
        before = jnp.dot(onehot, earlier, preferred_element_type=F32) + cnt_acc[:, 0:1]
        r1 = jnp.sum(jnp.where(eidx == i1, before, 0.0), axis=0, keepdims=True).astype(jnp.int32)
        r2 = jnp.sum(jnp.where(eidx == i2, before, 0.0), axis=0, keepdims=True).astype(jnp.int32)
        ri_ref[...] = jnp.where(eidx == 0, i1, jnp.where(eidx == 1, i2,
                                jnp.where(eidx == 2, r1, jnp.where(eidx == 3, r2, 0))))
        cnt_acc[...] += jnp.sum(onehot.astype(F32), axis=1, keepdims=True)
        cnt_ref[...] = cnt_acc[...]


def _mixout(x2, attn, pool, ys, w_glu, b_glu, g_attn, g_pool, g_ssm, w_out, g_post, g_pre,
            router=None, *, tm=512):
    n, d = x2.shape
    moe = router is not None
    row = lambda w: pl.BlockSpec((tm, w), lambda i: (i, 0))
    full = lambda arr: pl.BlockSpec(arr.shape, lambda i: (0, 0))
    ins = [x2, attn, pool, ys, w_glu, b_glu, g_attn, g_pool, g_ssm, w_out, g_post, g_pre]
    in_specs = [row(d), row(attn.shape[1]), row(pool.shape[1]), row(ys.shape[1])] + [full(t) for t in ins[4:]]
    if moe:
        assert d == SUBLANES * LANES
        h_shape = jax.ShapeDtypeStruct((n * SUBLANES, LANES), F32)
        h_spec = pl.BlockSpec((tm * SUBLANES, LANES), lambda i: (i, 0))
    else:
        h_shape, h_spec = jax.ShapeDtypeStruct((n, d), BF16), row(d)
    out_shape = [jax.ShapeDtypeStruct((n, d), F32), h_shape]
    out_specs = [row(d), h_spec]
    scratch = []
    if moe:
        ins.append(router)
        in_specs.append(full(router))
        ne = router.shape[0]
        col = pl.BlockSpec((ne, tm), lambda i: (0, i))
        out_shape += [jax.ShapeDtypeStruct((ne, n), jnp.int32), jax.ShapeDtypeStruct((ne, n), F32),
                      jax.ShapeDtypeStruct((ne, LANES), F32)]
        out_specs += [col, col, pl.BlockSpec((ne, LANES), lambda i: (0, 0))]
        scratch = [pltpu.VMEM((ne, LANES), F32)]
    return pl.pallas_call(
        functools.partial(_mixout_kernel, moe=moe),
        grid=(n // tm,),
        in_specs=in_specs,
        out_specs=tuple(out_specs),
        out_shape=tuple(out_shape),
        scratch_shapes=scratch,
        compiler_params=_params(("arbitrary",)),
        name="mix_out",
    )(*ins)


FFN_STAGE_CHUNKS = 4


def _stage_cast(src_hbm, dst, lane0, stage, sem):
    rows = stage.shape[1]
    width = src_hbm.shape[1]

    def copy(c):
        return pltpu.make_async_copy(src_hbm.at[pl.ds(c * rows, rows)], stage.at[c % 2], sem.at[c % 2])

    nchunks = src_hbm.shape[0] // rows
    copy(0).start()
    for c in range(nchunks):
        if c + 1 < nchunks:
            copy(c + 1).start()
        copy(c).wait()
        dst[pl.ds(c * rows, rows), lane0:lane0 + width] = stage[c % 2].astype(BF16)


def _ffn_kernel(h_ref, x_ref, wg_hbm, wu_hbm, wd_hbm, g_ref, o_ref, wgu_s, wd_s, stage_in, stage_out, sem, *, sub):
    fdim = wd_s.shape[0]

    @pl.when(pl.program_id(0) == 0)
    def _():
        _stage_cast(wg_hbm, wgu_s, 0, stage_in, sem)
        _stage_cast(wu_hbm, wgu_s, fdim, stage_in, sem)
        _stage_cast(wd_hbm, wd_s, 0, stage_out, sem)

    for r in range(h_ref.shape[0] // sub):
        rows = pl.ds(r * sub, sub)
        hgu = jnp.dot(h_ref[rows, :], wgu_s[...], preferred_element_type=F32)
        hg, hu = hgu[:, :fdim], hgu[:, fdim:]
        act = (hg * jax.nn.sigmoid(hg) * hu).astype(BF16)
        y = jnp.dot(act, wd_s[...], preferred_element_type=F32)
        o_ref[rows, :] = x_ref[rows, :] + _rms(y, g_ref[...])


def _ffn(h, x2, wg, wu, wd, g_post, *, tm=512, sub=256):
    n, d = x2.shape
    fdim = wg.shape[1]
    row = pl.BlockSpec((tm, d), lambda i: (i, 0))
    hbm = pl.BlockSpec(memory_space=pl.ANY)
    return pl.pallas_call(
        functools.partial(_ffn_kernel, sub=sub),
        grid=(n // tm,),
        in_specs=[row, row, hbm, hbm, hbm, pl.BlockSpec((1, d), lambda i: (0, 0))],
        out_specs=row,
        out_shape=jax.ShapeDtypeStruct((n, d), F32),
        scratch_shapes=[pltpu.VMEM((d, 2 * fdim), BF16), pltpu.VMEM((fdim, d), BF16),
                        pltpu.VMEM((2, d // FFN_STAGE_CHUNKS, fdim), F32),
                        pltpu.VMEM((2, fdim // FFN_STAGE_CHUNKS, d), F32),
                        pltpu.SemaphoreType.DMA((2,))],
        compiler_params=_params(("arbitrary",)),
        name="dense_ffn",
    )(h, x2, wg, wu, wd, g_post)


MOE_TILE = 512


def _moe_tables(counts, n_items):
    tm = MOE_TILE
    counts = counts.astype(jnp.int32)
    ntiles = (counts + tm - 1) // tm
    k = jnp.arange(ntiles.shape[0])
    ends = jnp.sum(jnp.where(k[None, :] <= k[:, None], ntiles[None, :], 0), axis=1)
    starts = ends - ntiles
    w = jnp.arange(n_items, dtype=jnp.int32)
    wc = jnp.minimum(w, ends[-1] - 1)
    e = jnp.sum((wc[:, None] >= ends[None, :]).astype(jnp.int32), axis=1)
    valid = w < ends[-1]
    rowblock = jnp.where(valid, w, n_items)
    nvalid = jnp.where(valid, jnp.clip(counts[e] - (wc - starts[e]) * tm, 0, tm), 0)
    first = ((w == 0) | (e != jnp.roll(e, 1))).astype(jnp.int32)
    present = ntiles > 0
    later = present[None, :] & (k[None, :] > k[:, None])
    nxt = jnp.where(jnp.any(later, axis=1), jnp.argmax(later, axis=1), -1).astype(jnp.int32)
    slot = (jnp.sum((present[None, :] & (k[None, :] < k[:, None])).astype(jnp.int32), axis=1) % 2)
    return (e, rowblock, nvalid.astype(jnp.int32), first), (nxt[e], slot[e].astype(jnp.int32)), starts * tm


def _slab(ref, token):
    return ref.at[pl.ds(pl.multiple_of(token * SUBLANES, SUBLANES), SUBLANES)]


def _dispatch_kernel(dest_ref, h_ref, xs_ref, sem, *, n):
    tm = h_ref.shape[0] // SUBLANES
    base = pl.program_id(0) * tm

    def body(r, carry):
        for k in range(2):
            d = dest_ref[k * n + base + r]
            pltpu.make_async_copy(_slab(h_ref, r), _slab(xs_ref, d), sem).start(priority=k)
        return carry

    lax.fori_loop(0, tm, body, 0, unroll=8)
    for k in range(2):
        pltpu.make_async_copy(h_ref, h_ref, sem).wait()


def _dispatch(dest, h, cap_rows, *, tm=512):
    n = h.shape[0] // SUBLANES
    return pl.pallas_call(
        functools.partial(_dispatch_kernel, n=n),
        grid_spec=pltpu.PrefetchScalarGridSpec(
            num_scalar_prefetch=1,
            grid=(n // tm,),
            in_specs=[pl.BlockSpec((tm * SUBLANES, LANES), lambda i, dest: (i, 0))],
            out_specs=pl.BlockSpec(memory_space=pl.ANY),
            scratch_shapes=[pltpu.SemaphoreType.DMA(())]),
        out_shape=jax.ShapeDtypeStruct((cap_rows * SUBLANES, LANES), F32),
        compiler_params=_params(("arbitrary",)),
        name="moe_dispatch",
    )(dest, h)


MOE_SUB = 256


def _moe_up_kernel(ie_ref, rb_ref, nv_ref, first_ref, xs_ref, wg_ref, wu_ref, h_ref, wgu_s):
    w = pl.program_id(1)
    tf = h_ref.shape[1]

    @pl.when(first_ref[w] == 1)
    def _():
        wgu_s[:, :tf] = wg_ref[0].astype(BF16)
        wgu_s[:, tf:] = wu_ref[0].astype(BF16)

    nv = nv_ref[w]

    @pl.when(nv > 0)
    def _():
        for r in range(h_ref.shape[0] // MOE_SUB):
            rows = pl.ds(r * MOE_SUB, MOE_SUB)
            idx = r * MOE_SUB + lax.broadcasted_iota(jnp.int32, (MOE_SUB, 1), 0)
            x = _load_slabs(xs_ref, r * MOE_SUB, MOE_SUB)
            x = jnp.where(idx < nv, x, 0.0).astype(BF16)
            hgu = jnp.dot(x, wgu_s[...], preferred_element_type=F32)
            hg, hu = hgu[:, :tf], hgu[:, tf:]
            h_ref[rows, :] = (hg * jax.nn.sigmoid(hg) * hu).astype(BF16)

    @pl.when(nv == 0)
    def _():
        h_ref[...] = jnp.zeros_like(h_ref)


def _moe_up(tables, xs, wg, wu, *, tf=1792):
    tm = MOE_TILE
    cap_rows = xs.shape[0] // SUBLANES
    ne, d, fdim = wg.shape
    n_items = tables[0].shape[0]
    return pl.pallas_call(
        _moe_up_kernel,
        grid_spec=pltpu.PrefetchScalarGridSpec(
            num_scalar_prefetch=4,
            grid=(fdim // tf, n_items),
            in_specs=[pl.BlockSpec((tm * SUBLANES, LANES), lambda f, w, ie, rb, nv, fi: (rb[w], 0)),
                      pl.BlockSpec((1, d, tf), lambda f, w, ie, rb, nv, fi: (ie[w], 0, f)),
                      pl.BlockSpec((1, d, tf), lambda f, w, ie, rb, nv, fi: (ie[w], 0, f))],
            out_specs=pl.BlockSpec((tm, tf), lambda f, w, ie, rb, nv, fi: (rb[w], f)),
            scratch_shapes=[pltpu.VMEM((d, 2 * tf), BF16)]),
        out_shape=jax.ShapeDtypeStruct((cap_rows, fdim), BF16),
        compiler_params=_params(("arbitrary", "arbitrary")),
        name="moe_up",
    )(*tables, xs, wg, wu)


def _moe_down_kernel(ie_ref, rb_ref, nv_ref, first_ref, nxt_ref, slot_ref, h_ref, wd_hbm, y_ref,
                     wd_s, stage, sem):
    w = pl.program_id(0)
    nchunks = sem.shape[0]
    chunk = stage.shape[0] // nchunks

    def copies(e):
        return [pltpu.make_async_copy(wd_hbm.at[e, pl.ds(c * chunk, chunk)], stage.at[pl.ds(c * chunk, chunk)],
                                      sem.at[c]) for c in range(nchunks)]

    @pl.when(w == 0)
    def _():
        for cp in copies(ie_ref[0]):
            cp.start()

    @pl.when(first_ref[w] == 1)
    def _():
        for c, cp in enumerate(copies(ie_ref[w])):
            cp.wait()
            wd_s[slot_ref[w], pl.ds(c * chunk, chunk), :] = stage[pl.ds(c * chunk, chunk), :].astype(BF16)

        @pl.when(nxt_ref[w] >= 0)
        def _():
            for cp in copies(nxt_ref[w]):
                cp.start()

    @pl.when(nv_ref[w] > 0)
    def _():
        wd = wd_s[slot_ref[w]]
        for r in range(h_ref.shape[0] // MOE_SUB):
            y = jnp.dot(h_ref[pl.ds(r * MOE_SUB, MOE_SUB), :], wd, preferred_element_type=F32)
            _store_slabs(y_ref, r * MOE_SUB, y)

    @pl.when(nv_ref[w] == 0)
    def _():
        y_ref[...] = jnp.zeros_like(y_ref)


def _moe_down(tables, order, hid, wd, *, chunk=512):
    tm = MOE_TILE
    cap_rows, fdim = hid.shape
    d = wd.shape[2]
    n_items = tables[0].shape[0]
    assert d == SUBLANES * LANES and fdim % chunk == 0
    return pl.pallas_call(
        _moe_down_kernel,
        grid_spec=pltpu.PrefetchScalarGridSpec(
            num_scalar_prefetch=6,
            grid=(n_items,),
            in_specs=[pl.BlockSpec((tm, fdim), lambda w, ie, rb, *_: (rb[w], 0)),
                      pl.BlockSpec(memory_space=pl.ANY)],
            out_specs=pl.BlockSpec((tm * SUBLANES, LANES), lambda w, ie, rb, *_: (rb[w], 0)),
            scratch_shapes=[pltpu.VMEM((2, fdim, d), BF16), pltpu.VMEM((fdim, d), F32),
                            pltpu.SemaphoreType.DMA((fdim // chunk,))]),
        out_shape=jax.ShapeDtypeStruct((cap_rows * SUBLANES, LANES), F32),
        compiler_params=_params(("arbitrary",)),
        name="moe_down",
    )(*tables, *order, hid, wd)


def _combine_kernel(dest_ref, x_ref, gate_ref, g_ref, ys_ref, o_ref, buf, sem, *, n):
    tm = x_ref.shape[0]
    i = pl.program_id(0)

    def issue(tile, slot):
        def body(r, carry):
            for k in range(2):
                d = dest_ref[k * n + tile * tm + r]
                pltpu.make_async_copy(_slab(ys_ref, d), _slab(buf.at[slot, k], r),
                                      sem.at[slot]).start(priority=k)
            return carry
        lax.fori_loop(0, tm, body, 0, unroll=8)

    @pl.when(i == 0)
    def _():
        issue(0, 0)

    @pl.when(i + 1 < pl.num_programs(0))
    def _():
        issue(i + 1, (i + 1) % 2)

    slot = i % 2
    for k in range(2):
        pltpu.make_async_copy(buf.at[slot, k], buf.at[slot, k], sem.at[slot]).wait()
    gates = gate_ref[...].T
    f = (gates[:, 0:1] * _load_slabs(buf.at[slot, 0], 0, tm)
         + gates[:, 1:2] * _load_slabs(buf.at[slot, 1], 0, tm))
    o_ref[...] = x_ref[...] + _rms(f, g_ref[...])


def _combine(dest, x2, gates, g_post, ys, *, tm=256):
    n, d = x2.shape
    return pl.pallas_call(
        functools.partial(_combine_kernel, n=n),
        grid_spec=pltpu.PrefetchScalarGridSpec(
            num_scalar_prefetch=1,
            grid=(n // tm,),
            in_specs=[pl.BlockSpec((tm, d), lambda i, dest: (i, 0)),
                      pl.BlockSpec((gates.shape[0], tm), lambda i, dest: (0, i)),
                      pl.BlockSpec((1, d), lambda i, dest: (0, 0)),
                      pl.BlockSpec(memory_space=pl.ANY)],
            out_specs=pl.BlockSpec((tm, d), lambda i, dest: (i, 0)),
            scratch_shapes=[pltpu.VMEM((2, 2, tm * SUBLANES, LANES), F32), pltpu.SemaphoreType.DMA((2,))]),
        out_shape=jax.ShapeDtypeStruct((n, d), F32),
        compiler_params=_params(("arbitrary",)),
        name="moe_combine",
    )(dest, x2, gates, g_post, ys)


def _moe(h, x2, route_i, route_g, counts, wg, wu, wd, g_post):
    n = x2.shape[0]
    n_items = 2 * n // MOE_TILE + N_EXPERTS
    tables, order, base = _moe_tables(counts[:, 0], n_items)
    base_of = lambda e: jnp.sum(jnp.where(e[None, :] == jnp.arange(N_EXPERTS)[:, None], base[:, None], 0), axis=0)
    dest = jnp.concatenate([base_of(route_i[0]) + route_i[2], base_of(route_i[1]) + route_i[3]])
    xs = _dispatch(dest, h, (n_items + 1) * MOE_TILE)
    hid = _moe_up(tables, xs, wg, wu)
    ys = _moe_down(tables, order, hid, wd)
    return _combine(dest, x2, route_g, g_post, ys)


def _row(v):
    return v.reshape(1, -1).astype(F32)


def kernel(x, norm_mix_pre, norm_mix_post, norm_ffn_pre, norm_ffn_post, w_in, b_forget, pool_w, pool_scale, ssm_a_re, ssm_a_im, ssm_b_re, ssm_b_im, ssm_c_re, ssm_c_im, ssm_d, ssm_log_dt, ssm_w_glu, ssm_b_glu, branch_norm_attn, branch_norm_pool, branch_norm_ssm, w_out, ffn_w_gate, ffn_w_up, ffn_w_down, moe_router, moe_w_gate, moe_w_up, moe_w_down):
    b, L, d = x.shape
    depth = w_in.shape[0]
    n = b * L
    a = ATTN_WIDTH
    f_off = 3 * a
    p_off = f_off + ATTN_HEADS
    attn_tile = 512
    x2 = x.reshape(n, d)
    for i in range(depth):
        w = w_in[i]
        w_main = jnp.concatenate([w[:, :f_off], w[:, p_off:]], axis=1).astype(BF16)
        w_f = jnp.pad(w[:, f_off:p_off], ((0, 0), (0, LANES - ATTN_HEADS))).astype(BF16)
        b_f = jnp.pad(b_forget[i], (0, LANES - ATTN_HEADS)).reshape(1, LANES)
        q, k, v, up, us, c = _inproj(x2, _row(norm_mix_pre[i]), w_main, w_f, b_f, seq=L)

        ct = c.reshape(b, L, ATTN_HEADS).transpose(0, 2, 1).reshape(b, ATTN_HEADS * (L // attn_tile), attn_tile)
        attn = _attention(q.reshape(b, L, a), k.reshape(b, L, a), v.reshape(b, L, a),
                          ct, tq=attn_tile).reshape(n, a)

        pool = _pool(up.reshape(b, L, POOL_WIDTH),
                     jax.scipy.linalg.block_diag(*pool_w[i]).astype(BF16),
                     _row(pool_scale[i])).reshape(n, POOL_WIDTH)

        mats = _ssm_matrices(ssm_a_re[i], ssm_a_im[i], ssm_b_re[i], ssm_b_im[i],
                             ssm_c_re[i], ssm_c_im[i], ssm_d[i], ssm_log_dt[i])
        ys = _ssm_core(us, *mats, seq=L)

        moe = i % 2 == 1
        j = i // 2
        router = moe_router[j].T if moe else None
        res = _mixout(x2, attn, pool, ys, ssm_w_glu[i].astype(BF16), _row(ssm_b_glu[i]),
                      _row(branch_norm_attn[i]), _row(branch_norm_pool[i]), _row(branch_norm_ssm[i]),
                      w_out[i].astype(BF16), _row(norm_mix_post[i]), _row(norm_ffn_pre[i]), router)
        if moe:
            x2, h, route_i, route_g, counts = res
            x2 = _moe(h, x2, route_i, route_g, counts, moe_w_gate[j], moe_w_up[j], moe_w_down[j],
                      _row(norm_ffn_post[i]))
        else:
            x2, h = res
            x2 = _ffn(h, x2, ffn_w_gate[j], ffn_w_up[j], ffn_w_down[j], _row(norm_ffn_post[i]))
    return x2.reshape(b, L, d)
```

```python
import functools

import jax
import jax.numpy as jnp
from jax import lax
from jax.experimental import pallas as pl
from jax.experimental.pallas import tpu as pltpu

F32 = jnp.float32
BF16 = jnp.bfloat16

RMS_EPS = 1e-6
NEG_INF = -1e30
LOG2E = 1.4426950408889634

ATTN_HEADS = 8
ATTN_HEAD_DIM = 64
ATTN_WIDTH = ATTN_HEADS * ATTN_HEAD_DIM
POOL_WINDOWS = (2, 4, 8, 16)
POOL_GROUP_DIM = 64
POOL_WIDTH = len(POOL_WINDOWS) * POOL_GROUP_DIM
SSM_GROUPS = 16
SSM_GROUP_DIM = 16
SSM_STATE = 64
SSM_WIDTH = SSM_GROUPS * SSM_GROUP_DIM
SSM_CHUNK = 8
N_EXPERTS = 8

LANES = 128
SUBLANES = 8
VMEM_LIMIT = 56 * 1024 * 1024


def _params(sem):
    return pltpu.CompilerParams(dimension_semantics=sem, vmem_limit_bytes=VMEM_LIMIT)


def _rms(x, g):
    return x * lax.rsqrt(jnp.mean(x * x, axis=-1, keepdims=True) + RMS_EPS) * g


def _load_slabs(ref, row0, rows):
    return jnp.concatenate([ref[pl.ds(row0 * SUBLANES + s, rows, stride=SUBLANES), :]
                            for s in range(SUBLANES)], axis=1)


def _store_slabs(ref, row0, val):
    for s in range(SUBLANES):
        ref[pl.ds(row0 * SUBLANES + s, val.shape[0], stride=SUBLANES), :] = val[:, s * LANES:(s + 1) * LANES]


def _split3(x):
    hi = x.astype(BF16)
    r = x - hi.astype(F32)
    mid = r.astype(BF16)
    lo = (r - mid.astype(F32)).astype(BF16)
    return hi, mid, lo


def _inproj_kernel(x_ref, g_ref, wm_ref, wf_ref, bf_ref,
                   q_ref, k_ref, v_ref, up_ref, us_ref, c_ref, carry_ref, *, tiles_per_seq):
    @pl.when(pl.program_id(0) % tiles_per_seq == 0)
    def _():
        carry_ref[...] = jnp.zeros_like(carry_ref)

    tm = x_ref.shape[0]
    h = _rms(x_ref[...], g_ref[...]).astype(BF16)
    proj = jnp.dot(h, wm_ref[...], preferred_element_type=F32)
    a = ATTN_WIDTH
    q_ref[...] = (proj[:, 0:a] * (LOG2E * ATTN_HEAD_DIM ** -0.5)).astype(BF16)
    k_ref[...] = proj[:, a:2 * a].astype(BF16)
    v_ref[...] = proj[:, 2 * a:3 * a].astype(BF16)
    up_ref[...] = proj[:, 3 * a:3 * a + POOL_WIDTH].astype(BF16)
    us_ref[...] = proj[:, 3 * a + POOL_WIDTH:]

    z = jnp.dot(h, wf_ref[...], preferred_element_type=F32) + bf_ref[...]
    logf = jnp.minimum(z, 0.0) - jnp.log(1.0 + jnp.exp(-jnp.abs(z)))
    row = lax.broadcasted_iota(jnp.int32, (tm, tm), 0)
    col = lax.broadcasted_iota(jnp.int32, (tm, tm), 1)
    tri = (row >= col).astype(BF16)
    hi, mid, _ = _split3(logf)
    c = (jnp.dot(tri, hi, preferred_element_type=F32)
         + jnp.dot(tri, mid, preferred_element_type=F32)) + carry_ref[...]
    c_ref[...] = c[:, :ATTN_HEADS]
    carry_ref[...] = c[tm - 1:tm, :]


def _inproj(x2, g, w_main, w_f, b_f, *, seq, tm=512):
    n, d = x2.shape
    nm = w_main.shape[1]
    a = ATTN_WIDTH
    out_shape = (
        jax.ShapeDtypeStruct((n, a), BF16), jax.ShapeDtypeStruct((n, a), BF16),
        jax.ShapeDtypeStruct((n, a), BF16), jax.ShapeDtypeStruct((n, POOL_WIDTH), BF16),
        jax.ShapeDtypeStruct((n, SSM_WIDTH), F32), jax.ShapeDtypeStruct((n, ATTN_HEADS), F32))
    row = lambda w: pl.BlockSpec((tm, w), lambda i: (i, 0))
    full = lambda r, c: pl.BlockSpec((r, c), lambda i: (0, 0))
    return pl.pallas_call(
        functools.partial(_inproj_kernel, tiles_per_seq=seq // tm),
        grid=(n // tm,),
        in_specs=[row(d), full(1, d), full(d, nm), full(d, LANES), full(1, LANES)],
        out_specs=(row(a), row(a), row(a), row(POOL_WIDTH), row(SSM_WIDTH), row(ATTN_HEADS)),
        out_shape=out_shape,
        scratch_shapes=[pltpu.VMEM((1, LANES), F32)],
        compiler_params=_params(("arbitrary",)),
        name="inproj",
    )(x2, g, w_main, w_f, b_f)


HEADS_PER_BLOCK = LANES // ATTN_HEAD_DIM


def _attn_kernel(q_ref, k_ref, v_ref, ct_ref, o_ref, *, tq, tk):
    nlb = q_ref.shape[2] // LANES
    hp = pl.program_id(1)
    qi = pl.program_id(2)
    nkb = k_ref.shape[1] // tk
    lane = lax.broadcasted_iota(jnp.int32, (1, LANES), 1)
    in_head = [(lane // ATTN_HEAD_DIM) == hh for hh in range(HEADS_PER_BLOCK)]
    chains = [(lb, hh) for lb in range(nlb) for hh in range(HEADS_PER_BLOCK)]
    qm, crow, cref = [], [], []
    for lb, hh in chains:
        q2 = q_ref[0, :, lb * LANES:(lb + 1) * LANES]
        qm.append(jnp.where(in_head[hh], q2, jnp.zeros_like(q2)))
        crow.append(((hp * nlb + lb) * HEADS_PER_BLOCK + hh) * nkb)
        cref.append(ct_ref[0, pl.ds(crow[-1] + qi, 1), :][:, 0:1])

    def step(kb, carry, masked):
        new = []
        for ci, (lb, hh) in enumerate(chains):
            kblk = k_ref[0, pl.ds(kb * tk, tk), lb * LANES:(lb + 1) * LANES]
            vblk = v_ref[0, pl.ds(kb * tk, tk), lb * LANES:(lb + 1) * LANES]
            m, acc = carry[ci]
            s = lax.dot_general(qm[ci], kblk, (((1,), (1,)), ((), ())), preferred_element_type=F32)
            s = s - (ct_ref[0, pl.ds(crow[ci] + kb, 1), :] - cref[ci]) * LOG2E
            if masked:
                r = lax.broadcasted_iota(jnp.int32, (tq, tk), 0)
                cidx = lax.broadcasted_iota(jnp.int32, (tq, tk), 1)
                s = jnp.where(cidx <= r, s, NEG_INF)
            m_new = jnp.maximum(m, jnp.max(s, axis=1, keepdims=True))
            p = jnp.exp2(s - m_new)
            vsel = jnp.where(in_head[hh], vblk, jnp.ones_like(vblk))
            acc = jnp.exp2(m - m_new) * acc + jnp.dot(p.astype(BF16), vsel, preferred_element_type=F32)
            new.append((m_new, acc))
        return tuple(new)

    init = tuple((jnp.full((tq, 1), NEG_INF, F32), jnp.zeros((tq, LANES), F32)) for _ in chains)
    carry = lax.fori_loop(0, qi, lambda kb, cr: step(kb, cr, False), init)
    carry = step(qi, carry, True)
    for lb in range(nlb):
        out = jnp.zeros((tq, LANES), F32)
        for hh in range(HEADS_PER_BLOCK):
            acc = carry[lb * HEADS_PER_BLOCK + hh][1]
            out = jnp.where(in_head[hh], acc / pltpu.roll(acc, ATTN_HEAD_DIM, 1), out)
        o_ref[0, :, lb * LANES:(lb + 1) * LANES] = out.astype(o_ref.dtype)


def _attention(q, k, v, ct, *, tq, lane_blocks=2):
    b, L, a = q.shape
    tk = ct.shape[2]
    assert tq == tk, "the diagonal key block of query tile i must be key block i"
    w = lane_blocks * LANES
    blk = lambda: pl.BlockSpec((1, tq, w), lambda bi, hi, qi: (bi, qi, hi))
    seq = lambda: pl.BlockSpec((1, L, w), lambda bi, hi, qi: (bi, 0, hi))
    return pl.pallas_call(
        functools.partial(_attn_kernel, tq=tq, tk=tk),
        grid=(b, a // w, L // tq),
        in_specs=[blk(), seq(), seq(),
                  pl.BlockSpec((1, ct.shape[1], tk), lambda bi, hi, qi: (bi, 0, 0))],
        out_specs=blk(),
        out_shape=jax.ShapeDtypeStruct((b, L, a), BF16),
        compiler_params=_params(("arbitrary", "arbitrary", "arbitrary")),
        name="fox_attention",
    )(q, k, v, ct)


def _pool_kernel(u_ref, w_ref, s_ref, o_ref):
    x = u_ref[0].astype(F32)
    L, w = x.shape
    row = lax.broadcasted_iota(jnp.int32, (L, w), 0)
    group = lax.broadcasted_iota(jnp.int32, (L, w), 1) // POOL_GROUP_DIM

    def shifted(y, s):
        return jnp.where(row >= s, pltpu.roll(y, s, 0), 0.0)

    acc = x
    sel = jnp.zeros_like(x)
    win_lane = jnp.zeros((L, w), F32)
    span = 1
    for gi, win in enumerate(POOL_WINDOWS):
        while span < win:
            acc = acc + shifted(acc, span)
            span *= 2
        sel = jnp.where(group == gi, acc, sel)
        win_lane = jnp.where(group == gi, float(win), win_lane)
    count = jnp.minimum(row.astype(F32) + 1.0, win_lane)
    d = sel / count - x
    y = jnp.dot(d.astype(BF16), w_ref[...], preferred_element_type=F32) * s_ref[...]
    o_ref[0] = y.astype(o_ref.dtype)


def _pool(u, w_bd, scale):
    b, L, w = u.shape
    return pl.pallas_call(
        _pool_kernel,
        grid=(b,),
        in_specs=[pl.BlockSpec((1, L, w), lambda i: (i, 0, 0)),
                  pl.BlockSpec((w, w), lambda i: (0, 0)),
                  pl.BlockSpec((1, w), lambda i: (0, 0))],
        out_specs=pl.BlockSpec((1, L, w), lambda i: (i, 0, 0)),
        out_shape=jax.ShapeDtypeStruct((b, L, w), BF16),
        compiler_params=_params(("arbitrary",)),
        name="pool_mixer",
    )(u, w_bd, scale)


def _group_of(idx, width):
    groups = LANES // SSM_GROUP_DIM
    return (idx >> (width.bit_length() - 1)) & (groups - 1)


def _expand_blockdiag(r_ref, row_inner, col_inner):
    rows, w = r_ref.shape[1:]
    groups = LANES // SSM_GROUP_DIM
    r = lax.broadcasted_iota(jnp.int32, (w, groups * w), 0)
    c = lax.broadcasted_iota(jnp.int32, (w, groups * w), 1)
    shift = col_inner.bit_length() - 1
    outer_c = c >> (shift + groups.bit_length() - 1)
    spread = (outer_c == (r >> shift)) & ((c & (col_inner - 1)) == (r & (col_inner - 1)))
    big = jnp.dot(r_ref[0], spread.astype(BF16), preferred_element_type=F32)
    ri = lax.broadcasted_iota(jnp.int32, (rows, groups * w), 0)
    ci = lax.broadcasted_iota(jnp.int32, (rows, groups * w), 1)
    return jnp.where(_group_of(ri, row_inner) == _group_of(ci, col_inner), big, 0.0).astype(BF16)


def _ssm_kernel(u_ref, rm_ref, rz_ref, ry_ref, coef_ref, y_ref, m_s, wz_s, wy_s, ucat, z_ref, s_ref, *, seqs):
    t = SSM_CHUNK

    @pl.when(pl.program_id(1) == 0)
    def _():
        m_s[...] = _expand_blockdiag(rm_ref, SSM_GROUP_DIM, SSM_GROUP_DIM)
        wz_s[...] = _expand_blockdiag(rz_ref, SSM_GROUP_DIM, SSM_STATE)
        wy_s[...] = _expand_blockdiag(ry_ref, SSM_STATE, SSM_GROUP_DIM)

    nc = u_ref.shape[0] // t
    cps = nc // seqs
    for k in range(t):
        ucat[:, k * LANES:(k + 1) * LANES] = u_ref[pl.ds(k, nc, stride=t), :].astype(BF16)
    u = ucat[...]
    z_ref[...] = jnp.dot(u, wz_s[...], preferred_element_type=F32)
    half = z_ref.shape[1] // 2
    a1 = coef_ref[0, 0:1, :]
    a2 = coef_ref[0, 1:2, :]

    def step(c, states):
        new = []
        for b in range(seqs):
            st = states[b]
            r = b * cps + c
            s_ref[pl.ds(r, 1), :] = st
            sw = jnp.concatenate([st[:, half:], st[:, :half]], axis=1)
            new.append(a1 * st + a2 * sw + z_ref[pl.ds(r, 1), :])
        return tuple(new)

    init = tuple(jnp.zeros((1, 2 * half), F32) for _ in range(seqs))
    lax.fori_loop(0, cps, step, init, unroll=8)
    y = jnp.dot(u, m_s[...], preferred_element_type=F32)
    y = y + jnp.dot(s_ref[...].astype(BF16), wy_s[...], preferred_element_type=F32)
    for k in range(t):
        y_ref[pl.ds(k, nc, stride=t), :] = y[:, k * LANES:(k + 1) * LANES]


def _ssm_core(u, rm, rz, ry, coef, *, seq, seqs_per_block=2):
    n, w = u.shape
    rows = seq * seqs_per_block
    nc = rows // SSM_CHUNK
    kw = SSM_CHUNK * LANES
    p2 = coef.shape[2]
    blk = pl.BlockSpec((rows, LANES), lambda hf, r: (r, hf))
    wspec = lambda arr: pl.BlockSpec((1,) + arr.shape[1:], lambda hf, r: (hf, 0, 0))
    return pl.pallas_call(
        functools.partial(_ssm_kernel, seqs=seqs_per_block),
        grid=(w // LANES, n // rows),
        in_specs=[blk, wspec(rm), wspec(rz), wspec(ry), wspec(coef)],
        out_specs=blk,
        out_shape=jax.ShapeDtypeStruct((n, w), F32),
        scratch_shapes=[pltpu.VMEM((kw, kw), BF16), pltpu.VMEM((kw, p2), BF16), pltpu.VMEM((p2, kw), BF16),
                        pltpu.VMEM((nc, kw), BF16), pltpu.VMEM((nc, p2), F32), pltpu.VMEM((nc, p2), F32)],
        compiler_params=_params(("arbitrary", "arbitrary")),
        name="s5_core",
    )(u, rm, rz, ry, coef)


def _ssm_matrices(a_re, a_im, b_re, b_im, c_re, c_im, d, log_dt):
    t = SSM_CHUNK
    g, p = a_re.shape
    hdim = d.shape[1]
    gl = LANES // hdim
    nb = g // gl
    lam = lax.complex(a_re, a_im)
    lam_dt = lam * jnp.exp(log_dt)[:, None]
    lam_bar = jnp.exp(lam_dt)
    b_bar = ((lam_bar - 1.0) / lam)[:, :, None] * lax.complex(b_re, b_im)
    cmat = lax.complex(c_re, c_im)
    steps = jnp.arange(t + 1, dtype=F32)
    pw = jnp.exp(lam_dt[:, None, :] * steps[None, :, None])
    kern = jnp.einsum('gop,gdp,gpi->gdoi', cmat, pw[:, :t], b_bar).real
    kern = kern.at[:, 0].add(jax.vmap(jnp.diag)(d))
    j = jnp.arange(t)[:, None]
    i = jnp.arange(t)[None, :]
    m = jnp.where((i >= j)[None, :, :, None, None], kern[:, jnp.clip(i - j, 0, t - 1)], 0.0)
    wz = jnp.einsum('gjp,gpi->gjip', pw[:, t - 1 - jnp.arange(t)], b_bar)
    cw = jnp.einsum('gop,gip->giop', cmat, pw[:, 1:t + 1])
    rm = m.reshape(nb, gl, t, t, hdim, hdim).transpose(0, 2, 1, 5, 3, 4).reshape(nb, t * LANES, t * hdim)
    rz = jnp.stack([wz.real, wz.imag], axis=3).reshape(nb, gl, t, hdim, 2, p)
    rz = rz.transpose(0, 2, 1, 3, 4, 5).reshape(nb, t * LANES, 2 * p)
    ry = jnp.stack([cw.real, -cw.imag], axis=1).reshape(nb, gl, 2, t, hdim, p)
    ry = ry.transpose(0, 2, 1, 5, 3, 4).reshape(nb, 2 * gl * p, t * hdim)
    ar = pw[:, t].real.reshape(nb, gl * p)
    ai = pw[:, t].imag.reshape(nb, gl * p)
    coef = jnp.stack([jnp.concatenate([ar, ar], -1), jnp.concatenate([-ai, ai], -1)], axis=1)
    return rm.astype(BF16), rz.astype(BF16), ry.astype(BF16), coef.astype(F32)


def _mixout_kernel(*refs, moe):
    (x_ref, attn_ref, pool_ref, ys_ref, wglu_ref, bglu_ref, ga_ref, gp_ref, gs_ref,
     wo_ref, gpost_ref, gpre_ref) = refs[:12]
    if moe:
        router_ref, xo_ref, h_ref, ri_ref, rg_ref, cnt_ref, cnt_acc = refs[12:]

        @pl.when(pl.program_id(0) == 0)
        def _():
            cnt_acc[...] = jnp.zeros_like(cnt_acc)
    else:
        xo_ref, h_ref = refs[12:]
    a = attn_ref.shape[1]
    pw = pool_ref.shape[1]
    attn_n = _rms(attn_ref[...].astype(F32), ga_ref[...]).astype(BF16)
    pool_n = _rms(pool_ref[...].astype(F32), gp_ref[...]).astype(BF16)
    y = jax.nn.gelu(ys_ref[...])
    gate = jnp.dot(y.astype(BF16), wglu_ref[...], preferred_element_type=F32) + bglu_ref[...]
    ssm = y * jax.nn.sigmoid(gate)
    ssm_n = _rms(ssm, gs_ref[...]).astype(BF16)
    mix = (jnp.dot(attn_n, wo_ref[0:a, :], preferred_element_type=F32)
           + jnp.dot(pool_n, wo_ref[a:a + pw, :], preferred_element_type=F32)
           + jnp.dot(ssm_n, wo_ref[a + pw:, :], preferred_element_type=F32))
    x = x_ref[...] + _rms(mix, gpost_ref[...])
    xo_ref[...] = x
    h = _rms(x, gpre_ref[...])
    if moe:
        _store_slabs(h_ref, 0, h)
    else:
        h_ref[...] = h.astype(h_ref.dtype)
    if moe:
        hi, mid, _ = _split3(h)
        rhi, rmid, _ = _split3(router_ref[...])
        dot_t = lambda p, q: lax.dot_general(p, q, (((1,), (1,)), ((), ())), preferred_element_type=F32)
        logits = dot_t(rhi, hi) + dot_t(rmid, hi) + dot_t(rhi, mid)
        ne, tm = logits.shape
        eidx = lax.broadcasted_iota(jnp.int32, (ne, tm), 0)
        m1 = jnp.max(logits, axis=0, keepdims=True)
        i1 = jnp.min(jnp.where(logits == m1, eidx, ne), axis=0, keepdims=True)
        rest = jnp.where(eidx == i1, -jnp.inf, logits)
        m2 = jnp.max(rest, axis=0, keepdims=True)
        i2 = jnp.min(jnp.where(rest == m2, eidx, ne), axis=0, keepdims=True)
        e2 = jnp.exp(m2 - m1)
        g1 = 1.0 / (1.0 + e2)
        rg_ref[...] = jnp.where(eidx == 0, g1, jnp.where(eidx == 1, e2 * g1, 0.0))
        onehot = ((eidx == i1) | (eidx == i2)).astype(BF16)
        earlier = (lax.broadcasted_iota(jnp.int32, (tm, tm), 0)
                   < lax.broadcasted_iota(jnp.int32, (tm, tm), 1)).astype(BF16)
        before = jnp.dot(onehot, earlier, preferred_element_type=F32) + cnt_acc[:, 0:1]
        r1 = jnp.sum(jnp.where(eidx == i1, before, 0.0), axis=0, keepdims=True).astype(jnp.int32)
        r2 = jnp.sum(jnp.where(eidx == i2, before, 0.0), axis=0, keepdims=True).astype(jnp.int32)
        ri_ref[...] = jnp.where(eidx == 0, i1, jnp.where(eidx == 1, i2,
                                jnp.where(eidx == 2, r1, jnp.where(eidx == 3, r2, 0))))
        cnt_acc[...] += jnp.sum(onehot.astype(F32), axis=1, keepdims=True)
        cnt_ref[...] = cnt_acc[...]


def _mixout(x2, attn, pool, ys, w_glu, b_glu, g_attn, g_pool, g_ssm, w_out, g_post, g_pre,
            router=None, *, tm=512):
    n, d = x2.shape
    moe = router is not None
    row = lambda w: pl.BlockSpec((tm, w), lambda i: (i, 0))
    full = lambda arr: pl.BlockSpec(arr.shape, lambda i: (0, 0))
    ins = [x2, attn, pool, ys, w_glu, b_glu, g_attn, g_pool, g_ssm, w_out, g_post, g_pre]
    in_specs = [row(d), row(attn.shape[1]), row(pool.shape[1]), row(ys.shape[1])] + [full(t) for t in ins[4:]]
    if moe:
        assert d == SUBLANES * LANES
        h_shape = jax.ShapeDtypeStruct((n * SUBLANES, LANES), F32)
        h_spec = pl.BlockSpec((tm * SUBLANES, LANES), lambda i: (i, 0))
    else:
        h_shape, h_spec = jax.ShapeDtypeStruct((n, d), BF16), row(d)
    out_shape = [jax.ShapeDtypeStruct((n, d), F32), h_shape]
    out_specs = [row(d), h_spec]
    scratch = []
    if moe:
        ins.append(router)
        in_specs.append(full(router))
        ne = router.shape[0]
        col = pl.BlockSpec((ne, tm), lambda i: (0, i))
        out_shape += [jax.ShapeDtypeStruct((ne, n), jnp.int32), jax.ShapeDtypeStruct((ne, n), F32),
                      jax.ShapeDtypeStruct((ne, LANES), F32)]
        out_specs += [col, col, pl.BlockSpec((ne, LANES), lambda i: (0, 0))]
        scratch = [pltpu.VMEM((ne, LANES), F32)]
    return pl.pallas_call(
        functools.partial(_mixout_kernel, moe=moe),
        grid=(n // tm,),
        in_specs=in_specs,
        out_specs=tuple(out_specs),
        out_shape=tuple(out_shape),
        scratch_shapes=scratch,
        compiler_params=_params(("arbitrary",)),
        name="mix_out",
    )(*ins)


FFN_STAGE_CHUNKS = 4


def _stage_cast(src_hbm, dst, lane0, stage, sem):
    rows = stage.shape[1]
    width = src_hbm.shape[1]

    def copy(c):
        return pltpu.make_async_copy(src_hbm.at[pl.ds(c * rows, rows)], stage.at[c % 2], sem.at[c % 2])

    nchunks = src_hbm.shape[0] // rows
    copy(0).start()
    for c in range(nchunks):
        if c + 1 < nchunks:
            copy(c + 1).start()
        copy(c).wait()
        dst[pl.ds(c * rows, rows), lane0:lane0 + width] = stage[c % 2].astype(BF16)


def _ffn_kernel(h_ref, x_ref, wg_hbm, wu_hbm, wd_hbm, g_ref, o_ref, wgu_s, wd_s, stage_in, stage_out, sem, *, sub):
    fdim = wd_s.shape[0]

    @pl.when(pl.program_id(0) == 0)
    def _():
        _stage_cast(wg_hbm, wgu_s, 0, stage_in, sem)
        _stage_cast(wu_hbm, wgu_s, fdim, stage_in, sem)
        _stage_cast(wd_hbm, wd_s, 0, stage_out, sem)

    for r in range(h_ref.shape[0] // sub):
        rows = pl.ds(r * sub, sub)
        hgu = jnp.dot(h_ref[rows, :], wgu_s[...], preferred_element_type=F32)
        hg, hu = hgu[:, :fdim], hgu[:, fdim:]
        act = (hg * jax.nn.sigmoid(hg) * hu).astype(BF16)
        y = jnp.dot(act, wd_s[...], preferred_element_type=F32)
        o_ref[rows, :] = x_ref[rows, :] + _rms(y, g_ref[...])


def _ffn(h, x2, wg, wu, wd, g_post, *, tm=512, sub=256):
    n, d = x2.shape
    fdim = wg.shape[1]
    row = pl.BlockSpec((tm, d), lambda i: (i, 0))
    hbm = pl.BlockSpec(memory_space=pl.ANY)
    return pl.pallas_call(
        functools.partial(_ffn_kernel, sub=sub),
        grid=(n // tm,),
        in_specs=[row, row, hbm, hbm, hbm, pl.BlockSpec((1, d), lambda i: (0, 0))],
        out_specs=row,
        out_shape=jax.ShapeDtypeStruct((n, d), F32),
        scratch_shapes=[pltpu.VMEM((d, 2 * fdim), BF16), pltpu.VMEM((fdim, d), BF16),
                        pltpu.VMEM((2, d // FFN_STAGE_CHUNKS, fdim), F32),
                        pltpu.VMEM((2, fdim // FFN_STAGE_CHUNKS, d), F32),
                        pltpu.SemaphoreType.DMA((2,))],
        compiler_params=_params(("arbitrary",)),
        name="dense_ffn",
    )(h, x2, wg, wu, wd, g_post)


MOE_TILE = 512


def _moe_tables(counts, n_items):
    tm = MOE_TILE
    i32 = jnp.int32
    ne = counts.shape[0]
    assert n_items <= LANES
    counts = counts.astype(i32)
    ntiles = (counts + tm - 1) // tm
    k = jnp.arange(ne, dtype=i32)
    ends = jnp.sum(jnp.where(k[None, :] <= k[:, None], ntiles[None, :], 0), axis=1)
    starts = ends - ntiles
    present = ntiles > 0
    later = present[None, :] & (k[None, :] > k[:, None])
    nxt = jnp.min(jnp.where(later, k[None, :], ne), axis=1)
    nxt = jnp.where(nxt < ne, nxt, -1)
    order = jnp.sum((present[None, :] & (k[None, :] < k[:, None])).astype(i32), axis=1)
    w = jnp.arange(LANES, dtype=i32)[None, :]
    wc = jnp.minimum(w, ends[-1] - 1)
    e = jnp.sum((wc >= ends[:, None]).astype(i32), axis=0, keepdims=True)
    of_item = lambda v: jnp.sum(jnp.where(k[:, None] == e, v[:, None], 0), axis=0, keepdims=True)
    valid = w < ends[-1]
    rowblock = jnp.where(valid, w, n_items)
    nvalid = jnp.where(valid, jnp.clip(of_item(counts) - (wc - of_item(starts)) * tm, 0, tm), 0)
    e_prev = jnp.concatenate([e[:, :1], e[:, :-1]], axis=1)
    first = ((w == 0) | (e != e_prev)).astype(i32)
    item = lambda v: v[0, :n_items].astype(i32)
    return ((item(e), item(rowblock), item(nvalid), item(first)),
            (item(of_item(nxt)), item(of_item(order)), jnp.sum(present.astype(i32)).reshape(1)), starts * tm)


def _slab(ref, token):
    return ref.at[pl.ds(pl.multiple_of(token * SUBLANES, SUBLANES), SUBLANES)]


def _dispatch_kernel(dest_ref, h_ref, xs_ref, sem, *, n):
    tm = h_ref.shape[0] // SUBLANES
    base = pl.program_id(0) * tm

    def body(r, carry):
        for k in range(2):
            d = dest_ref[k * n + base + r]
            pltpu.make_async_copy(_slab(h_ref, r), _slab(xs_ref, d), sem).start(priority=k)
        return carry

    lax.fori_loop(0, tm, body, 0, unroll=8)
    for k in range(2):
        pltpu.make_async_copy(h_ref, h_ref, sem).wait()


def _dispatch(dest, h, cap_rows, *, tm=512):
    n = h.shape[0] // SUBLANES
    return pl.pallas_call(
        functools.partial(_dispatch_kernel, n=n),
        grid_spec=pltpu.PrefetchScalarGridSpec(
            num_scalar_prefetch=1,
            grid=(n // tm,),
            in_specs=[pl.BlockSpec((tm * SUBLANES, LANES), lambda i, dest: (i, 0))],
            out_specs=pl.BlockSpec(memory_space=pl.ANY),
            scratch_shapes=[pltpu.SemaphoreType.DMA(())]),
        out_shape=jax.ShapeDtypeStruct((cap_rows * SUBLANES, LANES), F32),
        compiler_params=_params(("arbitrary",)),
        name="moe_dispatch",
    )(dest, h)


MOE_SUB = 256


MOE_UP_CHUNKS = 4


def _moe_up_kernel(ie_ref, rb_ref, nv_ref, first_ref, nxt_ref, ord_ref, npres_ref, xs_ref, wg_hbm, wu_hbm, h_ref,
                   wgu_s, stage, sem):
    f = pl.program_id(0)
    w = pl.program_id(1)
    nf = pl.num_programs(0)
    tf = h_ref.shape[1]
    rows = stage.shape[0] // MOE_UP_CHUNKS

    def copies(e, fp):
        return [pltpu.make_async_copy(src.at[e, pl.ds(c * rows, rows), pl.ds(fp * tf, tf)],
                                      stage.at[pl.ds(c * rows, rows), pl.ds(m * tf, tf)],
                                      sem.at[m * MOE_UP_CHUNKS + c])
                for m, src in enumerate((wg_hbm, wu_hbm)) for c in range(MOE_UP_CHUNKS)]

    @pl.when((f == 0) & (w == 0))
    def _():
        for cp in copies(ie_ref[0], 0):
            cp.start()

    slot = (f * npres_ref[0] + ord_ref[w]) % 2
    is_first = first_ref[w] == 1
    last_expert = nxt_ref[w] < 0
    for fp in range(nf):
        @pl.when(is_first & (f == fp))
        def _():
            for cp in copies(ie_ref[w], fp):
                cp.wait()
            for c in range(MOE_UP_CHUNKS):
                r = pl.ds(c * rows, rows)
                wgu_s[slot, r, :] = stage[r, :].astype(BF16)

            @pl.when(jnp.logical_not(last_expert))
            def _():
                for cp in copies(nxt_ref[w], fp):
                    cp.start()

            if fp + 1 < nf:
                @pl.when(last_expert)
                def _():
                    for cp in copies(ie_ref[0], fp + 1):
                        cp.start()

    nv = nv_ref[w]

    @pl.when(nv > 0)
    def _():
        wgu_view = wgu_s.at[slot]
        for r in range(h_ref.shape[0] // MOE_SUB):
            rws = pl.ds(r * MOE_SUB, MOE_SUB)
            idx = r * MOE_SUB + lax.broadcasted_iota(jnp.int32, (MOE_SUB, 1), 0)
            x = _load_slabs(xs_ref, r * MOE_SUB, MOE_SUB)
            x = jnp.where(idx < nv, x, 0.0).astype(BF16)
            hgu = jnp.dot(x, wgu_view[...], preferred_element_type=F32)
            hg, hu = hgu[:, :tf], hgu[:, tf:]
            h_ref[rws, :] = (hg * jax.nn.sigmoid(hg) * hu).astype(BF16)

    @pl.when(nv == 0)
    def _():
        h_ref[...] = jnp.zeros_like(h_ref)


def _moe_up(tables, order, xs, wg, wu, *, tf=1792):
    tm = MOE_TILE
    cap_rows = xs.shape[0] // SUBLANES
    ne, d, fdim = wg.shape
    n_items = tables[0].shape[0]
    hbm = pl.BlockSpec(memory_space=pl.ANY)
    return pl.pallas_call(
        _moe_up_kernel,
        grid_spec=pltpu.PrefetchScalarGridSpec(
            num_scalar_prefetch=7,
            grid=(fdim // tf, n_items),
            in_specs=[pl.BlockSpec((tm * SUBLANES, LANES), lambda f, w, ie, rb, *_: (rb[w], 0)), hbm, hbm],
            out_specs=pl.BlockSpec((tm, tf), lambda f, w, ie, rb, *_: (rb[w], f)),
            scratch_shapes=[pltpu.VMEM((2, d, 2 * tf), BF16), pltpu.VMEM((d, 2 * tf), F32),
                            pltpu.SemaphoreType.DMA((2 * MOE_UP_CHUNKS,))]),
        out_shape=jax.ShapeDtypeStruct((cap_rows, fdim), BF16),
        compiler_params=_params(("arbitrary", "arbitrary")),
        name="moe_up",
    )(*tables, *order, xs, wg, wu)


def _moe_down_kernel(ie_ref, rb_ref, nv_ref, first_ref, nxt_ref, ord_ref, npres_ref, h_ref, wd_hbm, y_ref,
                     wd_s, stage, sem):
    w = pl.program_id(0)
    slot = ord_ref[w] % 2
    nchunks = sem.shape[0]
    chunk = stage.shape[0] // nchunks

    def copies(e):
        return [pltpu.make_async_copy(wd_hbm.at[e, pl.ds(c * chunk, chunk)], stage.at[pl.ds(c * chunk, chunk)],
                                      sem.at[c]) for c in range(nchunks)]

    @pl.when(w == 0)
    def _():
        for cp in copies(ie_ref[0]):
            cp.start()

    @pl.when(first_ref[w] == 1)
    def _():
        for c, cp in enumerate(copies(ie_ref[w])):
            cp.wait()
            wd_s[slot, pl.ds(c * chunk, chunk), :] = stage[pl.ds(c * chunk, chunk), :].astype(BF16)

        @pl.when(nxt_ref[w] >= 0)
        def _():
            for cp in copies(nxt_ref[w]):
                cp.start()

    @pl.when(nv_ref[w] > 0)
    def _():
        wd = wd_s[slot]
        for r in range(h_ref.shape[0] // MOE_SUB):
            y = jnp.dot(h_ref[pl.ds(r * MOE_SUB, MOE_SUB), :], wd, preferred_element_type=F32)
            _store_slabs(y_ref, r * MOE_SUB, y)

    @pl.when(nv_ref[w] == 0)
    def _():
        y_ref[...] = jnp.zeros_like(y_ref)


def _moe_down(tables, order, hid, wd, *, chunk=512):
    tm = MOE_TILE
    cap_rows, fdim = hid.shape
    d = wd.shape[2]
    n_items = tables[0].shape[0]
    assert d == SUBLANES * LANES and fdim % chunk == 0
    return pl.pallas_call(
        _moe_down_kernel,
        grid_spec=pltpu.PrefetchScalarGridSpec(
            num_scalar_prefetch=7,
            grid=(n_items,),
            in_specs=[pl.BlockSpec((tm, fdim), lambda w, ie, rb, *_: (rb[w], 0)),
                      pl.BlockSpec(memory_space=pl.ANY)],
            out_specs=pl.BlockSpec((tm * SUBLANES, LANES), lambda w, ie, rb, *_: (rb[w], 0)),
            scratch_shapes=[pltpu.VMEM((2, fdim, d), BF16), pltpu.VMEM((fdim, d), F32),
                            pltpu.SemaphoreType.DMA((fdim // chunk,))]),
        out_shape=jax.ShapeDtypeStruct((cap_rows * SUBLANES, LANES), F32),
        compiler_params=_params(("arbitrary",)),
        name="moe_down",
    )(*tables, *order, hid, wd)


def _combine_kernel(dest_ref, x_ref, gate_ref, g_ref, ys_ref, o_ref, buf, sem, *, n):
    tm = x_ref.shape[0]
    i = pl.program_id(0)

    def issue(tile, slot):
        def body(r, carry):
            for k in range(2):
                d = dest_ref[k * n + tile * tm + r]
                pltpu.make_async_copy(_slab(ys_ref, d), _slab(buf.at[slot, k], r),
                                      sem.at[slot]).start(priority=k)
            return carry
        lax.fori_loop(0, tm, body, 0, unroll=8)

    @pl.when(i == 0)
    def _():
        issue(0, 0)

    @pl.when(i + 1 < pl.num_programs(0))
    def _():
        issue(i + 1, (i + 1) % 2)

    slot = i % 2
    for k in range(2):
        pltpu.make_async_copy(buf.at[slot, k], buf.at[slot, k], sem.at[slot]).wait()
    gates = gate_ref[...].T
    f = (gates[:, 0:1] * _load_slabs(buf.at[slot, 0], 0, tm)
         + gates[:, 1:2] * _load_slabs(buf.at[slot, 1], 0, tm))
    o_ref[...] = x_ref[...] + _rms(f, g_ref[...])


def _combine(dest, x2, gates, g_post, ys, *, tm=256):
    n, d = x2.shape
    return pl.pallas_call(
        functools.partial(_combine_kernel, n=n),
        grid_spec=pltpu.PrefetchScalarGridSpec(
            num_scalar_prefetch=1,
            grid=(n // tm,),
            in_specs=[pl.BlockSpec((tm, d), lambda i, dest: (i, 0)),
                      pl.BlockSpec((gates.shape[0], tm), lambda i, dest: (0, i)),
                      pl.BlockSpec((1, d), lambda i, dest: (0, 0)),
                      pl.BlockSpec(memory_space=pl.ANY)],
            out_specs=pl.BlockSpec((tm, d), lambda i, dest: (i, 0)),
            scratch_shapes=[pltpu.VMEM((2, 2, tm * SUBLANES, LANES), F32), pltpu.SemaphoreType.DMA((2,))]),
        out_shape=jax.ShapeDtypeStruct((n, d), F32),
        compiler_params=_params(("arbitrary",)),
        name="moe_combine",
    )(dest, x2, gates, g_post, ys)


def _moe(h, x2, route_i, route_g, counts, wg, wu, wd, g_post):
    n = x2.shape[0]
    n_items = 2 * n // MOE_TILE + N_EXPERTS
    tables, order, base = _moe_tables(counts[:, 0], n_items)
    base_of = lambda e: jnp.sum(jnp.where(e[None, :] == jnp.arange(N_EXPERTS)[:, None], base[:, None], 0), axis=0)
    dest = jnp.concatenate([base_of(route_i[0]) + route_i[2], base_of(route_i[1]) + route_i[3]])
    xs = _dispatch(dest, h, (n_items + 1) * MOE_TILE)
    hid = _moe_up(tables, order, xs, wg, wu)
    ys = _moe_down(tables, order, hid, wd)
    return _combine(dest, x2, route_g, g_post, ys)


def _row(v):
    return v.reshape(1, -1).astype(F32)


def kernel(x, norm_mix_pre, norm_mix_post, norm_ffn_pre, norm_ffn_post, w_in, b_forget, pool_w, pool_scale, ssm_a_re, ssm_a_im, ssm_b_re, ssm_b_im, ssm_c_re, ssm_c_im, ssm_d, ssm_log_dt, ssm_w_glu, ssm_b_glu, branch_norm_attn, branch_norm_pool, branch_norm_ssm, w_out, ffn_w_gate, ffn_w_up, ffn_w_down, moe_router, moe_w_gate, moe_w_up, moe_w_down):
    b, L, d = x.shape
    depth = w_in.shape[0]
    n = b * L
    a = ATTN_WIDTH
    f_off = 3 * a
    p_off = f_off + ATTN_HEADS
    attn_tile = 512
    x2 = x.reshape(n, d)
    for i in range(depth):
        w = w_in[i]
        w_main = jnp.concatenate([w[:, :f_off], w[:, p_off:]], axis=1).astype(BF16)
        w_f = jnp.pad(w[:, f_off:p_off], ((0, 0), (0, LANES - ATTN_HEADS))).astype(BF16)
        b_f = jnp.pad(b_forget[i], (0, LANES - ATTN_HEADS)).reshape(1, LANES)
        q, k, v, up, us, c = _inproj(x2, _row(norm_mix_pre[i]), w_main, w_f, b_f, seq=L)

        ct = c.reshape(b, L, ATTN_HEADS).transpose(0, 2, 1).reshape(b, ATTN_HEADS * (L // attn_tile), attn_tile)
        attn = _attention(q.reshape(b, L, a), k.reshape(b, L, a), v.reshape(b, L, a),
                          ct, tq=attn_tile).reshape(n, a)

        pool = _pool(up.reshape(b, L, POOL_WIDTH),
                     jax.scipy.linalg.block_diag(*pool_w[i]).astype(BF16),
                     _row(pool_scale[i])).reshape(n, POOL_WIDTH)

        mats = _ssm_matrices(ssm_a_re[i], ssm_a_im[i], ssm_b_re[i], ssm_b_im[i],
                             ssm_c_re[i], ssm_c_im[i], ssm_d[i], ssm_log_dt[i])
        ys = _ssm_core(us, *mats, seq=L)

        moe = i % 2 == 1
        j = i // 2
        router = moe_router[j].T if moe else None
        res = _mixout(x2, attn, pool, ys, ssm_w_glu[i].astype(BF16), _row(ssm_b_glu[i]),
                      _row(branch_norm_attn[i]), _row(branch_norm_pool[i]), _row(branch_norm_ssm[i]),
                      w_out[i].astype(BF16), _row(norm_mix_post[i]), _row(norm_ffn_pre[i]), router)
        if moe:
            x2, h, route_i, route_g, counts = res
            x2 = _moe(h, x2, route_i, route_g, counts, moe_w_gate[j], moe_w_up[j], moe_w_down[j],
                      _row(norm_ffn_post[i]))
        else:
            x2, h = res
            x2 = _ffn(h, x2, ffn_w_gate[j], ffn_w_up[j], ffn_w_down[j], _row(norm_ffn_post[i]))
    return x2.reshape(b, L, d)
```

```python
import functools

import jax
import jax.numpy as jnp
from jax import lax
from jax.experimental import pallas as pl
from jax.experimental.pallas import tpu as pltpu

F32 = jnp.float32
BF16 = jnp.bfloat16

RMS_EPS = 1e-6
NEG_INF = -1e30
LOG2E = 1.4426950408889634

ATTN_HEADS = 8
ATTN_HEAD_DIM = 64
ATTN_WIDTH = ATTN_HEADS * ATTN_HEAD_DIM
POOL_WINDOWS = (2, 4, 8, 16)
POOL_GROUP_DIM = 64
POOL_WIDTH = len(POOL_WINDOWS) * POOL_GROUP_DIM
SSM_GROUPS = 16
SSM_GROUP_DIM = 16
SSM_STATE = 64
SSM_WIDTH = SSM_GROUPS * SSM_GROUP_DIM
SSM_CHUNK = 8
N_EXPERTS = 8

LANES = 128
SUBLANES = 8
VMEM_LIMIT = 56 * 1024 * 1024


def _params(sem):
    return pltpu.CompilerParams(dimension_semantics=sem, vmem_limit_bytes=VMEM_LIMIT)


def _rms(x, g):
    return x * lax.rsqrt(jnp.mean(x * x, axis=-1, keepdims=True) + RMS_EPS) * g


def _load_slabs(ref, row0, rows):
    return jnp.concatenate([ref[pl.ds(row0 * SUBLANES + s, rows, stride=SUBLANES), :]
                            for s in range(SUBLANES)], axis=1)


def _store_slabs(ref, row0, val):
    for s in range(SUBLANES):
        ref[pl.ds(row0 * SUBLANES + s, val.shape[0], stride=SUBLANES), :] = val[:, s * LANES:(s + 1) * LANES]


def _split3(x):
    hi = x.astype(BF16)
    r = x - hi.astype(F32)
    mid = r.astype(BF16)
    lo = (r - mid.astype(F32)).astype(BF16)
    return hi, mid, lo


def _inproj_kernel(x_ref, g_ref, wm_ref, wf_ref, bf_ref,
                   q_ref, k_ref, v_ref, up_ref, us_ref, c_ref, carry_ref, *, tiles_per_seq):
    @pl.when(pl.program_id(0) % tiles_per_seq == 0)
    def _():
        carry_ref[...] = jnp.zeros_like(carry_ref)

    tm = x_ref.shape[0]
    h = _rms(x_ref[...], g_ref[...]).astype(BF16)
    proj = jnp.dot(h, wm_ref[...], preferred_element_type=F32)
    a = ATTN_WIDTH
    q_ref[...] = (proj[:, 0:a] * (LOG2E * ATTN_HEAD_DIM ** -0.5)).astype(BF16)
    k_ref[...] = proj[:, a:2 * a].astype(BF16)
    v_ref[...] = proj[:, 2 * a:3 * a].astype(BF16)
    up_ref[...] = proj[:, 3 * a:3 * a + POOL_WIDTH].astype(BF16)
    us_ref[...] = proj[:, 3 * a + POOL_WIDTH:]

    z = jnp.dot(h, wf_ref[...], preferred_element_type=F32) + bf_ref[...]
    logf = jnp.minimum(z, 0.0) - jnp.log(1.0 + jnp.exp(-jnp.abs(z)))
    row = lax.broadcasted_iota(jnp.int32, (tm, tm), 0)
    col = lax.broadcasted_iota(jnp.int32, (tm, tm), 1)
    tri = (row >= col).astype(BF16)
    hi, mid, _ = _split3(logf)
    c = (jnp.dot(tri, hi, preferred_element_type=F32)
         + jnp.dot(tri, mid, preferred_element_type=F32)) + carry_ref[...]
    c_ref[...] = c[:, :ATTN_HEADS]
    carry_ref[...] = c[tm - 1:tm, :]


def _inproj(x2, g, w_main, w_f, b_f, *, seq, tm=512):
    n, d = x2.shape
    nm = w_main.shape[1]
    a = ATTN_WIDTH
    out_shape = (
        jax.ShapeDtypeStruct((n, a), BF16), jax.ShapeDtypeStruct((n, a), BF16),
        jax.ShapeDtypeStruct((n, a), BF16), jax.ShapeDtypeStruct((n, POOL_WIDTH), BF16),
        jax.ShapeDtypeStruct((n, SSM_WIDTH), F32), jax.ShapeDtypeStruct((n, ATTN_HEADS), F32))
    row = lambda w: pl.BlockSpec((tm, w), lambda i: (i, 0))
    full = lambda r, c: pl.BlockSpec((r, c), lambda i: (0, 0))
    return pl.pallas_call(
        functools.partial(_inproj_kernel, tiles_per_seq=seq // tm),
        grid=(n // tm,),
        in_specs=[row(d), full(1, d), full(d, nm), full(d, LANES), full(1, LANES)],
        out_specs=(row(a), row(a), row(a), row(POOL_WIDTH), row(SSM_WIDTH), row(ATTN_HEADS)),
        out_shape=out_shape,
        scratch_shapes=[pltpu.VMEM((1, LANES), F32)],
        compiler_params=_params(("arbitrary",)),
        name="inproj",
    )(x2, g, w_main, w_f, b_f)


HEADS_PER_BLOCK = LANES // ATTN_HEAD_DIM


def _attn_kernel(q_ref, k_ref, v_ref, ct_ref, o_ref, *, tq, tk):
    nlb = q_ref.shape[2] // LANES
    hp = pl.program_id(1)
    qi = pl.program_id(2)
    nkb = k_ref.shape[1] // tk
    lane = lax.broadcasted_iota(jnp.int32, (1, LANES), 1)
    in_head = [(lane // ATTN_HEAD_DIM) == hh for hh in range(HEADS_PER_BLOCK)]
    chains = [(lb, hh) for lb in range(nlb) for hh in range(HEADS_PER_BLOCK)]
    qm, crow, cref = [], [], []
    for lb in range(nlb):
        q2 = q_ref[0, :, lb * LANES:(lb + 1) * LANES]
        qm.append(jnp.concatenate([jnp.where(msk, q2, jnp.zeros_like(q2)) for msk in in_head], axis=0))
    for lb, hh in chains:
        crow.append(((hp * nlb + lb) * HEADS_PER_BLOCK + hh) * nkb)
        cref.append(ct_ref[0, pl.ds(crow[-1] + qi, 1), :][:, 0:1])

    def step(kb, carry, masked):
        new = []
        scores = {}
        for ci, (lb, hh) in enumerate(chains):
            kblk = k_ref[0, pl.ds(kb * tk, tk), lb * LANES:(lb + 1) * LANES]
            vblk = v_ref[0, pl.ds(kb * tk, tk), lb * LANES:(lb + 1) * LANES]
            m, acc = carry[ci]
            if lb not in scores:
                scores[lb] = lax.dot_general(qm[lb], kblk, (((1,), (1,)), ((), ())), preferred_element_type=F32)
            s = scores[lb][hh * tq:(hh + 1) * tq, :]
            s = s - (ct_ref[0, pl.ds(crow[ci] + kb, 1), :] - cref[ci]) * LOG2E
            if masked:
                r = lax.broadcasted_iota(jnp.int32, (tq, tk), 0)
                cidx = lax.broadcasted_iota(jnp.int32, (tq, tk), 1)
                s = jnp.where(cidx <= r, s, NEG_INF)
            m_new = jnp.maximum(m, jnp.max(s, axis=1, keepdims=True))
            p = jnp.exp2(s - m_new)
            vsel = jnp.where(in_head[hh], vblk, jnp.ones_like(vblk))
            acc = jnp.exp2(m - m_new) * acc + jnp.dot(p.astype(BF16), vsel, preferred_element_type=F32)
            new.append((m_new, acc))
        return tuple(new)

    init = tuple((jnp.full((tq, 1), NEG_INF, F32), jnp.zeros((tq, LANES), F32)) for _ in chains)
    carry = lax.fori_loop(0, qi, lambda kb, cr: step(kb, cr, False), init)
    carry = step(qi, carry, True)
    for lb in range(nlb):
        out = jnp.zeros((tq, LANES), F32)
        for hh in range(HEADS_PER_BLOCK):
            acc = carry[lb * HEADS_PER_BLOCK + hh][1]
            out = jnp.where(in_head[hh], acc / pltpu.roll(acc, ATTN_HEAD_DIM, 1), out)
        o_ref[0, :, lb * LANES:(lb + 1) * LANES] = out.astype(o_ref.dtype)


def _attention(q, k, v, ct, *, tq, lane_blocks=2):
    b, L, a = q.shape
    tk = ct.shape[2]
    assert tq == tk, "the diagonal key block of query tile i must be key block i"
    w = lane_blocks * LANES
    blk = lambda: pl.BlockSpec((1, tq, w), lambda bi, hi, qi: (bi, qi, hi))
    seq = lambda: pl.BlockSpec((1, L, w), lambda bi, hi, qi: (bi, 0, hi))
    return pl.pallas_call(
        functools.partial(_attn_kernel, tq=tq, tk=tk),
        grid=(b, a // w, L // tq),
        in_specs=[blk(), seq(), seq(),
                  pl.BlockSpec((1, ct.shape[1], tk), lambda bi, hi, qi: (bi, 0, 0))],
        out_specs=blk(),
        out_shape=jax.ShapeDtypeStruct((b, L, a), BF16),
        compiler_params=_params(("arbitrary", "arbitrary", "arbitrary")),
        name="fox_attention",
    )(q, k, v, ct)


def _pool_kernel(u_ref, w_ref, s_ref, o_ref):
    x = u_ref[0].astype(F32)
    L, w = x.shape
    row = lax.broadcasted_iota(jnp.int32, (L, w), 0)
    group = lax.broadcasted_iota(jnp.int32, (L, w), 1) // POOL_GROUP_DIM

    def shifted(y, s):
        return jnp.where(row >= s, pltpu.roll(y, s, 0), 0.0)

    acc = x
    sel = jnp.zeros_like(x)
    win_lane = jnp.zeros((L, w), F32)
    span = 1
    for gi, win in enumerate(POOL_WINDOWS):
        while span < win:
            acc = acc + shifted(acc, span)
            span *= 2
        sel = jnp.where(group == gi, acc, sel)
        win_lane = jnp.where(group == gi, float(win), win_lane)
    count = jnp.minimum(row.astype(F32) + 1.0, win_lane)
    d = sel / count - x
    y = jnp.dot(d.astype(BF16), w_ref[...], preferred_element_type=F32) * s_ref[...]
    o_ref[0] = y.astype(o_ref.dtype)


def _pool(u, w_bd, scale):
    b, L, w = u.shape
    return pl.pallas_call(
        _pool_kernel,
        grid=(b,),
        in_specs=[pl.BlockSpec((1, L, w), lambda i: (i, 0, 0)),
                  pl.BlockSpec((w, w), lambda i: (0, 0)),
                  pl.BlockSpec((1, w), lambda i: (0, 0))],
        out_specs=pl.BlockSpec((1, L, w), lambda i: (i, 0, 0)),
        out_shape=jax.ShapeDtypeStruct((b, L, w), BF16),
        compiler_params=_params(("arbitrary",)),
        name="pool_mixer",
    )(u, w_bd, scale)


def _group_of(idx, width):
    groups = LANES // SSM_GROUP_DIM
    return (idx >> (width.bit_length() - 1)) & (groups - 1)


def _expand_blockdiag(r_ref, row_inner, col_inner):
    rows, w = r_ref.shape[1:]
    groups = LANES // SSM_GROUP_DIM
    r = lax.broadcasted_iota(jnp.int32, (w, groups * w), 0)
    c = lax.broadcasted_iota(jnp.int32, (w, groups * w), 1)
    shift = col_inner.bit_length() - 1
    outer_c = c >> (shift + groups.bit_length() - 1)
    spread = (outer_c == (r >> shift)) & ((c & (col_inner - 1)) == (r & (col_inner - 1)))
    big = jnp.dot(r_ref[0], spread.astype(BF16), preferred_element_type=F32)
    ri = lax.broadcasted_iota(jnp.int32, (rows, groups * w), 0)
    ci = lax.broadcasted_iota(jnp.int32, (rows, groups * w), 1)
    return jnp.where(_group_of(ri, row_inner) == _group_of(ci, col_inner), big, 0.0).astype(BF16)


def _ssm_kernel(u_ref, rm_ref, rz_ref, ry_ref, coef_ref, y_ref, m_s, wz_s, wy_s, ucat, z_ref, s_ref, *, seqs):
    t = SSM_CHUNK

    @pl.when(pl.program_id(1) == 0)
    def _():
        m_s[...] = _expand_blockdiag(rm_ref, SSM_GROUP_DIM, SSM_GROUP_DIM)
        wz_s[...] = _expand_blockdiag(rz_ref, SSM_GROUP_DIM, SSM_STATE)
        wy_s[...] = _expand_blockdiag(ry_ref, SSM_STATE, SSM_GROUP_DIM)

    nc = u_ref.shape[0] // t
    cps = nc // seqs
    for k in range(t):
        ucat[:, k * LANES:(k + 1) * LANES] = u_ref[pl.ds(k, nc, stride=t), :].astype(BF16)
    u = ucat[...]
    z_ref[...] = jnp.dot(u, wz_s[...], preferred_element_type=F32)
    half = z_ref.shape[1] // 2
    a1 = coef_ref[0, 0:1, :]
    a2 = coef_ref[0, 1:2, :]

    def step(c, states):
        new = []
        for b in range(seqs):
            st = states[b]
            r = b * cps + c
            s_ref[pl.ds(r, 1), :] = st
            sw = jnp.concatenate([st[:, half:], st[:, :half]], axis=1)
            new.append(a1 * st + a2 * sw + z_ref[pl.ds(r, 1), :])
        return tuple(new)

    init = tuple(jnp.zeros((1, 2 * half), F32) for _ in range(seqs))
    lax.fori_loop(0, cps, step, init, unroll=8)
    y = jnp.dot(u, m_s[...], preferred_element_type=F32)
    y = y + jnp.dot(s_ref[...].astype(BF16), wy_s[...], preferred_element_type=F32)
    for k in range(t):
        y_ref[pl.ds(k, nc, stride=t), :] = y[:, k * LANES:(k + 1) * LANES]


def _ssm_core(u, rm, rz, ry, coef, *, seq, seqs_per_block=2):
    n, w = u.shape
    rows = seq * seqs_per_block
    nc = rows // SSM_CHUNK
    kw = SSM_CHUNK * LANES
    p2 = coef.shape[2]
    blk = pl.BlockSpec((rows, LANES), lambda hf, r: (r, hf))
    wspec = lambda arr: pl.BlockSpec((1,) + arr.shape[1:], lambda hf, r: (hf, 0, 0))
    return pl.pallas_call(
        functools.partial(_ssm_kernel, seqs=seqs_per_block),
        grid=(w // LANES, n // rows),
        in_specs=[blk, wspec(rm), wspec(rz), wspec(ry), wspec(coef)],
        out_specs=blk,
        out_shape=jax.ShapeDtypeStruct((n, w), F32),
        scratch_shapes=[pltpu.VMEM((kw, kw), BF16), pltpu.VMEM((kw, p2), BF16), pltpu.VMEM((p2, kw), BF16),
                        pltpu.VMEM((nc, kw), BF16), pltpu.VMEM((nc, p2), F32), pltpu.VMEM((nc, p2), F32)],
        compiler_params=_params(("arbitrary", "arbitrary")),
        name="s5_core",
    )(u, rm, rz, ry, coef)


def _ssm_matrices(a_re, a_im, b_re, b_im, c_re, c_im, d, log_dt):
    t = SSM_CHUNK
    g, p = a_re.shape
    hdim = d.shape[1]
    gl = LANES // hdim
    nb = g // gl
    lam = lax.complex(a_re, a_im)
    lam_dt = lam * jnp.exp(log_dt)[:, None]
    lam_bar = jnp.exp(lam_dt)
    b_bar = ((lam_bar - 1.0) / lam)[:, :, None] * lax.complex(b_re, b_im)
    cmat = lax.complex(c_re, c_im)
    steps = jnp.arange(t + 1, dtype=F32)
    pw = jnp.exp(lam_dt[:, None, :] * steps[None, :, None])
    kern = jnp.einsum('gop,gdp,gpi->gdoi', cmat, pw[:, :t], b_bar).real
    kern = kern.at[:, 0].add(jax.vmap(jnp.diag)(d))
    j = jnp.arange(t)[:, None]
    i = jnp.arange(t)[None, :]
    m = jnp.where((i >= j)[None, :, :, None, None], kern[:, jnp.clip(i - j, 0, t - 1)], 0.0)
    wz = jnp.einsum('gjp,gpi->gjip', pw[:, t - 1 - jnp.arange(t)], b_bar)
    cw = jnp.einsum('gop,gip->giop', cmat, pw[:, 1:t + 1])
    rm = m.reshape(nb, gl, t, t, hdim, hdim).transpose(0, 2, 1, 5, 3, 4).reshape(nb, t * LANES, t * hdim)
    rz = jnp.stack([wz.real, wz.imag], axis=3).reshape(nb, gl, t, hdim, 2, p)
    rz = rz.transpose(0, 2, 1, 3, 4, 5).reshape(nb, t * LANES, 2 * p)
    ry = jnp.stack([cw.real, -cw.imag], axis=1).reshape(nb, gl, 2, t, hdim, p)
    ry = ry.transpose(0, 2, 1, 5, 3, 4).reshape(nb, 2 * gl * p, t * hdim)
    ar = pw[:, t].real.reshape(nb, gl * p)
    ai = pw[:, t].imag.reshape(nb, gl * p)
    coef = jnp.stack([jnp.concatenate([ar, ar], -1), jnp.concatenate([-ai, ai], -1)], axis=1)
    return rm.astype(BF16), rz.astype(BF16), ry.astype(BF16), coef.astype(F32)


def _mixout_kernel(*refs, moe):
    (x_ref, attn_ref, pool_ref, ys_ref, wglu_ref, bglu_ref, ga_ref, gp_ref, gs_ref,
     wo_ref, gpost_ref, gpre_ref) = refs[:12]
    if moe:
        router_ref, xo_ref, h_ref, ri_ref, rg_ref, cnt_ref, cnt_acc = refs[12:]

        @pl.when(pl.program_id(0) == 0)
        def _():
            cnt_acc[...] = jnp.zeros_like(cnt_acc)
    else:
        xo_ref, h_ref = refs[12:]
    a = attn_ref.shape[1]
    pw = pool_ref.shape[1]
    attn_n = _rms(attn_ref[...].astype(F32), ga_ref[...]).astype(BF16)
    pool_n = _rms(pool_ref[...].astype(F32), gp_ref[...]).astype(BF16)
    y = jax.nn.gelu(ys_ref[...])
    gate = jnp.dot(y.astype(BF16), wglu_ref[...], preferred_element_type=F32) + bglu_ref[...]
    ssm = y * jax.nn.sigmoid(gate)
    ssm_n = _rms(ssm, gs_ref[...]).astype(BF16)
    mix = (jnp.dot(attn_n, wo_ref[0:a, :], preferred_element_type=F32)
           + jnp.dot(pool_n, wo_ref[a:a + pw, :], preferred_element_type=F32)
           + jnp.dot(ssm_n, wo_ref[a + pw:, :], preferred_element_type=F32))
    x = x_ref[...] + _rms(mix, gpost_ref[...])
    xo_ref[...] = x
    h = _rms(x, gpre_ref[...])
    if moe:
        _store_slabs(h_ref, 0, h)
    else:
        h_ref[...] = h.astype(h_ref.dtype)
    if moe:
        hi, mid, _ = _split3(h)
        rhi, rmid, _ = _split3(router_ref[...])
        dot_t = lambda p, q: lax.dot_general(p, q, (((1,), (1,)), ((), ())), preferred_element_type=F32)
        logits = dot_t(rhi, hi) + dot_t(rmid, hi) + dot_t(rhi, mid)
        ne, tm = logits.shape
        eidx = lax.broadcasted_iota(jnp.int32, (ne, tm), 0)
        m1 = jnp.max(logits, axis=0, keepdims=True)
        i1 = jnp.min(jnp.where(logits == m1, eidx, ne), axis=0, keepdims=True)
        rest = jnp.where(eidx == i1, -jnp.inf, logits)
        m2 = jnp.max(rest, axis=0, keepdims=True)
        i2 = jnp.min(jnp.where(rest == m2, eidx, ne), axis=0, keepdims=True)
        e2 = jnp.exp(m2 - m1)
        g1 = 1.0 / (1.0 + e2)
        rg_ref[...] = jnp.where(eidx == 0, g1, jnp.where(eidx == 1, e2 * g1, 0.0))
        onehot = ((eidx == i1) | (eidx == i2)).astype(BF16)
        earlier = (lax.broadcasted_iota(jnp.int32, (tm, tm), 0)
                   < lax.broadcasted_iota(jnp.int32, (tm, tm), 1)).astype(BF16)
        before = jnp.dot(onehot, earlier, preferred_element_type=F32) + cnt_acc[:, 0:1]
        r1 = jnp.sum(jnp.where(eidx == i1, before, 0.0), axis=0, keepdims=True).astype(jnp.int32)
        r2 = jnp.sum(jnp.where(eidx == i2, before, 0.0), axis=0, keepdims=True).astype(jnp.int32)
        ri_ref[...] = jnp.where(eidx == 0, i1, jnp.where(eidx == 1, i2,
                                jnp.where(eidx == 2, r1, jnp.where(eidx == 3, r2, 0))))
        cnt_acc[...] += jnp.sum(onehot.astype(F32), axis=1, keepdims=True)
        cnt_ref[...] = cnt_acc[...]


def _mixout(x2, attn, pool, ys, w_glu, b_glu, g_attn, g_pool, g_ssm, w_out, g_post, g_pre,
            router=None, *, tm=512):
    n, d = x2.shape
    moe = router is not None
    row = lambda w: pl.BlockSpec((tm, w), lambda i: (i, 0))
    full = lambda arr: pl.BlockSpec(arr.shape, lambda i: (0, 0))
    ins = [x2, attn, pool, ys, w_glu, b_glu, g_attn, g_pool, g_ssm, w_out, g_post, g_pre]
    in_specs = [row(d), row(attn.shape[1]), row(pool.shape[1]), row(ys.shape[1])] + [full(t) for t in ins[4:]]
    if moe:
        assert d == SUBLANES * LANES
        h_shape = jax.ShapeDtypeStruct((n * SUBLANES, LANES), F32)
        h_spec = pl.BlockSpec((tm * SUBLANES, LANES), lambda i: (i, 0))
    else:
        h_shape, h_spec = jax.ShapeDtypeStruct((n, d), BF16), row(d)
    out_shape = [jax.ShapeDtypeStruct((n, d), F32), h_shape]
    out_specs = [row(d), h_spec]
    scratch = []
    if moe:
        ins.append(router)
        in_specs.append(full(router))
        ne = router.shape[0]
        col = pl.BlockSpec((ne, tm), lambda i: (0, i))
        out_shape += [jax.ShapeDtypeStruct((ne, n), jnp.int32), jax.ShapeDtypeStruct((ne, n), F32),
                      jax.ShapeDtypeStruct((ne, LANES), F32)]
        out_specs += [col, col, pl.BlockSpec((ne, LANES), lambda i: (0, 0))]
        scratch = [pltpu.VMEM((ne, LANES), F32)]
    return pl.pallas_call(
        functools.partial(_mixout_kernel, moe=moe),
        grid=(n // tm,),
        in_specs=in_specs,
        out_specs=tuple(out_specs),
        out_shape=tuple(out_shape),
        scratch_shapes=scratch,
        compiler_params=_params(("arbitrary",)),
        name="mix_out",
    )(*ins)


FFN_STAGE_CHUNKS = 4


def _stage_cast(src_hbm, dst, lane0, stage, sem):
    rows = stage.shape[1]
    width = src_hbm.shape[1]

    def copy(c):
        return pltpu.make_async_copy(src_hbm.at[pl.ds(c * rows, rows)], stage.at[c % 2], sem.at[c % 2])

    nchunks = src_hbm.shape[0] // rows
    copy(0).start()
    for c in range(nchunks):
        if c + 1 < nchunks:
            copy(c + 1).start()
        copy(c).wait()
        dst[pl.ds(c * rows, rows), lane0:lane0 + width] = stage[c % 2].astype(BF16)


def _ffn_kernel(h_ref, x_ref, wg_hbm, wu_hbm, wd_hbm, g_ref, o_ref, wgu_s, wd_s, stage_in, stage_out, sem, *, sub):
    fdim = wd_s.shape[0]

    @pl.when(pl.program_id(0) == 0)
    def _():
        _stage_cast(wg_hbm, wgu_s, 0, stage_in, sem)
        _stage_cast(wu_hbm, wgu_s, fdim, stage_in, sem)
        _stage_cast(wd_hbm, wd_s, 0, stage_out, sem)

    for r in range(h_ref.shape[0] // sub):
        rows = pl.ds(r * sub, sub)
        hgu = jnp.dot(h_ref[rows, :], wgu_s[...], preferred_element_type=F32)
        hg, hu = hgu[:, :fdim], hgu[:, fdim:]
        act = (hg * jax.nn.sigmoid(hg) * hu).astype(BF16)
        y = jnp.dot(act, wd_s[...], preferred_element_type=F32)
        o_ref[rows, :] = x_ref[rows, :] + _rms(y, g_ref[...])


def _ffn(h, x2, wg, wu, wd, g_post, *, tm=512, sub=256):
    n, d = x2.shape
    fdim = wg.shape[1]
    row = pl.BlockSpec((tm, d), lambda i: (i, 0))
    hbm = pl.BlockSpec(memory_space=pl.ANY)
    return pl.pallas_call(
        functools.partial(_ffn_kernel, sub=sub),
        grid=(n // tm,),
        in_specs=[row, row, hbm, hbm, hbm, pl.BlockSpec((1, d), lambda i: (0, 0))],
        out_specs=row,
        out_shape=jax.ShapeDtypeStruct((n, d), F32),
        scratch_shapes=[pltpu.VMEM((d, 2 * fdim), BF16), pltpu.VMEM((fdim, d), BF16),
                        pltpu.VMEM((2, d // FFN_STAGE_CHUNKS, fdim), F32),
                        pltpu.VMEM((2, fdim // FFN_STAGE_CHUNKS, d), F32),
                        pltpu.SemaphoreType.DMA((2,))],
        compiler_params=_params(("arbitrary",)),
        name="dense_ffn",
    )(h, x2, wg, wu, wd, g_post)


MOE_TILE = 512


def _moe_tables(counts, n_items):
    tm = MOE_TILE
    i32 = jnp.int32
    ne = counts.shape[0]
    assert n_items <= LANES
    counts = counts.astype(i32)
    ntiles = (counts + tm - 1) // tm
    k = jnp.arange(ne, dtype=i32)
    ends = jnp.sum(jnp.where(k[None, :] <= k[:, None], ntiles[None, :], 0), axis=1)
    starts = ends - ntiles
    present = ntiles > 0
    later = present[None, :] & (k[None, :] > k[:, None])
    nxt = jnp.min(jnp.where(later, k[None, :], ne), axis=1)
    nxt = jnp.where(nxt < ne, nxt, -1)
    order = jnp.sum((present[None, :] & (k[None, :] < k[:, None])).astype(i32), axis=1)
    w = jnp.arange(LANES, dtype=i32)[None, :]
    wc = jnp.minimum(w, ends[-1] - 1)
    e = jnp.sum((wc >= ends[:, None]).astype(i32), axis=0, keepdims=True)
    of_item = lambda v: jnp.sum(jnp.where(k[:, None] == e, v[:, None], 0), axis=0, keepdims=True)
    valid = w < ends[-1]
    rowblock = jnp.where(valid, w, n_items)
    nvalid = jnp.where(valid, jnp.clip(of_item(counts) - (wc - of_item(starts)) * tm, 0, tm), 0)
    e_prev = jnp.concatenate([e[:, :1], e[:, :-1]], axis=1)
    first = ((w == 0) | (e != e_prev)).astype(i32)
    item = lambda v: v[0, :n_items].astype(i32)
    return ((item(e), item(rowblock), item(nvalid), item(first)),
            (item(of_item(nxt)), item(of_item(order)), jnp.sum(present.astype(i32)).reshape(1)), starts * tm)


def _slab(ref, token):
    return ref.at[pl.ds(pl.multiple_of(token * SUBLANES, SUBLANES), SUBLANES)]


def _dispatch_kernel(dest_ref, h_ref, xs_ref, sem, *, n):
    tm = h_ref.shape[0] // SUBLANES
    base = pl.program_id(0) * tm

    def body(r, carry):
        for k in range(2):
            d = dest_ref[k * n + base + r]
            pltpu.make_async_copy(_slab(h_ref, r), _slab(xs_ref, d), sem).start(priority=k)
        return carry

    lax.fori_loop(0, tm, body, 0, unroll=8)
    for k in range(2):
        pltpu.make_async_copy(h_ref, h_ref, sem).wait()


def _dispatch(dest, h, cap_rows, *, tm=512):
    n = h.shape[0] // SUBLANES
    return pl.pallas_call(
        functools.partial(_dispatch_kernel, n=n),
        grid_spec=pltpu.PrefetchScalarGridSpec(
            num_scalar_prefetch=1,
            grid=(n // tm,),
            in_specs=[pl.BlockSpec((tm * SUBLANES, LANES), lambda i, dest: (i, 0))],
            out_specs=pl.BlockSpec(memory_space=pl.ANY),
            scratch_shapes=[pltpu.SemaphoreType.DMA(())]),
        out_shape=jax.ShapeDtypeStruct((cap_rows * SUBLANES, LANES), F32),
        compiler_params=_params(("arbitrary",)),
        name="moe_dispatch",
    )(dest, h)


MOE_SUB = 256


MOE_UP_CHUNKS = 4


def _moe_up_kernel(ie_ref, rb_ref, nv_ref, first_ref, nxt_ref, ord_ref, npres_ref, xs_ref, wg_hbm, wu_hbm, h_ref,
                   wgu_s, stage, sem):
    f = pl.program_id(0)
    w = pl.program_id(1)
    nf = pl.num_programs(0)
    tf = h_ref.shape[1]
    rows = stage.shape[0] // MOE_UP_CHUNKS

    def copies(e, fp):
        return [pltpu.make_async_copy(src.at[e, pl.ds(c * rows, rows), pl.ds(fp * tf, tf)],
                                      stage.at[pl.ds(c * rows, rows), pl.ds(m * tf, tf)],
                                      sem.at[m * MOE_UP_CHUNKS + c])
                for m, src in enumerate((wg_hbm, wu_hbm)) for c in range(MOE_UP_CHUNKS)]

    @pl.when((f == 0) & (w == 0))
    def _():
        for cp in copies(ie_ref[0], 0):
            cp.start()

    slot = (f * npres_ref[0] + ord_ref[w]) % 2
    is_first = first_ref[w] == 1
    last_expert = nxt_ref[w] < 0
    for fp in range(nf):
        @pl.when(is_first & (f == fp))
        def _():
            for cp in copies(ie_ref[w], fp):
                cp.wait()
            for c in range(MOE_UP_CHUNKS):
                r = pl.ds(c * rows, rows)
                wgu_s[slot, r, :] = stage[r, :].astype(BF16)

            @pl.when(jnp.logical_not(last_expert))
            def _():
                for cp in copies(nxt_ref[w], fp):
                    cp.start()

            if fp + 1 < nf:
                @pl.when(last_expert)
                def _():
                    for cp in copies(ie_ref[0], fp + 1):
                        cp.start()

    nv = nv_ref[w]

    @pl.when(nv > 0)
    def _():
        wgu_view = wgu_s.at[slot]
        for r in range(h_ref.shape[0] // MOE_SUB):
            rws = pl.ds(r * MOE_SUB, MOE_SUB)
            idx = r * MOE_SUB + lax.broadcasted_iota(jnp.int32, (MOE_SUB, 1), 0)
            x = _load_slabs(xs_ref, r * MOE_SUB, MOE_SUB)
            x = jnp.where(idx < nv, x, 0.0).astype(BF16)
            hgu = jnp.dot(x, wgu_view[...], preferred_element_type=F32)
            hg, hu = hgu[:, :tf], hgu[:, tf:]
            h_ref[rws, :] = (hg * jax.nn.sigmoid(hg) * hu).astype(BF16)

    @pl.when(nv == 0)
    def _():
        h_ref[...] = jnp.zeros_like(h_ref)


def _moe_up(tables, order, xs, wg, wu, *, tf=1792):
    tm = MOE_TILE
    cap_rows = xs.shape[0] // SUBLANES
    ne, d, fdim = wg.shape
    n_items = tables[0].shape[0]
    hbm = pl.BlockSpec(memory_space=pl.ANY)
    return pl.pallas_call(
        _moe_up_kernel,
        grid_spec=pltpu.PrefetchScalarGridSpec(
            num_scalar_prefetch=7,
            grid=(fdim // tf, n_items),
            in_specs=[pl.BlockSpec((tm * SUBLANES, LANES), lambda f, w, ie, rb, *_: (rb[w], 0)), hbm, hbm],
            out_specs=pl.BlockSpec((tm, tf), lambda f, w, ie, rb, *_: (rb[w], f)),
            scratch_shapes=[pltpu.VMEM((2, d, 2 * tf), BF16), pltpu.VMEM((d, 2 * tf), F32),
                            pltpu.SemaphoreType.DMA((2 * MOE_UP_CHUNKS,))]),
        out_shape=jax.ShapeDtypeStruct((cap_rows, fdim), BF16),
        compiler_params=_params(("arbitrary", "arbitrary")),
        name="moe_up",
    )(*tables, *order, xs, wg, wu)


def _moe_down_kernel(ie_ref, rb_ref, nv_ref, first_ref, nxt_ref, ord_ref, npres_ref, h_ref, wd_hbm, y_ref,
                     wd_s, stage, sem):
    w = pl.program_id(0)
    slot = ord_ref[w] % 2
    nchunks = sem.shape[0]
    chunk = stage.shape[0] // nchunks

    def copies(e):
        return [pltpu.make_async_copy(wd_hbm.at[e, pl.ds(c * chunk, chunk)], stage.at[pl.ds(c * chunk, chunk)],
                                      sem.at[c]) for c in range(nchunks)]

    @pl.when(w == 0)
    def _():
        for cp in copies(ie_ref[0]):
            cp.start()

    @pl.when(first_ref[w] == 1)
    def _():
        for c, cp in enumerate(copies(ie_ref[w])):
            cp.wait()
            wd_s[slot, pl.ds(c * chunk, chunk), :] = stage[pl.ds(c * chunk, chunk), :].astype(BF16)

        @pl.when(nxt_ref[w] >= 0)
        def _():
            for cp in copies(nxt_ref[w]):
                cp.start()

    @pl.when(nv_ref[w] > 0)
    def _():
        wd = wd_s[slot]
        for r in range(h_ref.shape[0] // MOE_SUB):
            y = jnp.dot(h_ref[pl.ds(r * MOE_SUB, MOE_SUB), :], wd, preferred_element_type=F32)
            _store_slabs(y_ref, r * MOE_SUB, y)

    @pl.when(nv_ref[w] == 0)
    def _():
        y_ref[...] = jnp.zeros_like(y_ref)


def _moe_down(tables, order, hid, wd, *, chunk=512):
    tm = MOE_TILE
    cap_rows, fdim = hid.shape
    d = wd.shape[2]
    n_items = tables[0].shape[0]
    assert d == SUBLANES * LANES and fdim % chunk == 0
    return pl.pallas_call(
        _moe_down_kernel,
        grid_spec=pltpu.PrefetchScalarGridSpec(
            num_scalar_prefetch=7,
            grid=(n_items,),
            in_specs=[pl.BlockSpec((tm, fdim), lambda w, ie, rb, *_: (rb[w], 0)),
                      pl.BlockSpec(memory_space=pl.ANY)],
            out_specs=pl.BlockSpec((tm * SUBLANES, LANES), lambda w, ie, rb, *_: (rb[w], 0)),
            scratch_shapes=[pltpu.VMEM((2, fdim, d), BF16), pltpu.VMEM((fdim, d), F32),
                            pltpu.SemaphoreType.DMA((fdim // chunk,))]),
        out_shape=jax.ShapeDtypeStruct((cap_rows * SUBLANES, LANES), F32),
        compiler_params=_params(("arbitrary",)),
        name="moe_down",
    )(*tables, *order, hid, wd)


def _combine_kernel(dest_ref, x_ref, gate_ref, g_ref, ys_ref, o_ref, buf, sem, *, n):
    tm = x_ref.shape[0]
    i = pl.program_id(0)

    def issue(tile, slot):
        def body(r, carry):
            for k in range(2):
                d = dest_ref[k * n + tile * tm + r]
                pltpu.make_async_copy(_slab(ys_ref, d), _slab(buf.at[slot, k], r),
                                      sem.at[slot]).start(priority=k)
            return carry
        lax.fori_loop(0, tm, body, 0, unroll=8)

    @pl.when(i == 0)
    def _():
        issue(0, 0)

    @pl.when(i + 1 < pl.num_programs(0))
    def _():
        issue(i + 1, (i + 1) % 2)

    slot = i % 2
    for k in range(2):
        pltpu.make_async_copy(buf.at[slot, k], buf.at[slot, k], sem.at[slot]).wait()
    gates = gate_ref[...].T
    f = (gates[:, 0:1] * _load_slabs(buf.at[slot, 0], 0, tm)
         + gates[:, 1:2] * _load_slabs(buf.at[slot, 1], 0, tm))
    o_ref[...] = x_ref[...] + _rms(f, g_ref[...])


def _combine(dest, x2, gates, g_post, ys, *, tm=256):
    n, d = x2.shape
    return pl.pallas_call(
        functools.partial(_combine_kernel, n=n),
        grid_spec=pltpu.PrefetchScalarGridSpec(
            num_scalar_prefetch=1,
            grid=(n // tm,),
            in_specs=[pl.BlockSpec((tm, d), lambda i, dest: (i, 0)),
                      pl.BlockSpec((gates.shape[0], tm), lambda i, dest: (0, i)),
                      pl.BlockSpec((1, d), lambda i, dest: (0, 0)),
                      pl.BlockSpec(memory_space=pl.ANY)],
            out_specs=pl.BlockSpec((tm, d), lambda i, dest: (i, 0)),
            scratch_shapes=[pltpu.VMEM((2, 2, tm * SUBLANES, LANES), F32), pltpu.SemaphoreType.DMA((2,))]),
        out_shape=jax.ShapeDtypeStruct((n, d), F32),
        compiler_params=_params(("arbitrary",)),
        name="moe_combine",
    )(dest, x2, gates, g_post, ys)


def _moe(h, x2, route_i, route_g, counts, wg, wu, wd, g_post):
    n = x2.shape[0]
    n_items = 2 * n // MOE_TILE + N_EXPERTS
    tables, order, base = _moe_tables(counts[:, 0], n_items)
    base_of = lambda e: jnp.sum(jnp.where(e[None, :] == jnp.arange(N_EXPERTS)[:, None], base[:, None], 0), axis=0)
    dest = jnp.concatenate([base_of(route_i[0]) + route_i[2], base_of(route_i[1]) + route_i[3]])
    xs = _dispatch(dest, h, (n_items + 1) * MOE_TILE)
    hid = _moe_up(tables, order, xs, wg, wu)
    ys = _moe_down(tables, order, hid, wd)
    return _combine(dest, x2, route_g, g_post, ys)


def _row(v):
    return v.reshape(1, -1).astype(F32)


def kernel(x, norm_mix_pre, norm_mix_post, norm_ffn_pre, norm_ffn_post, w_in, b_forget, pool_w, pool_scale, ssm_a_re, ssm_a_im, ssm_b_re, ssm_b_im, ssm_c_re, ssm_c_im, ssm_d, ssm_log_dt, ssm_w_glu, ssm_b_glu, branch_norm_attn, branch_norm_pool, branch_norm_ssm, w_out, ffn_w_gate, ffn_w_up, ffn_w_down, moe_router, moe_w_gate, moe_w_up, moe_w_down):
    b, L, d = x.shape
    depth = w_in.shape[0]
    n = b * L
    a = ATTN_WIDTH
    f_off = 3 * a
    p_off = f_off + ATTN_HEADS
    attn_tile = 512
    x2 = x.reshape(n, d)
    for i in range(depth):
        w = w_in[i]
        w_main = jnp.concatenate([w[:, :f_off], w[:, p_off:]], axis=1).astype(BF16)
        w_f = jnp.pad(w[:, f_off:p_off], ((0, 0), (0, LANES - ATTN_HEADS))).astype(BF16)
        b_f = jnp.pad(b_forget[i], (0, LANES - ATTN_HEADS)).reshape(1, LANES)
        q, k, v, up, us, c = _inproj(x2, _row(norm_mix_pre[i]), w_main, w_f, b_f, seq=L)

        ct = c.reshape(b, L, ATTN_HEADS).transpose(0, 2, 1).reshape(b, ATTN_HEADS * (L // attn_tile), attn_tile)
        attn = _attention(q.reshape(b, L, a), k.reshape(b, L, a), v.reshape(b, L, a),
                          ct, tq=attn_tile).reshape(n, a)

        pool = _pool(up.reshape(b, L, POOL_WIDTH),
                     jax.scipy.linalg.block_diag(*pool_w[i]).astype(BF16),
                     _row(pool_scale[i])).reshape(n, POOL_WIDTH)

        mats = _ssm_matrices(ssm_a_re[i], ssm_a_im[i], ssm_b_re[i], ssm_b_im[i],
                             ssm_c_re[i], ssm_c_im[i], ssm_d[i], ssm_log_dt[i])
        ys = _ssm_core(us, *mats, seq=L)

        moe = i % 2 == 1
        j = i // 2
        router = moe_router[j].T if moe else None
        res = _mixout(x2, attn, pool, ys, ssm_w_glu[i].astype(BF16), _row(ssm_b_glu[i]),
                      _row(branch_norm_attn[i]), _row(branch_norm_pool[i]), _row(branch_norm_ssm[i]),
                      w_out[i].astype(BF16), _row(norm_mix_post[i]), _row(norm_ffn_pre[i]), router)
        if moe:
            x2, h, route_i, route_g, counts = res
            x2 = _moe(h, x2, route_i, route_g, counts, moe_w_gate[j], moe_w_up[j], moe_w_down[j],
                      _row(norm_ffn_post[i]))
        else:
            x2, h = res
            x2 = _ffn(h, x2, ffn_w_gate[j], ffn_w_up[j], ffn_w_down[j], _row(norm_ffn_post[i]))
    return x2.reshape(b, L, d)
```

```python
import functools

import jax
import jax.numpy as jnp
from jax import lax
from jax.experimental import pallas as pl
from jax.experimental.pallas import tpu as pltpu

F32 = jnp.float32
BF16 = jnp.bfloat16

RMS_EPS = 1e-6
NEG_INF = -1e30
LOG2E = 1.4426950408889634

ATTN_HEADS = 8
ATTN_HEAD_DIM = 64
ATTN_WIDTH = ATTN_HEADS * ATTN_HEAD_DIM
POOL_WINDOWS = (2, 4, 8, 16)
POOL_GROUP_DIM = 64
POOL_WIDTH = len(POOL_WINDOWS) * POOL_GROUP_DIM
SSM_GROUPS = 16
SSM_GROUP_DIM = 16
SSM_STATE = 64
SSM_WIDTH = SSM_GROUPS * SSM_GROUP_DIM
SSM_CHUNK = 8
N_EXPERTS = 8

LANES = 128
SUBLANES = 8
VMEM_LIMIT = 56 * 1024 * 1024


def _params(sem):
    return pltpu.CompilerParams(dimension_semantics=sem, vmem_limit_bytes=VMEM_LIMIT)


def _rms(x, g):
    return x * lax.rsqrt(jnp.mean(x * x, axis=-1, keepdims=True) + RMS_EPS) * g


def _load_slabs(ref, row0, rows):
    return jnp.concatenate([ref[pl.ds(row0 * SUBLANES + s, rows, stride=SUBLANES), :]
                            for s in range(SUBLANES)], axis=1)


def _store_slabs(ref, row0, val):
    for s in range(SUBLANES):
        ref[pl.ds(row0 * SUBLANES + s, val.shape[0], stride=SUBLANES), :] = val[:, s * LANES:(s + 1) * LANES]


def _split3(x):
    hi = x.astype(BF16)
    r = x - hi.astype(F32)
    mid = r.astype(BF16)
    lo = (r - mid.astype(F32)).astype(BF16)
    return hi, mid, lo


INPROJ_STAGE_ROWS = 256
INPROJ_SUB = 256


def _inproj_kernel(x_ref, g_ref, w_hbm, bf_ref, q_ref, k_ref, v_ref, up_ref, us_ref, c_ref,
                   w_s, stage, sem, carry_ref, *, layer, tiles_per_seq):
    a = ATTN_WIDTH
    f_off = 3 * a
    p_off = f_off + ATTN_HEADS
    main = f_off + POOL_WIDTH + SSM_WIDTH

    @pl.when(pl.program_id(0) == 0)
    def _():
        rows = stage.shape[1]

        def copy(c):
            return pltpu.make_async_copy(w_hbm.at[layer, pl.ds(c * rows, rows)], stage.at[c % 2], sem.at[c % 2])

        nchunks = w_s.shape[0] // rows
        copy(0).start()
        for c in range(nchunks):
            if c + 1 < nchunks:
                copy(c + 1).start()
            copy(c).wait()
            blk = stage[c % 2]
            r = pl.ds(c * rows, rows)
            w_s[r, 0:f_off] = blk[:, 0:f_off].astype(BF16)
            w_s[r, f_off:main] = blk[:, p_off:p_off + main - f_off].astype(BF16)
            lane = lax.broadcasted_iota(jnp.int32, (rows, LANES), 1)
            w_s[r, main:] = jnp.where(lane < ATTN_HEADS, blk[:, f_off:f_off + LANES], 0.0).astype(BF16)

    @pl.when(pl.program_id(0) % tiles_per_seq == 0)
    def _():
        carry_ref[...] = jnp.zeros_like(carry_ref)

    tm = x_ref.shape[0]
    zs = []
    for s in range(tm // INPROJ_SUB):
        r = pl.ds(s * INPROJ_SUB, INPROJ_SUB)
        h = _rms(x_ref[r, :], g_ref[...]).astype(BF16)
        proj = jnp.dot(h, w_s[...], preferred_element_type=F32)
        q_ref[r, :] = (proj[:, 0:a] * (LOG2E * ATTN_HEAD_DIM ** -0.5)).astype(BF16)
        k_ref[r, :] = proj[:, a:2 * a].astype(BF16)
        v_ref[r, :] = proj[:, 2 * a:3 * a].astype(BF16)
        up_ref[r, :] = proj[:, 3 * a:3 * a + POOL_WIDTH].astype(BF16)
        us_ref[r, :] = proj[:, 3 * a + POOL_WIDTH:main]
        zs.append(proj[:, main:])

    z = jnp.concatenate(zs, axis=0) + bf_ref[...]
    logf = jnp.minimum(z, 0.0) - jnp.log(1.0 + jnp.exp(-jnp.abs(z)))
    row = lax.broadcasted_iota(jnp.int32, (tm, tm), 0)
    col = lax.broadcasted_iota(jnp.int32, (tm, tm), 1)
    tri = (row >= col).astype(BF16)
    hi, mid, _ = _split3(logf)
    c = (jnp.dot(tri, hi, preferred_element_type=F32)
         + jnp.dot(tri, mid, preferred_element_type=F32)) + carry_ref[...]
    c_ref[...] = c[:, :ATTN_HEADS]
    carry_ref[...] = c[tm - 1:tm, :]


def _inproj(x2, g, w_in, b_f, *, layer, seq, tm=512):
    n, d = x2.shape
    n_in = w_in.shape[2]
    a = ATTN_WIDTH
    n_res = n_in - ATTN_HEADS + LANES
    out_shape = (
        jax.ShapeDtypeStruct((n, a), BF16), jax.ShapeDtypeStruct((n, a), BF16),
        jax.ShapeDtypeStruct((n, a), BF16), jax.ShapeDtypeStruct((n, POOL_WIDTH), BF16),
        jax.ShapeDtypeStruct((n, SSM_WIDTH), F32), jax.ShapeDtypeStruct((n, ATTN_HEADS), F32))
    row = lambda w: pl.BlockSpec((tm, w), lambda i: (i, 0))
    full = lambda r, c: pl.BlockSpec((r, c), lambda i: (0, 0))
    return pl.pallas_call(
        functools.partial(_inproj_kernel, layer=layer, tiles_per_seq=seq // tm),
        grid=(n // tm,),
        in_specs=[row(d), full(1, d), pl.BlockSpec(memory_space=pl.ANY), full(1, LANES)],
        out_specs=(row(a), row(a), row(a), row(POOL_WIDTH), row(SSM_WIDTH), row(ATTN_HEADS)),
        out_shape=out_shape,
        scratch_shapes=[pltpu.VMEM((d, n_res), BF16), pltpu.VMEM((2, INPROJ_STAGE_ROWS, n_in), F32),
                        pltpu.SemaphoreType.DMA((2,)), pltpu.VMEM((1, LANES), F32)],
        compiler_params=_params(("arbitrary",)),
        name="inproj",
    )(x2, g, w_in, b_f)


HEADS_PER_BLOCK = LANES // ATTN_HEAD_DIM


def _attn_kernel(q_ref, k_ref, v_ref, ct_ref, o_ref, *, tq, tk):
    nlb = q_ref.shape[2] // LANES
    hp = pl.program_id(1)
    qi = pl.program_id(2)
    nkb = k_ref.shape[1] // tk
    lane = lax.broadcasted_iota(jnp.int32, (1, LANES), 1)
    in_head = [(lane // ATTN_HEAD_DIM) == hh for hh in range(HEADS_PER_BLOCK)]
    chains = [(lb, hh) for lb in range(nlb) for hh in range(HEADS_PER_BLOCK)]
    qm, crow, cref = [], [], []
    for lb in range(nlb):
        q2 = q_ref[0, :, lb * LANES:(lb + 1) * LANES]
        qm.append(jnp.concatenate([jnp.where(msk, q2, jnp.zeros_like(q2)) for msk in in_head], axis=0))
    for lb, hh in chains:
        crow.append(((hp * nlb + lb) * HEADS_PER_BLOCK + hh) * nkb)
        cref.append(ct_ref[0, pl.ds(crow[-1] + qi, 1), :][:, 0:1])

    def step(kb, carry, masked):
        new = []
        scores = {}
        for ci, (lb, hh) in enumerate(chains):
            kblk = k_ref[0, pl.ds(kb * tk, tk), lb * LANES:(lb + 1) * LANES]
            vblk = v_ref[0, pl.ds(kb * tk, tk), lb * LANES:(lb + 1) * LANES]
            m, acc = carry[ci]
            if lb not in scores:
                scores[lb] = lax.dot_general(qm[lb], kblk, (((1,), (1,)), ((), ())), preferred_element_type=F32)
            s = scores[lb][hh * tq:(hh + 1) * tq, :]
            s = s - (ct_ref[0, pl.ds(crow[ci] + kb, 1), :] - cref[ci]) * LOG2E
            if masked:
                r = lax.broadcasted_iota(jnp.int32, (tq, tk), 0)
                cidx = lax.broadcasted_iota(jnp.int32, (tq, tk), 1)
                s = jnp.where(cidx <= r, s, NEG_INF)
            m_new = jnp.maximum(m, jnp.max(s, axis=1, keepdims=True))
            p = jnp.exp2(s - m_new)
            vsel = jnp.where(in_head[hh], vblk, jnp.ones_like(vblk))
            acc = jnp.exp2(m - m_new) * acc + jnp.dot(p.astype(BF16), vsel, preferred_element_type=F32)
            new.append((m_new, acc))
        return tuple(new)

    init = tuple((jnp.full((tq, 1), NEG_INF, F32), jnp.zeros((tq, LANES), F32)) for _ in chains)
    carry = lax.fori_loop(0, qi, lambda kb, cr: step(kb, cr, False), init)
    carry = step(qi, carry, True)
    for lb in range(nlb):
        out = jnp.zeros((tq, LANES), F32)
        for hh in range(HEADS_PER_BLOCK):
            acc = carry[lb * HEADS_PER_BLOCK + hh][1]
            out = jnp.where(in_head[hh], acc / pltpu.roll(acc, ATTN_HEAD_DIM, 1), out)
        o_ref[0, :, lb * LANES:(lb + 1) * LANES] = out.astype(o_ref.dtype)


def _attention(q, k, v, ct, *, tq, lane_blocks=4):
    b, L, a = q.shape
    tk = ct.shape[2]
    assert tq == tk, "the diagonal key block of query tile i must be key block i"
    w = lane_blocks * LANES
    blk = lambda: pl.BlockSpec((1, tq, w), lambda bi, hi, qi: (bi, qi, hi))
    seq = lambda: pl.BlockSpec((1, L, w), lambda bi, hi, qi: (bi, 0, hi))
    return pl.pallas_call(
        functools.partial(_attn_kernel, tq=tq, tk=tk),
        grid=(b, a // w, L // tq),
        in_specs=[blk(), seq(), seq(),
                  pl.BlockSpec((1, ct.shape[1], tk), lambda bi, hi, qi: (bi, 0, 0))],
        out_specs=blk(),
        out_shape=jax.ShapeDtypeStruct((b, L, a), BF16),
        compiler_params=_params(("arbitrary", "arbitrary", "arbitrary")),
        name="fox_attention",
    )(q, k, v, ct)


def _pool_kernel(u_ref, w_ref, s_ref, o_ref):
    x = u_ref[0].astype(F32)
    L, w = x.shape
    row = lax.broadcasted_iota(jnp.int32, (L, w), 0)
    group = lax.broadcasted_iota(jnp.int32, (L, w), 1) // POOL_GROUP_DIM

    def shifted(y, s):
        return jnp.where(row >= s, pltpu.roll(y, s, 0), 0.0)

    acc = x
    sel = jnp.zeros_like(x)
    win_lane = jnp.zeros((L, w), F32)
    span = 1
    for gi, win in enumerate(POOL_WINDOWS):
        while span < win:
            acc = acc + shifted(acc, span)
            span *= 2
        sel = jnp.where(group == gi, acc, sel)
        win_lane = jnp.where(group == gi, float(win), win_lane)
    count = jnp.minimum(row.astype(F32) + 1.0, win_lane)
    d = sel / count - x
    y = jnp.dot(d.astype(BF16), w_ref[...], preferred_element_type=F32) * s_ref[...]
    o_ref[0] = y.astype(o_ref.dtype)


def _pool(u, w_bd, scale):
    b, L, w = u.shape
    return pl.pallas_call(
        _pool_kernel,
        grid=(b,),
        in_specs=[pl.BlockSpec((1, L, w), lambda i: (i, 0, 0)),
                  pl.BlockSpec((w, w), lambda i: (0, 0)),
                  pl.BlockSpec((1, w), lambda i: (0, 0))],
        out_specs=pl.BlockSpec((1, L, w), lambda i: (i, 0, 0)),
        out_shape=jax.ShapeDtypeStruct((b, L, w), BF16),
        compiler_params=_params(("arbitrary",)),
        name="pool_mixer",
    )(u, w_bd, scale)


def _group_of(idx, width):
    groups = LANES // SSM_GROUP_DIM
    return (idx >> (width.bit_length() - 1)) & (groups - 1)


def _expand_blockdiag(r_ref, row_inner, col_inner):
    rows, w = r_ref.shape[1:]
    groups = LANES // SSM_GROUP_DIM
    r = lax.broadcasted_iota(jnp.int32, (w, groups * w), 0)
    c = lax.broadcasted_iota(jnp.int32, (w, groups * w), 1)
    shift = col_inner.bit_length() - 1
    outer_c = c >> (shift + groups.bit_length() - 1)
    spread = (outer_c == (r >> shift)) & ((c & (col_inner - 1)) == (r & (col_inner - 1)))
    big = jnp.dot(r_ref[0], spread.astype(BF16), preferred_element_type=F32)
    ri = lax.broadcasted_iota(jnp.int32, (rows, groups * w), 0)
    ci = lax.broadcasted_iota(jnp.int32, (rows, groups * w), 1)
    return jnp.where(_group_of(ri, row_inner) == _group_of(ci, col_inner), big, 0.0).astype(BF16)


def _ssm_kernel(u_ref, rm_ref, rz_ref, ry_ref, coef_ref, y_ref, m_s, wz_s, wy_s, ucat, z_ref, s_ref, *, seqs):
    t = SSM_CHUNK

    @pl.when(pl.program_id(1) == 0)
    def _():
        m_s[...] = _expand_blockdiag(rm_ref, SSM_GROUP_DIM, SSM_GROUP_DIM)
        wz_s[...] = _expand_blockdiag(rz_ref, SSM_GROUP_DIM, SSM_STATE)
        wy_s[...] = _expand_blockdiag(ry_ref, SSM_STATE, SSM_GROUP_DIM)

    nc = u_ref.shape[0] // t
    cps = nc // seqs
    for k in range(t):
        ucat[:, k * LANES:(k + 1) * LANES] = u_ref[pl.ds(k, nc, stride=t), :].astype(BF16)
    u = ucat[...]
    z_ref[...] = jnp.dot(u, wz_s[...], preferred_element_type=F32)
    half = z_ref.shape[1] // 2
    a1 = coef_ref[0, 0:1, :]
    a2 = coef_ref[0, 1:2, :]

    def step(c, states):
        new = []
        for b in range(seqs):
            st = states[b]
            r = b * cps + c
            s_ref[pl.ds(r, 1), :] = st
            sw = jnp.concatenate([st[:, half:], st[:, :half]], axis=1)
            new.append(a1 * st + a2 * sw + z_ref[pl.ds(r, 1), :])
        return tuple(new)

    init = tuple(jnp.zeros((1, 2 * half), F32) for _ in range(seqs))
    lax.fori_loop(0, cps, step, init, unroll=8)
    y = jnp.dot(u, m_s[...], preferred_element_type=F32)
    y = y + jnp.dot(s_ref[...].astype(BF16), wy_s[...], preferred_element_type=F32)
    for k in range(t):
        y_ref[pl.ds(k, nc, stride=t), :] = y[:, k * LANES:(k + 1) * LANES]


def _ssm_core(u, rm, rz, ry, coef, *, seq, seqs_per_block=2):
    n, w = u.shape
    rows = seq * seqs_per_block
    nc = rows // SSM_CHUNK
    kw = SSM_CHUNK * LANES
    p2 = coef.shape[2]
    blk = pl.BlockSpec((rows, LANES), lambda hf, r: (r, hf))
    wspec = lambda arr: pl.BlockSpec((1,) + arr.shape[1:], lambda hf, r: (hf, 0, 0))
    return pl.pallas_call(
        functools.partial(_ssm_kernel, seqs=seqs_per_block),
        grid=(w // LANES, n // rows),
        in_specs=[blk, wspec(rm), wspec(rz), wspec(ry), wspec(coef)],
        out_specs=blk,
        out_shape=jax.ShapeDtypeStruct((n, w), F32),
        scratch_shapes=[pltpu.VMEM((kw, kw), BF16), pltpu.VMEM((kw, p2), BF16), pltpu.VMEM((p2, kw), BF16),
                        pltpu.VMEM((nc, kw), BF16), pltpu.VMEM((nc, p2), F32), pltpu.VMEM((nc, p2), F32)],
        compiler_params=_params(("arbitrary", "arbitrary")),
        name="s5_core",
    )(u, rm, rz, ry, coef)


def _ssm_matrices(a_re, a_im, b_re, b_im, c_re, c_im, d, log_dt):
    t = SSM_CHUNK
    g, p = a_re.shape
    hdim = d.shape[1]
    gl = LANES // hdim
    nb = g // gl
    lam = lax.complex(a_re, a_im)
    lam_dt = lam * jnp.exp(log_dt)[:, None]
    lam_bar = jnp.exp(lam_dt)
    b_bar = ((lam_bar - 1.0) / lam)[:, :, None] * lax.complex(b_re, b_im)
    cmat = lax.complex(c_re, c_im)
    steps = jnp.arange(t + 1, dtype=F32)
    pw = jnp.exp(lam_dt[:, None, :] * steps[None, :, None])
    kern = jnp.einsum('gop,gdp,gpi->gdoi', cmat, pw[:, :t], b_bar).real
    kern = kern.at[:, 0].add(jax.vmap(jnp.diag)(d))
    j = jnp.arange(t)[:, None]
    i = jnp.arange(t)[None, :]
    m = jnp.where((i >= j)[None, :, :, None, None], kern[:, jnp.clip(i - j, 0, t - 1)], 0.0)
    wz = jnp.einsum('gjp,gpi->gjip', pw[:, t - 1 - jnp.arange(t)], b_bar)
    cw = jnp.einsum('gop,gip->giop', cmat, pw[:, 1:t + 1])
    rm = m.reshape(nb, gl, t, t, hdim, hdim).transpose(0, 2, 1, 5, 3, 4).reshape(nb, t * LANES, t * hdim)
    rz = jnp.stack([wz.real, wz.imag], axis=3).reshape(nb, gl, t, hdim, 2, p)
    rz = rz.transpose(0, 2, 1, 3, 4, 5).reshape(nb, t * LANES, 2 * p)
    ry = jnp.stack([cw.real, -cw.imag], axis=1).reshape(nb, gl, 2, t, hdim, p)
    ry = ry.transpose(0, 2, 1, 5, 3, 4).reshape(nb, 2 * gl * p, t * hdim)
    ar = pw[:, t].real.reshape(nb, gl * p)
    ai = pw[:, t].imag.reshape(nb, gl * p)
    coef = jnp.stack([jnp.concatenate([ar, ar], -1), jnp.concatenate([-ai, ai], -1)], axis=1)
    return rm.astype(BF16), rz.astype(BF16), ry.astype(BF16), coef.astype(F32)


def _mixout_kernel(*refs, moe):
    (x_ref, attn_ref, pool_ref, ys_ref, wglu_ref, bglu_ref, ga_ref, gp_ref, gs_ref,
     wo_ref, gpost_ref, gpre_ref) = refs[:12]
    if moe:
        router_ref, xo_ref, h_ref, ri_ref, rg_ref, cnt_ref, cnt_acc = refs[12:]

        @pl.when(pl.program_id(0) == 0)
        def _():
            cnt_acc[...] = jnp.zeros_like(cnt_acc)
    else:
        xo_ref, h_ref = refs[12:]
    a = attn_ref.shape[1]
    pw = pool_ref.shape[1]
    attn_n = _rms(attn_ref[...].astype(F32), ga_ref[...]).astype(BF16)
    pool_n = _rms(pool_ref[...].astype(F32), gp_ref[...]).astype(BF16)
    y = jax.nn.gelu(ys_ref[...])
    gate = jnp.dot(y.astype(BF16), wglu_ref[...], preferred_element_type=F32) + bglu_ref[...]
    ssm = y * jax.nn.sigmoid(gate)
    ssm_n = _rms(ssm, gs_ref[...]).astype(BF16)
    mix = (jnp.dot(attn_n, wo_ref[0:a, :], preferred_element_type=F32)
           + jnp.dot(pool_n, wo_ref[a:a + pw, :], preferred_element_type=F32)
           + jnp.dot(ssm_n, wo_ref[a + pw:, :], preferred_element_type=F32))
    x = x_ref[...] + _rms(mix, gpost_ref[...])
    xo_ref[...] = x
    h = _rms(x, gpre_ref[...])
    if moe:
        _store_slabs(h_ref, 0, h)
    else:
        h_ref[...] = h.astype(h_ref.dtype)
    if moe:
        hi, mid, _ = _split3(h)
        rhi, rmid, _ = _split3(router_ref[...])
        dot_t = lambda p, q: lax.dot_general(p, q, (((1,), (1,)), ((), ())), preferred_element_type=F32)
        logits = dot_t(rhi, hi) + dot_t(rmid, hi) + dot_t(rhi, mid)
        ne, tm = logits.shape
        eidx = lax.broadcasted_iota(jnp.int32, (ne, tm), 0)
        m1 = jnp.max(logits, axis=0, keepdims=True)
        i1 = jnp.min(jnp.where(logits == m1, eidx, ne), axis=0, keepdims=True)
        rest = jnp.where(eidx == i1, -jnp.inf, logits)
        m2 = jnp.max(rest, axis=0, keepdims=True)
        i2 = jnp.min(jnp.where(rest == m2, eidx, ne), axis=0, keepdims=True)
        e2 = jnp.exp(m2 - m1)
        g1 = 1.0 / (1.0 + e2)
        rg_ref[...] = jnp.where(eidx == 0, g1, jnp.where(eidx == 1, e2 * g1, 0.0))
        onehot = ((eidx == i1) | (eidx == i2)).astype(BF16)
        earlier = (lax.broadcasted_iota(jnp.int32, (tm, tm), 0)
                   < lax.broadcasted_iota(jnp.int32, (tm, tm), 1)).astype(BF16)
        before = jnp.dot(onehot, earlier, preferred_element_type=F32) + cnt_acc[:, 0:1]
        r1 = jnp.sum(jnp.where(eidx == i1, before, 0.0), axis=0, keepdims=True).astype(jnp.int32)
        r2 = jnp.sum(jnp.where(eidx == i2, before, 0.0), axis=0, keepdims=True).astype(jnp.int32)
        ri_ref[...] = jnp.where(eidx == 0, i1, jnp.where(eidx == 1, i2,
                                jnp.where(eidx == 2, r1, jnp.where(eidx == 3, r2, 0))))
        cnt_acc[...] += jnp.sum(onehot.astype(F32), axis=1, keepdims=True)
        cnt_ref[...] = cnt_acc[...]


def _mixout(x2, attn, pool, ys, w_glu, b_glu, g_attn, g_pool, g_ssm, w_out, g_post, g_pre,
            router=None, *, tm=512):
    n, d = x2.shape
    moe = router is not None
    row = lambda w: pl.BlockSpec((tm, w), lambda i: (i, 0))
    full = lambda arr: pl.BlockSpec(arr.shape, lambda i: (0, 0))
    ins = [x2, attn, pool, ys, w_glu, b_glu, g_attn, g_pool, g_ssm, w_out, g_post, g_pre]
    in_specs = [row(d), row(attn.shape[1]), row(pool.shape[1]), row(ys.shape[1])] + [full(t) for t in ins[4:]]
    if moe:
        assert d == SUBLANES * LANES
        h_shape = jax.ShapeDtypeStruct((n * SUBLANES, LANES), F32)
        h_spec = pl.BlockSpec((tm * SUBLANES, LANES), lambda i: (i, 0))
    else:
        h_shape, h_spec = jax.ShapeDtypeStruct((n, d), BF16), row(d)
    out_shape = [jax.ShapeDtypeStruct((n, d), F32), h_shape]
    out_specs = [row(d), h_spec]
    scratch = []
    if moe:
        ins.append(router)
        in_specs.append(full(router))
        ne = router.shape[0]
        col = pl.BlockSpec((ne, tm), lambda i: (0, i))
        out_shape += [jax.ShapeDtypeStruct((ne, n), jnp.int32), jax.ShapeDtypeStruct((ne, n), F32),
                      jax.ShapeDtypeStruct((ne, LANES), F32)]
        out_specs += [col, col, pl.BlockSpec((ne, LANES), lambda i: (0, 0))]
        scratch = [pltpu.VMEM((ne, LANES), F32)]
    return pl.pallas_call(
        functools.partial(_mixout_kernel, moe=moe),
        grid=(n // tm,),
        in_specs=in_specs,
        out_specs=tuple(out_specs),
        out_shape=tuple(out_shape),
        scratch_shapes=scratch,
        compiler_params=_params(("arbitrary",)),
        name="mix_out",
    )(*ins)


FFN_STAGE_CHUNKS = 4


def _stage_cast(src_hbm, dst, lane0, stage, sem):
    rows = stage.shape[1]
    width = src_hbm.shape[1]

    def copy(c):
        return pltpu.make_async_copy(src_hbm.at[pl.ds(c * rows, rows)], stage.at[c % 2], sem.at[c % 2])

    nchunks = src_hbm.shape[0] // rows
    copy(0).start()
    for c in range(nchunks):
        if c + 1 < nchunks:
            copy(c + 1).start()
        copy(c).wait()
        dst[pl.ds(c * rows, rows), lane0:lane0 + width] = stage[c % 2].astype(BF16)


def _ffn_kernel(h_ref, x_ref, wg_hbm, wu_hbm, wd_hbm, g_ref, o_ref, wgu_s, wd_s, stage_in, stage_out, sem, *, sub):
    fdim = wd_s.shape[0]

    @pl.when(pl.program_id(0) == 0)
    def _():
        _stage_cast(wg_hbm, wgu_s, 0, stage_in, sem)
        _stage_cast(wu_hbm, wgu_s, fdim, stage_in, sem)
        _stage_cast(wd_hbm, wd_s, 0, stage_out, sem)

    for r in range(h_ref.shape[0] // sub):
        rows = pl.ds(r * sub, sub)
        hgu = jnp.dot(h_ref[rows, :], wgu_s[...], preferred_element_type=F32)
        hg, hu = hgu[:, :fdim], hgu[:, fdim:]
        act = (hg * jax.nn.sigmoid(hg) * hu).astype(BF16)
        y = jnp.dot(act, wd_s[...], preferred_element_type=F32)
        o_ref[rows, :] = x_ref[rows, :] + _rms(y, g_ref[...])


def _ffn(h, x2, wg, wu, wd, g_post, *, tm=512, sub=256):
    n, d = x2.shape
    fdim = wg.shape[1]
    row = pl.BlockSpec((tm, d), lambda i: (i, 0))
    hbm = pl.BlockSpec(memory_space=pl.ANY)
    return pl.pallas_call(
        functools.partial(_ffn_kernel, sub=sub),
        grid=(n // tm,),
        in_specs=[row, row, hbm, hbm, hbm, pl.BlockSpec((1, d), lambda i: (0, 0))],
        out_specs=row,
        out_shape=jax.ShapeDtypeStruct((n, d), F32),
        scratch_shapes=[pltpu.VMEM((d, 2 * fdim), BF16), pltpu.VMEM((fdim, d), BF16),
                        pltpu.VMEM((2, d // FFN_STAGE_CHUNKS, fdim), F32),
                        pltpu.VMEM((2, fdim // FFN_STAGE_CHUNKS, d), F32),
                        pltpu.SemaphoreType.DMA((2,))],
        compiler_params=_params(("arbitrary",)),
        name="dense_ffn",
    )(h, x2, wg, wu, wd, g_post)


MOE_TILE = 512


def _moe_tables(counts, n_items):
    tm = MOE_TILE
    i32 = jnp.int32
    ne = counts.shape[0]
    assert n_items <= LANES
    counts = counts.astype(i32)
    ntiles = (counts + tm - 1) // tm
    k = jnp.arange(ne, dtype=i32)
    ends = jnp.sum(jnp.where(k[None, :] <= k[:, None], ntiles[None, :], 0), axis=1)
    starts = ends - ntiles
    present = ntiles > 0
    later = present[None, :] & (k[None, :] > k[:, None])
    nxt = jnp.min(jnp.where(later, k[None, :], ne), axis=1)
    nxt = jnp.where(nxt < ne, nxt, -1)
    order = jnp.sum((present[None, :] & (k[None, :] < k[:, None])).astype(i32), axis=1)
    w = jnp.arange(LANES, dtype=i32)[None, :]
    wc = jnp.minimum(w, ends[-1] - 1)
    e = jnp.sum((wc >= ends[:, None]).astype(i32), axis=0, keepdims=True)
    of_item = lambda v: jnp.sum(jnp.where(k[:, None] == e, v[:, None], 0), axis=0, keepdims=True)
    valid = w < ends[-1]
    rowblock = jnp.where(valid, w, n_items)
    nvalid = jnp.where(valid, jnp.clip(of_item(counts) - (wc - of_item(starts)) * tm, 0, tm), 0)
    e_prev = jnp.concatenate([e[:, :1], e[:, :-1]], axis=1)
    first = ((w == 0) | (e != e_prev)).astype(i32)
    item = lambda v: v[0, :n_items].astype(i32)
    return ((item(e), item(rowblock), item(nvalid), item(first)),
            (item(of_item(nxt)), item(of_item(order)), jnp.sum(present.astype(i32)).reshape(1)), starts * tm)


def _slab(ref, token):
    return ref.at[pl.ds(pl.multiple_of(token * SUBLANES, SUBLANES), SUBLANES)]


def _dispatch_kernel(dest_ref, h_ref, xs_ref, sem, *, n):
    tm = h_ref.shape[0] // SUBLANES
    base = pl.program_id(0) * tm

    def body(r, carry):
        for k in range(2):
            d = dest_ref[k * n + base + r]
            pltpu.make_async_copy(_slab(h_ref, r), _slab(xs_ref, d), sem).start(priority=k)
        return carry

    lax.fori_loop(0, tm, body, 0, unroll=8)
    for k in range(2):
        pltpu.make_async_copy(h_ref, h_ref, sem).wait()


def _dispatch(dest, h, cap_rows, *, tm=512):
    n = h.shape[0] // SUBLANES
    return pl.pallas_call(
        functools.partial(_dispatch_kernel, n=n),
        grid_spec=pltpu.PrefetchScalarGridSpec(
            num_scalar_prefetch=1,
            grid=(n // tm,),
            in_specs=[pl.BlockSpec((tm * SUBLANES, LANES), lambda i, dest: (i, 0))],
            out_specs=pl.BlockSpec(memory_space=pl.ANY),
            scratch_shapes=[pltpu.SemaphoreType.DMA(())]),
        out_shape=jax.ShapeDtypeStruct((cap_rows * SUBLANES, LANES), F32),
        compiler_params=_params(("arbitrary",)),
        name="moe_dispatch",
    )(dest, h)


MOE_SUB = 256


MOE_UP_CHUNKS = 4


def _moe_up_kernel(ie_ref, rb_ref, nv_ref, first_ref, nxt_ref, ord_ref, npres_ref, xs_ref, wg_hbm, wu_hbm, h_ref,
                   wgu_s, stage, sem):
    f = pl.program_id(0)
    w = pl.program_id(1)
    nf = pl.num_programs(0)
    tf = h_ref.shape[1]
    rows = stage.shape[0] // MOE_UP_CHUNKS

    def copies(e, fp):
        return [pltpu.make_async_copy(src.at[e, pl.ds(c * rows, rows), pl.ds(fp * tf, tf)],
                                      stage.at[pl.ds(c * rows, rows), pl.ds(m * tf, tf)],
                                      sem.at[m * MOE_UP_CHUNKS + c])
                for m, src in enumerate((wg_hbm, wu_hbm)) for c in range(MOE_UP_CHUNKS)]

    @pl.when((f == 0) & (w == 0))
    def _():
        for cp in copies(ie_ref[0], 0):
            cp.start()

    slot = (f * npres_ref[0] + ord_ref[w]) % 2
    is_first = first_ref[w] == 1
    last_expert = nxt_ref[w] < 0
    for fp in range(nf):
        @pl.when(is_first & (f == fp))
        def _():
            for cp in copies(ie_ref[w], fp):
                cp.wait()
            for c in range(MOE_UP_CHUNKS):
                r = pl.ds(c * rows, rows)
                wgu_s[slot, r, :] = stage[r, :].astype(BF16)

            @pl.when(jnp.logical_not(last_expert))
            def _():
                for cp in copies(nxt_ref[w], fp):
                    cp.start()

            if fp + 1 < nf:
                @pl.when(last_expert)
                def _():
                    for cp in copies(ie_ref[0], fp + 1):
                        cp.start()

    nv = nv_ref[w]

    @pl.when(nv > 0)
    def _():
        wgu_view = wgu_s.at[slot]
        for r in range(h_ref.shape[0] // MOE_SUB):
            rws = pl.ds(r * MOE_SUB, MOE_SUB)
            idx = r * MOE_SUB + lax.broadcasted_iota(jnp.int32, (MOE_SUB, 1), 0)
            x = _load_slabs(xs_ref, r * MOE_SUB, MOE_SUB)
            x = jnp.where(idx < nv, x, 0.0).astype(BF16)
            hgu = jnp.dot(x, wgu_view[...], preferred_element_type=F32)
            hg, hu = hgu[:, :tf], hgu[:, tf:]
            h_ref[rws, :] = (hg * jax.nn.sigmoid(hg) * hu).astype(BF16)

    @pl.when(nv == 0)
    def _():
        h_ref[...] = jnp.zeros_like(h_ref)


def _moe_up(tables, order, xs, wg, wu, *, tf=1792):
    tm = MOE_TILE
    cap_rows = xs.shape[0] // SUBLANES
    ne, d, fdim = wg.shape
    n_items = tables[0].shape[0]
    hbm = pl.BlockSpec(memory_space=pl.ANY)
    return pl.pallas_call(
        _moe_up_kernel,
        grid_spec=pltpu.PrefetchScalarGridSpec(
            num_scalar_prefetch=7,
            grid=(fdim // tf, n_items),
            in_specs=[pl.BlockSpec((tm * SUBLANES, LANES), lambda f, w, ie, rb, *_: (rb[w], 0)), hbm, hbm],
            out_specs=pl.BlockSpec((tm, tf), lambda f, w, ie, rb, *_: (rb[w], f)),
            scratch_shapes=[pltpu.VMEM((2, d, 2 * tf), BF16), pltpu.VMEM((d, 2 * tf), F32),
                            pltpu.SemaphoreType.DMA((2 * MOE_UP_CHUNKS,))]),
        out_shape=jax.ShapeDtypeStruct((cap_rows, fdim), BF16),
        compiler_params=_params(("arbitrary", "arbitrary")),
        name="moe_up",
    )(*tables, *order, xs, wg, wu)


def _moe_down_kernel(ie_ref, rb_ref, nv_ref, first_ref, nxt_ref, ord_ref, npres_ref, h_ref, wd_hbm, y_ref,
                     wd_s, stage, sem):
    w = pl.program_id(0)
    slot = ord_ref[w] % 2
    nchunks = sem.shape[0]
    chunk = stage.shape[0] // nchunks

    def copies(e):
        return [pltpu.make_async_copy(wd_hbm.at[e, pl.ds(c * chunk, chunk)], stage.at[pl.ds(c * chunk, chunk)],
                                      sem.at[c]) for c in range(nchunks)]

    @pl.when(w == 0)
    def _():
        for cp in copies(ie_ref[0]):
            cp.start()

    @pl.when(first_ref[w] == 1)
    def _():
        for c, cp in enumerate(copies(ie_ref[w])):
            cp.wait()
            wd_s[slot, pl.ds(c * chunk, chunk), :] = stage[pl.ds(c * chunk, chunk), :].astype(BF16)

        @pl.when(nxt_ref[w] >= 0)
        def _():
            for cp in copies(nxt_ref[w]):
                cp.start()

    @pl.when(nv_ref[w] > 0)
    def _():
        wd = wd_s[slot]
        for r in range(h_ref.shape[0] // MOE_SUB):
            y = jnp.dot(h_ref[pl.ds(r * MOE_SUB, MOE_SUB), :], wd, preferred_element_type=F32)
            _store_slabs(y_ref, r * MOE_SUB, y)

    @pl.when(nv_ref[w] == 0)
    def _():
        y_ref[...] = jnp.zeros_like(y_ref)


def _moe_down(tables, order, hid, wd, *, chunk=512):
    tm = MOE_TILE
    cap_rows, fdim = hid.shape
    d = wd.shape[2]
    n_items = tables[0].shape[0]
    assert d == SUBLANES * LANES and fdim % chunk == 0
    return pl.pallas_call(
        _moe_down_kernel,
        grid_spec=pltpu.PrefetchScalarGridSpec(
            num_scalar_prefetch=7,
            grid=(n_items,),
            in_specs=[pl.BlockSpec((tm, fdim), lambda w, ie, rb, *_: (rb[w], 0)),
                      pl.BlockSpec(memory_space=pl.ANY)],
            out_specs=pl.BlockSpec((tm * SUBLANES, LANES), lambda w, ie, rb, *_: (rb[w], 0)),
            scratch_shapes=[pltpu.VMEM((2, fdim, d), BF16), pltpu.VMEM((fdim, d), F32),
                            pltpu.SemaphoreType.DMA((fdim // chunk,))]),
        out_shape=jax.ShapeDtypeStruct((cap_rows * SUBLANES, LANES), F32),
        compiler_params=_params(("arbitrary",)),
        name="moe_down",
    )(*tables, *order, hid, wd)


def _combine_kernel(dest_ref, x_ref, gate_ref, g_ref, ys_ref, o_ref, buf, sem, *, n):
    tm = x_ref.shape[0]
    i = pl.program_id(0)

    def issue(tile, slot):
        def body(r, carry):
            for k in range(2):
                d = dest_ref[k * n + tile * tm + r]
                pltpu.make_async_copy(_slab(ys_ref, d), _slab(buf.at[slot, k], r),
                                      sem.at[slot]).start(priority=k)
            return carry
        lax.fori_loop(0, tm, body, 0, unroll=8)

    @pl.when(i == 0)
    def _():
        issue(0, 0)

    @pl.when(i + 1 < pl.num_programs(0))
    def _():
        issue(i + 1, (i + 1) % 2)

    slot = i % 2
    for k in range(2):
        pltpu.make_async_copy(buf.at[slot, k], buf.at[slot, k], sem.at[slot]).wait()
    gates = gate_ref[...].T
    f = (gates[:, 0:1] * _load_slabs(buf.at[slot, 0], 0, tm)
         + gates[:, 1:2] * _load_slabs(buf.at[slot, 1], 0, tm))
    o_ref[...] = x_ref[...] + _rms(f, g_ref[...])


def _combine(dest, x2, gates, g_post, ys, *, tm=256):
    n, d = x2.shape
    return pl.pallas_call(
        functools.partial(_combine_kernel, n=n),
        grid_spec=pltpu.PrefetchScalarGridSpec(
            num_scalar_prefetch=1,
            grid=(n // tm,),
            in_specs=[pl.BlockSpec((tm, d), lambda i, dest: (i, 0)),
                      pl.BlockSpec((gates.shape[0], tm), lambda i, dest: (0, i)),
                      pl.BlockSpec((1, d), lambda i, dest: (0, 0)),
                      pl.BlockSpec(memory_space=pl.ANY)],
            out_specs=pl.BlockSpec((tm, d), lambda i, dest: (i, 0)),
            scratch_shapes=[pltpu.VMEM((2, 2, tm * SUBLANES, LANES), F32), pltpu.SemaphoreType.DMA((2,))]),
        out_shape=jax.ShapeDtypeStruct((n, d), F32),
        compiler_params=_params(("arbitrary",)),
        name="moe_combine",
    )(dest, x2, gates, g_post, ys)


def _moe(h, x2, route_i, route_g, counts, wg, wu, wd, g_post):
    n = x2.shape[0]
    n_items = 2 * n // MOE_TILE + N_EXPERTS
    tables, order, base = _moe_tables(counts[:, 0], n_items)
    base_of = lambda e: jnp.sum(jnp.where(e[None, :] == jnp.arange(N_EXPERTS)[:, None], base[:, None], 0), axis=0)
    dest = jnp.concatenate([base_of(route_i[0]) + route_i[2], base_of(route_i[1]) + route_i[3]])
    xs = _dispatch(dest, h, (n_items + 1) * MOE_TILE)
    hid = _moe_up(tables, order, xs, wg, wu)
    ys = _moe_down(tables, order, hid, wd)
    return _combine(dest, x2, route_g, g_post, ys)


def _row(v):
    return v.reshape(1, -1).astype(F32)


def kernel(x, norm_mix_pre, norm_mix_post, norm_ffn_pre, norm_ffn_post, w_in, b_forget, pool_w, pool_scale, ssm_a_re, ssm_a_im, ssm_b_re, ssm_b_im, ssm_c_re, ssm_c_im, ssm_d, ssm_log_dt, ssm_w_glu, ssm_b_glu, branch_norm_attn, branch_norm_pool, branch_norm_ssm, w_out, ffn_w_gate, ffn_w_up, ffn_w_down, moe_router, moe_w_gate, moe_w_up, moe_w_down):
    b, L, d = x.shape
    depth = w_in.shape[0]
    n = b * L
    a = ATTN_WIDTH
    attn_tile = 512
    x2 = x.reshape(n, d)
    for i in range(depth):
        b_f = jnp.pad(b_forget[i], (0, LANES - ATTN_HEADS)).reshape(1, LANES)
        q, k, v, up, us, c = _inproj(x2, _row(norm_mix_pre[i]), w_in, b_f, layer=i, seq=L)

        ct = c.reshape(b, L, ATTN_HEADS).transpose(0, 2, 1).reshape(b, ATTN_HEADS * (L // attn_tile), attn_tile)
        attn = _attention(q.reshape(b, L, a), k.reshape(b, L, a), v.reshape(b, L, a),
                          ct, tq=attn_tile).reshape(n, a)

        pool = _pool(up.reshape(b, L, POOL_WIDTH),
                     jax.scipy.linalg.block_diag(*pool_w[i]).astype(BF16),
                     _row(pool_scale[i])).reshape(n, POOL_WIDTH)

        mats = _ssm_matrices(ssm_a_re[i], ssm_a_im[i], ssm_b_re[i], ssm_b_im[i],
                             ssm_c_re[i], ssm_c_im[i], ssm_d[i], ssm_log_dt[i])
        ys = _ssm_core(us, *mats, seq=L)

        moe = i % 2 == 1
        j = i // 2
        router = moe_router[j].T if moe else None
        res = _mixout(x2, attn, pool, ys, ssm_w_glu[i].astype(BF16), _row(ssm_b_glu[i]),
                      _row(branch_norm_attn[i]), _row(branch_norm_pool[i]), _row(branch_norm_ssm[i]),
                      w_out[i].astype(BF16), _row(norm_mix_post[i]), _row(norm_ffn_pre[i]), router)
        if moe:
            x2, h, route_i, route_g, counts = res
            x2 = _moe(h, x2, route_i, route_g, counts, moe_w_gate[j], moe_w_up[j], moe_w_down[j],
                      _row(norm_ffn_post[i]))
        else:
            x2, h = res
            x2 = _ffn(h, x2, ffn_w_gate[j], ffn_w_up[j], ffn_w_down[j], _row(norm_ffn_post[i]))
    return x2.reshape(b, L, d)
```

```python
import functools
import math

import jax
import jax.numpy as jnp
from jax import lax
from jax.experimental import pallas as pl
from jax.experimental.pallas import tpu as pltpu

F32 = jnp.float32
BF16 = jnp.bfloat16

RMS_EPS = 1e-6
NEG_INF = -1e30
LOG2E = 1.4426950408889634

ATTN_HEADS = 8
ATTN_HEAD_DIM = 64
ATTN_WIDTH = ATTN_HEADS * ATTN_HEAD_DIM
POOL_WINDOWS = (2, 4, 8, 16)
POOL_GROUP_DIM = 64
POOL_WIDTH = len(POOL_WINDOWS) * POOL_GROUP_DIM
SSM_GROUPS = 16
SSM_GROUP_DIM = 16
SSM_STATE = 64
SSM_WIDTH = SSM_GROUPS * SSM_GROUP_DIM
SSM_CHUNK = 8
N_EXPERTS = 8

LANES = 128
SUBLANES = 8
VMEM_LIMIT = 56 * 1024 * 1024


def _params(sem):
    return pltpu.CompilerParams(dimension_semantics=sem, vmem_limit_bytes=VMEM_LIMIT)


def _rms(x, g):
    return x * lax.rsqrt(jnp.mean(x * x, axis=-1, keepdims=True) + RMS_EPS) * g


def _load_slabs(ref, row0, rows):
    return jnp.concatenate([ref[pl.ds(row0 * SUBLANES + s, rows, stride=SUBLANES), :]
                            for s in range(SUBLANES)], axis=1)


def _store_slabs(ref, row0, val):
    for s in range(SUBLANES):
        ref[pl.ds(row0 * SUBLANES + s, val.shape[0], stride=SUBLANES), :] = val[:, s * LANES:(s + 1) * LANES]


def _split3(x):
    hi = x.astype(BF16)
    r = x - hi.astype(F32)
    mid = r.astype(BF16)
    lo = (r - mid.astype(F32)).astype(BF16)
    return hi, mid, lo


INPROJ_STAGE_ROWS = 256
INPROJ_SUB = 256


def _inproj_kernel(x_ref, g_ref, w_hbm, bf_ref, q_ref, k_ref, v_ref, up_ref, us_ref, c_ref,
                   w_s, stage, sem, carry_ref, *, layer, tiles_per_seq):
    a = ATTN_WIDTH
    f_off = 3 * a
    p_off = f_off + ATTN_HEADS
    main = f_off + POOL_WIDTH + SSM_WIDTH

    @pl.when(pl.program_id(0) == 0)
    def _():
        rows = stage.shape[1]

        def copy(c):
            return pltpu.make_async_copy(w_hbm.at[layer, pl.ds(c * rows, rows)], stage.at[c % 2], sem.at[c % 2])

        nchunks = w_s.shape[0] // rows
        copy(0).start()
        for c in range(nchunks):
            if c + 1 < nchunks:
                copy(c + 1).start()
            copy(c).wait()
            blk = stage[c % 2]
            r = pl.ds(c * rows, rows)
            w_s[r, 0:f_off] = blk[:, 0:f_off].astype(BF16)
            w_s[r, f_off:main] = blk[:, p_off:p_off + main - f_off].astype(BF16)
            lane = lax.broadcasted_iota(jnp.int32, (rows, LANES), 1)
            w_s[r, main:] = jnp.where(lane < ATTN_HEADS, blk[:, f_off:f_off + LANES], 0.0).astype(BF16)

    @pl.when(pl.program_id(0) % tiles_per_seq == 0)
    def _():
        carry_ref[...] = jnp.zeros_like(carry_ref)

    tm = x_ref.shape[0]
    zs = []
    for s in range(tm // INPROJ_SUB):
        r = pl.ds(s * INPROJ_SUB, INPROJ_SUB)
        h = _rms(x_ref[r, :], g_ref[...]).astype(BF16)
        proj = jnp.dot(h, w_s[...], preferred_element_type=F32)
        q_ref[r, :] = (proj[:, 0:a] * (LOG2E * ATTN_HEAD_DIM ** -0.5)).astype(BF16)
        k_ref[r, :] = proj[:, a:2 * a].astype(BF16)
        v_ref[r, :] = proj[:, 2 * a:3 * a].astype(BF16)
        up_ref[r, :] = proj[:, 3 * a:3 * a + POOL_WIDTH].astype(BF16)
        us_ref[r, :] = proj[:, 3 * a + POOL_WIDTH:main]
        zs.append(proj[:, main:])

    z = jnp.concatenate(zs, axis=0) + bf_ref[...]
    logf = jnp.minimum(z, 0.0) - jnp.log(1.0 + jnp.exp(-jnp.abs(z)))
    row = lax.broadcasted_iota(jnp.int32, (tm, tm), 0)
    col = lax.broadcasted_iota(jnp.int32, (tm, tm), 1)
    tri = (row >= col).astype(BF16)
    hi, mid, _ = _split3(logf)
    c = (jnp.dot(tri, hi, preferred_element_type=F32)
         + jnp.dot(tri, mid, preferred_element_type=F32)) + carry_ref[...]
    c_ref[...] = c[:, :ATTN_HEADS]
    carry_ref[...] = c[tm - 1:tm, :]


def _inproj(x2, g, w_in, b_f, *, layer, seq, tm=512):
    n, d = x2.shape
    n_in = w_in.shape[2]
    a = ATTN_WIDTH
    n_res = n_in - ATTN_HEADS + LANES
    out_shape = (
        jax.ShapeDtypeStruct((n, a), BF16), jax.ShapeDtypeStruct((n, a), BF16),
        jax.ShapeDtypeStruct((n, a), BF16), jax.ShapeDtypeStruct((n, POOL_WIDTH), BF16),
        jax.ShapeDtypeStruct((n, SSM_WIDTH), F32), jax.ShapeDtypeStruct((n, ATTN_HEADS), F32))
    row = lambda w: pl.BlockSpec((tm, w), lambda i: (i, 0))
    full = lambda r, c: pl.BlockSpec((r, c), lambda i: (0, 0))
    return pl.pallas_call(
        functools.partial(_inproj_kernel, layer=layer, tiles_per_seq=seq // tm),
        grid=(n // tm,),
        in_specs=[row(d), full(1, d), pl.BlockSpec(memory_space=pl.ANY), full(1, LANES)],
        out_specs=(row(a), row(a), row(a), row(POOL_WIDTH), row(SSM_WIDTH), row(ATTN_HEADS)),
        out_shape=out_shape,
        scratch_shapes=[pltpu.VMEM((d, n_res), BF16), pltpu.VMEM((2, INPROJ_STAGE_ROWS, n_in), F32),
                        pltpu.SemaphoreType.DMA((2,)), pltpu.VMEM((1, LANES), F32)],
        compiler_params=_params(("arbitrary",)),
        name="inproj",
    )(x2, g, w_in, b_f)


HEADS_PER_BLOCK = LANES // ATTN_HEAD_DIM


def _attn_kernel(q_ref, k_ref, v_ref, ct_ref, o_ref, *, tq, tk):
    nlb = q_ref.shape[2] // LANES
    hp = pl.program_id(1)
    qi = pl.program_id(2)
    nkb = k_ref.shape[1] // tk
    lane = lax.broadcasted_iota(jnp.int32, (1, LANES), 1)
    in_head = [(lane // ATTN_HEAD_DIM) == hh for hh in range(HEADS_PER_BLOCK)]
    chains = [(lb, hh) for lb in range(nlb) for hh in range(HEADS_PER_BLOCK)]
    qm, crow, cref = [], [], []
    for lb in range(nlb):
        q2 = q_ref[0, :, lb * LANES:(lb + 1) * LANES]
        qm.append(jnp.concatenate([jnp.where(msk, q2, jnp.zeros_like(q2)) for msk in in_head], axis=0))
    for lb, hh in chains:
        crow.append(((hp * nlb + lb) * HEADS_PER_BLOCK + hh) * nkb)
        cref.append(ct_ref[0, pl.ds(crow[-1] + qi, 1), :][:, 0:1])

    def step(kb, carry, masked):
        new = []
        scores = {}
        for ci, (lb, hh) in enumerate(chains):
            kblk = k_ref[0, pl.ds(kb * tk, tk), lb * LANES:(lb + 1) * LANES]
            vblk = v_ref[0, pl.ds(kb * tk, tk), lb * LANES:(lb + 1) * LANES]
            m, acc = carry[ci]
            if lb not in scores:
                scores[lb] = lax.dot_general(qm[lb], kblk, (((1,), (1,)), ((), ())), preferred_element_type=F32)
            s = scores[lb][hh * tq:(hh + 1) * tq, :]
            s = s - (ct_ref[0, pl.ds(crow[ci] + kb, 1), :] - cref[ci]) * LOG2E
            if masked:
                r = lax.broadcasted_iota(jnp.int32, (tq, tk), 0)
                cidx = lax.broadcasted_iota(jnp.int32, (tq, tk), 1)
                s = jnp.where(cidx <= r, s, NEG_INF)
            m_new = jnp.maximum(m, jnp.max(s, axis=1, keepdims=True))
            p = jnp.exp2(s - m_new)
            vsel = jnp.where(in_head[hh], vblk, jnp.ones_like(vblk))
            acc = jnp.exp2(m - m_new) * acc + jnp.dot(p.astype(BF16), vsel, preferred_element_type=F32)
            new.append((m_new, acc))
        return tuple(new)

    init = tuple((jnp.full((tq, 1), NEG_INF, F32), jnp.zeros((tq, LANES), F32)) for _ in chains)
    carry = lax.fori_loop(0, qi, lambda kb, cr: step(kb, cr, False), init)
    carry = step(qi, carry, True)
    for lb in range(nlb):
        out = jnp.zeros((tq, LANES), F32)
        for hh in range(HEADS_PER_BLOCK):
            acc = carry[lb * HEADS_PER_BLOCK + hh][1]
            out = jnp.where(in_head[hh], acc / pltpu.roll(acc, ATTN_HEAD_DIM, 1), out)
        o_ref[0, :, lb * LANES:(lb + 1) * LANES] = out.astype(o_ref.dtype)


def _attention(q, k, v, ct, *, tq, lane_blocks=4):
    b, L, a = q.shape
    tk = ct.shape[2]
    assert tq == tk, "the diagonal key block of query tile i must be key block i"
    w = lane_blocks * LANES
    blk = lambda: pl.BlockSpec((1, tq, w), lambda bi, hi, qi: (bi, qi, hi))
    seq = lambda: pl.BlockSpec((1, L, w), lambda bi, hi, qi: (bi, 0, hi))
    return pl.pallas_call(
        functools.partial(_attn_kernel, tq=tq, tk=tk),
        grid=(b, a // w, L // tq),
        in_specs=[blk(), seq(), seq(),
                  pl.BlockSpec((1, ct.shape[1], tk), lambda bi, hi, qi: (bi, 0, 0))],
        out_specs=blk(),
        out_shape=jax.ShapeDtypeStruct((b, L, a), BF16),
        compiler_params=_params(("arbitrary", "arbitrary", "arbitrary")),
        name="fox_attention",
    )(q, k, v, ct)


def _pool_kernel(u_ref, w_ref, s_ref, o_ref):
    x = u_ref[0].astype(F32)
    L, w = x.shape
    row = lax.broadcasted_iota(jnp.int32, (L, w), 0)
    group = lax.broadcasted_iota(jnp.int32, (L, w), 1) // POOL_GROUP_DIM

    def shifted(y, s):
        return jnp.where(row >= s, pltpu.roll(y, s, 0), 0.0)

    acc = x
    sel = jnp.zeros_like(x)
    win_lane = jnp.zeros((L, w), F32)
    span = 1
    for gi, win in enumerate(POOL_WINDOWS):
        while span < win:
            acc = acc + shifted(acc, span)
            span *= 2
        sel = jnp.where(group == gi, acc, sel)
        win_lane = jnp.where(group == gi, float(win), win_lane)
    count = jnp.minimum(row.astype(F32) + 1.0, win_lane)
    d = sel / count - x
    y = jnp.dot(d.astype(BF16), w_ref[...], preferred_element_type=F32) * s_ref[...]
    o_ref[0] = y.astype(o_ref.dtype)


def _pool(u, w_bd, scale):
    b, L, w = u.shape
    return pl.pallas_call(
        _pool_kernel,
        grid=(b,),
        in_specs=[pl.BlockSpec((1, L, w), lambda i: (i, 0, 0)),
                  pl.BlockSpec((w, w), lambda i: (0, 0)),
                  pl.BlockSpec((1, w), lambda i: (0, 0))],
        out_specs=pl.BlockSpec((1, L, w), lambda i: (i, 0, 0)),
        out_shape=jax.ShapeDtypeStruct((b, L, w), BF16),
        compiler_params=_params(("arbitrary",)),
        name="pool_mixer",
    )(u, w_bd, scale)


def _group_of(idx, width):
    groups = LANES // SSM_GROUP_DIM
    return (idx >> (width.bit_length() - 1)) & (groups - 1)


def _expand_blockdiag(r_ref, row_inner, col_inner):
    rows, w = r_ref.shape[1:]
    groups = LANES // SSM_GROUP_DIM
    r = lax.broadcasted_iota(jnp.int32, (w, groups * w), 0)
    c = lax.broadcasted_iota(jnp.int32, (w, groups * w), 1)
    shift = col_inner.bit_length() - 1
    outer_c = c >> (shift + groups.bit_length() - 1)
    spread = (outer_c == (r >> shift)) & ((c & (col_inner - 1)) == (r & (col_inner - 1)))
    big = jnp.dot(r_ref[0], spread.astype(BF16), preferred_element_type=F32)
    ri = lax.broadcasted_iota(jnp.int32, (rows, groups * w), 0)
    ci = lax.broadcasted_iota(jnp.int32, (rows, groups * w), 1)
    return jnp.where(_group_of(ri, row_inner) == _group_of(ci, col_inner), big, 0.0).astype(BF16)


def _ssm_kernel(u_ref, rm_ref, rz_ref, ry_ref, coef_ref, y_ref, m_s, wz_s, wy_s, ucat, z_ref, s_ref, *, seqs):
    t = SSM_CHUNK

    @pl.when(pl.program_id(1) == 0)
    def _():
        m_s[...] = _expand_blockdiag(rm_ref, SSM_GROUP_DIM, SSM_GROUP_DIM)
        wz_s[...] = _expand_blockdiag(rz_ref, SSM_GROUP_DIM, SSM_STATE)
        wy_s[...] = _expand_blockdiag(ry_ref, SSM_STATE, SSM_GROUP_DIM)

    nc = u_ref.shape[0] // t
    cps = nc // seqs
    for k in range(t):
        ucat[:, k * LANES:(k + 1) * LANES] = u_ref[pl.ds(k, nc, stride=t), :].astype(BF16)
    u = ucat[...]
    z_ref[...] = jnp.dot(u, wz_s[...], preferred_element_type=F32)
    half = z_ref.shape[1] // 2
    a1 = coef_ref[0, 0:1, :]
    a2 = coef_ref[0, 1:2, :]

    def step(c, states):
        new = []
        for b in range(seqs):
            st = states[b]
            r = b * cps + c
            s_ref[pl.ds(r, 1), :] = st
            sw = jnp.concatenate([st[:, half:], st[:, :half]], axis=1)
            new.append(a1 * st + a2 * sw + z_ref[pl.ds(r, 1), :])
        return tuple(new)

    init = tuple(jnp.zeros((1, 2 * half), F32) for _ in range(seqs))
    lax.fori_loop(0, cps, step, init, unroll=8)
    y = jnp.dot(u, m_s[...], preferred_element_type=F32)
    y = y + jnp.dot(s_ref[...].astype(BF16), wy_s[...], preferred_element_type=F32)
    for k in range(t):
        y_ref[pl.ds(k, nc, stride=t), :] = y[:, k * LANES:(k + 1) * LANES]


def _ssm_core(u, rm, rz, ry, coef, *, seq, max_seqs_per_block=4):
    n, w = u.shape
    seqs_per_block = math.gcd(max_seqs_per_block, n // seq)
    rows = seq * seqs_per_block
    nc = rows // SSM_CHUNK
    kw = SSM_CHUNK * LANES
    p2 = coef.shape[2]
    blk = pl.BlockSpec((rows, LANES), lambda hf, r: (r, hf))
    wspec = lambda arr: pl.BlockSpec((1,) + arr.shape[1:], lambda hf, r: (hf, 0, 0))
    return pl.pallas_call(
        functools.partial(_ssm_kernel, seqs=seqs_per_block),
        grid=(w // LANES, n // rows),
        in_specs=[blk, wspec(rm), wspec(rz), wspec(ry), wspec(coef)],
        out_specs=blk,
        out_shape=jax.ShapeDtypeStruct((n, w), F32),
        scratch_shapes=[pltpu.VMEM((kw, kw), BF16), pltpu.VMEM((kw, p2), BF16), pltpu.VMEM((p2, kw), BF16),
                        pltpu.VMEM((nc, kw), BF16), pltpu.VMEM((nc, p2), F32), pltpu.VMEM((nc, p2), F32)],
        compiler_params=_params(("arbitrary", "arbitrary")),
        name="s5_core",
    )(u, rm, rz, ry, coef)


def _ssm_matrices(a_re, a_im, b_re, b_im, c_re, c_im, d, log_dt):
    t = SSM_CHUNK
    g, p = a_re.shape
    hdim = d.shape[1]
    gl = LANES // hdim
    nb = g // gl
    lam = lax.complex(a_re, a_im)
    lam_dt = lam * jnp.exp(log_dt)[:, None]
    lam_bar = jnp.exp(lam_dt)
    b_bar = ((lam_bar - 1.0) / lam)[:, :, None] * lax.complex(b_re, b_im)
    cmat = lax.complex(c_re, c_im)
    steps = jnp.arange(t + 1, dtype=F32)
    pw = jnp.exp(lam_dt[:, None, :] * steps[None, :, None])
    kern = jnp.einsum('gop,gdp,gpi->gdoi', cmat, pw[:, :t], b_bar).real
    kern = kern.at[:, 0].add(jax.vmap(jnp.diag)(d))
    j = jnp.arange(t)[:, None]
    i = jnp.arange(t)[None, :]
    m = jnp.where((i >= j)[None, :, :, None, None], kern[:, jnp.clip(i - j, 0, t - 1)], 0.0)
    wz = jnp.einsum('gjp,gpi->gjip', pw[:, t - 1 - jnp.arange(t)], b_bar)
    cw = jnp.einsum('gop,gip->giop', cmat, pw[:, 1:t + 1])
    rm = m.reshape(nb, gl, t, t, hdim, hdim).transpose(0, 2, 1, 5, 3, 4).reshape(nb, t * LANES, t * hdim)
    rz = jnp.stack([wz.real, wz.imag], axis=3).reshape(nb, gl, t, hdim, 2, p)
    rz = rz.transpose(0, 2, 1, 3, 4, 5).reshape(nb, t * LANES, 2 * p)
    ry = jnp.stack([cw.real, -cw.imag], axis=1).reshape(nb, gl, 2, t, hdim, p)
    ry = ry.transpose(0, 2, 1, 5, 3, 4).reshape(nb, 2 * gl * p, t * hdim)
    ar = pw[:, t].real.reshape(nb, gl * p)
    ai = pw[:, t].imag.reshape(nb, gl * p)
    coef = jnp.stack([jnp.concatenate([ar, ar], -1), jnp.concatenate([-ai, ai], -1)], axis=1)
    return rm.astype(BF16), rz.astype(BF16), ry.astype(BF16), coef.astype(F32)


MIX_SUB = 256


def _mixout_kernel(*refs, moe):
    (x_ref, attn_ref, pool_ref, ys_ref, wglu_ref, bglu_ref, ga_ref, gp_ref, gs_ref,
     wo_ref, gpost_ref, gpre_ref) = refs[:12]
    if moe:
        router_ref, xo_ref, h_ref, ri_ref, rg_ref, cnt_ref, cnt_acc = refs[12:]

        @pl.when(pl.program_id(0) == 0)
        def _():
            cnt_acc[...] = jnp.zeros_like(cnt_acc)
    else:
        xo_ref, h_ref = refs[12:]
    a = attn_ref.shape[1]
    pw = pool_ref.shape[1]
    h_parts = []
    for s in range(x_ref.shape[0] // MIX_SUB):
        r = pl.ds(s * MIX_SUB, MIX_SUB)
        attn_n = _rms(attn_ref[r, :].astype(F32), ga_ref[...]).astype(BF16)
        pool_n = _rms(pool_ref[r, :].astype(F32), gp_ref[...]).astype(BF16)
        y = jax.nn.gelu(ys_ref[r, :])
        gate = jnp.dot(y.astype(BF16), wglu_ref[...], preferred_element_type=F32) + bglu_ref[...]
        ssm = y * jax.nn.sigmoid(gate)
        ssm_n = _rms(ssm, gs_ref[...]).astype(BF16)
        mix = (jnp.dot(attn_n, wo_ref[0:a, :], preferred_element_type=F32)
               + jnp.dot(pool_n, wo_ref[a:a + pw, :], preferred_element_type=F32)
               + jnp.dot(ssm_n, wo_ref[a + pw:, :], preferred_element_type=F32))
        x = x_ref[r, :] + _rms(mix, gpost_ref[...])
        xo_ref[r, :] = x
        h = _rms(x, gpre_ref[...])
        if moe:
            _store_slabs(h_ref, s * MIX_SUB, h)
            h_parts.append(h)
        else:
            h_ref[r, :] = h.astype(h_ref.dtype)
    if moe:
        hi, mid, _ = _split3(jnp.concatenate(h_parts, axis=0))
        rhi, rmid, _ = _split3(router_ref[...])
        dot_t = lambda p, q: lax.dot_general(p, q, (((1,), (1,)), ((), ())), preferred_element_type=F32)
        logits = dot_t(rhi, hi) + dot_t(rmid, hi) + dot_t(rhi, mid)
        ne, tm = logits.shape
        eidx = lax.broadcasted_iota(jnp.int32, (ne, tm), 0)
        m1 = jnp.max(logits, axis=0, keepdims=True)
        i1 = jnp.min(jnp.where(logits == m1, eidx, ne), axis=0, keepdims=True)
        rest = jnp.where(eidx == i1, -jnp.inf, logits)
        m2 = jnp.max(rest, axis=0, keepdims=True)
        i2 = jnp.min(jnp.where(rest == m2, eidx, ne), axis=0, keepdims=True)
        e2 = jnp.exp(m2 - m1)
        g1 = 1.0 / (1.0 + e2)
        rg_ref[...] = jnp.where(eidx == 0, g1, jnp.where(eidx == 1, e2 * g1, 0.0))
        onehot = ((eidx == i1) | (eidx == i2)).astype(BF16)
        earlier = (lax.broadcasted_iota(jnp.int32, (tm, tm), 0)
                   < lax.broadcasted_iota(jnp.int32, (tm, tm), 1)).astype(BF16)
        before = jnp.dot(onehot, earlier, preferred_element_type=F32) + cnt_acc[:, 0:1]
        r1 = jnp.sum(jnp.where(eidx == i1, before, 0.0), axis=0, keepdims=True).astype(jnp.int32)
        r2 = jnp.sum(jnp.where(eidx == i2, before, 0.0), axis=0, keepdims=True).astype(jnp.int32)
        ri_ref[...] = jnp.where(eidx == 0, i1, jnp.where(eidx == 1, i2,
                                jnp.where(eidx == 2, r1, jnp.where(eidx == 3, r2, 0))))
        cnt_acc[...] += jnp.sum(onehot.astype(F32), axis=1, keepdims=True)
        cnt_ref[...] = cnt_acc[...]


def _mixout(x2, attn, pool, ys, w_glu, b_glu, g_attn, g_pool, g_ssm, w_out, g_post, g_pre,
            router=None, *, tm=512):
    n, d = x2.shape
    moe = router is not None
    row = lambda w: pl.BlockSpec((tm, w), lambda i: (i, 0))
    full = lambda arr: pl.BlockSpec(arr.shape, lambda i: (0, 0))
    ins = [x2, attn, pool, ys, w_glu, b_glu, g_attn, g_pool, g_ssm, w_out, g_post, g_pre]
    in_specs = [row(d), row(attn.shape[1]), row(pool.shape[1]), row(ys.shape[1])] + [full(t) for t in ins[4:]]
    if moe:
        assert d == SUBLANES * LANES
        h_shape = jax.ShapeDtypeStruct((n * SUBLANES, LANES), F32)
        h_spec = pl.BlockSpec((tm * SUBLANES, LANES), lambda i: (i, 0))
    else:
        h_shape, h_spec = jax.ShapeDtypeStruct((n, d), BF16), row(d)
    out_shape = [jax.ShapeDtypeStruct((n, d), F32), h_shape]
    out_specs = [row(d), h_spec]
    scratch = []
    if moe:
        ins.append(router)
        in_specs.append(full(router))
        ne = router.shape[0]
        col = pl.BlockSpec((ne, tm), lambda i: (0, i))
        out_shape += [jax.ShapeDtypeStruct((ne, n), jnp.int32), jax.ShapeDtypeStruct((ne, n), F32),
                      jax.ShapeDtypeStruct((ne, LANES), F32)]
        out_specs += [col, col, pl.BlockSpec((ne, LANES), lambda i: (0, 0))]
        scratch = [pltpu.VMEM((ne, LANES), F32)]
    return pl.pallas_call(
        functools.partial(_mixout_kernel, moe=moe),
        grid=(n // tm,),
        in_specs=in_specs,
        out_specs=tuple(out_specs),
        out_shape=tuple(out_shape),
        scratch_shapes=scratch,
        compiler_params=_params(("arbitrary",)),
        name="mix_out",
    )(*ins)


FFN_STAGE_CHUNKS = 4


def _stage_cast(src_hbm, dst, lane0, stage, sem):
    rows = stage.shape[1]
    width = src_hbm.shape[1]

    def copy(c):
        return pltpu.make_async_copy(src_hbm.at[pl.ds(c * rows, rows)], stage.at[c % 2], sem.at[c % 2])

    nchunks = src_hbm.shape[0] // rows
    copy(0).start()
    for c in range(nchunks):
        if c + 1 < nchunks:
            copy(c + 1).start()
        copy(c).wait()
        dst[pl.ds(c * rows, rows), lane0:lane0 + width] = stage[c % 2].astype(BF16)


def _ffn_kernel(h_ref, x_ref, wg_hbm, wu_hbm, wd_hbm, g_ref, o_ref, wgu_s, wd_s, stage_in, stage_out, sem, *, sub):
    fdim = wd_s.shape[0]

    @pl.when(pl.program_id(0) == 0)
    def _():
        _stage_cast(wg_hbm, wgu_s, 0, stage_in, sem)
        _stage_cast(wu_hbm, wgu_s, fdim, stage_in, sem)
        _stage_cast(wd_hbm, wd_s, 0, stage_out, sem)

    for r in range(h_ref.shape[0] // sub):
        rows = pl.ds(r * sub, sub)
        hgu = jnp.dot(h_ref[rows, :], wgu_s[...], preferred_element_type=F32)
        hg, hu = hgu[:, :fdim], hgu[:, fdim:]
        act = (hg * jax.nn.sigmoid(hg) * hu).astype(BF16)
        y = jnp.dot(act, wd_s[...], preferred_element_type=F32)
        o_ref[rows, :] = x_ref[rows, :] + _rms(y, g_ref[...])


def _ffn(h, x2, wg, wu, wd, g_post, *, tm=512, sub=256):
    n, d = x2.shape
    fdim = wg.shape[1]
    row = pl.BlockSpec((tm, d), lambda i: (i, 0))
    hbm = pl.BlockSpec(memory_space=pl.ANY)
    return pl.pallas_call(
        functools.partial(_ffn_kernel, sub=sub),
        grid=(n // tm,),
        in_specs=[row, row, hbm, hbm, hbm, pl.BlockSpec((1, d), lambda i: (0, 0))],
        out_specs=row,
        out_shape=jax.ShapeDtypeStruct((n, d), F32),
        scratch_shapes=[pltpu.VMEM((d, 2 * fdim), BF16), pltpu.VMEM((fdim, d), BF16),
                        pltpu.VMEM((2, d // FFN_STAGE_CHUNKS, fdim), F32),
                        pltpu.VMEM((2, fdim // FFN_STAGE_CHUNKS, d), F32),
                        pltpu.SemaphoreType.DMA((2,))],
        compiler_params=_params(("arbitrary",)),
        name="dense_ffn",
    )(h, x2, wg, wu, wd, g_post)


MOE_TILE = 512


def _moe_tables(counts, n_items):
    tm = MOE_TILE
    i32 = jnp.int32
    ne = counts.shape[0]
    assert n_items <= LANES
    counts = counts.astype(i32)
    ntiles = (counts + tm - 1) // tm
    k = jnp.arange(ne, dtype=i32)
    ends = jnp.sum(jnp.where(k[None, :] <= k[:, None], ntiles[None, :], 0), axis=1)
    starts = ends - ntiles
    present = ntiles > 0
    later = present[None, :] & (k[None, :] > k[:, None])
    nxt = jnp.min(jnp.where(later, k[None, :], ne), axis=1)
    nxt = jnp.where(nxt < ne, nxt, -1)
    order = jnp.sum((present[None, :] & (k[None, :] < k[:, None])).astype(i32), axis=1)
    w = jnp.arange(LANES, dtype=i32)[None, :]
    wc = jnp.minimum(w, ends[-1] - 1)
    e = jnp.sum((wc >= ends[:, None]).astype(i32), axis=0, keepdims=True)
    of_item = lambda v: jnp.sum(jnp.where(k[:, None] == e, v[:, None], 0), axis=0, keepdims=True)
    valid = w < ends[-1]
    rowblock = jnp.where(valid, w, n_items)
    nvalid = jnp.where(valid, jnp.clip(of_item(counts) - (wc - of_item(starts)) * tm, 0, tm), 0)
    e_prev = jnp.concatenate([e[:, :1], e[:, :-1]], axis=1)
    first = ((w == 0) | (e != e_prev)).astype(i32)
    item = lambda v: v[0, :n_items].astype(i32)
    return ((item(e), item(rowblock), item(nvalid), item(first)),
            (item(of_item(nxt)), item(of_item(order)), jnp.sum(present.astype(i32)).reshape(1)), starts * tm)


def _slab(ref, token):
    return ref.at[pl.ds(pl.multiple_of(token * SUBLANES, SUBLANES), SUBLANES)]


def _dispatch_kernel(dest_ref, h_ref, xs_ref, sem, *, n):
    tm = h_ref.shape[0] // SUBLANES
    base = pl.program_id(0) * tm

    def body(r, carry):
        for k in range(2):
            d = dest_ref[k * n + base + r]
            pltpu.make_async_copy(_slab(h_ref, r), _slab(xs_ref, d), sem).start(priority=k)
        return carry

    lax.fori_loop(0, tm, body, 0, unroll=8)
    for k in range(2):
        pltpu.make_async_copy(h_ref, h_ref, sem).wait()


def _dispatch(dest, h, cap_rows, *, tm=512):
    n = h.shape[0] // SUBLANES
    return pl.pallas_call(
        functools.partial(_dispatch_kernel, n=n),
        grid_spec=pltpu.PrefetchScalarGridSpec(
            num_scalar_prefetch=1,
            grid=(n // tm,),
            in_specs=[pl.BlockSpec((tm * SUBLANES, LANES), lambda i, dest: (i, 0))],
            out_specs=pl.BlockSpec(memory_space=pl.ANY),
            scratch_shapes=[pltpu.SemaphoreType.DMA(())]),
        out_shape=jax.ShapeDtypeStruct((cap_rows * SUBLANES, LANES), F32),
        compiler_params=_params(("arbitrary",)),
        name="moe_dispatch",
    )(dest, h)


MOE_SUB = 256


MOE_UP_CHUNKS = 4


def _moe_up_kernel(ie_ref, rb_ref, nv_ref, first_ref, nxt_ref, ord_ref, npres_ref, xs_ref, wg_hbm, wu_hbm, h_ref,
                   wgu_s, stage, sem):
    f = pl.program_id(0)
    w = pl.program_id(1)
    nf = pl.num_programs(0)
    tf = h_ref.shape[1]
    rows = stage.shape[0] // MOE_UP_CHUNKS

    def copies(e, fp):
        return [pltpu.make_async_copy(src.at[e, pl.ds(c * rows, rows), pl.ds(fp * tf, tf)],
                                      stage.at[pl.ds(c * rows, rows), pl.ds(m * tf, tf)],
                                      sem.at[m * MOE_UP_CHUNKS + c])
                for m, src in enumerate((wg_hbm, wu_hbm)) for c in range(MOE_UP_CHUNKS)]

    @pl.when((f == 0) & (w == 0))
    def _():
        for cp in copies(ie_ref[0], 0):
            cp.start()

    slot = (f * npres_ref[0] + ord_ref[w]) % 2
    is_first = first_ref[w] == 1
    last_expert = nxt_ref[w] < 0
    for fp in range(nf):
        @pl.when(is_first & (f == fp))
        def _():
            for cp in copies(ie_ref[w], fp):
                cp.wait()
            for c in range(MOE_UP_CHUNKS):
                r = pl.ds(c * rows, rows)
                wgu_s[slot, r, :] = stage[r, :].astype(BF16)

            @pl.when(jnp.logical_not(last_expert))
            def _():
                for cp in copies(nxt_ref[w], fp):
                    cp.start()

            if fp + 1 < nf:
                @pl.when(last_expert)
                def _():
                    for cp in copies(ie_ref[0], fp + 1):
                        cp.start()

    nv = nv_ref[w]

    @pl.when(nv > 0)
    def _():
        wgu_view = wgu_s.at[slot]
        for r in range(h_ref.shape[0] // MOE_SUB):
            rws = pl.ds(r * MOE_SUB, MOE_SUB)
            idx = r * MOE_SUB + lax.broadcasted_iota(jnp.int32, (MOE_SUB, 1), 0)
            x = _load_slabs(xs_ref, r * MOE_SUB, MOE_SUB)
            x = jnp.where(idx < nv, x, 0.0).astype(BF16)
            hgu = jnp.dot(x, wgu_view[...], preferred_element_type=F32)
            hg, hu = hgu[:, :tf], hgu[:, tf:]
            h_ref[rws, :] = (hg * jax.nn.sigmoid(hg) * hu).astype(BF16)

    @pl.when(nv == 0)
    def _():
        h_ref[...] = jnp.zeros_like(h_ref)


def _moe_up(tables, order, xs, wg, wu, *, tf=1792):
    tm = MOE_TILE
    cap_rows = xs.shape[0] // SUBLANES
    ne, d, fdim = wg.shape
    n_items = tables[0].shape[0]
    hbm = pl.BlockSpec(memory_space=pl.ANY)
    return pl.pallas_call(
        _moe_up_kernel,
        grid_spec=pltpu.PrefetchScalarGridSpec(
            num_scalar_prefetch=7,
            grid=(fdim // tf, n_items),
            in_specs=[pl.BlockSpec((tm * SUBLANES, LANES), lambda f, w, ie, rb, *_: (rb[w], 0)), hbm, hbm],
            out_specs=pl.BlockSpec((tm, tf), lambda f, w, ie, rb, *_: (rb[w], f)),
            scratch_shapes=[pltpu.VMEM((2, d, 2 * tf), BF16), pltpu.VMEM((d, 2 * tf), F32),
                            pltpu.SemaphoreType.DMA((2 * MOE_UP_CHUNKS,))]),
        out_shape=jax.ShapeDtypeStruct((cap_rows, fdim), BF16),
        compiler_params=_params(("arbitrary", "arbitrary")),
        name="moe_up",
    )(*tables, *order, xs, wg, wu)


def _moe_down_kernel(ie_ref, rb_ref, nv_ref, first_ref, nxt_ref, ord_ref, npres_ref, h_ref, wd_hbm, y_ref,
                     wd_s, stage, sem):
    w = pl.program_id(0)
    slot = ord_ref[w] % 2
    nchunks = sem.shape[0]
    chunk = stage.shape[0] // nchunks

    def copies(e):
        return [pltpu.make_async_copy(wd_hbm.at[e, pl.ds(c * chunk, chunk)], stage.at[pl.ds(c * chunk, chunk)],
                                      sem.at[c]) for c in range(nchunks)]

    @pl.when(w == 0)
    def _():
        for cp in copies(ie_ref[0]):
            cp.start()

    @pl.when(first_ref[w] == 1)
    def _():
        for c, cp in enumerate(copies(ie_ref[w])):
            cp.wait()
            wd_s[slot, pl.ds(c * chunk, chunk), :] = stage[pl.ds(c * chunk, chunk), :].astype(BF16)

        @pl.when(nxt_ref[w] >= 0)
        def _():
            for cp in copies(nxt_ref[w]):
                cp.start()

    @pl.when(nv_ref[w] > 0)
    def _():
        wd = wd_s[slot]
        for r in range(h_ref.shape[0] // MOE_SUB):
            y = jnp.dot(h_ref[pl.ds(r * MOE_SUB, MOE_SUB), :], wd, preferred_element_type=F32)
            _store_slabs(y_ref, r * MOE_SUB, y)

    @pl.when(nv_ref[w] == 0)
    def _():
        y_ref[...] = jnp.zeros_like(y_ref)


def _moe_down(tables, order, hid, wd, *, chunk=512):
    tm = MOE_TILE
    cap_rows, fdim = hid.shape
    d = wd.shape[2]
    n_items = tables[0].shape[0]
    assert d == SUBLANES * LANES and fdim % chunk == 0
    return pl.pallas_call(
        _moe_down_kernel,
        grid_spec=pltpu.PrefetchScalarGridSpec(
            num_scalar_prefetch=7,
            grid=(n_items,),
            in_specs=[pl.BlockSpec((tm, fdim), lambda w, ie, rb, *_: (rb[w], 0)),
                      pl.BlockSpec(memory_space=pl.ANY)],
            out_specs=pl.BlockSpec((tm * SUBLANES, LANES), lambda w, ie, rb, *_: (rb[w], 0)),
            scratch_shapes=[pltpu.VMEM((2, fdim, d), BF16), pltpu.VMEM((fdim, d), F32),
                            pltpu.SemaphoreType.DMA((fdim // chunk,))]),
        out_shape=jax.ShapeDtypeStruct((cap_rows * SUBLANES, LANES), F32),
        compiler_params=_params(("arbitrary",)),
        name="moe_down",
    )(*tables, *order, hid, wd)


def _combine_kernel(dest_ref, x_ref, gate_ref, g_ref, ys_ref, o_ref, buf, sem, *, n):
    tm = x_ref.shape[0]
    i = pl.program_id(0)

    def issue(tile, slot):
        def body(r, carry):
            for k in range(2):
                d = dest_ref[k * n + tile * tm + r]
                pltpu.make_async_copy(_slab(ys_ref, d), _slab(buf.at[slot, k], r),
                                      sem.at[slot]).start(priority=k)
            return carry
        lax.fori_loop(0, tm, body, 0, unroll=8)

    @pl.when(i == 0)
    def _():
        issue(0, 0)

    @pl.when(i + 1 < pl.num_programs(0))
    def _():
        issue(i + 1, (i + 1) % 2)

    slot = i % 2
    for k in range(2):
        pltpu.make_async_copy(buf.at[slot, k], buf.at[slot, k], sem.at[slot]).wait()
    gates = gate_ref[...].T
    f = (gates[:, 0:1] * _load_slabs(buf.at[slot, 0], 0, tm)
         + gates[:, 1:2] * _load_slabs(buf.at[slot, 1], 0, tm))
    o_ref[...] = x_ref[...] + _rms(f, g_ref[...])


def _combine(dest, x2, gates, g_post, ys, *, tm=256):
    n, d = x2.shape
    return pl.pallas_call(
        functools.partial(_combine_kernel, n=n),
        grid_spec=pltpu.PrefetchScalarGridSpec(
            num_scalar_prefetch=1,
            grid=(n // tm,),
            in_specs=[pl.BlockSpec((tm, d), lambda i, dest: (i, 0)),
                      pl.BlockSpec((gates.shape[0], tm), lambda i, dest: (0, i)),
                      pl.BlockSpec((1, d), lambda i, dest: (0, 0)),
                      pl.BlockSpec(memory_space=pl.ANY)],
            out_specs=pl.BlockSpec((tm, d), lambda i, dest: (i, 0)),
            scratch_shapes=[pltpu.VMEM((2, 2, tm * SUBLANES, LANES), F32), pltpu.SemaphoreType.DMA((2,))]),
        out_shape=jax.ShapeDtypeStruct((n, d), F32),
        compiler_params=_params(("arbitrary",)),
        name="moe_combine",
    )(dest, x2, gates, g_post, ys)


def _moe(h, x2, route_i, route_g, counts, wg, wu, wd, g_post):
    n = x2.shape[0]
    n_items = 2 * n // MOE_TILE + N_EXPERTS
    tables, order, base = _moe_tables(counts[:, 0], n_items)
    base_of = lambda e: jnp.sum(jnp.where(e[None, :] == jnp.arange(N_EXPERTS)[:, None], base[:, None], 0), axis=0)
    dest = jnp.concatenate([base_of(route_i[0]) + route_i[2], base_of(route_i[1]) + route_i[3]])
    xs = _dispatch(dest, h, (n_items + 1) * MOE_TILE)
    hid = _moe_up(tables, order, xs, wg, wu)
    ys = _moe_down(tables, order, hid, wd)
    return _combine(dest, x2, route_g, g_post, ys)


def _row(v):
    return v.reshape(1, -1).astype(F32)


def kernel(x, norm_mix_pre, norm_mix_post, norm_ffn_pre, norm_ffn_post, w_in, b_forget, pool_w, pool_scale, ssm_a_re, ssm_a_im, ssm_b_re, ssm_b_im, ssm_c_re, ssm_c_im, ssm_d, ssm_log_dt, ssm_w_glu, ssm_b_glu, branch_norm_attn, branch_norm_pool, branch_norm_ssm, w_out, ffn_w_gate, ffn_w_up, ffn_w_down, moe_router, moe_w_gate, moe_w_up, moe_w_down):
    b, L, d = x.shape
    depth = w_in.shape[0]
    n = b * L
    a = ATTN_WIDTH
    attn_tile = 512
    x2 = x.reshape(n, d)
    for i in range(depth):
        b_f = jnp.pad(b_forget[i], (0, LANES - ATTN_HEADS)).reshape(1, LANES)
        q, k, v, up, us, c = _inproj(x2, _row(norm_mix_pre[i]), w_in, b_f, layer=i, seq=L)

        ct = c.reshape(b, L, ATTN_HEADS).transpose(0, 2, 1).reshape(b, ATTN_HEADS * (L // attn_tile), attn_tile)
        attn = _attention(q.reshape(b, L, a), k.reshape(b, L, a), v.reshape(b, L, a),
                          ct, tq=attn_tile).reshape(n, a)

        pool = _pool(up.reshape(b, L, POOL_WIDTH),
                     jax.scipy.linalg.block_diag(*pool_w[i]).astype(BF16),
                     _row(pool_scale[i])).reshape(n, POOL_WIDTH)

        mats = _ssm_matrices(ssm_a_re[i], ssm_a_im[i], ssm_b_re[i], ssm_b_im[i],
                             ssm_c_re[i], ssm_c_im[i], ssm_d[i], ssm_log_dt[i])
        ys = _ssm_core(us, *mats, seq=L)

        moe = i % 2 == 1
        j = i // 2
        router = moe_router[j].T if moe else None
        res = _mixout(x2, attn, pool, ys, ssm_w_glu[i].astype(BF16), _row(ssm_b_glu[i]),
                      _row(branch_norm_attn[i]), _row(branch_norm_pool[i]), _row(branch_norm_ssm[i]),
                      w_out[i].astype(BF16), _row(norm_mix_post[i]), _row(norm_ffn_pre[i]), router)
        if moe:
            x2, h, route_i, route_g, counts = res
            x2 = _moe(h, x2, route_i, route_g, counts, moe_w_gate[j], moe_w_up[j], moe_w_down[j],
                      _row(norm_ffn_post[i]))
        else:
            x2, h = res
            x2 = _ffn(h, x2, ffn_w_gate[j], ffn_w_up[j], ffn_w_down[j], _row(norm_ffn_post[i]))
    return x2.reshape(b, L, d)
```

```python
import functools
import math

import jax
import jax.numpy as jnp
from jax import lax
from jax.experimental import pallas as pl
from jax.experimental.pallas import tpu as pltpu

F32 = jnp.float32
BF16 = jnp.bfloat16

RMS_EPS = 1e-6
NEG_INF = -1e30
LOG2E = 1.4426950408889634

ATTN_HEADS = 8
ATTN_HEAD_DIM = 64
ATTN_WIDTH = ATTN_HEADS * ATTN_HEAD_DIM
POOL_WINDOWS = (2, 4, 8, 16)
POOL_GROUP_DIM = 64
POOL_WIDTH = len(POOL_WINDOWS) * POOL_GROUP_DIM
SSM_GROUPS = 16
SSM_GROUP_DIM = 16
SSM_STATE = 64
SSM_WIDTH = SSM_GROUPS * SSM_GROUP_DIM
SSM_CHUNK = 8
N_EXPERTS = 8

LANES = 128
SUBLANES = 8
VMEM_LIMIT = 56 * 1024 * 1024


def _params(sem):
    return pltpu.CompilerParams(dimension_semantics=sem, vmem_limit_bytes=VMEM_LIMIT)


def _rms(x, g):
    return x * lax.rsqrt(jnp.mean(x * x, axis=-1, keepdims=True) + RMS_EPS) * g


def _load_slabs(ref, row0, rows):
    return jnp.concatenate([ref[pl.ds(row0 * SUBLANES + s, rows, stride=SUBLANES), :]
                            for s in range(SUBLANES)], axis=1)


def _store_slabs(ref, row0, val):
    for s in range(SUBLANES):
        ref[pl.ds(row0 * SUBLANES + s, val.shape[0], stride=SUBLANES), :] = val[:, s * LANES:(s + 1) * LANES]


def _split3(x):
    hi = x.astype(BF16)
    r = x - hi.astype(F32)
    mid = r.astype(BF16)
    lo = (r - mid.astype(F32)).astype(BF16)
    return hi, mid, lo


INPROJ_STAGE_ROWS = 256
INPROJ_SUB = 256


def _inproj_kernel(x_ref, g_ref, w_hbm, bf_ref, q_ref, k_ref, v_ref, up_ref, us_ref, c_ref,
                   w_s, stage, sem, carry_ref, *, layer, tiles_per_seq):
    a = ATTN_WIDTH
    f_off = 3 * a
    p_off = f_off + ATTN_HEADS
    main = f_off + POOL_WIDTH + SSM_WIDTH

    @pl.when(pl.program_id(0) == 0)
    def _():
        rows = stage.shape[1]

        def copy(c):
            return pltpu.make_async_copy(w_hbm.at[layer, pl.ds(c * rows, rows)], stage.at[c % 2], sem.at[c % 2])

        nchunks = w_s.shape[0] // rows
        copy(0).start()
        for c in range(nchunks):
            if c + 1 < nchunks:
                copy(c + 1).start()
            copy(c).wait()
            blk = stage[c % 2]
            r = pl.ds(c * rows, rows)
            w_s[r, 0:f_off] = blk[:, 0:f_off].astype(BF16)
            w_s[r, f_off:main] = blk[:, p_off:p_off + main - f_off].astype(BF16)
            lane = lax.broadcasted_iota(jnp.int32, (rows, LANES), 1)
            w_s[r, main:] = jnp.where(lane < ATTN_HEADS, blk[:, f_off:f_off + LANES], 0.0).astype(BF16)

    @pl.when(pl.program_id(0) % tiles_per_seq == 0)
    def _():
        carry_ref[...] = jnp.zeros_like(carry_ref)

    tm = x_ref.shape[0]
    zs = []
    for s in range(tm // INPROJ_SUB):
        r = pl.ds(s * INPROJ_SUB, INPROJ_SUB)
        h = _rms(x_ref[r, :], g_ref[...]).astype(BF16)
        proj = jnp.dot(h, w_s[...], preferred_element_type=F32)
        q_ref[r, :] = (proj[:, 0:a] * (LOG2E * ATTN_HEAD_DIM ** -0.5)).astype(BF16)
        k_ref[r, :] = proj[:, a:2 * a].astype(BF16)
        v_ref[r, :] = proj[:, 2 * a:3 * a].astype(BF16)
        up_ref[r, :] = proj[:, 3 * a:3 * a + POOL_WIDTH].astype(BF16)
        us_ref[r, :] = proj[:, 3 * a + POOL_WIDTH:main]
        zs.append(proj[:, main:])

    z = jnp.concatenate(zs, axis=0) + bf_ref[...]
    logf = jnp.minimum(z, 0.0) - jnp.log(1.0 + jnp.exp(-jnp.abs(z)))
    row = lax.broadcasted_iota(jnp.int32, (tm, tm), 0)
    col = lax.broadcasted_iota(jnp.int32, (tm, tm), 1)
    tri = (row >= col).astype(BF16)
    hi, mid, _ = _split3(logf)
    c = (jnp.dot(tri, hi, preferred_element_type=F32)
         + jnp.dot(tri, mid, preferred_element_type=F32)) + carry_ref[...]
    c_ref[...] = c[:, :ATTN_HEADS]
    carry_ref[...] = c[tm - 1:tm, :]


def _inproj(x2, g, w_in, b_f, *, layer, seq, tm=512):
    n, d = x2.shape
    n_in = w_in.shape[2]
    a = ATTN_WIDTH
    n_res = n_in - ATTN_HEADS + LANES
    out_shape = (
        jax.ShapeDtypeStruct((n, a), BF16), jax.ShapeDtypeStruct((n, a), BF16),
        jax.ShapeDtypeStruct((n, a), BF16), jax.ShapeDtypeStruct((n, POOL_WIDTH), BF16),
        jax.ShapeDtypeStruct((n, SSM_WIDTH), F32), jax.ShapeDtypeStruct((n, ATTN_HEADS), F32))
    row = lambda w: pl.BlockSpec((tm, w), lambda i: (i, 0))
    full = lambda r, c: pl.BlockSpec((r, c), lambda i: (0, 0))
    return pl.pallas_call(
        functools.partial(_inproj_kernel, layer=layer, tiles_per_seq=seq // tm),
        grid=(n // tm,),
        in_specs=[row(d), full(1, d), pl.BlockSpec(memory_space=pl.ANY), full(1, LANES)],
        out_specs=(row(a), row(a), row(a), row(POOL_WIDTH), row(SSM_WIDTH), row(ATTN_HEADS)),
        out_shape=out_shape,
        scratch_shapes=[pltpu.VMEM((d, n_res), BF16), pltpu.VMEM((2, INPROJ_STAGE_ROWS, n_in), F32),
                        pltpu.SemaphoreType.DMA((2,)), pltpu.VMEM((1, LANES), F32)],
        compiler_params=_params(("arbitrary",)),
        name="inproj",
    )(x2, g, w_in, b_f)


HEADS_PER_BLOCK = LANES // ATTN_HEAD_DIM


def _attn_kernel(q_ref, k_ref, v_ref, ct_ref, o_ref, *, tq, tk):
    nlb = q_ref.shape[2] // LANES
    hp = pl.program_id(1)
    qi = pl.program_id(2)
    nkb = k_ref.shape[1] // tk
    lane = lax.broadcasted_iota(jnp.int32, (1, LANES), 1)
    in_head = [(lane // ATTN_HEAD_DIM) == hh for hh in range(HEADS_PER_BLOCK)]
    chains = [(lb, hh) for lb in range(nlb) for hh in range(HEADS_PER_BLOCK)]
    qm, crow, cref = [], [], []
    for lb in range(nlb):
        q2 = q_ref[0, :, lb * LANES:(lb + 1) * LANES]
        qm.append(jnp.concatenate([jnp.where(msk, q2, jnp.zeros_like(q2)) for msk in in_head], axis=0))
    for lb, hh in chains:
        crow.append(((hp * nlb + lb) * HEADS_PER_BLOCK + hh) * nkb)
        cref.append(ct_ref[0, pl.ds(crow[-1] + qi, 1), :][:, 0:1])

    def step(kb, carry, masked):
        new = []
        scores = {}
        for ci, (lb, hh) in enumerate(chains):
            kblk = k_ref[0, pl.ds(kb * tk, tk), lb * LANES:(lb + 1) * LANES]
            vblk = v_ref[0, pl.ds(kb * tk, tk), lb * LANES:(lb + 1) * LANES]
            m, acc = carry[ci]
            if lb not in scores:
                scores[lb] = lax.dot_general(qm[lb], kblk, (((1,), (1,)), ((), ())), preferred_element_type=F32)
            s = scores[lb][hh * tq:(hh + 1) * tq, :]
            s = s - (ct_ref[0, pl.ds(crow[ci] + kb, 1), :] - cref[ci]) * LOG2E
            if masked:
                r = lax.broadcasted_iota(jnp.int32, (tq, tk), 0)
                cidx = lax.broadcasted_iota(jnp.int32, (tq, tk), 1)
                s = jnp.where(cidx <= r, s, NEG_INF)
            m_new = jnp.maximum(m, jnp.max(s, axis=1, keepdims=True))
            p = jnp.exp2(s - m_new)
            vsel = jnp.where(in_head[hh], vblk, jnp.ones_like(vblk))
            acc = jnp.exp2(m - m_new) * acc + jnp.dot(p.astype(BF16), vsel, preferred_element_type=F32)
            new.append((m_new, acc))
        return tuple(new)

    init = tuple((jnp.full((tq, 1), NEG_INF, F32), jnp.zeros((tq, LANES), F32)) for _ in chains)
    carry = lax.fori_loop(0, qi, lambda kb, cr: step(kb, cr, False), init)
    carry = step(qi, carry, True)
    for lb in range(nlb):
        out = jnp.zeros((tq, LANES), F32)
        for hh in range(HEADS_PER_BLOCK):
            acc = carry[lb * HEADS_PER_BLOCK + hh][1]
            out = jnp.where(in_head[hh], acc / pltpu.roll(acc, ATTN_HEAD_DIM, 1), out)
        o_ref[0, :, lb * LANES:(lb + 1) * LANES] = out.astype(o_ref.dtype)


def _attention(q, k, v, ct, *, tq, lane_blocks=4):
    b, L, a = q.shape
    tk = ct.shape[2]
    assert tq == tk, "the diagonal key block of query tile i must be key block i"
    w = lane_blocks * LANES
    blk = lambda: pl.BlockSpec((1, tq, w), lambda bi, hi, qi: (bi, qi, hi))
    seq = lambda: pl.BlockSpec((1, L, w), lambda bi, hi, qi: (bi, 0, hi))
    return pl.pallas_call(
        functools.partial(_attn_kernel, tq=tq, tk=tk),
        grid=(b, a // w, L // tq),
        in_specs=[blk(), seq(), seq(),
                  pl.BlockSpec((1, ct.shape[1], tk), lambda bi, hi, qi: (bi, 0, 0))],
        out_specs=blk(),
        out_shape=jax.ShapeDtypeStruct((b, L, a), BF16),
        compiler_params=_params(("arbitrary", "arbitrary", "arbitrary")),
        name="fox_attention",
    )(q, k, v, ct)


def _pool_kernel(u_ref, w_ref, s_ref, o_ref):
    x = u_ref[0].astype(F32)
    L, w = x.shape
    row = lax.broadcasted_iota(jnp.int32, (L, w), 0)
    group = lax.broadcasted_iota(jnp.int32, (L, w), 1) // POOL_GROUP_DIM

    def shifted(y, s):
        return jnp.where(row >= s, pltpu.roll(y, s, 0), 0.0)

    acc = x
    sel = jnp.zeros_like(x)
    win_lane = jnp.zeros((L, w), F32)
    span = 1
    for gi, win in enumerate(POOL_WINDOWS):
        while span < win:
            acc = acc + shifted(acc, span)
            span *= 2
        sel = jnp.where(group == gi, acc, sel)
        win_lane = jnp.where(group == gi, float(win), win_lane)
    count = jnp.minimum(row.astype(F32) + 1.0, win_lane)
    d = sel / count - x
    y = jnp.dot(d.astype(BF16), w_ref[...], preferred_element_type=F32) * s_ref[...]
    o_ref[0] = y.astype(o_ref.dtype)


def _pool(u, w_bd, scale):
    b, L, w = u.shape
    return pl.pallas_call(
        _pool_kernel,
        grid=(b,),
        in_specs=[pl.BlockSpec((1, L, w), lambda i: (i, 0, 0)),
                  pl.BlockSpec((w, w), lambda i: (0, 0)),
                  pl.BlockSpec((1, w), lambda i: (0, 0))],
        out_specs=pl.BlockSpec((1, L, w), lambda i: (i, 0, 0)),
        out_shape=jax.ShapeDtypeStruct((b, L, w), BF16),
        compiler_params=_params(("arbitrary",)),
        name="pool_mixer",
    )(u, w_bd, scale)


def _group_of(idx, width):
    groups = LANES // SSM_GROUP_DIM
    return (idx >> (width.bit_length() - 1)) & (groups - 1)


def _expand_blockdiag(r_ref, row_inner, col_inner):
    rows, w = r_ref.shape[1:]
    groups = LANES // SSM_GROUP_DIM
    r = lax.broadcasted_iota(jnp.int32, (w, groups * w), 0)
    c = lax.broadcasted_iota(jnp.int32, (w, groups * w), 1)
    shift = col_inner.bit_length() - 1
    outer_c = c >> (shift + groups.bit_length() - 1)
    spread = (outer_c == (r >> shift)) & ((c & (col_inner - 1)) == (r & (col_inner - 1)))
    big = jnp.dot(r_ref[0], spread.astype(BF16), preferred_element_type=F32)
    ri = lax.broadcasted_iota(jnp.int32, (rows, groups * w), 0)
    ci = lax.broadcasted_iota(jnp.int32, (rows, groups * w), 1)
    return jnp.where(_group_of(ri, row_inner) == _group_of(ci, col_inner), big, 0.0).astype(BF16)


def _ssm_kernel(u_ref, rm_ref, rz_ref, ry_ref, coef_ref, y_ref, m_s, wz_s, wy_s, ucat, z_ref, s_ref, *, seqs):
    t = SSM_CHUNK

    @pl.when(pl.program_id(1) == 0)
    def _():
        m_s[...] = _expand_blockdiag(rm_ref, SSM_GROUP_DIM, SSM_GROUP_DIM)
        wz_s[...] = _expand_blockdiag(rz_ref, SSM_GROUP_DIM, SSM_STATE)
        wy_s[...] = _expand_blockdiag(ry_ref, SSM_STATE, SSM_GROUP_DIM)

    nc = u_ref.shape[0] // t
    cps = nc // seqs
    for k in range(t):
        ucat[:, k * LANES:(k + 1) * LANES] = u_ref[pl.ds(k, nc, stride=t), :].astype(BF16)
    u = ucat[...]
    z_ref[...] = jnp.dot(u, wz_s[...], preferred_element_type=F32)
    half = z_ref.shape[1] // 2
    a1 = coef_ref[0, 0:1, :]
    a2 = coef_ref[0, 1:2, :]

    def step(c, states):
        new = []
        for b in range(seqs):
            st = states[b]
            r = b * cps + c
            s_ref[pl.ds(r, 1), :] = st
            sw = jnp.concatenate([st[:, half:], st[:, :half]], axis=1)
            new.append(a1 * st + a2 * sw + z_ref[pl.ds(r, 1), :])
        return tuple(new)

    init = tuple(jnp.zeros((1, 2 * half), F32) for _ in range(seqs))
    lax.fori_loop(0, cps, step, init, unroll=8)
    y = jnp.dot(u, m_s[...], preferred_element_type=F32)
    y = y + jnp.dot(s_ref[...].astype(BF16), wy_s[...], preferred_element_type=F32)
    for k in range(t):
        y_ref[pl.ds(k, nc, stride=t), :] = y[:, k * LANES:(k + 1) * LANES]


def _ssm_core(u, rm, rz, ry, coef, *, seq, max_seqs_per_block=4):
    n, w = u.shape
    seqs_per_block = math.gcd(max_seqs_per_block, n // seq)
    rows = seq * seqs_per_block
    nc = rows // SSM_CHUNK
    kw = SSM_CHUNK * LANES
    p2 = coef.shape[2]
    blk = pl.BlockSpec((rows, LANES), lambda hf, r: (r, hf))
    wspec = lambda arr: pl.BlockSpec((1,) + arr.shape[1:], lambda hf, r: (hf, 0, 0))
    return pl.pallas_call(
        functools.partial(_ssm_kernel, seqs=seqs_per_block),
        grid=(w // LANES, n // rows),
        in_specs=[blk, wspec(rm), wspec(rz), wspec(ry), wspec(coef)],
        out_specs=blk,
        out_shape=jax.ShapeDtypeStruct((n, w), F32),
        scratch_shapes=[pltpu.VMEM((kw, kw), BF16), pltpu.VMEM((kw, p2), BF16), pltpu.VMEM((p2, kw), BF16),
                        pltpu.VMEM((nc, kw), BF16), pltpu.VMEM((nc, p2), F32), pltpu.VMEM((nc, p2), F32)],
        compiler_params=_params(("arbitrary", "arbitrary")),
        name="s5_core",
    )(u, rm, rz, ry, coef)


def _ssm_matrices(a_re, a_im, b_re, b_im, c_re, c_im, d, log_dt):
    t = SSM_CHUNK
    g, p = a_re.shape
    hdim = d.shape[1]
    gl = LANES // hdim
    nb = g // gl
    lam = lax.complex(a_re, a_im)
    lam_dt = lam * jnp.exp(log_dt)[:, None]
    lam_bar = jnp.exp(lam_dt)
    b_bar = ((lam_bar - 1.0) / lam)[:, :, None] * lax.complex(b_re, b_im)
    cmat = lax.complex(c_re, c_im)
    steps = jnp.arange(t + 1, dtype=F32)
    pw = jnp.exp(lam_dt[:, None, :] * steps[None, :, None])
    kern = jnp.einsum('gop,gdp,gpi->gdoi', cmat, pw[:, :t], b_bar).real
    kern = kern.at[:, 0].add(jax.vmap(jnp.diag)(d))
    j = jnp.arange(t)[:, None]
    i = jnp.arange(t)[None, :]
    m = jnp.where((i >= j)[None, :, :, None, None], kern[:, jnp.clip(i - j, 0, t - 1)], 0.0)
    wz = jnp.einsum('gjp,gpi->gjip', pw[:, t - 1 - jnp.arange(t)], b_bar)
    cw = jnp.einsum('gop,gip->giop', cmat, pw[:, 1:t + 1])
    rm = m.reshape(nb, gl, t, t, hdim, hdim).transpose(0, 2, 1, 5, 3, 4).reshape(nb, t * LANES, t * hdim)
    rz = jnp.stack([wz.real, wz.imag], axis=3).reshape(nb, gl, t, hdim, 2, p)
    rz = rz.transpose(0, 2, 1, 3, 4, 5).reshape(nb, t * LANES, 2 * p)
    ry = jnp.stack([cw.real, -cw.imag], axis=1).reshape(nb, gl, 2, t, hdim, p)
    ry = ry.transpose(0, 2, 1, 5, 3, 4).reshape(nb, 2 * gl * p, t * hdim)
    ar = pw[:, t].real.reshape(nb, gl * p)
    ai = pw[:, t].imag.reshape(nb, gl * p)
    coef = jnp.stack([jnp.concatenate([ar, ar], -1), jnp.concatenate([-ai, ai], -1)], axis=1)
    return rm.astype(BF16), rz.astype(BF16), ry.astype(BF16), coef.astype(F32)


MIX_SUB = 256


def _mixout_kernel(*refs, moe):
    (x_ref, attn_ref, pool_ref, ys_ref, wglu_ref, bglu_ref, ga_ref, gp_ref, gs_ref,
     wo_ref, gpost_ref, gpre_ref) = refs[:12]
    if moe:
        router_ref, xo_ref, h_ref, ri_ref, rg_ref, cnt_ref, cnt_acc = refs[12:]

        @pl.when(pl.program_id(0) == 0)
        def _():
            cnt_acc[...] = jnp.zeros_like(cnt_acc)
    else:
        xo_ref, h_ref = refs[12:]
    a = attn_ref.shape[1]
    pw = pool_ref.shape[1]
    h_parts = []
    for s in range(x_ref.shape[0] // MIX_SUB):
        r = pl.ds(s * MIX_SUB, MIX_SUB)
        attn_n = _rms(attn_ref[r, :].astype(F32), ga_ref[...]).astype(BF16)
        pool_n = _rms(pool_ref[r, :].astype(F32), gp_ref[...]).astype(BF16)
        y = jax.nn.gelu(ys_ref[r, :])
        gate = jnp.dot(y.astype(BF16), wglu_ref[...], preferred_element_type=F32) + bglu_ref[...]
        ssm = y * jax.nn.sigmoid(gate)
        ssm_n = _rms(ssm, gs_ref[...]).astype(BF16)
        mix = (jnp.dot(attn_n, wo_ref[0:a, :], preferred_element_type=F32)
               + jnp.dot(pool_n, wo_ref[a:a + pw, :], preferred_element_type=F32)
               + jnp.dot(ssm_n, wo_ref[a + pw:, :], preferred_element_type=F32))
        x = x_ref[r, :] + _rms(mix, gpost_ref[...])
        xo_ref[r, :] = x
        h = _rms(x, gpre_ref[...])
        if moe:
            _store_slabs(h_ref, s * MIX_SUB, h)
            h_parts.append(h)
        else:
            h_ref[r, :] = h.astype(h_ref.dtype)
    if moe:
        hi, mid, _ = _split3(jnp.concatenate(h_parts, axis=0))
        rhi, rmid, _ = _split3(router_ref[...])
        dot_t = lambda p, q: lax.dot_general(p, q, (((1,), (1,)), ((), ())), preferred_element_type=F32)
        logits = dot_t(rhi, hi) + dot_t(rmid, hi) + dot_t(rhi, mid)
        ne, tm = logits.shape
        eidx = lax.broadcasted_iota(jnp.int32, (ne, tm), 0)
        m1 = jnp.max(logits, axis=0, keepdims=True)
        i1 = jnp.min(jnp.where(logits == m1, eidx, ne), axis=0, keepdims=True)
        rest = jnp.where(eidx == i1, -jnp.inf, logits)
        m2 = jnp.max(rest, axis=0, keepdims=True)
        i2 = jnp.min(jnp.where(rest == m2, eidx, ne), axis=0, keepdims=True)
        e2 = jnp.exp(m2 - m1)
        g1 = 1.0 / (1.0 + e2)
        rg_ref[...] = jnp.where(eidx == 0, g1, jnp.where(eidx == 1, e2 * g1, 0.0))
        onehot = ((eidx == i1) | (eidx == i2)).astype(BF16)
        earlier = (lax.broadcasted_iota(jnp.int32, (tm, tm), 0)
                   < lax.broadcasted_iota(jnp.int32, (tm, tm), 1)).astype(BF16)
        before = jnp.dot(onehot, earlier, preferred_element_type=F32) + cnt_acc[:, 0:1]
        r1 = jnp.sum(jnp.where(eidx == i1, before, 0.0), axis=0, keepdims=True).astype(jnp.int32)
        r2 = jnp.sum(jnp.where(eidx == i2, before, 0.0), axis=0, keepdims=True).astype(jnp.int32)
        ri_ref[...] = jnp.where(eidx == 0, i1, jnp.where(eidx == 1, i2,
                                jnp.where(eidx == 2, r1, jnp.where(eidx == 3, r2, 0))))
        cnt_acc[...] += jnp.sum(onehot.astype(F32), axis=1, keepdims=True)
        cnt_ref[...] = cnt_acc[...]


def _mixout(x2, attn, pool, ys, w_glu, b_glu, g_attn, g_pool, g_ssm, w_out, g_post, g_pre,
            router=None, *, tm=512):
    n, d = x2.shape
    moe = router is not None
    row = lambda w: pl.BlockSpec((tm, w), lambda i: (i, 0))
    full = lambda arr: pl.BlockSpec(arr.shape, lambda i: (0, 0))
    ins = [x2, attn, pool, ys, w_glu, b_glu, g_attn, g_pool, g_ssm, w_out, g_post, g_pre]
    in_specs = [row(d), row(attn.shape[1]), row(pool.shape[1]), row(ys.shape[1])] + [full(t) for t in ins[4:]]
    if moe:
        assert d == SUBLANES * LANES
        h_shape = jax.ShapeDtypeStruct((n * SUBLANES, LANES), F32)
        h_spec = pl.BlockSpec((tm * SUBLANES, LANES), lambda i: (i, 0))
    else:
        h_shape, h_spec = jax.ShapeDtypeStruct((n, d), BF16), row(d)
    out_shape = [jax.ShapeDtypeStruct((n, d), F32), h_shape]
    out_specs = [row(d), h_spec]
    scratch = []
    if moe:
        ins.append(router)
        in_specs.append(full(router))
        ne = router.shape[0]
        col = pl.BlockSpec((ne, tm), lambda i: (0, i))
        out_shape += [jax.ShapeDtypeStruct((ne, n), jnp.int32), jax.ShapeDtypeStruct((ne, n), F32),
                      jax.ShapeDtypeStruct((ne, LANES), F32)]
        out_specs += [col, col, pl.BlockSpec((ne, LANES), lambda i: (0, 0))]
        scratch = [pltpu.VMEM((ne, LANES), F32)]
    return pl.pallas_call(
        functools.partial(_mixout_kernel, moe=moe),
        grid=(n // tm,),
        in_specs=in_specs,
        out_specs=tuple(out_specs),
        out_shape=tuple(out_shape),
        scratch_shapes=scratch,
        compiler_params=_params(("arbitrary",)),
        name="mix_out",
    )(*ins)


FFN_STAGE_CHUNKS = 4


def _stage_cast(src_hbm, dst, lane0, stage, sem):
    rows = stage.shape[1]
    width = src_hbm.shape[1]

    def copy(c):
        return pltpu.make_async_copy(src_hbm.at[pl.ds(c * rows, rows)], stage.at[c % 2], sem.at[c % 2])

    nchunks = src_hbm.shape[0] // rows
    copy(0).start()
    for c in range(nchunks):
        if c + 1 < nchunks:
            copy(c + 1).start()
        copy(c).wait()
        dst[pl.ds(c * rows, rows), lane0:lane0 + width] = stage[c % 2].astype(BF16)


def _ffn_kernel(h_ref, x_ref, wg_hbm, wu_hbm, wd_hbm, g_ref, o_ref, wgu_s, wd_s, stage_in, stage_out, sem, *, sub):
    fdim = wd_s.shape[0]

    @pl.when(pl.program_id(0) == 0)
    def _():
        _stage_cast(wg_hbm, wgu_s, 0, stage_in, sem)
        _stage_cast(wu_hbm, wgu_s, fdim, stage_in, sem)
        _stage_cast(wd_hbm, wd_s, 0, stage_out, sem)

    for r in range(h_ref.shape[0] // sub):
        rows = pl.ds(r * sub, sub)
        hgu = jnp.dot(h_ref[rows, :], wgu_s[...], preferred_element_type=F32)
        hg, hu = hgu[:, :fdim], hgu[:, fdim:]
        act = (hg * jax.nn.sigmoid(hg) * hu).astype(BF16)
        y = jnp.dot(act, wd_s[...], preferred_element_type=F32)
        o_ref[rows, :] = x_ref[rows, :] + _rms(y, g_ref[...])


def _ffn(h, x2, wg, wu, wd, g_post, *, tm=512, sub=256):
    n, d = x2.shape
    fdim = wg.shape[1]
    row = pl.BlockSpec((tm, d), lambda i: (i, 0))
    hbm = pl.BlockSpec(memory_space=pl.ANY)
    return pl.pallas_call(
        functools.partial(_ffn_kernel, sub=sub),
        grid=(n // tm,),
        in_specs=[row, row, hbm, hbm, hbm, pl.BlockSpec((1, d), lambda i: (0, 0))],
        out_specs=row,
        out_shape=jax.ShapeDtypeStruct((n, d), F32),
        scratch_shapes=[pltpu.VMEM((d, 2 * fdim), BF16), pltpu.VMEM((fdim, d), BF16),
                        pltpu.VMEM((2, d // FFN_STAGE_CHUNKS, fdim), F32),
                        pltpu.VMEM((2, fdim // FFN_STAGE_CHUNKS, d), F32),
                        pltpu.SemaphoreType.DMA((2,))],
        compiler_params=_params(("arbitrary",)),
        name="dense_ffn",
    )(h, x2, wg, wu, wd, g_post)


MOE_TILE = 512


def _moe_tables(counts, n_items):
    tm = MOE_TILE
    i32 = jnp.int32
    ne = counts.shape[0]
    assert n_items <= LANES
    counts = counts.astype(i32)
    ntiles = (counts + tm - 1) // tm
    k = jnp.arange(ne, dtype=i32)
    ends = jnp.sum(jnp.where(k[None, :] <= k[:, None], ntiles[None, :], 0), axis=1)
    starts = ends - ntiles
    present = ntiles > 0
    later = present[None, :] & (k[None, :] > k[:, None])
    nxt = jnp.min(jnp.where(later, k[None, :], ne), axis=1)
    nxt = jnp.where(nxt < ne, nxt, -1)
    order = jnp.sum((present[None, :] & (k[None, :] < k[:, None])).astype(i32), axis=1)
    w = jnp.arange(LANES, dtype=i32)[None, :]
    wc = jnp.minimum(w, ends[-1] - 1)
    e = jnp.sum((wc >= ends[:, None]).astype(i32), axis=0, keepdims=True)
    of_item = lambda v: jnp.sum(jnp.where(k[:, None] == e, v[:, None], 0), axis=0, keepdims=True)
    valid = w < ends[-1]
    rowblock = jnp.where(valid, w, n_items)
    nvalid = jnp.where(valid, jnp.clip(of_item(counts) - (wc - of_item(starts)) * tm, 0, tm), 0)
    e_prev = jnp.concatenate([e[:, :1], e[:, :-1]], axis=1)
    first = ((w == 0) | (e != e_prev)).astype(i32)
    item = lambda v: v[0, :n_items].astype(i32)
    return ((item(e), item(rowblock), item(nvalid), item(first)),
            (item(of_item(nxt)), item(of_item(order)), jnp.sum(present.astype(i32)).reshape(1)), starts * tm)


def _slab(ref, token):
    return ref.at[pl.ds(pl.multiple_of(token * SUBLANES, SUBLANES), SUBLANES)]


def _dispatch_kernel(dest_ref, h_ref, xs_ref, sem, *, n):
    tm = h_ref.shape[0] // SUBLANES
    base = pl.program_id(0) * tm

    def body(r, carry):
        for k in range(2):
            d = dest_ref[k * n + base + r]
            pltpu.make_async_copy(_slab(h_ref, r), _slab(xs_ref, d), sem).start(priority=k)
        return carry

    lax.fori_loop(0, tm, body, 0, unroll=8)
    for k in range(2):
        pltpu.make_async_copy(h_ref, h_ref, sem).wait()


def _dispatch(dest, h, cap_rows, *, tm=512):
    n = h.shape[0] // SUBLANES
    return pl.pallas_call(
        functools.partial(_dispatch_kernel, n=n),
        grid_spec=pltpu.PrefetchScalarGridSpec(
            num_scalar_prefetch=1,
            grid=(n // tm,),
            in_specs=[pl.BlockSpec((tm * SUBLANES, LANES), lambda i, dest: (i, 0))],
            out_specs=pl.BlockSpec(memory_space=pl.ANY),
            scratch_shapes=[pltpu.SemaphoreType.DMA(())]),
        out_shape=jax.ShapeDtypeStruct((cap_rows * SUBLANES, LANES), F32),
        compiler_params=_params(("arbitrary",)),
        name="moe_dispatch",
    )(dest, h)


MOE_SUB = 256


MOE_UP_CHUNKS = 4


def _moe_up_kernel(ie_ref, rb_ref, nv_ref, first_ref, nxt_ref, ord_ref, npres_ref, xs_ref, wg_hbm, wu_hbm, h_ref,
                   wgu_s, stage, sem):
    f = pl.program_id(0)
    w = pl.program_id(1)
    nf = pl.num_programs(0)
    tf = h_ref.shape[1]
    rows = stage.shape[0] // MOE_UP_CHUNKS

    def copies(e, fp):
        return [pltpu.make_async_copy(src.at[e, pl.ds(c * rows, rows), pl.ds(fp * tf, tf)],
                                      stage.at[pl.ds(c * rows, rows), pl.ds(m * tf, tf)],
                                      sem.at[m * MOE_UP_CHUNKS + c])
                for m, src in enumerate((wg_hbm, wu_hbm)) for c in range(MOE_UP_CHUNKS)]

    @pl.when((f == 0) & (w == 0))
    def _():
        for cp in copies(ie_ref[0], 0):
            cp.start()

    slot = (f * npres_ref[0] + ord_ref[w]) % 2
    is_first = first_ref[w] == 1
    last_expert = nxt_ref[w] < 0
    for fp in range(nf):
        @pl.when(is_first & (f == fp))
        def _():
            for cp in copies(ie_ref[w], fp):
                cp.wait()
            for c in range(MOE_UP_CHUNKS):
                r = pl.ds(c * rows, rows)
                wgu_s[slot, r, :] = stage[r, :].astype(BF16)

            @pl.when(jnp.logical_not(last_expert))
            def _():
                for cp in copies(nxt_ref[w], fp):
                    cp.start()

            if fp + 1 < nf:
                @pl.when(last_expert)
                def _():
                    for cp in copies(ie_ref[0], fp + 1):
                        cp.start()

    nv = nv_ref[w]

    nsub = h_ref.shape[0] // MOE_SUB

    def compute(nchains):
        wgu_view = wgu_s.at[slot]
        for r in range(nchains):
            rws = pl.ds(r * MOE_SUB, MOE_SUB)
            idx = r * MOE_SUB + lax.broadcasted_iota(jnp.int32, (MOE_SUB, 1), 0)
            x = _load_slabs(xs_ref, r * MOE_SUB, MOE_SUB)
            x = jnp.where(idx < nv, x, 0.0).astype(BF16)
            hgu = jnp.dot(x, wgu_view[...], preferred_element_type=F32)
            hg, hu = hgu[:, :tf], hgu[:, tf:]
            h_ref[rws, :] = (hg * jax.nn.sigmoid(hg) * hu).astype(BF16)
        if nchains < nsub:
            rest = pl.ds(nchains * MOE_SUB, (nsub - nchains) * MOE_SUB)
            h_ref[rest, :] = jnp.zeros((rest.size, tf), BF16)

    for nchains in range(nsub + 1):
        pl.when((nv > (nchains - 1) * MOE_SUB) & (nv <= nchains * MOE_SUB))(functools.partial(compute, nchains))


def _moe_up(tables, order, xs, wg, wu, *, tf=1792):
    tm = MOE_TILE
    cap_rows = xs.shape[0] // SUBLANES
    ne, d, fdim = wg.shape
    n_items = tables[0].shape[0]
    hbm = pl.BlockSpec(memory_space=pl.ANY)
    return pl.pallas_call(
        _moe_up_kernel,
        grid_spec=pltpu.PrefetchScalarGridSpec(
            num_scalar_prefetch=7,
            grid=(fdim // tf, n_items),
            in_specs=[pl.BlockSpec((tm * SUBLANES, LANES), lambda f, w, ie, rb, *_: (rb[w], 0)), hbm, hbm],
            out_specs=pl.BlockSpec((tm, tf), lambda f, w, ie, rb, *_: (rb[w], f)),
            scratch_shapes=[pltpu.VMEM((2, d, 2 * tf), BF16), pltpu.VMEM((d, 2 * tf), F32),
                            pltpu.SemaphoreType.DMA((2 * MOE_UP_CHUNKS,))]),
        out_shape=jax.ShapeDtypeStruct((cap_rows, fdim), BF16),
        compiler_params=_params(("arbitrary", "arbitrary")),
        name="moe_up",
    )(*tables, *order, xs, wg, wu)


def _moe_down_kernel(ie_ref, rb_ref, nv_ref, first_ref, nxt_ref, ord_ref, npres_ref, h_ref, wd_hbm, y_ref,
                     wd_s, stage, sem):
    w = pl.program_id(0)
    slot = ord_ref[w] % 2
    nchunks = sem.shape[0]
    chunk = stage.shape[0] // nchunks

    def copies(e):
        return [pltpu.make_async_copy(wd_hbm.at[e, pl.ds(c * chunk, chunk)], stage.at[pl.ds(c * chunk, chunk)],
                                      sem.at[c]) for c in range(nchunks)]

    @pl.when(w == 0)
    def _():
        for cp in copies(ie_ref[0]):
            cp.start()

    @pl.when(first_ref[w] == 1)
    def _():
        for c, cp in enumerate(copies(ie_ref[w])):
            cp.wait()
            wd_s[slot, pl.ds(c * chunk, chunk), :] = stage[pl.ds(c * chunk, chunk), :].astype(BF16)

        @pl.when(nxt_ref[w] >= 0)
        def _():
            for cp in copies(nxt_ref[w]):
                cp.start()

    nv = nv_ref[w]
    nsub = h_ref.shape[0] // MOE_SUB

    def compute(nchains):
        wd = wd_s[slot]
        for r in range(nchains):
            y = jnp.dot(h_ref[pl.ds(r * MOE_SUB, MOE_SUB), :], wd, preferred_element_type=F32)
            _store_slabs(y_ref, r * MOE_SUB, y)
        if nchains < nsub:
            rest = pl.ds(nchains * MOE_SUB * SUBLANES, (nsub - nchains) * MOE_SUB * SUBLANES)
            y_ref[rest, :] = jnp.zeros((rest.size, LANES), F32)

    for nchains in range(nsub + 1):
        pl.when((nv > (nchains - 1) * MOE_SUB) & (nv <= nchains * MOE_SUB))(functools.partial(compute, nchains))


def _moe_down(tables, order, hid, wd, *, chunk=512):
    tm = MOE_TILE
    cap_rows, fdim = hid.shape
    d = wd.shape[2]
    n_items = tables[0].shape[0]
    assert d == SUBLANES * LANES and fdim % chunk == 0
    return pl.pallas_call(
        _moe_down_kernel,
        grid_spec=pltpu.PrefetchScalarGridSpec(
            num_scalar_prefetch=7,
            grid=(n_items,),
            in_specs=[pl.BlockSpec((tm, fdim), lambda w, ie, rb, *_: (rb[w], 0)),
                      pl.BlockSpec(memory_space=pl.ANY)],
            out_specs=pl.BlockSpec((tm * SUBLANES, LANES), lambda w, ie, rb, *_: (rb[w], 0)),
            scratch_shapes=[pltpu.VMEM((2, fdim, d), BF16), pltpu.VMEM((fdim, d), F32),
                            pltpu.SemaphoreType.DMA((fdim // chunk,))]),
        out_shape=jax.ShapeDtypeStruct((cap_rows * SUBLANES, LANES), F32),
        compiler_params=_params(("arbitrary",)),
        name="moe_down",
    )(*tables, *order, hid, wd)


def _combine_kernel(dest_ref, x_ref, gate_ref, g_ref, ys_ref, o_ref, buf, sem, *, n):
    tm = x_ref.shape[0]
    i = pl.program_id(0)

    def issue(tile, slot):
        def body(r, carry):
            for k in range(2):
                d = dest_ref[k * n + tile * tm + r]
                pltpu.make_async_copy(_slab(ys_ref, d), _slab(buf.at[slot, k], r),
                                      sem.at[slot]).start(priority=k)
            return carry
        lax.fori_loop(0, tm, body, 0, unroll=8)

    @pl.when(i == 0)
    def _():
        issue(0, 0)

    @pl.when(i + 1 < pl.num_programs(0))
    def _():
        issue(i + 1, (i + 1) % 2)

    slot = i % 2
    for k in range(2):
        pltpu.make_async_copy(buf.at[slot, k], buf.at[slot, k], sem.at[slot]).wait()
    gates = gate_ref[...].T
    f = (gates[:, 0:1] * _load_slabs(buf.at[slot, 0], 0, tm)
         + gates[:, 1:2] * _load_slabs(buf.at[slot, 1], 0, tm))
    o_ref[...] = x_ref[...] + _rms(f, g_ref[...])


def _combine(dest, x2, gates, g_post, ys, *, tm=512):
    n, d = x2.shape
    return pl.pallas_call(
        functools.partial(_combine_kernel, n=n),
        grid_spec=pltpu.PrefetchScalarGridSpec(
            num_scalar_prefetch=1,
            grid=(n // tm,),
            in_specs=[pl.BlockSpec((tm, d), lambda i, dest: (i, 0)),
                      pl.BlockSpec((gates.shape[0], tm), lambda i, dest: (0, i)),
                      pl.BlockSpec((1, d), lambda i, dest: (0, 0)),
                      pl.BlockSpec(memory_space=pl.ANY)],
            out_specs=pl.BlockSpec((tm, d), lambda i, dest: (i, 0)),
            scratch_shapes=[pltpu.VMEM((2, 2, tm * SUBLANES, LANES), F32), pltpu.SemaphoreType.DMA((2,))]),
        out_shape=jax.ShapeDtypeStruct((n, d), F32),
        compiler_params=_params(("arbitrary",)),
        name="moe_combine",
    )(dest, x2, gates, g_post, ys)


def _moe(h, x2, route_i, route_g, counts, wg, wu, wd, g_post):
    n = x2.shape[0]
    n_items = 2 * n // MOE_TILE + N_EXPERTS
    tables, order, base = _moe_tables(counts[:, 0], n_items)
    base_of = lambda e: jnp.sum(jnp.where(e[None, :] == jnp.arange(N_EXPERTS)[:, None], base[:, None], 0), axis=0)
    dest = jnp.concatenate([base_of(route_i[0]) + route_i[2], base_of(route_i[1]) + route_i[3]])
    xs = _dispatch(dest, h, (n_items + 1) * MOE_TILE)
    hid = _moe_up(tables, order, xs, wg, wu)
    ys = _moe_down(tables, order, hid, wd)
    return _combine(dest, x2, route_g, g_post, ys)


def _row(v):
    return v.reshape(1, -1).astype(F32)


def kernel(x, norm_mix_pre, norm_mix_post, norm_ffn_pre, norm_ffn_post, w_in, b_forget, pool_w, pool_scale, ssm_a_re, ssm_a_im, ssm_b_re, ssm_b_im, ssm_c_re, ssm_c_im, ssm_d, ssm_log_dt, ssm_w_glu, ssm_b_glu, branch_norm_attn, branch_norm_pool, branch_norm_ssm, w_out, ffn_w_gate, ffn_w_up, ffn_w_down, moe_router, moe_w_gate, moe_w_up, moe_w_down):
    b, L, d = x.shape
    depth = w_in.shape[0]
    n = b * L
    a = ATTN_WIDTH
    attn_tile = 512
    x2 = x.reshape(n, d)
    for i in range(depth):
        b_f = jnp.pad(b_forget[i], (0, LANES - ATTN_HEADS)).reshape(1, LANES)
        q, k, v, up, us, c = _inproj(x2, _row(norm_mix_pre[i]), w_in, b_f, layer=i, seq=L)

        ct = c.reshape(b, L, ATTN_HEADS).transpose(0, 2, 1).reshape(b, ATTN_HEADS * (L // attn_tile), attn_tile)
        attn = _attention(q.reshape(b, L, a), k.reshape(b, L, a), v.reshape(b, L, a),
                          ct, tq=attn_tile).reshape(n, a)

        pool = _pool(up.reshape(b, L, POOL_WIDTH),
                     jax.scipy.linalg.block_diag(*pool_w[i]).astype(BF16),
                     _row(pool_scale[i])).reshape(n, POOL_WIDTH)

        mats = _ssm_matrices(ssm_a_re[i], ssm_a_im[i], ssm_b_re[i], ssm_b_im[i],
                             ssm_c_re[i], ssm_c_im[i], ssm_d[i], ssm_log_dt[i])
        ys = _ssm_core(us, *mats, seq=L)

        moe = i % 2 == 1
        j = i // 2
        router = moe_router[j].T if moe else None
        res = _mixout(x2, attn, pool, ys, ssm_w_glu[i].astype(BF16), _row(ssm_b_glu[i]),
                      _row(branch_norm_attn[i]), _row(branch_norm_pool[i]), _row(branch_norm_ssm[i]),
                      w_out[i].astype(BF16), _row(norm_mix_post[i]), _row(norm_ffn_pre[i]), router)
        if moe:
            x2, h, route_i, route_g, counts = res
            x2 = _moe(h, x2, route_i, route_g, counts, moe_w_gate[j], moe_w_up[j], moe_w_down[j],
                      _row(norm_ffn_post[i]))
        else:
            x2, h = res
            x2 = _ffn(h, x2, ffn_w_gate[j], ffn_w_up[j], ffn_w_down[j], _row(norm_ffn_post[i]))
    return x2.reshape(b, L, d)
```

```python
import functools
import math

import jax
import jax.numpy as jnp
from jax import lax
from jax.experimental import pallas as pl
from jax.experimental.pallas import tpu as pltpu

F32 = jnp.float32
BF16 = jnp.bfloat16

RMS_EPS = 1e-6
NEG_INF = -1e30
LOG2E = 1.4426950408889634

ATTN_HEADS = 8
ATTN_HEAD_DIM = 64
ATTN_WIDTH = ATTN_HEADS * ATTN_HEAD_DIM
POOL_WINDOWS = (2, 4, 8, 16)
POOL_GROUP_DIM = 64
POOL_WIDTH = len(POOL_WINDOWS) * POOL_GROUP_DIM
SSM_GROUPS = 16
SSM_GROUP_DIM = 16
SSM_STATE = 64
SSM_WIDTH = SSM_GROUPS * SSM_GROUP_DIM
SSM_CHUNK = 8
N_EXPERTS = 8

LANES = 128
SUBLANES = 8
VMEM_LIMIT = 56 * 1024 * 1024


def _params(sem):
    return pltpu.CompilerParams(dimension_semantics=sem, vmem_limit_bytes=VMEM_LIMIT)


def _rms(x, g):
    return x * lax.rsqrt(jnp.mean(x * x, axis=-1, keepdims=True) + RMS_EPS) * g


def _layer_spec(arr, layer):
    return pl.BlockSpec((None,) + arr.shape[1:], lambda *_: (layer,) + (0,) * (arr.ndim - 1))


def _load_slabs(ref, row0, rows):
    return jnp.concatenate([ref[pl.ds(row0 * SUBLANES + s, rows, stride=SUBLANES), :]
                            for s in range(SUBLANES)], axis=1)


def _store_slabs(ref, row0, val):
    for s in range(SUBLANES):
        ref[pl.ds(row0 * SUBLANES + s, val.shape[0], stride=SUBLANES), :] = val[:, s * LANES:(s + 1) * LANES]


def _split3(x):
    hi = x.astype(BF16)
    r = x - hi.astype(F32)
    mid = r.astype(BF16)
    lo = (r - mid.astype(F32)).astype(BF16)
    return hi, mid, lo


INPROJ_STAGE_ROWS = 256
INPROJ_SUB = 256


def _inproj_kernel(x_ref, g_ref, w_hbm, bf_ref, q_ref, k_ref, v_ref, up_ref, us_ref, c_ref,
                   w_s, stage, sem, carry_ref, *, layer, tiles_per_seq):
    a = ATTN_WIDTH
    f_off = 3 * a
    p_off = f_off + ATTN_HEADS
    main = f_off + POOL_WIDTH + SSM_WIDTH

    @pl.when(pl.program_id(0) == 0)
    def _():
        rows = stage.shape[1]

        def copy(c):
            return pltpu.make_async_copy(w_hbm.at[layer, pl.ds(c * rows, rows)], stage.at[c % 2], sem.at[c % 2])

        nchunks = w_s.shape[0] // rows
        copy(0).start()
        for c in range(nchunks):
            if c + 1 < nchunks:
                copy(c + 1).start()
            copy(c).wait()
            blk = stage[c % 2]
            r = pl.ds(c * rows, rows)
            w_s[r, 0:f_off] = blk[:, 0:f_off].astype(BF16)
            w_s[r, f_off:main] = blk[:, p_off:p_off + main - f_off].astype(BF16)
            lane = lax.broadcasted_iota(jnp.int32, (rows, LANES), 1)
            w_s[r, main:] = jnp.where(lane < ATTN_HEADS, blk[:, f_off:f_off + LANES], 0.0).astype(BF16)

    @pl.when(pl.program_id(0) % tiles_per_seq == 0)
    def _():
        carry_ref[...] = jnp.zeros_like(carry_ref)

    tm = x_ref.shape[0]
    zs = []
    for s in range(tm // INPROJ_SUB):
        r = pl.ds(s * INPROJ_SUB, INPROJ_SUB)
        h = _rms(x_ref[r, :], g_ref[...]).astype(BF16)
        proj = jnp.dot(h, w_s[...], preferred_element_type=F32)
        q_ref[r, :] = (proj[:, 0:a] * (LOG2E * ATTN_HEAD_DIM ** -0.5)).astype(BF16)
        k_ref[r, :] = proj[:, a:2 * a].astype(BF16)
        v_ref[r, :] = proj[:, 2 * a:3 * a].astype(BF16)
        up_ref[r, :] = proj[:, 3 * a:3 * a + POOL_WIDTH].astype(BF16)
        us_ref[r, :] = proj[:, 3 * a + POOL_WIDTH:main]
        zs.append(proj[:, main:])

    z = jnp.concatenate(zs, axis=0) + bf_ref[...]
    logf = jnp.minimum(z, 0.0) - jnp.log(1.0 + jnp.exp(-jnp.abs(z)))
    row = lax.broadcasted_iota(jnp.int32, (tm, tm), 0)
    col = lax.broadcasted_iota(jnp.int32, (tm, tm), 1)
    tri = (row >= col).astype(BF16)
    hi, mid, _ = _split3(logf)
    c = (jnp.dot(tri, hi, preferred_element_type=F32)
         + jnp.dot(tri, mid, preferred_element_type=F32)) + carry_ref[...]
    c_ref[...] = c[:, :ATTN_HEADS]
    carry_ref[...] = c[tm - 1:tm, :]


def _inproj(x2, g, w_in, b_f, *, layer, seq, tm=512):
    n, d = x2.shape
    n_in = w_in.shape[2]
    a = ATTN_WIDTH
    n_res = n_in - ATTN_HEADS + LANES
    out_shape = (
        jax.ShapeDtypeStruct((n, a), BF16), jax.ShapeDtypeStruct((n, a), BF16),
        jax.ShapeDtypeStruct((n, a), BF16), jax.ShapeDtypeStruct((n, POOL_WIDTH), BF16),
        jax.ShapeDtypeStruct((n, SSM_WIDTH), F32), jax.ShapeDtypeStruct((n, ATTN_HEADS), F32))
    row = lambda w: pl.BlockSpec((tm, w), lambda i: (i, 0))
    return pl.pallas_call(
        functools.partial(_inproj_kernel, layer=layer, tiles_per_seq=seq // tm),
        grid=(n // tm,),
        in_specs=[row(d), _layer_spec(g, layer), pl.BlockSpec(memory_space=pl.ANY), _layer_spec(b_f, layer)],
        out_specs=(row(a), row(a), row(a), row(POOL_WIDTH), row(SSM_WIDTH), row(ATTN_HEADS)),
        out_shape=out_shape,
        scratch_shapes=[pltpu.VMEM((d, n_res), BF16), pltpu.VMEM((2, INPROJ_STAGE_ROWS, n_in), F32),
                        pltpu.SemaphoreType.DMA((2,)), pltpu.VMEM((1, LANES), F32)],
        compiler_params=_params(("arbitrary",)),
        name="inproj",
    )(x2, g, w_in, b_f)


HEADS_PER_BLOCK = LANES // ATTN_HEAD_DIM


def _attn_kernel(q_ref, k_ref, v_ref, ct_ref, o_ref, *, tq, tk):
    nlb = q_ref.shape[2] // LANES
    hp = pl.program_id(1)
    qi = pl.program_id(2)
    nkb = k_ref.shape[1] // tk
    lane = lax.broadcasted_iota(jnp.int32, (1, LANES), 1)
    in_head = [(lane // ATTN_HEAD_DIM) == hh for hh in range(HEADS_PER_BLOCK)]
    chains = [(lb, hh) for lb in range(nlb) for hh in range(HEADS_PER_BLOCK)]
    qm, crow, cref = [], [], []
    for lb in range(nlb):
        q2 = q_ref[0, :, lb * LANES:(lb + 1) * LANES]
        qm.append(jnp.concatenate([jnp.where(msk, q2, jnp.zeros_like(q2)) for msk in in_head], axis=0))
    for lb, hh in chains:
        crow.append(((hp * nlb + lb) * HEADS_PER_BLOCK + hh) * nkb)
        cref.append(ct_ref[0, pl.ds(crow[-1] + qi, 1), :][:, 0:1])

    def step(kb, carry, masked):
        new = []
        scores = {}
        for ci, (lb, hh) in enumerate(chains):
            kblk = k_ref[0, pl.ds(kb * tk, tk), lb * LANES:(lb + 1) * LANES]
            vblk = v_ref[0, pl.ds(kb * tk, tk), lb * LANES:(lb + 1) * LANES]
            m, acc = carry[ci]
            if lb not in scores:
                scores[lb] = lax.dot_general(qm[lb], kblk, (((1,), (1,)), ((), ())), preferred_element_type=F32)
            s = scores[lb][hh * tq:(hh + 1) * tq, :]
            s = s - (ct_ref[0, pl.ds(crow[ci] + kb, 1), :] - cref[ci]) * LOG2E
            if masked:
                r = lax.broadcasted_iota(jnp.int32, (tq, tk), 0)
                cidx = lax.broadcasted_iota(jnp.int32, (tq, tk), 1)
                s = jnp.where(cidx <= r, s, NEG_INF)
            m_new = jnp.maximum(m, jnp.max(s, axis=1, keepdims=True))
            p = jnp.exp2(s - m_new)
            vsel = jnp.where(in_head[hh], vblk, jnp.ones_like(vblk))
            acc = jnp.exp2(m - m_new) * acc + jnp.dot(p.astype(BF16), vsel, preferred_element_type=F32)
            new.append((m_new, acc))
        return tuple(new)

    init = tuple((jnp.full((tq, 1), NEG_INF, F32), jnp.zeros((tq, LANES), F32)) for _ in chains)
    carry = lax.fori_loop(0, qi, lambda kb, cr: step(kb, cr, False), init)
    carry = step(qi, carry, True)
    for lb in range(nlb):
        out = jnp.zeros((tq, LANES), F32)
        for hh in range(HEADS_PER_BLOCK):
            acc = carry[lb * HEADS_PER_BLOCK + hh][1]
            out = jnp.where(in_head[hh], acc / pltpu.roll(acc, ATTN_HEAD_DIM, 1), out)
        o_ref[0, :, lb * LANES:(lb + 1) * LANES] = out.astype(o_ref.dtype)


def _attention(q, k, v, ct, *, tq, lane_blocks=4):
    b, L, a = q.shape
    tk = ct.shape[2]
    assert tq == tk, "the diagonal key block of query tile i must be key block i"
    w = lane_blocks * LANES
    blk = lambda: pl.BlockSpec((1, tq, w), lambda bi, hi, qi: (bi, qi, hi))
    seq = lambda: pl.BlockSpec((1, L, w), lambda bi, hi, qi: (bi, 0, hi))
    return pl.pallas_call(
        functools.partial(_attn_kernel, tq=tq, tk=tk),
        grid=(b, a // w, L // tq),
        in_specs=[blk(), seq(), seq(),
                  pl.BlockSpec((1, ct.shape[1], tk), lambda bi, hi, qi: (bi, 0, 0))],
        out_specs=blk(),
        out_shape=jax.ShapeDtypeStruct((b, L, a), BF16),
        compiler_params=_params(("arbitrary", "arbitrary", "arbitrary")),
        name="fox_attention",
    )(q, k, v, ct)


def _pool_kernel(u_ref, w_ref, s_ref, o_ref):
    x = u_ref[0].astype(F32)
    L, w = x.shape
    row = lax.broadcasted_iota(jnp.int32, (L, w), 0)
    group = lax.broadcasted_iota(jnp.int32, (L, w), 1) // POOL_GROUP_DIM

    def shifted(y, s):
        return jnp.where(row >= s, pltpu.roll(y, s, 0), 0.0)

    acc = x
    sel = jnp.zeros_like(x)
    win_lane = jnp.zeros((L, w), F32)
    span = 1
    for gi, win in enumerate(POOL_WINDOWS):
        while span < win:
            acc = acc + shifted(acc, span)
            span *= 2
        sel = jnp.where(group == gi, acc, sel)
        win_lane = jnp.where(group == gi, float(win), win_lane)
    count = jnp.minimum(row.astype(F32) + 1.0, win_lane)
    d = sel / count - x
    y = jnp.dot(d.astype(BF16), w_ref[...], preferred_element_type=F32) * s_ref[...]
    o_ref[0] = y.astype(o_ref.dtype)


def _pool(u, w_bd, scale, *, layer):
    b, L, w = u.shape
    return pl.pallas_call(
        _pool_kernel,
        grid=(b,),
        in_specs=[pl.BlockSpec((1, L, w), lambda i: (i, 0, 0)),
                  _layer_spec(w_bd, layer), _layer_spec(scale, layer)],
        out_specs=pl.BlockSpec((1, L, w), lambda i: (i, 0, 0)),
        out_shape=jax.ShapeDtypeStruct((b, L, w), BF16),
        compiler_params=_params(("arbitrary",)),
        name="pool_mixer",
    )(u, w_bd, scale)


def _group_of(idx, width):
    groups = LANES // SSM_GROUP_DIM
    return (idx >> (width.bit_length() - 1)) & (groups - 1)


def _expand_blockdiag(r_ref, row_inner, col_inner):
    rows, w = r_ref.shape[1:]
    groups = LANES // SSM_GROUP_DIM
    r = lax.broadcasted_iota(jnp.int32, (w, groups * w), 0)
    c = lax.broadcasted_iota(jnp.int32, (w, groups * w), 1)
    shift = col_inner.bit_length() - 1
    outer_c = c >> (shift + groups.bit_length() - 1)
    spread = (outer_c == (r >> shift)) & ((c & (col_inner - 1)) == (r & (col_inner - 1)))
    big = jnp.dot(r_ref[0], spread.astype(BF16), preferred_element_type=F32)
    ri = lax.broadcasted_iota(jnp.int32, (rows, groups * w), 0)
    ci = lax.broadcasted_iota(jnp.int32, (rows, groups * w), 1)
    return jnp.where(_group_of(ri, row_inner) == _group_of(ci, col_inner), big, 0.0).astype(BF16)


def _ssm_kernel(u_ref, rm_ref, rz_ref, ry_ref, coef_ref, y_ref, m_s, wz_s, wy_s, ucat, z_ref, s_ref, *, seqs):
    t = SSM_CHUNK

    @pl.when(pl.program_id(1) == 0)
    def _():
        m_s[...] = _expand_blockdiag(rm_ref, SSM_GROUP_DIM, SSM_GROUP_DIM)
        wz_s[...] = _expand_blockdiag(rz_ref, SSM_GROUP_DIM, SSM_STATE)
        wy_s[...] = _expand_blockdiag(ry_ref, SSM_STATE, SSM_GROUP_DIM)

    nc = u_ref.shape[0] // t
    cps = nc // seqs
    for k in range(t):
        ucat[:, k * LANES:(k + 1) * LANES] = u_ref[pl.ds(k, nc, stride=t), :].astype(BF16)
    u = ucat[...]
    z_ref[...] = jnp.dot(u, wz_s[...], preferred_element_type=F32)
    half = z_ref.shape[1] // 2
    a1 = coef_ref[0, 0:1, :]
    a2 = coef_ref[0, 1:2, :]

    def step(c, states):
        new = []
        for b in range(seqs):
            st = states[b]
            r = b * cps + c
            s_ref[pl.ds(r, 1), :] = st
            sw = jnp.concatenate([st[:, half:], st[:, :half]], axis=1)
            new.append(a1 * st + a2 * sw + z_ref[pl.ds(r, 1), :])
        return tuple(new)

    init = tuple(jnp.zeros((1, 2 * half), F32) for _ in range(seqs))
    lax.fori_loop(0, cps, step, init, unroll=8)
    y = jnp.dot(u, m_s[...], preferred_element_type=F32)
    y = y + jnp.dot(s_ref[...].astype(BF16), wy_s[...], preferred_element_type=F32)
    for k in range(t):
        y_ref[pl.ds(k, nc, stride=t), :] = y[:, k * LANES:(k + 1) * LANES]


def _ssm_core(u, rm, rz, ry, coef, *, layer, seq, max_seqs_per_block=4):
    n, w = u.shape
    seqs_per_block = math.gcd(max_seqs_per_block, n // seq)
    rows = seq * seqs_per_block
    nc = rows // SSM_CHUNK
    kw = SSM_CHUNK * LANES
    p2 = coef.shape[3]
    blk = pl.BlockSpec((rows, LANES), lambda hf, r: (r, hf))
    wspec = lambda arr: pl.BlockSpec((None, 1) + arr.shape[2:], lambda hf, r: (layer, hf, 0, 0))
    return pl.pallas_call(
        functools.partial(_ssm_kernel, seqs=seqs_per_block),
        grid=(w // LANES, n // rows),
        in_specs=[blk, wspec(rm), wspec(rz), wspec(ry), wspec(coef)],
        out_specs=blk,
        out_shape=jax.ShapeDtypeStruct((n, w), F32),
        scratch_shapes=[pltpu.VMEM((kw, kw), BF16), pltpu.VMEM((kw, p2), BF16), pltpu.VMEM((p2, kw), BF16),
                        pltpu.VMEM((nc, kw), BF16), pltpu.VMEM((nc, p2), F32), pltpu.VMEM((nc, p2), F32)],
        compiler_params=_params(("arbitrary", "arbitrary")),
        name="s5_core",
    )(u, rm, rz, ry, coef)


def _ssm_matrices(a_re, a_im, b_re, b_im, c_re, c_im, d, log_dt):
    t = SSM_CHUNK
    g, p = a_re.shape
    hdim = d.shape[1]
    gl = LANES // hdim
    nb = g // gl
    lam = lax.complex(a_re, a_im)
    lam_dt = lam * jnp.exp(log_dt)[:, None]
    lam_bar = jnp.exp(lam_dt)
    b_bar = ((lam_bar - 1.0) / lam)[:, :, None] * lax.complex(b_re, b_im)
    cmat = lax.complex(c_re, c_im)
    steps = jnp.arange(t + 1, dtype=F32)
    pw = jnp.exp(lam_dt[:, None, :] * steps[None, :, None])
    kern = jnp.einsum('gop,gdp,gpi->gdoi', cmat, pw[:, :t], b_bar).real
    kern = kern.at[:, 0].add(jax.vmap(jnp.diag)(d))
    j = jnp.arange(t)[:, None]
    i = jnp.arange(t)[None, :]
    m = jnp.where((i >= j)[None, :, :, None, None], kern[:, jnp.clip(i - j, 0, t - 1)], 0.0)
    wz = jnp.einsum('gjp,gpi->gjip', pw[:, t - 1 - jnp.arange(t)], b_bar)
    cw = jnp.einsum('gop,gip->giop', cmat, pw[:, 1:t + 1])
    rm = m.reshape(nb, gl, t, t, hdim, hdim).transpose(0, 2, 1, 5, 3, 4).reshape(nb, t * LANES, t * hdim)
    rz = jnp.stack([wz.real, wz.imag], axis=3).reshape(nb, gl, t, hdim, 2, p)
    rz = rz.transpose(0, 2, 1, 3, 4, 5).reshape(nb, t * LANES, 2 * p)
    ry = jnp.stack([cw.real, -cw.imag], axis=1).reshape(nb, gl, 2, t, hdim, p)
    ry = ry.transpose(0, 2, 1, 5, 3, 4).reshape(nb, 2 * gl * p, t * hdim)
    ar = pw[:, t].real.reshape(nb, gl * p)
    ai = pw[:, t].imag.reshape(nb, gl * p)
    coef = jnp.stack([jnp.concatenate([ar, ar], -1), jnp.concatenate([-ai, ai], -1)], axis=1)
    return rm.astype(BF16), rz.astype(BF16), ry.astype(BF16), coef.astype(F32)


MIX_SUB = 256


def _mixout_kernel(*refs, moe):
    (x_ref, attn_ref, pool_ref, ys_ref, wglu_ref, bglu_ref, ga_ref, gp_ref, gs_ref,
     wo_ref, gpost_ref, gpre_ref) = refs[:12]
    if moe:
        router_ref, xo_ref, h_ref, ri_ref, rg_ref, cnt_ref, cnt_acc = refs[12:]

        @pl.when(pl.program_id(0) == 0)
        def _():
            cnt_acc[...] = jnp.zeros_like(cnt_acc)
    else:
        xo_ref, h_ref = refs[12:]
    a = attn_ref.shape[1]
    pw = pool_ref.shape[1]
    h_parts = []
    for s in range(x_ref.shape[0] // MIX_SUB):
        r = pl.ds(s * MIX_SUB, MIX_SUB)
        attn_n = _rms(attn_ref[r, :].astype(F32), ga_ref[...]).astype(BF16)
        pool_n = _rms(pool_ref[r, :].astype(F32), gp_ref[...]).astype(BF16)
        y = jax.nn.gelu(ys_ref[r, :])
        gate = jnp.dot(y.astype(BF16), wglu_ref[...], preferred_element_type=F32) + bglu_ref[...]
        ssm = y * jax.nn.sigmoid(gate)
        ssm_n = _rms(ssm, gs_ref[...]).astype(BF16)
        mix = (jnp.dot(attn_n, wo_ref[0:a, :], preferred_element_type=F32)
               + jnp.dot(pool_n, wo_ref[a:a + pw, :], preferred_element_type=F32)
               + jnp.dot(ssm_n, wo_ref[a + pw:, :], preferred_element_type=F32))
        x = x_ref[r, :] + _rms(mix, gpost_ref[...])
        xo_ref[r, :] = x
        h = _rms(x, gpre_ref[...])
        if moe:
            _store_slabs(h_ref, s * MIX_SUB, h)
            h_parts.append(h)
        else:
            h_ref[r, :] = h.astype(h_ref.dtype)
    if moe:
        hi, mid, _ = _split3(jnp.concatenate(h_parts, axis=0))
        rhi, rmid, _ = _split3(router_ref[...])
        dot_t = lambda p, q: lax.dot_general(p, q, (((1,), (1,)), ((), ())), preferred_element_type=F32)
        logits = dot_t(rhi, hi) + dot_t(rmid, hi) + dot_t(rhi, mid)
        ne, tm = logits.shape
        eidx = lax.broadcasted_iota(jnp.int32, (ne, tm), 0)
        m1 = jnp.max(logits, axis=0, keepdims=True)
        i1 = jnp.min(jnp.where(logits == m1, eidx, ne), axis=0, keepdims=True)
        rest = jnp.where(eidx == i1, -jnp.inf, logits)
        m2 = jnp.max(rest, axis=0, keepdims=True)
        i2 = jnp.min(jnp.where(rest == m2, eidx, ne), axis=0, keepdims=True)
        e2 = jnp.exp(m2 - m1)
        g1 = 1.0 / (1.0 + e2)
        rg_ref[...] = jnp.where(eidx == 0, g1, jnp.where(eidx == 1, e2 * g1, 0.0))
        onehot = ((eidx == i1) | (eidx == i2)).astype(BF16)
        earlier = (lax.broadcasted_iota(jnp.int32, (tm, tm), 0)
                   < lax.broadcasted_iota(jnp.int32, (tm, tm), 1)).astype(BF16)
        before = jnp.dot(onehot, earlier, preferred_element_type=F32) + cnt_acc[:, 0:1]
        r1 = jnp.sum(jnp.where(eidx == i1, before, 0.0), axis=0, keepdims=True).astype(jnp.int32)
        r2 = jnp.sum(jnp.where(eidx == i2, before, 0.0), axis=0, keepdims=True).astype(jnp.int32)
        ri_ref[...] = jnp.where(eidx == 0, i1, jnp.where(eidx == 1, i2,
                                jnp.where(eidx == 2, r1, jnp.where(eidx == 3, r2, 0))))
        cnt_acc[...] += jnp.sum(onehot.astype(F32), axis=1, keepdims=True)
        cnt_ref[...] = cnt_acc[...]


def _mixout(x2, attn, pool, ys, w_glu, b_glu, g_attn, g_pool, g_ssm, w_out, g_post, g_pre,
            router=None, *, layer, moe_layer=None, tm=512):
    n, d = x2.shape
    moe = router is not None
    row = lambda w: pl.BlockSpec((tm, w), lambda i: (i, 0))
    ins = [x2, attn, pool, ys, w_glu, b_glu, g_attn, g_pool, g_ssm, w_out, g_post, g_pre]
    in_specs = ([row(d), row(attn.shape[1]), row(pool.shape[1]), row(ys.shape[1])]
                + [_layer_spec(t, layer) for t in ins[4:]])
    if moe:
        assert d == SUBLANES * LANES
        h_shape = jax.ShapeDtypeStruct((n * SUBLANES, LANES), F32)
        h_spec = pl.BlockSpec((tm * SUBLANES, LANES), lambda i: (i, 0))
    else:
        h_shape, h_spec = jax.ShapeDtypeStruct((n, d), BF16), row(d)
    out_shape = [jax.ShapeDtypeStruct((n, d), F32), h_shape]
    out_specs = [row(d), h_spec]
    scratch = []
    if moe:
        ins.append(router)
        in_specs.append(_layer_spec(router, moe_layer))
        ne = router.shape[1]
        col = pl.BlockSpec((ne, tm), lambda i: (0, i))
        out_shape += [jax.ShapeDtypeStruct((ne, n), jnp.int32), jax.ShapeDtypeStruct((ne, n), F32),
                      jax.ShapeDtypeStruct((ne, LANES), F32)]
        out_specs += [col, col, pl.BlockSpec((ne, LANES), lambda i: (0, 0))]
        scratch = [pltpu.VMEM((ne, LANES), F32)]
    return pl.pallas_call(
        functools.partial(_mixout_kernel, moe=moe),
        grid=(n // tm,),
        in_specs=in_specs,
        out_specs=tuple(out_specs),
        out_shape=tuple(out_shape),
        scratch_shapes=scratch,
        compiler_params=_params(("arbitrary",)),
        name="mix_out",
    )(*ins)


FFN_STAGE_CHUNKS = 4


def _stage_cast(src_hbm, dst, lane0, stage, sem):
    rows = stage.shape[1]
    width = src_hbm.shape[1]

    def copy(c):
        return pltpu.make_async_copy(src_hbm.at[pl.ds(c * rows, rows)], stage.at[c % 2], sem.at[c % 2])

    nchunks = src_hbm.shape[0] // rows
    copy(0).start()
    for c in range(nchunks):
        if c + 1 < nchunks:
            copy(c + 1).start()
        copy(c).wait()
        dst[pl.ds(c * rows, rows), lane0:lane0 + width] = stage[c % 2].astype(BF16)


def _ffn_kernel(h_ref, x_ref, wg_hbm, wu_hbm, wd_hbm, g_ref, o_ref, wgu_s, wd_s, stage_in, stage_out, sem, *, sub):
    fdim = wd_s.shape[0]

    @pl.when(pl.program_id(0) == 0)
    def _():
        _stage_cast(wg_hbm, wgu_s, 0, stage_in, sem)
        _stage_cast(wu_hbm, wgu_s, fdim, stage_in, sem)
        _stage_cast(wd_hbm, wd_s, 0, stage_out, sem)

    for r in range(h_ref.shape[0] // sub):
        rows = pl.ds(r * sub, sub)
        hgu = jnp.dot(h_ref[rows, :], wgu_s[...], preferred_element_type=F32)
        hg, hu = hgu[:, :fdim], hgu[:, fdim:]
        act = (hg * jax.nn.sigmoid(hg) * hu).astype(BF16)
        y = jnp.dot(act, wd_s[...], preferred_element_type=F32)
        o_ref[rows, :] = x_ref[rows, :] + _rms(y, g_ref[...])


def _ffn(h, x2, wg, wu, wd, g_post, *, layer, tm=512, sub=256):
    n, d = x2.shape
    fdim = wg.shape[1]
    row = pl.BlockSpec((tm, d), lambda i: (i, 0))
    hbm = pl.BlockSpec(memory_space=pl.ANY)
    return pl.pallas_call(
        functools.partial(_ffn_kernel, sub=sub),
        grid=(n // tm,),
        in_specs=[row, row, hbm, hbm, hbm, _layer_spec(g_post, layer)],
        out_specs=row,
        out_shape=jax.ShapeDtypeStruct((n, d), F32),
        scratch_shapes=[pltpu.VMEM((d, 2 * fdim), BF16), pltpu.VMEM((fdim, d), BF16),
                        pltpu.VMEM((2, d // FFN_STAGE_CHUNKS, fdim), F32),
                        pltpu.VMEM((2, fdim // FFN_STAGE_CHUNKS, d), F32),
                        pltpu.SemaphoreType.DMA((2,))],
        compiler_params=_params(("arbitrary",)),
        name="dense_ffn",
    )(h, x2, wg, wu, wd, g_post)


MOE_TILE = 512


def _moe_tables(counts, n_items):
    tm = MOE_TILE
    i32 = jnp.int32
    ne = counts.shape[0]
    assert n_items <= LANES
    counts = counts.astype(i32)
    ntiles = (counts + tm - 1) // tm
    k = jnp.arange(ne, dtype=i32)
    ends = jnp.sum(jnp.where(k[None, :] <= k[:, None], ntiles[None, :], 0), axis=1)
    starts = ends - ntiles
    present = ntiles > 0
    later = present[None, :] & (k[None, :] > k[:, None])
    nxt = jnp.min(jnp.where(later, k[None, :], ne), axis=1)
    nxt = jnp.where(nxt < ne, nxt, -1)
    order = jnp.sum((present[None, :] & (k[None, :] < k[:, None])).astype(i32), axis=1)
    w = jnp.arange(LANES, dtype=i32)[None, :]
    wc = jnp.minimum(w, ends[-1] - 1)
    e = jnp.sum((wc >= ends[:, None]).astype(i32), axis=0, keepdims=True)
    of_item = lambda v: jnp.sum(jnp.where(k[:, None] == e, v[:, None], 0), axis=0, keepdims=True)
    valid = w < ends[-1]
    rowblock = jnp.where(valid, w, n_items)
    nvalid = jnp.where(valid, jnp.clip(of_item(counts) - (wc - of_item(starts)) * tm, 0, tm), 0)
    e_prev = jnp.concatenate([e[:, :1], e[:, :-1]], axis=1)
    first = ((w == 0) | (e != e_prev)).astype(i32)
    item = lambda v: v[0, :n_items].astype(i32)
    return ((item(e), item(rowblock), item(nvalid), item(first)),
            (item(of_item(nxt)), item(of_item(order)), jnp.sum(present.astype(i32)).reshape(1)), starts * tm)


def _slab(ref, token):
    return ref.at[pl.ds(pl.multiple_of(token * SUBLANES, SUBLANES), SUBLANES)]


def _dispatch_kernel(dest_ref, h_ref, xs_ref, sem, *, n):
    tm = h_ref.shape[0] // SUBLANES
    base = pl.program_id(0) * tm

    def body(r, carry):
        for k in range(2):
            d = dest_ref[k * n + base + r]
            pltpu.make_async_copy(_slab(h_ref, r), _slab(xs_ref, d), sem).start(priority=k)
        return carry

    lax.fori_loop(0, tm, body, 0, unroll=8)
    for k in range(2):
        pltpu.make_async_copy(h_ref, h_ref, sem).wait()


def _dispatch(dest, h, cap_rows, *, tm=512):
    n = h.shape[0] // SUBLANES
    return pl.pallas_call(
        functools.partial(_dispatch_kernel, n=n),
        grid_spec=pltpu.PrefetchScalarGridSpec(
            num_scalar_prefetch=1,
            grid=(n // tm,),
            in_specs=[pl.BlockSpec((tm * SUBLANES, LANES), lambda i, dest: (i, 0))],
            out_specs=pl.BlockSpec(memory_space=pl.ANY),
            scratch_shapes=[pltpu.SemaphoreType.DMA(())]),
        out_shape=jax.ShapeDtypeStruct((cap_rows * SUBLANES, LANES), F32),
        compiler_params=_params(("arbitrary",)),
        name="moe_dispatch",
    )(dest, h)


MOE_SUB = 256


MOE_UP_CHUNKS = 4


def _moe_up_kernel(ie_ref, rb_ref, nv_ref, first_ref, nxt_ref, ord_ref, npres_ref, xs_ref, wg_hbm, wu_hbm, h_ref,
                   wgu_s, stage, sem):
    f = pl.program_id(0)
    w = pl.program_id(1)
    nf = pl.num_programs(0)
    tf = h_ref.shape[1]
    rows = stage.shape[0] // MOE_UP_CHUNKS

    def copies(e, fp):
        return [pltpu.make_async_copy(src.at[e, pl.ds(c * rows, rows), pl.ds(fp * tf, tf)],
                                      stage.at[pl.ds(c * rows, rows), pl.ds(m * tf, tf)],
                                      sem.at[m * MOE_UP_CHUNKS + c])
                for m, src in enumerate((wg_hbm, wu_hbm)) for c in range(MOE_UP_CHUNKS)]

    @pl.when((f == 0) & (w == 0))
    def _():
        for cp in copies(ie_ref[0], 0):
            cp.start()

    slot = (f * npres_ref[0] + ord_ref[w]) % 2
    is_first = first_ref[w] == 1
    last_expert = nxt_ref[w] < 0
    for fp in range(nf):
        @pl.when(is_first & (f == fp))
        def _():
            for cp in copies(ie_ref[w], fp):
                cp.wait()
            for c in range(MOE_UP_CHUNKS):
                r = pl.ds(c * rows, rows)
                wgu_s[slot, r, :] = stage[r, :].astype(BF16)

            @pl.when(jnp.logical_not(last_expert))
            def _():
                for cp in copies(nxt_ref[w], fp):
                    cp.start()

            if fp + 1 < nf:
                @pl.when(last_expert)
                def _():
                    for cp in copies(ie_ref[0], fp + 1):
                        cp.start()

    nv = nv_ref[w]

    nsub = h_ref.shape[0] // MOE_SUB

    def compute(nchains):
        wgu_view = wgu_s.at[slot]
        for r in range(nchains):
            rws = pl.ds(r * MOE_SUB, MOE_SUB)
            idx = r * MOE_SUB + lax.broadcasted_iota(jnp.int32, (MOE_SUB, 1), 0)
            x = _load_slabs(xs_ref, r * MOE_SUB, MOE_SUB)
            x = jnp.where(idx < nv, x, 0.0).astype(BF16)
            hgu = jnp.dot(x, wgu_view[...], preferred_element_type=F32)
            hg, hu = hgu[:, :tf], hgu[:, tf:]
            h_ref[rws, :] = (hg * jax.nn.sigmoid(hg) * hu).astype(BF16)
        if nchains < nsub:
            rest = pl.ds(nchains * MOE_SUB, (nsub - nchains) * MOE_SUB)
            h_ref[rest, :] = jnp.zeros((rest.size, tf), BF16)

    for nchains in range(nsub + 1):
        pl.when((nv > (nchains - 1) * MOE_SUB) & (nv <= nchains * MOE_SUB))(functools.partial(compute, nchains))


def _moe_up(tables, order, xs, wg, wu, *, tf=1792):
    tm = MOE_TILE
    cap_rows = xs.shape[0] // SUBLANES
    ne, d, fdim = wg.shape
    n_items = tables[0].shape[0]
    hbm = pl.BlockSpec(memory_space=pl.ANY)
    return pl.pallas_call(
        _moe_up_kernel,
        grid_spec=pltpu.PrefetchScalarGridSpec(
            num_scalar_prefetch=7,
            grid=(fdim // tf, n_items),
            in_specs=[pl.BlockSpec((tm * SUBLANES, LANES), lambda f, w, ie, rb, *_: (rb[w], 0)), hbm, hbm],
            out_specs=pl.BlockSpec((tm, tf), lambda f, w, ie, rb, *_: (rb[w], f)),
            scratch_shapes=[pltpu.VMEM((2, d, 2 * tf), BF16), pltpu.VMEM((d, 2 * tf), F32),
                            pltpu.SemaphoreType.DMA((2 * MOE_UP_CHUNKS,))]),
        out_shape=jax.ShapeDtypeStruct((cap_rows, fdim), BF16),
        compiler_params=_params(("arbitrary", "arbitrary")),
        name="moe_up",
    )(*tables, *order, xs, wg, wu)


def _moe_down_kernel(ie_ref, rb_ref, nv_ref, first_ref, nxt_ref, ord_ref, npres_ref, h_ref, wd_hbm, y_ref,
                     wd_s, stage, sem):
    w = pl.program_id(0)
    slot = ord_ref[w] % 2
    nchunks = sem.shape[0]
    chunk = stage.shape[0] // nchunks

    def copies(e):
        return [pltpu.make_async_copy(wd_hbm.at[e, pl.ds(c * chunk, chunk)], stage.at[pl.ds(c * chunk, chunk)],
                                      sem.at[c]) for c in range(nchunks)]

    @pl.when(w == 0)
    def _():
        for cp in copies(ie_ref[0]):
            cp.start()

    @pl.when(first_ref[w] == 1)
    def _():
        for c, cp in enumerate(copies(ie_ref[w])):
            cp.wait()
            wd_s[slot, pl.ds(c * chunk, chunk), :] = stage[pl.ds(c * chunk, chunk), :].astype(BF16)

        @pl.when(nxt_ref[w] >= 0)
        def _():
            for cp in copies(nxt_ref[w]):
                cp.start()

    nv = nv_ref[w]
    nsub = h_ref.shape[0] // MOE_SUB

    def compute(nchains):
        wd = wd_s[slot]
        for r in range(nchains):
            y = jnp.dot(h_ref[pl.ds(r * MOE_SUB, MOE_SUB), :], wd, preferred_element_type=F32)
            _store_slabs(y_ref, r * MOE_SUB, y)
        if nchains < nsub:
            rest = pl.ds(nchains * MOE_SUB * SUBLANES, (nsub - nchains) * MOE_SUB * SUBLANES)
            y_ref[rest, :] = jnp.zeros((rest.size, LANES), F32)

    for nchains in range(nsub + 1):
        pl.when((nv > (nchains - 1) * MOE_SUB) & (nv <= nchains * MOE_SUB))(functools.partial(compute, nchains))


def _moe_down(tables, order, hid, wd, *, chunk=512):
    tm = MOE_TILE
    cap_rows, fdim = hid.shape
    d = wd.shape[2]
    n_items = tables[0].shape[0]
    assert d == SUBLANES * LANES and fdim % chunk == 0
    return pl.pallas_call(
        _moe_down_kernel,
        grid_spec=pltpu.PrefetchScalarGridSpec(
            num_scalar_prefetch=7,
            grid=(n_items,),
            in_specs=[pl.BlockSpec((tm, fdim), lambda w, ie, rb, *_: (rb[w], 0)),
                      pl.BlockSpec(memory_space=pl.ANY)],
            out_specs=pl.BlockSpec((tm * SUBLANES, LANES), lambda w, ie, rb, *_: (rb[w], 0)),
            scratch_shapes=[pltpu.VMEM((2, fdim, d), BF16), pltpu.VMEM((fdim, d), F32),
                            pltpu.SemaphoreType.DMA((fdim // chunk,))]),
        out_shape=jax.ShapeDtypeStruct((cap_rows * SUBLANES, LANES), F32),
        compiler_params=_params(("arbitrary",)),
        name="moe_down",
    )(*tables, *order, hid, wd)


def _combine_kernel(dest_ref, x_ref, gate_ref, g_ref, ys_ref, o_ref, buf, sem, *, n):
    tm = x_ref.shape[0]
    i = pl.program_id(0)

    def issue(tile, slot):
        def body(r, carry):
            for k in range(2):
                d = dest_ref[k * n + tile * tm + r]
                pltpu.make_async_copy(_slab(ys_ref, d), _slab(buf.at[slot, k], r),
                                      sem.at[slot]).start(priority=k)
            return carry
        lax.fori_loop(0, tm, body, 0, unroll=8)

    @pl.when(i == 0)
    def _():
        issue(0, 0)

    @pl.when(i + 1 < pl.num_programs(0))
    def _():
        issue(i + 1, (i + 1) % 2)

    slot = i % 2
    for k in range(2):
        pltpu.make_async_copy(buf.at[slot, k], buf.at[slot, k], sem.at[slot]).wait()
    gates = gate_ref[...].T
    f = (gates[:, 0:1] * _load_slabs(buf.at[slot, 0], 0, tm)
         + gates[:, 1:2] * _load_slabs(buf.at[slot, 1], 0, tm))
    o_ref[...] = x_ref[...] + _rms(f, g_ref[...])


def _combine(dest, x2, gates, g_post, ys, *, layer, tm=512):
    n, d = x2.shape
    return pl.pallas_call(
        functools.partial(_combine_kernel, n=n),
        grid_spec=pltpu.PrefetchScalarGridSpec(
            num_scalar_prefetch=1,
            grid=(n // tm,),
            in_specs=[pl.BlockSpec((tm, d), lambda i, dest: (i, 0)),
                      pl.BlockSpec((gates.shape[0], tm), lambda i, dest: (0, i)),
                      _layer_spec(g_post, layer),
                      pl.BlockSpec(memory_space=pl.ANY)],
            out_specs=pl.BlockSpec((tm, d), lambda i, dest: (i, 0)),
            scratch_shapes=[pltpu.VMEM((2, 2, tm * SUBLANES, LANES), F32), pltpu.SemaphoreType.DMA((2,))]),
        out_shape=jax.ShapeDtypeStruct((n, d), F32),
        compiler_params=_params(("arbitrary",)),
        name="moe_combine",
    )(dest, x2, gates, g_post, ys)


def _moe(h, x2, route_i, route_g, counts, wg, wu, wd, g_post, *, layer):
    n = x2.shape[0]
    n_items = 2 * n // MOE_TILE + N_EXPERTS
    tables, order, base = _moe_tables(counts[:, 0], n_items)
    base_of = lambda e: jnp.sum(jnp.where(e[None, :] == jnp.arange(N_EXPERTS)[:, None], base[:, None], 0), axis=0)
    dest = jnp.concatenate([base_of(route_i[0]) + route_i[2], base_of(route_i[1]) + route_i[3]])
    xs = _dispatch(dest, h, (n_items + 1) * MOE_TILE)
    hid = _moe_up(tables, order, xs, wg, wu)
    ys = _moe_down(tables, order, hid, wd)
    return _combine(dest, x2, route_g, g_post, ys, layer=layer)


def _rows(v):
    return v.reshape(v.shape[0], 1, -1).astype(F32)


def kernel(x, norm_mix_pre, norm_mix_post, norm_ffn_pre, norm_ffn_post, w_in, b_forget, pool_w, pool_scale, ssm_a_re, ssm_a_im, ssm_b_re, ssm_b_im, ssm_c_re, ssm_c_im, ssm_d, ssm_log_dt, ssm_w_glu, ssm_b_glu, branch_norm_attn, branch_norm_pool, branch_norm_ssm, w_out, ffn_w_gate, ffn_w_up, ffn_w_down, moe_router, moe_w_gate, moe_w_up, moe_w_down):
    b, L, d = x.shape
    depth = w_in.shape[0]
    n = b * L
    a = ATTN_WIDTH
    attn_tile = 512
    x2 = x.reshape(n, d)
    g_mix_pre, g_mix_post = _rows(norm_mix_pre), _rows(norm_mix_post)
    g_ffn_pre, g_ffn_post = _rows(norm_ffn_pre), _rows(norm_ffn_post)
    g_attn, g_pool, g_ssm = _rows(branch_norm_attn), _rows(branch_norm_pool), _rows(branch_norm_ssm)
    b_f = _rows(jnp.pad(b_forget, ((0, 0), (0, LANES - ATTN_HEADS))))
    pool_bd = jax.vmap(lambda w: jax.scipy.linalg.block_diag(*w))(pool_w).astype(BF16)
    pool_sc, b_glu = _rows(pool_scale), _rows(ssm_b_glu)
    w_glu, w_o = ssm_w_glu.astype(BF16), w_out.astype(BF16)
    mats = jax.vmap(_ssm_matrices)(ssm_a_re, ssm_a_im, ssm_b_re, ssm_b_im, ssm_c_re, ssm_c_im, ssm_d, ssm_log_dt)
    router_t = jnp.swapaxes(moe_router, 1, 2)
    for i in range(depth):
        q, k, v, up, us, c = _inproj(x2, g_mix_pre, w_in, b_f, layer=i, seq=L)

        ct = c.reshape(b, L, ATTN_HEADS).transpose(0, 2, 1).reshape(b, ATTN_HEADS * (L // attn_tile), attn_tile)
        attn = _attention(q.reshape(b, L, a), k.reshape(b, L, a), v.reshape(b, L, a),
                          ct, tq=attn_tile).reshape(n, a)
        pool = _pool(up.reshape(b, L, POOL_WIDTH), pool_bd, pool_sc, layer=i).reshape(n, POOL_WIDTH)
        ys = _ssm_core(us, *mats, layer=i, seq=L)

        moe = i % 2 == 1
        j = i // 2
        res = _mixout(x2, attn, pool, ys, w_glu, b_glu, g_attn, g_pool, g_ssm, w_o, g_mix_post, g_ffn_pre,
                      router_t if moe else None, layer=i, moe_layer=j)
        if moe:
            x2, h, route_i, route_g, counts = res
            x2 = _moe(h, x2, route_i, route_g, counts, moe_w_gate[j], moe_w_up[j], moe_w_down[j],
                      g_ffn_post, layer=i)
        else:
            x2, h = res
            x2 = _ffn(h, x2, ffn_w_gate[j], ffn_w_up[j], ffn_w_down[j], g_ffn_post, layer=i)
    return x2.reshape(b, L, d)
```

```python
import functools
import math

import jax
import jax.numpy as jnp
from jax import lax
from jax.experimental import pallas as pl
from jax.experimental.pallas import tpu as pltpu

F32 = jnp.float32
BF16 = jnp.bfloat16

RMS_EPS = 1e-6
NEG_INF = -1e30
LOG2E = 1.4426950408889634

ATTN_HEADS = 8
ATTN_HEAD_DIM = 64
ATTN_WIDTH = ATTN_HEADS * ATTN_HEAD_DIM
POOL_WINDOWS = (2, 4, 8, 16)
POOL_GROUP_DIM = 64
POOL_WIDTH = len(POOL_WINDOWS) * POOL_GROUP_DIM
SSM_GROUPS = 16
SSM_GROUP_DIM = 16
SSM_STATE = 64
SSM_WIDTH = SSM_GROUPS * SSM_GROUP_DIM
SSM_CHUNK = 8
N_EXPERTS = 8

LANES = 128
SUBLANES = 8
VMEM_LIMIT = 56 * 1024 * 1024


def _params(sem):
    return pltpu.CompilerParams(dimension_semantics=sem, vmem_limit_bytes=VMEM_LIMIT)


def _rms(x, g):
    return x * lax.rsqrt(jnp.mean(x * x, axis=-1, keepdims=True) + RMS_EPS) * g


def _layer_spec(arr, layer):
    return pl.BlockSpec((None,) + arr.shape[1:], lambda *_: (layer,) + (0,) * (arr.ndim - 1))


def _load_slabs(ref, row0, rows):
    return jnp.concatenate([ref[pl.ds(row0 * SUBLANES + s, rows, stride=SUBLANES), :]
                            for s in range(SUBLANES)], axis=1)


def _store_slabs(ref, row0, val):
    for s in range(SUBLANES):
        ref[pl.ds(row0 * SUBLANES + s, val.shape[0], stride=SUBLANES), :] = val[:, s * LANES:(s + 1) * LANES]


def _split3(x):
    hi = x.astype(BF16)
    r = x - hi.astype(F32)
    mid = r.astype(BF16)
    lo = (r - mid.astype(F32)).astype(BF16)
    return hi, mid, lo


INPROJ_STAGE_ROWS = 256
INPROJ_SUB = 256


def _inproj_kernel(x_ref, g_ref, w_hbm, bf_ref, q_ref, k_ref, v_ref, up_ref, us_ref, c_ref,
                   w_s, stage, sem, carry_ref, *, layer, tiles_per_seq):
    a = ATTN_WIDTH
    f_off = 3 * a
    p_off = f_off + ATTN_HEADS
    main = f_off + POOL_WIDTH + SSM_WIDTH

    @pl.when(pl.program_id(0) == 0)
    def _():
        rows = stage.shape[1]

        def copy(c):
            return pltpu.make_async_copy(w_hbm.at[layer, pl.ds(c * rows, rows)], stage.at[c % 2], sem.at[c % 2])

        nchunks = w_s.shape[0] // rows
        copy(0).start()
        for c in range(nchunks):
            if c + 1 < nchunks:
                copy(c + 1).start()
            copy(c).wait()
            blk = stage[c % 2]
            r = pl.ds(c * rows, rows)
            w_s[r, 0:f_off] = blk[:, 0:f_off].astype(BF16)
            w_s[r, f_off:main] = blk[:, p_off:p_off + main - f_off].astype(BF16)
            lane = lax.broadcasted_iota(jnp.int32, (rows, LANES), 1)
            w_s[r, main:] = jnp.where(lane < ATTN_HEADS, blk[:, f_off:f_off + LANES], 0.0).astype(BF16)

    @pl.when(pl.program_id(0) % tiles_per_seq == 0)
    def _():
        carry_ref[...] = jnp.zeros_like(carry_ref)

    tm = x_ref.shape[0]
    zs = []
    for s in range(tm // INPROJ_SUB):
        r = pl.ds(s * INPROJ_SUB, INPROJ_SUB)
        h = _rms(x_ref[r, :], g_ref[...]).astype(BF16)
        proj = jnp.dot(h, w_s[...], preferred_element_type=F32)
        q_ref[r, :] = (proj[:, 0:a] * (LOG2E * ATTN_HEAD_DIM ** -0.5)).astype(BF16)
        k_ref[r, :] = proj[:, a:2 * a].astype(BF16)
        v_ref[r, :] = proj[:, 2 * a:3 * a].astype(BF16)
        up_ref[r, :] = proj[:, 3 * a:3 * a + POOL_WIDTH].astype(BF16)
        us_ref[r, :] = proj[:, 3 * a + POOL_WIDTH:main]
        zs.append(proj[:, main:])

    z = jnp.concatenate(zs, axis=0) + bf_ref[...]
    logf = jnp.minimum(z, 0.0) - jnp.log(1.0 + jnp.exp(-jnp.abs(z)))
    row = lax.broadcasted_iota(jnp.int32, (tm, tm), 0)
    col = lax.broadcasted_iota(jnp.int32, (tm, tm), 1)
    tri = (row >= col).astype(BF16)
    hi, mid, _ = _split3(logf)
    c = (jnp.dot(tri, hi, preferred_element_type=F32)
         + jnp.dot(tri, mid, preferred_element_type=F32)) + carry_ref[...]
    c_ref[...] = c[:, :ATTN_HEADS]
    carry_ref[...] = c[tm - 1:tm, :]


def _inproj(x2, g, w_in, b_f, *, layer, seq, tm=512):
    n, d = x2.shape
    n_in = w_in.shape[2]
    a = ATTN_WIDTH
    n_res = n_in - ATTN_HEADS + LANES
    out_shape = (
        jax.ShapeDtypeStruct((n, a), BF16), jax.ShapeDtypeStruct((n, a), BF16),
        jax.ShapeDtypeStruct((n, a), BF16), jax.ShapeDtypeStruct((n, POOL_WIDTH), BF16),
        jax.ShapeDtypeStruct((n, SSM_WIDTH), F32), jax.ShapeDtypeStruct((n, ATTN_HEADS), F32))
    row = lambda w: pl.BlockSpec((tm, w), lambda i: (i, 0))
    return pl.pallas_call(
        functools.partial(_inproj_kernel, layer=layer, tiles_per_seq=seq // tm),
        grid=(n // tm,),
        in_specs=[row(d), _layer_spec(g, layer), pl.BlockSpec(memory_space=pl.ANY), _layer_spec(b_f, layer)],
        out_specs=(row(a), row(a), row(a), row(POOL_WIDTH), row(SSM_WIDTH), row(ATTN_HEADS)),
        out_shape=out_shape,
        scratch_shapes=[pltpu.VMEM((d, n_res), BF16), pltpu.VMEM((2, INPROJ_STAGE_ROWS, n_in), F32),
                        pltpu.SemaphoreType.DMA((2,)), pltpu.VMEM((1, LANES), F32)],
        compiler_params=_params(("arbitrary",)),
        name="inproj",
    )(x2, g, w_in, b_f)


HEADS_PER_BLOCK = LANES // ATTN_HEAD_DIM


def _attn_kernel(q_ref, k_ref, v_ref, ct_ref, o_ref, *, tq, tk):
    nlb = q_ref.shape[2] // LANES
    hp = pl.program_id(1)
    qi = pl.program_id(2)
    nkb = k_ref.shape[1] // tk
    lane = lax.broadcasted_iota(jnp.int32, (1, LANES), 1)
    in_head = [(lane // ATTN_HEAD_DIM) == hh for hh in range(HEADS_PER_BLOCK)]
    chains = [(lb, hh) for lb in range(nlb) for hh in range(HEADS_PER_BLOCK)]
    qm, crow, cref = [], [], []
    for lb in range(nlb):
        q2 = q_ref[0, :, lb * LANES:(lb + 1) * LANES]
        qm.append(jnp.concatenate([jnp.where(msk, q2, jnp.zeros_like(q2)) for msk in in_head], axis=0))
    for lb, hh in chains:
        crow.append(((hp * nlb + lb) * HEADS_PER_BLOCK + hh) * nkb)
        cref.append(ct_ref[0, pl.ds(crow[-1] + qi, 1), :][:, 0:1])

    def step(kb, carry, masked):
        new = []
        scores = {}
        for ci, (lb, hh) in enumerate(chains):
            kblk = k_ref[0, pl.ds(kb * tk, tk), lb * LANES:(lb + 1) * LANES]
            vblk = v_ref[0, pl.ds(kb * tk, tk), lb * LANES:(lb + 1) * LANES]
            m, acc = carry[ci]
            if lb not in scores:
                scores[lb] = lax.dot_general(qm[lb], kblk, (((1,), (1,)), ((), ())), preferred_element_type=F32)
            s = scores[lb][hh * tq:(hh + 1) * tq, :]
            s = s - (ct_ref[0, pl.ds(crow[ci] + kb, 1), :] - cref[ci]) * LOG2E
            if masked:
                r = lax.broadcasted_iota(jnp.int32, (tq, tk), 0)
                cidx = lax.broadcasted_iota(jnp.int32, (tq, tk), 1)
                s = jnp.where(cidx <= r, s, NEG_INF)
            m_new = jnp.maximum(m, jnp.max(s, axis=1, keepdims=True))
            p = jnp.exp2(s - m_new)
            vsel = jnp.where(in_head[hh], vblk, jnp.ones_like(vblk))
            acc = jnp.exp2(m - m_new) * acc + jnp.dot(p.astype(BF16), vsel, preferred_element_type=F32)
            new.append((m_new, acc))
        return tuple(new)

    init = tuple((jnp.full((tq, 1), NEG_INF, F32), jnp.zeros((tq, LANES), F32)) for _ in chains)
    carry = lax.fori_loop(0, qi, lambda kb, cr: step(kb, cr, False), init)
    carry = step(qi, carry, True)
    for lb in range(nlb):
        out = jnp.zeros((tq, LANES), F32)
        for hh in range(HEADS_PER_BLOCK):
            acc = carry[lb * HEADS_PER_BLOCK + hh][1]
            out = jnp.where(in_head[hh], acc / pltpu.roll(acc, ATTN_HEAD_DIM, 1), out)
        o_ref[0, :, lb * LANES:(lb + 1) * LANES] = out.astype(o_ref.dtype)


def _attention(q, k, v, ct, *, tq, lane_blocks=4):
    b, L, a = q.shape
    tk = ct.shape[2]
    assert tq == tk, "the diagonal key block of query tile i must be key block i"
    w = lane_blocks * LANES
    blk = lambda: pl.BlockSpec((1, tq, w), lambda bi, hi, qi: (bi, qi, hi))
    seq = lambda: pl.BlockSpec((1, L, w), lambda bi, hi, qi: (bi, 0, hi))
    return pl.pallas_call(
        functools.partial(_attn_kernel, tq=tq, tk=tk),
        grid=(b, a // w, L // tq),
        in_specs=[blk(), seq(), seq(),
                  pl.BlockSpec((1, ct.shape[1], tk), lambda bi, hi, qi: (bi, 0, 0))],
        out_specs=blk(),
        out_shape=jax.ShapeDtypeStruct((b, L, a), BF16),
        compiler_params=_params(("arbitrary", "arbitrary", "arbitrary")),
        name="fox_attention",
    )(q, k, v, ct)


def _pool_kernel(u_ref, w_ref, s_ref, o_ref):
    x = u_ref[0].astype(F32)
    L, w = x.shape
    row = lax.broadcasted_iota(jnp.int32, (L, w), 0)
    group = lax.broadcasted_iota(jnp.int32, (L, w), 1) // POOL_GROUP_DIM

    def shifted(y, s):
        return jnp.where(row >= s, pltpu.roll(y, s, 0), 0.0)

    acc = x
    sel = jnp.zeros_like(x)
    win_lane = jnp.zeros((L, w), F32)
    span = 1
    for gi, win in enumerate(POOL_WINDOWS):
        while span < win:
            acc = acc + shifted(acc, span)
            span *= 2
        sel = jnp.where(group == gi, acc, sel)
        win_lane = jnp.where(group == gi, float(win), win_lane)
    count = jnp.minimum(row.astype(F32) + 1.0, win_lane)
    d = sel / count - x
    y = jnp.dot(d.astype(BF16), w_ref[...], preferred_element_type=F32) * s_ref[...]
    o_ref[0] = y.astype(o_ref.dtype)


def _pool(u, w_bd, scale, *, layer):
    b, L, w = u.shape
    return pl.pallas_call(
        _pool_kernel,
        grid=(b,),
        in_specs=[pl.BlockSpec((1, L, w), lambda i: (i, 0, 0)),
                  _layer_spec(w_bd, layer), _layer_spec(scale, layer)],
        out_specs=pl.BlockSpec((1, L, w), lambda i: (i, 0, 0)),
        out_shape=jax.ShapeDtypeStruct((b, L, w), BF16),
        compiler_params=_params(("arbitrary",)),
        name="pool_mixer",
    )(u, w_bd, scale)


def _group_of(idx, width):
    groups = LANES // SSM_GROUP_DIM
    return (idx >> (width.bit_length() - 1)) & (groups - 1)


def _expand_blockdiag(r_ref, row_inner, col_inner):
    rows, w = r_ref.shape[1:]
    groups = LANES // SSM_GROUP_DIM
    r = lax.broadcasted_iota(jnp.int32, (w, groups * w), 0)
    c = lax.broadcasted_iota(jnp.int32, (w, groups * w), 1)
    shift = col_inner.bit_length() - 1
    outer_c = c >> (shift + groups.bit_length() - 1)
    spread = (outer_c == (r >> shift)) & ((c & (col_inner - 1)) == (r & (col_inner - 1)))
    big = jnp.dot(r_ref[0], spread.astype(BF16), preferred_element_type=F32)
    ri = lax.broadcasted_iota(jnp.int32, (rows, groups * w), 0)
    ci = lax.broadcasted_iota(jnp.int32, (rows, groups * w), 1)
    return jnp.where(_group_of(ri, row_inner) == _group_of(ci, col_inner), big, 0.0).astype(BF16)


def _ssm_kernel(u_ref, rm_ref, rz_ref, ry_ref, coef_ref, y_ref, m_s, wz_s, wy_s, ucat, z_ref, s_ref, *, seqs):
    t = SSM_CHUNK

    @pl.when(pl.program_id(1) == 0)
    def _():
        m_s[...] = _expand_blockdiag(rm_ref, SSM_GROUP_DIM, SSM_GROUP_DIM)
        wz_s[...] = _expand_blockdiag(rz_ref, SSM_GROUP_DIM, SSM_STATE)
        wy_s[...] = _expand_blockdiag(ry_ref, SSM_STATE, SSM_GROUP_DIM)

    nc = u_ref.shape[0] // t
    cps = nc // seqs
    for k in range(t):
        ucat[:, k * LANES:(k + 1) * LANES] = u_ref[pl.ds(k, nc, stride=t), :].astype(BF16)
    u = ucat[...]
    z_ref[...] = jnp.dot(u, wz_s[...], preferred_element_type=F32)
    half = z_ref.shape[1] // 2
    a1 = coef_ref[0, 0:1, :]
    a2 = coef_ref[0, 1:2, :]

    def step(c, states):
        new = []
        for b in range(seqs):
            st = states[b]
            r = b * cps + c
            s_ref[pl.ds(r, 1), :] = st
            sw = jnp.concatenate([st[:, half:], st[:, :half]], axis=1)
            new.append(a1 * st + a2 * sw + z_ref[pl.ds(r, 1), :])
        return tuple(new)

    init = tuple(jnp.zeros((1, 2 * half), F32) for _ in range(seqs))
    lax.fori_loop(0, cps, step, init, unroll=8)
    y = jnp.dot(u, m_s[...], preferred_element_type=F32)
    y = y + jnp.dot(s_ref[...].astype(BF16), wy_s[...], preferred_element_type=F32)
    for k in range(t):
        y_ref[pl.ds(k, nc, stride=t), :] = y[:, k * LANES:(k + 1) * LANES]


def _ssm_core(u, rm, rz, ry, coef, *, layer, seq, max_seqs_per_block=4):
    n, w = u.shape
    seqs_per_block = math.gcd(max_seqs_per_block, n // seq)
    rows = seq * seqs_per_block
    nc = rows // SSM_CHUNK
    kw = SSM_CHUNK * LANES
    p2 = coef.shape[3]
    blk = pl.BlockSpec((rows, LANES), lambda hf, r: (r, hf))
    wspec = lambda arr: pl.BlockSpec((None, 1) + arr.shape[2:], lambda hf, r: (layer, hf, 0, 0))
    return pl.pallas_call(
        functools.partial(_ssm_kernel, seqs=seqs_per_block),
        grid=(w // LANES, n // rows),
        in_specs=[blk, wspec(rm), wspec(rz), wspec(ry), wspec(coef)],
        out_specs=blk,
        out_shape=jax.ShapeDtypeStruct((n, w), F32),
        scratch_shapes=[pltpu.VMEM((kw, kw), BF16), pltpu.VMEM((kw, p2), BF16), pltpu.VMEM((p2, kw), BF16),
                        pltpu.VMEM((nc, kw), BF16), pltpu.VMEM((nc, p2), F32), pltpu.VMEM((nc, p2), F32)],
        compiler_params=_params(("arbitrary", "arbitrary")),
        name="s5_core",
    )(u, rm, rz, ry, coef)


def _ssm_matrices(a_re, a_im, b_re, b_im, c_re, c_im, d, log_dt):
    t = SSM_CHUNK
    g, p = a_re.shape
    hdim = d.shape[1]
    gl = LANES // hdim
    nb = g // gl
    lam = lax.complex(a_re, a_im)
    lam_dt = lam * jnp.exp(log_dt)[:, None]
    lam_bar = jnp.exp(lam_dt)
    b_bar = ((lam_bar - 1.0) / lam)[:, :, None] * lax.complex(b_re, b_im)
    cmat = lax.complex(c_re, c_im)
    steps = jnp.arange(t + 1, dtype=F32)
    pw = jnp.exp(lam_dt[:, None, :] * steps[None, :, None])
    kern = jnp.einsum('gop,gdp,gpi->gdoi', cmat, pw[:, :t], b_bar).real
    kern = kern.at[:, 0].add(jax.vmap(jnp.diag)(d))
    j = jnp.arange(t)[:, None]
    i = jnp.arange(t)[None, :]
    m = jnp.where((i >= j)[None, :, :, None, None], kern[:, jnp.clip(i - j, 0, t - 1)], 0.0)
    wz = jnp.einsum('gjp,gpi->gjip', pw[:, t - 1 - jnp.arange(t)], b_bar)
    cw = jnp.einsum('gop,gip->giop', cmat, pw[:, 1:t + 1])
    rm = m.reshape(nb, gl, t, t, hdim, hdim).transpose(0, 2, 1, 5, 3, 4).reshape(nb, t * LANES, t * hdim)
    rz = jnp.stack([wz.real, wz.imag], axis=3).reshape(nb, gl, t, hdim, 2, p)
    rz = rz.transpose(0, 2, 1, 3, 4, 5).reshape(nb, t * LANES, 2 * p)
    ry = jnp.stack([cw.real, -cw.imag], axis=1).reshape(nb, gl, 2, t, hdim, p)
    ry = ry.transpose(0, 2, 1, 5, 3, 4).reshape(nb, 2 * gl * p, t * hdim)
    ar = pw[:, t].real.reshape(nb, gl * p)
    ai = pw[:, t].imag.reshape(nb, gl * p)
    coef = jnp.stack([jnp.concatenate([ar, ar], -1), jnp.concatenate([-ai, ai], -1)], axis=1)
    return rm.astype(BF16), rz.astype(BF16), ry.astype(BF16), coef.astype(F32)


MIX_SUB = 256


def _mixout_kernel(*refs, moe):
    (x_ref, attn_ref, pool_ref, ys_ref, wglu_ref, bglu_ref, ga_ref, gp_ref, gs_ref,
     wo_ref, gpost_ref, gpre_ref) = refs[:12]
    if moe:
        router_ref, xo_ref, h_ref, ri_ref, rg_ref, cnt_ref, cnt_acc = refs[12:]

        @pl.when(pl.program_id(0) == 0)
        def _():
            cnt_acc[...] = jnp.zeros_like(cnt_acc)
    else:
        xo_ref, h_ref = refs[12:]
    a = attn_ref.shape[1]
    pw = pool_ref.shape[1]
    h_parts = []
    for s in range(x_ref.shape[0] // MIX_SUB):
        r = pl.ds(s * MIX_SUB, MIX_SUB)
        attn_n = _rms(attn_ref[r, :].astype(F32), ga_ref[...]).astype(BF16)
        pool_n = _rms(pool_ref[r, :].astype(F32), gp_ref[...]).astype(BF16)
        y = jax.nn.gelu(ys_ref[r, :])
        gate = jnp.dot(y.astype(BF16), wglu_ref[...], preferred_element_type=F32) + bglu_ref[...]
        ssm = y * jax.nn.sigmoid(gate)
        ssm_n = _rms(ssm, gs_ref[...]).astype(BF16)
        mix = (jnp.dot(attn_n, wo_ref[0:a, :], preferred_element_type=F32)
               + jnp.dot(pool_n, wo_ref[a:a + pw, :], preferred_element_type=F32)
               + jnp.dot(ssm_n, wo_ref[a + pw:, :], preferred_element_type=F32))
        x = x_ref[r, :] + _rms(mix, gpost_ref[...])
        xo_ref[r, :] = x
        h = _rms(x, gpre_ref[...])
        if moe:
            _store_slabs(h_ref, s * MIX_SUB, h)
            h_parts.append(h)
        else:
            h_ref[r, :] = h.astype(h_ref.dtype)
    if moe:
        hi, mid, _ = _split3(jnp.concatenate(h_parts, axis=0))
        rhi, rmid, _ = _split3(router_ref[...])
        dot_t = lambda p, q: lax.dot_general(p, q, (((1,), (1,)), ((), ())), preferred_element_type=F32)
        logits = dot_t(rhi, hi) + dot_t(rmid, hi) + dot_t(rhi, mid)
        ne, tm = logits.shape
        eidx = lax.broadcasted_iota(jnp.int32, (ne, tm), 0)
        m1 = jnp.max(logits, axis=0, keepdims=True)
        i1 = jnp.min(jnp.where(logits == m1, eidx, ne), axis=0, keepdims=True)
        rest = jnp.where(eidx == i1, -jnp.inf, logits)
        m2 = jnp.max(rest, axis=0, keepdims=True)
        i2 = jnp.min(jnp.where(rest == m2, eidx, ne), axis=0, keepdims=True)
        e2 = jnp.exp(m2 - m1)
        g1 = 1.0 / (1.0 + e2)
        rg_ref[...] = jnp.where(eidx == 0, g1, jnp.where(eidx == 1, e2 * g1, 0.0))
        onehot = ((eidx == i1) | (eidx == i2)).astype(BF16)
        earlier = (lax.broadcasted_iota(jnp.int32, (tm, tm), 0)
                   < lax.broadcasted_iota(jnp.int32, (tm, tm), 1)).astype(BF16)
        before = jnp.dot(onehot, earlier, preferred_element_type=F32) + cnt_acc[:, 0:1]
        r1 = jnp.sum(jnp.where(eidx == i1, before, 0.0), axis=0, keepdims=True).astype(jnp.int32)
        r2 = jnp.sum(jnp.where(eidx == i2, before, 0.0), axis=0, keepdims=True).astype(jnp.int32)
        ri_ref[...] = jnp.where(eidx == 0, i1, jnp.where(eidx == 1, i2,
                                jnp.where(eidx == 2, r1, jnp.where(eidx == 3, r2, 0))))
        cnt_acc[...] += jnp.sum(onehot.astype(F32), axis=1, keepdims=True)
        cnt_ref[...] = cnt_acc[...]


def _mixout(x2, attn, pool, ys, w_glu, b_glu, g_attn, g_pool, g_ssm, w_out, g_post, g_pre,
            router=None, *, layer, moe_layer=None, tm=512):
    n, d = x2.shape
    moe = router is not None
    row = lambda w: pl.BlockSpec((tm, w), lambda i: (i, 0))
    ins = [x2, attn, pool, ys, w_glu, b_glu, g_attn, g_pool, g_ssm, w_out, g_post, g_pre]
    in_specs = ([row(d), row(attn.shape[1]), row(pool.shape[1]), row(ys.shape[1])]
                + [_layer_spec(t, layer) for t in ins[4:]])
    if moe:
        assert d == SUBLANES * LANES
        h_shape = jax.ShapeDtypeStruct((n * SUBLANES, LANES), F32)
        h_spec = pl.BlockSpec((tm * SUBLANES, LANES), lambda i: (i, 0))
    else:
        h_shape, h_spec = jax.ShapeDtypeStruct((n, d), BF16), row(d)
    out_shape = [jax.ShapeDtypeStruct((n, d), F32), h_shape]
    out_specs = [row(d), h_spec]
    scratch = []
    if moe:
        ins.append(router)
        in_specs.append(_layer_spec(router, moe_layer))
        ne = router.shape[1]
        col = pl.BlockSpec((ne, tm), lambda i: (0, i))
        out_shape += [jax.ShapeDtypeStruct((ne, n), jnp.int32), jax.ShapeDtypeStruct((ne, n), F32),
                      jax.ShapeDtypeStruct((ne, LANES), F32)]
        out_specs += [col, col, pl.BlockSpec((ne, LANES), lambda i: (0, 0))]
        scratch = [pltpu.VMEM((ne, LANES), F32)]
    return pl.pallas_call(
        functools.partial(_mixout_kernel, moe=moe),
        grid=(n // tm,),
        in_specs=in_specs,
        out_specs=tuple(out_specs),
        out_shape=tuple(out_shape),
        scratch_shapes=scratch,
        compiler_params=_params(("arbitrary",)),
        name="mix_out",
    )(*ins)


FFN_STAGE_CHUNKS = 4


def _stage_cast(src_hbm, dst, lane0, stage, sem):
    rows = stage.shape[1]
    width = src_hbm.shape[1]

    def copy(c):
        return pltpu.make_async_copy(src_hbm.at[pl.ds(c * rows, rows)], stage.at[c % 2], sem.at[c % 2])

    nchunks = src_hbm.shape[0] // rows
    copy(0).start()
    for c in range(nchunks):
        if c + 1 < nchunks:
            copy(c + 1).start()
        copy(c).wait()
        dst[pl.ds(c * rows, rows), lane0:lane0 + width] = stage[c % 2].astype(BF16)


def _ffn_kernel(h_ref, x_ref, wg_hbm, wu_hbm, wd_hbm, g_ref, o_ref, wgu_s, wd_s, stage_in, stage_out, sem, *, sub):
    fdim = wd_s.shape[0]

    @pl.when(pl.program_id(0) == 0)
    def _():
        _stage_cast(wg_hbm, wgu_s, 0, stage_in, sem)
        _stage_cast(wu_hbm, wgu_s, fdim, stage_in, sem)
        _stage_cast(wd_hbm, wd_s, 0, stage_out, sem)

    for r in range(h_ref.shape[0] // sub):
        rows = pl.ds(r * sub, sub)
        hgu = jnp.dot(h_ref[rows, :], wgu_s[...], preferred_element_type=F32)
        hg, hu = hgu[:, :fdim], hgu[:, fdim:]
        act = (hg * jax.nn.sigmoid(hg) * hu).astype(BF16)
        y = jnp.dot(act, wd_s[...], preferred_element_type=F32)
        o_ref[rows, :] = x_ref[rows, :] + _rms(y, g_ref[...])


def _ffn(h, x2, wg, wu, wd, g_post, *, layer, tm=512, sub=256):
    n, d = x2.shape
    fdim = wg.shape[1]
    row = pl.BlockSpec((tm, d), lambda i: (i, 0))
    hbm = pl.BlockSpec(memory_space=pl.ANY)
    return pl.pallas_call(
        functools.partial(_ffn_kernel, sub=sub),
        grid=(n // tm,),
        in_specs=[row, row, hbm, hbm, hbm, _layer_spec(g_post, layer)],
        out_specs=row,
        out_shape=jax.ShapeDtypeStruct((n, d), F32),
        scratch_shapes=[pltpu.VMEM((d, 2 * fdim), BF16), pltpu.VMEM((fdim, d), BF16),
                        pltpu.VMEM((2, d // FFN_STAGE_CHUNKS, fdim), F32),
                        pltpu.VMEM((2, fdim // FFN_STAGE_CHUNKS, d), F32),
                        pltpu.SemaphoreType.DMA((2,))],
        compiler_params=_params(("arbitrary",)),
        name="dense_ffn",
    )(h, x2, wg, wu, wd, g_post)


MOE_TILE = 1024


def _moe_tables(counts, n_items):
    tm = MOE_TILE
    i32 = jnp.int32
    ne = counts.shape[0]
    assert n_items <= LANES
    counts = counts.astype(i32)
    ntiles = (counts + tm - 1) // tm
    k = jnp.arange(ne, dtype=i32)
    ends = jnp.sum(jnp.where(k[None, :] <= k[:, None], ntiles[None, :], 0), axis=1)
    starts = ends - ntiles
    later = (ntiles > 0)[None, :] & (k[None, :] > k[:, None])
    nxt = jnp.min(jnp.where(later, k[None, :], ne), axis=1)
    nxt = jnp.where(nxt < ne, nxt, -1)
    w = jnp.arange(LANES, dtype=i32)[None, :]
    wc = jnp.minimum(w, ends[-1] - 1)
    e = jnp.sum((wc >= ends[:, None]).astype(i32), axis=0, keepdims=True)
    of_item = lambda v: jnp.sum(jnp.where(k[:, None] == e, v[:, None], 0), axis=0, keepdims=True)
    valid = w < ends[-1]
    rowblock = jnp.where(valid, w, n_items)
    nvalid = jnp.where(valid, jnp.clip(of_item(counts) - (wc - of_item(starts)) * tm, 0, tm), 0)
    e_prev = jnp.concatenate([e[:, :1], e[:, :-1]], axis=1)
    first = ((w == 0) | (e != e_prev)).astype(i32)
    item = lambda v: v[0, :n_items].astype(i32)
    return (item(e), item(rowblock), item(nvalid), item(first), item(of_item(nxt))), starts * tm


def _slab(ref, token):
    return ref.at[pl.ds(pl.multiple_of(token * SUBLANES, SUBLANES), SUBLANES)]


def _dispatch_kernel(dest_ref, h_ref, xs_ref, sem, *, n):
    tm = h_ref.shape[0] // SUBLANES
    base = pl.program_id(0) * tm

    def body(r, carry):
        for k in range(2):
            d = dest_ref[k * n + base + r]
            pltpu.make_async_copy(_slab(h_ref, r), _slab(xs_ref, d), sem).start(priority=k)
        return carry

    lax.fori_loop(0, tm, body, 0, unroll=8)
    for k in range(2):
        pltpu.make_async_copy(h_ref, h_ref, sem).wait()


def _dispatch(dest, h, cap_rows, *, tm=512):
    n = h.shape[0] // SUBLANES
    return pl.pallas_call(
        functools.partial(_dispatch_kernel, n=n),
        grid_spec=pltpu.PrefetchScalarGridSpec(
            num_scalar_prefetch=1,
            grid=(n // tm,),
            in_specs=[pl.BlockSpec((tm * SUBLANES, LANES), lambda i, dest: (i, 0))],
            out_specs=pl.BlockSpec(memory_space=pl.ANY),
            scratch_shapes=[pltpu.SemaphoreType.DMA(())]),
        out_shape=jax.ShapeDtypeStruct((cap_rows * SUBLANES, LANES), F32),
        compiler_params=_params(("arbitrary",)),
        name="moe_dispatch",
    )(dest, h)


MOE_SUB = 256


MOE_UP_CHUNKS = 4


def _moe_up_kernel(ie_ref, rb_ref, nv_ref, first_ref, nxt_ref, xs_ref, wg_hbm, wu_hbm, h_ref, wgu_s, stage, sem):
    f = pl.program_id(0)
    w = pl.program_id(1)
    nf = pl.num_programs(0)
    tf = h_ref.shape[1]
    rows = stage.shape[0] // MOE_UP_CHUNKS

    def copies(e, fp):
        return [pltpu.make_async_copy(src.at[e, pl.ds(c * rows, rows), pl.ds(fp * tf, tf)],
                                      stage.at[pl.ds(c * rows, rows), pl.ds(m * tf, tf)],
                                      sem.at[m * MOE_UP_CHUNKS + c])
                for m, src in enumerate((wg_hbm, wu_hbm)) for c in range(MOE_UP_CHUNKS)]

    @pl.when((f == 0) & (w == 0))
    def _():
        for cp in copies(ie_ref[0], 0):
            cp.start()

    is_first = first_ref[w] == 1
    last_expert = nxt_ref[w] < 0
    for fp in range(nf):
        @pl.when(is_first & (f == fp))
        def _():
            for cp in copies(ie_ref[w], fp):
                cp.wait()
            for c in range(MOE_UP_CHUNKS):
                r = pl.ds(c * rows, rows)
                wgu_s[r, :] = stage[r, :].astype(BF16)

            @pl.when(jnp.logical_not(last_expert))
            def _():
                for cp in copies(nxt_ref[w], fp):
                    cp.start()

            if fp + 1 < nf:
                @pl.when(last_expert)
                def _():
                    for cp in copies(ie_ref[0], fp + 1):
                        cp.start()

    nv = nv_ref[w]

    nsub = h_ref.shape[0] // MOE_SUB

    def compute(nchains):
        for r in range(nchains):
            rws = pl.ds(r * MOE_SUB, MOE_SUB)
            idx = r * MOE_SUB + lax.broadcasted_iota(jnp.int32, (MOE_SUB, 1), 0)
            x = _load_slabs(xs_ref, r * MOE_SUB, MOE_SUB)
            x = jnp.where(idx < nv, x, 0.0).astype(BF16)
            hgu = jnp.dot(x, wgu_s[...], preferred_element_type=F32)
            hg, hu = hgu[:, :tf], hgu[:, tf:]
            h_ref[rws, :] = (hg * jax.nn.sigmoid(hg) * hu).astype(BF16)
        if nchains < nsub:
            rest = pl.ds(nchains * MOE_SUB, (nsub - nchains) * MOE_SUB)
            h_ref[rest, :] = jnp.zeros((rest.size, tf), BF16)

    for nchains in range(nsub + 1):
        pl.when((nv > (nchains - 1) * MOE_SUB) & (nv <= nchains * MOE_SUB))(functools.partial(compute, nchains))


def _moe_up(tables, xs, wg, wu, *, tf=1792):
    tm = MOE_TILE
    cap_rows = xs.shape[0] // SUBLANES
    ne, d, fdim = wg.shape
    n_items = tables[0].shape[0]
    hbm = pl.BlockSpec(memory_space=pl.ANY)
    return pl.pallas_call(
        _moe_up_kernel,
        grid_spec=pltpu.PrefetchScalarGridSpec(
            num_scalar_prefetch=5,
            grid=(fdim // tf, n_items),
            in_specs=[pl.BlockSpec((tm * SUBLANES, LANES), lambda f, w, ie, rb, *_: (rb[w], 0)), hbm, hbm],
            out_specs=pl.BlockSpec((tm, tf), lambda f, w, ie, rb, *_: (rb[w], f)),
            scratch_shapes=[pltpu.VMEM((d, 2 * tf), BF16), pltpu.VMEM((d, 2 * tf), F32),
                            pltpu.SemaphoreType.DMA((2 * MOE_UP_CHUNKS,))]),
        out_shape=jax.ShapeDtypeStruct((cap_rows, fdim), BF16),
        compiler_params=_params(("arbitrary", "arbitrary")),
        name="moe_up",
    )(*tables, xs, wg, wu)


def _moe_down_kernel(ie_ref, rb_ref, nv_ref, first_ref, nxt_ref, h_ref, wd_hbm, y_ref, wd_s, stage, sem):
    w = pl.program_id(0)
    nchunks = sem.shape[0]
    chunk = stage.shape[0] // nchunks

    def copies(e):
        return [pltpu.make_async_copy(wd_hbm.at[e, pl.ds(c * chunk, chunk)], stage.at[pl.ds(c * chunk, chunk)],
                                      sem.at[c]) for c in range(nchunks)]

    @pl.when(w == 0)
    def _():
        for cp in copies(ie_ref[0]):
            cp.start()

    @pl.when(first_ref[w] == 1)
    def _():
        for c, cp in enumerate(copies(ie_ref[w])):
            cp.wait()
            wd_s[pl.ds(c * chunk, chunk), :] = stage[pl.ds(c * chunk, chunk), :].astype(BF16)

        @pl.when(nxt_ref[w] >= 0)
        def _():
            for cp in copies(nxt_ref[w]):
                cp.start()

    nv = nv_ref[w]
    nsub = h_ref.shape[0] // MOE_SUB

    def compute(nchains):
        for r in range(nchains):
            y = jnp.dot(h_ref[pl.ds(r * MOE_SUB, MOE_SUB), :], wd_s[...], preferred_element_type=F32)
            _store_slabs(y_ref, r * MOE_SUB, y)
        if nchains < nsub:
            rest = pl.ds(nchains * MOE_SUB * SUBLANES, (nsub - nchains) * MOE_SUB * SUBLANES)
            y_ref[rest, :] = jnp.zeros((rest.size, LANES), F32)

    for nchains in range(nsub + 1):
        pl.when((nv > (nchains - 1) * MOE_SUB) & (nv <= nchains * MOE_SUB))(functools.partial(compute, nchains))


def _moe_down(tables, hid, wd, *, chunk=512):
    tm = MOE_TILE
    cap_rows, fdim = hid.shape
    d = wd.shape[2]
    n_items = tables[0].shape[0]
    assert d == SUBLANES * LANES and fdim % chunk == 0
    return pl.pallas_call(
        _moe_down_kernel,
        grid_spec=pltpu.PrefetchScalarGridSpec(
            num_scalar_prefetch=5,
            grid=(n_items,),
            in_specs=[pl.BlockSpec((tm, fdim), lambda w, ie, rb, *_: (rb[w], 0)),
                      pl.BlockSpec(memory_space=pl.ANY)],
            out_specs=pl.BlockSpec((tm * SUBLANES, LANES), lambda w, ie, rb, *_: (rb[w], 0)),
            scratch_shapes=[pltpu.VMEM((fdim, d), BF16), pltpu.VMEM((fdim, d), F32),
                            pltpu.SemaphoreType.DMA((fdim // chunk,))]),
        out_shape=jax.ShapeDtypeStruct((cap_rows * SUBLANES, LANES), F32),
        compiler_params=_params(("arbitrary",)),
        name="moe_down",
    )(*tables, hid, wd)


def _combine_kernel(dest_ref, x_ref, gate_ref, g_ref, ys_ref, o_ref, buf, sem, *, n):
    tm = x_ref.shape[0]
    i = pl.program_id(0)

    def issue(tile, slot):
        def body(r, carry):
            for k in range(2):
                d = dest_ref[k * n + tile * tm + r]
                pltpu.make_async_copy(_slab(ys_ref, d), _slab(buf.at[slot, k], r),
                                      sem.at[slot]).start(priority=k)
            return carry
        lax.fori_loop(0, tm, body, 0, unroll=8)

    @pl.when(i == 0)
    def _():
        issue(0, 0)

    @pl.when(i + 1 < pl.num_programs(0))
    def _():
        issue(i + 1, (i + 1) % 2)

    slot = i % 2
    for k in range(2):
        pltpu.make_async_copy(buf.at[slot, k], buf.at[slot, k], sem.at[slot]).wait()
    gates = gate_ref[...].T
    f = (gates[:, 0:1] * _load_slabs(buf.at[slot, 0], 0, tm)
         + gates[:, 1:2] * _load_slabs(buf.at[slot, 1], 0, tm))
    o_ref[...] = x_ref[...] + _rms(f, g_ref[...])


def _combine(dest, x2, gates, g_post, ys, *, layer, tm=512):
    n, d = x2.shape
    return pl.pallas_call(
        functools.partial(_combine_kernel, n=n),
        grid_spec=pltpu.PrefetchScalarGridSpec(
            num_scalar_prefetch=1,
            grid=(n // tm,),
            in_specs=[pl.BlockSpec((tm, d), lambda i, dest: (i, 0)),
                      pl.BlockSpec((gates.shape[0], tm), lambda i, dest: (0, i)),
                      _layer_spec(g_post, layer),
                      pl.BlockSpec(memory_space=pl.ANY)],
            out_specs=pl.BlockSpec((tm, d), lambda i, dest: (i, 0)),
            scratch_shapes=[pltpu.VMEM((2, 2, tm * SUBLANES, LANES), F32), pltpu.SemaphoreType.DMA((2,))]),
        out_shape=jax.ShapeDtypeStruct((n, d), F32),
        compiler_params=_params(("arbitrary",)),
        name="moe_combine",
    )(dest, x2, gates, g_post, ys)


def _moe(h, x2, route_i, route_g, counts, wg, wu, wd, g_post, *, layer):
    n = x2.shape[0]
    n_items = 2 * n // MOE_TILE + N_EXPERTS
    tables, base = _moe_tables(counts[:, 0], n_items)
    base_of = lambda e: jnp.sum(jnp.where(e[None, :] == jnp.arange(N_EXPERTS)[:, None], base[:, None], 0), axis=0)
    dest = jnp.concatenate([base_of(route_i[0]) + route_i[2], base_of(route_i[1]) + route_i[3]])
    xs = _dispatch(dest, h, (n_items + 1) * MOE_TILE)
    hid = _moe_up(tables, xs, wg, wu)
    ys = _moe_down(tables, hid, wd)
    return _combine(dest, x2, route_g, g_post, ys, layer=layer)


def _rows(v):
    return v.reshape(v.shape[0], 1, -1).astype(F32)


def kernel(x, norm_mix_pre, norm_mix_post, norm_ffn_pre, norm_ffn_post, w_in, b_forget, pool_w, pool_scale, ssm_a_re, ssm_a_im, ssm_b_re, ssm_b_im, ssm_c_re, ssm_c_im, ssm_d, ssm_log_dt, ssm_w_glu, ssm_b_glu, branch_norm_attn, branch_norm_pool, branch_norm_ssm, w_out, ffn_w_gate, ffn_w_up, ffn_w_down, moe_router, moe_w_gate, moe_w_up, moe_w_down):
    b, L, d = x.shape
    depth = w_in.shape[0]
    n = b * L
    a = ATTN_WIDTH
    attn_tile = 512
    x2 = x.reshape(n, d)
    g_mix_pre, g_mix_post = _rows(norm_mix_pre), _rows(norm_mix_post)
    g_ffn_pre, g_ffn_post = _rows(norm_ffn_pre), _rows(norm_ffn_post)
    g_attn, g_pool, g_ssm = _rows(branch_norm_attn), _rows(branch_norm_pool), _rows(branch_norm_ssm)
    b_f = _rows(jnp.pad(b_forget, ((0, 0), (0, LANES - ATTN_HEADS))))
    pool_bd = jax.vmap(lambda w: jax.scipy.linalg.block_diag(*w))(pool_w).astype(BF16)
    pool_sc, b_glu = _rows(pool_scale), _rows(ssm_b_glu)
    w_glu, w_o = ssm_w_glu.astype(BF16), w_out.astype(BF16)
    mats = jax.vmap(_ssm_matrices)(ssm_a_re, ssm_a_im, ssm_b_re, ssm_b_im, ssm_c_re, ssm_c_im, ssm_d, ssm_log_dt)
    router_t = jnp.swapaxes(moe_router, 1, 2)
    for i in range(depth):
        q, k, v, up, us, c = _inproj(x2, g_mix_pre, w_in, b_f, layer=i, seq=L)

        ct = c.reshape(b, L, ATTN_HEADS).transpose(0, 2, 1).reshape(b, ATTN_HEADS * (L // attn_tile), attn_tile)
        attn = _attention(q.reshape(b, L, a), k.reshape(b, L, a), v.reshape(b, L, a),
                          ct, tq=attn_tile).reshape(n, a)
        pool = _pool(up.reshape(b, L, POOL_WIDTH), pool_bd, pool_sc, layer=i).reshape(n, POOL_WIDTH)
        ys = _ssm_core(us, *mats, layer=i, seq=L)

        moe = i % 2 == 1
        j = i // 2
        res = _mixout(x2, attn, pool, ys, w_glu, b_glu, g_attn, g_pool, g_ssm, w_o, g_mix_post, g_ffn_pre,
                      router_t if moe else None, layer=i, moe_layer=j)
        if moe:
            x2, h, route_i, route_g, counts = res
            x2 = _moe(h, x2, route_i, route_g, counts, moe_w_gate[j], moe_w_up[j], moe_w_down[j],
                      g_ffn_post, layer=i)
        else:
            x2, h = res
            x2 = _ffn(h, x2, ffn_w_gate[j], ffn_w_up[j], ffn_w_down[j], g_ffn_post, layer=i)
    return x2.reshape(b, L, d)
```

```python
import functools
import math

import jax
import jax.numpy as jnp
from jax import lax
from jax.experimental import pallas as pl
from jax.experimental.pallas import tpu as pltpu

F32 = jnp.float32
BF16 = jnp.bfloat16

RMS_EPS = 1e-6
NEG_INF = -1e30
LOG2E = 1.4426950408889634

ATTN_HEADS = 8
ATTN_HEAD_DIM = 64
ATTN_WIDTH = ATTN_HEADS * ATTN_HEAD_DIM
POOL_WINDOWS = (2, 4, 8, 16)
POOL_GROUP_DIM = 64
POOL_WIDTH = len(POOL_WINDOWS) * POOL_GROUP_DIM
SSM_GROUPS = 16
SSM_GROUP_DIM = 16
SSM_STATE = 64
SSM_WIDTH = SSM_GROUPS * SSM_GROUP_DIM
SSM_CHUNK = 8
N_EXPERTS = 8

LANES = 128
SUBLANES = 8
VMEM_LIMIT = 56 * 1024 * 1024


def _params(sem):
    return pltpu.CompilerParams(dimension_semantics=sem, vmem_limit_bytes=VMEM_LIMIT)


def _rms(x, g):
    return x * lax.rsqrt(jnp.mean(x * x, axis=-1, keepdims=True) + RMS_EPS) * g


def _layer_spec(arr, layer):
    return pl.BlockSpec((None,) + arr.shape[1:], lambda *_: (layer,) + (0,) * (arr.ndim - 1))


def _load_slabs(ref, row0, rows):
    return jnp.concatenate([ref[pl.ds(row0 * SUBLANES + s, rows, stride=SUBLANES), :]
                            for s in range(SUBLANES)], axis=1)


def _store_slabs(ref, row0, val):
    for s in range(SUBLANES):
        ref[pl.ds(row0 * SUBLANES + s, val.shape[0], stride=SUBLANES), :] = val[:, s * LANES:(s + 1) * LANES]


def _split3(x):
    hi = x.astype(BF16)
    r = x - hi.astype(F32)
    mid = r.astype(BF16)
    lo = (r - mid.astype(F32)).astype(BF16)
    return hi, mid, lo


INPROJ_STAGE_ROWS = 256
INPROJ_SUB = 256


def _inproj_kernel(x_ref, g_ref, w_hbm, bf_ref, q_ref, k_ref, v_ref, up_ref, us_ref, c_ref,
                   w_s, stage, sem, carry_ref, *, layer, tiles_per_seq):
    a = ATTN_WIDTH
    f_off = 3 * a
    p_off = f_off + ATTN_HEADS
    main = f_off + POOL_WIDTH + SSM_WIDTH

    @pl.when(pl.program_id(0) == 0)
    def _():
        rows = stage.shape[1]

        def copy(c):
            return pltpu.make_async_copy(w_hbm.at[layer, pl.ds(c * rows, rows)], stage.at[c % 2], sem.at[c % 2])

        nchunks = w_s.shape[0] // rows
        copy(0).start()
        for c in range(nchunks):
            if c + 1 < nchunks:
                copy(c + 1).start()
            copy(c).wait()
            blk = stage[c % 2]
            r = pl.ds(c * rows, rows)
            w_s[r, 0:f_off] = blk[:, 0:f_off].astype(BF16)
            w_s[r, f_off:main] = blk[:, p_off:p_off + main - f_off].astype(BF16)
            lane = lax.broadcasted_iota(jnp.int32, (rows, LANES), 1)
            w_s[r, main:] = jnp.where(lane < ATTN_HEADS, blk[:, f_off:f_off + LANES], 0.0).astype(BF16)

    @pl.when(pl.program_id(0) % tiles_per_seq == 0)
    def _():
        carry_ref[...] = jnp.zeros_like(carry_ref)

    tm = x_ref.shape[0]
    zs = []
    for s in range(tm // INPROJ_SUB):
        r = pl.ds(s * INPROJ_SUB, INPROJ_SUB)
        h = _rms(x_ref[r, :], g_ref[...]).astype(BF16)
        proj = jnp.dot(h, w_s[...], preferred_element_type=F32)
        q_ref[r, :] = (proj[:, 0:a] * (LOG2E * ATTN_HEAD_DIM ** -0.5)).astype(BF16)
        k_ref[r, :] = proj[:, a:2 * a].astype(BF16)
        v_ref[r, :] = proj[:, 2 * a:3 * a].astype(BF16)
        up_ref[r, :] = proj[:, 3 * a:3 * a + POOL_WIDTH].astype(BF16)
        us_ref[r, :] = proj[:, 3 * a + POOL_WIDTH:main]
        zs.append(proj[:, main:])

    z = jnp.concatenate(zs, axis=0) + bf_ref[...]
    logf = jnp.minimum(z, 0.0) - jnp.log(1.0 + jnp.exp(-jnp.abs(z)))
    row = lax.broadcasted_iota(jnp.int32, (tm, tm), 0)
    col = lax.broadcasted_iota(jnp.int32, (tm, tm), 1)
    tri = (row >= col).astype(BF16)
    hi, mid, _ = _split3(logf)
    c = (jnp.dot(tri, hi, preferred_element_type=F32)
         + jnp.dot(tri, mid, preferred_element_type=F32)) + carry_ref[...]
    c_ref[...] = c[:, :ATTN_HEADS]
    carry_ref[...] = c[tm - 1:tm, :]


def _inproj(x2, g, w_in, b_f, *, layer, seq, tm=512):
    n, d = x2.shape
    n_in = w_in.shape[2]
    a = ATTN_WIDTH
    n_res = n_in - ATTN_HEADS + LANES
    out_shape = (
        jax.ShapeDtypeStruct((n, a), BF16), jax.ShapeDtypeStruct((n, a), BF16),
        jax.ShapeDtypeStruct((n, a), BF16), jax.ShapeDtypeStruct((n, POOL_WIDTH), BF16),
        jax.ShapeDtypeStruct((n, SSM_WIDTH), F32), jax.ShapeDtypeStruct((n, ATTN_HEADS), F32))
    row = lambda w: pl.BlockSpec((tm, w), lambda i: (i, 0))
    return pl.pallas_call(
        functools.partial(_inproj_kernel, layer=layer, tiles_per_seq=seq // tm),
        grid=(n // tm,),
        in_specs=[row(d), _layer_spec(g, layer), pl.BlockSpec(memory_space=pl.ANY), _layer_spec(b_f, layer)],
        out_specs=(row(a), row(a), row(a), row(POOL_WIDTH), row(SSM_WIDTH), row(ATTN_HEADS)),
        out_shape=out_shape,
        scratch_shapes=[pltpu.VMEM((d, n_res), BF16), pltpu.VMEM((2, INPROJ_STAGE_ROWS, n_in), F32),
                        pltpu.SemaphoreType.DMA((2,)), pltpu.VMEM((1, LANES), F32)],
        compiler_params=_params(("arbitrary",)),
        name="inproj",
    )(x2, g, w_in, b_f)


HEADS_PER_BLOCK = LANES // ATTN_HEAD_DIM


def _attn_kernel(q_ref, k_ref, v_ref, ct_ref, o_ref, *, tq, tk):
    nlb = q_ref.shape[2] // LANES
    hp = pl.program_id(1)
    qi = pl.program_id(2)
    nkb = k_ref.shape[1] // tk
    lane = lax.broadcasted_iota(jnp.int32, (1, LANES), 1)
    in_head = [(lane // ATTN_HEAD_DIM) == hh for hh in range(HEADS_PER_BLOCK)]
    chains = [(lb, hh) for lb in range(nlb) for hh in range(HEADS_PER_BLOCK)]
    qm, crow, cref = [], [], []
    for lb in range(nlb):
        q2 = q_ref[0, :, lb * LANES:(lb + 1) * LANES]
        qm.append(jnp.concatenate([jnp.where(msk, q2, jnp.zeros_like(q2)) for msk in in_head], axis=0))
    for lb, hh in chains:
        crow.append(((hp * nlb + lb) * HEADS_PER_BLOCK + hh) * nkb)
        cref.append(ct_ref[0, pl.ds(crow[-1] + qi, 1), :][:, 0:1])

    def step(kb, carry, masked):
        new = []
        scores = {}
        for ci, (lb, hh) in enumerate(chains):
            kblk = k_ref[0, pl.ds(kb * tk, tk), lb * LANES:(lb + 1) * LANES]
            vblk = v_ref[0, pl.ds(kb * tk, tk), lb * LANES:(lb + 1) * LANES]
            m, acc = carry[ci]
            if lb not in scores:
                scores[lb] = lax.dot_general(qm[lb], kblk, (((1,), (1,)), ((), ())), preferred_element_type=F32)
            s = scores[lb][hh * tq:(hh + 1) * tq, :]
            s = s - (ct_ref[0, pl.ds(crow[ci] + kb, 1), :] - cref[ci]) * LOG2E
            if masked:
                r = lax.broadcasted_iota(jnp.int32, (tq, tk), 0)
                cidx = lax.broadcasted_iota(jnp.int32, (tq, tk), 1)
                s = jnp.where(cidx <= r, s, NEG_INF)
            m_new = jnp.maximum(m, jnp.max(s, axis=1, keepdims=True))
            p = jnp.exp2(s - m_new)
            vsel = jnp.where(in_head[hh], vblk, jnp.ones_like(vblk))
            acc = jnp.exp2(m - m_new) * acc + jnp.dot(p.astype(BF16), vsel, preferred_element_type=F32)
            new.append((m_new, acc))
        return tuple(new)

    init = tuple((jnp.full((tq, 1), NEG_INF, F32), jnp.zeros((tq, LANES), F32)) for _ in chains)
    carry = lax.fori_loop(0, qi, lambda kb, cr: step(kb, cr, False), init)
    carry = step(qi, carry, True)
    for lb in range(nlb):
        out = jnp.zeros((tq, LANES), F32)
        for hh in range(HEADS_PER_BLOCK):
            acc = carry[lb * HEADS_PER_BLOCK + hh][1]
            out = jnp.where(in_head[hh], acc / pltpu.roll(acc, ATTN_HEAD_DIM, 1), out)
        o_ref[0, :, lb * LANES:(lb + 1) * LANES] = out.astype(o_ref.dtype)


def _attention(q, k, v, ct, *, tq, lane_blocks=4):
    b, L, a = q.shape
    tk = ct.shape[2]
    assert tq == tk, "the diagonal key block of query tile i must be key block i"
    w = lane_blocks * LANES
    blk = lambda: pl.BlockSpec((1, tq, w), lambda bi, hi, qi: (bi, qi, hi))
    seq = lambda: pl.BlockSpec((1, L, w), lambda bi, hi, qi: (bi, 0, hi))
    return pl.pallas_call(
        functools.partial(_attn_kernel, tq=tq, tk=tk),
        grid=(b, a // w, L // tq),
        in_specs=[blk(), seq(), seq(),
                  pl.BlockSpec((1, ct.shape[1], tk), lambda bi, hi, qi: (bi, 0, 0))],
        out_specs=blk(),
        out_shape=jax.ShapeDtypeStruct((b, L, a), BF16),
        compiler_params=_params(("arbitrary", "arbitrary", "arbitrary")),
        name="fox_attention",
    )(q, k, v, ct)


def _pool_kernel(u_ref, w_ref, s_ref, o_ref):
    x = u_ref[0].astype(F32)
    L, w = x.shape
    row = lax.broadcasted_iota(jnp.int32, (L, w), 0)
    group = lax.broadcasted_iota(jnp.int32, (L, w), 1) // POOL_GROUP_DIM

    def shifted(y, s):
        return jnp.where(row >= s, pltpu.roll(y, s, 0), 0.0)

    acc = x
    sel = jnp.zeros_like(x)
    win_lane = jnp.zeros((L, w), F32)
    span = 1
    for gi, win in enumerate(POOL_WINDOWS):
        while span < win:
            acc = acc + shifted(acc, span)
            span *= 2
        sel = jnp.where(group == gi, acc, sel)
        win_lane = jnp.where(group == gi, float(win), win_lane)
    count = jnp.minimum(row.astype(F32) + 1.0, win_lane)
    d = sel / count - x
    y = jnp.dot(d.astype(BF16), w_ref[...], preferred_element_type=F32) * s_ref[...]
    o_ref[0] = y.astype(o_ref.dtype)


def _pool(u, w_bd, scale, *, layer):
    b, L, w = u.shape
    return pl.pallas_call(
        _pool_kernel,
        grid=(b,),
        in_specs=[pl.BlockSpec((1, L, w), lambda i: (i, 0, 0)),
                  _layer_spec(w_bd, layer), _layer_spec(scale, layer)],
        out_specs=pl.BlockSpec((1, L, w), lambda i: (i, 0, 0)),
        out_shape=jax.ShapeDtypeStruct((b, L, w), BF16),
        compiler_params=_params(("arbitrary",)),
        name="pool_mixer",
    )(u, w_bd, scale)


def _group_of(idx, width):
    groups = LANES // SSM_GROUP_DIM
    return (idx >> (width.bit_length() - 1)) & (groups - 1)


def _expand_blockdiag(r_ref, row_inner, col_inner):
    rows, w = r_ref.shape[1:]
    groups = LANES // SSM_GROUP_DIM
    r = lax.broadcasted_iota(jnp.int32, (w, groups * w), 0)
    c = lax.broadcasted_iota(jnp.int32, (w, groups * w), 1)
    shift = col_inner.bit_length() - 1
    outer_c = c >> (shift + groups.bit_length() - 1)
    spread = (outer_c == (r >> shift)) & ((c & (col_inner - 1)) == (r & (col_inner - 1)))
    big = jnp.dot(r_ref[0], spread.astype(BF16), preferred_element_type=F32)
    ri = lax.broadcasted_iota(jnp.int32, (rows, groups * w), 0)
    ci = lax.broadcasted_iota(jnp.int32, (rows, groups * w), 1)
    return jnp.where(_group_of(ri, row_inner) == _group_of(ci, col_inner), big, 0.0).astype(BF16)


def _ssm_kernel(u_ref, rm_ref, rz_ref, ry_ref, coef_ref, y_ref, m_s, wz_s, wy_s, ucat, z_ref, s_ref, *, seqs):
    t = SSM_CHUNK

    @pl.when(pl.program_id(1) == 0)
    def _():
        m_s[...] = _expand_blockdiag(rm_ref, SSM_GROUP_DIM, SSM_GROUP_DIM)
        wz_s[...] = _expand_blockdiag(rz_ref, SSM_GROUP_DIM, SSM_STATE)
        wy_s[...] = _expand_blockdiag(ry_ref, SSM_STATE, SSM_GROUP_DIM)

    nc = u_ref.shape[0] // t
    cps = nc // seqs
    for k in range(t):
        ucat[:, k * LANES:(k + 1) * LANES] = u_ref[pl.ds(k, nc, stride=t), :].astype(BF16)
    u = ucat[...]
    z_ref[...] = jnp.dot(u, wz_s[...], preferred_element_type=F32)
    half = z_ref.shape[1] // 2
    a1 = coef_ref[0, 0:1, :]
    a2 = coef_ref[0, 1:2, :]

    def step(c, states):
        new = []
        for b in range(seqs):
            st = states[b]
            r = b * cps + c
            s_ref[pl.ds(r, 1), :] = st
            sw = jnp.concatenate([st[:, half:], st[:, :half]], axis=1)
            new.append(a1 * st + a2 * sw + z_ref[pl.ds(r, 1), :])
        return tuple(new)

    init = tuple(jnp.zeros((1, 2 * half), F32) for _ in range(seqs))
    lax.fori_loop(0, cps, step, init, unroll=8)
    y = jnp.dot(u, m_s[...], preferred_element_type=F32)
    y = y + jnp.dot(s_ref[...].astype(BF16), wy_s[...], preferred_element_type=F32)
    for k in range(t):
        y_ref[pl.ds(k, nc, stride=t), :] = y[:, k * LANES:(k + 1) * LANES]


def _ssm_core(u, rm, rz, ry, coef, *, layer, seq, max_seqs_per_block=4):
    n, w = u.shape
    seqs_per_block = math.gcd(max_seqs_per_block, n // seq)
    rows = seq * seqs_per_block
    nc = rows // SSM_CHUNK
    kw = SSM_CHUNK * LANES
    p2 = coef.shape[3]
    blk = pl.BlockSpec((rows, LANES), lambda hf, r: (r, hf))
    wspec = lambda arr: pl.BlockSpec((None, 1) + arr.shape[2:], lambda hf, r: (layer, hf, 0, 0))
    return pl.pallas_call(
        functools.partial(_ssm_kernel, seqs=seqs_per_block),
        grid=(w // LANES, n // rows),
        in_specs=[blk, wspec(rm), wspec(rz), wspec(ry), wspec(coef)],
        out_specs=blk,
        out_shape=jax.ShapeDtypeStruct((n, w), F32),
        scratch_shapes=[pltpu.VMEM((kw, kw), BF16), pltpu.VMEM((kw, p2), BF16), pltpu.VMEM((p2, kw), BF16),
                        pltpu.VMEM((nc, kw), BF16), pltpu.VMEM((nc, p2), F32), pltpu.VMEM((nc, p2), F32)],
        compiler_params=_params(("arbitrary", "arbitrary")),
        name="s5_core",
    )(u, rm, rz, ry, coef)


def _ssm_matrices(a_re, a_im, b_re, b_im, c_re, c_im, d, log_dt):
    t = SSM_CHUNK
    g, p = a_re.shape
    hdim = d.shape[1]
    gl = LANES // hdim
    nb = g // gl
    lam = lax.complex(a_re, a_im)
    lam_dt = lam * jnp.exp(log_dt)[:, None]
    lam_bar = jnp.exp(lam_dt)
    b_bar = ((lam_bar - 1.0) / lam)[:, :, None] * lax.complex(b_re, b_im)
    cmat = lax.complex(c_re, c_im)
    steps = jnp.arange(t + 1, dtype=F32)
    pw = jnp.exp(lam_dt[:, None, :] * steps[None, :, None])
    kern = jnp.einsum('gop,gdp,gpi->gdoi', cmat, pw[:, :t], b_bar).real
    kern = kern.at[:, 0].add(jax.vmap(jnp.diag)(d))
    j = jnp.arange(t)[:, None]
    i = jnp.arange(t)[None, :]
    m = jnp.where((i >= j)[None, :, :, None, None], kern[:, jnp.clip(i - j, 0, t - 1)], 0.0)
    wz = jnp.einsum('gjp,gpi->gjip', pw[:, t - 1 - jnp.arange(t)], b_bar)
    cw = jnp.einsum('gop,gip->giop', cmat, pw[:, 1:t + 1])
    rm = m.reshape(nb, gl, t, t, hdim, hdim).transpose(0, 2, 1, 5, 3, 4).reshape(nb, t * LANES, t * hdim)
    rz = jnp.stack([wz.real, wz.imag], axis=3).reshape(nb, gl, t, hdim, 2, p)
    rz = rz.transpose(0, 2, 1, 3, 4, 5).reshape(nb, t * LANES, 2 * p)
    ry = jnp.stack([cw.real, -cw.imag], axis=1).reshape(nb, gl, 2, t, hdim, p)
    ry = ry.transpose(0, 2, 1, 5, 3, 4).reshape(nb, 2 * gl * p, t * hdim)
    ar = pw[:, t].real.reshape(nb, gl * p)
    ai = pw[:, t].imag.reshape(nb, gl * p)
    coef = jnp.stack([jnp.concatenate([ar, ar], -1), jnp.concatenate([-ai, ai], -1)], axis=1)
    return rm.astype(BF16), rz.astype(BF16), ry.astype(BF16), coef.astype(F32)


MIX_SUB = 256


def _mixout_kernel(*refs, moe):
    (x_ref, attn_ref, pool_ref, ys_ref, wglu_ref, bglu_ref, ga_ref, gp_ref, gs_ref,
     wo_ref, gpost_ref, gpre_ref) = refs[:12]
    if moe:
        router_ref, xo_ref, h_ref, ri_ref, rg_ref, cnt_ref, cnt_acc = refs[12:]

        @pl.when(pl.program_id(0) == 0)
        def _():
            cnt_acc[...] = jnp.zeros_like(cnt_acc)
    else:
        xo_ref, h_ref = refs[12:]
    a = attn_ref.shape[1]
    pw = pool_ref.shape[1]
    h_parts = []
    for s in range(x_ref.shape[0] // MIX_SUB):
        r = pl.ds(s * MIX_SUB, MIX_SUB)
        attn_n = _rms(attn_ref[r, :].astype(F32), ga_ref[...]).astype(BF16)
        pool_n = _rms(pool_ref[r, :].astype(F32), gp_ref[...]).astype(BF16)
        y = jax.nn.gelu(ys_ref[r, :])
        gate = jnp.dot(y.astype(BF16), wglu_ref[...], preferred_element_type=F32) + bglu_ref[...]
        ssm = y * jax.nn.sigmoid(gate)
        ssm_n = _rms(ssm, gs_ref[...]).astype(BF16)
        mix = (jnp.dot(attn_n, wo_ref[0:a, :], preferred_element_type=F32)
               + jnp.dot(pool_n, wo_ref[a:a + pw, :], preferred_element_type=F32)
               + jnp.dot(ssm_n, wo_ref[a + pw:, :], preferred_element_type=F32))
        x = x_ref[r, :] + _rms(mix, gpost_ref[...])
        xo_ref[r, :] = x
        h = _rms(x, gpre_ref[...])
        if moe:
            _store_slabs(h_ref, s * MIX_SUB, h)
            h_parts.append(h)
        else:
            h_ref[r, :] = h.astype(h_ref.dtype)
    if moe:
        hi, mid, _ = _split3(jnp.concatenate(h_parts, axis=0))
        rhi, rmid, _ = _split3(router_ref[...])
        dot_t = lambda p, q: lax.dot_general(p, q, (((1,), (1,)), ((), ())), preferred_element_type=F32)
        logits = dot_t(rhi, hi) + dot_t(rmid, hi) + dot_t(rhi, mid)
        ne, tm = logits.shape
        eidx = lax.broadcasted_iota(jnp.int32, (ne, tm), 0)
        m1 = jnp.max(logits, axis=0, keepdims=True)
        i1 = jnp.min(jnp.where(logits == m1, eidx, ne), axis=0, keepdims=True)
        rest = jnp.where(eidx == i1, -jnp.inf, logits)
        m2 = jnp.max(rest, axis=0, keepdims=True)
        i2 = jnp.min(jnp.where(rest == m2, eidx, ne), axis=0, keepdims=True)
        e2 = jnp.exp(m2 - m1)
        g1 = 1.0 / (1.0 + e2)
        rg_ref[...] = jnp.where(eidx == 0, g1, jnp.where(eidx == 1, e2 * g1, 0.0))
        onehot = ((eidx == i1) | (eidx == i2)).astype(BF16)
        earlier = (lax.broadcasted_iota(jnp.int32, (tm, tm), 0)
                   < lax.broadcasted_iota(jnp.int32, (tm, tm), 1)).astype(BF16)
        before = jnp.dot(onehot, earlier, preferred_element_type=F32) + cnt_acc[:, 0:1]
        r1 = jnp.sum(jnp.where(eidx == i1, before, 0.0), axis=0, keepdims=True).astype(jnp.int32)
        r2 = jnp.sum(jnp.where(eidx == i2, before, 0.0), axis=0, keepdims=True).astype(jnp.int32)
        ri_ref[...] = jnp.where(eidx == 0, i1, jnp.where(eidx == 1, i2,
                                jnp.where(eidx == 2, r1, jnp.where(eidx == 3, r2, 0))))
        cnt_acc[...] += jnp.sum(onehot.astype(F32), axis=1, keepdims=True)
        cnt_ref[...] = cnt_acc[...]


def _mixout(x2, attn, pool, ys, w_glu, b_glu, g_attn, g_pool, g_ssm, w_out, g_post, g_pre,
            router=None, *, layer, moe_layer=None, tm=512):
    n, d = x2.shape
    moe = router is not None
    row = lambda w: pl.BlockSpec((tm, w), lambda i: (i, 0))
    ins = [x2, attn, pool, ys, w_glu, b_glu, g_attn, g_pool, g_ssm, w_out, g_post, g_pre]
    in_specs = ([row(d), row(attn.shape[1]), row(pool.shape[1]), row(ys.shape[1])]
                + [_layer_spec(t, layer) for t in ins[4:]])
    if moe:
        assert d == SUBLANES * LANES
        h_shape = jax.ShapeDtypeStruct((n * SUBLANES, LANES), F32)
        h_spec = pl.BlockSpec((tm * SUBLANES, LANES), lambda i: (i, 0))
    else:
        h_shape, h_spec = jax.ShapeDtypeStruct((n, d), BF16), row(d)
    out_shape = [jax.ShapeDtypeStruct((n, d), F32), h_shape]
    out_specs = [row(d), h_spec]
    scratch = []
    if moe:
        ins.append(router)
        in_specs.append(_layer_spec(router, moe_layer))
        ne = router.shape[1]
        col = pl.BlockSpec((ne, tm), lambda i: (0, i))
        out_shape += [jax.ShapeDtypeStruct((ne, n), jnp.int32), jax.ShapeDtypeStruct((ne, n), F32),
                      jax.ShapeDtypeStruct((ne, LANES), F32)]
        out_specs += [col, col, pl.BlockSpec((ne, LANES), lambda i: (0, 0))]
        scratch = [pltpu.VMEM((ne, LANES), F32)]
    return pl.pallas_call(
        functools.partial(_mixout_kernel, moe=moe),
        grid=(n // tm,),
        in_specs=in_specs,
        out_specs=tuple(out_specs),
        out_shape=tuple(out_shape),
        scratch_shapes=scratch,
        compiler_params=_params(("arbitrary",)),
        name="mix_out",
    )(*ins)


FFN_STAGE_CHUNKS = 4


def _stage_cast(src_hbm, dst, lane0, stage, sem):
    rows = stage.shape[1]
    width = src_hbm.shape[1]

    def copy(c):
        return pltpu.make_async_copy(src_hbm.at[pl.ds(c * rows, rows)], stage.at[c % 2], sem.at[c % 2])

    nchunks = src_hbm.shape[0] // rows
    copy(0).start()
    for c in range(nchunks):
        if c + 1 < nchunks:
            copy(c + 1).start()
        copy(c).wait()
        dst[pl.ds(c * rows, rows), lane0:lane0 + width] = stage[c % 2].astype(BF16)


def _ffn_kernel(h_ref, x_ref, wg_hbm, wu_hbm, wd_hbm, g_ref, o_ref, wgu_s, wd_s, stage_in, stage_out, sem, *, sub):
    fdim = wd_s.shape[0]

    @pl.when(pl.program_id(0) == 0)
    def _():
        _stage_cast(wg_hbm, wgu_s, 0, stage_in, sem)
        _stage_cast(wu_hbm, wgu_s, fdim, stage_in, sem)
        _stage_cast(wd_hbm, wd_s, 0, stage_out, sem)

    for r in range(h_ref.shape[0] // sub):
        rows = pl.ds(r * sub, sub)
        hgu = jnp.dot(h_ref[rows, :], wgu_s[...], preferred_element_type=F32)
        hg, hu = hgu[:, :fdim], hgu[:, fdim:]
        act = (hg * jax.nn.sigmoid(hg) * hu).astype(BF16)
        y = jnp.dot(act, wd_s[...], preferred_element_type=F32)
        o_ref[rows, :] = x_ref[rows, :] + _rms(y, g_ref[...])


def _ffn(h, x2, wg, wu, wd, g_post, *, layer, tm=512, sub=256):
    n, d = x2.shape
    fdim = wg.shape[1]
    row = pl.BlockSpec((tm, d), lambda i: (i, 0))
    hbm = pl.BlockSpec(memory_space=pl.ANY)
    return pl.pallas_call(
        functools.partial(_ffn_kernel, sub=sub),
        grid=(n // tm,),
        in_specs=[row, row, hbm, hbm, hbm, _layer_spec(g_post, layer)],
        out_specs=row,
        out_shape=jax.ShapeDtypeStruct((n, d), F32),
        scratch_shapes=[pltpu.VMEM((d, 2 * fdim), BF16), pltpu.VMEM((fdim, d), BF16),
                        pltpu.VMEM((2, d // FFN_STAGE_CHUNKS, fdim), F32),
                        pltpu.VMEM((2, fdim // FFN_STAGE_CHUNKS, d), F32),
                        pltpu.SemaphoreType.DMA((2,))],
        compiler_params=_params(("arbitrary",)),
        name="dense_ffn",
    )(h, x2, wg, wu, wd, g_post)


MOE_TILE = 512


def _moe_tables(counts, n_items):
    tm = MOE_TILE
    i32 = jnp.int32
    ne = counts.shape[0]
    assert n_items <= LANES
    counts = counts.astype(i32)
    ntiles = (counts + tm - 1) // tm
    k = jnp.arange(ne, dtype=i32)
    ends = jnp.sum(jnp.where(k[None, :] <= k[:, None], ntiles[None, :], 0), axis=1)
    starts = ends - ntiles
    later = (ntiles > 0)[None, :] & (k[None, :] > k[:, None])
    nxt = jnp.min(jnp.where(later, k[None, :], ne), axis=1)
    nxt = jnp.where(nxt < ne, nxt, -1)
    w = jnp.arange(LANES, dtype=i32)[None, :]
    wc = jnp.minimum(w, ends[-1] - 1)
    e = jnp.sum((wc >= ends[:, None]).astype(i32), axis=0, keepdims=True)
    of_item = lambda v: jnp.sum(jnp.where(k[:, None] == e, v[:, None], 0), axis=0, keepdims=True)
    valid = w < ends[-1]
    rowblock = jnp.where(valid, w, n_items)
    nvalid = jnp.where(valid, jnp.clip(of_item(counts) - (wc - of_item(starts)) * tm, 0, tm), 0)
    e_prev = jnp.concatenate([e[:, :1], e[:, :-1]], axis=1)
    first = ((w == 0) | (e != e_prev)).astype(i32)
    item = lambda v: v[0, :n_items].astype(i32)
    return (item(e), item(rowblock), item(nvalid), item(first), item(of_item(nxt))), starts * tm


def _slab(ref, token):
    return ref.at[pl.ds(pl.multiple_of(token * SUBLANES, SUBLANES), SUBLANES)]


def _dispatch_kernel(dest_ref, h_ref, xs_ref, sem, *, n):
    tm = h_ref.shape[0] // SUBLANES
    base = pl.program_id(0) * tm

    def body(r, carry):
        for k in range(2):
            d = dest_ref[k * n + base + r]
            pltpu.make_async_copy(_slab(h_ref, r), _slab(xs_ref, d), sem).start(priority=k)
        return carry

    lax.fori_loop(0, tm, body, 0, unroll=8)
    for k in range(2):
        pltpu.make_async_copy(h_ref, h_ref, sem).wait()


def _dispatch(dest, h, cap_rows, *, tm=512):
    n = h.shape[0] // SUBLANES
    return pl.pallas_call(
        functools.partial(_dispatch_kernel, n=n),
        grid_spec=pltpu.PrefetchScalarGridSpec(
            num_scalar_prefetch=1,
            grid=(n // tm,),
            in_specs=[pl.BlockSpec((tm * SUBLANES, LANES), lambda i, dest: (i, 0))],
            out_specs=pl.BlockSpec(memory_space=pl.ANY),
            scratch_shapes=[pltpu.SemaphoreType.DMA(())]),
        out_shape=jax.ShapeDtypeStruct((cap_rows * SUBLANES, LANES), F32),
        compiler_params=_params(("arbitrary",)),
        name="moe_dispatch",
    )(dest, h)


MOE_SUB = 256


MOE_UP_CHUNKS = 4


def _moe_up_kernel(ie_ref, rb_ref, nv_ref, first_ref, nxt_ref, xs_ref, wg_hbm, wu_hbm, h_ref, wgu_s, stage, sem):
    f = pl.program_id(0)
    w = pl.program_id(1)
    nf = pl.num_programs(0)
    tf = h_ref.shape[1]
    rows = stage.shape[0] // MOE_UP_CHUNKS

    def copies(e, fp):
        return [pltpu.make_async_copy(src.at[e, pl.ds(c * rows, rows), pl.ds(fp * tf, tf)],
                                      stage.at[pl.ds(c * rows, rows), pl.ds(m * tf, tf)],
                                      sem.at[m * MOE_UP_CHUNKS + c])
                for m, src in enumerate((wg_hbm, wu_hbm)) for c in range(MOE_UP_CHUNKS)]

    @pl.when((f == 0) & (w == 0))
    def _():
        for cp in copies(ie_ref[0], 0):
            cp.start()

    is_first = first_ref[w] == 1
    last_expert = nxt_ref[w] < 0
    for fp in range(nf):
        @pl.when(is_first & (f == fp))
        def _():
            for cp in copies(ie_ref[w], fp):
                cp.wait()
            for c in range(MOE_UP_CHUNKS):
                r = pl.ds(c * rows, rows)
                wgu_s[r, :] = stage[r, :].astype(BF16)

            @pl.when(jnp.logical_not(last_expert))
            def _():
                for cp in copies(nxt_ref[w], fp):
                    cp.start()

            if fp + 1 < nf:
                @pl.when(last_expert)
                def _():
                    for cp in copies(ie_ref[0], fp + 1):
                        cp.start()

    nv = nv_ref[w]

    nsub = h_ref.shape[0] // MOE_SUB

    def compute(nchains):
        for r in range(nchains):
            rws = pl.ds(r * MOE_SUB, MOE_SUB)
            idx = r * MOE_SUB + lax.broadcasted_iota(jnp.int32, (MOE_SUB, 1), 0)
            x = _load_slabs(xs_ref, r * MOE_SUB, MOE_SUB)
            x = jnp.where(idx < nv, x, 0.0).astype(BF16)
            hgu = jnp.dot(x, wgu_s[...], preferred_element_type=F32)
            hg, hu = hgu[:, :tf], hgu[:, tf:]
            h_ref[rws, :] = (hg * jax.nn.sigmoid(hg) * hu).astype(BF16)
        if nchains < nsub:
            rest = pl.ds(nchains * MOE_SUB, (nsub - nchains) * MOE_SUB)
            h_ref[rest, :] = jnp.zeros((rest.size, tf), BF16)

    for nchains in range(nsub + 1):
        pl.when((nv > (nchains - 1) * MOE_SUB) & (nv <= nchains * MOE_SUB))(functools.partial(compute, nchains))


def _moe_up(tables, xs, wg, wu, *, tf=1792):
    tm = MOE_TILE
    cap_rows = xs.shape[0] // SUBLANES
    ne, d, fdim = wg.shape
    n_items = tables[0].shape[0]
    hbm = pl.BlockSpec(memory_space=pl.ANY)
    return pl.pallas_call(
        _moe_up_kernel,
        grid_spec=pltpu.PrefetchScalarGridSpec(
            num_scalar_prefetch=5,
            grid=(fdim // tf, n_items),
            in_specs=[pl.BlockSpec((tm * SUBLANES, LANES), lambda f, w, ie, rb, *_: (rb[w], 0)), hbm, hbm],
            out_specs=pl.BlockSpec((tm, tf), lambda f, w, ie, rb, *_: (rb[w], f)),
            scratch_shapes=[pltpu.VMEM((d, 2 * tf), BF16), pltpu.VMEM((d, 2 * tf), F32),
                            pltpu.SemaphoreType.DMA((2 * MOE_UP_CHUNKS,))]),
        out_shape=jax.ShapeDtypeStruct((cap_rows, fdim), BF16),
        compiler_params=_params(("arbitrary", "arbitrary")),
        name="moe_up",
    )(*tables, xs, wg, wu)


def _moe_down_kernel(ie_ref, rb_ref, nv_ref, first_ref, nxt_ref, h_ref, wd_hbm, y_ref, wd_s, stage, sem):
    w = pl.program_id(0)
    nchunks = sem.shape[0]
    chunk = stage.shape[0] // nchunks

    def copies(e):
        return [pltpu.make_async_copy(wd_hbm.at[e, pl.ds(c * chunk, chunk)], stage.at[pl.ds(c * chunk, chunk)],
                                      sem.at[c]) for c in range(nchunks)]

    @pl.when(w == 0)
    def _():
        for cp in copies(ie_ref[0]):
            cp.start()

    @pl.when(first_ref[w] == 1)
    def _():
        for c, cp in enumerate(copies(ie_ref[w])):
            cp.wait()
            wd_s[pl.ds(c * chunk, chunk), :] = stage[pl.ds(c * chunk, chunk), :].astype(BF16)

        @pl.when(nxt_ref[w] >= 0)
        def _():
            for cp in copies(nxt_ref[w]):
                cp.start()

    nv = nv_ref[w]
    nsub = h_ref.shape[0] // MOE_SUB

    def compute(nchains):
        for r in range(nchains):
            y = jnp.dot(h_ref[pl.ds(r * MOE_SUB, MOE_SUB), :], wd_s[...], preferred_element_type=F32)
            _store_slabs(y_ref, r * MOE_SUB, y)
        if nchains < nsub:
            rest = pl.ds(nchains * MOE_SUB * SUBLANES, (nsub - nchains) * MOE_SUB * SUBLANES)
            y_ref[rest, :] = jnp.zeros((rest.size, LANES), F32)

    for nchains in range(nsub + 1):
        pl.when((nv > (nchains - 1) * MOE_SUB) & (nv <= nchains * MOE_SUB))(functools.partial(compute, nchains))


def _moe_down(tables, hid, wd, *, chunk=512):
    tm = MOE_TILE
    cap_rows, fdim = hid.shape
    d = wd.shape[2]
    n_items = tables[0].shape[0]
    assert d == SUBLANES * LANES and fdim % chunk == 0
    return pl.pallas_call(
        _moe_down_kernel,
        grid_spec=pltpu.PrefetchScalarGridSpec(
            num_scalar_prefetch=5,
            grid=(n_items,),
            in_specs=[pl.BlockSpec((tm, fdim), lambda w, ie, rb, *_: (rb[w], 0)),
                      pl.BlockSpec(memory_space=pl.ANY)],
            out_specs=pl.BlockSpec((tm * SUBLANES, LANES), lambda w, ie, rb, *_: (rb[w], 0)),
            scratch_shapes=[pltpu.VMEM((fdim, d), BF16), pltpu.VMEM((fdim, d), F32),
                            pltpu.SemaphoreType.DMA((fdim // chunk,))]),
        out_shape=jax.ShapeDtypeStruct((cap_rows * SUBLANES, LANES), F32),
        compiler_params=_params(("arbitrary",)),
        name="moe_down",
    )(*tables, hid, wd)


def _combine_kernel(dest_ref, x_ref, gate_ref, g_ref, ys_ref, o_ref, buf, sem, *, n):
    tm = x_ref.shape[0]
    i = pl.program_id(0)

    def issue(tile, slot):
        def body(r, carry):
            for k in range(2):
                d = dest_ref[k * n + tile * tm + r]
                pltpu.make_async_copy(_slab(ys_ref, d), _slab(buf.at[slot, k], r),
                                      sem.at[slot]).start(priority=k)
            return carry
        lax.fori_loop(0, tm, body, 0, unroll=8)

    @pl.when(i == 0)
    def _():
        issue(0, 0)

    @pl.when(i + 1 < pl.num_programs(0))
    def _():
        issue(i + 1, (i + 1) % 2)

    slot = i % 2
    for k in range(2):
        pltpu.make_async_copy(buf.at[slot, k], buf.at[slot, k], sem.at[slot]).wait()
    gates = gate_ref[...].T
    f = (gates[:, 0:1] * _load_slabs(buf.at[slot, 0], 0, tm)
         + gates[:, 1:2] * _load_slabs(buf.at[slot, 1], 0, tm))
    o_ref[...] = x_ref[...] + _rms(f, g_ref[...])


def _combine(dest, x2, gates, g_post, ys, *, layer, tm=512):
    n, d = x2.shape
    return pl.pallas_call(
        functools.partial(_combine_kernel, n=n),
        grid_spec=pltpu.PrefetchScalarGridSpec(
            num_scalar_prefetch=1,
            grid=(n // tm,),
            in_specs=[pl.BlockSpec((tm, d), lambda i, dest: (i, 0)),
                      pl.BlockSpec((gates.shape[0], tm), lambda i, dest: (0, i)),
                      _layer_spec(g_post, layer),
                      pl.BlockSpec(memory_space=pl.ANY)],
            out_specs=pl.BlockSpec((tm, d), lambda i, dest: (i, 0)),
            scratch_shapes=[pltpu.VMEM((2, 2, tm * SUBLANES, LANES), F32), pltpu.SemaphoreType.DMA((2,))]),
        out_shape=jax.ShapeDtypeStruct((n, d), F32),
        compiler_params=_params(("arbitrary",)),
        name="moe_combine",
    )(dest, x2, gates, g_post, ys)


def _moe(h, x2, route_i, route_g, counts, wg, wu, wd, g_post, *, layer):
    n = x2.shape[0]
    n_items = 2 * n // MOE_TILE + N_EXPERTS
    tables, base = _moe_tables(counts[:, 0], n_items)
    base_of = lambda e: jnp.sum(jnp.where(e[None, :] == jnp.arange(N_EXPERTS)[:, None], base[:, None], 0), axis=0)
    dest = jnp.concatenate([base_of(route_i[0]) + route_i[2], base_of(route_i[1]) + route_i[3]])
    xs = _dispatch(dest, h, (n_items + 1) * MOE_TILE)
    hid = _moe_up(tables, xs, wg, wu)
    ys = _moe_down(tables, hid, wd)
    return _combine(dest, x2, route_g, g_post, ys, layer=layer)


def _rows(v):
    return v.reshape(v.shape[0], 1, -1).astype(F32)


def kernel(x, norm_mix_pre, norm_mix_post, norm_ffn_pre, norm_ffn_post, w_in, b_forget, pool_w, pool_scale, ssm_a_re, ssm_a_im, ssm_b_re, ssm_b_im, ssm_c_re, ssm_c_im, ssm_d, ssm_log_dt, ssm_w_glu, ssm_b_glu, branch_norm_attn, branch_norm_pool, branch_norm_ssm, w_out, ffn_w_gate, ffn_w_up, ffn_w_down, moe_router, moe_w_gate, moe_w_up, moe_w_down):
    b, L, d = x.shape
    depth = w_in.shape[0]
    n = b * L
    a = ATTN_WIDTH
    attn_tile = 512
    x2 = x.reshape(n, d)
    g_mix_pre, g_mix_post = _rows(norm_mix_pre), _rows(norm_mix_post)
    g_ffn_pre, g_ffn_post = _rows(norm_ffn_pre), _rows(norm_ffn_post)
    g_attn, g_pool, g_ssm = _rows(branch_norm_attn), _rows(branch_norm_pool), _rows(branch_norm_ssm)
    b_f = _rows(jnp.pad(b_forget, ((0, 0), (0, LANES - ATTN_HEADS))))
    pool_bd = jax.vmap(lambda w: jax.scipy.linalg.block_diag(*w))(pool_w).astype(BF16)
    pool_sc, b_glu = _rows(pool_scale), _rows(ssm_b_glu)
    w_glu, w_o = ssm_w_glu.astype(BF16), w_out.astype(BF16)
    mats = jax.vmap(_ssm_matrices)(ssm_a_re, ssm_a_im, ssm_b_re, ssm_b_im, ssm_c_re, ssm_c_im, ssm_d, ssm_log_dt)
    router_t = jnp.swapaxes(moe_router, 1, 2)
    for i in range(depth):
        q, k, v, up, us, c = _inproj(x2, g_mix_pre, w_in, b_f, layer=i, seq=L)

        ct = c.reshape(b, L, ATTN_HEADS).transpose(0, 2, 1).reshape(b, ATTN_HEADS * (L // attn_tile), attn_tile)
        attn = _attention(q.reshape(b, L, a), k.reshape(b, L, a), v.reshape(b, L, a),
                          ct, tq=attn_tile).reshape(n, a)
        pool = _pool(up.reshape(b, L, POOL_WIDTH), pool_bd, pool_sc, layer=i).reshape(n, POOL_WIDTH)
        ys = _ssm_core(us, *mats, layer=i, seq=L)

        moe = i % 2 == 1
        j = i // 2
        res = _mixout(x2, attn, pool, ys, w_glu, b_glu, g_attn, g_pool, g_ssm, w_o, g_mix_post, g_ffn_pre,
                      router_t if moe else None, layer=i, moe_layer=j)
        if moe:
            x2, h, route_i, route_g, counts = res
            x2 = _moe(h, x2, route_i, route_g, counts, moe_w_gate[j], moe_w_up[j], moe_w_down[j],
                      g_ffn_post, layer=i)
        else:
            x2, h = res
            x2 = _ffn(h, x2, ffn_w_gate[j], ffn_w_up[j], ffn_w_down[j], g_ffn_post, layer=i)
    return x2.reshape(b, L, d)
```

```python
import functools
import math

import jax
import jax.numpy as jnp
from jax import lax
from jax.experimental import pallas as pl
from jax.experimental.pallas import tpu as pltpu

F32 = jnp.float32
BF16 = jnp.bfloat16

RMS_EPS = 1e-6
NEG_INF = -1e30
LOG2E = 1.4426950408889634

ATTN_HEADS = 8
ATTN_HEAD_DIM = 64
ATTN_WIDTH = ATTN_HEADS * ATTN_HEAD_DIM
POOL_WINDOWS = (2, 4, 8, 16)
POOL_GROUP_DIM = 64
POOL_WIDTH = len(POOL_WINDOWS) * POOL_GROUP_DIM
SSM_GROUPS = 16
SSM_GROUP_DIM = 16
SSM_STATE = 64
SSM_WIDTH = SSM_GROUPS * SSM_GROUP_DIM
SSM_CHUNK = 8
N_EXPERTS = 8

LANES = 128
SUBLANES = 8
VMEM_LIMIT = 56 * 1024 * 1024


def _params(sem):
    return pltpu.CompilerParams(dimension_semantics=sem, vmem_limit_bytes=VMEM_LIMIT)


def _rms(x, g):
    return x * lax.rsqrt(jnp.mean(x * x, axis=-1, keepdims=True) + RMS_EPS) * g


def _layer_spec(arr, layer):
    return pl.BlockSpec((None,) + arr.shape[1:], lambda *_: (layer,) + (0,) * (arr.ndim - 1))


def _load_slabs(ref, row0, rows):
    return jnp.concatenate([ref[pl.ds(row0 * SUBLANES + s, rows, stride=SUBLANES), :]
                            for s in range(SUBLANES)], axis=1)


def _store_slabs(ref, row0, val):
    for s in range(SUBLANES):
        ref[pl.ds(row0 * SUBLANES + s, val.shape[0], stride=SUBLANES), :] = val[:, s * LANES:(s + 1) * LANES]


def _split3(x):
    hi = x.astype(BF16)
    r = x - hi.astype(F32)
    mid = r.astype(BF16)
    lo = (r - mid.astype(F32)).astype(BF16)
    return hi, mid, lo


INPROJ_STAGE_ROWS = 256
INPROJ_SUB = 256


def _inproj_kernel(x_ref, g_ref, w_hbm, bf_ref, q_ref, k_ref, v_ref, up_ref, us_ref, c_ref,
                   w_s, stage, sem, carry_ref, *, layer, tiles_per_seq):
    a = ATTN_WIDTH
    f_off = 3 * a
    p_off = f_off + ATTN_HEADS
    main = f_off + POOL_WIDTH + SSM_WIDTH

    @pl.when(pl.program_id(0) == 0)
    def _():
        rows = stage.shape[1]

        def copy(c):
            return pltpu.make_async_copy(w_hbm.at[layer, pl.ds(c * rows, rows)], stage.at[c % 2], sem.at[c % 2])

        nchunks = w_s.shape[0] // rows
        copy(0).start()
        for c in range(nchunks):
            if c + 1 < nchunks:
                copy(c + 1).start()
            copy(c).wait()
            blk = stage[c % 2]
            r = pl.ds(c * rows, rows)
            w_s[r, 0:f_off] = blk[:, 0:f_off].astype(BF16)
            w_s[r, f_off:main] = blk[:, p_off:p_off + main - f_off].astype(BF16)
            lane = lax.broadcasted_iota(jnp.int32, (rows, LANES), 1)
            w_s[r, main:] = jnp.where(lane < ATTN_HEADS, blk[:, f_off:f_off + LANES], 0.0).astype(BF16)

    @pl.when(pl.program_id(0) % tiles_per_seq == 0)
    def _():
        carry_ref[...] = jnp.zeros_like(carry_ref)

    tm = x_ref.shape[0]
    zs = []
    for s in range(tm // INPROJ_SUB):
        r = pl.ds(s * INPROJ_SUB, INPROJ_SUB)
        h = _rms(x_ref[r, :], g_ref[...]).astype(BF16)
        proj = jnp.dot(h, w_s[...], preferred_element_type=F32)
        q_ref[r, :] = (proj[:, 0:a] * (LOG2E * ATTN_HEAD_DIM ** -0.5)).astype(BF16)
        k_ref[r, :] = proj[:, a:2 * a].astype(BF16)
        v_ref[r, :] = proj[:, 2 * a:3 * a].astype(BF16)
        up_ref[r, :] = proj[:, 3 * a:3 * a + POOL_WIDTH].astype(BF16)
        us_ref[r, :] = proj[:, 3 * a + POOL_WIDTH:main]
        zs.append(proj[:, main:])

    z = jnp.concatenate(zs, axis=0) + bf_ref[...]
    logf = jnp.minimum(z, 0.0) - jnp.log(1.0 + jnp.exp(-jnp.abs(z)))
    row = lax.broadcasted_iota(jnp.int32, (LANES, LANES), 0)
    col = lax.broadcasted_iota(jnp.int32, (LANES, LANES), 1)
    tri = (row >= col).astype(BF16)
    hi, mid, _ = _split3(logf)
    carry = carry_ref[...]
    for blk in range(tm // LANES):
        rows = slice(blk * LANES, (blk + 1) * LANES)
        c = (jnp.dot(tri, hi[rows], preferred_element_type=F32)
             + jnp.dot(tri, mid[rows], preferred_element_type=F32)) + carry
        c_ref[rows, :] = c[:, :ATTN_HEADS]
        carry = c[LANES - 1:LANES, :]
    carry_ref[...] = carry


def _inproj(x2, g, w_in, b_f, *, layer, seq, tm=512):
    n, d = x2.shape
    n_in = w_in.shape[2]
    a = ATTN_WIDTH
    n_res = n_in - ATTN_HEADS + LANES
    out_shape = (
        jax.ShapeDtypeStruct((n, a), BF16), jax.ShapeDtypeStruct((n, a), BF16),
        jax.ShapeDtypeStruct((n, a), BF16), jax.ShapeDtypeStruct((n, POOL_WIDTH), BF16),
        jax.ShapeDtypeStruct((n, SSM_WIDTH), F32), jax.ShapeDtypeStruct((n, ATTN_HEADS), F32))
    row = lambda w: pl.BlockSpec((tm, w), lambda i: (i, 0))
    return pl.pallas_call(
        functools.partial(_inproj_kernel, layer=layer, tiles_per_seq=seq // tm),
        grid=(n // tm,),
        in_specs=[row(d), _layer_spec(g, layer), pl.BlockSpec(memory_space=pl.ANY), _layer_spec(b_f, layer)],
        out_specs=(row(a), row(a), row(a), row(POOL_WIDTH), row(SSM_WIDTH), row(ATTN_HEADS)),
        out_shape=out_shape,
        scratch_shapes=[pltpu.VMEM((d, n_res), BF16), pltpu.VMEM((2, INPROJ_STAGE_ROWS, n_in), F32),
                        pltpu.SemaphoreType.DMA((2,)), pltpu.VMEM((1, LANES), F32)],
        compiler_params=_params(("arbitrary",)),
        name="inproj",
    )(x2, g, w_in, b_f)


HEADS_PER_BLOCK = LANES // ATTN_HEAD_DIM


def _attn_kernel(q_ref, k_ref, v_ref, ct_ref, o_ref, *, tq, tk):
    nlb = q_ref.shape[2] // LANES
    hp = pl.program_id(1)
    qi = pl.program_id(2)
    nkb = k_ref.shape[1] // tk
    lane = lax.broadcasted_iota(jnp.int32, (1, LANES), 1)
    in_head = [(lane // ATTN_HEAD_DIM) == hh for hh in range(HEADS_PER_BLOCK)]
    chains = [(lb, hh) for lb in range(nlb) for hh in range(HEADS_PER_BLOCK)]
    qm, crow, cref = [], [], []
    for lb in range(nlb):
        q2 = q_ref[0, :, lb * LANES:(lb + 1) * LANES]
        qm.append(jnp.concatenate([jnp.where(msk, q2, jnp.zeros_like(q2)) for msk in in_head], axis=0))
    for lb, hh in chains:
        crow.append(((hp * nlb + lb) * HEADS_PER_BLOCK + hh) * nkb)
        cref.append(ct_ref[0, pl.ds(crow[-1] + qi, 1), :][:, 0:1])

    def step(kb, carry, masked):
        new = []
        scores = {}
        for ci, (lb, hh) in enumerate(chains):
            kblk = k_ref[0, pl.ds(kb * tk, tk), lb * LANES:(lb + 1) * LANES]
            vblk = v_ref[0, pl.ds(kb * tk, tk), lb * LANES:(lb + 1) * LANES]
            m, acc = carry[ci]
            if lb not in scores:
                scores[lb] = lax.dot_general(qm[lb], kblk, (((1,), (1,)), ((), ())), preferred_element_type=F32)
            s = scores[lb][hh * tq:(hh + 1) * tq, :]
            s = s - (ct_ref[0, pl.ds(crow[ci] + kb, 1), :] - cref[ci]) * LOG2E
            if masked:
                r = lax.broadcasted_iota(jnp.int32, (tq, tk), 0)
                cidx = lax.broadcasted_iota(jnp.int32, (tq, tk), 1)
                s = jnp.where(cidx <= r, s, NEG_INF)
            m_new = jnp.maximum(m, jnp.max(s, axis=1, keepdims=True))
            p = jnp.exp2(s - m_new)
            vsel = jnp.where(in_head[hh], vblk, jnp.ones_like(vblk))
            acc = jnp.exp2(m - m_new) * acc + jnp.dot(p.astype(BF16), vsel, preferred_element_type=F32)
            new.append((m_new, acc))
        return tuple(new)

    init = tuple((jnp.full((tq, 1), NEG_INF, F32), jnp.zeros((tq, LANES), F32)) for _ in chains)
    carry = lax.fori_loop(0, qi, lambda kb, cr: step(kb, cr, False), init)
    carry = step(qi, carry, True)
    for lb in range(nlb):
        out = jnp.zeros((tq, LANES), F32)
        for hh in range(HEADS_PER_BLOCK):
            acc = carry[lb * HEADS_PER_BLOCK + hh][1]
            out = jnp.where(in_head[hh], acc / pltpu.roll(acc, ATTN_HEAD_DIM, 1), out)
        o_ref[0, :, lb * LANES:(lb + 1) * LANES] = out.astype(o_ref.dtype)


def _attention(q, k, v, ct, *, tq, lane_blocks=4):
    b, L, a = q.shape
    tk = ct.shape[2]
    assert tq == tk, "the diagonal key block of query tile i must be key block i"
    w = lane_blocks * LANES
    blk = lambda: pl.BlockSpec((1, tq, w), lambda bi, hi, qi: (bi, qi, hi))
    seq = lambda: pl.BlockSpec((1, L, w), lambda bi, hi, qi: (bi, 0, hi))
    return pl.pallas_call(
        functools.partial(_attn_kernel, tq=tq, tk=tk),
        grid=(b, a // w, L // tq),
        in_specs=[blk(), seq(), seq(),
                  pl.BlockSpec((1, ct.shape[1], tk), lambda bi, hi, qi: (bi, 0, 0))],
        out_specs=blk(),
        out_shape=jax.ShapeDtypeStruct((b, L, a), BF16),
        compiler_params=_params(("arbitrary", "arbitrary", "arbitrary")),
        name="fox_attention",
    )(q, k, v, ct)


def _pool_kernel(u_ref, w_ref, s_ref, o_ref):
    x = u_ref[0].astype(F32)
    L, w = x.shape
    row = lax.broadcasted_iota(jnp.int32, (L, w), 0)
    group = lax.broadcasted_iota(jnp.int32, (L, w), 1) // POOL_GROUP_DIM

    def shifted(y, s):
        return jnp.where(row >= s, pltpu.roll(y, s, 0), 0.0)

    acc = x
    sel = jnp.zeros_like(x)
    win_lane = jnp.zeros((L, w), F32)
    span = 1
    for gi, win in enumerate(POOL_WINDOWS):
        while span < win:
            acc = acc + shifted(acc, span)
            span *= 2
        sel = jnp.where(group == gi, acc, sel)
        win_lane = jnp.where(group == gi, float(win), win_lane)
    count = jnp.minimum(row.astype(F32) + 1.0, win_lane)
    d = sel / count - x
    y = jnp.dot(d.astype(BF16), w_ref[...], preferred_element_type=F32) * s_ref[...]
    o_ref[0] = y.astype(o_ref.dtype)


def _pool(u, w_bd, scale, *, layer):
    b, L, w = u.shape
    return pl.pallas_call(
        _pool_kernel,
        grid=(b,),
        in_specs=[pl.BlockSpec((1, L, w), lambda i: (i, 0, 0)),
                  _layer_spec(w_bd, layer), _layer_spec(scale, layer)],
        out_specs=pl.BlockSpec((1, L, w), lambda i: (i, 0, 0)),
        out_shape=jax.ShapeDtypeStruct((b, L, w), BF16),
        compiler_params=_params(("arbitrary",)),
        name="pool_mixer",
    )(u, w_bd, scale)


def _group_of(idx, width):
    groups = LANES // SSM_GROUP_DIM
    return (idx >> (width.bit_length() - 1)) & (groups - 1)


def _expand_blockdiag(r_ref, row_inner, col_inner):
    rows, w = r_ref.shape[1:]
    groups = LANES // SSM_GROUP_DIM
    r = lax.broadcasted_iota(jnp.int32, (w, groups * w), 0)
    c = lax.broadcasted_iota(jnp.int32, (w, groups * w), 1)
    shift = col_inner.bit_length() - 1
    outer_c = c >> (shift + groups.bit_length() - 1)
    spread = (outer_c == (r >> shift)) & ((c & (col_inner - 1)) == (r & (col_inner - 1)))
    big = jnp.dot(r_ref[0], spread.astype(BF16), preferred_element_type=F32)
    ri = lax.broadcasted_iota(jnp.int32, (rows, groups * w), 0)
    ci = lax.broadcasted_iota(jnp.int32, (rows, groups * w), 1)
    return jnp.where(_group_of(ri, row_inner) == _group_of(ci, col_inner), big, 0.0).astype(BF16)


def _ssm_kernel(u_ref, rm_ref, rz_ref, ry_ref, coef_ref, y_ref, m_s, wz_s, wy_s, ucat, z_ref, s_ref, *, seqs):
    t = SSM_CHUNK

    @pl.when(pl.program_id(1) == 0)
    def _():
        m_s[...] = _expand_blockdiag(rm_ref, SSM_GROUP_DIM, SSM_GROUP_DIM)
        wz_s[...] = _expand_blockdiag(rz_ref, SSM_GROUP_DIM, SSM_STATE)
        wy_s[...] = _expand_blockdiag(ry_ref, SSM_STATE, SSM_GROUP_DIM)

    nc = u_ref.shape[0] // t
    cps = nc // seqs
    for k in range(t):
        ucat[:, k * LANES:(k + 1) * LANES] = u_ref[pl.ds(k, nc, stride=t), :].astype(BF16)
    u = ucat[...]
    z_ref[...] = jnp.dot(u, wz_s[...], preferred_element_type=F32)
    half = z_ref.shape[1] // 2
    a1 = coef_ref[0, 0:1, :]
    a2 = coef_ref[0, 1:2, :]

    def step(c, states):
        new = []
        for b in range(seqs):
            st = states[b]
            r = b * cps + c
            s_ref[pl.ds(r, 1), :] = st
            sw = jnp.concatenate([st[:, half:], st[:, :half]], axis=1)
            new.append(a1 * st + a2 * sw + z_ref[pl.ds(r, 1), :])
        return tuple(new)

    init = tuple(jnp.zeros((1, 2 * half), F32) for _ in range(seqs))
    lax.fori_loop(0, cps, step, init, unroll=8)
    y = jnp.dot(u, m_s[...], preferred_element_type=F32)
    y = y + jnp.dot(s_ref[...].astype(BF16), wy_s[...], preferred_element_type=F32)
    for k in range(t):
        y_ref[pl.ds(k, nc, stride=t), :] = y[:, k * LANES:(k + 1) * LANES]


def _ssm_core(u, rm, rz, ry, coef, *, layer, seq, max_seqs_per_block=4):
    n, w = u.shape
    seqs_per_block = math.gcd(max_seqs_per_block, n // seq)
    rows = seq * seqs_per_block
    nc = rows // SSM_CHUNK
    kw = SSM_CHUNK * LANES
    p2 = coef.shape[3]
    blk = pl.BlockSpec((rows, LANES), lambda hf, r: (r, hf))
    wspec = lambda arr: pl.BlockSpec((None, 1) + arr.shape[2:], lambda hf, r: (layer, hf, 0, 0))
    return pl.pallas_call(
        functools.partial(_ssm_kernel, seqs=seqs_per_block),
        grid=(w // LANES, n // rows),
        in_specs=[blk, wspec(rm), wspec(rz), wspec(ry), wspec(coef)],
        out_specs=blk,
        out_shape=jax.ShapeDtypeStruct((n, w), F32),
        scratch_shapes=[pltpu.VMEM((kw, kw), BF16), pltpu.VMEM((kw, p2), BF16), pltpu.VMEM((p2, kw), BF16),
                        pltpu.VMEM((nc, kw), BF16), pltpu.VMEM((nc, p2), F32), pltpu.VMEM((nc, p2), F32)],
        compiler_params=_params(("arbitrary", "arbitrary")),
        name="s5_core",
    )(u, rm, rz, ry, coef)


def _ssm_matrices(a_re, a_im, b_re, b_im, c_re, c_im, d, log_dt):
    t = SSM_CHUNK
    g, p = a_re.shape
    hdim = d.shape[1]
    gl = LANES // hdim
    nb = g // gl
    lam = lax.complex(a_re, a_im)
    lam_dt = lam * jnp.exp(log_dt)[:, None]
    lam_bar = jnp.exp(lam_dt)
    b_bar = ((lam_bar - 1.0) / lam)[:, :, None] * lax.complex(b_re, b_im)
    cmat = lax.complex(c_re, c_im)
    steps = jnp.arange(t + 1, dtype=F32)
    pw = jnp.exp(lam_dt[:, None, :] * steps[None, :, None])
    kern = jnp.einsum('gop,gdp,gpi->gdoi', cmat, pw[:, :t], b_bar).real
    kern = kern.at[:, 0].add(jax.vmap(jnp.diag)(d))
    j = jnp.arange(t)[:, None]
    i = jnp.arange(t)[None, :]
    m = jnp.where((i >= j)[None, :, :, None, None], kern[:, jnp.clip(i - j, 0, t - 1)], 0.0)
    wz = jnp.einsum('gjp,gpi->gjip', pw[:, t - 1 - jnp.arange(t)], b_bar)
    cw = jnp.einsum('gop,gip->giop', cmat, pw[:, 1:t + 1])
    rm = m.reshape(nb, gl, t, t, hdim, hdim).transpose(0, 2, 1, 5, 3, 4).reshape(nb, t * LANES, t * hdim)
    rz = jnp.stack([wz.real, wz.imag], axis=3).reshape(nb, gl, t, hdim, 2, p)
    rz = rz.transpose(0, 2, 1, 3, 4, 5).reshape(nb, t * LANES, 2 * p)
    ry = jnp.stack([cw.real, -cw.imag], axis=1).reshape(nb, gl, 2, t, hdim, p)
    ry = ry.transpose(0, 2, 1, 5, 3, 4).reshape(nb, 2 * gl * p, t * hdim)
    ar = pw[:, t].real.reshape(nb, gl * p)
    ai = pw[:, t].imag.reshape(nb, gl * p)
    coef = jnp.stack([jnp.concatenate([ar, ar], -1), jnp.concatenate([-ai, ai], -1)], axis=1)
    return rm.astype(BF16), rz.astype(BF16), ry.astype(BF16), coef.astype(F32)


MIX_SUB = 256


def _mixout_kernel(*refs, moe):
    (x_ref, attn_ref, pool_ref, ys_ref, wglu_ref, bglu_ref, ga_ref, gp_ref, gs_ref,
     wo_ref, gpost_ref, gpre_ref) = refs[:12]
    if moe:
        router_ref, xo_ref, h_ref, ri_ref, rg_ref, cnt_ref, cnt_acc = refs[12:]

        @pl.when(pl.program_id(0) == 0)
        def _():
            cnt_acc[...] = jnp.zeros_like(cnt_acc)
    else:
        xo_ref, h_ref = refs[12:]
    a = attn_ref.shape[1]
    pw = pool_ref.shape[1]
    h_parts = []
    for s in range(x_ref.shape[0] // MIX_SUB):
        r = pl.ds(s * MIX_SUB, MIX_SUB)
        attn_n = _rms(attn_ref[r, :].astype(F32), ga_ref[...]).astype(BF16)
        pool_n = _rms(pool_ref[r, :].astype(F32), gp_ref[...]).astype(BF16)
        y = jax.nn.gelu(ys_ref[r, :])
        gate = jnp.dot(y.astype(BF16), wglu_ref[...], preferred_element_type=F32) + bglu_ref[...]
        ssm = y * jax.nn.sigmoid(gate)
        ssm_n = _rms(ssm, gs_ref[...]).astype(BF16)
        mix = (jnp.dot(attn_n, wo_ref[0:a, :], preferred_element_type=F32)
               + jnp.dot(pool_n, wo_ref[a:a + pw, :], preferred_element_type=F32)
               + jnp.dot(ssm_n, wo_ref[a + pw:, :], preferred_element_type=F32))
        x = x_ref[r, :] + _rms(mix, gpost_ref[...])
        xo_ref[r, :] = x
        h = _rms(x, gpre_ref[...])
        if moe:
            _store_slabs(h_ref, s * MIX_SUB, h)
            h_parts.append(h)
        else:
            h_ref[r, :] = h.astype(h_ref.dtype)
    if moe:
        hi, mid, _ = _split3(jnp.concatenate(h_parts, axis=0))
        rhi, rmid, _ = _split3(router_ref[...])
        dot_t = lambda p, q: lax.dot_general(p, q, (((1,), (1,)), ((), ())), preferred_element_type=F32)
        logits = dot_t(rhi, hi) + dot_t(rmid, hi) + dot_t(rhi, mid)
        ne, tm = logits.shape
        eidx = lax.broadcasted_iota(jnp.int32, (ne, tm), 0)
        m1 = jnp.max(logits, axis=0, keepdims=True)
        i1 = jnp.min(jnp.where(logits == m1, eidx, ne), axis=0, keepdims=True)
        rest = jnp.where(eidx == i1, -jnp.inf, logits)
        m2 = jnp.max(rest, axis=0, keepdims=True)
        i2 = jnp.min(jnp.where(rest == m2, eidx, ne), axis=0, keepdims=True)
        e2 = jnp.exp(m2 - m1)
        g1 = 1.0 / (1.0 + e2)
        rg_ref[...] = jnp.where(eidx == 0, g1, jnp.where(eidx == 1, e2 * g1, 0.0))
        onehot = ((eidx == i1) | (eidx == i2)).astype(BF16)
        earlier = (lax.broadcasted_iota(jnp.int32, (tm, tm), 0)
                   < lax.broadcasted_iota(jnp.int32, (tm, tm), 1)).astype(BF16)
        before = jnp.dot(onehot, earlier, preferred_element_type=F32) + cnt_acc[:, 0:1]
        r1 = jnp.sum(jnp.where(eidx == i1, before, 0.0), axis=0, keepdims=True).astype(jnp.int32)
        r2 = jnp.sum(jnp.where(eidx == i2, before, 0.0), axis=0, keepdims=True).astype(jnp.int32)
        ri_ref[...] = jnp.where(eidx == 0, i1, jnp.where(eidx == 1, i2,
                                jnp.where(eidx == 2, r1, jnp.where(eidx == 3, r2, 0))))
        cnt_acc[...] += jnp.sum(onehot.astype(F32), axis=1, keepdims=True)
        cnt_ref[...] = cnt_acc[...]


def _mixout(x2, attn, pool, ys, w_glu, b_glu, g_attn, g_pool, g_ssm, w_out, g_post, g_pre,
            router=None, *, layer, moe_layer=None, tm=512):
    n, d = x2.shape
    moe = router is not None
    row = lambda w: pl.BlockSpec((tm, w), lambda i: (i, 0))
    ins = [x2, attn, pool, ys, w_glu, b_glu, g_attn, g_pool, g_ssm, w_out, g_post, g_pre]
    in_specs = ([row(d), row(attn.shape[1]), row(pool.shape[1]), row(ys.shape[1])]
                + [_layer_spec(t, layer) for t in ins[4:]])
    if moe:
        assert d == SUBLANES * LANES
        h_shape = jax.ShapeDtypeStruct((n * SUBLANES, LANES), F32)
        h_spec = pl.BlockSpec((tm * SUBLANES, LANES), lambda i: (i, 0))
    else:
        h_shape, h_spec = jax.ShapeDtypeStruct((n, d), BF16), row(d)
    out_shape = [jax.ShapeDtypeStruct((n, d), F32), h_shape]
    out_specs = [row(d), h_spec]
    scratch = []
    if moe:
        ins.append(router)
        in_specs.append(_layer_spec(router, moe_layer))
        ne = router.shape[1]
        col = pl.BlockSpec((ne, tm), lambda i: (0, i))
        out_shape += [jax.ShapeDtypeStruct((ne, n), jnp.int32), jax.ShapeDtypeStruct((ne, n), F32),
                      jax.ShapeDtypeStruct((ne, LANES), F32)]
        out_specs += [col, col, pl.BlockSpec((ne, LANES), lambda i: (0, 0))]
        scratch = [pltpu.VMEM((ne, LANES), F32)]
    return pl.pallas_call(
        functools.partial(_mixout_kernel, moe=moe),
        grid=(n // tm,),
        in_specs=in_specs,
        out_specs=tuple(out_specs),
        out_shape=tuple(out_shape),
        scratch_shapes=scratch,
        compiler_params=_params(("arbitrary",)),
        name="mix_out",
    )(*ins)


FFN_STAGE_CHUNKS = 4


def _stage_cast(src_hbm, dst, lane0, stage, sem):
    rows = stage.shape[1]
    width = src_hbm.shape[1]

    def copy(c):
        return pltpu.make_async_copy(src_hbm.at[pl.ds(c * rows, rows)], stage.at[c % 2], sem.at[c % 2])

    nchunks = src_hbm.shape[0] // rows
    copy(0).start()
    for c in range(nchunks):
        if c + 1 < nchunks:
            copy(c + 1).start()
        copy(c).wait()
        dst[pl.ds(c * rows, rows), lane0:lane0 + width] = stage[c % 2].astype(BF16)


def _ffn_kernel(h_ref, x_ref, wg_hbm, wu_hbm, wd_hbm, g_ref, o_ref, wgu_s, wd_s, stage_in, stage_out, sem, *, sub):
    fdim = wd_s.shape[0]

    @pl.when(pl.program_id(0) == 0)
    def _():
        _stage_cast(wg_hbm, wgu_s, 0, stage_in, sem)
        _stage_cast(wu_hbm, wgu_s, fdim, stage_in, sem)
        _stage_cast(wd_hbm, wd_s, 0, stage_out, sem)

    for r in range(h_ref.shape[0] // sub):
        rows = pl.ds(r * sub, sub)
        hgu = jnp.dot(h_ref[rows, :], wgu_s[...], preferred_element_type=F32)
        hg, hu = hgu[:, :fdim], hgu[:, fdim:]
        act = (hg * jax.nn.sigmoid(hg) * hu).astype(BF16)
        y = jnp.dot(act, wd_s[...], preferred_element_type=F32)
        o_ref[rows, :] = x_ref[rows, :] + _rms(y, g_ref[...])


def _ffn(h, x2, wg, wu, wd, g_post, *, layer, tm=512, sub=256):
    n, d = x2.shape
    fdim = wg.shape[1]
    row = pl.BlockSpec((tm, d), lambda i: (i, 0))
    hbm = pl.BlockSpec(memory_space=pl.ANY)
    return pl.pallas_call(
        functools.partial(_ffn_kernel, sub=sub),
        grid=(n // tm,),
        in_specs=[row, row, hbm, hbm, hbm, _layer_spec(g_post, layer)],
        out_specs=row,
        out_shape=jax.ShapeDtypeStruct((n, d), F32),
        scratch_shapes=[pltpu.VMEM((d, 2 * fdim), BF16), pltpu.VMEM((fdim, d), BF16),
                        pltpu.VMEM((2, d // FFN_STAGE_CHUNKS, fdim), F32),
                        pltpu.VMEM((2, fdim // FFN_STAGE_CHUNKS, d), F32),
                        pltpu.SemaphoreType.DMA((2,))],
        compiler_params=_params(("arbitrary",)),
        name="dense_ffn",
    )(h, x2, wg, wu, wd, g_post)


MOE_TILE = 512


def _moe_tables(counts, n_items):
    tm = MOE_TILE
    i32 = jnp.int32
    ne = counts.shape[0]
    assert n_items <= LANES
    counts = counts.astype(i32)
    ntiles = (counts + tm - 1) // tm
    k = jnp.arange(ne, dtype=i32)
    ends = jnp.sum(jnp.where(k[None, :] <= k[:, None], ntiles[None, :], 0), axis=1)
    starts = ends - ntiles
    later = (ntiles > 0)[None, :] & (k[None, :] > k[:, None])
    nxt = jnp.min(jnp.where(later, k[None, :], ne), axis=1)
    nxt = jnp.where(nxt < ne, nxt, -1)
    w = jnp.arange(LANES, dtype=i32)[None, :]
    wc = jnp.minimum(w, ends[-1] - 1)
    e = jnp.sum((wc >= ends[:, None]).astype(i32), axis=0, keepdims=True)
    of_item = lambda v: jnp.sum(jnp.where(k[:, None] == e, v[:, None], 0), axis=0, keepdims=True)
    valid = w < ends[-1]
    rowblock = jnp.where(valid, w, n_items)
    nvalid = jnp.where(valid, jnp.clip(of_item(counts) - (wc - of_item(starts)) * tm, 0, tm), 0)
    e_prev = jnp.concatenate([e[:, :1], e[:, :-1]], axis=1)
    first = ((w == 0) | (e != e_prev)).astype(i32)
    item = lambda v: v[0, :n_items].astype(i32)
    return (item(e), item(rowblock), item(nvalid), item(first), item(of_item(nxt))), starts * tm


def _slab(ref, token):
    return ref.at[pl.ds(pl.multiple_of(token * SUBLANES, SUBLANES), SUBLANES)]


def _dispatch_kernel(dest_ref, h_ref, xs_ref, sem, *, n):
    tm = h_ref.shape[0] // SUBLANES
    base = pl.program_id(0) * tm

    def body(r, carry):
        for k in range(2):
            d = dest_ref[k * n + base + r]
            pltpu.make_async_copy(_slab(h_ref, r), _slab(xs_ref, d), sem).start(priority=k)
        return carry

    lax.fori_loop(0, tm, body, 0, unroll=8)
    for k in range(2):
        pltpu.make_async_copy(h_ref, h_ref, sem).wait()


def _dispatch(dest, h, cap_rows, *, tm=1024):
    n = h.shape[0] // SUBLANES
    return pl.pallas_call(
        functools.partial(_dispatch_kernel, n=n),
        grid_spec=pltpu.PrefetchScalarGridSpec(
            num_scalar_prefetch=1,
            grid=(n // tm,),
            in_specs=[pl.BlockSpec((tm * SUBLANES, LANES), lambda i, dest: (i, 0))],
            out_specs=pl.BlockSpec(memory_space=pl.ANY),
            scratch_shapes=[pltpu.SemaphoreType.DMA(())]),
        out_shape=jax.ShapeDtypeStruct((cap_rows * SUBLANES, LANES), F32),
        compiler_params=_params(("arbitrary",)),
        name="moe_dispatch",
    )(dest, h)


MOE_SUB = 256


MOE_UP_CHUNKS = 4


def _moe_up_kernel(ie_ref, rb_ref, nv_ref, first_ref, nxt_ref, xs_ref, wg_hbm, wu_hbm, h_ref, wgu_s, stage, sem):
    f = pl.program_id(0)
    w = pl.program_id(1)
    nf = pl.num_programs(0)
    tf = h_ref.shape[1]
    rows = stage.shape[0] // MOE_UP_CHUNKS

    def copies(e, fp):
        return [pltpu.make_async_copy(src.at[e, pl.ds(c * rows, rows), pl.ds(fp * tf, tf)],
                                      stage.at[pl.ds(c * rows, rows), pl.ds(m * tf, tf)],
                                      sem.at[m * MOE_UP_CHUNKS + c])
                for m, src in enumerate((wg_hbm, wu_hbm)) for c in range(MOE_UP_CHUNKS)]

    @pl.when((f == 0) & (w == 0))
    def _():
        for cp in copies(ie_ref[0], 0):
            cp.start()

    is_first = first_ref[w] == 1
    last_expert = nxt_ref[w] < 0
    for fp in range(nf):
        @pl.when(is_first & (f == fp))
        def _():
            for cp in copies(ie_ref[w], fp):
                cp.wait()
            for c in range(MOE_UP_CHUNKS):
                r = pl.ds(c * rows, rows)
                wgu_s[r, :] = stage[r, :].astype(BF16)

            @pl.when(jnp.logical_not(last_expert))
            def _():
                for cp in copies(nxt_ref[w], fp):
                    cp.start()

            if fp + 1 < nf:
                @pl.when(last_expert)
                def _():
                    for cp in copies(ie_ref[0], fp + 1):
                        cp.start()

    nv = nv_ref[w]

    nsub = h_ref.shape[0] // MOE_SUB

    def compute(nchains):
        for r in range(nchains):
            rws = pl.ds(r * MOE_SUB, MOE_SUB)
            idx = r * MOE_SUB + lax.broadcasted_iota(jnp.int32, (MOE_SUB, 1), 0)
            x = _load_slabs(xs_ref, r * MOE_SUB, MOE_SUB)
            x = jnp.where(idx < nv, x, 0.0).astype(BF16)
            hgu = jnp.dot(x, wgu_s[...], preferred_element_type=F32)
            hg, hu = hgu[:, :tf], hgu[:, tf:]
            h_ref[rws, :] = (hg * jax.nn.sigmoid(hg) * hu).astype(BF16)
        if nchains < nsub:
            rest = pl.ds(nchains * MOE_SUB, (nsub - nchains) * MOE_SUB)
            h_ref[rest, :] = jnp.zeros((rest.size, tf), BF16)

    for nchains in range(nsub + 1):
        pl.when((nv > (nchains - 1) * MOE_SUB) & (nv <= nchains * MOE_SUB))(functools.partial(compute, nchains))


def _moe_up(tables, xs, wg, wu, *, tf=1792):
    tm = MOE_TILE
    cap_rows = xs.shape[0] // SUBLANES
    ne, d, fdim = wg.shape
    n_items = tables[0].shape[0]
    hbm = pl.BlockSpec(memory_space=pl.ANY)
    return pl.pallas_call(
        _moe_up_kernel,
        grid_spec=pltpu.PrefetchScalarGridSpec(
            num_scalar_prefetch=5,
            grid=(fdim // tf, n_items),
            in_specs=[pl.BlockSpec((tm * SUBLANES, LANES), lambda f, w, ie, rb, *_: (rb[w], 0)), hbm, hbm],
            out_specs=pl.BlockSpec((tm, tf), lambda f, w, ie, rb, *_: (rb[w], f)),
            scratch_shapes=[pltpu.VMEM((d, 2 * tf), BF16), pltpu.VMEM((d, 2 * tf), F32),
                            pltpu.SemaphoreType.DMA((2 * MOE_UP_CHUNKS,))]),
        out_shape=jax.ShapeDtypeStruct((cap_rows, fdim), BF16),
        compiler_params=_params(("arbitrary", "arbitrary")),
        name="moe_up",
    )(*tables, xs, wg, wu)


def _moe_down_kernel(ie_ref, rb_ref, nv_ref, first_ref, nxt_ref, h_ref, wd_hbm, y_ref, wd_s, stage, sem):
    w = pl.program_id(0)
    nchunks = sem.shape[0]
    chunk = stage.shape[0] // nchunks

    def copies(e):
        return [pltpu.make_async_copy(wd_hbm.at[e, pl.ds(c * chunk, chunk)], stage.at[pl.ds(c * chunk, chunk)],
                                      sem.at[c]) for c in range(nchunks)]

    @pl.when(w == 0)
    def _():
        for cp in copies(ie_ref[0]):
            cp.start()

    @pl.when(first_ref[w] == 1)
    def _():
        for c, cp in enumerate(copies(ie_ref[w])):
            cp.wait()
            wd_s[pl.ds(c * chunk, chunk), :] = stage[pl.ds(c * chunk, chunk), :].astype(BF16)

        @pl.when(nxt_ref[w] >= 0)
        def _():
            for cp in copies(nxt_ref[w]):
                cp.start()

    nv = nv_ref[w]
    nsub = h_ref.shape[0] // MOE_SUB

    def compute(nchains):
        for r in range(nchains):
            y = jnp.dot(h_ref[pl.ds(r * MOE_SUB, MOE_SUB), :], wd_s[...], preferred_element_type=F32)
            _store_slabs(y_ref, r * MOE_SUB, y)
        if nchains < nsub:
            rest = pl.ds(nchains * MOE_SUB * SUBLANES, (nsub - nchains) * MOE_SUB * SUBLANES)
            y_ref[rest, :] = jnp.zeros((rest.size, LANES), F32)

    for nchains in range(nsub + 1):
        pl.when((nv > (nchains - 1) * MOE_SUB) & (nv <= nchains * MOE_SUB))(functools.partial(compute, nchains))


def _moe_down(tables, hid, wd, *, chunk=512):
    tm = MOE_TILE
    cap_rows, fdim = hid.shape
    d = wd.shape[2]
    n_items = tables[0].shape[0]
    assert d == SUBLANES * LANES and fdim % chunk == 0
    return pl.pallas_call(
        _moe_down_kernel,
        grid_spec=pltpu.PrefetchScalarGridSpec(
            num_scalar_prefetch=5,
            grid=(n_items,),
            in_specs=[pl.BlockSpec((tm, fdim), lambda w, ie, rb, *_: (rb[w], 0)),
                      pl.BlockSpec(memory_space=pl.ANY)],
            out_specs=pl.BlockSpec((tm * SUBLANES, LANES), lambda w, ie, rb, *_: (rb[w], 0)),
            scratch_shapes=[pltpu.VMEM((fdim, d), BF16), pltpu.VMEM((fdim, d), F32),
                            pltpu.SemaphoreType.DMA((fdim // chunk,))]),
        out_shape=jax.ShapeDtypeStruct((cap_rows * SUBLANES, LANES), F32),
        compiler_params=_params(("arbitrary",)),
        name="moe_down",
    )(*tables, hid, wd)


def _combine_kernel(dest_ref, x_ref, gate_ref, g_ref, ys_ref, o_ref, buf, sem, *, n):
    tm = x_ref.shape[0]
    i = pl.program_id(0)

    def issue(tile, slot):
        def body(r, carry):
            for k in range(2):
                d = dest_ref[k * n + tile * tm + r]
                pltpu.make_async_copy(_slab(ys_ref, d), _slab(buf.at[slot, k], r),
                                      sem.at[slot]).start(priority=k)
            return carry
        lax.fori_loop(0, tm, body, 0, unroll=8)

    @pl.when(i == 0)
    def _():
        issue(0, 0)

    @pl.when(i + 1 < pl.num_programs(0))
    def _():
        issue(i + 1, (i + 1) % 2)

    slot = i % 2
    for k in range(2):
        pltpu.make_async_copy(buf.at[slot, k], buf.at[slot, k], sem.at[slot]).wait()
    gates = gate_ref[...].T
    f = (gates[:, 0:1] * _load_slabs(buf.at[slot, 0], 0, tm)
         + gates[:, 1:2] * _load_slabs(buf.at[slot, 1], 0, tm))
    o_ref[...] = x_ref[...] + _rms(f, g_ref[...])


def _combine(dest, x2, gates, g_post, ys, *, layer, tm=1024):
    n, d = x2.shape
    return pl.pallas_call(
        functools.partial(_combine_kernel, n=n),
        grid_spec=pltpu.PrefetchScalarGridSpec(
            num_scalar_prefetch=1,
            grid=(n // tm,),
            in_specs=[pl.BlockSpec((tm, d), lambda i, dest: (i, 0)),
                      pl.BlockSpec((gates.shape[0], tm), lambda i, dest: (0, i)),
                      _layer_spec(g_post, layer),
                      pl.BlockSpec(memory_space=pl.ANY)],
            out_specs=pl.BlockSpec((tm, d), lambda i, dest: (i, 0)),
            scratch_shapes=[pltpu.VMEM((2, 2, tm * SUBLANES, LANES), F32), pltpu.SemaphoreType.DMA((2,))]),
        out_shape=jax.ShapeDtypeStruct((n, d), F32),
        compiler_params=_params(("arbitrary",)),
        name="moe_combine",
    )(dest, x2, gates, g_post, ys)


def _moe(h, x2, route_i, route_g, counts, wg, wu, wd, g_post, *, layer):
    n = x2.shape[0]
    n_items = 2 * n // MOE_TILE + N_EXPERTS
    tables, base = _moe_tables(counts[:, 0], n_items)
    base_of = lambda e: jnp.sum(jnp.where(e[None, :] == jnp.arange(N_EXPERTS)[:, None], base[:, None], 0), axis=0)
    dest = jnp.concatenate([base_of(route_i[0]) + route_i[2], base_of(route_i[1]) + route_i[3]])
    xs = _dispatch(dest, h, (n_items + 1) * MOE_TILE)
    hid = _moe_up(tables, xs, wg, wu)
    ys = _moe_down(tables, hid, wd)
    return _combine(dest, x2, route_g, g_post, ys, layer=layer)


def _rows(v):
    return v.reshape(v.shape[0], 1, -1).astype(F32)


def kernel(x, norm_mix_pre, norm_mix_post, norm_ffn_pre, norm_ffn_post, w_in, b_forget, pool_w, pool_scale, ssm_a_re, ssm_a_im, ssm_b_re, ssm_b_im, ssm_c_re, ssm_c_im, ssm_d, ssm_log_dt, ssm_w_glu, ssm_b_glu, branch_norm_attn, branch_norm_pool, branch_norm_ssm, w_out, ffn_w_gate, ffn_w_up, ffn_w_down, moe_router, moe_w_gate, moe_w_up, moe_w_down):
    b, L, d = x.shape
    depth = w_in.shape[0]
    n = b * L
    a = ATTN_WIDTH
    attn_tile = 512
    x2 = x.reshape(n, d)
    g_mix_pre, g_mix_post = _rows(norm_mix_pre), _rows(norm_mix_post)
    g_ffn_pre, g_ffn_post = _rows(norm_ffn_pre), _rows(norm_ffn_post)
    g_attn, g_pool, g_ssm = _rows(branch_norm_attn), _rows(branch_norm_pool), _rows(branch_norm_ssm)
    b_f = _rows(jnp.pad(b_forget, ((0, 0), (0, LANES - ATTN_HEADS))))
    pool_bd = jax.vmap(lambda w: jax.scipy.linalg.block_diag(*w))(pool_w).astype(BF16)
    pool_sc, b_glu = _rows(pool_scale), _rows(ssm_b_glu)
    w_glu, w_o = ssm_w_glu.astype(BF16), w_out.astype(BF16)
    mats = jax.vmap(_ssm_matrices)(ssm_a_re, ssm_a_im, ssm_b_re, ssm_b_im, ssm_c_re, ssm_c_im, ssm_d, ssm_log_dt)
    router_t = jnp.swapaxes(moe_router, 1, 2)
    for i in range(depth):
        q, k, v, up, us, c = _inproj(x2, g_mix_pre, w_in, b_f, layer=i, seq=L)

        ct = c.reshape(b, L, ATTN_HEADS).transpose(0, 2, 1).reshape(b, ATTN_HEADS * (L // attn_tile), attn_tile)
        attn = _attention(q.reshape(b, L, a), k.reshape(b, L, a), v.reshape(b, L, a),
                          ct, tq=attn_tile).reshape(n, a)
        pool = _pool(up.reshape(b, L, POOL_WIDTH), pool_bd, pool_sc, layer=i).reshape(n, POOL_WIDTH)
        ys = _ssm_core(us, *mats, layer=i, seq=L)

        moe = i % 2 == 1
        j = i // 2
        res = _mixout(x2, attn, pool, ys, w_glu, b_glu, g_attn, g_pool, g_ssm, w_o, g_mix_post, g_ffn_pre,
                      router_t if moe else None, layer=i, moe_layer=j)
        if moe:
            x2, h, route_i, route_g, counts = res
            x2 = _moe(h, x2, route_i, route_g, counts, moe_w_gate[j], moe_w_up[j], moe_w_down[j],
                      g_ffn_post, layer=i)
        else:
            x2, h = res
            x2 = _ffn(h, x2, ffn_w_gate[j], ffn_w_up[j], ffn_w_down[j], g_ffn_post, layer=i)
    return x2.reshape(b, L, d)
```

```python
import functools
import math

import jax
import jax.numpy as jnp
from jax import lax
from jax.experimental import pallas as pl
from jax.experimental.pallas import tpu as pltpu

F32 = jnp.float32
BF16 = jnp.bfloat16

RMS_EPS = 1e-6
NEG_INF = -1e30
LOG2E = 1.4426950408889634

ATTN_HEADS = 8
ATTN_HEAD_DIM = 64
ATTN_WIDTH = ATTN_HEADS * ATTN_HEAD_DIM
POOL_WINDOWS = (2, 4, 8, 16)
POOL_GROUP_DIM = 64
POOL_WIDTH = len(POOL_WINDOWS) * POOL_GROUP_DIM
SSM_GROUPS = 16
SSM_GROUP_DIM = 16
SSM_STATE = 64
SSM_WIDTH = SSM_GROUPS * SSM_GROUP_DIM
SSM_CHUNK = 8
N_EXPERTS = 8

LANES = 128
SUBLANES = 8
VMEM_LIMIT = 56 * 1024 * 1024


def _params(sem):
    return pltpu.CompilerParams(dimension_semantics=sem, vmem_limit_bytes=VMEM_LIMIT)


def _rms(x, g):
    return x * lax.rsqrt(jnp.mean(x * x, axis=-1, keepdims=True) + RMS_EPS) * g


def _layer_spec(arr, layer):
    return pl.BlockSpec((None,) + arr.shape[1:], lambda *_: (layer,) + (0,) * (arr.ndim - 1))


def _load_slabs(ref, row0, rows):
    return jnp.concatenate([ref[pl.ds(row0 * SUBLANES + s, rows, stride=SUBLANES), :]
                            for s in range(SUBLANES)], axis=1)


def _store_slabs(ref, row0, val):
    for s in range(SUBLANES):
        ref[pl.ds(row0 * SUBLANES + s, val.shape[0], stride=SUBLANES), :] = val[:, s * LANES:(s + 1) * LANES]


def _split3(x):
    hi = x.astype(BF16)
    r = x - hi.astype(F32)
    mid = r.astype(BF16)
    lo = (r - mid.astype(F32)).astype(BF16)
    return hi, mid, lo


INPROJ_STAGE_ROWS = 256
INPROJ_SUB = 256


def _inproj_kernel(x_ref, g_ref, w_hbm, bf_ref, q_ref, k_ref, v_ref, up_ref, us_ref, c_ref,
                   w_s, stage, sem, carry_ref, *, layer, tiles_per_seq):
    a = ATTN_WIDTH
    f_off = 3 * a
    p_off = f_off + ATTN_HEADS
    main = f_off + POOL_WIDTH + SSM_WIDTH

    @pl.when(pl.program_id(0) == 0)
    def _():
        rows = stage.shape[1]

        def copy(c):
            return pltpu.make_async_copy(w_hbm.at[layer, pl.ds(c * rows, rows)], stage.at[c % 2], sem.at[c % 2])

        nchunks = w_s.shape[0] // rows
        copy(0).start()
        for c in range(nchunks):
            if c + 1 < nchunks:
                copy(c + 1).start()
            copy(c).wait()
            blk = stage[c % 2]
            r = pl.ds(c * rows, rows)
            w_s[r, 0:f_off] = blk[:, 0:f_off].astype(BF16)
            w_s[r, f_off:main] = blk[:, p_off:p_off + main - f_off].astype(BF16)
            lane = lax.broadcasted_iota(jnp.int32, (rows, LANES), 1)
            w_s[r, main:] = jnp.where(lane < ATTN_HEADS, blk[:, f_off:f_off + LANES], 0.0).astype(BF16)

    @pl.when(pl.program_id(0) % tiles_per_seq == 0)
    def _():
        carry_ref[...] = jnp.zeros_like(carry_ref)

    tm = x_ref.shape[0]
    zs = []
    for s in range(tm // INPROJ_SUB):
        r = pl.ds(s * INPROJ_SUB, INPROJ_SUB)
        h = _rms(x_ref[r, :], g_ref[...]).astype(BF16)
        proj = jnp.dot(h, w_s[...], preferred_element_type=F32)
        q_ref[r, :] = (proj[:, 0:a] * (LOG2E * ATTN_HEAD_DIM ** -0.5)).astype(BF16)
        k_ref[r, :] = proj[:, a:2 * a].astype(BF16)
        v_ref[r, :] = proj[:, 2 * a:3 * a].astype(BF16)
        up_ref[r, :] = proj[:, 3 * a:3 * a + POOL_WIDTH].astype(BF16)
        us_ref[r, :] = proj[:, 3 * a + POOL_WIDTH:main]
        zs.append(proj[:, main:])

    z = jnp.concatenate(zs, axis=0) + bf_ref[...]
    logf = jnp.minimum(z, 0.0) - jnp.log(1.0 + jnp.exp(-jnp.abs(z)))
    row = lax.broadcasted_iota(jnp.int32, (LANES, LANES), 0)
    col = lax.broadcasted_iota(jnp.int32, (LANES, LANES), 1)
    tri = (row >= col).astype(BF16)
    hi, mid, _ = _split3(logf)
    carry = carry_ref[...]
    for blk in range(tm // LANES):
        rows = slice(blk * LANES, (blk + 1) * LANES)
        c = (jnp.dot(tri, hi[rows], preferred_element_type=F32)
             + jnp.dot(tri, mid[rows], preferred_element_type=F32)) + carry
        c_ref[rows, :] = c[:, :ATTN_HEADS]
        carry = c[LANES - 1:LANES, :]
    carry_ref[...] = carry


def _inproj(x2, g, w_in, b_f, *, layer, seq, tm=512):
    n, d = x2.shape
    n_in = w_in.shape[2]
    a = ATTN_WIDTH
    n_res = n_in - ATTN_HEADS + LANES
    out_shape = (
        jax.ShapeDtypeStruct((n, a), BF16), jax.ShapeDtypeStruct((n, a), BF16),
        jax.ShapeDtypeStruct((n, a), BF16), jax.ShapeDtypeStruct((n, POOL_WIDTH), BF16),
        jax.ShapeDtypeStruct((n, SSM_WIDTH), F32), jax.ShapeDtypeStruct((n, ATTN_HEADS), F32))
    row = lambda w: pl.BlockSpec((tm, w), lambda i: (i, 0))
    return pl.pallas_call(
        functools.partial(_inproj_kernel, layer=layer, tiles_per_seq=seq // tm),
        grid=(n // tm,),
        in_specs=[row(d), _layer_spec(g, layer), pl.BlockSpec(memory_space=pl.ANY), _layer_spec(b_f, layer)],
        out_specs=(row(a), row(a), row(a), row(POOL_WIDTH), row(SSM_WIDTH), row(ATTN_HEADS)),
        out_shape=out_shape,
        scratch_shapes=[pltpu.VMEM((d, n_res), BF16), pltpu.VMEM((2, INPROJ_STAGE_ROWS, n_in), F32),
                        pltpu.SemaphoreType.DMA((2,)), pltpu.VMEM((1, LANES), F32)],
        compiler_params=_params(("arbitrary",)),
        name="inproj",
    )(x2, g, w_in, b_f)


HEADS_PER_BLOCK = LANES // ATTN_HEAD_DIM


def _attn_kernel(q_ref, k_ref, v_ref, ct_ref, o_ref, *, tq, tk):
    nlb = q_ref.shape[2] // LANES
    hp = pl.program_id(1)
    qi = pl.program_id(2)
    nkb = k_ref.shape[1] // tk
    lane = lax.broadcasted_iota(jnp.int32, (1, LANES), 1)
    in_head = [(lane // ATTN_HEAD_DIM) == hh for hh in range(HEADS_PER_BLOCK)]
    chains = [(lb, hh) for lb in range(nlb) for hh in range(HEADS_PER_BLOCK)]
    qm, crow, cref = [], [], []
    for lb in range(nlb):
        q2 = q_ref[0, :, lb * LANES:(lb + 1) * LANES]
        qm.append(jnp.concatenate([jnp.where(msk, q2, jnp.zeros_like(q2)) for msk in in_head], axis=0))
    for lb, hh in chains:
        crow.append(((hp * nlb + lb) * HEADS_PER_BLOCK + hh) * nkb)
        cref.append(ct_ref[0, pl.ds(crow[-1] + qi, 1), :][:, 0:1])

    def step(kb, carry, masked):
        new = []
        scores = {}
        for ci, (lb, hh) in enumerate(chains):
            kblk = k_ref[0, pl.ds(kb * tk, tk), lb * LANES:(lb + 1) * LANES]
            vblk = v_ref[0, pl.ds(kb * tk, tk), lb * LANES:(lb + 1) * LANES]
            m, acc = carry[ci]
            if lb not in scores:
                scores[lb] = lax.dot_general(qm[lb], kblk, (((1,), (1,)), ((), ())), preferred_element_type=F32)
            s = scores[lb][hh * tq:(hh + 1) * tq, :]
            s = s - (ct_ref[0, pl.ds(crow[ci] + kb, 1), :] - cref[ci]) * LOG2E
            if masked:
                r = lax.broadcasted_iota(jnp.int32, (tq, tk), 0)
                cidx = lax.broadcasted_iota(jnp.int32, (tq, tk), 1)
                s = jnp.where(cidx <= r, s, NEG_INF)
            m_new = jnp.maximum(m, jnp.max(s, axis=1, keepdims=True))
            p = jnp.exp2(s - m_new)
            vsel = jnp.where(in_head[hh], vblk, jnp.ones_like(vblk))
            acc = jnp.exp2(m - m_new) * acc + jnp.dot(p.astype(BF16), vsel, preferred_element_type=F32)
            new.append((m_new, acc))
        return tuple(new)

    init = tuple((jnp.full((tq, 1), NEG_INF, F32), jnp.zeros((tq, LANES), F32)) for _ in chains)
    carry = lax.fori_loop(0, qi, lambda kb, cr: step(kb, cr, False), init)
    carry = step(qi, carry, True)
    for lb in range(nlb):
        out = jnp.zeros((tq, LANES), F32)
        for hh in range(HEADS_PER_BLOCK):
            acc = carry[lb * HEADS_PER_BLOCK + hh][1]
            out = jnp.where(in_head[hh], acc / pltpu.roll(acc, ATTN_HEAD_DIM, 1), out)
        o_ref[0, :, lb * LANES:(lb + 1) * LANES] = out.astype(o_ref.dtype)


def _attention(q, k, v, ct, *, tq, lane_blocks=4):
    b, L, a = q.shape
    tk = ct.shape[2]
    assert tq == tk, "the diagonal key block of query tile i must be key block i"
    w = lane_blocks * LANES
    blk = lambda: pl.BlockSpec((1, tq, w), lambda bi, hi, qi: (bi, qi, hi))
    seq = lambda: pl.BlockSpec((1, L, w), lambda bi, hi, qi: (bi, 0, hi))
    return pl.pallas_call(
        functools.partial(_attn_kernel, tq=tq, tk=tk),
        grid=(b, a // w, L // tq),
        in_specs=[blk(), seq(), seq(),
                  pl.BlockSpec((1, ct.shape[1], tk), lambda bi, hi, qi: (bi, 0, 0))],
        out_specs=blk(),
        out_shape=jax.ShapeDtypeStruct((b, L, a), BF16),
        compiler_params=_params(("arbitrary", "arbitrary", "arbitrary")),
        name="fox_attention",
    )(q, k, v, ct)


def _pool_kernel(u_ref, w_ref, s_ref, o_ref):
    x = u_ref[0].astype(F32)
    L, w = x.shape
    row = lax.broadcasted_iota(jnp.int32, (L, w), 0)
    group = lax.broadcasted_iota(jnp.int32, (L, w), 1) // POOL_GROUP_DIM

    def shifted(y, s):
        return jnp.where(row >= s, pltpu.roll(y, s, 0), 0.0)

    acc = x
    sel = jnp.zeros_like(x)
    win_lane = jnp.zeros((L, w), F32)
    span = 1
    for gi, win in enumerate(POOL_WINDOWS):
        while span < win:
            acc = acc + shifted(acc, span)
            span *= 2
        sel = jnp.where(group == gi, acc, sel)
        win_lane = jnp.where(group == gi, float(win), win_lane)
    count = jnp.minimum(row.astype(F32) + 1.0, win_lane)
    d = sel / count - x
    y = jnp.dot(d.astype(BF16), w_ref[...], preferred_element_type=F32) * s_ref[...]
    o_ref[0] = y.astype(o_ref.dtype)


def _pool(u, w_bd, scale, *, layer):
    b, L, w = u.shape
    return pl.pallas_call(
        _pool_kernel,
        grid=(b,),
        in_specs=[pl.BlockSpec((1, L, w), lambda i: (i, 0, 0)),
                  _layer_spec(w_bd, layer), _layer_spec(scale, layer)],
        out_specs=pl.BlockSpec((1, L, w), lambda i: (i, 0, 0)),
        out_shape=jax.ShapeDtypeStruct((b, L, w), BF16),
        compiler_params=_params(("arbitrary",)),
        name="pool_mixer",
    )(u, w_bd, scale)


def _group_of(idx, width):
    groups = LANES // SSM_GROUP_DIM
    return (idx >> (width.bit_length() - 1)) & (groups - 1)


def _expand_blockdiag(r_ref, row_inner, col_inner):
    rows, w = r_ref.shape[1:]
    groups = LANES // SSM_GROUP_DIM
    r = lax.broadcasted_iota(jnp.int32, (w, groups * w), 0)
    c = lax.broadcasted_iota(jnp.int32, (w, groups * w), 1)
    shift = col_inner.bit_length() - 1
    outer_c = c >> (shift + groups.bit_length() - 1)
    spread = (outer_c == (r >> shift)) & ((c & (col_inner - 1)) == (r & (col_inner - 1)))
    big = jnp.dot(r_ref[0], spread.astype(BF16), preferred_element_type=F32)
    ri = lax.broadcasted_iota(jnp.int32, (rows, groups * w), 0)
    ci = lax.broadcasted_iota(jnp.int32, (rows, groups * w), 1)
    return jnp.where(_group_of(ri, row_inner) == _group_of(ci, col_inner), big, 0.0).astype(BF16)


def _ssm_kernel(u_ref, rm_ref, rz_ref, ry_ref, coef_ref, y_ref, m_s, wz_s, wy_s, ucat, z_ref, s_ref, *, seqs):
    t = SSM_CHUNK

    @pl.when(pl.program_id(1) == 0)
    def _():
        m_s[...] = _expand_blockdiag(rm_ref, SSM_GROUP_DIM, SSM_GROUP_DIM)
        wz_s[...] = _expand_blockdiag(rz_ref, SSM_GROUP_DIM, SSM_STATE)
        wy_s[...] = _expand_blockdiag(ry_ref, SSM_STATE, SSM_GROUP_DIM)

    nc = u_ref.shape[0] // t
    cps = nc // seqs
    for k in range(t):
        ucat[:, k * LANES:(k + 1) * LANES] = u_ref[pl.ds(k, nc, stride=t), :].astype(BF16)
    u = ucat[...]
    z_ref[...] = jnp.dot(u, wz_s[...], preferred_element_type=F32)
    half = z_ref.shape[1] // 2
    a1 = coef_ref[0, 0:1, :]
    a2 = coef_ref[0, 1:2, :]

    def step(c, states):
        new = []
        for b in range(seqs):
            st = states[b]
            r = b * cps + c
            s_ref[pl.ds(r, 1), :] = st
            sw = jnp.concatenate([st[:, half:], st[:, :half]], axis=1)
            new.append(a1 * st + a2 * sw + z_ref[pl.ds(r, 1), :])
        return tuple(new)

    init = tuple(jnp.zeros((1, 2 * half), F32) for _ in range(seqs))
    lax.fori_loop(0, cps, step, init, unroll=8)
    y = jnp.dot(u, m_s[...], preferred_element_type=F32)
    y = y + jnp.dot(s_ref[...].astype(BF16), wy_s[...], preferred_element_type=F32)
    for k in range(t):
        y_ref[pl.ds(k, nc, stride=t), :] = y[:, k * LANES:(k + 1) * LANES]


def _ssm_core(u, rm, rz, ry, coef, *, layer, seq, max_seqs_per_block=4):
    n, w = u.shape
    seqs_per_block = math.gcd(max_seqs_per_block, n // seq)
    rows = seq * seqs_per_block
    nc = rows // SSM_CHUNK
    kw = SSM_CHUNK * LANES
    p2 = coef.shape[3]
    blk = pl.BlockSpec((rows, LANES), lambda hf, r: (r, hf))
    wspec = lambda arr: pl.BlockSpec((None, 1) + arr.shape[2:], lambda hf, r: (layer, hf, 0, 0))
    return pl.pallas_call(
        functools.partial(_ssm_kernel, seqs=seqs_per_block),
        grid=(w // LANES, n // rows),
        in_specs=[blk, wspec(rm), wspec(rz), wspec(ry), wspec(coef)],
        out_specs=blk,
        out_shape=jax.ShapeDtypeStruct((n, w), F32),
        scratch_shapes=[pltpu.VMEM((kw, kw), BF16), pltpu.VMEM((kw, p2), BF16), pltpu.VMEM((p2, kw), BF16),
                        pltpu.VMEM((nc, kw), BF16), pltpu.VMEM((nc, p2), F32), pltpu.VMEM((nc, p2), F32)],
        compiler_params=_params(("arbitrary", "arbitrary")),
        name="s5_core",
    )(u, rm, rz, ry, coef)


def _ssm_matrices(a_re, a_im, b_re, b_im, c_re, c_im, d, log_dt):
    t = SSM_CHUNK
    g, p = a_re.shape
    hdim = d.shape[1]
    gl = LANES // hdim
    nb = g // gl
    lam = lax.complex(a_re, a_im)
    lam_dt = lam * jnp.exp(log_dt)[:, None]
    lam_bar = jnp.exp(lam_dt)
    b_bar = ((lam_bar - 1.0) / lam)[:, :, None] * lax.complex(b_re, b_im)
    cmat = lax.complex(c_re, c_im)
    steps = jnp.arange(t + 1, dtype=F32)
    pw = jnp.exp(lam_dt[:, None, :] * steps[None, :, None])
    kern = jnp.einsum('gop,gdp,gpi->gdoi', cmat, pw[:, :t], b_bar).real
    kern = kern.at[:, 0].add(jax.vmap(jnp.diag)(d))
    j = jnp.arange(t)[:, None]
    i = jnp.arange(t)[None, :]
    m = jnp.where((i >= j)[None, :, :, None, None], kern[:, jnp.clip(i - j, 0, t - 1)], 0.0)
    wz = jnp.einsum('gjp,gpi->gjip', pw[:, t - 1 - jnp.arange(t)], b_bar)
    cw = jnp.einsum('gop,gip->giop', cmat, pw[:, 1:t + 1])
    rm = m.reshape(nb, gl, t, t, hdim, hdim).transpose(0, 2, 1, 5, 3, 4).reshape(nb, t * LANES, t * hdim)
    rz = jnp.stack([wz.real, wz.imag], axis=3).reshape(nb, gl, t, hdim, 2, p)
    rz = rz.transpose(0, 2, 1, 3, 4, 5).reshape(nb, t * LANES, 2 * p)
    ry = jnp.stack([cw.real, -cw.imag], axis=1).reshape(nb, gl, 2, t, hdim, p)
    ry = ry.transpose(0, 2, 1, 5, 3, 4).reshape(nb, 2 * gl * p, t * hdim)
    ar = pw[:, t].real.reshape(nb, gl * p)
    ai = pw[:, t].imag.reshape(nb, gl * p)
    coef = jnp.stack([jnp.concatenate([ar, ar], -1), jnp.concatenate([-ai, ai], -1)], axis=1)
    return rm.astype(BF16), rz.astype(BF16), ry.astype(BF16), coef.astype(F32)


MIX_SUB = 256


def _mixout_kernel(*refs, moe):
    (x_ref, attn_ref, pool_ref, ys_ref, wglu_ref, bglu_ref, ga_ref, gp_ref, gs_ref,
     wo_ref, gpost_ref, gpre_ref) = refs[:12]
    if moe:
        router_ref, xo_ref, h_ref, ri_ref, rg_ref, cnt_ref, cnt_acc = refs[12:]

        @pl.when(pl.program_id(0) == 0)
        def _():
            cnt_acc[...] = jnp.zeros_like(cnt_acc)
    else:
        xo_ref, h_ref = refs[12:]
    a = attn_ref.shape[1]
    pw = pool_ref.shape[1]
    h_parts = []
    for s in range(x_ref.shape[0] // MIX_SUB):
        r = pl.ds(s * MIX_SUB, MIX_SUB)
        attn_n = _rms(attn_ref[r, :].astype(F32), ga_ref[...]).astype(BF16)
        pool_n = _rms(pool_ref[r, :].astype(F32), gp_ref[...]).astype(BF16)
        y = jax.nn.gelu(ys_ref[r, :])
        gate = jnp.dot(y.astype(BF16), wglu_ref[...], preferred_element_type=F32) + bglu_ref[...]
        ssm = y * jax.nn.sigmoid(gate)
        ssm_n = _rms(ssm, gs_ref[...]).astype(BF16)
        mix = (jnp.dot(attn_n, wo_ref[0:a, :], preferred_element_type=F32)
               + jnp.dot(pool_n, wo_ref[a:a + pw, :], preferred_element_type=F32)
               + jnp.dot(ssm_n, wo_ref[a + pw:, :], preferred_element_type=F32))
        x = x_ref[r, :] + _rms(mix, gpost_ref[...])
        xo_ref[r, :] = x
        h = _rms(x, gpre_ref[...])
        if moe:
            _store_slabs(h_ref, s * MIX_SUB, h)
            h_parts.append(h)
        else:
            h_ref[r, :] = h.astype(h_ref.dtype)
    if moe:
        hi, mid, _ = _split3(jnp.concatenate(h_parts, axis=0))
        rhi, rmid, _ = _split3(router_ref[...])
        dot_t = lambda p, q: lax.dot_general(p, q, (((1,), (1,)), ((), ())), preferred_element_type=F32)
        logits = dot_t(rhi, hi) + dot_t(rmid, hi) + dot_t(rhi, mid)
        ne, tm = logits.shape
        eidx = lax.broadcasted_iota(jnp.int32, (ne, tm), 0)
        m1 = jnp.max(logits, axis=0, keepdims=True)
        i1 = jnp.min(jnp.where(logits == m1, eidx, ne), axis=0, keepdims=True)
        rest = jnp.where(eidx == i1, -jnp.inf, logits)
        m2 = jnp.max(rest, axis=0, keepdims=True)
        i2 = jnp.min(jnp.where(rest == m2, eidx, ne), axis=0, keepdims=True)
        e2 = jnp.exp(m2 - m1)
        g1 = 1.0 / (1.0 + e2)
        rg_ref[...] = jnp.where(eidx == 0, g1, jnp.where(eidx == 1, e2 * g1, 0.0))
        onehot = ((eidx == i1) | (eidx == i2)).astype(BF16)
        earlier = (lax.broadcasted_iota(jnp.int32, (tm, tm), 0)
                   < lax.broadcasted_iota(jnp.int32, (tm, tm), 1)).astype(BF16)
        before = jnp.dot(onehot, earlier, preferred_element_type=F32) + cnt_acc[:, 0:1]
        r1 = jnp.sum(jnp.where(eidx == i1, before, 0.0), axis=0, keepdims=True).astype(jnp.int32)
        r2 = jnp.sum(jnp.where(eidx == i2, before, 0.0), axis=0, keepdims=True).astype(jnp.int32)
        ri_ref[...] = jnp.where(eidx == 0, i1, jnp.where(eidx == 1, i2,
                                jnp.where(eidx == 2, r1, jnp.where(eidx == 3, r2, 0))))
        cnt_acc[...] += jnp.sum(onehot.astype(F32), axis=1, keepdims=True)
        cnt_ref[...] = cnt_acc[...]


def _mixout(x2, attn, pool, ys, w_glu, b_glu, g_attn, g_pool, g_ssm, w_out, g_post, g_pre,
            router=None, *, layer, moe_layer=None, tm=512):
    n, d = x2.shape
    moe = router is not None
    row = lambda w: pl.BlockSpec((tm, w), lambda i: (i, 0))
    ins = [x2, attn, pool, ys, w_glu, b_glu, g_attn, g_pool, g_ssm, w_out, g_post, g_pre]
    in_specs = ([row(d), row(attn.shape[1]), row(pool.shape[1]), row(ys.shape[1])]
                + [_layer_spec(t, layer) for t in ins[4:]])
    if moe:
        assert d == SUBLANES * LANES
        h_shape = jax.ShapeDtypeStruct((n * SUBLANES, LANES), F32)
        h_spec = pl.BlockSpec((tm * SUBLANES, LANES), lambda i: (i, 0))
    else:
        h_shape, h_spec = jax.ShapeDtypeStruct((n, d), BF16), row(d)
    out_shape = [jax.ShapeDtypeStruct((n, d), F32), h_shape]
    out_specs = [row(d), h_spec]
    scratch = []
    if moe:
        ins.append(router)
        in_specs.append(_layer_spec(router, moe_layer))
        ne = router.shape[1]
        col = pl.BlockSpec((ne, tm), lambda i: (0, i))
        out_shape += [jax.ShapeDtypeStruct((ne, n), jnp.int32), jax.ShapeDtypeStruct((ne, n), F32),
                      jax.ShapeDtypeStruct((ne, LANES), F32)]
        out_specs += [col, col, pl.BlockSpec((ne, LANES), lambda i: (0, 0))]
        scratch = [pltpu.VMEM((ne, LANES), F32)]
    return pl.pallas_call(
        functools.partial(_mixout_kernel, moe=moe),
        grid=(n // tm,),
        in_specs=in_specs,
        out_specs=tuple(out_specs),
        out_shape=tuple(out_shape),
        scratch_shapes=scratch,
        compiler_params=_params(("arbitrary",)),
        name="mix_out",
    )(*ins)


FFN_STAGE_CHUNKS = 4


def _stage_cast(src_hbm, dst, lane0, stage, sem):
    rows = stage.shape[1]
    width = src_hbm.shape[1]

    def copy(c):
        return pltpu.make_async_copy(src_hbm.at[pl.ds(c * rows, rows)], stage.at[c % 2], sem.at[c % 2])

    nchunks = src_hbm.shape[0] // rows
    copy(0).start()
    for c in range(nchunks):
        if c + 1 < nchunks:
            copy(c + 1).start()
        copy(c).wait()
        dst[pl.ds(c * rows, rows), lane0:lane0 + width] = stage[c % 2].astype(BF16)


def _ffn_kernel(h_ref, x_ref, wg_hbm, wu_hbm, wd_hbm, g_ref, o_ref, wgu_s, wd_s, stage_in, stage_out, sem, *, sub):
    fdim = wd_s.shape[0]

    @pl.when(pl.program_id(0) == 0)
    def _():
        _stage_cast(wg_hbm, wgu_s, 0, stage_in, sem)
        _stage_cast(wu_hbm, wgu_s, fdim, stage_in, sem)
        _stage_cast(wd_hbm, wd_s, 0, stage_out, sem)

    for r in range(h_ref.shape[0] // sub):
        rows = pl.ds(r * sub, sub)
        hgu = jnp.dot(h_ref[rows, :], wgu_s[...], preferred_element_type=F32)
        hg, hu = hgu[:, :fdim], hgu[:, fdim:]
        act = (hg * jax.nn.sigmoid(hg) * hu).astype(BF16)
        y = jnp.dot(act, wd_s[...], preferred_element_type=F32)
        o_ref[rows, :] = x_ref[rows, :] + _rms(y, g_ref[...])


def _ffn(h, x2, wg, wu, wd, g_post, *, layer, tm=512, sub=256):
    n, d = x2.shape
    fdim = wg.shape[1]
    row = pl.BlockSpec((tm, d), lambda i: (i, 0))
    hbm = pl.BlockSpec(memory_space=pl.ANY)
    return pl.pallas_call(
        functools.partial(_ffn_kernel, sub=sub),
        grid=(n // tm,),
        in_specs=[row, row, hbm, hbm, hbm, _layer_spec(g_post, layer)],
        out_specs=row,
        out_shape=jax.ShapeDtypeStruct((n, d), F32),
        scratch_shapes=[pltpu.VMEM((d, 2 * fdim), BF16), pltpu.VMEM((fdim, d), BF16),
                        pltpu.VMEM((2, d // FFN_STAGE_CHUNKS, fdim), F32),
                        pltpu.VMEM((2, fdim // FFN_STAGE_CHUNKS, d), F32),
                        pltpu.SemaphoreType.DMA((2,))],
        compiler_params=_params(("arbitrary",)),
        name="dense_ffn",
    )(h, x2, wg, wu, wd, g_post)


MOE_TILE = 512


def _moe_tables(counts, n_items):
    tm = MOE_TILE
    i32 = jnp.int32
    ne = counts.shape[0]
    assert n_items <= LANES
    counts = counts.astype(i32)
    ntiles = (counts + tm - 1) // tm
    k = jnp.arange(ne, dtype=i32)
    ends = jnp.sum(jnp.where(k[None, :] <= k[:, None], ntiles[None, :], 0), axis=1)
    starts = ends - ntiles
    later = (ntiles > 0)[None, :] & (k[None, :] > k[:, None])
    nxt = jnp.min(jnp.where(later, k[None, :], ne), axis=1)
    nxt = jnp.where(nxt < ne, nxt, -1)
    w = jnp.arange(LANES, dtype=i32)[None, :]
    wc = jnp.minimum(w, ends[-1] - 1)
    e = jnp.sum((wc >= ends[:, None]).astype(i32), axis=0, keepdims=True)
    of_item = lambda v: jnp.sum(jnp.where(k[:, None] == e, v[:, None], 0), axis=0, keepdims=True)
    valid = w < ends[-1]
    rowblock = jnp.where(valid, w, n_items)
    nvalid = jnp.where(valid, jnp.clip(of_item(counts) - (wc - of_item(starts)) * tm, 0, tm), 0)
    e_prev = jnp.concatenate([e[:, :1], e[:, :-1]], axis=1)
    first = ((w == 0) | (e != e_prev)).astype(i32)
    item = lambda v: v[0, :n_items].astype(i32)
    return (item(e), item(rowblock), item(nvalid), item(first), item(of_item(nxt))), starts * tm


def _slab(ref, token):
    return ref.at[pl.ds(pl.multiple_of(token * SUBLANES, SUBLANES), SUBLANES)]


def _dispatch_kernel(dest_ref, h_ref, xs_ref, sem, *, n):
    tm = h_ref.shape[0] // SUBLANES
    base = pl.program_id(0) * tm

    def body(r, carry):
        for k in range(2):
            d = dest_ref[k * n + base + r]
            pltpu.make_async_copy(_slab(h_ref, r), _slab(xs_ref, d), sem).start(priority=k)
        return carry

    lax.fori_loop(0, tm, body, 0, unroll=8)
    for k in range(2):
        pltpu.make_async_copy(h_ref, h_ref, sem).wait()


def _dispatch(dest, h, cap_rows, *, tm=1024):
    n = h.shape[0] // SUBLANES
    return pl.pallas_call(
        functools.partial(_dispatch_kernel, n=n),
        grid_spec=pltpu.PrefetchScalarGridSpec(
            num_scalar_prefetch=1,
            grid=(n // tm,),
            in_specs=[pl.BlockSpec((tm * SUBLANES, LANES), lambda i, dest: (i, 0))],
            out_specs=pl.BlockSpec(memory_space=pl.ANY),
            scratch_shapes=[pltpu.SemaphoreType.DMA(())]),
        out_shape=jax.ShapeDtypeStruct((cap_rows * SUBLANES, LANES), F32),
        compiler_params=_params(("arbitrary",)),
        name="moe_dispatch",
    )(dest, h)


MOE_SUB = 256


MOE_UP_CHUNKS = 4


def _moe_up_kernel(ie_ref, rb_ref, nv_ref, first_ref, nxt_ref, xs_ref, wg_hbm, wu_hbm, h_ref, wgu_s, stage, sem):
    f = pl.program_id(0)
    w = pl.program_id(1)
    nf = pl.num_programs(0)
    tf = h_ref.shape[1]
    rows = stage.shape[0] // MOE_UP_CHUNKS

    def copies(e, fp):
        return [pltpu.make_async_copy(src.at[e, pl.ds(c * rows, rows), pl.ds(fp * tf, tf)],
                                      stage.at[pl.ds(c * rows, rows), pl.ds(m * tf, tf)],
                                      sem.at[m * MOE_UP_CHUNKS + c])
                for m, src in enumerate((wg_hbm, wu_hbm)) for c in range(MOE_UP_CHUNKS)]

    @pl.when((f == 0) & (w == 0))
    def _():
        for cp in copies(ie_ref[0], 0):
            cp.start()

    is_first = first_ref[w] == 1
    last_expert = nxt_ref[w] < 0
    for fp in range(nf):
        @pl.when(is_first & (f == fp))
        def _():
            for cp in copies(ie_ref[w], fp):
                cp.wait()
            for c in range(MOE_UP_CHUNKS):
                r = pl.ds(c * rows, rows)
                wgu_s[r, :] = stage[r, :].astype(BF16)

            @pl.when(jnp.logical_not(last_expert))
            def _():
                for cp in copies(nxt_ref[w], fp):
                    cp.start()

            if fp + 1 < nf:
                @pl.when(last_expert)
                def _():
                    for cp in copies(ie_ref[0], fp + 1):
                        cp.start()

    nv = nv_ref[w]

    nsub = h_ref.shape[0] // MOE_SUB

    def compute(nchains):
        for r in range(nchains):
            rws = pl.ds(r * MOE_SUB, MOE_SUB)
            idx = r * MOE_SUB + lax.broadcasted_iota(jnp.int32, (MOE_SUB, 1), 0)
            x = _load_slabs(xs_ref, r * MOE_SUB, MOE_SUB)
            x = jnp.where(idx < nv, x, 0.0).astype(BF16)
            hgu = jnp.dot(x, wgu_s[...], preferred_element_type=F32)
            hg, hu = hgu[:, :tf], hgu[:, tf:]
            h_ref[rws, :] = (hg * jax.nn.sigmoid(hg) * hu).astype(BF16)
        if nchains < nsub:
            rest = pl.ds(nchains * MOE_SUB, (nsub - nchains) * MOE_SUB)
            h_ref[rest, :] = jnp.zeros((rest.size, tf), BF16)

    for nchains in range(nsub + 1):
        pl.when((nv > (nchains - 1) * MOE_SUB) & (nv <= nchains * MOE_SUB))(functools.partial(compute, nchains))


def _moe_up(tables, xs, wg, wu, *, tf=1792):
    tm = MOE_TILE
    cap_rows = xs.shape[0] // SUBLANES
    ne, d, fdim = wg.shape
    n_items = tables[0].shape[0]
    hbm = pl.BlockSpec(memory_space=pl.ANY)
    return pl.pallas_call(
        _moe_up_kernel,
        grid_spec=pltpu.PrefetchScalarGridSpec(
            num_scalar_prefetch=5,
            grid=(fdim // tf, n_items),
            in_specs=[pl.BlockSpec((tm * SUBLANES, LANES), lambda f, w, ie, rb, *_: (rb[w], 0)), hbm, hbm],
            out_specs=pl.BlockSpec((tm, tf), lambda f, w, ie, rb, *_: (rb[w], f)),
            scratch_shapes=[pltpu.VMEM((d, 2 * tf), BF16), pltpu.VMEM((d, 2 * tf), F32),
                            pltpu.SemaphoreType.DMA((2 * MOE_UP_CHUNKS,))]),
        out_shape=jax.ShapeDtypeStruct((cap_rows, fdim), BF16),
        compiler_params=_params(("arbitrary", "arbitrary")),
        name="moe_up",
    )(*tables, xs, wg, wu)


def _moe_down_kernel(ie_ref, rb_ref, nv_ref, first_ref, nxt_ref, h_ref, wd_hbm, y_ref, wd_s, stage, sem):
    w = pl.program_id(0)
    nchunks = sem.shape[0]
    chunk = stage.shape[0] // nchunks

    def copies(e):
        return [pltpu.make_async_copy(wd_hbm.at[e, pl.ds(c * chunk, chunk)], stage.at[pl.ds(c * chunk, chunk)],
                                      sem.at[c]) for c in range(nchunks)]

    @pl.when(w == 0)
    def _():
        for cp in copies(ie_ref[0]):
            cp.start()

    @pl.when(first_ref[w] == 1)
    def _():
        for c, cp in enumerate(copies(ie_ref[w])):
            cp.wait()
            wd_s[pl.ds(c * chunk, chunk), :] = stage[pl.ds(c * chunk, chunk), :].astype(BF16)

        @pl.when(nxt_ref[w] >= 0)
        def _():
            for cp in copies(nxt_ref[w]):
                cp.start()

    nv = nv_ref[w]
    nsub = h_ref.shape[0] // MOE_SUB

    def compute(nchains):
        for r in range(nchains):
            y = jnp.dot(h_ref[pl.ds(r * MOE_SUB, MOE_SUB), :], wd_s[...], preferred_element_type=F32)
            _store_slabs(y_ref, r * MOE_SUB, y)
        if nchains < nsub:
            rest = pl.ds(nchains * MOE_SUB * SUBLANES, (nsub - nchains) * MOE_SUB * SUBLANES)
            y_ref[rest, :] = jnp.zeros((rest.size, LANES), F32)

    for nchains in range(nsub + 1):
        pl.when((nv > (nchains - 1) * MOE_SUB) & (nv <= nchains * MOE_SUB))(functools.partial(compute, nchains))


def _moe_down(tables, hid, wd, *, chunk=512):
    tm = MOE_TILE
    cap_rows, fdim = hid.shape
    d = wd.shape[2]
    n_items = tables[0].shape[0]
    assert d == SUBLANES * LANES and fdim % chunk == 0
    return pl.pallas_call(
        _moe_down_kernel,
        grid_spec=pltpu.PrefetchScalarGridSpec(
            num_scalar_prefetch=5,
            grid=(n_items,),
            in_specs=[pl.BlockSpec((tm, fdim), lambda w, ie, rb, *_: (rb[w], 0)),
                      pl.BlockSpec(memory_space=pl.ANY)],
            out_specs=pl.BlockSpec((tm * SUBLANES, LANES), lambda w, ie, rb, *_: (rb[w], 0)),
            scratch_shapes=[pltpu.VMEM((fdim, d), BF16), pltpu.VMEM((fdim, d), F32),
                            pltpu.SemaphoreType.DMA((fdim // chunk,))]),
        out_shape=jax.ShapeDtypeStruct((cap_rows * SUBLANES, LANES), F32),
        compiler_params=_params(("arbitrary",)),
        name="moe_down",
    )(*tables, hid, wd)


def _combine_kernel(dest_ref, x_ref, gate_ref, g_ref, ys_ref, o_ref, buf, sem, *, n):
    tm = x_ref.shape[0]
    i = pl.program_id(0)

    def issue(tile, slot):
        def body(r, carry):
            for k in range(2):
                d = dest_ref[k * n + tile * tm + r]
                pltpu.make_async_copy(_slab(ys_ref, d), _slab(buf.at[slot, k], r),
                                      sem.at[slot]).start(priority=k)
            return carry
        lax.fori_loop(0, tm, body, 0, unroll=8)

    @pl.when(i == 0)
    def _():
        issue(0, 0)

    @pl.when(i + 1 < pl.num_programs(0))
    def _():
        issue(i + 1, (i + 1) % 2)

    slot = i % 2
    for k in range(2):
        pltpu.make_async_copy(buf.at[slot, k], buf.at[slot, k], sem.at[slot]).wait()
    gates = gate_ref[...].T
    f = (gates[:, 0:1] * _load_slabs(buf.at[slot, 0], 0, tm)
         + gates[:, 1:2] * _load_slabs(buf.at[slot, 1], 0, tm))
    o_ref[...] = x_ref[...] + _rms(f, g_ref[...])


def _combine(dest, x2, gates, g_post, ys, *, layer, tm=512):
    n, d = x2.shape
    return pl.pallas_call(
        functools.partial(_combine_kernel, n=n),
        grid_spec=pltpu.PrefetchScalarGridSpec(
            num_scalar_prefetch=1,
            grid=(n // tm,),
            in_specs=[pl.BlockSpec((tm, d), lambda i, dest: (i, 0)),
                      pl.BlockSpec((gates.shape[0], tm), lambda i, dest: (0, i)),
                      _layer_spec(g_post, layer),
                      pl.BlockSpec(memory_space=pl.ANY)],
            out_specs=pl.BlockSpec((tm, d), lambda i, dest: (i, 0)),
            scratch_shapes=[pltpu.VMEM((2, 2, tm * SUBLANES, LANES), F32), pltpu.SemaphoreType.DMA((2,))]),
        out_shape=jax.ShapeDtypeStruct((n, d), F32),
        compiler_params=_params(("arbitrary",)),
        name="moe_combine",
    )(dest, x2, gates, g_post, ys)


def _moe(h, x2, route_i, route_g, counts, wg, wu, wd, g_post, *, layer):
    n = x2.shape[0]
    n_items = 2 * n // MOE_TILE + N_EXPERTS
    tables, base = _moe_tables(counts[:, 0], n_items)
    base_of = lambda e: jnp.sum(jnp.where(e[None, :] == jnp.arange(N_EXPERTS)[:, None], base[:, None], 0), axis=0)
    dest = jnp.concatenate([base_of(route_i[0]) + route_i[2], base_of(route_i[1]) + route_i[3]])
    xs = _dispatch(dest, h, (n_items + 1) * MOE_TILE)
    hid = _moe_up(tables, xs, wg, wu)
    ys = _moe_down(tables, hid, wd)
    return _combine(dest, x2, route_g, g_post, ys, layer=layer)


def _rows(v):
    return v.reshape(v.shape[0], 1, -1).astype(F32)


def kernel(x, norm_mix_pre, norm_mix_post, norm_ffn_pre, norm_ffn_post, w_in, b_forget, pool_w, pool_scale, ssm_a_re, ssm_a_im, ssm_b_re, ssm_b_im, ssm_c_re, ssm_c_im, ssm_d, ssm_log_dt, ssm_w_glu, ssm_b_glu, branch_norm_attn, branch_norm_pool, branch_norm_ssm, w_out, ffn_w_gate, ffn_w_up, ffn_w_down, moe_router, moe_w_gate, moe_w_up, moe_w_down):
    b, L, d = x.shape
    depth = w_in.shape[0]
    n = b * L
    a = ATTN_WIDTH
    attn_tile = 512
    x2 = x.reshape(n, d)
    g_mix_pre, g_mix_post = _rows(norm_mix_pre), _rows(norm_mix_post)
    g_ffn_pre, g_ffn_post = _rows(norm_ffn_pre), _rows(norm_ffn_post)
    g_attn, g_pool, g_ssm = _rows(branch_norm_attn), _rows(branch_norm_pool), _rows(branch_norm_ssm)
    b_f = _rows(jnp.pad(b_forget, ((0, 0), (0, LANES - ATTN_HEADS))))
    pool_bd = jax.vmap(lambda w: jax.scipy.linalg.block_diag(*w))(pool_w).astype(BF16)
    pool_sc, b_glu = _rows(pool_scale), _rows(ssm_b_glu)
    w_glu, w_o = ssm_w_glu.astype(BF16), w_out.astype(BF16)
    mats = jax.vmap(_ssm_matrices)(ssm_a_re, ssm_a_im, ssm_b_re, ssm_b_im, ssm_c_re, ssm_c_im, ssm_d, ssm_log_dt)
    router_t = jnp.swapaxes(moe_router, 1, 2)
    for i in range(depth):
        q, k, v, up, us, c = _inproj(x2, g_mix_pre, w_in, b_f, layer=i, seq=L)

        ct = c.reshape(b, L, ATTN_HEADS).transpose(0, 2, 1).reshape(b, ATTN_HEADS * (L // attn_tile), attn_tile)
        attn = _attention(q.reshape(b, L, a), k.reshape(b, L, a), v.reshape(b, L, a),
                          ct, tq=attn_tile).reshape(n, a)
        pool = _pool(up.reshape(b, L, POOL_WIDTH), pool_bd, pool_sc, layer=i).reshape(n, POOL_WIDTH)
        ys = _ssm_core(us, *mats, layer=i, seq=L)

        moe = i % 2 == 1
        j = i // 2
        res = _mixout(x2, attn, pool, ys, w_glu, b_glu, g_attn, g_pool, g_ssm, w_o, g_mix_post, g_ffn_pre,
                      router_t if moe else None, layer=i, moe_layer=j)
        if moe:
            x2, h, route_i, route_g, counts = res
            x2 = _moe(h, x2, route_i, route_g, counts, moe_w_gate[j], moe_w_up[j], moe_w_down[j],
                      g_ffn_post, layer=i)
        else:
            x2, h = res
            x2 = _ffn(h, x2, ffn_w_gate[j], ffn_w_up[j], ffn_w_down[j], g_ffn_post, layer=i)
    return x2.reshape(b, L, d)
```
